```python
import jax, jax.numpy as jnp
from jax import lax
import numpy as np

D_MODEL = 1024
BATCH = 4
SEQ = 8192
DEPTH = 1

HEAD_DIM = 64
MOBA_HEADS = 8
NSA_HEADS = 8
NSA_KV_HEADS = 2
NSA_GROUP = NSA_HEADS // NSA_KV_HEADS
MIX_WIDTH = (MOBA_HEADS + NSA_HEADS) * HEAD_DIM
MOBA_BLOCK = 256
MOBA_TOPK = 3
NSA_CMP_LEN = 32
NSA_CMP_STRIDE = 16
NSA_SLC_BLOCK = 64
NSA_SLC_TOPN = 16
NSA_WINDOW = 512
NSA_BRANCHES = 3
N_EXPERTS = 32
TOP_K = 4
D_FF = D_MODEL
SWIGLU_LIMIT = 7.0
SWIGLU_ALPHA = 1.702
MOE_ROW_BLOCK = 256
Q_BLOCK = 64
RMS_EPS = 1e-5
NEG_BIG = -1e30
NSA_GATE_COLS = NSA_BRANCHES * NSA_HEADS
IN_COLS = 3 * MOBA_HEADS * HEAD_DIM + NSA_HEADS * HEAD_DIM + 2 * NSA_BRANCHES * NSA_KV_HEADS * HEAD_DIM + NSA_GATE_COLS

kernel_name = 'hybrid_moba_nsa_moe_alibi'


def rmsnorm(x, g):
    xf = x.astype(jnp.float32)
    y = xf * lax.rsqrt(jnp.mean(xf * xf, axis=-1, keepdims=True) + RMS_EPS)
    return (y * g.astype(jnp.float32)).astype(x.dtype)


def masked_softmax(logits, mask):
    z = jnp.where(mask, logits, NEG_BIG)
    z = z - jnp.max(z, axis=-1, keepdims=True)
    p = jnp.where(mask, jnp.exp(z), 0.0)
    return p / jnp.maximum(jnp.sum(p, axis=-1, keepdims=True), 1e-30)


def alibi_slopes():
    n = MOBA_HEADS + NSA_HEADS
    s = jnp.exp2(-8.0 * jnp.arange(1, n + 1, dtype=jnp.float32) / n)
    return s[0::2], s[1::2]


def moba_attention(q, k, v, slopes):
    B, S, H, hd = q.shape
    nb = -(-S // MOBA_BLOCK)
    pad = nb * MOBA_BLOCK - S
    padw = ((0, 0), (0, pad), (0, 0), (0, 0))
    kb = jnp.pad(k, padw).reshape(B, nb, MOBA_BLOCK, H, hd).transpose(0, 3, 1, 2, 4)
    vb = jnp.pad(v, padw).reshape(B, nb, MOBA_BLOCK, H, hd).transpose(0, 3, 1, 2, 4)
    kmean = jnp.mean(kb.astype(jnp.float32), axis=3)
    qh = q.transpose(0, 2, 1, 3) * (hd ** -0.5)
    topk = min(MOBA_TOPK, nb)
    nsel = topk * MOBA_BLOCK
    bi = jnp.arange(B)[:, None, None, None]
    hi = jnp.arange(H)[None, :, None, None]
    sl = slopes[None, :, None, None]
    offs = jnp.arange(MOBA_BLOCK)

    def chunk(c):
        t0 = c * Q_BLOCK
        qc = lax.dynamic_slice_in_dim(qh, t0, Q_BLOCK, axis=2)
        t = t0 + jnp.arange(Q_BLOCK)
        cur = t // MOBA_BLOCK
        gate = jnp.einsum('bhcd,bhnd->bhcn', qc.astype(jnp.float32), kmean)
        past = jnp.arange(nb)[None, :] < cur[:, None]
        _, idx = lax.top_k(jnp.where(past, gate, -jnp.inf), topk)
        ksel = kb[bi, hi, idx]
        vsel = vb[bi, hi, idx]
        pos_sel = idx[..., None] * MOBA_BLOCK + offs
        dist_sel = (t[None, None, :, None, None] - pos_sel).astype(jnp.float32)
        s_sel = jnp.einsum('bhcd,bhcnjd->bhcnj', qc, ksel, preferred_element_type=jnp.float32) - sl[..., None] * dist_sel
        mask_sel = jnp.broadcast_to((jnp.arange(topk)[None, :] < cur[:, None])[None, None, :, :, None], s_sel.shape)
        cb = t0 // MOBA_BLOCK
        kown = lax.dynamic_index_in_dim(kb, cb, axis=2, keepdims=False)
        vown = lax.dynamic_index_in_dim(vb, cb, axis=2, keepdims=False)
        pos_own = cb * MOBA_BLOCK + offs
        dist_own = (t[:, None] - pos_own[None, :]).astype(jnp.float32)
        s_own = jnp.einsum('bhcd,bhjd->bhcj', qc, kown, preferred_element_type=jnp.float32) - sl * dist_own
        mask_own = jnp.broadcast_to((dist_own >= 0)[None, None], s_own.shape)
        logits = jnp.concatenate([s_sel.reshape(B, H, Q_BLOCK, nsel), s_own], axis=-1)
        mask = jnp.concatenate([mask_sel.reshape(B, H, Q_BLOCK, nsel), mask_own], axis=-1)
        p = masked_softmax(logits, mask).astype(v.dtype)
        out = jnp.einsum('bhcm,bhcmd->bhcd', p[..., :nsel], vsel.reshape(B, H, Q_BLOCK, nsel, hd))
        return out + jnp.einsum('bhcj,bhjd->bhcd', p[..., nsel:], vown)

    outs = lax.map(chunk, jnp.arange(S // Q_BLOCK))
    return outs.transpose(1, 0, 3, 2, 4).reshape(B, S, H * hd)


def compress_blocks(kv, pe, w1, w2):
    B, S, G, hd = kv.shape
    n_cmp = (S - NSA_CMP_LEN) // NSA_CMP_STRIDE + 1
    idx = jnp.arange(n_cmp)[:, None] * NSA_CMP_STRIDE + jnp.arange(NSA_CMP_LEN)[None, :]
    blk = kv[:, idx] + pe[None, None, :, None, :]
    flat = blk.transpose(0, 1, 3, 2, 4).reshape(B, n_cmp, G, NSA_CMP_LEN * hd)
    return jax.nn.gelu(flat @ w1) @ w2


def nsa_attention(q, kc, vc, ks, vs, kw, vw, gate_logits, pe_k, pe_v, w1_k, w2_k, w1_v, w2_v, slopes):
    B, S, Hq, hd = q.shape
    G, Hg = NSA_KV_HEADS, NSA_GROUP
    dt = q.dtype
    n_cmp = (S - NSA_CMP_LEN) // NSA_CMP_STRIDE + 1
    n_slc = S // NSA_SLC_BLOCK
    topn = min(NSA_SLC_TOPN, n_slc)
    W = NSA_WINDOW
    kcmp = compress_blocks(kc, pe_k, w1_k, w2_k)
    vcmp = compress_blocks(vc, pe_v, w1_v, w2_v)
    cmp_pos = jnp.arange(n_cmp) * NSA_CMP_STRIDE + NSA_CMP_LEN - 1
    cu, su = NSA_CMP_LEN // NSA_CMP_STRIDE, NSA_SLC_BLOCK // NSA_CMP_STRIDE
    ii = jnp.arange(n_cmp)[:, None]
    jj = jnp.arange(n_slc)[None, :]
    overlap = jnp.clip(jnp.minimum(ii + cu, (jj + 1) * su) - jnp.maximum(ii, jj * su), 0, None).astype(jnp.float32)
    qh = q.reshape(B, S, G, Hg, hd).transpose(0, 2, 3, 1, 4) * (hd ** -0.5)
    ksb = ks.reshape(B, n_slc, NSA_SLC_BLOCK, G, hd).transpose(0, 3, 1, 2, 4)
    vsb = vs.reshape(B, n_slc, NSA_SLC_BLOCK, G, hd).transpose(0, 3, 1, 2, 4)
    padw = ((0, 0), (W, 0), (0, 0), (0, 0))
    kwp = jnp.pad(kw, padw).transpose(0, 2, 1, 3)
    vwp = jnp.pad(vw, padw).transpose(0, 2, 1, 3)
    gates = jax.nn.sigmoid(gate_logits.astype(jnp.float32)).reshape(B, S, G, Hg, NSA_BRANCHES).transpose(0, 2, 3, 1, 4)
    sl = slopes.reshape(G, Hg)[None, :, :, None, None]
    bi = jnp.arange(B)[:, None, None, None]
    gi = jnp.arange(G)[None, :, None, None]
    soffs = jnp.arange(NSA_SLC_BLOCK)
    nwin = Q_BLOCK + W - 1
    nsel = topn * NSA_SLC_BLOCK

    def chunk(c):
        t0 = c * Q_BLOCK
        qc = lax.dynamic_slice_in_dim(qh, t0, Q_BLOCK, axis=3)
        t = t0 + jnp.arange(Q_BLOCK)
        dist_c = (t[:, None] - cmp_pos[None, :]).astype(jnp.float32)
        s_c = jnp.einsum('bghcd,bngd->bghcn', qc, kcmp, preferred_element_type=jnp.float32)
        p_c = masked_softmax(s_c - sl * dist_c, dist_c >= 0)
        o_c = jnp.einsum('bghcn,bngd->bghcd', p_c.astype(dt), vcmp)
        imp = jnp.einsum('bghcn,nj->bgcj', p_c, overlap)
        cur = t // NSA_SLC_BLOCK
        jr = jnp.arange(n_slc)[None, :]
        forced = (jr == 0) | (jr == cur[:, None]) | (jr == cur[:, None] - 1)
        allowed = jr <= cur[:, None]
        score = jnp.where(allowed, jnp.where(forced, jnp.inf, imp), -jnp.inf)
        _, sidx = lax.top_k(score, topn)
        ksel = ksb[bi, gi, sidx]
        vsel = vsb[bi, gi, sidx]
        pos_s = sidx[..., None] * NSA_SLC_BLOCK + soffs
        dist_s = (t[None, None, :, None, None] - pos_s).astype(jnp.float32)[:, :, None]
        s_s = jnp.einsum('bghcd,bgcnjd->bghcnj', qc, ksel, preferred_element_type=jnp.float32) - sl[..., None] * dist_s
        p_s = masked_softmax(s_s.reshape(B, G, Hg, Q_BLOCK, nsel), (dist_s >= 0).reshape(B, G, 1, Q_BLOCK, nsel))
        o_s = jnp.einsum('bghcm,bgcmd->bghcd', p_s.astype(dt), vsel.reshape(B, G, Q_BLOCK, nsel, hd))
        kwin = lax.dynamic_slice_in_dim(kwp, t0 + 1, nwin, axis=2)
        vwin = lax.dynamic_slice_in_dim(vwp, t0 + 1, nwin, axis=2)
        pos_w = t0 - W + 1 + jnp.arange(nwin)
        dist_w = t[:, None] - pos_w[None, :]
        mask_w = (dist_w >= 0) & (dist_w < W) & (pos_w[None, :] >= 0)
        s_w = jnp.einsum('bghcd,bgjd->bghcj', qc, kwin, preferred_element_type=jnp.float32)
        p_w = masked_softmax(s_w - sl * dist_w.astype(jnp.float32), mask_w)
        o_w = jnp.einsum('bghcj,bgjd->bghcd', p_w.astype(dt), vwin)
        g = lax.dynamic_slice_in_dim(gates, t0, Q_BLOCK, axis=3).astype(dt)
        return g[..., 0:1] * o_c + g[..., 1:2] * o_s + g[..., 2:3] * o_w

    outs = lax.map(chunk, jnp.arange(S // Q_BLOCK))
    return outs.transpose(1, 0, 4, 2, 3, 5).reshape(B, S, Hq * hd)


def clamped_swiglu(h):
    g = jnp.minimum(h[..., 0::2], SWIGLU_LIMIT)
    u = jnp.clip(h[..., 1::2], -SWIGLU_LIMIT, SWIGLU_LIMIT)
    return g * jax.nn.sigmoid(SWIGLU_ALPHA * g) * (u + 1.0)


def moe_ffn(h, w_router, b_router, w_up, b_up, w_down, b_down):
    B, S, D = h.shape
    N = B * S
    NK = N * TOP_K
    ht = h.reshape(N, D)
    logits = (ht @ w_router + b_router).astype(jnp.float32)
    top_val, top_idx = lax.top_k(logits, TOP_K)
    wts = jax.nn.softmax(top_val, axis=-1)
    flat_e = top_idx.reshape(NK)
    flat_tok = jnp.repeat(jnp.arange(N, dtype=jnp.int32), TOP_K)
    order = jnp.argsort(flat_e, stable=True)
    se, stok, sw = flat_e[order], flat_tok[order], wts.reshape(NK)[order]
    counts = jnp.bincount(flat_e, length=N_EXPERTS)
    starts = jnp.cumsum(counts) - counts
    padded = (counts + MOE_ROW_BLOCK - 1) // MOE_ROW_BLOCK * MOE_ROW_BLOCK
    pends = jnp.cumsum(padded)
    pstarts = pends - padded
    dest = pstarts[se] + jnp.arange(NK) - starts[se]
    n_blk = (NK + N_EXPERTS * MOE_ROW_BLOCK + MOE_ROW_BLOCK - 1) // MOE_ROW_BLOCK
    P = n_blk * MOE_ROW_BLOCK
    row_tok = jnp.full((P,), N, jnp.int32).at[dest].set(stok)
    xrows = jnp.concatenate([ht, jnp.zeros((1, D), ht.dtype)], axis=0)[row_tok].reshape(n_blk, MOE_ROW_BLOCK, D)
    blk_e = jnp.minimum(jnp.searchsorted(pends, jnp.arange(n_blk) * MOE_ROW_BLOCK, side='right'), N_EXPERTS - 1)

    def expert_block(args):
        xb, e = args
        a = clamped_swiglu(xb @ w_up[e] + b_up[e])
        return a @ w_down[e] + b_down[e]

    yrows = lax.map(expert_block, (xrows, blk_e)).reshape(P, D)
    contrib = yrows[dest] * sw[:, None].astype(yrows.dtype)
    return jax.ops.segment_sum(contrib, stok, num_segments=N).reshape(B, S, D)


def setup_inputs(seed: int = 0) -> dict:
    key = jax.random.key(seed)
    k = jax.random.split(key, 20)
    hd, L = HEAD_DIM, NSA_CMP_LEN

    def nrm(kk, shape, scale):
        return jax.random.normal(kk, shape, jnp.float32) * scale

    return {
        'x': nrm(k[0], (BATCH, SEQ, D_MODEL), 1.0),
        'attn_norm_g': 1.0 + nrm(k[1], (DEPTH, D_MODEL), 0.02),
        'w_in': nrm(k[2], (DEPTH, D_MODEL, IN_COLS), D_MODEL ** -0.5),
        'cmp_pe_k': nrm(k[3], (DEPTH, L, hd), 0.1),
        'cmp_pe_v': nrm(k[4], (DEPTH, L, hd), 0.1),
        'cmp_w1_k': nrm(k[5], (DEPTH, L * hd, hd), (L * hd) ** -0.5),
        'cmp_w2_k': nrm(k[6], (DEPTH, hd, hd), hd ** -0.5),
        'cmp_w1_v': nrm(k[7], (DEPTH, L * hd, hd), (L * hd) ** -0.5),
        'cmp_w2_v': nrm(k[8], (DEPTH, hd, hd), hd ** -0.5),
        'w_out': nrm(k[9], (DEPTH, MIX_WIDTH, D_MODEL), MIX_WIDTH ** -0.5),
        'ffn_norm_g': 1.0 + nrm(k[10], (DEPTH, D_MODEL), 0.02),
        'w_router': nrm(k[11], (DEPTH, D_MODEL, N_EXPERTS), D_MODEL ** -0.5),
        'b_router': nrm(k[12], (DEPTH, N_EXPERTS), 0.01),
        'w_up': nrm(k[13], (DEPTH, N_EXPERTS, D_MODEL, 2 * D_FF), D_MODEL ** -0.5),
        'b_up': nrm(k[14], (DEPTH, N_EXPERTS, 2 * D_FF), 0.01),
        'w_down': nrm(k[15], (DEPTH, N_EXPERTS, D_FF, D_MODEL), D_FF ** -0.5),
        'b_down': nrm(k[16], (DEPTH, N_EXPERTS, D_MODEL), 0.01),
        'final_norm_g': 1.0 + nrm(k[17], (D_MODEL,), 0.02),
    }


def reference(x, attn_norm_g, w_in, cmp_pe_k, cmp_pe_v, cmp_w1_k, cmp_w2_k, cmp_w1_v, cmp_w2_v, w_out, ffn_norm_g, w_router, b_router, w_up, b_up, w_down, b_down, final_norm_g):
    B, S, _ = x.shape
    moba_sl, nsa_sl = alibi_slopes()
    sizes = [MOBA_HEADS * HEAD_DIM] * 3 + [NSA_HEADS * HEAD_DIM] + [NSA_KV_HEADS * HEAD_DIM] * (2 * NSA_BRANCHES) + [NSA_GATE_COLS]
    cuts = [int(c) for c in np.cumsum(sizes)[:-1]]

    def heads(t, n):
        return t.reshape(B, S, n, HEAD_DIM)

    for l in range(DEPTH):
        xn = rmsnorm(x, attn_norm_g[l])
        mq, mk, mv, nq, kc, vc, ks, vs, kw, vw, ng = jnp.split(xn @ w_in[l], cuts, axis=-1)
        o_moba = moba_attention(heads(mq, MOBA_HEADS), heads(mk, MOBA_HEADS), heads(mv, MOBA_HEADS), moba_sl)
        o_nsa = nsa_attention(heads(nq, NSA_HEADS), heads(kc, NSA_KV_HEADS), heads(vc, NSA_KV_HEADS),
                              heads(ks, NSA_KV_HEADS), heads(vs, NSA_KV_HEADS), heads(kw, NSA_KV_HEADS),
                              heads(vw, NSA_KV_HEADS), ng, cmp_pe_k[l], cmp_pe_v[l], cmp_w1_k[l],
                              cmp_w2_k[l], cmp_w1_v[l], cmp_w2_v[l], nsa_sl)
        x = x + jnp.concatenate([o_moba, o_nsa], axis=-1) @ w_out[l]
        x = x + moe_ffn(rmsnorm(x, ffn_norm_g[l]), w_router[l], b_router[l], w_up[l], b_up[l], w_down[l], b_down[l])
    return rmsnorm(x, final_norm_g)
```

```python
import functools

import jax
import jax.numpy as jnp
import numpy as np
from jax import lax
from jax.experimental import pallas as pl
from jax.experimental.pallas import tpu as pltpu

F32 = jnp.float32
BF16 = jnp.bfloat16
I32 = jnp.int32

HEAD_DIM = 64
MOBA_HEADS = 8
NSA_HEADS = 8
NSA_KV_HEADS = 2
NSA_GROUP = NSA_HEADS // NSA_KV_HEADS
MOBA_BLOCK = 256
MOBA_TOPK = 3
NSA_CMP_LEN = 32
NSA_CMP_STRIDE = 16
NSA_SLC_BLOCK = 64
NSA_SLC_TOPN = 16
NSA_WINDOW = 512
NSA_BRANCHES = 3
N_EXPERTS = 32
TOP_K = 4
SWIGLU_LIMIT = 7.0
SWIGLU_ALPHA = 1.702
RMS_EPS = 1e-5
NEG_BIG = -1e30
LOG2E = 1.4426950408889634

LANES = 128
VMEM_LIMIT = 56 * 1024 * 1024

NSA_TQ = 128
SLC_TILE = 256
WIN_KEYS = NSA_WINDOW + NSA_TQ
MOE_ROWS = 512
ROUTE_TILE = 256
RANK_TILE = 512

NT_DIMS = (((1,), (1,)), ((), ()))


def _params(n_grid):
    return pltpu.CompilerParams(
        dimension_semantics=("arbitrary",) * n_grid,
        vmem_limit_bytes=VMEM_LIMIT,
    )


def _rmsnorm(x, g):
    return x * lax.rsqrt(jnp.mean(x * x, axis=-1, keepdims=True) + RMS_EPS) * g


def _inproj_kernel(x_ref, g_ref, wr_ref, wt_ref, wg_ref,
                   mk_ref, kc_ref, vc_ref, ks_ref, kw_ref,
                   mqT_ref, mvT_ref, nqT_ref, vsT_ref, vwT_ref, gT_ref):
    xn = _rmsnorm(x_ref[0], g_ref[...])
    xb = xn.astype(BF16)
    yr = jnp.dot(xb, wr_ref[...], preferred_element_type=F32)
    mk_ref[0] = yr[:, 0:512].astype(BF16)
    kc_ref[0] = yr[:, 512:640].astype(BF16)
    vc_ref[0] = yr[:, 640:768].astype(BF16)
    ks_ref[0] = yr[:, 768:896].astype(BF16)
    kw_ref[0] = yr[:, 896:1024].astype(BF16)
    yt = lax.dot_general(wt_ref[...], xb, NT_DIMS, preferred_element_type=F32)
    mqT_ref[0] = yt[0:512].astype(BF16)
    mvT_ref[0] = yt[512:1024].astype(BF16)
    nqT_ref[0] = yt[1024:1536].astype(BF16)
    vsT_ref[0] = yt[1536:1664].astype(BF16)
    vwT_ref[0] = yt[1664:1792].astype(BF16)
    gl = lax.dot_general(wg_ref[...], xn, NT_DIMS, precision=lax.Precision.HIGHEST,
                         preferred_element_type=F32)
    gT_ref[0] = jax.nn.sigmoid(gl)


def _inproj(x, g, wr, wt, wg, tm):
    B, S, D = x.shape
    grid = (B, S // tm)
    row = lambda w: pl.BlockSpec((1, tm, w), lambda b, i: (b, i, 0))
    col = lambda h: pl.BlockSpec((1, h, tm), lambda b, i: (b, 0, i))
    full = lambda a: pl.BlockSpec(a.shape, lambda b, i: (0,) * a.ndim)
    out_shape = [
        jax.ShapeDtypeStruct((B, S, 512), BF16),
        jax.ShapeDtypeStruct((B, S, 128), BF16),
        jax.ShapeDtypeStruct((B, S, 128), BF16),
        jax.ShapeDtypeStruct((B, S, 128), BF16),
        jax.ShapeDtypeStruct((B, S, 128), BF16),
        jax.ShapeDtypeStruct((B, 512, S), BF16),
        jax.ShapeDtypeStruct((B, 512, S), BF16),
        jax.ShapeDtypeStruct((B, 512, S), BF16),
        jax.ShapeDtypeStruct((B, 128, S), BF16),
        jax.ShapeDtypeStruct((B, 128, S), BF16),
        jax.ShapeDtypeStruct((B, 32, S), F32),
    ]
    out_specs = [row(512), row(128), row(128), row(128), row(128),
                 col(512), col(512), col(512), col(128), col(128), col(32)]
    return pl.pallas_call(
        _inproj_kernel, grid=grid,
        in_specs=[pl.BlockSpec((1, tm, D), lambda b, i: (b, i, 0)),
                  full(g), full(wr), full(wt), full(wg)],
        out_specs=out_specs, out_shape=out_shape,
        compiler_params=_params(2), name="inproj",
    )(x, g, wr, wt, wg)


def _online_update(h, z, c, vt, m_ref, l_ref, acc_ref):
    mt = jnp.max(z, axis=0, keepdims=True) - c
    m_old = m_ref[h]
    m_new = jnp.maximum(m_old, mt)
    alpha = jnp.exp2(m_old - m_new)
    p = jnp.exp2(z - (m_new + c))
    l_ref[h] = alpha * l_ref[h] + jnp.sum(p, axis=0, keepdims=True)
    acc_ref[h] = alpha * acc_ref[h] + jnp.dot(vt, p.astype(BF16), preferred_element_type=F32)
    m_ref[h] = m_new


def _moba_kernel(qT_ref, k_ref, vT_ref, sd_ref, sl_ref, o_ref,
                 kmean_ref, sel_ref, m_ref, l_ref, acc_ref, *, nb, topk):
    qi = pl.program_id(2)
    blk = MOBA_BLOCK

    @pl.when(qi == 0)
    def _():
        def body(n, carry):
            kb = k_ref[0, pl.ds(pl.multiple_of(n * blk, blk), blk), :].astype(F32)
            kmean_ref[pl.ds(n, 1), :] = jnp.mean(kb, axis=0, keepdims=True)
            return carry
        lax.fori_loop(0, nb, body, 0)

    qT = qT_ref[0]
    row = lax.broadcasted_iota(I32, qT.shape, 0)
    qpad = [jnp.where((row >> 6) == h, qT, jnp.zeros_like(qT)) for h in range(2)]

    bidx = lax.broadcasted_iota(I32, (nb, blk), 0)
    for h in range(2):
        gate = jnp.dot(kmean_ref[...], qpad[h].astype(F32),
                       precision=lax.Precision.HIGHEST, preferred_element_type=F32)
        gsc = jnp.where(bidx < qi, gate, -jnp.inf)
        sel = jnp.zeros((nb, blk), F32)
        for _ in range(topk):
            mx = jnp.max(gsc, axis=0, keepdims=True)
            idx = jnp.min(jnp.where(gsc == mx, bidx, nb), axis=0, keepdims=True)
            pick = jnp.logical_and(bidx == idx, mx > -jnp.inf)
            sel = jnp.where(pick, 1.0, sel)
            gsc = jnp.where(pick, -jnp.inf, gsc)
        sel_ref[h] = sel

    m_ref[...] = jnp.full(m_ref.shape, NEG_BIG, F32)
    l_ref[...] = jnp.zeros(l_ref.shape, F32)
    acc_ref[...] = jnp.zeros(acc_ref.shape, F32)

    def tile(j, own):
        k0 = pl.multiple_of(j * blk, blk)
        kt = k_ref[0, pl.ds(k0, blk), :]
        for h in range(2):
            s = jnp.dot(kt, qpad[h], preferred_element_type=F32)
            u = s - sd_ref[h]
            if own:
                ik = lax.broadcasted_iota(I32, (blk, blk), 0)
                iq = lax.broadcasted_iota(I32, (blk, blk), 1)
                mask = ik <= iq
            else:
                mask = sel_ref[h, pl.ds(j, 1), :] > 0.0
            z = jnp.where(mask, u, NEG_BIG)
            c = sl_ref[h] * ((qi - j) * blk).astype(F32)
            vt = vT_ref[0, h * HEAD_DIM:(h + 1) * HEAD_DIM, pl.ds(k0, blk)]
            _online_update(h, z, c, vt, m_ref, l_ref, acc_ref)

    tile(qi, True)

    def past(j, carry):
        tile(j, False)
        return carry
    lax.fori_loop(0, qi, past, 0)

    o = jnp.concatenate(
        [acc_ref[h] / jnp.maximum(l_ref[h], 1e-30) for h in range(2)], axis=0)
    o_ref[0] = o.T.astype(BF16)


def _moba(mqT, mk, mvT, sd, sl):
    B, _, S = mqT.shape
    blk = MOBA_BLOCK
    nb = S // blk
    topk = min(MOBA_TOPK, nb)
    grid = (B, MOBA_HEADS // 2, nb)
    return pl.pallas_call(
        functools.partial(_moba_kernel, nb=nb, topk=topk), grid=grid,
        in_specs=[
            pl.BlockSpec((1, 128, blk), lambda b, p, i: (b, p, i)),
            pl.BlockSpec((1, S, 128), lambda b, p, i: (b, 0, p)),
            pl.BlockSpec((1, 128, S), lambda b, p, i: (b, p, 0)),
            pl.BlockSpec((2, blk, blk), lambda b, p, i: (p, 0, 0)),
            pl.BlockSpec((2, 1, blk), lambda b, p, i: (p, 0, 0)),
        ],
        out_specs=pl.BlockSpec((1, blk, 128), lambda b, p, i: (b, i, p)),
        out_shape=jax.ShapeDtypeStruct((B, S, 512), BF16),
        scratch_shapes=[
            pltpu.VMEM((nb, 128), F32),
            pltpu.VMEM((2, nb, blk), F32),
            pltpu.VMEM((2, 1, blk), F32),
            pltpu.VMEM((2, 1, blk), F32),
            pltpu.VMEM((2, HEAD_DIM, blk), F32),
        ],
        compiler_params=_params(3), name="moba",
    )(mqT, mk, mvT, sd, sl)


def _compress_kernel(kc_ref, vc_ref, wk_ref, wvT_ref, pek_ref, pev_ref, w2k_ref, w2vT_ref,
                     kcmp_ref, vcmpT_ref):
    nc = kc_ref.shape[1]

    wk = wk_ref[...]
    ab = jnp.dot(kc_ref[0], wk, preferred_element_type=F32)
    pt = (jnp.dot(pek_ref[0], wk[:, 0:128].astype(F32), preferred_element_type=F32)
          + jnp.dot(pek_ref[1], wk[:, 128:256].astype(F32), preferred_element_type=F32))
    pre = ab[:, 0:128] + pltpu.roll(ab[:, 128:256], nc - 1, 0) + pt[0:1]
    hid = jax.nn.gelu(pre)
    kcmp_ref[0] = jnp.dot(hid.astype(BF16), w2k_ref[...], preferred_element_type=F32).astype(BF16)

    wvT = wvT_ref[...]
    abT = lax.dot_general(wvT, vc_ref[0], NT_DIMS, preferred_element_type=F32)
    ptT = (lax.dot_general(wvT[0:128].astype(F32), pev_ref[0], NT_DIMS, preferred_element_type=F32)
           + lax.dot_general(wvT[128:256].astype(F32), pev_ref[1], NT_DIMS, preferred_element_type=F32))
    preT = abT[0:128] + pltpu.roll(abT[128:256], nc - 1, 1) + ptT[:, 0:1]
    hidT = jax.nn.gelu(preT)
    vcmpT_ref[0] = jnp.dot(w2vT_ref[...], hidT.astype(BF16), preferred_element_type=F32).astype(BF16)


def _compress(kc2, vc2, wk, wvT, pek, pev, w2k, w2vT):
    B, nc, _ = kc2.shape
    full = lambda a: pl.BlockSpec(a.shape, lambda b: (0,) * a.ndim)
    blk = pl.BlockSpec((1, nc, kc2.shape[2]), lambda b: (b, 0, 0))
    return pl.pallas_call(
        _compress_kernel, grid=(B,),
        in_specs=[blk, blk, full(wk), full(wvT), full(pek), full(pev), full(w2k), full(w2vT)],
        out_specs=[pl.BlockSpec((1, nc, 128), lambda b: (b, 0, 0)),
                   pl.BlockSpec((1, 128, nc), lambda b: (b, 0, 0))],
        out_shape=[jax.ShapeDtypeStruct((B, nc, 128), BF16),
                   jax.ShapeDtypeStruct((B, 128, nc), BF16)],
        compiler_params=_params(1), name="nsa_compress",
    )(kc2, vc2, wk, wvT, pek, pev, w2k, w2vT)


def _nsa_kernel(qT_ref, kcmp_ref, vcmpT_ref, ks_ref, vsT_ref, kw_ref, vwT_ref,
                g_ref, ovT_ref, sl_ref, bs_ref, o_ref,
                sel_ref, m_ref, l_ref, acc_ref, *, n_slc, topn):
    g = pl.program_id(1)
    qi = pl.program_id(2)
    tq = NSA_TQ
    hg = NSA_GROUP
    wl = hg * tq
    q0 = qi * tq

    q4 = qT_ref[0]
    qT = jnp.concatenate([q4[h * HEAD_DIM:(h + 1) * HEAD_DIM] for h in range(hg)], axis=1)
    qT2 = jnp.concatenate([qT, qT], axis=0)
    rowi = lax.broadcasted_iota(I32, qT2.shape, 0)
    qpad = jnp.where((rowi >> 6) == g, qT2, jnp.zeros_like(qT2))
    slope = sl_ref[0]
    lane = lax.broadcasted_iota(I32, (1, wl), 1)
    t_q = q0 + (lane & (tq - 1))

    nc = kcmp_ref.shape[1]
    s = jnp.dot(kcmp_ref[0], qpad, preferred_element_type=F32)
    ci = lax.broadcasted_iota(I32, (nc, wl), 0)
    dist = t_q - (ci * NSA_CMP_STRIDE + (NSA_CMP_LEN - 1))
    mask = dist >= 0
    z = jnp.where(mask, s - slope * dist.astype(F32), NEG_BIG)
    mx = jnp.max(z, axis=0, keepdims=True)
    p = jnp.where(mask, jnp.exp2(z - mx), 0.0)
    p = p / jnp.maximum(jnp.sum(p, axis=0, keepdims=True), 1e-30)
    o_c = jnp.dot(vcmpT_ref[0], p.astype(BF16), preferred_element_type=F32)

    pc = p[:, 0:tq]
    for h in range(1, hg):
        pc = pc + p[:, h * tq:(h + 1) * tq]
    imp = jnp.dot(ovT_ref[...], pc, precision=lax.Precision.HIGHEST,
                  preferred_element_type=F32)
    cur = (q0 + lax.broadcasted_iota(I32, (1, tq), 1)) >> 6
    jb = lax.broadcasted_iota(I32, (n_slc, tq), 0)
    allowed = jb <= cur
    forced = jnp.logical_or(jb == 0, jnp.logical_or(jb == cur, jb == cur - 1))
    sel = jnp.where(jnp.logical_and(allowed, forced), 1.0, 0.0)
    sc = jnp.where(jnp.logical_and(allowed, jnp.logical_not(forced)), imp, -1.0)
    for _ in range(topn - 3):
        smx = jnp.max(sc, axis=0, keepdims=True)
        idx = jnp.min(jnp.where(sc == smx, jb, n_slc), axis=0, keepdims=True)
        pick = jnp.logical_and(jb == idx, smx >= 0.0)
        sel = jnp.where(pick, 1.0, sel)
        sc = jnp.where(pick, -1.0, sc)
    sel_ref[...] = sel

    m_ref[...] = jnp.full(m_ref.shape, NEG_BIG, F32)
    l_ref[...] = jnp.zeros(l_ref.shape, F32)
    acc_ref[...] = jnp.zeros(acc_ref.shape, F32)
    per = SLC_TILE // NSA_SLC_BLOCK

    def slc_tile(j, diag):
        k0 = pl.multiple_of(j * SLC_TILE, SLC_TILE)
        kt = ks_ref[0, pl.ds(k0, SLC_TILE), :]
        u = jnp.dot(kt, qpad, preferred_element_type=F32) - bs_ref[0]
        rows = []
        for mb in range(per):
            r = sel_ref[pl.ds(j * per + mb, 1), :]
            r = jnp.concatenate([r] * hg, axis=1)
            rows.append(jnp.broadcast_to(r, (NSA_SLC_BLOCK, wl)))
        msk = jnp.concatenate(rows, axis=0) > 0.0
        if diag:
            t_k = k0 + lax.broadcasted_iota(I32, (SLC_TILE, wl), 0)
            msk = jnp.logical_and(msk, t_k <= t_q)
        z = jnp.where(msk, u, NEG_BIG)
        c = slope * (q0 - k0).astype(F32)
        vt = vsT_ref[0, :, pl.ds(k0, SLC_TILE)]
        _online_update(0, z, c, vt, m_ref, l_ref, acc_ref)

    jd = lax.div(q0, SLC_TILE)
    slc_tile(jd, True)

    def past(j, carry):
        slc_tile(j, False)
        return carry
    lax.fori_loop(0, jd, past, 0)
    o_s = acc_ref[0] / jnp.maximum(l_ref[0], 1e-30)

    start = pl.multiple_of(jnp.maximum(q0 - NSA_WINDOW, 0), tq)
    kt = kw_ref[0, pl.ds(start, WIN_KEYS), :]
    s = jnp.dot(kt, qpad, preferred_element_type=F32)
    d = t_q - (start + lax.broadcasted_iota(I32, (WIN_KEYS, wl), 0))
    mask = jnp.logical_and(d >= 0, d < NSA_WINDOW)
    z = jnp.where(mask, s - slope * d.astype(F32), NEG_BIG)
    mx = jnp.max(z, axis=0, keepdims=True)
    p = jnp.exp2(z - mx)
    den = jnp.maximum(jnp.sum(p, axis=0, keepdims=True), 1e-30)
    o_w = jnp.dot(vwT_ref[0, :, pl.ds(start, WIN_KEYS)], p.astype(BF16),
                  preferred_element_type=F32) / den

    gt = g_ref[0]

    def gate_row(br):
        return jnp.concatenate([gt[br * hg + h:br * hg + h + 1] for h in range(hg)], axis=1)

    o = gate_row(0) * o_c + gate_row(1) * o_s + gate_row(2) * o_w
    o4 = jnp.concatenate([o[:, h * tq:(h + 1) * tq] for h in range(hg)], axis=0)
    o_ref[0] = o4.T.astype(BF16)


def _nsa(nqT, kcmp, vcmpT, ks, vsT, kw, vwT, gT, ovT, sl, bs):
    B, _, S = nqT.shape
    tq = NSA_TQ
    nc = kcmp.shape[1]
    n_slc = S // NSA_SLC_BLOCK
    topn = min(NSA_SLC_TOPN, n_slc)
    wl = NSA_GROUP * tq
    grid = (B, NSA_KV_HEADS, S // tq)
    return pl.pallas_call(
        functools.partial(_nsa_kernel, n_slc=n_slc, topn=topn), grid=grid,
        in_specs=[
            pl.BlockSpec((1, NSA_GROUP * HEAD_DIM, tq), lambda b, g, i: (b, g, i)),
            pl.BlockSpec((1, nc, 128), lambda b, g, i: (b, 0, 0)),
            pl.BlockSpec((1, HEAD_DIM, nc), lambda b, g, i: (b, g, 0)),
            pl.BlockSpec((1, S, 128), lambda b, g, i: (b, 0, 0)),
            pl.BlockSpec((1, HEAD_DIM, S), lambda b, g, i: (b, g, 0)),
            pl.BlockSpec((1, S, 128), lambda b, g, i: (b, 0, 0)),
            pl.BlockSpec((1, HEAD_DIM, S), lambda b, g, i: (b, g, 0)),
            pl.BlockSpec((1, 16, tq), lambda b, g, i: (b, g, i)),
            pl.BlockSpec((n_slc, nc), lambda b, g, i: (0, 0)),
            pl.BlockSpec((1, 1, wl), lambda b, g, i: (g, 0, 0)),
            pl.BlockSpec((1, SLC_TILE, wl), lambda b, g, i: (g, 0, 0)),
        ],
        out_specs=pl.BlockSpec((1, tq, NSA_GROUP * HEAD_DIM), lambda b, g, i: (b, i, g)),
        out_shape=jax.ShapeDtypeStruct((B, S, 512), BF16),
        scratch_shapes=[
            pltpu.VMEM((n_slc, tq), F32),
            pltpu.VMEM((1, 1, wl), F32),
            pltpu.VMEM((1, 1, wl), F32),
            pltpu.VMEM((1, HEAD_DIM, wl), F32),
        ],
        compiler_params=_params(3), name="nsa",
    )(nqT, kcmp, vcmpT, ks, vsT, kw, vwT, gT, ovT, sl, bs)


def _outproj_kernel(om_ref, on_ref, x_ref, wo_ref, g_ref, wr_ref, br_ref,
                    x1_ref, hn_ref, e_ref, w_ref):
    attn = (jnp.dot(om_ref[...], wo_ref[0:512, :], preferred_element_type=F32)
            + jnp.dot(on_ref[...], wo_ref[512:1024, :], preferred_element_type=F32))
    x1 = x_ref[...] + attn
    x1_ref[...] = x1
    hn = _rmsnorm(x1, g_ref[...])
    hn_ref[...] = hn
    logits = jnp.dot(hn, wr_ref[...], precision=lax.Precision.HIGHEST,
                     preferred_element_type=F32) + br_ref[...]
    tm = logits.shape[0]
    lane = lax.broadcasted_iota(I32, (tm, LANES), 1)
    sc = jnp.where(lane < N_EXPERTS, logits, -jnp.inf)
    e_out = jnp.zeros((tm, LANES), I32)
    vals = []
    for k in range(TOP_K):
        mx = jnp.max(sc, axis=1, keepdims=True)
        idx = jnp.min(jnp.where(sc == mx, lane, LANES), axis=1, keepdims=True)
        e_out = jnp.where(lane == k, idx, e_out)
        sc = jnp.where(lane == idx, -jnp.inf, sc)
        vals.append(mx)
    ex = [jnp.exp(v - vals[0]) for v in vals]
    den = ex[0] + ex[1] + ex[2] + ex[3]
    w_out = jnp.zeros((tm, LANES), F32)
    for k in range(TOP_K):
        w_out = jnp.where(lane == k, ex[k] / den, w_out)
    e_ref[...] = e_out
    w_ref[...] = w_out


def _outproj(om, on, x, wo, g, wr, br, tm):
    N, D = x.shape
    full = lambda a: pl.BlockSpec(a.shape, lambda i: (0,) * a.ndim)
    row = lambda w: pl.BlockSpec((tm, w), lambda i: (i, 0))
    return pl.pallas_call(
        _outproj_kernel, grid=(N // tm,),
        in_specs=[row(512), row(512), row(D), full(wo), full(g), full(wr), full(br)],
        out_specs=[row(D), row(D), row(LANES), row(LANES)],
        out_shape=[jax.ShapeDtypeStruct((N, D), F32), jax.ShapeDtypeStruct((N, D), F32),
                   jax.ShapeDtypeStruct((N, LANES), I32), jax.ShapeDtypeStruct((N, LANES), F32)],
        compiler_params=_params(1), name="outproj_router",
    )(om, on, x, wo, g, wr, br)


def _rank_kernel(e_ref, rank_ref, cnt_ref, base_ref):
    i = pl.program_id(0)
    T = e_ref.shape[0]

    @pl.when(i == 0)
    def _():
        base_ref[...] = jnp.zeros(base_ref.shape, F32)

    e = e_ref[...]
    lane = lax.broadcasted_iota(I32, (T, LANES), 1)
    tril = jnp.where(lax.broadcasted_iota(I32, (T, T), 0) >= lax.broadcasted_iota(I32, (T, T), 1),
                     1.0, 0.0).astype(BF16)
    out = jnp.zeros((T, LANES), I32)
    for k in range(TOP_K):
        hit = lane == e[:, k:k + 1]
        oh = jnp.where(hit, 1.0, 0.0)
        cum = jnp.dot(tril, oh.astype(BF16), preferred_element_type=F32)
        base = base_ref[0:1, :]
        r = jnp.sum(jnp.where(hit, cum - 1.0 + base, 0.0), axis=1, keepdims=True)
        out = jnp.where(lane == k, r.astype(I32), out)
        base_ref[...] = base_ref[...] + jnp.sum(oh, axis=0, keepdims=True)
    rank_ref[...] = out
    cnt_ref[...] = base_ref[...]


def _ranks(e128):
    N = e128.shape[0]
    T = RANK_TILE
    return pl.pallas_call(
        _rank_kernel, grid=(N // T,),
        in_specs=[pl.BlockSpec((T, LANES), lambda i: (i, 0))],
        out_specs=[pl.BlockSpec((T, LANES), lambda i: (i, 0)),
                   pl.BlockSpec((8, LANES), lambda i: (0, 0))],
        out_shape=[jax.ShapeDtypeStruct((N, LANES), I32),
                   jax.ShapeDtypeStruct((8, LANES), F32)],
        scratch_shapes=[pltpu.VMEM((8, LANES), F32)],
        compiler_params=_params(1), name="route_ranks",
    )(e128)


def _row_copy(src, dst, i_src, i_dst, sem):
    return pltpu.make_async_copy(src.at[pl.ds(i_src, 1)], dst.at[pl.ds(i_dst, 1)], sem)


def _dispatch_kernel(dest_hbm, hn_hbm, xz_hbm, out_hbm, idx_ref, isem, sem):
    del xz_hbm
    i = pl.program_id(0)
    T = ROUTE_TILE
    cp = pltpu.make_async_copy(dest_hbm.at[i], idx_ref, isem)
    cp.start()
    cp.wait()

    def issue(t, carry):
        for k in range(TOP_K):
            _row_copy(hn_hbm, out_hbm, i * T + t, idx_ref[t * TOP_K + k], sem).start()
        return carry
    lax.fori_loop(0, T, issue, 0)

    def drain(t, carry):
        for k in range(TOP_K):
            _row_copy(hn_hbm, out_hbm, 0, 0, sem).wait()
        return carry
    lax.fori_loop(0, T, drain, 0)


def _dispatch(dest2, hn, xzero):
    nsteps = dest2.shape[0]
    return pl.pallas_call(
        _dispatch_kernel, grid=(nsteps,),
        in_specs=[pl.BlockSpec(memory_space=pl.ANY)] * 3,
        out_specs=pl.BlockSpec(memory_space=pl.ANY),
        out_shape=jax.ShapeDtypeStruct(xzero.shape, xzero.dtype),
        scratch_shapes=[pltpu.SMEM((ROUTE_TILE * TOP_K,), I32),
                        pltpu.SemaphoreType.DMA, pltpu.SemaphoreType.DMA],
        input_output_aliases={2: 0},
        compiler_params=pltpu.CompilerParams(dimension_semantics=("arbitrary",),
                                             has_side_effects=True),
        name="moe_dispatch",
    )(dest2, hn, xzero)


def _expert_kernel(be_ref, na_ref, x_ref, wg_ref, wu_ref, bg_ref, bu_ref, wd_ref, bd_ref, y_ref):
    b = pl.program_id(0)

    @pl.when(b < na_ref[0])
    def _():
        xb = x_ref[...].astype(BF16)
        gg = jnp.dot(xb, wg_ref[0], preferred_element_type=F32) + bg_ref[0]
        uu = jnp.dot(xb, wu_ref[0], preferred_element_type=F32) + bu_ref[0]
        gg = jnp.minimum(gg, SWIGLU_LIMIT)
        uu = jnp.clip(uu, -SWIGLU_LIMIT, SWIGLU_LIMIT)
        a = gg * jax.nn.sigmoid(SWIGLU_ALPHA * gg) * (uu + 1.0)
        y_ref[...] = jnp.dot(a.astype(BF16), wd_ref[0], preferred_element_type=F32) + bd_ref[0]

    @pl.when(b >= na_ref[0])
    def _():
        y_ref[...] = jnp.zeros(y_ref.shape, F32)


def _experts(blk_e, n_act, xrows, wg, wu, bg, bu, wd, bd):
    P, D = xrows.shape
    F = wg.shape[2]
    n_blk = P // MOE_ROWS
    wspec = lambda r, c: pl.BlockSpec((1, r, c), lambda b, be, na: (be[b], 0, 0))
    grid_spec = pltpu.PrefetchScalarGridSpec(
        num_scalar_prefetch=2, grid=(n_blk,),
        in_specs=[pl.BlockSpec((MOE_ROWS, D), lambda b, be, na: (b, 0)),
                  wspec(D, F), wspec(D, F), wspec(1, F), wspec(1, F), wspec(F, D), wspec(1, D)],
        out_specs=pl.BlockSpec((MOE_ROWS, D), lambda b, be, na: (b, 0)),
    )
    return pl.pallas_call(
        _expert_kernel, grid_spec=grid_spec,
        out_shape=jax.ShapeDtypeStruct((P, D), F32),
        compiler_params=_params(1), name="moe_experts",
    )(blk_e, n_act, xrows, wg, wu, bg, bu, wd, bd)


def _combine_kernel(dest_hbm, x1_ref, w_ref, g_ref, y_hbm, o_ref, idx_ref, buf_ref, isem, sem):
    i = pl.program_id(0)
    T = ROUTE_TILE
    cp = pltpu.make_async_copy(dest_hbm.at[i], idx_ref, isem)
    cp.start()
    cp.wait()

    def issue(t, carry):
        for k in range(TOP_K):
            _row_copy(y_hbm, buf_ref.at[k], idx_ref[t * TOP_K + k], t, sem).start()
        return carry
    lax.fori_loop(0, T, issue, 0)

    def drain(t, carry):
        for k in range(TOP_K):
            _row_copy(y_hbm, buf_ref.at[k], 0, 0, sem).wait()
        return carry
    lax.fori_loop(0, T, drain, 0)

    acc = x1_ref[...]
    w = w_ref[...]
    for k in range(TOP_K):
        acc = acc + w[:, k:k + 1] * buf_ref[k]
    o_ref[...] = _rmsnorm(acc, g_ref[...])


def _combine(dest2, x1, w128, g, yrows):
    N, D = x1.shape
    T = ROUTE_TILE
    return pl.pallas_call(
        _combine_kernel, grid=(N // T,),
        in_specs=[pl.BlockSpec(memory_space=pl.ANY),
                  pl.BlockSpec((T, D), lambda i: (i, 0)),
                  pl.BlockSpec((T, LANES), lambda i: (i, 0)),
                  pl.BlockSpec(g.shape, lambda i: (0, 0)),
                  pl.BlockSpec(memory_space=pl.ANY)],
        out_specs=pl.BlockSpec((T, D), lambda i: (i, 0)),
        out_shape=jax.ShapeDtypeStruct((N, D), F32),
        scratch_shapes=[pltpu.SMEM((T * TOP_K,), I32),
                        pltpu.VMEM((TOP_K, T, D), F32),
                        pltpu.SemaphoreType.DMA, pltpu.SemaphoreType.DMA],
        compiler_params=_params(1), name="moe_combine",
    )(dest2, x1, w128, g, yrows)


def _alibi_slopes():
    n = MOBA_HEADS + NSA_HEADS
    s = jnp.exp2(-8.0 * jnp.arange(1, n + 1, dtype=F32) / n)
    return s[0::2], s[1::2]


def _prep_inproj(w_in):
    hd = HEAD_DIM
    sizes = [MOBA_HEADS * hd] * 3 + [NSA_HEADS * hd] + [NSA_KV_HEADS * hd] * 6 + [NSA_BRANCHES * NSA_HEADS]
    cuts = np.cumsum([0] + sizes)
    mq, mk, mv, nq, kc, vc, ks, vs, kw, vw, ng = [w_in[:, cuts[i]:cuts[i + 1]] for i in range(11)]
    qscale = (hd ** -0.5) * LOG2E
    wr = jnp.concatenate([mk, kc, vc, ks, kw], axis=1).astype(BF16)
    wt = jnp.concatenate([mq * qscale, mv, nq * qscale, vs, vw], axis=1).T.astype(BF16)
    ngr = ng.reshape(-1, NSA_KV_HEADS, NSA_GROUP, NSA_BRANCHES).transpose(1, 3, 2, 0)
    ngr = ngr.reshape(NSA_KV_HEADS, NSA_BRANCHES * NSA_GROUP, -1)
    wg = jnp.pad(ngr, ((0, 0), (0, 16 - NSA_BRANCHES * NSA_GROUP), (0, 0))).reshape(32, -1)
    return wr, wt, wg.astype(F32)


def _prep_compress(w1, w2, pe):
    hd, half = HEAD_DIM, NSA_CMP_STRIDE
    w1r = w1.reshape(2, half, hd, hd)
    eye = jnp.eye(NSA_KV_HEADS, dtype=w1.dtype)
    w = jnp.einsum('alde,gh->lgdahe', w1r, eye).reshape(half * NSA_KV_HEADS * hd, 2 * NSA_KV_HEADS * hd)
    w2b = jnp.einsum('de,gh->gdhe', w2, eye).reshape(NSA_KV_HEADS * hd, NSA_KV_HEADS * hd)
    per = pe.reshape(2, half, 1, hd)
    pe2 = jnp.broadcast_to(per, (2, half, NSA_KV_HEADS, hd)).reshape(2, 1, half * NSA_KV_HEADS * hd)
    pe2 = jnp.broadcast_to(pe2, (2, 8, pe2.shape[2]))
    return w.astype(BF16), w2b.astype(BF16), pe2.astype(F32)


def _attention_tables(S):
    moba_sl, nsa_sl = _alibi_slopes()
    moba_sl = moba_sl * LOG2E
    nsa_sl = nsa_sl * LOG2E
    blk = MOBA_BLOCK
    ik = jnp.arange(blk, dtype=F32)[:, None]
    iq = jnp.arange(blk, dtype=F32)[None, :]
    moba_sd = moba_sl[:, None, None] * (iq - ik)[None]
    moba_row = jnp.broadcast_to(moba_sl[:, None, None], (MOBA_HEADS, 1, blk))
    wl = NSA_GROUP * NSA_TQ
    nsa_row = jnp.repeat(nsa_sl.reshape(NSA_KV_HEADS, NSA_GROUP), NSA_TQ, axis=1)
    il = jnp.tile(jnp.arange(NSA_TQ, dtype=F32), NSA_GROUP)[None, None, :]
    ikk = jnp.arange(SLC_TILE, dtype=F32)[None, :, None]
    nsa_bs = nsa_row[:, None, :] * (il - ikk)
    nc = S // NSA_CMP_STRIDE
    n_slc = S // NSA_SLC_BLOCK
    cu, su = NSA_CMP_LEN // NSA_CMP_STRIDE, NSA_SLC_BLOCK // NSA_CMP_STRIDE
    ii = jnp.arange(nc)[None, :]
    jj = jnp.arange(n_slc)[:, None]
    ovT = jnp.clip(jnp.minimum(ii + cu, (jj + 1) * su) - jnp.maximum(ii, jj * su), 0, None).astype(F32)
    return moba_sd, moba_row, nsa_row.reshape(NSA_KV_HEADS, 1, wl), nsa_bs, ovT


def _attention(x, attn_norm_g, w_in, cmp_pe_k, cmp_pe_v, cmp_w1_k, cmp_w2_k, cmp_w1_v, cmp_w2_v):
    B, S, D = x.shape
    wr, wt, wg = _prep_inproj(w_in)
    (mk, kc, vc, ks, kw, mqT, mvT, nqT, vsT, vwT, gT) = _inproj(
        x, attn_norm_g.reshape(1, D), wr, wt, wg, tm=512)
    moba_sd, moba_row, nsa_row, nsa_bs, ovT = _attention_tables(S)
    o_moba = _moba(mqT, mk, mvT, moba_sd, moba_row)
    wk, w2k, pek = _prep_compress(cmp_w1_k, cmp_w2_k, cmp_pe_k)
    wv, w2v, pev = _prep_compress(cmp_w1_v, cmp_w2_v, cmp_pe_v)
    nc = S // NSA_CMP_STRIDE
    kcmp, vcmpT = _compress(kc.reshape(B, nc, -1), vc.reshape(B, nc, -1), wk, wv.T, pek, pev, w2k, w2v.T)
    o_nsa = _nsa(nqT, kcmp, vcmpT, ks, vsT, kw, vwT, gT, ovT, nsa_row, nsa_bs)
    return o_moba, o_nsa


def _moe(x1, hn, e128, w128, w_up, b_up, w_down, b_down, final_norm_g):
    N, D = x1.shape
    rank128, cnt = _ranks(e128)
    counts = cnt[0, :N_EXPERTS].astype(I32)
    padded = (counts + MOE_ROWS - 1) // MOE_ROWS * MOE_ROWS
    pends = jnp.cumsum(padded)
    pstarts = pends - padded
    e4 = e128[:, :TOP_K]
    dest = pstarts[e4] + rank128[:, :TOP_K]
    dest2 = dest.reshape(N // ROUTE_TILE, ROUTE_TILE * TOP_K)
    n_blk = (N * TOP_K + N_EXPERTS * MOE_ROWS + MOE_ROWS - 1) // MOE_ROWS
    P = n_blk * MOE_ROWS
    blk_e = jnp.minimum(jnp.searchsorted(pends, jnp.arange(n_blk, dtype=I32) * MOE_ROWS, side='right'),
                        N_EXPERTS - 1).astype(I32)
    n_act = (pends[-1:] // MOE_ROWS).astype(I32)
    xrows = _dispatch(dest2, hn, jnp.zeros((P, D), F32))
    wg = w_up[:, :, 0::2].astype(BF16)
    wu = w_up[:, :, 1::2].astype(BF16)
    bg = b_up[:, None, 0::2]
    bu = b_up[:, None, 1::2]
    yrows = _experts(blk_e, n_act, xrows, wg, wu, bg, bu, w_down.astype(BF16), b_down[:, None, :])
    return _combine(dest2, x1, w128, final_norm_g.reshape(1, D), yrows)


def kernel(x, attn_norm_g, w_in, cmp_pe_k, cmp_pe_v, cmp_w1_k, cmp_w2_k, cmp_w1_v, cmp_w2_v, w_out, ffn_norm_g, w_router, b_router, w_up, b_up, w_down, b_down, final_norm_g):
    B, S, D = x.shape
    assert attn_norm_g.shape[0] == 1, "single-layer kernel"
    o_moba, o_nsa = _attention(x, attn_norm_g[0], w_in[0], cmp_pe_k[0], cmp_pe_v[0],
                               cmp_w1_k[0], cmp_w2_k[0], cmp_w1_v[0], cmp_w2_v[0])
    N = B * S
    wr = jnp.pad(w_router[0], ((0, 0), (0, LANES - N_EXPERTS)))
    br = jnp.pad(b_router[0], (0, LANES - N_EXPERTS)).reshape(1, LANES)
    x1, hn, e128, w128 = _outproj(o_moba.reshape(N, -1), o_nsa.reshape(N, -1), x.reshape(N, D),
                                  w_out[0].astype(BF16), ffn_norm_g[0].reshape(1, D), wr, br, tm=512)
    out = _moe(x1, hn, e128, w128, w_up[0], b_up[0], w_down[0], b_down[0], final_norm_g)
    return out.reshape(B, S, D)
```

```python
import functools

import jax
import jax.numpy as jnp
import numpy as np
from jax import lax
from jax.experimental import pallas as pl
from jax.experimental.pallas import tpu as pltpu

F32 = jnp.float32
BF16 = jnp.bfloat16
I32 = jnp.int32

HEAD_DIM = 64
MOBA_HEADS = 8
NSA_HEADS = 8
NSA_KV_HEADS = 2
NSA_GROUP = NSA_HEADS // NSA_KV_HEADS
MOBA_BLOCK = 256
MOBA_TOPK = 3
NSA_CMP_LEN = 32
NSA_CMP_STRIDE = 16
NSA_SLC_BLOCK = 64
NSA_SLC_TOPN = 16
NSA_WINDOW = 512
NSA_BRANCHES = 3
N_EXPERTS = 32
TOP_K = 4
SWIGLU_LIMIT = 7.0
SWIGLU_ALPHA = 1.702
RMS_EPS = 1e-5
NEG_BIG = -1e30
LOG2E = 1.4426950408889634

LANES = 128
VMEM_LIMIT = 56 * 1024 * 1024

NSA_TQ = 128
SLC_TILE = 256
WIN_KEYS = NSA_WINDOW + NSA_TQ
MOE_ROWS = 512
ROUTE_TILE = 256
RANK_TILE = 512

NT_DIMS = (((1,), (1,)), ((), ()))


def _params(n_grid):
    return pltpu.CompilerParams(
        dimension_semantics=("arbitrary",) * n_grid,
        vmem_limit_bytes=VMEM_LIMIT,
    )


def _rmsnorm(x, g):
    return x * lax.rsqrt(jnp.mean(x * x, axis=-1, keepdims=True) + RMS_EPS) * g


def _inproj_kernel(x_ref, g_ref, wr_ref, wt_ref, wg_ref,
                   mk_ref, kc_ref, vc_ref, ks_ref, kw_ref,
                   mqT_ref, mvT_ref, nqT_ref, vsT_ref, vwT_ref, gT_ref):
    xn = _rmsnorm(x_ref[0], g_ref[...])
    xb = xn.astype(BF16)
    yr = jnp.dot(xb, wr_ref[...], preferred_element_type=F32)
    mk_ref[0] = yr[:, 0:512].astype(BF16)
    kc_ref[0] = yr[:, 512:640].astype(BF16)
    vc_ref[0] = yr[:, 640:768].astype(BF16)
    ks_ref[0] = yr[:, 768:896].astype(BF16)
    kw_ref[0] = yr[:, 896:1024].astype(BF16)
    yt = lax.dot_general(wt_ref[...], xb, NT_DIMS, preferred_element_type=F32)
    mqT_ref[0] = yt[0:512].astype(BF16)
    mvT_ref[0] = yt[512:1024].astype(BF16)
    nqT_ref[0] = yt[1024:1536].astype(BF16)
    vsT_ref[0] = yt[1536:1664].astype(BF16)
    vwT_ref[0] = yt[1664:1792].astype(BF16)
    gl = lax.dot_general(wg_ref[...], xn, NT_DIMS, precision=lax.Precision.HIGHEST,
                         preferred_element_type=F32)
    gT_ref[0] = jax.nn.sigmoid(gl)


def _inproj(x, g, wr, wt, wg, tm):
    B, S, D = x.shape
    grid = (B, S // tm)
    row = lambda w: pl.BlockSpec((1, tm, w), lambda b, i: (b, i, 0))
    col = lambda h: pl.BlockSpec((1, h, tm), lambda b, i: (b, 0, i))
    full = lambda a: pl.BlockSpec(a.shape, lambda b, i: (0,) * a.ndim)
    out_shape = [
        jax.ShapeDtypeStruct((B, S, 512), BF16),
        jax.ShapeDtypeStruct((B, S, 128), BF16),
        jax.ShapeDtypeStruct((B, S, 128), BF16),
        jax.ShapeDtypeStruct((B, S, 128), BF16),
        jax.ShapeDtypeStruct((B, S, 128), BF16),
        jax.ShapeDtypeStruct((B, 512, S), BF16),
        jax.ShapeDtypeStruct((B, 512, S), BF16),
        jax.ShapeDtypeStruct((B, 512, S), BF16),
        jax.ShapeDtypeStruct((B, 128, S), BF16),
        jax.ShapeDtypeStruct((B, 128, S), BF16),
        jax.ShapeDtypeStruct((B, 32, S), F32),
    ]
    out_specs = [row(512), row(128), row(128), row(128), row(128),
                 col(512), col(512), col(512), col(128), col(128), col(32)]
    return pl.pallas_call(
        _inproj_kernel, grid=grid,
        in_specs=[pl.BlockSpec((1, tm, D), lambda b, i: (b, i, 0)),
                  full(g), full(wr), full(wt), full(wg)],
        out_specs=out_specs, out_shape=out_shape,
        compiler_params=_params(2), name="inproj",
    )(x, g, wr, wt, wg)


def _online_update(h, z, c, vt, m_ref, l_ref, acc_ref):
    mt = jnp.max(z, axis=0, keepdims=True) - c
    m_old = m_ref[h]
    m_new = jnp.maximum(m_old, mt)
    alpha = jnp.exp2(m_old - m_new)
    p = jnp.exp2(z - (m_new + c))
    l_ref[h] = alpha * l_ref[h] + jnp.sum(p, axis=0, keepdims=True)
    acc_ref[h] = alpha * acc_ref[h] + jnp.dot(vt, p.astype(BF16), preferred_element_type=F32)
    m_ref[h] = m_new


def _moba_kernel(qT_ref, k_ref, vT_ref, sd_ref, sl_ref, o_ref,
                 kmean_ref, sel_ref, m_ref, l_ref, acc_ref, *, nb, topk):
    qi = pl.program_id(2)
    blk = MOBA_BLOCK

    @pl.when(qi == 0)
    def _():
        def body(n, carry):
            kb = k_ref[0, pl.ds(pl.multiple_of(n * blk, blk), blk), :].astype(F32)
            kmean_ref[pl.ds(n, 1), :] = jnp.mean(kb, axis=0, keepdims=True)
            return carry
        lax.fori_loop(0, nb, body, 0)

    qT = qT_ref[0]
    row = lax.broadcasted_iota(I32, qT.shape, 0)
    qpad = [jnp.where((row >> 6) == h, qT, jnp.zeros_like(qT)) for h in range(2)]

    bidx = lax.broadcasted_iota(I32, (nb, blk), 0)
    for h in range(2):
        gate = jnp.dot(kmean_ref[...], qpad[h].astype(F32),
                       precision=lax.Precision.HIGHEST, preferred_element_type=F32)
        gsc = jnp.where(bidx < qi, gate, -jnp.inf)
        sel = jnp.zeros((nb, blk), F32)
        for _ in range(topk):
            mx = jnp.max(gsc, axis=0, keepdims=True)
            idx = jnp.min(jnp.where(gsc == mx, bidx, nb), axis=0, keepdims=True)
            pick = jnp.logical_and(bidx == idx, mx > -jnp.inf)
            sel = jnp.where(pick, 1.0, sel)
            gsc = jnp.where(pick, -jnp.inf, gsc)
        sel_ref[h] = sel

    m_ref[...] = jnp.full(m_ref.shape, NEG_BIG, F32)
    l_ref[...] = jnp.zeros(l_ref.shape, F32)
    acc_ref[...] = jnp.zeros(acc_ref.shape, F32)

    def tile(j, own):
        k0 = pl.multiple_of(j * blk, blk)
        kt = k_ref[0, pl.ds(k0, blk), :]
        for h in range(2):
            s = jnp.dot(kt, qpad[h], preferred_element_type=F32)
            u = s - sd_ref[h]
            if own:
                ik = lax.broadcasted_iota(I32, (blk, blk), 0)
                iq = lax.broadcasted_iota(I32, (blk, blk), 1)
                mask = ik <= iq
            else:
                mask = sel_ref[h, pl.ds(j, 1), :] > 0.0
            z = jnp.where(mask, u, NEG_BIG)
            c = sl_ref[h] * ((qi - j) * blk).astype(F32)
            vt = vT_ref[0, h * HEAD_DIM:(h + 1) * HEAD_DIM, pl.ds(k0, blk)]
            _online_update(h, z, c, vt, m_ref, l_ref, acc_ref)

    tile(qi, True)

    def past(j, carry):
        tile(j, False)
        return carry
    lax.fori_loop(0, qi, past, 0)

    o = jnp.concatenate(
        [acc_ref[h] / jnp.maximum(l_ref[h], 1e-30) for h in range(2)], axis=0)
    o_ref[0] = o.T.astype(BF16)


def _moba(mqT, mk, mvT, sd, sl):
    B, _, S = mqT.shape
    blk = MOBA_BLOCK
    nb = S // blk
    topk = min(MOBA_TOPK, nb)
    grid = (B, MOBA_HEADS // 2, nb)
    return pl.pallas_call(
        functools.partial(_moba_kernel, nb=nb, topk=topk), grid=grid,
        in_specs=[
            pl.BlockSpec((1, 128, blk), lambda b, p, i: (b, p, i)),
            pl.BlockSpec((1, S, 128), lambda b, p, i: (b, 0, p)),
            pl.BlockSpec((1, 128, S), lambda b, p, i: (b, p, 0)),
            pl.BlockSpec((2, blk, blk), lambda b, p, i: (p, 0, 0)),
            pl.BlockSpec((2, 1, blk), lambda b, p, i: (p, 0, 0)),
        ],
        out_specs=pl.BlockSpec((1, blk, 128), lambda b, p, i: (b, i, p)),
        out_shape=jax.ShapeDtypeStruct((B, S, 512), BF16),
        scratch_shapes=[
            pltpu.VMEM((nb, 128), F32),
            pltpu.VMEM((2, nb, blk), F32),
            pltpu.VMEM((2, 1, blk), F32),
            pltpu.VMEM((2, 1, blk), F32),
            pltpu.VMEM((2, HEAD_DIM, blk), F32),
        ],
        compiler_params=_params(3), name="moba",
    )(mqT, mk, mvT, sd, sl)


def _compress_kernel(kc_ref, vc_ref, wk_ref, wvT_ref, pek_ref, pev_ref, w2k_ref, w2vT_ref,
                     kcmp_ref, vcmpT_ref):
    nc = kc_ref.shape[1]

    wk = wk_ref[...]
    ab = jnp.dot(kc_ref[0], wk, preferred_element_type=F32)
    pt = (jnp.dot(pek_ref[0], wk[:, 0:128].astype(F32), preferred_element_type=F32)
          + jnp.dot(pek_ref[1], wk[:, 128:256].astype(F32), preferred_element_type=F32))
    pre = ab[:, 0:128] + pltpu.roll(ab[:, 128:256], nc - 1, 0) + pt[0:1]
    hid = jax.nn.gelu(pre)
    kcmp_ref[0] = jnp.dot(hid.astype(BF16), w2k_ref[...], preferred_element_type=F32).astype(BF16)

    wvT = wvT_ref[...]
    abT = lax.dot_general(wvT, vc_ref[0], NT_DIMS, preferred_element_type=F32)
    ptT = (lax.dot_general(wvT[0:128].astype(F32), pev_ref[0], NT_DIMS, preferred_element_type=F32)
           + lax.dot_general(wvT[128:256].astype(F32), pev_ref[1], NT_DIMS, preferred_element_type=F32))
    preT = abT[0:128] + pltpu.roll(abT[128:256], nc - 1, 1) + ptT[:, 0:1]
    hidT = jax.nn.gelu(preT)
    vcmpT_ref[0] = jnp.dot(w2vT_ref[...], hidT.astype(BF16), preferred_element_type=F32).astype(BF16)


def _compress(kc2, vc2, wk, wvT, pek, pev, w2k, w2vT):
    B, nc, _ = kc2.shape
    full = lambda a: pl.BlockSpec(a.shape, lambda b: (0,) * a.ndim)
    blk = pl.BlockSpec((1, nc, kc2.shape[2]), lambda b: (b, 0, 0))
    return pl.pallas_call(
        _compress_kernel, grid=(B,),
        in_specs=[blk, blk, full(wk), full(wvT), full(pek), full(pev), full(w2k), full(w2vT)],
        out_specs=[pl.BlockSpec((1, nc, 128), lambda b: (b, 0, 0)),
                   pl.BlockSpec((1, 128, nc), lambda b: (b, 0, 0))],
        out_shape=[jax.ShapeDtypeStruct((B, nc, 128), BF16),
                   jax.ShapeDtypeStruct((B, 128, nc), BF16)],
        compiler_params=_params(1), name="nsa_compress",
    )(kc2, vc2, wk, wvT, pek, pev, w2k, w2vT)


def _nsa_kernel(qT_ref, kcmp_ref, vcmpT_ref, ks_ref, vsT_ref, kw_ref, vwT_ref,
                g_ref, ovT_ref, sl_ref, bs_ref, o_ref,
                sel_ref, m_ref, l_ref, acc_ref, *, n_slc, topn):
    g = pl.program_id(1)
    qi = pl.program_id(2)
    tq = NSA_TQ
    hg = NSA_GROUP
    wl = hg * tq
    q0 = qi * tq

    q4 = qT_ref[0]
    qT = jnp.concatenate([q4[h * HEAD_DIM:(h + 1) * HEAD_DIM] for h in range(hg)], axis=1)
    qT2 = jnp.concatenate([qT, qT], axis=0)
    rowi = lax.broadcasted_iota(I32, qT2.shape, 0)
    qpad = jnp.where((rowi >> 6) == g, qT2, jnp.zeros_like(qT2))
    slope = sl_ref[0]
    lane = lax.broadcasted_iota(I32, (1, wl), 1)
    t_q = q0 + (lane & (tq - 1))

    nc = kcmp_ref.shape[1]
    s = jnp.dot(kcmp_ref[0], qpad, preferred_element_type=F32)
    ci = lax.broadcasted_iota(I32, (nc, wl), 0)
    dist = t_q - (ci * NSA_CMP_STRIDE + (NSA_CMP_LEN - 1))
    mask = dist >= 0
    z = jnp.where(mask, s - slope * dist.astype(F32), NEG_BIG)
    mx = jnp.max(z, axis=0, keepdims=True)
    p = jnp.where(mask, jnp.exp2(z - mx), 0.0)
    p = p / jnp.maximum(jnp.sum(p, axis=0, keepdims=True), 1e-30)
    o_c = jnp.dot(vcmpT_ref[0], p.astype(BF16), preferred_element_type=F32)

    pc = p[:, 0:tq]
    for h in range(1, hg):
        pc = pc + p[:, h * tq:(h + 1) * tq]
    imp = jnp.dot(ovT_ref[...], pc, precision=lax.Precision.HIGHEST,
                  preferred_element_type=F32)
    cur = (q0 + lax.broadcasted_iota(I32, (1, tq), 1)) >> 6
    jb = lax.broadcasted_iota(I32, (n_slc, tq), 0)
    allowed = jb <= cur
    forced = jnp.logical_or(jb == 0, jnp.logical_or(jb == cur, jb == cur - 1))
    sel = jnp.where(jnp.logical_and(allowed, forced), 1.0, 0.0)
    sc = jnp.where(jnp.logical_and(allowed, jnp.logical_not(forced)), imp, -1.0)
    for _ in range(topn - 3):
        smx = jnp.max(sc, axis=0, keepdims=True)
        idx = jnp.min(jnp.where(sc == smx, jb, n_slc), axis=0, keepdims=True)
        pick = jnp.logical_and(jb == idx, smx >= 0.0)
        sel = jnp.where(pick, 1.0, sel)
        sc = jnp.where(pick, -1.0, sc)
    sel_ref[...] = sel

    m_ref[...] = jnp.full(m_ref.shape, NEG_BIG, F32)
    l_ref[...] = jnp.zeros(l_ref.shape, F32)
    acc_ref[...] = jnp.zeros(acc_ref.shape, F32)
    per = SLC_TILE // NSA_SLC_BLOCK

    def slc_tile(j, diag):
        k0 = pl.multiple_of(j * SLC_TILE, SLC_TILE)
        kt = ks_ref[0, pl.ds(k0, SLC_TILE), :]
        u = jnp.dot(kt, qpad, preferred_element_type=F32) - bs_ref[0]
        rows = []
        for mb in range(per):
            r = sel_ref[pl.ds(j * per + mb, 1), :]
            r = jnp.concatenate([r] * hg, axis=1)
            rows.append(jnp.broadcast_to(r, (NSA_SLC_BLOCK, wl)))
        msk = jnp.concatenate(rows, axis=0) > 0.0
        if diag:
            t_k = k0 + lax.broadcasted_iota(I32, (SLC_TILE, wl), 0)
            msk = jnp.logical_and(msk, t_k <= t_q)
        z = jnp.where(msk, u, NEG_BIG)
        c = slope * (q0 - k0).astype(F32)
        vt = vsT_ref[0, :, pl.ds(k0, SLC_TILE)]
        _online_update(0, z, c, vt, m_ref, l_ref, acc_ref)

    jd = lax.div(q0, SLC_TILE)
    slc_tile(jd, True)

    def past(j, carry):
        slc_tile(j, False)
        return carry
    lax.fori_loop(0, jd, past, 0)
    o_s = acc_ref[0] / jnp.maximum(l_ref[0], 1e-30)

    start = pl.multiple_of(jnp.maximum(q0 - NSA_WINDOW, 0), tq)
    kt = kw_ref[0, pl.ds(start, WIN_KEYS), :]
    s = jnp.dot(kt, qpad, preferred_element_type=F32)
    d = t_q - (start + lax.broadcasted_iota(I32, (WIN_KEYS, wl), 0))
    mask = jnp.logical_and(d >= 0, d < NSA_WINDOW)
    z = jnp.where(mask, s - slope * d.astype(F32), NEG_BIG)
    mx = jnp.max(z, axis=0, keepdims=True)
    p = jnp.exp2(z - mx)
    den = jnp.maximum(jnp.sum(p, axis=0, keepdims=True), 1e-30)
    o_w = jnp.dot(vwT_ref[0, :, pl.ds(start, WIN_KEYS)], p.astype(BF16),
                  preferred_element_type=F32) / den

    gt = g_ref[0]

    def gate_row(br):
        return jnp.concatenate([gt[br * hg + h:br * hg + h + 1] for h in range(hg)], axis=1)

    o = gate_row(0) * o_c + gate_row(1) * o_s + gate_row(2) * o_w
    o4 = jnp.concatenate([o[:, h * tq:(h + 1) * tq] for h in range(hg)], axis=0)
    o_ref[0] = o4.T.astype(BF16)


def _nsa(nqT, kcmp, vcmpT, ks, vsT, kw, vwT, gT, ovT, sl, bs):
    B, _, S = nqT.shape
    tq = NSA_TQ
    nc = kcmp.shape[1]
    n_slc = S // NSA_SLC_BLOCK
    topn = min(NSA_SLC_TOPN, n_slc)
    wl = NSA_GROUP * tq
    grid = (B, NSA_KV_HEADS, S // tq)
    return pl.pallas_call(
        functools.partial(_nsa_kernel, n_slc=n_slc, topn=topn), grid=grid,
        in_specs=[
            pl.BlockSpec((1, NSA_GROUP * HEAD_DIM, tq), lambda b, g, i: (b, g, i)),
            pl.BlockSpec((1, nc, 128), lambda b, g, i: (b, 0, 0)),
            pl.BlockSpec((1, HEAD_DIM, nc), lambda b, g, i: (b, g, 0)),
            pl.BlockSpec((1, S, 128), lambda b, g, i: (b, 0, 0)),
            pl.BlockSpec((1, HEAD_DIM, S), lambda b, g, i: (b, g, 0)),
            pl.BlockSpec((1, S, 128), lambda b, g, i: (b, 0, 0)),
            pl.BlockSpec((1, HEAD_DIM, S), lambda b, g, i: (b, g, 0)),
            pl.BlockSpec((1, 16, tq), lambda b, g, i: (b, g, i)),
            pl.BlockSpec((n_slc, nc), lambda b, g, i: (0, 0)),
            pl.BlockSpec((1, 1, wl), lambda b, g, i: (g, 0, 0)),
            pl.BlockSpec((1, SLC_TILE, wl), lambda b, g, i: (g, 0, 0)),
        ],
        out_specs=pl.BlockSpec((1, tq, NSA_GROUP * HEAD_DIM), lambda b, g, i: (b, i, g)),
        out_shape=jax.ShapeDtypeStruct((B, S, 512), BF16),
        scratch_shapes=[
            pltpu.VMEM((n_slc, tq), F32),
            pltpu.VMEM((1, 1, wl), F32),
            pltpu.VMEM((1, 1, wl), F32),
            pltpu.VMEM((1, HEAD_DIM, wl), F32),
        ],
        compiler_params=_params(3), name="nsa",
    )(nqT, kcmp, vcmpT, ks, vsT, kw, vwT, gT, ovT, sl, bs)


HI16 = 0xFFFF0000


def _pack_bf16_pairs(x):
    c = x.shape[1] // 2
    bits = pltpu.bitcast(x.astype(BF16).astype(F32), jnp.uint32)
    return (bits[:, c:] & jnp.uint32(HI16)) | (bits[:, :c] >> jnp.uint32(16))


def _unpack_bf16_pairs(p):
    lo = pltpu.bitcast(p << jnp.uint32(16), F32)
    hi = pltpu.bitcast(p & jnp.uint32(HI16), F32)
    return jnp.concatenate([lo, hi], axis=1).astype(BF16)


def _outproj_kernel(om_ref, on_ref, x_ref, wo_ref, g_ref, wr_ref, br_ref,
                    x1_ref, hn_ref, e_ref, w_ref):
    attn = (jnp.dot(om_ref[...], wo_ref[0:512, :], preferred_element_type=F32)
            + jnp.dot(on_ref[...], wo_ref[512:1024, :], preferred_element_type=F32))
    x1 = x_ref[...] + attn
    x1_ref[...] = x1
    hn = _rmsnorm(x1, g_ref[...])
    hn_ref[...] = _pack_bf16_pairs(hn)
    logits = jnp.dot(hn, wr_ref[...], precision=lax.Precision.HIGHEST,
                     preferred_element_type=F32) + br_ref[...]
    tm = logits.shape[0]
    lane = lax.broadcasted_iota(I32, (tm, LANES), 1)
    sc = jnp.where(lane < N_EXPERTS, logits, -jnp.inf)
    e_out = jnp.zeros((tm, LANES), I32)
    vals = []
    for k in range(TOP_K):
        mx = jnp.max(sc, axis=1, keepdims=True)
        idx = jnp.min(jnp.where(sc == mx, lane, LANES), axis=1, keepdims=True)
        e_out = jnp.where(lane == k, idx, e_out)
        sc = jnp.where(lane == idx, -jnp.inf, sc)
        vals.append(mx)
    ex = [jnp.exp(v - vals[0]) for v in vals]
    den = ex[0] + ex[1] + ex[2] + ex[3]
    w_out = jnp.zeros((tm, LANES), F32)
    for k in range(TOP_K):
        w_out = jnp.where(lane == k, ex[k] / den, w_out)
    e_ref[...] = e_out
    w_ref[...] = w_out


def _outproj(om, on, x, wo, g, wr, br, tm):
    N, D = x.shape
    full = lambda a: pl.BlockSpec(a.shape, lambda i: (0,) * a.ndim)
    row = lambda w: pl.BlockSpec((tm, w), lambda i: (i, 0))
    return pl.pallas_call(
        _outproj_kernel, grid=(N // tm,),
        in_specs=[row(512), row(512), row(D), full(wo), full(g), full(wr), full(br)],
        out_specs=[row(D), row(D // 2), row(LANES), row(LANES)],
        out_shape=[jax.ShapeDtypeStruct((N, D), F32), jax.ShapeDtypeStruct((N, D // 2), jnp.uint32),
                   jax.ShapeDtypeStruct((N, LANES), I32), jax.ShapeDtypeStruct((N, LANES), F32)],
        compiler_params=_params(1), name="outproj_router",
    )(om, on, x, wo, g, wr, br)


def _rank_kernel(e_ref, rank_ref, cnt_ref, base_ref):
    i = pl.program_id(0)
    T = e_ref.shape[0]

    @pl.when(i == 0)
    def _():
        base_ref[...] = jnp.zeros(base_ref.shape, F32)

    e = e_ref[...]
    lane = lax.broadcasted_iota(I32, (T, LANES), 1)
    tril = jnp.where(lax.broadcasted_iota(I32, (T, T), 0) >= lax.broadcasted_iota(I32, (T, T), 1),
                     1.0, 0.0).astype(BF16)
    out = jnp.zeros((T, LANES), I32)
    for k in range(TOP_K):
        hit = lane == e[:, k:k + 1]
        oh = jnp.where(hit, 1.0, 0.0)
        cum = jnp.dot(tril, oh.astype(BF16), preferred_element_type=F32)
        base = base_ref[0:1, :]
        r = jnp.sum(jnp.where(hit, cum - 1.0 + base, 0.0), axis=1, keepdims=True)
        out = jnp.where(lane == k, r.astype(I32), out)
        base_ref[...] = base_ref[...] + jnp.sum(oh, axis=0, keepdims=True)
    rank_ref[...] = out
    cnt_ref[...] = base_ref[...]


def _ranks(e128):
    N = e128.shape[0]
    T = RANK_TILE
    return pl.pallas_call(
        _rank_kernel, grid=(N // T,),
        in_specs=[pl.BlockSpec((T, LANES), lambda i: (i, 0))],
        out_specs=[pl.BlockSpec((T, LANES), lambda i: (i, 0)),
                   pl.BlockSpec((8, LANES), lambda i: (0, 0))],
        out_shape=[jax.ShapeDtypeStruct((N, LANES), I32),
                   jax.ShapeDtypeStruct((8, LANES), F32)],
        scratch_shapes=[pltpu.VMEM((8, LANES), F32)],
        compiler_params=_params(1), name="route_ranks",
    )(e128)


def _row_copy(src, dst, i_src, i_dst, sem):
    return pltpu.make_async_copy(src.at[pl.ds(i_src, 1)], dst.at[pl.ds(i_dst, 1)], sem)


def _dispatch_kernel(dest_hbm, hp_ref, xz_hbm, out_hbm, idx_ref, isem, sem):
    del xz_hbm
    i = pl.program_id(0)
    T = ROUTE_TILE
    cp = pltpu.make_async_copy(dest_hbm.at[i], idx_ref, isem)
    cp.start()
    cp.wait()

    def issue(t, carry):
        for k in range(TOP_K):
            _row_copy(hp_ref, out_hbm, t, idx_ref[t * TOP_K + k], sem).start()
        return carry
    lax.fori_loop(0, T, issue, 0)

    def drain(t, carry):
        for k in range(TOP_K):
            _row_copy(hp_ref, out_hbm, 0, 0, sem).wait()
        return carry
    lax.fori_loop(0, T, drain, 0)


def _dispatch(dest2, hp, xzero):
    nsteps = dest2.shape[0]
    T = ROUTE_TILE
    return pl.pallas_call(
        _dispatch_kernel, grid=(nsteps,),
        in_specs=[pl.BlockSpec(memory_space=pl.ANY),
                  pl.BlockSpec((T, hp.shape[1]), lambda i: (i, 0)),
                  pl.BlockSpec(memory_space=pl.ANY)],
        out_specs=pl.BlockSpec(memory_space=pl.ANY),
        out_shape=jax.ShapeDtypeStruct(xzero.shape, xzero.dtype),
        scratch_shapes=[pltpu.SMEM((T * TOP_K,), I32),
                        pltpu.SemaphoreType.DMA, pltpu.SemaphoreType.DMA],
        input_output_aliases={2: 0},
        compiler_params=_params(1), name="moe_dispatch",
    )(dest2, hp, xzero)


def _expert_kernel(be_ref, na_ref, x_ref, wT_ref, bg_ref, bu_ref, wd_ref, bd_ref, y_ref,
                   wg_ref, wu_ref):
    b = pl.program_id(0)
    active = b < na_ref[0]
    new_expert = jnp.logical_or(b == 0, be_ref[b] != be_ref[jnp.maximum(b - 1, 0)])

    @pl.when(jnp.logical_and(active, new_expert))
    def _():
        w32 = pltpu.bitcast(wT_ref[0], jnp.uint32)
        wg_ref[...] = pltpu.bitcast(w32 << jnp.uint32(16), F32).astype(BF16)
        wu_ref[...] = pltpu.bitcast(w32 & jnp.uint32(HI16), F32).astype(BF16)

    @pl.when(active)
    def _():
        xb = _unpack_bf16_pairs(x_ref[...])
        gg = lax.dot_general(xb, wg_ref[...], NT_DIMS, preferred_element_type=F32) + bg_ref[0]
        uu = lax.dot_general(xb, wu_ref[...], NT_DIMS, preferred_element_type=F32) + bu_ref[0]
        gg = jnp.minimum(gg, SWIGLU_LIMIT)
        uu = jnp.clip(uu, -SWIGLU_LIMIT, SWIGLU_LIMIT)
        a = gg * jax.nn.sigmoid(SWIGLU_ALPHA * gg) * (uu + 1.0)
        y_ref[...] = jnp.dot(a.astype(BF16), wd_ref[0], preferred_element_type=F32) + bd_ref[0]

    @pl.when(jnp.logical_not(active))
    def _():
        y_ref[...] = jnp.zeros(y_ref.shape, F32)


def _experts(blk_e, n_act, xrows, wT, bg, bu, wd, bd):
    P = xrows.shape[0]
    _, F2, D = wT.shape
    F = F2 // 2
    n_blk = P // MOE_ROWS
    wspec = lambda r, c: pl.BlockSpec((1, r, c), lambda b, be, na: (be[b], 0, 0))
    grid_spec = pltpu.PrefetchScalarGridSpec(
        num_scalar_prefetch=2, grid=(n_blk,),
        in_specs=[pl.BlockSpec((MOE_ROWS, D // 2), lambda b, be, na: (b, 0)),
                  wspec(F2, D), wspec(1, F), wspec(1, F), wspec(F, D), wspec(1, D)],
        out_specs=pl.BlockSpec((MOE_ROWS, D), lambda b, be, na: (b, 0)),
        scratch_shapes=[pltpu.VMEM((F, D), BF16), pltpu.VMEM((F, D), BF16)],
    )
    return pl.pallas_call(
        _expert_kernel, grid_spec=grid_spec,
        out_shape=jax.ShapeDtypeStruct((P, D), F32),
        compiler_params=_params(1), name="moe_experts",
    )(blk_e, n_act, xrows, wT, bg, bu, wd, bd)


def _combine_kernel(dest_hbm, x1_ref, w_ref, g_ref, y_hbm, o_ref, idx_ref, buf_ref, isem, sem):
    i = pl.program_id(0)
    T = ROUTE_TILE
    cp = pltpu.make_async_copy(dest_hbm.at[i], idx_ref, isem)
    cp.start()
    cp.wait()

    def issue(t, carry):
        for k in range(TOP_K):
            _row_copy(y_hbm, buf_ref.at[k], idx_ref[t * TOP_K + k], t, sem).start()
        return carry
    lax.fori_loop(0, T, issue, 0)

    def drain(t, carry):
        for k in range(TOP_K):
            _row_copy(y_hbm, buf_ref.at[k], 0, 0, sem).wait()
        return carry
    lax.fori_loop(0, T, drain, 0)

    acc = x1_ref[...]
    w = w_ref[...]
    for k in range(TOP_K):
        acc = acc + w[:, k:k + 1] * buf_ref[k]
    o_ref[...] = _rmsnorm(acc, g_ref[...])


def _combine(dest2, x1, w128, g, yrows):
    N, D = x1.shape
    T = ROUTE_TILE
    return pl.pallas_call(
        _combine_kernel, grid=(N // T,),
        in_specs=[pl.BlockSpec(memory_space=pl.ANY),
                  pl.BlockSpec((T, D), lambda i: (i, 0)),
                  pl.BlockSpec((T, LANES), lambda i: (i, 0)),
                  pl.BlockSpec(g.shape, lambda i: (0, 0)),
                  pl.BlockSpec(memory_space=pl.ANY)],
        out_specs=pl.BlockSpec((T, D), lambda i: (i, 0)),
        out_shape=jax.ShapeDtypeStruct((N, D), F32),
        scratch_shapes=[pltpu.SMEM((T * TOP_K,), I32),
                        pltpu.VMEM((TOP_K, T, D), F32),
                        pltpu.SemaphoreType.DMA, pltpu.SemaphoreType.DMA],
        compiler_params=_params(1), name="moe_combine",
    )(dest2, x1, w128, g, yrows)


def _alibi_slopes():
    n = MOBA_HEADS + NSA_HEADS
    s = jnp.exp2(-8.0 * jnp.arange(1, n + 1, dtype=F32) / n)
    return s[0::2], s[1::2]


def _prep_inproj(w_in):
    hd = HEAD_DIM
    sizes = [MOBA_HEADS * hd] * 3 + [NSA_HEADS * hd] + [NSA_KV_HEADS * hd] * 6 + [NSA_BRANCHES * NSA_HEADS]
    cuts = np.cumsum([0] + sizes)
    mq, mk, mv, nq, kc, vc, ks, vs, kw, vw, ng = [w_in[:, cuts[i]:cuts[i + 1]] for i in range(11)]
    qscale = (hd ** -0.5) * LOG2E
    wr = jnp.concatenate([mk, kc, vc, ks, kw], axis=1).astype(BF16)
    wt = jnp.concatenate([mq * qscale, mv, nq * qscale, vs, vw], axis=1).T.astype(BF16)
    ngr = ng.reshape(-1, NSA_KV_HEADS, NSA_GROUP, NSA_BRANCHES).transpose(1, 3, 2, 0)
    ngr = ngr.reshape(NSA_KV_HEADS, NSA_BRANCHES * NSA_GROUP, -1)
    wg = jnp.pad(ngr, ((0, 0), (0, 16 - NSA_BRANCHES * NSA_GROUP), (0, 0))).reshape(32, -1)
    return wr, wt, wg.astype(F32)


def _prep_compress(w1, w2, pe):
    hd, half = HEAD_DIM, NSA_CMP_STRIDE
    w1r = w1.reshape(2, half, hd, hd)
    eye = jnp.eye(NSA_KV_HEADS, dtype=w1.dtype)
    w = jnp.einsum('alde,gh->lgdahe', w1r, eye).reshape(half * NSA_KV_HEADS * hd, 2 * NSA_KV_HEADS * hd)
    w2b = jnp.einsum('de,gh->gdhe', w2, eye).reshape(NSA_KV_HEADS * hd, NSA_KV_HEADS * hd)
    per = pe.reshape(2, half, 1, hd)
    pe2 = jnp.broadcast_to(per, (2, half, NSA_KV_HEADS, hd)).reshape(2, 1, half * NSA_KV_HEADS * hd)
    pe2 = jnp.broadcast_to(pe2, (2, 8, pe2.shape[2]))
    return w.astype(BF16), w2b.astype(BF16), pe2.astype(F32)


def _attention_tables(S):
    moba_sl, nsa_sl = _alibi_slopes()
    moba_sl = moba_sl * LOG2E
    nsa_sl = nsa_sl * LOG2E
    blk = MOBA_BLOCK
    ik = jnp.arange(blk, dtype=F32)[:, None]
    iq = jnp.arange(blk, dtype=F32)[None, :]
    moba_sd = moba_sl[:, None, None] * (iq - ik)[None]
    moba_row = jnp.broadcast_to(moba_sl[:, None, None], (MOBA_HEADS, 1, blk))
    wl = NSA_GROUP * NSA_TQ
    nsa_row = jnp.repeat(nsa_sl.reshape(NSA_KV_HEADS, NSA_GROUP), NSA_TQ, axis=1)
    il = jnp.tile(jnp.arange(NSA_TQ, dtype=F32), NSA_GROUP)[None, None, :]
    ikk = jnp.arange(SLC_TILE, dtype=F32)[None, :, None]
    nsa_bs = nsa_row[:, None, :] * (il - ikk)
    nc = S // NSA_CMP_STRIDE
    n_slc = S // NSA_SLC_BLOCK
    cu, su = NSA_CMP_LEN // NSA_CMP_STRIDE, NSA_SLC_BLOCK // NSA_CMP_STRIDE
    ii = jnp.arange(nc)[None, :]
    jj = jnp.arange(n_slc)[:, None]
    ovT = jnp.clip(jnp.minimum(ii + cu, (jj + 1) * su) - jnp.maximum(ii, jj * su), 0, None).astype(F32)
    return moba_sd, moba_row, nsa_row.reshape(NSA_KV_HEADS, 1, wl), nsa_bs, ovT


def _attention(x, attn_norm_g, w_in, cmp_pe_k, cmp_pe_v, cmp_w1_k, cmp_w2_k, cmp_w1_v, cmp_w2_v):
    B, S, D = x.shape
    wr, wt, wg = _prep_inproj(w_in)
    (mk, kc, vc, ks, kw, mqT, mvT, nqT, vsT, vwT, gT) = _inproj(
        x, attn_norm_g.reshape(1, D), wr, wt, wg, tm=512)
    moba_sd, moba_row, nsa_row, nsa_bs, ovT = _attention_tables(S)
    o_moba = _moba(mqT, mk, mvT, moba_sd, moba_row)
    wk, w2k, pek = _prep_compress(cmp_w1_k, cmp_w2_k, cmp_pe_k)
    wv, w2v, pev = _prep_compress(cmp_w1_v, cmp_w2_v, cmp_pe_v)
    nc = S // NSA_CMP_STRIDE
    kcmp, vcmpT = _compress(kc.reshape(B, nc, -1), vc.reshape(B, nc, -1), wk, wv.T, pek, pev, w2k, w2v.T)
    o_nsa = _nsa(nqT, kcmp, vcmpT, ks, vsT, kw, vwT, gT, ovT, nsa_row, nsa_bs)
    return o_moba, o_nsa


def _moe(x1, hn, e128, w128, w_up, b_up, w_down, b_down, final_norm_g):
    N, D = x1.shape
    rank128, cnt = _ranks(e128)
    counts = cnt[0, :N_EXPERTS].astype(I32)
    padded = (counts + MOE_ROWS - 1) // MOE_ROWS * MOE_ROWS
    pends = jnp.cumsum(padded)
    pstarts = pends - padded
    e4 = e128[:, :TOP_K]
    dest = pstarts[e4] + rank128[:, :TOP_K]
    dest2 = dest.reshape(N // ROUTE_TILE, ROUTE_TILE * TOP_K)
    n_blk = (N * TOP_K + N_EXPERTS * MOE_ROWS + MOE_ROWS - 1) // MOE_ROWS
    P = n_blk * MOE_ROWS
    blk_start = jnp.arange(n_blk, dtype=I32) * MOE_ROWS
    blk_e = jnp.minimum(jnp.sum((pends[None, :] <= blk_start[:, None]).astype(I32), axis=1), N_EXPERTS - 1)
    n_act = (pends[-1:] // MOE_ROWS).astype(I32)
    xrows = _dispatch(dest2, hn, jnp.zeros((P, D // 2), jnp.uint32))
    wT = jnp.swapaxes(w_up, 1, 2).astype(BF16)
    bg = b_up[:, None, 0::2]
    bu = b_up[:, None, 1::2]
    yrows = _experts(blk_e, n_act, xrows, wT, bg, bu, w_down.astype(BF16), b_down[:, None, :])
    return _combine(dest2, x1, w128, final_norm_g.reshape(1, D), yrows)


def kernel(x, attn_norm_g, w_in, cmp_pe_k, cmp_pe_v, cmp_w1_k, cmp_w2_k, cmp_w1_v, cmp_w2_v, w_out, ffn_norm_g, w_router, b_router, w_up, b_up, w_down, b_down, final_norm_g):
    B, S, D = x.shape
    assert attn_norm_g.shape[0] == 1, "single-layer kernel"
    o_moba, o_nsa = _attention(x, attn_norm_g[0], w_in[0], cmp_pe_k[0], cmp_pe_v[0],
                               cmp_w1_k[0], cmp_w2_k[0], cmp_w1_v[0], cmp_w2_v[0])
    N = B * S
    wr = jnp.pad(w_router[0], ((0, 0), (0, LANES - N_EXPERTS)))
    br = jnp.pad(b_router[0], (0, LANES - N_EXPERTS)).reshape(1, LANES)
    x1, hn, e128, w128 = _outproj(o_moba.reshape(N, -1), o_nsa.reshape(N, -1), x.reshape(N, D),
                                  w_out[0].astype(BF16), ffn_norm_g[0].reshape(1, D), wr, br, tm=512)
    out = _moe(x1, hn, e128, w128, w_up[0], b_up[0], w_down[0], b_down[0], final_norm_g)
    return out.reshape(B, S, D)
```

```python
import functools

import jax
import jax.numpy as jnp
import numpy as np
from jax import lax
from jax.experimental import pallas as pl
from jax.experimental.pallas import tpu as pltpu

F32 = jnp.float32
BF16 = jnp.bfloat16
I32 = jnp.int32

HEAD_DIM = 64
MOBA_HEADS = 8
NSA_HEADS = 8
NSA_KV_HEADS = 2
NSA_GROUP = NSA_HEADS // NSA_KV_HEADS
MOBA_BLOCK = 256
MOBA_TOPK = 3
NSA_CMP_LEN = 32
NSA_CMP_STRIDE = 16
NSA_SLC_BLOCK = 64
NSA_SLC_TOPN = 16
NSA_WINDOW = 512
NSA_BRANCHES = 3
N_EXPERTS = 32
TOP_K = 4
SWIGLU_LIMIT = 7.0
SWIGLU_ALPHA = 1.702
RMS_EPS = 1e-5
NEG_BIG = -1e30
LOG2E = 1.4426950408889634

LANES = 128
VMEM_LIMIT = 56 * 1024 * 1024

NSA_TQ = 128
SLC_TILE = 256
WIN_KEYS = NSA_WINDOW + NSA_TQ
MOE_ROWS = 512
ROUTE_TILE = 256
RANK_TILE = 512

NT_DIMS = (((1,), (1,)), ((), ()))


def _params(n_grid):
    return pltpu.CompilerParams(
        dimension_semantics=("arbitrary",) * n_grid,
        vmem_limit_bytes=VMEM_LIMIT,
    )


def _rmsnorm(x, g):
    return x * lax.rsqrt(jnp.mean(x * x, axis=-1, keepdims=True) + RMS_EPS) * g


def _inproj_kernel(x_ref, g_ref, wr_ref, wt_ref, wg_ref,
                   mk_ref, kc_ref, vc_ref, ks_ref, kw_ref,
                   mqT_ref, mvT_ref, nqT_ref, vsT_ref, vwT_ref, gT_ref):
    xn = _rmsnorm(x_ref[0], g_ref[...])
    xb = xn.astype(BF16)
    yr = jnp.dot(xb, wr_ref[...], preferred_element_type=F32)
    mk_ref[0] = yr[:, 0:512].astype(BF16)
    kc_ref[0] = yr[:, 512:640].astype(BF16)
    vc_ref[0] = yr[:, 640:768].astype(BF16)
    ks_ref[0] = yr[:, 768:896].astype(BF16)
    kw_ref[0] = yr[:, 896:1024].astype(BF16)
    yt = lax.dot_general(wt_ref[...], xb, NT_DIMS, preferred_element_type=F32)
    mqT_ref[0] = yt[0:512].astype(BF16)
    mvT_ref[0] = yt[512:1024].astype(BF16)
    nqT_ref[0] = yt[1024:1536].astype(BF16)
    vsT_ref[0] = yt[1536:1664].astype(BF16)
    vwT_ref[0] = yt[1664:1792].astype(BF16)
    gl = lax.dot_general(wg_ref[...], xn, NT_DIMS, precision=lax.Precision.HIGHEST,
                         preferred_element_type=F32)
    gT_ref[0] = jax.nn.sigmoid(gl)


def _inproj(x, g, wr, wt, wg, tm):
    B, S, D = x.shape
    grid = (B, S // tm)
    row = lambda w: pl.BlockSpec((1, tm, w), lambda b, i: (b, i, 0))
    col = lambda h: pl.BlockSpec((1, h, tm), lambda b, i: (b, 0, i))
    full = lambda a: pl.BlockSpec(a.shape, lambda b, i: (0,) * a.ndim)
    out_shape = [
        jax.ShapeDtypeStruct((B, S, 512), BF16),
        jax.ShapeDtypeStruct((B, S, 128), BF16),
        jax.ShapeDtypeStruct((B, S, 128), BF16),
        jax.ShapeDtypeStruct((B, S, 128), BF16),
        jax.ShapeDtypeStruct((B, S, 128), BF16),
        jax.ShapeDtypeStruct((B, 512, S), BF16),
        jax.ShapeDtypeStruct((B, 512, S), BF16),
        jax.ShapeDtypeStruct((B, 512, S), BF16),
        jax.ShapeDtypeStruct((B, 128, S), BF16),
        jax.ShapeDtypeStruct((B, 128, S), BF16),
        jax.ShapeDtypeStruct((B, 32, S), F32),
    ]
    out_specs = [row(512), row(128), row(128), row(128), row(128),
                 col(512), col(512), col(512), col(128), col(128), col(32)]
    return pl.pallas_call(
        _inproj_kernel, grid=grid,
        in_specs=[pl.BlockSpec((1, tm, D), lambda b, i: (b, i, 0)),
                  full(g), full(wr), full(wt), full(wg)],
        out_specs=out_specs, out_shape=out_shape,
        compiler_params=_params(2), name="inproj",
    )(x, g, wr, wt, wg)


def _flash_step(zs, cs, vts, states):
    mts = [jnp.max(z, axis=0, keepdims=True) - c for z, c in zip(zs, cs)]
    m_news = [jnp.maximum(st[0], mt) for st, mt in zip(states, mts)]
    alphas = [jnp.exp2(st[0] - mn) for st, mn in zip(states, m_news)]
    ps = [jnp.exp2(z - (mn + c)) for z, mn, c in zip(zs, m_news, cs)]
    ls = [a * st[1] + jnp.sum(p, axis=0, keepdims=True) for a, st, p in zip(alphas, states, ps)]
    pvs = [jnp.dot(vt, p.astype(BF16), preferred_element_type=F32) for vt, p in zip(vts, ps)]
    accs = [a * st[2] + pv for a, st, pv in zip(alphas, states, pvs)]
    return [(mn, l, acc) for mn, l, acc in zip(m_news, ls, accs)]


def _flash_init(n_q):
    return (jnp.full((1, n_q), NEG_BIG, F32), jnp.zeros((1, n_q), F32),
            jnp.zeros((HEAD_DIM, n_q), F32))


def _softmax_stage(s, c, m, l):
    mt = jnp.max(s, axis=0, keepdims=True) - c
    m_new = jnp.maximum(m, mt)
    alpha = jnp.exp2(m - m_new)
    p = jnp.exp2(s - (m_new + c))
    l_new = alpha * l + jnp.sum(p, axis=0, keepdims=True)
    return m_new, l_new, p.astype(BF16), alpha


def _pipelined_tiles(scores, values, offsets, s_ref, p_ref, first, n_tiles, j_first):
    n_ch = len(first)
    chains = range(n_ch)

    def qk_into(slot, j):
        sc = scores(j)
        for c in chains:
            s_ref[slot, c] = sc[c]

    def pv_from(slot, j, alphas, accs):
        vt = values(j)
        return [alphas[c] * accs[c] + jnp.dot(vt[c], p_ref[slot, c], preferred_element_type=F32)
                for c in chains]

    def softmax_into(slot, j, ms, ls):
        cs = offsets(j)
        new = [_softmax_stage(s_ref[slot, c], cs[c], ms[c], ls[c]) for c in chains]
        for c in chains:
            p_ref[slot, c] = new[c][2]
        return [n[0] for n in new], [n[1] for n in new], [n[3] for n in new]

    for c in chains:
        p_ref[1, c] = first[c][2]
    qk_into(0, 0)

    def pair(i, carry):
        ms, ls, alphas, accs, j_prev = carry
        t = 2 * i
        qk_into(1, t + 1)
        accs = pv_from(1, j_prev, alphas, accs)
        ms, ls, alphas = softmax_into(0, t, ms, ls)
        qk_into(0, t + 2)
        accs = pv_from(0, t, alphas, accs)
        ms, ls, alphas = softmax_into(1, t + 1, ms, ls)
        return ms, ls, alphas, accs, t + 1

    n_q = first[0][0].shape[1]
    init = ([f[0] for f in first], [f[1] for f in first], [f[3] for f in first],
            [jnp.zeros((HEAD_DIM, n_q), F32)] * n_ch, j_first)
    _, ls, alphas, accs, j_last = lax.fori_loop(0, (n_tiles + 1) // 2, pair, init)
    accs = pv_from(1, j_last, alphas, accs)
    return [accs[c] / jnp.maximum(ls[c], 1e-30) for c in chains]


def _moba_kernel(qT_ref, k_ref, vT_ref, aug_ref, srow_ref, sl_ref, o_ref,
                 kmean_ref, s_ref, p_ref, *, nb, nbp, topk):
    qi = pl.program_id(2)
    blk = MOBA_BLOCK

    @pl.when(qi == 0)
    def _():
        kmean_ref[...] = jnp.zeros(kmean_ref.shape, F32)

        def body(n, carry):
            kb = k_ref[0, pl.ds(pl.multiple_of(n * blk, blk), blk), :].astype(F32)
            kmean_ref[pl.ds(n, 1), :] = jnp.mean(kb, axis=0, keepdims=True)
            return carry
        lax.fori_loop(0, nb, body, 0)

    qT = qT_ref[0]
    row = lax.broadcasted_iota(I32, qT.shape, 0)
    qpad = [jnp.where((row >> 6) == h, qT, jnp.zeros_like(qT)) for h in range(2)]

    bidx = lax.broadcasted_iota(I32, (nbp, blk), 0)
    rhs = []
    for h in range(2):
        gate = jnp.dot(kmean_ref[...], qpad[h].astype(F32),
                       precision=lax.Precision.HIGHEST, preferred_element_type=F32)
        gsc = jnp.where(bidx < qi, gate, -jnp.inf)
        bias = jnp.full((nbp, blk), NEG_BIG, F32)
        for _ in range(topk):
            mx = jnp.max(gsc, axis=0, keepdims=True)
            idx = jnp.min(jnp.where(gsc == mx, bidx, nbp), axis=0, keepdims=True)
            pick = jnp.logical_and(bidx == idx, mx > -jnp.inf)
            bias = jnp.where(pick, 0.0, bias)
            gsc = jnp.where(pick, -jnp.inf, gsc)
        pad = jnp.zeros((2 * HEAD_DIM - nbp - 16, blk), BF16)
        rhs.append(jnp.concatenate([qpad[h], bias.astype(BF16), srow_ref[h], pad], axis=0))

    def scores(j, a):
        k0 = pl.multiple_of(j * blk, blk)
        lhs = jnp.concatenate([k_ref[0, pl.ds(k0, blk), :], aug_ref[a]], axis=1)
        return [jnp.dot(lhs, rhs[h], preferred_element_type=F32) for h in range(2)]

    def values(j):
        k0 = pl.multiple_of(j * blk, blk)
        return [vT_ref[0, h * HEAD_DIM:(h + 1) * HEAD_DIM, pl.ds(k0, blk)] for h in range(2)]

    def offsets(j):
        dq = ((qi - j) * blk).astype(F32)
        return [sl_ref[h] * dq for h in range(2)]

    ik = lax.broadcasted_iota(I32, (blk, blk), 0)
    iq = lax.broadcasted_iota(I32, (blk, blk), 1)
    s_own = [jnp.where(ik <= iq, s, NEG_BIG) for s in scores(qi, nb)]
    m0 = jnp.full((1, blk), NEG_BIG, F32)
    l0 = jnp.zeros((1, blk), F32)
    first = [_softmax_stage(s_own[h], jnp.zeros((1, blk), F32), m0, l0) for h in range(2)]

    outs = _pipelined_tiles(lambda j: scores(jnp.minimum(j, nb - 1), jnp.minimum(j, nb - 1)),
                            values, offsets, s_ref, p_ref, first, qi, qi)
    o_ref[0] = jnp.concatenate(outs, axis=0).T.astype(BF16)


def _moba(mqT, mk, mvT, aug, srow, sl):
    B, _, S = mqT.shape
    blk = MOBA_BLOCK
    nb = S // blk
    topk = min(MOBA_TOPK, nb)
    nbp = -(-nb // 16) * 16
    grid = (B, MOBA_HEADS // 2, nb)
    return pl.pallas_call(
        functools.partial(_moba_kernel, nb=nb, nbp=nbp, topk=topk), grid=grid,
        in_specs=[
            pl.BlockSpec((1, 128, blk), lambda b, p, i: (b, p, i)),
            pl.BlockSpec((1, S, 128), lambda b, p, i: (b, 0, p)),
            pl.BlockSpec((1, 128, S), lambda b, p, i: (b, p, 0)),
            pl.BlockSpec(aug.shape, lambda b, p, i: (0, 0, 0)),
            pl.BlockSpec((2, 16, blk), lambda b, p, i: (p, 0, 0)),
            pl.BlockSpec((2, 1, blk), lambda b, p, i: (p, 0, 0)),
        ],
        out_specs=pl.BlockSpec((1, blk, 128), lambda b, p, i: (b, i, p)),
        out_shape=jax.ShapeDtypeStruct((B, S, 512), BF16),
        scratch_shapes=[
            pltpu.VMEM((nbp, 128), F32),
            pltpu.VMEM((2, 2, blk, blk), F32),
            pltpu.VMEM((2, 2, blk, blk), BF16),
        ],
        compiler_params=_params(3), name="moba",
    )(mqT, mk, mvT, aug, srow, sl)


def _compress_kernel(kc_ref, vc_ref, wk_ref, wvT_ref, pek_ref, pev_ref, w2k_ref, w2vT_ref,
                     kcmp_ref, vcmpT_ref):
    nc = kc_ref.shape[1]

    wk = wk_ref[...]
    ab = jnp.dot(kc_ref[0], wk, preferred_element_type=F32)
    pt = (jnp.dot(pek_ref[0], wk[:, 0:128].astype(F32), preferred_element_type=F32)
          + jnp.dot(pek_ref[1], wk[:, 128:256].astype(F32), preferred_element_type=F32))
    pre = ab[:, 0:128] + pltpu.roll(ab[:, 128:256], nc - 1, 0) + pt[0:1]
    hid = jax.nn.gelu(pre)
    kcmp_ref[0] = jnp.dot(hid.astype(BF16), w2k_ref[...], preferred_element_type=F32).astype(BF16)

    wvT = wvT_ref[...]
    abT = lax.dot_general(wvT, vc_ref[0], NT_DIMS, preferred_element_type=F32)
    ptT = (lax.dot_general(wvT[0:128].astype(F32), pev_ref[0], NT_DIMS, preferred_element_type=F32)
           + lax.dot_general(wvT[128:256].astype(F32), pev_ref[1], NT_DIMS, preferred_element_type=F32))
    preT = abT[0:128] + pltpu.roll(abT[128:256], nc - 1, 1) + ptT[:, 0:1]
    hidT = jax.nn.gelu(preT)
    vcmpT_ref[0] = jnp.dot(w2vT_ref[...], hidT.astype(BF16), preferred_element_type=F32).astype(BF16)


def _compress(kc2, vc2, wk, wvT, pek, pev, w2k, w2vT):
    B, nc, _ = kc2.shape
    full = lambda a: pl.BlockSpec(a.shape, lambda b: (0,) * a.ndim)
    blk = pl.BlockSpec((1, nc, kc2.shape[2]), lambda b: (b, 0, 0))
    return pl.pallas_call(
        _compress_kernel, grid=(B,),
        in_specs=[blk, blk, full(wk), full(wvT), full(pek), full(pev), full(w2k), full(w2vT)],
        out_specs=[pl.BlockSpec((1, nc, 128), lambda b: (b, 0, 0)),
                   pl.BlockSpec((1, 128, nc), lambda b: (b, 0, 0))],
        out_shape=[jax.ShapeDtypeStruct((B, nc, 128), BF16),
                   jax.ShapeDtypeStruct((B, 128, nc), BF16)],
        compiler_params=_params(1), name="nsa_compress",
    )(kc2, vc2, wk, wvT, pek, pev, w2k, w2vT)


def _nsa_kernel(qT_ref, kcmp_ref, vcmpT_ref, ks_ref, vsT_ref, kw_ref, vwT_ref,
                g_ref, ovT_ref, sl_ref, srow_ref, auga_ref, augb_ref, o_ref,
                s_ref, p_ref, *, n_slc, topn):
    g = pl.program_id(1)
    qi = pl.program_id(2)
    tq = NSA_TQ
    hg = NSA_GROUP
    wl = hg * tq
    q0 = qi * tq

    q4 = qT_ref[0]
    qT = jnp.concatenate([q4[h * HEAD_DIM:(h + 1) * HEAD_DIM] for h in range(hg)], axis=1)
    qT2 = jnp.concatenate([qT, qT], axis=0)
    rowi = lax.broadcasted_iota(I32, qT2.shape, 0)
    qpad = jnp.where((rowi >> 6) == g, qT2, jnp.zeros_like(qT2))
    slope = sl_ref[0]
    lane = lax.broadcasted_iota(I32, (1, wl), 1)
    t_q = q0 + (lane & (tq - 1))

    nc = kcmp_ref.shape[1]
    s = jnp.dot(kcmp_ref[0], qpad, preferred_element_type=F32)
    ci = lax.broadcasted_iota(I32, (nc, wl), 0)
    dist = t_q - (ci * NSA_CMP_STRIDE + (NSA_CMP_LEN - 1))
    mask = dist >= 0
    z = jnp.where(mask, s - slope * dist.astype(F32), NEG_BIG)
    mx = jnp.max(z, axis=0, keepdims=True)
    p = jnp.where(mask, jnp.exp2(z - mx), 0.0)
    p = p / jnp.maximum(jnp.sum(p, axis=0, keepdims=True), 1e-30)
    o_c = jnp.dot(vcmpT_ref[0], p.astype(BF16), preferred_element_type=F32)

    pc = p[:, 0:tq]
    for h in range(1, hg):
        pc = pc + p[:, h * tq:(h + 1) * tq]
    imp = jnp.dot(ovT_ref[...], pc, precision=lax.Precision.HIGHEST,
                  preferred_element_type=F32)
    cur = (q0 + lax.broadcasted_iota(I32, (1, tq), 1)) >> 6
    jb = lax.broadcasted_iota(I32, (n_slc, tq), 0)
    allowed = jb <= cur
    forced = jnp.logical_or(jb == 0, jnp.logical_or(jb == cur, jb == cur - 1))
    bias = jnp.where(jnp.logical_and(allowed, forced), 0.0, NEG_BIG)
    sc = jnp.where(jnp.logical_and(allowed, jnp.logical_not(forced)), imp, -1.0)
    for _ in range(topn - 3):
        smx = jnp.max(sc, axis=0, keepdims=True)
        idx = jnp.min(jnp.where(sc == smx, jb, n_slc), axis=0, keepdims=True)
        pick = jnp.logical_and(jb == idx, smx >= 0.0)
        bias = jnp.where(pick, 0.0, bias)
        sc = jnp.where(pick, -1.0, sc)

    n_ch = 2
    wc = wl // n_ch

    def lane_split(a):
        return [a[:, c * wc:(c + 1) * wc] for c in range(n_ch)]

    if n_slc < LANES:
        bias = jnp.concatenate([bias, jnp.zeros((LANES - n_slc, tq), F32)], axis=0)
    bias4 = jnp.concatenate([bias.astype(BF16)] * hg, axis=1)
    rhs = jnp.concatenate([jnp.where((rowi >> 6) == g, qT2, srow_ref[0]), bias4], axis=0)
    mine = (lax.broadcasted_iota(I32, (SLC_TILE, LANES), 1) >> 6) == g
    n_tiles = augb_ref.shape[0]

    def scores(j, null):
        k0 = pl.multiple_of(j * SLC_TILE, SLC_TILE)
        kt = ks_ref[0, pl.ds(k0, SLC_TILE), :]
        lhs = jnp.concatenate([jnp.where(mine, kt, auga_ref[null]), augb_ref[j]], axis=1)
        return jnp.dot(lhs, rhs, preferred_element_type=F32)

    def values(j):
        return [vsT_ref[0, :, pl.ds(pl.multiple_of(j * SLC_TILE, SLC_TILE), SLC_TILE)]] * n_ch

    def offsets(j):
        return lane_split(slope * (q0 - j * SLC_TILE).astype(F32))

    jd = lax.div(q0, SLC_TILE)
    t_k = jd * SLC_TILE + lax.broadcasted_iota(I32, (SLC_TILE, wl), 0)
    s_diag = lane_split(jnp.where(t_k <= t_q, scores(jd, 0), NEG_BIG))
    c_diag = offsets(jd)
    first = [_softmax_stage(s_diag[c], c_diag[c], jnp.full((1, wc), NEG_BIG, F32),
                            jnp.zeros((1, wc), F32)) for c in range(n_ch)]

    def past_scores(j):
        return lane_split(scores(jnp.minimum(j, n_tiles - 1), (j >= jd).astype(I32)))
    o_s = jnp.concatenate(
        _pipelined_tiles(past_scores, values, offsets, s_ref, p_ref, first, jd, jd), axis=1)

    start = pl.multiple_of(jnp.maximum(q0 - NSA_WINDOW, 0), tq)
    kt = kw_ref[0, pl.ds(start, WIN_KEYS), :]
    s = jnp.dot(kt, qpad, preferred_element_type=F32)
    d = t_q - (start + lax.broadcasted_iota(I32, (WIN_KEYS, wl), 0))
    mask = jnp.logical_and(d >= 0, d < NSA_WINDOW)
    z = jnp.where(mask, s - slope * d.astype(F32), NEG_BIG)
    mx = jnp.max(z, axis=0, keepdims=True)
    p = jnp.exp2(z - mx)
    den = jnp.maximum(jnp.sum(p, axis=0, keepdims=True), 1e-30)
    o_w = jnp.dot(vwT_ref[0, :, pl.ds(start, WIN_KEYS)], p.astype(BF16),
                  preferred_element_type=F32) / den

    gt = g_ref[0]

    def gate_row(br):
        return jnp.concatenate([gt[br * hg + h:br * hg + h + 1] for h in range(hg)], axis=1)

    o = gate_row(0) * o_c + gate_row(1) * o_s + gate_row(2) * o_w
    o4 = jnp.concatenate([o[:, h * tq:(h + 1) * tq] for h in range(hg)], axis=0)
    o_ref[0] = o4.T.astype(BF16)


def _nsa(nqT, kcmp, vcmpT, ks, vsT, kw, vwT, gT, ovT, sl, srow, auga, augb):
    B, _, S = nqT.shape
    tq = NSA_TQ
    nc = kcmp.shape[1]
    n_slc = S // NSA_SLC_BLOCK
    topn = min(NSA_SLC_TOPN, n_slc)
    wl = NSA_GROUP * tq
    grid = (B, NSA_KV_HEADS, S // tq)
    return pl.pallas_call(
        functools.partial(_nsa_kernel, n_slc=n_slc, topn=topn), grid=grid,
        in_specs=[
            pl.BlockSpec((1, NSA_GROUP * HEAD_DIM, tq), lambda b, g, i: (b, g, i)),
            pl.BlockSpec((1, nc, 128), lambda b, g, i: (b, 0, 0)),
            pl.BlockSpec((1, HEAD_DIM, nc), lambda b, g, i: (b, g, 0)),
            pl.BlockSpec((1, S, 128), lambda b, g, i: (b, 0, 0)),
            pl.BlockSpec((1, HEAD_DIM, S), lambda b, g, i: (b, g, 0)),
            pl.BlockSpec((1, S, 128), lambda b, g, i: (b, 0, 0)),
            pl.BlockSpec((1, HEAD_DIM, S), lambda b, g, i: (b, g, 0)),
            pl.BlockSpec((1, 16, tq), lambda b, g, i: (b, g, i)),
            pl.BlockSpec((n_slc, nc), lambda b, g, i: (0, 0)),
            pl.BlockSpec((1, 1, wl), lambda b, g, i: (g, 0, 0)),
            pl.BlockSpec((1, 2 * HEAD_DIM, wl), lambda b, g, i: (g, 0, 0)),
            pl.BlockSpec((2, SLC_TILE, LANES), lambda b, g, i: (g, 0, 0)),
            pl.BlockSpec(augb.shape, lambda b, g, i: (0, 0, 0)),
        ],
        out_specs=pl.BlockSpec((1, tq, NSA_GROUP * HEAD_DIM), lambda b, g, i: (b, i, g)),
        out_shape=jax.ShapeDtypeStruct((B, S, 512), BF16),
        scratch_shapes=[
            pltpu.VMEM((2, 2, SLC_TILE, wl // 2), F32),
            pltpu.VMEM((2, 2, SLC_TILE, wl // 2), BF16),
        ],
        compiler_params=_params(3), name="nsa",
    )(nqT, kcmp, vcmpT, ks, vsT, kw, vwT, gT, ovT, sl, srow, auga, augb)


HI16 = 0xFFFF0000


def _pack_bf16_pairs(x):
    c = x.shape[1] // 2
    bits = pltpu.bitcast(x.astype(BF16).astype(F32), jnp.uint32)
    return (bits[:, c:] & jnp.uint32(HI16)) | (bits[:, :c] >> jnp.uint32(16))


def _unpack_bf16_pairs(p):
    lo = pltpu.bitcast(p << jnp.uint32(16), F32)
    hi = pltpu.bitcast(p & jnp.uint32(HI16), F32)
    return jnp.concatenate([lo, hi], axis=1).astype(BF16)


def _outproj_kernel(om_ref, on_ref, x_ref, wo_ref, g_ref, wr_ref, br_ref,
                    x1_ref, hn_ref, e_ref, w_ref):
    attn = (jnp.dot(om_ref[...], wo_ref[0:512, :], preferred_element_type=F32)
            + jnp.dot(on_ref[...], wo_ref[512:1024, :], preferred_element_type=F32))
    x1 = x_ref[...] + attn
    x1_ref[...] = x1
    hn = _rmsnorm(x1, g_ref[...])
    hn_ref[...] = _pack_bf16_pairs(hn)
    logits = jnp.dot(hn, wr_ref[...], precision=lax.Precision.HIGHEST,
                     preferred_element_type=F32) + br_ref[...]
    tm = logits.shape[0]
    lane = lax.broadcasted_iota(I32, (tm, LANES), 1)
    sc = jnp.where(lane < N_EXPERTS, logits, -jnp.inf)
    e_out = jnp.zeros((tm, LANES), I32)
    vals = []
    for k in range(TOP_K):
        mx = jnp.max(sc, axis=1, keepdims=True)
        idx = jnp.min(jnp.where(sc == mx, lane, LANES), axis=1, keepdims=True)
        e_out = jnp.where(lane == k, idx, e_out)
        sc = jnp.where(lane == idx, -jnp.inf, sc)
        vals.append(mx)
    ex = [jnp.exp(v - vals[0]) for v in vals]
    den = ex[0] + ex[1] + ex[2] + ex[3]
    w_out = jnp.zeros((tm, LANES), F32)
    for k in range(TOP_K):
        w_out = jnp.where(lane == k, ex[k] / den, w_out)
    e_ref[...] = e_out
    w_ref[...] = w_out


def _outproj(om, on, x, wo, g, wr, br, tm):
    N, D = x.shape
    full = lambda a: pl.BlockSpec(a.shape, lambda i: (0,) * a.ndim)
    row = lambda w: pl.BlockSpec((tm, w), lambda i: (i, 0))
    return pl.pallas_call(
        _outproj_kernel, grid=(N // tm,),
        in_specs=[row(512), row(512), row(D), full(wo), full(g), full(wr), full(br)],
        out_specs=[row(D), row(D // 2), row(LANES), row(LANES)],
        out_shape=[jax.ShapeDtypeStruct((N, D), F32), jax.ShapeDtypeStruct((N, D // 2), jnp.uint32),
                   jax.ShapeDtypeStruct((N, LANES), I32), jax.ShapeDtypeStruct((N, LANES), F32)],
        compiler_params=_params(1), name="outproj_router",
    )(om, on, x, wo, g, wr, br)


def _rank_kernel(e_ref, rank_ref, cnt_ref, base_ref):
    i = pl.program_id(0)
    T = e_ref.shape[0]

    @pl.when(i == 0)
    def _():
        base_ref[...] = jnp.zeros(base_ref.shape, F32)

    e = e_ref[...]
    lane = lax.broadcasted_iota(I32, (T, LANES), 1)
    tril = jnp.where(lax.broadcasted_iota(I32, (T, T), 0) >= lax.broadcasted_iota(I32, (T, T), 1),
                     1.0, 0.0).astype(BF16)
    out = jnp.zeros((T, LANES), I32)
    for k in range(TOP_K):
        hit = lane == e[:, k:k + 1]
        oh = jnp.where(hit, 1.0, 0.0)
        cum = jnp.dot(tril, oh.astype(BF16), preferred_element_type=F32)
        base = base_ref[0:1, :]
        r = jnp.sum(jnp.where(hit, cum - 1.0 + base, 0.0), axis=1, keepdims=True)
        out = jnp.where(lane == k, r.astype(I32), out)
        base_ref[...] = base_ref[...] + jnp.sum(oh, axis=0, keepdims=True)
    rank_ref[...] = out
    cnt_ref[...] = base_ref[...]


def _ranks(e128):
    N = e128.shape[0]
    T = RANK_TILE
    return pl.pallas_call(
        _rank_kernel, grid=(N // T,),
        in_specs=[pl.BlockSpec((T, LANES), lambda i: (i, 0))],
        out_specs=[pl.BlockSpec((T, LANES), lambda i: (i, 0)),
                   pl.BlockSpec((8, LANES), lambda i: (0, 0))],
        out_shape=[jax.ShapeDtypeStruct((N, LANES), I32),
                   jax.ShapeDtypeStruct((8, LANES), F32)],
        scratch_shapes=[pltpu.VMEM((8, LANES), F32)],
        compiler_params=_params(1), name="route_ranks",
    )(e128)


def _row_copy(src, dst, i_src, i_dst, sem):
    return pltpu.make_async_copy(src.at[pl.ds(i_src, 1)], dst.at[pl.ds(i_dst, 1)], sem)


def _dispatch_kernel(dest_hbm, hp_ref, xz_hbm, out_hbm, idx_ref, isem, sem):
    del xz_hbm
    i = pl.program_id(0)
    T = ROUTE_TILE
    cp = pltpu.make_async_copy(dest_hbm.at[i], idx_ref, isem)
    cp.start()
    cp.wait()

    def issue(t, carry):
        for k in range(TOP_K):
            _row_copy(hp_ref, out_hbm, t, idx_ref[t * TOP_K + k], sem).start()
        return carry
    lax.fori_loop(0, T, issue, 0)

    def drain(t, carry):
        for k in range(TOP_K):
            _row_copy(hp_ref, out_hbm, 0, 0, sem).wait()
        return carry
    lax.fori_loop(0, T, drain, 0)


def _dispatch(dest2, hp, xzero):
    nsteps = dest2.shape[0]
    T = ROUTE_TILE
    return pl.pallas_call(
        _dispatch_kernel, grid=(nsteps,),
        in_specs=[pl.BlockSpec(memory_space=pl.ANY),
                  pl.BlockSpec((T, hp.shape[1]), lambda i: (i, 0)),
                  pl.BlockSpec(memory_space=pl.ANY)],
        out_specs=pl.BlockSpec(memory_space=pl.ANY),
        out_shape=jax.ShapeDtypeStruct(xzero.shape, xzero.dtype),
        scratch_shapes=[pltpu.SMEM((T * TOP_K,), I32),
                        pltpu.SemaphoreType.DMA, pltpu.SemaphoreType.DMA],
        input_output_aliases={2: 0},
        compiler_params=_params(1), name="moe_dispatch",
    )(dest2, hp, xzero)


def _expert_kernel(be_ref, na_ref, x_ref, wT_ref, bg_ref, bu_ref, wd_ref, bd_ref, y_ref,
                   wg_ref, wu_ref):
    b = pl.program_id(0)
    active = b < na_ref[0]
    new_expert = jnp.logical_or(b == 0, be_ref[b] != be_ref[jnp.maximum(b - 1, 0)])

    @pl.when(jnp.logical_and(active, new_expert))
    def _():
        w32 = pltpu.bitcast(wT_ref[0], jnp.uint32)
        wg_ref[...] = pltpu.bitcast(w32 << jnp.uint32(16), F32).astype(BF16)
        wu_ref[...] = pltpu.bitcast(w32 & jnp.uint32(HI16), F32).astype(BF16)

    @pl.when(active)
    def _():
        xb = _unpack_bf16_pairs(x_ref[...])
        gg = lax.dot_general(xb, wg_ref[...], NT_DIMS, preferred_element_type=F32) + bg_ref[0]
        uu = lax.dot_general(xb, wu_ref[...], NT_DIMS, preferred_element_type=F32) + bu_ref[0]
        gg = jnp.minimum(gg, SWIGLU_LIMIT)
        uu = jnp.clip(uu, -SWIGLU_LIMIT, SWIGLU_LIMIT)
        a = gg * jax.nn.sigmoid(SWIGLU_ALPHA * gg) * (uu + 1.0)
        y_ref[...] = jnp.dot(a.astype(BF16), wd_ref[0], preferred_element_type=F32) + bd_ref[0]

    @pl.when(jnp.logical_not(active))
    def _():
        y_ref[...] = jnp.zeros(y_ref.shape, F32)


def _experts(blk_e, n_act, xrows, wT, bg, bu, wd, bd):
    P = xrows.shape[0]
    _, F2, D = wT.shape
    F = F2 // 2
    n_blk = P // MOE_ROWS
    wspec = lambda r, c: pl.BlockSpec((1, r, c), lambda b, be, na: (be[b], 0, 0))
    grid_spec = pltpu.PrefetchScalarGridSpec(
        num_scalar_prefetch=2, grid=(n_blk,),
        in_specs=[pl.BlockSpec((MOE_ROWS, D // 2), lambda b, be, na: (b, 0)),
                  wspec(F2, D), wspec(1, F), wspec(1, F), wspec(F, D), wspec(1, D)],
        out_specs=pl.BlockSpec((MOE_ROWS, D), lambda b, be, na: (b, 0)),
        scratch_shapes=[pltpu.VMEM((F, D), BF16), pltpu.VMEM((F, D), BF16)],
    )
    return pl.pallas_call(
        _expert_kernel, grid_spec=grid_spec,
        out_shape=jax.ShapeDtypeStruct((P, D), F32),
        compiler_params=_params(1), name="moe_experts",
    )(blk_e, n_act, xrows, wT, bg, bu, wd, bd)


def _combine_kernel(dest_hbm, x1_ref, w_ref, g_ref, y_hbm, o_ref, idx_ref, buf_ref, isem, sem):
    i = pl.program_id(0)
    T = ROUTE_TILE
    cp = pltpu.make_async_copy(dest_hbm.at[i], idx_ref, isem)
    cp.start()
    cp.wait()

    def issue(t, carry):
        for k in range(TOP_K):
            _row_copy(y_hbm, buf_ref.at[k], idx_ref[t * TOP_K + k], t, sem).start()
        return carry
    lax.fori_loop(0, T, issue, 0)

    def drain(t, carry):
        for k in range(TOP_K):
            _row_copy(y_hbm, buf_ref.at[k], 0, 0, sem).wait()
        return carry
    lax.fori_loop(0, T, drain, 0)

    acc = x1_ref[...]
    w = w_ref[...]
    for k in range(TOP_K):
        acc = acc + w[:, k:k + 1] * buf_ref[k]
    o_ref[...] = _rmsnorm(acc, g_ref[...])


def _combine(dest2, x1, w128, g, yrows):
    N, D = x1.shape
    T = ROUTE_TILE
    return pl.pallas_call(
        _combine_kernel, grid=(N // T,),
        in_specs=[pl.BlockSpec(memory_space=pl.ANY),
                  pl.BlockSpec((T, D), lambda i: (i, 0)),
                  pl.BlockSpec((T, LANES), lambda i: (i, 0)),
                  pl.BlockSpec(g.shape, lambda i: (0, 0)),
                  pl.BlockSpec(memory_space=pl.ANY)],
        out_specs=pl.BlockSpec((T, D), lambda i: (i, 0)),
        out_shape=jax.ShapeDtypeStruct((N, D), F32),
        scratch_shapes=[pltpu.SMEM((T * TOP_K,), I32),
                        pltpu.VMEM((TOP_K, T, D), F32),
                        pltpu.SemaphoreType.DMA, pltpu.SemaphoreType.DMA],
        compiler_params=_params(1), name="moe_combine",
    )(dest2, x1, w128, g, yrows)


def _alibi_slopes():
    n = MOBA_HEADS + NSA_HEADS
    s = jnp.exp2(-8.0 * jnp.arange(1, n + 1, dtype=F32) / n)
    return s[0::2], s[1::2]


def _prep_inproj(w_in):
    hd = HEAD_DIM
    sizes = [MOBA_HEADS * hd] * 3 + [NSA_HEADS * hd] + [NSA_KV_HEADS * hd] * 6 + [NSA_BRANCHES * NSA_HEADS]
    cuts = np.cumsum([0] + sizes)
    mq, mk, mv, nq, kc, vc, ks, vs, kw, vw, ng = [w_in[:, cuts[i]:cuts[i + 1]] for i in range(11)]
    qscale = (hd ** -0.5) * LOG2E
    wr = jnp.concatenate([mk, kc, vc, ks, kw], axis=1).astype(BF16)
    wt = jnp.concatenate([mq * qscale, mv, nq * qscale, vs, vw], axis=1).T.astype(BF16)
    ngr = ng.reshape(-1, NSA_KV_HEADS, NSA_GROUP, NSA_BRANCHES).transpose(1, 3, 2, 0)
    ngr = ngr.reshape(NSA_KV_HEADS, NSA_BRANCHES * NSA_GROUP, -1)
    wg = jnp.pad(ngr, ((0, 0), (0, 16 - NSA_BRANCHES * NSA_GROUP), (0, 0))).reshape(32, -1)
    return wr, wt, wg.astype(F32)


def _prep_compress(w1, w2, pe):
    hd, half = HEAD_DIM, NSA_CMP_STRIDE
    w1r = w1.reshape(2, half, hd, hd)
    eye = jnp.eye(NSA_KV_HEADS, dtype=w1.dtype)
    w = jnp.einsum('alde,gh->lgdahe', w1r, eye).reshape(half * NSA_KV_HEADS * hd, 2 * NSA_KV_HEADS * hd)
    w2b = jnp.einsum('de,gh->gdhe', w2, eye).reshape(NSA_KV_HEADS * hd, NSA_KV_HEADS * hd)
    per = pe.reshape(2, half, 1, hd)
    pe2 = jnp.broadcast_to(per, (2, half, NSA_KV_HEADS, hd)).reshape(2, 1, half * NSA_KV_HEADS * hd)
    pe2 = jnp.broadcast_to(pe2, (2, 8, pe2.shape[2]))
    return w.astype(BF16), w2b.astype(BF16), pe2.astype(F32)


def _attention_tables(S):
    moba_sl, nsa_sl = _alibi_slopes()
    moba_sl = moba_sl * LOG2E
    nsa_sl = nsa_sl * LOG2E
    blk = MOBA_BLOCK
    moba_row = jnp.broadcast_to(moba_sl[:, None, None], (MOBA_HEADS, 1, blk))
    nb = S // blk
    nbp = -(-nb // 16) * 16
    col = jnp.arange(LANES)[None, None, :]
    tile = jnp.arange(nb + 1)[:, None, None]
    off = jnp.arange(blk, dtype=F32)[None, :, None]
    moba_aug = jnp.where(jnp.logical_and(col == tile, tile < nb), 1.0,
                         jnp.where(jnp.logical_and(col >= nbp, col < nbp + 3), off, 0.0)).astype(BF16)
    parts = jnp.stack(list(_split3(moba_sl)) + [jnp.zeros_like(moba_sl)] * 13, axis=1)
    moba_srow = jnp.broadcast_to(parts[:, :, None], (MOBA_HEADS, 16, blk)).astype(BF16)

    wl = NSA_GROUP * NSA_TQ
    n_slc = S // NSA_SLC_BLOCK
    assert n_slc <= LANES, "block-choice rows must fit the spare contraction rows"
    nsa_row = jnp.repeat(nsa_sl.reshape(NSA_KV_HEADS, NSA_GROUP), NSA_TQ, axis=1)
    hi, mid, lo = [t[:, None, :] for t in _split3(nsa_row)]
    base = ((1 - jnp.arange(NSA_KV_HEADS)) * HEAD_DIM)[:, None, None]
    rows = jnp.arange(2 * HEAD_DIM)[None, :, None]
    nsa_srow = jnp.where(rows == base, hi, jnp.where(rows == base + 1, mid, jnp.where(
        rows == base + 2, lo, jnp.where(rows == base + 3, NEG_BIG, 0.0)))).astype(BF16)
    lane = jnp.arange(LANES)[None, None, None, :]
    base4 = base[:, None]
    null = jnp.arange(2, dtype=F32)[None, :, None, None]
    koff = jnp.arange(SLC_TILE, dtype=F32)[None, None, :, None]
    nsa_auga = jnp.where(jnp.logical_and(lane >= base4, lane < base4 + 3), koff,
                         jnp.where(lane == base4 + 3, null, 0.0))
    nsa_auga = nsa_auga.reshape(2 * NSA_KV_HEADS, SLC_TILE, LANES).astype(BF16)
    per = SLC_TILE // NSA_SLC_BLOCK
    tile = jnp.arange(S // SLC_TILE)[:, None, None]
    blk_of = tile * per + jnp.arange(SLC_TILE)[None, :, None] // NSA_SLC_BLOCK
    nsa_augb = (jnp.arange(LANES)[None, None, :] == blk_of).astype(BF16)

    nc = S // NSA_CMP_STRIDE
    cu, su = NSA_CMP_LEN // NSA_CMP_STRIDE, NSA_SLC_BLOCK // NSA_CMP_STRIDE
    ii = jnp.arange(nc)[None, :]
    jj = jnp.arange(n_slc)[:, None]
    ovT = jnp.clip(jnp.minimum(ii + cu, (jj + 1) * su) - jnp.maximum(ii, jj * su), 0, None).astype(F32)
    return ((moba_aug, moba_srow, moba_row),
            (ovT, nsa_row.reshape(NSA_KV_HEADS, 1, wl), nsa_srow, nsa_auga, nsa_augb))


def _split3(x):
    hi = x.astype(BF16).astype(F32)
    mid = (x - hi).astype(BF16).astype(F32)
    lo = (x - hi - mid).astype(BF16).astype(F32)
    return hi, mid, lo


def _attention(x, attn_norm_g, w_in, cmp_pe_k, cmp_pe_v, cmp_w1_k, cmp_w2_k, cmp_w1_v, cmp_w2_v):
    B, S, D = x.shape
    wr, wt, wg = _prep_inproj(w_in)
    (mk, kc, vc, ks, kw, mqT, mvT, nqT, vsT, vwT, gT) = _inproj(
        x, attn_norm_g.reshape(1, D), wr, wt, wg, tm=512)
    moba_tabs, nsa_tabs = _attention_tables(S)
    o_moba = _moba(mqT, mk, mvT, *moba_tabs)
    wk, w2k, pek = _prep_compress(cmp_w1_k, cmp_w2_k, cmp_pe_k)
    wv, w2v, pev = _prep_compress(cmp_w1_v, cmp_w2_v, cmp_pe_v)
    nc = S // NSA_CMP_STRIDE
    kcmp, vcmpT = _compress(kc.reshape(B, nc, -1), vc.reshape(B, nc, -1), wk, wv.T, pek, pev, w2k, w2v.T)
    o_nsa = _nsa(nqT, kcmp, vcmpT, ks, vsT, kw, vwT, gT, *nsa_tabs)
    return o_moba, o_nsa


def _moe(x1, hn, e128, w128, w_up, b_up, w_down, b_down, final_norm_g):
    N, D = x1.shape
    rank128, cnt = _ranks(e128)
    counts = cnt[0, :N_EXPERTS].astype(I32)
    padded = (counts + MOE_ROWS - 1) // MOE_ROWS * MOE_ROWS
    pends = jnp.cumsum(padded)
    pstarts = pends - padded
    e4 = e128[:, :TOP_K]
    dest = pstarts[e4] + rank128[:, :TOP_K]
    dest2 = dest.reshape(N // ROUTE_TILE, ROUTE_TILE * TOP_K)
    n_blk = (N * TOP_K + N_EXPERTS * MOE_ROWS + MOE_ROWS - 1) // MOE_ROWS
    P = n_blk * MOE_ROWS
    blk_start = jnp.arange(n_blk, dtype=I32) * MOE_ROWS
    blk_e = jnp.minimum(jnp.sum((pends[None, :] <= blk_start[:, None]).astype(I32), axis=1), N_EXPERTS - 1)
    n_act = (pends[-1:] // MOE_ROWS).astype(I32)
    xrows = _dispatch(dest2, hn, jnp.zeros((P, D // 2), jnp.uint32))
    wT = jnp.swapaxes(w_up, 1, 2).astype(BF16)
    bg = b_up[:, None, 0::2]
    bu = b_up[:, None, 1::2]
    yrows = _experts(blk_e, n_act, xrows, wT, bg, bu, w_down.astype(BF16), b_down[:, None, :])
    return _combine(dest2, x1, w128, final_norm_g.reshape(1, D), yrows)


def kernel(x, attn_norm_g, w_in, cmp_pe_k, cmp_pe_v, cmp_w1_k, cmp_w2_k, cmp_w1_v, cmp_w2_v, w_out, ffn_norm_g, w_router, b_router, w_up, b_up, w_down, b_down, final_norm_g):
    B, S, D = x.shape
    assert attn_norm_g.shape[0] == 1, "single-layer kernel"
    o_moba, o_nsa = _attention(x, attn_norm_g[0], w_in[0], cmp_pe_k[0], cmp_pe_v[0],
                               cmp_w1_k[0], cmp_w2_k[0], cmp_w1_v[0], cmp_w2_v[0])
    N = B * S
    wr = jnp.pad(w_router[0], ((0, 0), (0, LANES - N_EXPERTS)))
    br = jnp.pad(b_router[0], (0, LANES - N_EXPERTS)).reshape(1, LANES)
    x1, hn, e128, w128 = _outproj(o_moba.reshape(N, -1), o_nsa.reshape(N, -1), x.reshape(N, D),
                                  w_out[0].astype(BF16), ffn_norm_g[0].reshape(1, D), wr, br, tm=512)
    out = _moe(x1, hn, e128, w128, w_up[0], b_up[0], w_down[0], b_down[0], final_norm_g)
    return out.reshape(B, S, D)
```

```python
import functools

import jax
import jax.numpy as jnp
import numpy as np
from jax import lax
from jax.experimental import pallas as pl
from jax.experimental.pallas import tpu as pltpu

F32 = jnp.float32
BF16 = jnp.bfloat16
I32 = jnp.int32

HEAD_DIM = 64
MOBA_HEADS = 8
NSA_HEADS = 8
NSA_KV_HEADS = 2
NSA_GROUP = NSA_HEADS // NSA_KV_HEADS
MOBA_BLOCK = 256
MOBA_TOPK = 3
NSA_CMP_LEN = 32
NSA_CMP_STRIDE = 16
NSA_SLC_BLOCK = 64
NSA_SLC_TOPN = 16
NSA_WINDOW = 512
NSA_BRANCHES = 3
N_EXPERTS = 32
TOP_K = 4
SWIGLU_LIMIT = 7.0
SWIGLU_ALPHA = 1.702
RMS_EPS = 1e-5
NEG_BIG = -1e30
LOG2E = 1.4426950408889634

LANES = 128
VMEM_LIMIT = 56 * 1024 * 1024

NSA_TQ = 128
SLC_TILE = 256
WIN_KEYS = NSA_WINDOW + NSA_TQ
MOE_ROWS = 512
ROUTE_TILE = 256
RANK_TILE = 512

NT_DIMS = (((1,), (1,)), ((), ()))


def _params(n_grid):
    return pltpu.CompilerParams(
        dimension_semantics=("arbitrary",) * n_grid,
        vmem_limit_bytes=VMEM_LIMIT,
    )


def _rmsnorm(x, g):
    return x * lax.rsqrt(jnp.mean(x * x, axis=-1, keepdims=True) + RMS_EPS) * g


def _inproj_kernel(x_ref, g_ref, wr_ref, wt_ref, wg_ref,
                   mk_ref, kc_ref, vc_ref, ks_ref, kw_ref,
                   mqT_ref, mvT_ref, nqT_ref, vsT_ref, vwT_ref, gT_ref):
    xn = _rmsnorm(x_ref[0], g_ref[...])
    xb = xn.astype(BF16)
    yr = jnp.dot(xb, wr_ref[...], preferred_element_type=F32)
    mk_ref[0] = yr[:, 0:512].astype(BF16)
    kc_ref[0] = yr[:, 512:640].astype(BF16)
    vc_ref[0] = yr[:, 640:768].astype(BF16)
    ks_ref[0] = yr[:, 768:896].astype(BF16)
    kw_ref[0] = yr[:, 896:1024].astype(BF16)
    yt = lax.dot_general(wt_ref[...], xb, NT_DIMS, preferred_element_type=F32)
    mqT_ref[0] = yt[0:512].astype(BF16)
    mvT_ref[0] = yt[512:1024].astype(BF16)
    nqT_ref[0] = yt[1024:1536].astype(BF16)
    vsT_ref[0] = yt[1536:1664].astype(BF16)
    vwT_ref[0] = yt[1664:1792].astype(BF16)
    gl = lax.dot_general(wg_ref[...], xn, NT_DIMS, precision=lax.Precision.HIGHEST,
                         preferred_element_type=F32)
    gT_ref[0] = jax.nn.sigmoid(gl)


def _inproj(x, g, wr, wt, wg, tm):
    B, S, D = x.shape
    grid = (B, S // tm)
    row = lambda w: pl.BlockSpec((1, tm, w), lambda b, i: (b, i, 0))
    col = lambda h: pl.BlockSpec((1, h, tm), lambda b, i: (b, 0, i))
    full = lambda a: pl.BlockSpec(a.shape, lambda b, i: (0,) * a.ndim)
    out_shape = [
        jax.ShapeDtypeStruct((B, S, 512), BF16),
        jax.ShapeDtypeStruct((B, S, 128), BF16),
        jax.ShapeDtypeStruct((B, S, 128), BF16),
        jax.ShapeDtypeStruct((B, S, 128), BF16),
        jax.ShapeDtypeStruct((B, S, 128), BF16),
        jax.ShapeDtypeStruct((B, 512, S), BF16),
        jax.ShapeDtypeStruct((B, 512, S), BF16),
        jax.ShapeDtypeStruct((B, 512, S), BF16),
        jax.ShapeDtypeStruct((B, 128, S), BF16),
        jax.ShapeDtypeStruct((B, 128, S), BF16),
        jax.ShapeDtypeStruct((B, 32, S), F32),
    ]
    out_specs = [row(512), row(128), row(128), row(128), row(128),
                 col(512), col(512), col(512), col(128), col(128), col(32)]
    return pl.pallas_call(
        _inproj_kernel, grid=grid,
        in_specs=[pl.BlockSpec((1, tm, D), lambda b, i: (b, i, 0)),
                  full(g), full(wr), full(wt), full(wg)],
        out_specs=out_specs, out_shape=out_shape,
        compiler_params=_params(2), name="inproj",
    )(x, g, wr, wt, wg)


def _flash_step(zs, cs, vts, states):
    mts = [jnp.max(z, axis=0, keepdims=True) - c for z, c in zip(zs, cs)]
    m_news = [jnp.maximum(st[0], mt) for st, mt in zip(states, mts)]
    alphas = [jnp.exp2(st[0] - mn) for st, mn in zip(states, m_news)]
    ps = [jnp.exp2(z - (mn + c)) for z, mn, c in zip(zs, m_news, cs)]
    ls = [a * st[1] + jnp.sum(p, axis=0, keepdims=True) for a, st, p in zip(alphas, states, ps)]
    pvs = [jnp.dot(vt, p.astype(BF16), preferred_element_type=F32) for vt, p in zip(vts, ps)]
    accs = [a * st[2] + pv for a, st, pv in zip(alphas, states, pvs)]
    return [(mn, l, acc) for mn, l, acc in zip(m_news, ls, accs)]


def _flash_init(n_q):
    return (jnp.full((1, n_q), NEG_BIG, F32), jnp.zeros((1, n_q), F32),
            jnp.zeros((HEAD_DIM, n_q), F32))


def _softmax_stage(s, c, m, l):
    mt = jnp.max(s, axis=0, keepdims=True) - c
    m_new = jnp.maximum(m, mt)
    alpha = jnp.exp2(m - m_new)
    p = jnp.exp2(s - (m_new + c))
    l_new = alpha * l + jnp.sum(p, axis=0, keepdims=True)
    return m_new, l_new, p.astype(BF16), alpha


def _pipelined_tiles(scores, values, offsets, s_ref, p_ref, first, n_tiles, j_first):
    n_ch = len(first)
    chains = range(n_ch)

    def qk_into(slot, j):
        sc = scores(j)
        for c in chains:
            s_ref[slot, c] = sc[c]

    def pv_from(slot, j, alphas, accs):
        vt = values(j)
        return [alphas[c] * accs[c] + jnp.dot(vt[c], p_ref[slot, c], preferred_element_type=F32)
                for c in chains]

    def softmax_into(slot, j, ms, ls):
        cs = offsets(j)
        new = [_softmax_stage(s_ref[slot, c], cs[c], ms[c], ls[c]) for c in chains]
        for c in chains:
            p_ref[slot, c] = new[c][2]
        return [n[0] for n in new], [n[1] for n in new], [n[3] for n in new]

    for c in chains:
        p_ref[1, c] = first[c][2]
    qk_into(0, 0)

    def pair(i, carry):
        ms, ls, alphas, accs, j_prev = carry
        t = 2 * i
        qk_into(1, t + 1)
        accs = pv_from(1, j_prev, alphas, accs)
        ms, ls, alphas = softmax_into(0, t, ms, ls)
        qk_into(0, t + 2)
        accs = pv_from(0, t, alphas, accs)
        ms, ls, alphas = softmax_into(1, t + 1, ms, ls)
        return ms, ls, alphas, accs, t + 1

    n_q = first[0][0].shape[1]
    init = ([f[0] for f in first], [f[1] for f in first], [f[3] for f in first],
            [jnp.zeros((HEAD_DIM, n_q), F32)] * n_ch, j_first)
    _, ls, alphas, accs, j_last = lax.fori_loop(0, (n_tiles + 1) // 2, pair, init)
    accs = pv_from(1, j_last, alphas, accs)
    return [accs[c] / jnp.maximum(ls[c], 1e-30) for c in chains]


def _moba_kernel(qT_ref, k_ref, vT_ref, aug_ref, srow_ref, sl_ref, o_ref,
                 kmean_ref, s_ref, p_ref, *, nb, nbp, topk):
    qi = pl.program_id(2)
    blk = MOBA_BLOCK

    @pl.when(qi == 0)
    def _():
        kmean_ref[...] = jnp.zeros(kmean_ref.shape, F32)

        def body(n, carry):
            kb = k_ref[0, pl.ds(pl.multiple_of(n * blk, blk), blk), :].astype(F32)
            kmean_ref[pl.ds(n, 1), :] = jnp.mean(kb, axis=0, keepdims=True)
            return carry
        lax.fori_loop(0, nb, body, 0)

    qT = qT_ref[0]
    row = lax.broadcasted_iota(I32, qT.shape, 0)
    qpad = [jnp.where((row >> 6) == h, qT, jnp.zeros_like(qT)) for h in range(2)]

    bidx = lax.broadcasted_iota(I32, (nbp, blk), 0)
    rhs = []
    for h in range(2):
        gate = jnp.dot(kmean_ref[...], qpad[h].astype(F32),
                       precision=lax.Precision.HIGHEST, preferred_element_type=F32)
        gsc = jnp.where(bidx < qi, gate, -jnp.inf)
        bias = jnp.full((nbp, blk), NEG_BIG, F32)
        for _ in range(topk):
            mx = jnp.max(gsc, axis=0, keepdims=True)
            idx = jnp.min(jnp.where(gsc == mx, bidx, nbp), axis=0, keepdims=True)
            pick = jnp.logical_and(bidx == idx, mx > -jnp.inf)
            bias = jnp.where(pick, 0.0, bias)
            gsc = jnp.where(pick, -jnp.inf, gsc)
        pad = jnp.zeros((2 * HEAD_DIM - nbp - 16, blk), BF16)
        rhs.append(jnp.concatenate([qpad[h], bias.astype(BF16), srow_ref[h], pad], axis=0))

    def scores(j, a):
        k0 = pl.multiple_of(j * blk, blk)
        lhs = jnp.concatenate([k_ref[0, pl.ds(k0, blk), :], aug_ref[a]], axis=1)
        return [jnp.dot(lhs, rhs[h], preferred_element_type=F32) for h in range(2)]

    def values(j):
        k0 = pl.multiple_of(j * blk, blk)
        return [vT_ref[0, h * HEAD_DIM:(h + 1) * HEAD_DIM, pl.ds(k0, blk)] for h in range(2)]

    def offsets(j):
        dq = ((qi - j) * blk).astype(F32)
        return [sl_ref[h] * dq for h in range(2)]

    ik = lax.broadcasted_iota(I32, (blk, blk), 0)
    iq = lax.broadcasted_iota(I32, (blk, blk), 1)
    s_own = [jnp.where(ik <= iq, s, NEG_BIG) for s in scores(qi, nb)]
    m0 = jnp.full((1, blk), NEG_BIG, F32)
    l0 = jnp.zeros((1, blk), F32)
    first = [_softmax_stage(s_own[h], jnp.zeros((1, blk), F32), m0, l0) for h in range(2)]

    outs = _pipelined_tiles(lambda j: scores(jnp.minimum(j, nb - 1), jnp.minimum(j, nb - 1)),
                            values, offsets, s_ref, p_ref, first, qi, qi)
    o_ref[0] = jnp.concatenate(outs, axis=0).T.astype(BF16)


def _moba(mqT, mk, mvT, aug, srow, sl):
    B, _, S = mqT.shape
    blk = MOBA_BLOCK
    nb = S // blk
    topk = min(MOBA_TOPK, nb)
    nbp = -(-nb // 16) * 16
    grid = (B, MOBA_HEADS // 2, nb)
    return pl.pallas_call(
        functools.partial(_moba_kernel, nb=nb, nbp=nbp, topk=topk), grid=grid,
        in_specs=[
            pl.BlockSpec((1, 128, blk), lambda b, p, i: (b, p, i)),
            pl.BlockSpec((1, S, 128), lambda b, p, i: (b, 0, p)),
            pl.BlockSpec((1, 128, S), lambda b, p, i: (b, p, 0)),
            pl.BlockSpec(aug.shape, lambda b, p, i: (0, 0, 0)),
            pl.BlockSpec((2, 16, blk), lambda b, p, i: (p, 0, 0)),
            pl.BlockSpec((2, 1, blk), lambda b, p, i: (p, 0, 0)),
        ],
        out_specs=pl.BlockSpec((1, blk, 128), lambda b, p, i: (b, i, p)),
        out_shape=jax.ShapeDtypeStruct((B, S, 512), BF16),
        scratch_shapes=[
            pltpu.VMEM((nbp, 128), F32),
            pltpu.VMEM((2, 2, blk, blk), F32),
            pltpu.VMEM((2, 2, blk, blk), BF16),
        ],
        compiler_params=_params(3), name="moba",
    )(mqT, mk, mvT, aug, srow, sl)


def _compress_kernel(kc_ref, vc_ref, wk_ref, wvT_ref, pek_ref, pev_ref, w2k_ref, w2vT_ref,
                     kcmp_ref, vcmpT_ref):
    nc = kc_ref.shape[1]

    wk = wk_ref[...]
    ab = jnp.dot(kc_ref[0], wk, preferred_element_type=F32)
    pt = (jnp.dot(pek_ref[0], wk[:, 0:128].astype(F32), preferred_element_type=F32)
          + jnp.dot(pek_ref[1], wk[:, 128:256].astype(F32), preferred_element_type=F32))
    pre = ab[:, 0:128] + pltpu.roll(ab[:, 128:256], nc - 1, 0) + pt[0:1]
    hid = jax.nn.gelu(pre)
    kcmp_ref[0] = jnp.dot(hid.astype(BF16), w2k_ref[...], preferred_element_type=F32).astype(BF16)

    wvT = wvT_ref[...]
    abT = lax.dot_general(wvT, vc_ref[0], NT_DIMS, preferred_element_type=F32)
    ptT = (lax.dot_general(wvT[0:128].astype(F32), pev_ref[0], NT_DIMS, preferred_element_type=F32)
           + lax.dot_general(wvT[128:256].astype(F32), pev_ref[1], NT_DIMS, preferred_element_type=F32))
    preT = abT[0:128] + pltpu.roll(abT[128:256], nc - 1, 1) + ptT[:, 0:1]
    hidT = jax.nn.gelu(preT)
    vcmpT_ref[0] = jnp.dot(w2vT_ref[...], hidT.astype(BF16), preferred_element_type=F32).astype(BF16)


def _compress(kc2, vc2, wk, wvT, pek, pev, w2k, w2vT):
    B, nc, _ = kc2.shape
    full = lambda a: pl.BlockSpec(a.shape, lambda b: (0,) * a.ndim)
    blk = pl.BlockSpec((1, nc, kc2.shape[2]), lambda b: (b, 0, 0))
    return pl.pallas_call(
        _compress_kernel, grid=(B,),
        in_specs=[blk, blk, full(wk), full(wvT), full(pek), full(pev), full(w2k), full(w2vT)],
        out_specs=[pl.BlockSpec((1, nc, 128), lambda b: (b, 0, 0)),
                   pl.BlockSpec((1, 128, nc), lambda b: (b, 0, 0))],
        out_shape=[jax.ShapeDtypeStruct((B, nc, 128), BF16),
                   jax.ShapeDtypeStruct((B, 128, nc), BF16)],
        compiler_params=_params(1), name="nsa_compress",
    )(kc2, vc2, wk, wvT, pek, pev, w2k, w2vT)


def _nsa_kernel(qT_ref, kcmp_ref, vcmpT_ref, ks_ref, vsT_ref, kw_ref, vwT_ref,
                g_ref, sl_ref, srow_ref, auga_ref, augb_ref, augc_ref, tblc_ref, bw_ref, o_ref,
                s_ref, p_ref, pc_ref, *, n_slc, topn):
    g = pl.program_id(1)
    qi = pl.program_id(2)
    tq = NSA_TQ
    hg = NSA_GROUP
    wl = hg * tq
    q0 = qi * tq

    q4 = qT_ref[0]
    qT = jnp.concatenate([q4[h * HEAD_DIM:(h + 1) * HEAD_DIM] for h in range(hg)], axis=1)
    qT2 = jnp.concatenate([qT, qT], axis=0)
    rowi = lax.broadcasted_iota(I32, qT2.shape, 0)
    qpad = jnp.where((rowi >> 6) == g, qT2, jnp.zeros_like(qT2))
    slope = sl_ref[0]
    lane = lax.broadcasted_iota(I32, (1, wl), 1)
    t_q = q0 + (lane & (tq - 1))

    nc = kcmp_ref.shape[1]
    rhs_top = jnp.where((rowi >> 6) == g, qT2, srow_ref[0])
    mine_c = (lax.broadcasted_iota(I32, (nc, LANES), 1) >> 6) == g
    lhs_c = jnp.where(mine_c, kcmp_ref[0], augc_ref[0])
    first_c = pl.multiple_of(nc - qi * (tq // NSA_CMP_STRIDE), 8)
    z = jnp.dot(lhs_c, rhs_top, preferred_element_type=F32) + tblc_ref[pl.ds(first_c, nc), :]
    mx = jnp.max(z, axis=0, keepdims=True)
    e = jnp.exp2(z - mx)
    den = jnp.maximum(jnp.sum(e, axis=0, keepdims=True), 1e-30)
    p = e * jnp.where(t_q >= NSA_CMP_LEN - 1, 1.0 / den, 0.0)
    o_c = jnp.dot(vcmpT_ref[0], p.astype(BF16), preferred_element_type=F32)

    pc = p[:, 0:tq]
    for h in range(1, hg):
        pc = pc + p[:, h * tq:(h + 1) * tq]
    pc_ref[...] = pc
    su = NSA_SLC_BLOCK // NSA_CMP_STRIDE
    x = [pc_ref[pl.ds(k, n_slc, stride=su), :] for k in range(su)]
    jb = lax.broadcasted_iota(I32, (n_slc, tq), 0)
    prev = jnp.where(jb == 0, 0.0, pltpu.roll(x[3], 1, 0))
    imp = 2.0 * (x[0] + x[1] + x[2]) + x[3] + prev
    cur = (q0 + lax.broadcasted_iota(I32, (1, tq), 1)) >> 6
    allowed = jb <= cur
    forced = jnp.logical_or(jb == 0, jnp.logical_or(jb == cur, jb == cur - 1))
    bias = jnp.where(jnp.logical_and(allowed, forced), 0.0, NEG_BIG)
    sc = jnp.where(jnp.logical_and(allowed, jnp.logical_not(forced)), imp, -1.0)
    for _ in range(topn - 3):
        smx = jnp.max(sc, axis=0, keepdims=True)
        idx = jnp.min(jnp.where(sc == smx, jb, n_slc), axis=0, keepdims=True)
        pick = jnp.logical_and(jb == idx, smx >= 0.0)
        bias = jnp.where(pick, 0.0, bias)
        sc = jnp.where(pick, -1.0, sc)

    n_ch = 2
    wc = wl // n_ch

    def lane_split(a):
        return [a[:, c * wc:(c + 1) * wc] for c in range(n_ch)]

    if n_slc < LANES:
        bias = jnp.concatenate([bias, jnp.zeros((LANES - n_slc, tq), F32)], axis=0)
    bias4 = jnp.concatenate([bias.astype(BF16)] * hg, axis=1)
    rhs = jnp.concatenate([rhs_top, bias4], axis=0)
    mine = (lax.broadcasted_iota(I32, (SLC_TILE, LANES), 1) >> 6) == g
    n_tiles = augb_ref.shape[0]

    def scores(j, null):
        k0 = pl.multiple_of(j * SLC_TILE, SLC_TILE)
        kt = ks_ref[0, pl.ds(k0, SLC_TILE), :]
        lhs = jnp.concatenate([jnp.where(mine, kt, auga_ref[null]), augb_ref[j]], axis=1)
        return jnp.dot(lhs, rhs, preferred_element_type=F32)

    def values(j):
        return [vsT_ref[0, :, pl.ds(pl.multiple_of(j * SLC_TILE, SLC_TILE), SLC_TILE)]] * n_ch

    def offsets(j):
        return lane_split(slope * (q0 - j * SLC_TILE).astype(F32))

    jd = lax.div(q0, SLC_TILE)
    t_k = jd * SLC_TILE + lax.broadcasted_iota(I32, (SLC_TILE, wl), 0)
    s_diag = lane_split(jnp.where(t_k <= t_q, scores(jd, 0), NEG_BIG))
    c_diag = offsets(jd)
    first = [_softmax_stage(s_diag[c], c_diag[c], jnp.full((1, wc), NEG_BIG, F32),
                            jnp.zeros((1, wc), F32)) for c in range(n_ch)]

    def past_scores(j):
        return lane_split(scores(jnp.minimum(j, n_tiles - 1), (j >= jd).astype(I32)))
    o_s = jnp.concatenate(
        _pipelined_tiles(past_scores, values, offsets, s_ref, p_ref, first, jd, jd), axis=1)

    start = pl.multiple_of(jnp.maximum(q0 - NSA_WINDOW, 0), tq)
    first_w = pl.multiple_of(NSA_WINDOW - (q0 - start), tq)
    kt = kw_ref[0, pl.ds(start, WIN_KEYS), :]
    z = jnp.dot(kt, qpad, preferred_element_type=F32) + bw_ref[0, pl.ds(first_w, WIN_KEYS), :]
    mx = jnp.max(z, axis=0, keepdims=True)
    p = jnp.exp2(z - mx)
    den = jnp.maximum(jnp.sum(p, axis=0, keepdims=True), 1e-30)
    o_w = jnp.dot(vwT_ref[0, :, pl.ds(start, WIN_KEYS)], p.astype(BF16),
                  preferred_element_type=F32) / den

    gt = g_ref[0]

    def gate_row(br):
        return jnp.concatenate([gt[br * hg + h:br * hg + h + 1] for h in range(hg)], axis=1)

    o = gate_row(0) * o_c + gate_row(1) * o_s + gate_row(2) * o_w
    o4 = jnp.concatenate([o[:, h * tq:(h + 1) * tq] for h in range(hg)], axis=0)
    o_ref[0] = o4.T.astype(BF16)


def _nsa(nqT, kcmp, vcmpT, ks, vsT, kw, vwT, gT, sl, srow, auga, augb, augc, tblc, bw):
    B, _, S = nqT.shape
    tq = NSA_TQ
    nc = kcmp.shape[1]
    n_slc = S // NSA_SLC_BLOCK
    topn = min(NSA_SLC_TOPN, n_slc)
    wl = NSA_GROUP * tq
    grid = (B, NSA_KV_HEADS, S // tq)
    return pl.pallas_call(
        functools.partial(_nsa_kernel, n_slc=n_slc, topn=topn), grid=grid,
        in_specs=[
            pl.BlockSpec((1, NSA_GROUP * HEAD_DIM, tq), lambda b, g, i: (b, g, i)),
            pl.BlockSpec((1, nc, 128), lambda b, g, i: (b, 0, 0)),
            pl.BlockSpec((1, HEAD_DIM, nc), lambda b, g, i: (b, g, 0)),
            pl.BlockSpec((1, S, 128), lambda b, g, i: (b, 0, 0)),
            pl.BlockSpec((1, HEAD_DIM, S), lambda b, g, i: (b, g, 0)),
            pl.BlockSpec((1, S, 128), lambda b, g, i: (b, 0, 0)),
            pl.BlockSpec((1, HEAD_DIM, S), lambda b, g, i: (b, g, 0)),
            pl.BlockSpec((1, 16, tq), lambda b, g, i: (b, g, i)),
            pl.BlockSpec((1, 1, wl), lambda b, g, i: (g, 0, 0)),
            pl.BlockSpec((1, 2 * HEAD_DIM, wl), lambda b, g, i: (g, 0, 0)),
            pl.BlockSpec((2, SLC_TILE, LANES), lambda b, g, i: (g, 0, 0)),
            pl.BlockSpec(augb.shape, lambda b, g, i: (0, 0, 0)),
            pl.BlockSpec((1, nc, LANES), lambda b, g, i: (g, 0, 0)),
            pl.BlockSpec(tblc.shape, lambda b, g, i: (0, 0)),
            pl.BlockSpec((1,) + bw.shape[1:], lambda b, g, i: (g, 0, 0)),
        ],
        out_specs=pl.BlockSpec((1, tq, NSA_GROUP * HEAD_DIM), lambda b, g, i: (b, i, g)),
        out_shape=jax.ShapeDtypeStruct((B, S, 512), BF16),
        scratch_shapes=[
            pltpu.VMEM((2, 2, SLC_TILE, wl // 2), F32),
            pltpu.VMEM((2, 2, SLC_TILE, wl // 2), BF16),
            pltpu.VMEM((nc, tq), F32),
        ],
        compiler_params=_params(3), name="nsa",
    )(nqT, kcmp, vcmpT, ks, vsT, kw, vwT, gT, sl, srow, auga, augb, augc, tblc, bw)


def _outproj_kernel(om_ref, on_ref, x_ref, wo_ref, g_ref, wr_ref, br_ref,
                    x1_ref, hn_ref, e_ref, w_ref):
    attn = (jnp.dot(om_ref[...], wo_ref[0:512, :], preferred_element_type=F32)
            + jnp.dot(on_ref[...], wo_ref[512:1024, :], preferred_element_type=F32))
    x1 = x_ref[...] + attn
    x1_ref[...] = x1
    hn = _rmsnorm(x1, g_ref[...])
    hn_ref[...] = hn
    logits = jnp.dot(hn, wr_ref[...], precision=lax.Precision.HIGHEST,
                     preferred_element_type=F32) + br_ref[...]
    tm = logits.shape[0]
    lane = lax.broadcasted_iota(I32, (tm, LANES), 1)
    sc = jnp.where(lane < N_EXPERTS, logits, -jnp.inf)
    e_out = jnp.zeros((tm, LANES), I32)
    vals = []
    for k in range(TOP_K):
        mx = jnp.max(sc, axis=1, keepdims=True)
        idx = jnp.min(jnp.where(sc == mx, lane, LANES), axis=1, keepdims=True)
        e_out = jnp.where(lane == k, idx, e_out)
        sc = jnp.where(lane == idx, -jnp.inf, sc)
        vals.append(mx)
    ex = [jnp.exp(v - vals[0]) for v in vals]
    den = ex[0] + ex[1] + ex[2] + ex[3]
    w_out = jnp.zeros((tm, LANES), F32)
    for k in range(TOP_K):
        w_out = jnp.where(lane == k, ex[k] / den, w_out)
    e_ref[...] = e_out
    w_ref[...] = w_out


def _outproj(om, on, x, wo, g, wr, br, tm):
    N, D = x.shape
    full = lambda a: pl.BlockSpec(a.shape, lambda i: (0,) * a.ndim)
    row = lambda w: pl.BlockSpec((tm, w), lambda i: (i, 0))
    return pl.pallas_call(
        _outproj_kernel, grid=(N // tm,),
        in_specs=[row(512), row(512), row(D), full(wo), full(g), full(wr), full(br)],
        out_specs=[row(D), row(D), row(LANES), row(LANES)],
        out_shape=[jax.ShapeDtypeStruct((N, D), F32), jax.ShapeDtypeStruct((N, D), F32),
                   jax.ShapeDtypeStruct((N, LANES), I32), jax.ShapeDtypeStruct((N, LANES), F32)],
        compiler_params=_params(1), name="outproj_router",
    )(om, on, x, wo, g, wr, br)


def _rank_kernel(e_ref, rank_ref, cnt_ref, base_ref):
    i = pl.program_id(0)
    T = e_ref.shape[0]

    @pl.when(i == 0)
    def _():
        base_ref[...] = jnp.zeros(base_ref.shape, F32)

    e = e_ref[...]
    lane = lax.broadcasted_iota(I32, (T, LANES), 1)
    tril = jnp.where(lax.broadcasted_iota(I32, (T, T), 0) >= lax.broadcasted_iota(I32, (T, T), 1),
                     1.0, 0.0).astype(BF16)
    out = jnp.zeros((T, LANES), I32)
    for k in range(TOP_K):
        hit = lane == e[:, k:k + 1]
        oh = jnp.where(hit, 1.0, 0.0)
        cum = jnp.dot(tril, oh.astype(BF16), preferred_element_type=F32)
        base = base_ref[0:1, :]
        r = jnp.sum(jnp.where(hit, cum - 1.0 + base, 0.0), axis=1, keepdims=True)
        out = jnp.where(lane == k, r.astype(I32), out)
        base_ref[...] = base_ref[...] + jnp.sum(oh, axis=0, keepdims=True)
    rank_ref[...] = out
    cnt_ref[...] = base_ref[...]


def _ranks(e128):
    N = e128.shape[0]
    T = RANK_TILE
    return pl.pallas_call(
        _rank_kernel, grid=(N // T,),
        in_specs=[pl.BlockSpec((T, LANES), lambda i: (i, 0))],
        out_specs=[pl.BlockSpec((T, LANES), lambda i: (i, 0)),
                   pl.BlockSpec((8, LANES), lambda i: (0, 0))],
        out_shape=[jax.ShapeDtypeStruct((N, LANES), I32),
                   jax.ShapeDtypeStruct((8, LANES), F32)],
        scratch_shapes=[pltpu.VMEM((8, LANES), F32)],
        compiler_params=_params(1), name="route_ranks",
    )(e128)


def _row_copy(src, dst, i_src, i_dst, sem):
    return pltpu.make_async_copy(src.at[pl.ds(i_src, 1)], dst.at[pl.ds(i_dst, 1)], sem)


def _dispatch_kernel(dest_hbm, hp_ref, xz_hbm, out_hbm, idx_ref, isem, sem):
    del xz_hbm
    i = pl.program_id(0)
    T = ROUTE_TILE
    cp = pltpu.make_async_copy(dest_hbm.at[i], idx_ref, isem)
    cp.start()
    cp.wait()

    def issue(t, carry):
        for k in range(TOP_K):
            _row_copy(hp_ref, out_hbm, t, idx_ref[t * TOP_K + k], sem).start()
        return carry
    lax.fori_loop(0, T, issue, 0)

    def drain(t, carry):
        for k in range(TOP_K):
            _row_copy(hp_ref, out_hbm, 0, 0, sem).wait()
        return carry
    lax.fori_loop(0, T, drain, 0)


def _dispatch(dest2, hp, xzero):
    nsteps = dest2.shape[0]
    T = ROUTE_TILE
    return pl.pallas_call(
        _dispatch_kernel, grid=(nsteps,),
        in_specs=[pl.BlockSpec(memory_space=pl.ANY),
                  pl.BlockSpec((T, hp.shape[1]), lambda i: (i, 0)),
                  pl.BlockSpec(memory_space=pl.ANY)],
        out_specs=pl.BlockSpec(memory_space=pl.ANY),
        out_shape=jax.ShapeDtypeStruct(xzero.shape, xzero.dtype),
        scratch_shapes=[pltpu.SMEM((T * TOP_K,), I32),
                        pltpu.SemaphoreType.DMA, pltpu.SemaphoreType.DMA],
        input_output_aliases={2: 0},
        compiler_params=_params(1), name="moe_dispatch",
    )(dest2, hp, xzero)


def _expert_kernel(be_ref, na_ref, x_ref, wg_ref, wu_ref, bg_ref, bu_ref, wd_ref, bd_ref, y_ref):
    b = pl.program_id(0)
    active = b < na_ref[0]

    @pl.when(active)
    def _():
        xb = x_ref[...].astype(BF16)
        gg = lax.dot_general(xb, wg_ref[0], NT_DIMS, preferred_element_type=F32) + bg_ref[0]
        uu = lax.dot_general(xb, wu_ref[0], NT_DIMS, preferred_element_type=F32) + bu_ref[0]
        gg = jnp.minimum(gg, SWIGLU_LIMIT)
        uu = jnp.clip(uu, -SWIGLU_LIMIT, SWIGLU_LIMIT)
        a = gg * jax.nn.sigmoid(SWIGLU_ALPHA * gg) * (uu + 1.0)
        y_ref[...] = jnp.dot(a.astype(BF16), wd_ref[0], preferred_element_type=F32) + bd_ref[0]

    @pl.when(jnp.logical_not(active))
    def _():
        y_ref[...] = jnp.zeros(y_ref.shape, F32)


def _experts(blk_e, n_act, xrows, wgT, wuT, bg, bu, wd, bd):
    P, D = xrows.shape
    F = wgT.shape[1]
    n_blk = P // MOE_ROWS
    wspec = lambda r, c: pl.BlockSpec((1, r, c), lambda b, be, na: (be[b], 0, 0))
    grid_spec = pltpu.PrefetchScalarGridSpec(
        num_scalar_prefetch=2, grid=(n_blk,),
        in_specs=[pl.BlockSpec((MOE_ROWS, D), lambda b, be, na: (b, 0)),
                  wspec(F, D), wspec(F, D), wspec(1, F), wspec(1, F), wspec(F, D), wspec(1, D)],
        out_specs=pl.BlockSpec((MOE_ROWS, D), lambda b, be, na: (b, 0)),
    )
    return pl.pallas_call(
        _expert_kernel, grid_spec=grid_spec,
        out_shape=jax.ShapeDtypeStruct((P, D), F32),
        compiler_params=_params(1), name="moe_experts",
    )(blk_e, n_act, xrows, wgT, wuT, bg, bu, wd, bd)


def _combine_kernel(dest_hbm, x1_ref, w_ref, g_ref, y_hbm, o_ref, idx_ref, buf_ref, isem, sem):
    i = pl.program_id(0)
    T = ROUTE_TILE
    cp = pltpu.make_async_copy(dest_hbm.at[i], idx_ref, isem)
    cp.start()
    cp.wait()

    def issue(t, carry):
        for k in range(TOP_K):
            _row_copy(y_hbm, buf_ref.at[k], idx_ref[t * TOP_K + k], t, sem).start()
        return carry
    lax.fori_loop(0, T, issue, 0)

    def drain(t, carry):
        for k in range(TOP_K):
            _row_copy(y_hbm, buf_ref.at[k], 0, 0, sem).wait()
        return carry
    lax.fori_loop(0, T, drain, 0)

    acc = x1_ref[...]
    w = w_ref[...]
    for k in range(TOP_K):
        acc = acc + w[:, k:k + 1] * buf_ref[k]
    o_ref[...] = _rmsnorm(acc, g_ref[...])


def _combine(dest2, x1, w128, g, yrows):
    N, D = x1.shape
    T = ROUTE_TILE
    return pl.pallas_call(
        _combine_kernel, grid=(N // T,),
        in_specs=[pl.BlockSpec(memory_space=pl.ANY),
                  pl.BlockSpec((T, D), lambda i: (i, 0)),
                  pl.BlockSpec((T, LANES), lambda i: (i, 0)),
                  pl.BlockSpec(g.shape, lambda i: (0, 0)),
                  pl.BlockSpec(memory_space=pl.ANY)],
        out_specs=pl.BlockSpec((T, D), lambda i: (i, 0)),
        out_shape=jax.ShapeDtypeStruct((N, D), F32),
        scratch_shapes=[pltpu.SMEM((T * TOP_K,), I32),
                        pltpu.VMEM((TOP_K, T, D), F32),
                        pltpu.SemaphoreType.DMA, pltpu.SemaphoreType.DMA],
        compiler_params=_params(1), name="moe_combine",
    )(dest2, x1, w128, g, yrows)


def _alibi_slopes():
    n = MOBA_HEADS + NSA_HEADS
    s = jnp.exp2(-8.0 * jnp.arange(1, n + 1, dtype=F32) / n)
    return s[0::2], s[1::2]


def _prep_inproj(w_in):
    hd = HEAD_DIM
    sizes = [MOBA_HEADS * hd] * 3 + [NSA_HEADS * hd] + [NSA_KV_HEADS * hd] * 6 + [NSA_BRANCHES * NSA_HEADS]
    cuts = np.cumsum([0] + sizes)
    mq, mk, mv, nq, kc, vc, ks, vs, kw, vw, ng = [w_in[:, cuts[i]:cuts[i + 1]] for i in range(11)]
    qscale = (hd ** -0.5) * LOG2E
    wr = jnp.concatenate([mk, kc, vc, ks, kw], axis=1).astype(BF16)
    wt = jnp.concatenate([mq * qscale, mv, nq * qscale, vs, vw], axis=1).T.astype(BF16)
    ngr = ng.reshape(-1, NSA_KV_HEADS, NSA_GROUP, NSA_BRANCHES).transpose(1, 3, 2, 0)
    ngr = ngr.reshape(NSA_KV_HEADS, NSA_BRANCHES * NSA_GROUP, -1)
    wg = jnp.pad(ngr, ((0, 0), (0, 16 - NSA_BRANCHES * NSA_GROUP), (0, 0))).reshape(32, -1)
    return wr, wt, wg.astype(F32)


def _prep_compress(w1, w2, pe):
    hd, half = HEAD_DIM, NSA_CMP_STRIDE
    w1r = w1.reshape(2, half, hd, hd)
    eye = jnp.eye(NSA_KV_HEADS, dtype=w1.dtype)
    w = jnp.einsum('alde,gh->lgdahe', w1r, eye).reshape(half * NSA_KV_HEADS * hd, 2 * NSA_KV_HEADS * hd)
    w2b = jnp.einsum('de,gh->gdhe', w2, eye).reshape(NSA_KV_HEADS * hd, NSA_KV_HEADS * hd)
    per = pe.reshape(2, half, 1, hd)
    pe2 = jnp.broadcast_to(per, (2, half, NSA_KV_HEADS, hd)).reshape(2, 1, half * NSA_KV_HEADS * hd)
    pe2 = jnp.broadcast_to(pe2, (2, 8, pe2.shape[2]))
    return w.astype(BF16), w2b.astype(BF16), pe2.astype(F32)


def _attention_tables(S):
    moba_sl, nsa_sl = _alibi_slopes()
    moba_sl = moba_sl * LOG2E
    nsa_sl = nsa_sl * LOG2E
    blk = MOBA_BLOCK
    moba_row = jnp.broadcast_to(moba_sl[:, None, None], (MOBA_HEADS, 1, blk))
    nb = S // blk
    nbp = -(-nb // 16) * 16
    col = jnp.arange(LANES)[None, None, :]
    tile = jnp.arange(nb + 1)[:, None, None]
    off = jnp.arange(blk, dtype=F32)[None, :, None]
    moba_aug = jnp.where(jnp.logical_and(col == tile, tile < nb), 1.0,
                         jnp.where(jnp.logical_and(col >= nbp, col < nbp + 3), off, 0.0)).astype(BF16)
    parts = jnp.stack(list(_split3(moba_sl)) + [jnp.zeros_like(moba_sl)] * 13, axis=1)
    moba_srow = jnp.broadcast_to(parts[:, :, None], (MOBA_HEADS, 16, blk)).astype(BF16)

    wl = NSA_GROUP * NSA_TQ
    n_slc = S // NSA_SLC_BLOCK
    assert n_slc <= LANES, "block-choice rows must fit the spare contraction rows"
    nsa_row = jnp.repeat(nsa_sl.reshape(NSA_KV_HEADS, NSA_GROUP), NSA_TQ, axis=1)
    hi, mid, lo = [t[:, None, :] for t in _split3(nsa_row)]
    base = ((1 - jnp.arange(NSA_KV_HEADS)) * HEAD_DIM)[:, None, None]
    rows = jnp.arange(2 * HEAD_DIM)[None, :, None]
    nsa_srow = jnp.where(rows == base, hi, jnp.where(rows == base + 1, mid, jnp.where(
        rows == base + 2, lo, jnp.where(rows == base + 3, NEG_BIG, 0.0)))).astype(BF16)
    lane = jnp.arange(LANES)[None, None, None, :]
    base4 = base[:, None]
    null = jnp.arange(2, dtype=F32)[None, :, None, None]
    koff = jnp.arange(SLC_TILE, dtype=F32)[None, None, :, None]
    nsa_auga = jnp.where(jnp.logical_and(lane >= base4, lane < base4 + 3), koff,
                         jnp.where(lane == base4 + 3, null, 0.0))
    nsa_auga = nsa_auga.reshape(2 * NSA_KV_HEADS, SLC_TILE, LANES).astype(BF16)
    per = SLC_TILE // NSA_SLC_BLOCK
    tile = jnp.arange(S // SLC_TILE)[:, None, None]
    blk_of = tile * per + jnp.arange(SLC_TILE)[None, :, None] // NSA_SLC_BLOCK
    nsa_augb = (jnp.arange(LANES)[None, None, :] == blk_of).astype(BF16)

    nc = S // NSA_CMP_STRIDE
    ci = jnp.arange(nc)[None, :, None]
    lane3 = jnp.arange(LANES)[None, None, :]
    nsa_augc = jnp.where(jnp.logical_and(lane3 >= base + 4, lane3 < base + 7), (ci >> 1).astype(F32),
                         jnp.where(jnp.logical_and(lane3 >= base + 7, lane3 < base + 10),
                                   (ci & 1).astype(F32), 0.0)).astype(BF16)
    step2 = [t[:, None, :] for t in _split3(nsa_row * (2.0 * NSA_CMP_STRIDE))]
    step1 = [t[:, None, :] for t in _split3(nsa_row * (1.0 * NSA_CMP_STRIDE))]
    for k in range(3):
        nsa_srow = jnp.where(rows == base + 4 + k, step2[k].astype(BF16),
                             jnp.where(rows == base + 7 + k, step1[k].astype(BF16), nsa_srow))
    il = jnp.tile(jnp.arange(NSA_TQ), NSA_GROUP)[None, :]
    rel = (jnp.arange(2 * nc) - nc)[:, None]
    nsa_tblc = jnp.where(rel * NSA_CMP_STRIDE + (NSA_CMP_LEN - 1) <= il, 0.0, NEG_BIG).astype(F32)
    dist = (NSA_WINDOW + il - jnp.arange(NSA_WINDOW + WIN_KEYS)[:, None])[None]
    nsa_bw = jnp.where(jnp.logical_and(dist >= 0, dist < NSA_WINDOW),
                       -nsa_row[:, None, :] * dist.astype(F32), NEG_BIG)
    return ((moba_aug, moba_srow, moba_row),
            (nsa_row.reshape(NSA_KV_HEADS, 1, wl), nsa_srow, nsa_auga, nsa_augb, nsa_augc, nsa_tblc, nsa_bw))


def _split3(x):
    hi = x.astype(BF16).astype(F32)
    mid = (x - hi).astype(BF16).astype(F32)
    lo = (x - hi - mid).astype(BF16).astype(F32)
    return hi, mid, lo


def _attention(x, attn_norm_g, w_in, cmp_pe_k, cmp_pe_v, cmp_w1_k, cmp_w2_k, cmp_w1_v, cmp_w2_v):
    B, S, D = x.shape
    wr, wt, wg = _prep_inproj(w_in)
    (mk, kc, vc, ks, kw, mqT, mvT, nqT, vsT, vwT, gT) = _inproj(
        x, attn_norm_g.reshape(1, D), wr, wt, wg, tm=512)
    moba_tabs, nsa_tabs = _attention_tables(S)
    o_moba = _moba(mqT, mk, mvT, *moba_tabs)
    wk, w2k, pek = _prep_compress(cmp_w1_k, cmp_w2_k, cmp_pe_k)
    wv, w2v, pev = _prep_compress(cmp_w1_v, cmp_w2_v, cmp_pe_v)
    nc = S // NSA_CMP_STRIDE
    kcmp, vcmpT = _compress(kc.reshape(B, nc, -1), vc.reshape(B, nc, -1), wk, wv.T, pek, pev, w2k, w2v.T)
    o_nsa = _nsa(nqT, kcmp, vcmpT, ks, vsT, kw, vwT, gT, *nsa_tabs)
    return o_moba, o_nsa


def _moe(x1, hn, e128, w128, w_up, b_up, w_down, b_down, final_norm_g):
    N, D = x1.shape
    rank128, cnt = _ranks(e128)
    counts = cnt[0, :N_EXPERTS].astype(I32)
    padded = (counts + MOE_ROWS - 1) // MOE_ROWS * MOE_ROWS
    pends = jnp.cumsum(padded)
    pstarts = pends - padded
    e4 = e128[:, :TOP_K]
    dest = pstarts[e4] + rank128[:, :TOP_K]
    dest2 = dest.reshape(N // ROUTE_TILE, ROUTE_TILE * TOP_K)
    n_blk = (N * TOP_K + N_EXPERTS * MOE_ROWS + MOE_ROWS - 1) // MOE_ROWS
    P = n_blk * MOE_ROWS
    blk_start = jnp.arange(n_blk, dtype=I32) * MOE_ROWS
    blk_e = jnp.minimum(jnp.sum((pends[None, :] <= blk_start[:, None]).astype(I32), axis=1), N_EXPERTS - 1)
    n_act = (pends[-1:] // MOE_ROWS).astype(I32)
    xrows = _dispatch(dest2, hn, jnp.zeros((P, D), F32))
    wT = jnp.swapaxes(w_up, 1, 2).astype(BF16).reshape(N_EXPERTS, -1, 2, D)
    bg = b_up[:, None, 0::2]
    bu = b_up[:, None, 1::2]
    yrows = _experts(blk_e, n_act, xrows, wT[:, :, 0], wT[:, :, 1], bg, bu,
                     w_down.astype(BF16), b_down[:, None, :])
    return _combine(dest2, x1, w128, final_norm_g.reshape(1, D), yrows)


def kernel(x, attn_norm_g, w_in, cmp_pe_k, cmp_pe_v, cmp_w1_k, cmp_w2_k, cmp_w1_v, cmp_w2_v, w_out, ffn_norm_g, w_router, b_router, w_up, b_up, w_down, b_down, final_norm_g):
    B, S, D = x.shape
    assert attn_norm_g.shape[0] == 1, "single-layer kernel"
    o_moba, o_nsa = _attention(x, attn_norm_g[0], w_in[0], cmp_pe_k[0], cmp_pe_v[0],
                               cmp_w1_k[0], cmp_w2_k[0], cmp_w1_v[0], cmp_w2_v[0])
    N = B * S
    wr = jnp.pad(w_router[0], ((0, 0), (0, LANES - N_EXPERTS)))
    br = jnp.pad(b_router[0], (0, LANES - N_EXPERTS)).reshape(1, LANES)
    x1, hn, e128, w128 = _outproj(o_moba.reshape(N, -1), o_nsa.reshape(N, -1), x.reshape(N, D),
                                  w_out[0].astype(BF16), ffn_norm_g[0].reshape(1, D), wr, br, tm=512)
    out = _moe(x1, hn, e128, w128, w_up[0], b_up[0], w_down[0], b_down[0], final_norm_g)
    return out.reshape(B, S, D)
```

```python
import functools

import jax
import jax.numpy as jnp
import numpy as np
from jax import lax
from jax.experimental import pallas as pl
from jax.experimental.pallas import tpu as pltpu

F32 = jnp.float32
BF16 = jnp.bfloat16
I32 = jnp.int32

HEAD_DIM = 64
MOBA_HEADS = 8
NSA_HEADS = 8
NSA_KV_HEADS = 2
NSA_GROUP = NSA_HEADS // NSA_KV_HEADS
MOBA_BLOCK = 256
MOBA_TOPK = 3
NSA_CMP_LEN = 32
NSA_CMP_STRIDE = 16
NSA_SLC_BLOCK = 64
NSA_SLC_TOPN = 16
NSA_WINDOW = 512
NSA_BRANCHES = 3
N_EXPERTS = 32
TOP_K = 4
SWIGLU_LIMIT = 7.0
SWIGLU_ALPHA = 1.702
RMS_EPS = 1e-5
NEG_BIG = -1e30
LOG2E = 1.4426950408889634

LANES = 128
SUBLANES = 8
VMEM_LIMIT = 56 * 1024 * 1024

NSA_TQ = 128
SLC_TILE = 256
WIN_KEYS = NSA_WINDOW + NSA_TQ
MOE_ROWS = 512
ROUTE_TILE = 256
RANK_TILE = 512
ROW_DMA_UNROLL = 8

NT_DIMS = (((1,), (1,)), ((), ()))


def _params(n_grid):
    return pltpu.CompilerParams(
        dimension_semantics=("arbitrary",) * n_grid,
        vmem_limit_bytes=VMEM_LIMIT,
    )


def _rmsnorm(x, g):
    return x * lax.rsqrt(jnp.mean(x * x, axis=-1, keepdims=True) + RMS_EPS) * g


def _inproj_kernel(x_ref, g_ref, wr_ref, wt_ref, wg_ref,
                   mk_ref, kc_ref, vc_ref, ks_ref, kw_ref,
                   mqT_ref, mvT_ref, nqT_ref, vsT_ref, vwT_ref, gT_ref):
    xn = _rmsnorm(x_ref[0], g_ref[...])
    xb = xn.astype(BF16)
    yr = jnp.dot(xb, wr_ref[...], preferred_element_type=F32)
    mk_ref[0] = yr[:, 0:512].astype(BF16)
    kc_ref[0] = yr[:, 512:640].astype(BF16)
    vc_ref[0] = yr[:, 640:768].astype(BF16)
    ks_ref[0] = yr[:, 768:896].astype(BF16)
    kw_ref[0] = yr[:, 896:1024].astype(BF16)
    yt = lax.dot_general(wt_ref[...], xb, NT_DIMS, preferred_element_type=F32)
    mqT_ref[0] = yt[0:512].astype(BF16)
    mvT_ref[0] = yt[512:1024].astype(BF16)
    nqT_ref[0] = yt[1024:1536].astype(BF16)
    vsT_ref[0] = yt[1536:1664].astype(BF16)
    vwT_ref[0] = yt[1664:1792].astype(BF16)
    gl = lax.dot_general(wg_ref[...], xn, NT_DIMS, precision=lax.Precision.HIGHEST,
                         preferred_element_type=F32)
    gT_ref[0] = jax.nn.sigmoid(gl)


def _inproj(x, g, wr, wt, wg, tm):
    B, S, D = x.shape
    grid = (B, S // tm)
    row = lambda w: pl.BlockSpec((1, tm, w), lambda b, i: (b, i, 0))
    col = lambda h: pl.BlockSpec((1, h, tm), lambda b, i: (b, 0, i))
    full = lambda a: pl.BlockSpec(a.shape, lambda b, i: (0,) * a.ndim)
    out_shape = [
        jax.ShapeDtypeStruct((B, S, 512), BF16),
        jax.ShapeDtypeStruct((B, S, 128), BF16),
        jax.ShapeDtypeStruct((B, S, 128), BF16),
        jax.ShapeDtypeStruct((B, S, 128), BF16),
        jax.ShapeDtypeStruct((B, S, 128), BF16),
        jax.ShapeDtypeStruct((B, 512, S), BF16),
        jax.ShapeDtypeStruct((B, 512, S), BF16),
        jax.ShapeDtypeStruct((B, 512, S), BF16),
        jax.ShapeDtypeStruct((B, 128, S), BF16),
        jax.ShapeDtypeStruct((B, 128, S), BF16),
        jax.ShapeDtypeStruct((B, 32, S), F32),
    ]
    out_specs = [row(512), row(128), row(128), row(128), row(128),
                 col(512), col(512), col(512), col(128), col(128), col(32)]
    return pl.pallas_call(
        _inproj_kernel, grid=grid,
        in_specs=[pl.BlockSpec((1, tm, D), lambda b, i: (b, i, 0)),
                  full(g), full(wr), full(wt), full(wg)],
        out_specs=out_specs, out_shape=out_shape,
        compiler_params=_params(2), name="inproj",
    )(x, g, wr, wt, wg)


def _flash_step(zs, cs, vts, states):
    mts = [jnp.max(z, axis=0, keepdims=True) - c for z, c in zip(zs, cs)]
    m_news = [jnp.maximum(st[0], mt) for st, mt in zip(states, mts)]
    alphas = [jnp.exp2(st[0] - mn) for st, mn in zip(states, m_news)]
    ps = [jnp.exp2(z - (mn + c)) for z, mn, c in zip(zs, m_news, cs)]
    ls = [a * st[1] + jnp.sum(p, axis=0, keepdims=True) for a, st, p in zip(alphas, states, ps)]
    pvs = [jnp.dot(vt, p.astype(BF16), preferred_element_type=F32) for vt, p in zip(vts, ps)]
    accs = [a * st[2] + pv for a, st, pv in zip(alphas, states, pvs)]
    return [(mn, l, acc) for mn, l, acc in zip(m_news, ls, accs)]


def _flash_init(n_q):
    return (jnp.full((1, n_q), NEG_BIG, F32), jnp.zeros((1, n_q), F32),
            jnp.zeros((HEAD_DIM, n_q), F32))


def _softmax_stage(s, c, m, l):
    mt = jnp.max(s, axis=0, keepdims=True) - c
    m_new = jnp.maximum(m, mt)
    alpha = jnp.exp2(m - m_new)
    p = jnp.exp2(s - (m_new + c))
    l_new = alpha * l + jnp.sum(p, axis=0, keepdims=True)
    return m_new, l_new, p.astype(BF16), alpha


def _pipelined_tiles(scores, values, offsets, s_ref, p_ref, first, n_tiles, j_first):
    n_ch = len(first)
    chains = range(n_ch)

    def qk_into(slot, j):
        sc = scores(j)
        for c in chains:
            s_ref[slot, c] = sc[c]

    def pv_from(slot, j, alphas, accs):
        vt = values(j)
        return [alphas[c] * accs[c] + jnp.dot(vt[c], p_ref[slot, c], preferred_element_type=F32)
                for c in chains]

    def softmax_into(slot, j, ms, ls):
        cs = offsets(j)
        new = [_softmax_stage(s_ref[slot, c], cs[c], ms[c], ls[c]) for c in chains]
        for c in chains:
            p_ref[slot, c] = new[c][2]
        return [n[0] for n in new], [n[1] for n in new], [n[3] for n in new]

    for c in chains:
        p_ref[1, c] = first[c][2]
    qk_into(0, 0)

    def pair(i, carry):
        ms, ls, alphas, accs, j_prev = carry
        t = 2 * i
        qk_into(1, t + 1)
        accs = pv_from(1, j_prev, alphas, accs)
        ms, ls, alphas = softmax_into(0, t, ms, ls)
        qk_into(0, t + 2)
        accs = pv_from(0, t, alphas, accs)
        ms, ls, alphas = softmax_into(1, t + 1, ms, ls)
        return ms, ls, alphas, accs, t + 1

    n_q = first[0][0].shape[1]
    init = ([f[0] for f in first], [f[1] for f in first], [f[3] for f in first],
            [jnp.zeros((HEAD_DIM, n_q), F32)] * n_ch, j_first)
    _, ls, alphas, accs, j_last = lax.fori_loop(0, (n_tiles + 1) // 2, pair, init)
    accs = pv_from(1, j_last, alphas, accs)
    return [accs[c] / jnp.maximum(ls[c], 1e-30) for c in chains]


def _moba_kernel(qT_ref, k_ref, vT_ref, aug_ref, srow_ref, sl_ref, o_ref,
                 kmean_ref, s_ref, p_ref, *, nb, nbp, topk):
    qi = pl.program_id(2)
    blk = MOBA_BLOCK

    @pl.when(qi == 0)
    def _():
        kmean_ref[...] = jnp.zeros(kmean_ref.shape, F32)

        def body(n, carry):
            kb = k_ref[0, pl.ds(pl.multiple_of(n * blk, blk), blk), :].astype(F32)
            kmean_ref[pl.ds(n, 1), :] = jnp.mean(kb, axis=0, keepdims=True)
            return carry
        lax.fori_loop(0, nb, body, 0)

    qT = qT_ref[0]
    row = lax.broadcasted_iota(I32, qT.shape, 0)
    qpad = [jnp.where((row >> 6) == h, qT, jnp.zeros_like(qT)) for h in range(2)]

    bidx = lax.broadcasted_iota(I32, (nbp, blk), 0)
    rhs = []
    for h in range(2):
        gate = jnp.dot(kmean_ref[...], qpad[h].astype(F32),
                       precision=lax.Precision.HIGHEST, preferred_element_type=F32)
        gsc = jnp.where(bidx < qi, gate, -jnp.inf)
        bias = jnp.full((nbp, blk), NEG_BIG, F32)
        for _ in range(topk):
            mx = jnp.max(gsc, axis=0, keepdims=True)
            idx = jnp.min(jnp.where(gsc == mx, bidx, nbp), axis=0, keepdims=True)
            pick = jnp.logical_and(bidx == idx, mx > -jnp.inf)
            bias = jnp.where(pick, 0.0, bias)
            gsc = jnp.where(pick, -jnp.inf, gsc)
        pad = jnp.zeros((2 * HEAD_DIM - nbp - 16, blk), BF16)
        rhs.append(jnp.concatenate([qpad[h], bias.astype(BF16), srow_ref[h], pad], axis=0))

    def scores(j, a):
        k0 = pl.multiple_of(j * blk, blk)
        lhs = jnp.concatenate([k_ref[0, pl.ds(k0, blk), :], aug_ref[a]], axis=1)
        return [jnp.dot(lhs, rhs[h], preferred_element_type=F32) for h in range(2)]

    def values(j):
        k0 = pl.multiple_of(j * blk, blk)
        return [vT_ref[0, h * HEAD_DIM:(h + 1) * HEAD_DIM, pl.ds(k0, blk)] for h in range(2)]

    def offsets(j):
        dq = ((qi - j) * blk).astype(F32)
        return [sl_ref[h] * dq for h in range(2)]

    ik = lax.broadcasted_iota(I32, (blk, blk), 0)
    iq = lax.broadcasted_iota(I32, (blk, blk), 1)
    s_own = [jnp.where(ik <= iq, s, NEG_BIG) for s in scores(qi, nb)]
    m0 = jnp.full((1, blk), NEG_BIG, F32)
    l0 = jnp.zeros((1, blk), F32)
    first = [_softmax_stage(s_own[h], jnp.zeros((1, blk), F32), m0, l0) for h in range(2)]

    outs = _pipelined_tiles(lambda j: scores(jnp.minimum(j, nb - 1), jnp.minimum(j, nb - 1)),
                            values, offsets, s_ref, p_ref, first, qi, qi)
    o_ref[0] = jnp.concatenate(outs, axis=0).T.astype(BF16)


def _moba(mqT, mk, mvT, aug, srow, sl):
    B, _, S = mqT.shape
    blk = MOBA_BLOCK
    nb = S // blk
    topk = min(MOBA_TOPK, nb)
    nbp = -(-nb // 16) * 16
    grid = (B, MOBA_HEADS // 2, nb)
    return pl.pallas_call(
        functools.partial(_moba_kernel, nb=nb, nbp=nbp, topk=topk), grid=grid,
        in_specs=[
            pl.BlockSpec((1, 128, blk), lambda b, p, i: (b, p, i)),
            pl.BlockSpec((1, S, 128), lambda b, p, i: (b, 0, p)),
            pl.BlockSpec((1, 128, S), lambda b, p, i: (b, p, 0)),
            pl.BlockSpec(aug.shape, lambda b, p, i: (0, 0, 0)),
            pl.BlockSpec((2, 16, blk), lambda b, p, i: (p, 0, 0)),
            pl.BlockSpec((2, 1, blk), lambda b, p, i: (p, 0, 0)),
        ],
        out_specs=pl.BlockSpec((1, blk, 128), lambda b, p, i: (b, i, p)),
        out_shape=jax.ShapeDtypeStruct((B, S, 512), BF16),
        scratch_shapes=[
            pltpu.VMEM((nbp, 128), F32),
            pltpu.VMEM((2, 2, blk, blk), F32),
            pltpu.VMEM((2, 2, blk, blk), BF16),
        ],
        compiler_params=_params(3), name="moba",
    )(mqT, mk, mvT, aug, srow, sl)


def _compress_kernel(kc_ref, vc_ref, wk_ref, wvT_ref, pek_ref, pev_ref, w2k_ref, w2vT_ref,
                     kcmp_ref, vcmpT_ref):
    nc = kc_ref.shape[1]

    wk = wk_ref[...]
    ab = jnp.dot(kc_ref[0], wk, preferred_element_type=F32)
    pt = (jnp.dot(pek_ref[0], wk[:, 0:128].astype(F32), preferred_element_type=F32)
          + jnp.dot(pek_ref[1], wk[:, 128:256].astype(F32), preferred_element_type=F32))
    pre = ab[:, 0:128] + pltpu.roll(ab[:, 128:256], nc - 1, 0) + pt[0:1]
    hid = jax.nn.gelu(pre)
    kcmp_ref[0] = jnp.dot(hid.astype(BF16), w2k_ref[...], preferred_element_type=F32).astype(BF16)

    wvT = wvT_ref[...]
    abT = lax.dot_general(wvT, vc_ref[0], NT_DIMS, preferred_element_type=F32)
    ptT = (lax.dot_general(wvT[0:128].astype(F32), pev_ref[0], NT_DIMS, preferred_element_type=F32)
           + lax.dot_general(wvT[128:256].astype(F32), pev_ref[1], NT_DIMS, preferred_element_type=F32))
    preT = abT[0:128] + pltpu.roll(abT[128:256], nc - 1, 1) + ptT[:, 0:1]
    hidT = jax.nn.gelu(preT)
    vcmpT_ref[0] = jnp.dot(w2vT_ref[...], hidT.astype(BF16), preferred_element_type=F32).astype(BF16)


def _compress(kc2, vc2, wk, wvT, pek, pev, w2k, w2vT):
    B, nc, _ = kc2.shape
    full = lambda a: pl.BlockSpec(a.shape, lambda b: (0,) * a.ndim)
    blk = pl.BlockSpec((1, nc, kc2.shape[2]), lambda b: (b, 0, 0))
    return pl.pallas_call(
        _compress_kernel, grid=(B,),
        in_specs=[blk, blk, full(wk), full(wvT), full(pek), full(pev), full(w2k), full(w2vT)],
        out_specs=[pl.BlockSpec((1, nc, 128), lambda b: (b, 0, 0)),
                   pl.BlockSpec((1, 128, nc), lambda b: (b, 0, 0))],
        out_shape=[jax.ShapeDtypeStruct((B, nc, 128), BF16),
                   jax.ShapeDtypeStruct((B, 128, nc), BF16)],
        compiler_params=_params(1), name="nsa_compress",
    )(kc2, vc2, wk, wvT, pek, pev, w2k, w2vT)


def _nsa_kernel(qT_ref, kcmp_ref, vcmpT_ref, ks_ref, vsT_ref, kw_ref, vwT_ref,
                g_ref, sl_ref, srow_ref, auga_ref, augb_ref, augc_ref, tblc_ref, bw_ref, o_ref,
                s_ref, p_ref, pc_ref, *, n_slc, topn):
    g = pl.program_id(1)
    qi = pl.program_id(2)
    tq = NSA_TQ
    hg = NSA_GROUP
    wl = hg * tq
    q0 = qi * tq

    q4 = qT_ref[0]
    qT = jnp.concatenate([q4[h * HEAD_DIM:(h + 1) * HEAD_DIM] for h in range(hg)], axis=1)
    qT2 = jnp.concatenate([qT, qT], axis=0)
    rowi = lax.broadcasted_iota(I32, qT2.shape, 0)
    qpad = jnp.where((rowi >> 6) == g, qT2, jnp.zeros_like(qT2))
    slope = sl_ref[0]
    lane = lax.broadcasted_iota(I32, (1, wl), 1)
    t_q = q0 + (lane & (tq - 1))

    nc = kcmp_ref.shape[1]
    rhs_top = jnp.where((rowi >> 6) == g, qT2, srow_ref[0])
    mine_c = (lax.broadcasted_iota(I32, (nc, LANES), 1) >> 6) == g
    lhs_c = jnp.where(mine_c, kcmp_ref[0], augc_ref[0])
    first_c = pl.multiple_of(nc - qi * (tq // NSA_CMP_STRIDE), 8)
    z = jnp.dot(lhs_c, rhs_top, preferred_element_type=F32) + tblc_ref[pl.ds(first_c, nc), :]
    mx = jnp.max(z, axis=0, keepdims=True)
    e = jnp.exp2(z - mx)
    den = jnp.maximum(jnp.sum(e, axis=0, keepdims=True), 1e-30)
    p = e * jnp.where(t_q >= NSA_CMP_LEN - 1, 1.0 / den, 0.0)
    o_c = jnp.dot(vcmpT_ref[0], p.astype(BF16), preferred_element_type=F32)

    pc = p[:, 0:tq]
    for h in range(1, hg):
        pc = pc + p[:, h * tq:(h + 1) * tq]
    pc_ref[...] = pc
    su = NSA_SLC_BLOCK // NSA_CMP_STRIDE
    x = [pc_ref[pl.ds(k, n_slc, stride=su), :] for k in range(su)]
    jb = lax.broadcasted_iota(I32, (n_slc, tq), 0)
    prev = jnp.where(jb == 0, 0.0, pltpu.roll(x[3], 1, 0))
    imp = 2.0 * (x[0] + x[1] + x[2]) + x[3] + prev
    cur = (q0 + lax.broadcasted_iota(I32, (1, tq), 1)) >> 6
    allowed = jb <= cur
    forced = jnp.logical_or(jb == 0, jnp.logical_or(jb == cur, jb == cur - 1))
    bias = jnp.where(jnp.logical_and(allowed, forced), 0.0, NEG_BIG)
    sc = jnp.where(jnp.logical_and(allowed, jnp.logical_not(forced)), imp, -1.0)
    for _ in range(topn - 3):
        smx = jnp.max(sc, axis=0, keepdims=True)
        idx = jnp.min(jnp.where(sc == smx, jb, n_slc), axis=0, keepdims=True)
        pick = jnp.logical_and(jb == idx, smx >= 0.0)
        bias = jnp.where(pick, 0.0, bias)
        sc = jnp.where(pick, -1.0, sc)

    n_ch = 2
    wc = wl // n_ch

    def lane_split(a):
        return [a[:, c * wc:(c + 1) * wc] for c in range(n_ch)]

    if n_slc < LANES:
        bias = jnp.concatenate([bias, jnp.zeros((LANES - n_slc, tq), F32)], axis=0)
    bias4 = jnp.concatenate([bias.astype(BF16)] * hg, axis=1)
    rhs = jnp.concatenate([rhs_top, bias4], axis=0)
    mine = (lax.broadcasted_iota(I32, (SLC_TILE, LANES), 1) >> 6) == g
    n_tiles = augb_ref.shape[0]

    def scores(j, null):
        k0 = pl.multiple_of(j * SLC_TILE, SLC_TILE)
        kt = ks_ref[0, pl.ds(k0, SLC_TILE), :]
        lhs = jnp.concatenate([jnp.where(mine, kt, auga_ref[null]), augb_ref[j]], axis=1)
        return jnp.dot(lhs, rhs, preferred_element_type=F32)

    def values(j):
        return [vsT_ref[0, :, pl.ds(pl.multiple_of(j * SLC_TILE, SLC_TILE), SLC_TILE)]] * n_ch

    def offsets(j):
        return lane_split(slope * (q0 - j * SLC_TILE).astype(F32))

    jd = lax.div(q0, SLC_TILE)
    t_k = jd * SLC_TILE + lax.broadcasted_iota(I32, (SLC_TILE, wl), 0)
    s_diag = lane_split(jnp.where(t_k <= t_q, scores(jd, 0), NEG_BIG))
    c_diag = offsets(jd)
    first = [_softmax_stage(s_diag[c], c_diag[c], jnp.full((1, wc), NEG_BIG, F32),
                            jnp.zeros((1, wc), F32)) for c in range(n_ch)]

    def past_scores(j):
        return lane_split(scores(jnp.minimum(j, n_tiles - 1), (j >= jd).astype(I32)))
    o_s = jnp.concatenate(
        _pipelined_tiles(past_scores, values, offsets, s_ref, p_ref, first, jd, jd), axis=1)

    start = pl.multiple_of(jnp.maximum(q0 - NSA_WINDOW, 0), tq)
    first_w = pl.multiple_of(NSA_WINDOW - (q0 - start), tq)
    kt = kw_ref[0, pl.ds(start, WIN_KEYS), :]
    z = jnp.dot(kt, qpad, preferred_element_type=F32) + bw_ref[0, pl.ds(first_w, WIN_KEYS), :]
    mx = jnp.max(z, axis=0, keepdims=True)
    p = jnp.exp2(z - mx)
    den = jnp.maximum(jnp.sum(p, axis=0, keepdims=True), 1e-30)
    o_w = jnp.dot(vwT_ref[0, :, pl.ds(start, WIN_KEYS)], p.astype(BF16),
                  preferred_element_type=F32) / den

    gt = g_ref[0]

    def gate_row(br):
        return jnp.concatenate([gt[br * hg + h:br * hg + h + 1] for h in range(hg)], axis=1)

    o = gate_row(0) * o_c + gate_row(1) * o_s + gate_row(2) * o_w
    o4 = jnp.concatenate([o[:, h * tq:(h + 1) * tq] for h in range(hg)], axis=0)
    o_ref[0] = o4.T.astype(BF16)


def _nsa(nqT, kcmp, vcmpT, ks, vsT, kw, vwT, gT, sl, srow, auga, augb, augc, tblc, bw):
    B, _, S = nqT.shape
    tq = NSA_TQ
    nc = kcmp.shape[1]
    n_slc = S // NSA_SLC_BLOCK
    topn = min(NSA_SLC_TOPN, n_slc)
    wl = NSA_GROUP * tq
    grid = (B, NSA_KV_HEADS, S // tq)
    return pl.pallas_call(
        functools.partial(_nsa_kernel, n_slc=n_slc, topn=topn), grid=grid,
        in_specs=[
            pl.BlockSpec((1, NSA_GROUP * HEAD_DIM, tq), lambda b, g, i: (b, g, i)),
            pl.BlockSpec((1, nc, 128), lambda b, g, i: (b, 0, 0)),
            pl.BlockSpec((1, HEAD_DIM, nc), lambda b, g, i: (b, g, 0)),
            pl.BlockSpec((1, S, 128), lambda b, g, i: (b, 0, 0)),
            pl.BlockSpec((1, HEAD_DIM, S), lambda b, g, i: (b, g, 0)),
            pl.BlockSpec((1, S, 128), lambda b, g, i: (b, 0, 0)),
            pl.BlockSpec((1, HEAD_DIM, S), lambda b, g, i: (b, g, 0)),
            pl.BlockSpec((1, 16, tq), lambda b, g, i: (b, g, i)),
            pl.BlockSpec((1, 1, wl), lambda b, g, i: (g, 0, 0)),
            pl.BlockSpec((1, 2 * HEAD_DIM, wl), lambda b, g, i: (g, 0, 0)),
            pl.BlockSpec((2, SLC_TILE, LANES), lambda b, g, i: (g, 0, 0)),
            pl.BlockSpec(augb.shape, lambda b, g, i: (0, 0, 0)),
            pl.BlockSpec((1, nc, LANES), lambda b, g, i: (g, 0, 0)),
            pl.BlockSpec(tblc.shape, lambda b, g, i: (0, 0)),
            pl.BlockSpec((1,) + bw.shape[1:], lambda b, g, i: (g, 0, 0)),
        ],
        out_specs=pl.BlockSpec((1, tq, NSA_GROUP * HEAD_DIM), lambda b, g, i: (b, i, g)),
        out_shape=jax.ShapeDtypeStruct((B, S, 512), BF16),
        scratch_shapes=[
            pltpu.VMEM((2, 2, SLC_TILE, wl // 2), F32),
            pltpu.VMEM((2, 2, SLC_TILE, wl // 2), BF16),
            pltpu.VMEM((nc, tq), F32),
        ],
        compiler_params=_params(3), name="nsa",
    )(nqT, kcmp, vcmpT, ks, vsT, kw, vwT, gT, sl, srow, auga, augb, augc, tblc, bw)


def _outproj_kernel(om_ref, on_ref, x_ref, wo_ref, g_ref, wr_ref, br_ref,
                    x1_ref, hn_ref, e_ref, w_ref):
    attn = (jnp.dot(om_ref[...], wo_ref[0:512, :], preferred_element_type=F32)
            + jnp.dot(on_ref[...], wo_ref[512:1024, :], preferred_element_type=F32))
    x1 = x_ref[...] + attn
    x1_ref[...] = x1
    hn = _rmsnorm(x1, g_ref[...])
    _store_token_tiles(hn_ref, hn)
    logits = jnp.dot(hn, wr_ref[...], precision=lax.Precision.HIGHEST,
                     preferred_element_type=F32) + br_ref[...]
    tm = logits.shape[0]
    lane = lax.broadcasted_iota(I32, (tm, LANES), 1)
    sc = jnp.where(lane < N_EXPERTS, logits, -jnp.inf)
    e_out = jnp.zeros((tm, LANES), I32)
    vals = []
    for k in range(TOP_K):
        mx = jnp.max(sc, axis=1, keepdims=True)
        idx = jnp.min(jnp.where(sc == mx, lane, LANES), axis=1, keepdims=True)
        e_out = jnp.where(lane == k, idx, e_out)
        sc = jnp.where(lane == idx, -jnp.inf, sc)
        vals.append(mx)
    ex = [jnp.exp(v - vals[0]) for v in vals]
    den = ex[0] + ex[1] + ex[2] + ex[3]
    w_out = jnp.zeros((tm, LANES), F32)
    for k in range(TOP_K):
        w_out = jnp.where(lane == k, ex[k] / den, w_out)
    e_ref[...] = e_out
    w_ref[...] = w_out


def _outproj(om, on, x, wo, g, wr, br, tm):
    N, D = x.shape
    full = lambda a: pl.BlockSpec(a.shape, lambda i: (0,) * a.ndim)
    row = lambda w: pl.BlockSpec((tm, w), lambda i: (i, 0))
    return pl.pallas_call(
        _outproj_kernel, grid=(N // tm,),
        in_specs=[row(512), row(512), row(D), full(wo), full(g), full(wr), full(br)],
        out_specs=[row(D), pl.BlockSpec((tm * SUBLANES, LANES), lambda i: (i, 0)), row(LANES), row(LANES)],
        out_shape=[jax.ShapeDtypeStruct((N, D), F32), jax.ShapeDtypeStruct((N * SUBLANES, LANES), F32),
                   jax.ShapeDtypeStruct((N, LANES), I32), jax.ShapeDtypeStruct((N, LANES), F32)],
        compiler_params=_params(1), name="outproj_router",
    )(om, on, x, wo, g, wr, br)


def _rank_kernel(e_ref, rank_ref, cnt_ref, base_ref):
    i = pl.program_id(0)
    T = e_ref.shape[0]

    @pl.when(i == 0)
    def _():
        base_ref[...] = jnp.zeros(base_ref.shape, F32)

    e = e_ref[...]
    lane = lax.broadcasted_iota(I32, (T, LANES), 1)
    tril = jnp.where(lax.broadcasted_iota(I32, (T, T), 0) >= lax.broadcasted_iota(I32, (T, T), 1),
                     1.0, 0.0).astype(BF16)
    out = jnp.zeros((T, LANES), I32)
    for k in range(TOP_K):
        hit = lane == e[:, k:k + 1]
        oh = jnp.where(hit, 1.0, 0.0)
        cum = jnp.dot(tril, oh.astype(BF16), preferred_element_type=F32)
        base = base_ref[0:1, :]
        r = jnp.sum(jnp.where(hit, cum - 1.0 + base, 0.0), axis=1, keepdims=True)
        out = jnp.where(lane == k, r.astype(I32), out)
        base_ref[...] = base_ref[...] + jnp.sum(oh, axis=0, keepdims=True)
    rank_ref[...] = out
    cnt_ref[...] = base_ref[...]


def _ranks(e128):
    N = e128.shape[0]
    T = RANK_TILE
    return pl.pallas_call(
        _rank_kernel, grid=(N // T,),
        in_specs=[pl.BlockSpec((T, LANES), lambda i: (i, 0))],
        out_specs=[pl.BlockSpec((T, LANES), lambda i: (i, 0)),
                   pl.BlockSpec((8, LANES), lambda i: (0, 0))],
        out_shape=[jax.ShapeDtypeStruct((N, LANES), I32),
                   jax.ShapeDtypeStruct((8, LANES), F32)],
        scratch_shapes=[pltpu.VMEM((8, LANES), F32)],
        compiler_params=_params(1), name="route_ranks",
    )(e128)


def _row_copy(src, dst, i_src, i_dst, sem):
    return pltpu.make_async_copy(src.at[pl.ds(pl.multiple_of(i_src * SUBLANES, SUBLANES), SUBLANES)],
                                 dst.at[pl.ds(pl.multiple_of(i_dst * SUBLANES, SUBLANES), SUBLANES)], sem)


def _store_token_tiles(ref, x):
    rows = x.shape[0]
    for c in range(SUBLANES):
        ref[pl.ds(c, rows, stride=SUBLANES), :] = x[:, c * LANES:(c + 1) * LANES]


def _load_token_tiles(ref, rows):
    return jnp.concatenate([ref[pl.ds(c, rows, stride=SUBLANES), :] for c in range(SUBLANES)], axis=1)


def _dispatch_kernel(dest_hbm, hp_ref, xz_hbm, out_hbm, idx_ref, isem, sem):
    del xz_hbm
    i = pl.program_id(0)
    T = ROUTE_TILE
    cp = pltpu.make_async_copy(dest_hbm.at[i], idx_ref, isem)
    cp.start()
    cp.wait()

    def issue(t, carry):
        for k in range(TOP_K):
            _row_copy(hp_ref, out_hbm, t, idx_ref[t * TOP_K + k], sem).start()
        return carry
    lax.fori_loop(0, T, issue, 0, unroll=ROW_DMA_UNROLL)

    def drain(t, carry):
        for k in range(TOP_K):
            _row_copy(hp_ref, out_hbm, 0, 0, sem).wait()
        return carry
    lax.fori_loop(0, T, drain, 0, unroll=ROW_DMA_UNROLL)


def _dispatch(dest2, hp, xzero):
    nsteps = dest2.shape[0]
    T = ROUTE_TILE
    return pl.pallas_call(
        _dispatch_kernel, grid=(nsteps,),
        in_specs=[pl.BlockSpec(memory_space=pl.ANY),
                  pl.BlockSpec((T * SUBLANES, LANES), lambda i: (i, 0)),
                  pl.BlockSpec(memory_space=pl.ANY)],
        out_specs=pl.BlockSpec(memory_space=pl.ANY),
        out_shape=jax.ShapeDtypeStruct(xzero.shape, xzero.dtype),
        scratch_shapes=[pltpu.SMEM((T * TOP_K,), I32),
                        pltpu.SemaphoreType.DMA, pltpu.SemaphoreType.DMA],
        input_output_aliases={2: 0},
        compiler_params=_params(1), name="moe_dispatch",
    )(dest2, hp, xzero)


def _expert_kernel(be_ref, na_ref, x_ref, wg_ref, wu_ref, bg_ref, bu_ref, wd_ref, bd_ref, y_ref):
    b = pl.program_id(0)
    active = b < na_ref[0]

    @pl.when(active)
    def _():
        xb = _load_token_tiles(x_ref, MOE_ROWS).astype(BF16)
        gg = lax.dot_general(xb, wg_ref[0], NT_DIMS, preferred_element_type=F32) + bg_ref[0]
        uu = lax.dot_general(xb, wu_ref[0], NT_DIMS, preferred_element_type=F32) + bu_ref[0]
        gg = jnp.minimum(gg, SWIGLU_LIMIT)
        uu = jnp.clip(uu, -SWIGLU_LIMIT, SWIGLU_LIMIT)
        a = gg * jax.nn.sigmoid(SWIGLU_ALPHA * gg) * (uu + 1.0)
        _store_token_tiles(y_ref, jnp.dot(a.astype(BF16), wd_ref[0], preferred_element_type=F32) + bd_ref[0])

    @pl.when(jnp.logical_not(active))
    def _():
        y_ref[...] = jnp.zeros(y_ref.shape, F32)


def _experts(blk_e, n_act, xrows, wgT, wuT, bg, bu, wd, bd):
    _, F, D = wgT.shape
    assert D == SUBLANES * LANES
    P = xrows.shape[0] // SUBLANES
    n_blk = P // MOE_ROWS
    rows_spec = pl.BlockSpec((MOE_ROWS * SUBLANES, LANES), lambda b, be, na: (b, 0))
    wspec = lambda r, c: pl.BlockSpec((1, r, c), lambda b, be, na: (be[b], 0, 0))
    grid_spec = pltpu.PrefetchScalarGridSpec(
        num_scalar_prefetch=2, grid=(n_blk,),
        in_specs=[rows_spec, wspec(F, D), wspec(F, D), wspec(1, F), wspec(1, F), wspec(F, D), wspec(1, D)],
        out_specs=rows_spec,
    )
    return pl.pallas_call(
        _expert_kernel, grid_spec=grid_spec,
        out_shape=jax.ShapeDtypeStruct(xrows.shape, F32),
        compiler_params=_params(1), name="moe_experts",
    )(blk_e, n_act, xrows, wgT, wuT, bg, bu, wd, bd)


def _combine_kernel(dest_hbm, x1_ref, w_ref, g_ref, y_hbm, o_ref, idx_ref, buf_ref, isem, sem):
    i = pl.program_id(0)
    T = ROUTE_TILE
    cp = pltpu.make_async_copy(dest_hbm.at[i], idx_ref, isem)
    cp.start()
    cp.wait()

    def issue(t, carry):
        for k in range(TOP_K):
            _row_copy(y_hbm, buf_ref.at[k], idx_ref[t * TOP_K + k], t, sem).start()
        return carry
    lax.fori_loop(0, T, issue, 0, unroll=ROW_DMA_UNROLL)

    def drain(t, carry):
        for k in range(TOP_K):
            _row_copy(y_hbm, buf_ref.at[k], 0, 0, sem).wait()
        return carry
    lax.fori_loop(0, T, drain, 0, unroll=ROW_DMA_UNROLL)

    x1 = x1_ref[...]
    w = w_ref[...]
    cols = []
    for c in range(SUBLANES):
        acc = x1[:, c * LANES:(c + 1) * LANES]
        for k in range(TOP_K):
            acc = acc + w[:, k:k + 1] * buf_ref[k, pl.ds(c, T, stride=SUBLANES), :]
        cols.append(acc)
    o_ref[...] = _rmsnorm(jnp.concatenate(cols, axis=1), g_ref[...])


def _combine(dest2, x1, w128, g, yrows):
    N, D = x1.shape
    T = ROUTE_TILE
    return pl.pallas_call(
        _combine_kernel, grid=(N // T,),
        in_specs=[pl.BlockSpec(memory_space=pl.ANY),
                  pl.BlockSpec((T, D), lambda i: (i, 0)),
                  pl.BlockSpec((T, LANES), lambda i: (i, 0)),
                  pl.BlockSpec(g.shape, lambda i: (0, 0)),
                  pl.BlockSpec(memory_space=pl.ANY)],
        out_specs=pl.BlockSpec((T, D), lambda i: (i, 0)),
        out_shape=jax.ShapeDtypeStruct((N, D), F32),
        scratch_shapes=[pltpu.SMEM((T * TOP_K,), I32),
                        pltpu.VMEM((TOP_K, T * SUBLANES, LANES), F32),
                        pltpu.SemaphoreType.DMA, pltpu.SemaphoreType.DMA],
        compiler_params=_params(1), name="moe_combine",
    )(dest2, x1, w128, g, yrows)


def _alibi_slopes():
    n = MOBA_HEADS + NSA_HEADS
    s = jnp.exp2(-8.0 * jnp.arange(1, n + 1, dtype=F32) / n)
    return s[0::2], s[1::2]


def _prep_inproj(w_in):
    hd = HEAD_DIM
    sizes = [MOBA_HEADS * hd] * 3 + [NSA_HEADS * hd] + [NSA_KV_HEADS * hd] * 6 + [NSA_BRANCHES * NSA_HEADS]
    cuts = np.cumsum([0] + sizes)
    mq, mk, mv, nq, kc, vc, ks, vs, kw, vw, ng = [w_in[:, cuts[i]:cuts[i + 1]] for i in range(11)]
    qscale = (hd ** -0.5) * LOG2E
    wr = jnp.concatenate([mk, kc, vc, ks, kw], axis=1).astype(BF16)
    wt = jnp.concatenate([mq * qscale, mv, nq * qscale, vs, vw], axis=1).T.astype(BF16)
    ngr = ng.reshape(-1, NSA_KV_HEADS, NSA_GROUP, NSA_BRANCHES).transpose(1, 3, 2, 0)
    ngr = ngr.reshape(NSA_KV_HEADS, NSA_BRANCHES * NSA_GROUP, -1)
    wg = jnp.pad(ngr, ((0, 0), (0, 16 - NSA_BRANCHES * NSA_GROUP), (0, 0))).reshape(32, -1)
    return wr, wt, wg.astype(F32)


def _prep_compress(w1, w2, pe):
    hd, half = HEAD_DIM, NSA_CMP_STRIDE
    w1r = w1.reshape(2, half, hd, hd)
    eye = jnp.eye(NSA_KV_HEADS, dtype=w1.dtype)
    w = jnp.einsum('alde,gh->lgdahe', w1r, eye).reshape(half * NSA_KV_HEADS * hd, 2 * NSA_KV_HEADS * hd)
    w2b = jnp.einsum('de,gh->gdhe', w2, eye).reshape(NSA_KV_HEADS * hd, NSA_KV_HEADS * hd)
    per = pe.reshape(2, half, 1, hd)
    pe2 = jnp.broadcast_to(per, (2, half, NSA_KV_HEADS, hd)).reshape(2, 1, half * NSA_KV_HEADS * hd)
    pe2 = jnp.broadcast_to(pe2, (2, 8, pe2.shape[2]))
    return w.astype(BF16), w2b.astype(BF16), pe2.astype(F32)


def _attention_tables(S):
    moba_sl, nsa_sl = _alibi_slopes()
    moba_sl = moba_sl * LOG2E
    nsa_sl = nsa_sl * LOG2E
    blk = MOBA_BLOCK
    moba_row = jnp.broadcast_to(moba_sl[:, None, None], (MOBA_HEADS, 1, blk))
    nb = S // blk
    nbp = -(-nb // 16) * 16
    col = jnp.arange(LANES)[None, None, :]
    tile = jnp.arange(nb + 1)[:, None, None]
    off = jnp.arange(blk, dtype=F32)[None, :, None]
    moba_aug = jnp.where(jnp.logical_and(col == tile, tile < nb), 1.0,
                         jnp.where(jnp.logical_and(col >= nbp, col < nbp + 3), off, 0.0)).astype(BF16)
    parts = jnp.stack(list(_split3(moba_sl)) + [jnp.zeros_like(moba_sl)] * 13, axis=1)
    moba_srow = jnp.broadcast_to(parts[:, :, None], (MOBA_HEADS, 16, blk)).astype(BF16)

    wl = NSA_GROUP * NSA_TQ
    n_slc = S // NSA_SLC_BLOCK
    assert n_slc <= LANES, "block-choice rows must fit the spare contraction rows"
    nsa_row = jnp.repeat(nsa_sl.reshape(NSA_KV_HEADS, NSA_GROUP), NSA_TQ, axis=1)
    hi, mid, lo = [t[:, None, :] for t in _split3(nsa_row)]
    base = ((1 - jnp.arange(NSA_KV_HEADS)) * HEAD_DIM)[:, None, None]
    rows = jnp.arange(2 * HEAD_DIM)[None, :, None]
    nsa_srow = jnp.where(rows == base, hi, jnp.where(rows == base + 1, mid, jnp.where(
        rows == base + 2, lo, jnp.where(rows == base + 3, NEG_BIG, 0.0)))).astype(BF16)
    lane = jnp.arange(LANES)[None, None, None, :]
    base4 = base[:, None]
    null = jnp.arange(2, dtype=F32)[None, :, None, None]
    koff = jnp.arange(SLC_TILE, dtype=F32)[None, None, :, None]
    nsa_auga = jnp.where(jnp.logical_and(lane >= base4, lane < base4 + 3), koff,
                         jnp.where(lane == base4 + 3, null, 0.0))
    nsa_auga = nsa_auga.reshape(2 * NSA_KV_HEADS, SLC_TILE, LANES).astype(BF16)
    per = SLC_TILE // NSA_SLC_BLOCK
    tile = jnp.arange(S // SLC_TILE)[:, None, None]
    blk_of = tile * per + jnp.arange(SLC_TILE)[None, :, None] // NSA_SLC_BLOCK
    nsa_augb = (jnp.arange(LANES)[None, None, :] == blk_of).astype(BF16)

    nc = S // NSA_CMP_STRIDE
    ci = jnp.arange(nc)[None, :, None]
    lane3 = jnp.arange(LANES)[None, None, :]
    nsa_augc = jnp.where(jnp.logical_and(lane3 >= base + 4, lane3 < base + 7), (ci >> 1).astype(F32),
                         jnp.where(jnp.logical_and(lane3 >= base + 7, lane3 < base + 10),
                                   (ci & 1).astype(F32), 0.0)).astype(BF16)
    step2 = [t[:, None, :] for t in _split3(nsa_row * (2.0 * NSA_CMP_STRIDE))]
    step1 = [t[:, None, :] for t in _split3(nsa_row * (1.0 * NSA_CMP_STRIDE))]
    for k in range(3):
        nsa_srow = jnp.where(rows == base + 4 + k, step2[k].astype(BF16),
                             jnp.where(rows == base + 7 + k, step1[k].astype(BF16), nsa_srow))
    il = jnp.tile(jnp.arange(NSA_TQ), NSA_GROUP)[None, :]
    rel = (jnp.arange(2 * nc) - nc)[:, None]
    nsa_tblc = jnp.where(rel * NSA_CMP_STRIDE + (NSA_CMP_LEN - 1) <= il, 0.0, NEG_BIG).astype(F32)
    dist = (NSA_WINDOW + il - jnp.arange(NSA_WINDOW + WIN_KEYS)[:, None])[None]
    nsa_bw = jnp.where(jnp.logical_and(dist >= 0, dist < NSA_WINDOW),
                       -nsa_row[:, None, :] * dist.astype(F32), NEG_BIG)
    return ((moba_aug, moba_srow, moba_row),
            (nsa_row.reshape(NSA_KV_HEADS, 1, wl), nsa_srow, nsa_auga, nsa_augb, nsa_augc, nsa_tblc, nsa_bw))


def _split3(x):
    hi = x.astype(BF16).astype(F32)
    mid = (x - hi).astype(BF16).astype(F32)
    lo = (x - hi - mid).astype(BF16).astype(F32)
    return hi, mid, lo


def _attention(x, attn_norm_g, w_in, cmp_pe_k, cmp_pe_v, cmp_w1_k, cmp_w2_k, cmp_w1_v, cmp_w2_v):
    B, S, D = x.shape
    wr, wt, wg = _prep_inproj(w_in)
    (mk, kc, vc, ks, kw, mqT, mvT, nqT, vsT, vwT, gT) = _inproj(
        x, attn_norm_g.reshape(1, D), wr, wt, wg, tm=512)
    moba_tabs, nsa_tabs = _attention_tables(S)
    o_moba = _moba(mqT, mk, mvT, *moba_tabs)
    wk, w2k, pek = _prep_compress(cmp_w1_k, cmp_w2_k, cmp_pe_k)
    wv, w2v, pev = _prep_compress(cmp_w1_v, cmp_w2_v, cmp_pe_v)
    nc = S // NSA_CMP_STRIDE
    kcmp, vcmpT = _compress(kc.reshape(B, nc, -1), vc.reshape(B, nc, -1), wk, wv.T, pek, pev, w2k, w2v.T)
    o_nsa = _nsa(nqT, kcmp, vcmpT, ks, vsT, kw, vwT, gT, *nsa_tabs)
    return o_moba, o_nsa


def _moe(x1, hn, e128, w128, w_up, b_up, w_down, b_down, final_norm_g):
    N, D = x1.shape
    rank128, cnt = _ranks(e128)
    counts = cnt[0, :N_EXPERTS].astype(I32)
    padded = (counts + MOE_ROWS - 1) // MOE_ROWS * MOE_ROWS
    pends = jnp.cumsum(padded)
    pstarts = pends - padded
    e4 = e128[:, :TOP_K]
    dest = pstarts[e4] + rank128[:, :TOP_K]
    dest2 = dest.reshape(N // ROUTE_TILE, ROUTE_TILE * TOP_K)
    n_blk = (N * TOP_K + N_EXPERTS * MOE_ROWS + MOE_ROWS - 1) // MOE_ROWS
    P = n_blk * MOE_ROWS
    blk_start = jnp.arange(n_blk, dtype=I32) * MOE_ROWS
    blk_e = jnp.minimum(jnp.sum((pends[None, :] <= blk_start[:, None]).astype(I32), axis=1), N_EXPERTS - 1)
    n_act = (pends[-1:] // MOE_ROWS).astype(I32)
    xrows = _dispatch(dest2, hn, jnp.zeros((P * SUBLANES, LANES), F32))
    wT = jnp.swapaxes(w_up, 1, 2).astype(BF16).reshape(N_EXPERTS, -1, 2, D)
    bg = b_up[:, None, 0::2]
    bu = b_up[:, None, 1::2]
    yrows = _experts(blk_e, n_act, xrows, wT[:, :, 0], wT[:, :, 1], bg, bu,
                     w_down.astype(BF16), b_down[:, None, :])
    return _combine(dest2, x1, w128, final_norm_g.reshape(1, D), yrows)


def kernel(x, attn_norm_g, w_in, cmp_pe_k, cmp_pe_v, cmp_w1_k, cmp_w2_k, cmp_w1_v, cmp_w2_v, w_out, ffn_norm_g, w_router, b_router, w_up, b_up, w_down, b_down, final_norm_g):
    B, S, D = x.shape
    assert attn_norm_g.shape[0] == 1, "single-layer kernel"
    o_moba, o_nsa = _attention(x, attn_norm_g[0], w_in[0], cmp_pe_k[0], cmp_pe_v[0],
                               cmp_w1_k[0], cmp_w2_k[0], cmp_w1_v[0], cmp_w2_v[0])
    N = B * S
    wr = jnp.pad(w_router[0], ((0, 0), (0, LANES - N_EXPERTS)))
    br = jnp.pad(b_router[0], (0, LANES - N_EXPERTS)).reshape(1, LANES)
    x1, hn, e128, w128 = _outproj(o_moba.reshape(N, -1), o_nsa.reshape(N, -1), x.reshape(N, D),
                                  w_out[0].astype(BF16), ffn_norm_g[0].reshape(1, D), wr, br, tm=512)
    out = _moe(x1, hn, e128, w128, w_up[0], b_up[0], w_down[0], b_down[0], final_norm_g)
    return out.reshape(B, S, D)
```

```python
import functools

import jax
import jax.numpy as jnp
import numpy as np
from jax import lax
from jax.experimental import pallas as pl
from jax.experimental.pallas import tpu as pltpu

F32 = jnp.float32
BF16 = jnp.bfloat16
I32 = jnp.int32

HEAD_DIM = 64
MOBA_HEADS = 8
NSA_HEADS = 8
NSA_KV_HEADS = 2
NSA_GROUP = NSA_HEADS // NSA_KV_HEADS
MOBA_BLOCK = 256
MOBA_TOPK = 3
NSA_CMP_LEN = 32
NSA_CMP_STRIDE = 16
NSA_SLC_BLOCK = 64
NSA_SLC_TOPN = 16
NSA_WINDOW = 512
NSA_BRANCHES = 3
N_EXPERTS = 32
TOP_K = 4
SWIGLU_LIMIT = 7.0
SWIGLU_ALPHA = 1.702
RMS_EPS = 1e-5
NEG_BIG = -1e30
LOG2E = 1.4426950408889634

LANES = 128
SUBLANES = 8
VMEM_LIMIT = 56 * 1024 * 1024

NSA_TQ = 256
NSA_CHAIN_LANES = 256
SLC_TILE = 256
WIN_KEYS = NSA_WINDOW + NSA_TQ
MOE_ROWS = 512
ROUTE_TILE = 256
RANK_TILE = 512
ROW_DMA_UNROLL = 8

NT_DIMS = (((1,), (1,)), ((), ()))


def _params(n_grid):
    return pltpu.CompilerParams(
        dimension_semantics=("arbitrary",) * n_grid,
        vmem_limit_bytes=VMEM_LIMIT,
    )


def _rmsnorm(x, g):
    return x * lax.rsqrt(jnp.mean(x * x, axis=-1, keepdims=True) + RMS_EPS) * g


def _inproj_kernel(x_ref, g_ref, wr_ref, wt_ref, wg_ref,
                   mk_ref, kc_ref, vc_ref, ks_ref, kw_ref,
                   mqT_ref, mvT_ref, nqT_ref, vsT_ref, vwT_ref, gT_ref):
    xn = _rmsnorm(x_ref[0], g_ref[...])
    xb = xn.astype(BF16)
    yr = jnp.dot(xb, wr_ref[...], preferred_element_type=F32)
    mk_ref[0] = yr[:, 0:512].astype(BF16)
    kc_ref[0] = yr[:, 512:640].astype(BF16)
    vc_ref[0] = yr[:, 640:768].astype(BF16)
    ks_ref[0] = yr[:, 768:896].astype(BF16)
    kw_ref[0] = yr[:, 896:1024].astype(BF16)
    yt = lax.dot_general(wt_ref[...], xb, NT_DIMS, preferred_element_type=F32)
    mqT_ref[0] = yt[0:512].astype(BF16)
    mvT_ref[0] = yt[512:1024].astype(BF16)
    nqT_ref[0] = yt[1024:1536].astype(BF16)
    vsT_ref[0] = yt[1536:1664].astype(BF16)
    vwT_ref[0] = yt[1664:1792].astype(BF16)
    gl = lax.dot_general(wg_ref[...], xn, NT_DIMS, precision=lax.Precision.HIGHEST,
                         preferred_element_type=F32)
    gT_ref[0] = jax.nn.sigmoid(gl)


def _inproj(x, g, wr, wt, wg, tm):
    B, S, D = x.shape
    grid = (B, S // tm)
    row = lambda w: pl.BlockSpec((1, tm, w), lambda b, i: (b, i, 0))
    col = lambda h: pl.BlockSpec((1, h, tm), lambda b, i: (b, 0, i))
    full = lambda a: pl.BlockSpec(a.shape, lambda b, i: (0,) * a.ndim)
    out_shape = [
        jax.ShapeDtypeStruct((B, S, 512), BF16),
        jax.ShapeDtypeStruct((B, S, 128), BF16),
        jax.ShapeDtypeStruct((B, S, 128), BF16),
        jax.ShapeDtypeStruct((B, S, 128), BF16),
        jax.ShapeDtypeStruct((B, S, 128), BF16),
        jax.ShapeDtypeStruct((B, 512, S), BF16),
        jax.ShapeDtypeStruct((B, 512, S), BF16),
        jax.ShapeDtypeStruct((B, 512, S), BF16),
        jax.ShapeDtypeStruct((B, 128, S), BF16),
        jax.ShapeDtypeStruct((B, 128, S), BF16),
        jax.ShapeDtypeStruct((B, 32, S), F32),
    ]
    out_specs = [row(512), row(128), row(128), row(128), row(128),
                 col(512), col(512), col(512), col(128), col(128), col(32)]
    return pl.pallas_call(
        _inproj_kernel, grid=grid,
        in_specs=[pl.BlockSpec((1, tm, D), lambda b, i: (b, i, 0)),
                  full(g), full(wr), full(wt), full(wg)],
        out_specs=out_specs, out_shape=out_shape,
        compiler_params=_params(2), name="inproj",
    )(x, g, wr, wt, wg)


ONES_ROWS = 16


def _softmax_stage(s, c, m):
    mt = jnp.max(s, axis=0, keepdims=True) - c
    m_new = jnp.maximum(m, mt)
    alpha = jnp.exp2(m - m_new)
    p = jnp.exp2(s - (m_new + c))
    return m_new, p.astype(BF16), alpha


def _pipelined_tiles(scores, values, offsets, s_ref, p_ref, first, n_tiles, j_first):
    n_ch = len(first)
    chains = range(n_ch)

    def qk_into(slot, j):
        sc = scores(j)
        for c in chains:
            s_ref[slot, c] = sc[c]

    def pv_from(slot, j, alphas, accs):
        out = []
        for c, vt in zip(chains, values(j)):
            vt1 = jnp.concatenate([vt, jnp.ones((ONES_ROWS, vt.shape[1]), BF16)], axis=0)
            out.append(alphas[c] * accs[c] + jnp.dot(vt1, p_ref[slot, c], preferred_element_type=F32))
        return out

    def softmax_into(slot, j, ms):
        cs = offsets(j)
        new = [_softmax_stage(s_ref[slot, c], cs[c], ms[c]) for c in chains]
        for c in chains:
            p_ref[slot, c] = new[c][1]
        return [n[0] for n in new], [n[2] for n in new]

    for c in chains:
        p_ref[1, c] = first[c][1]
    qk_into(0, 0)

    def pair(i, carry):
        ms, alphas, accs, j_prev = carry
        t = 2 * i
        qk_into(1, t + 1)
        accs = pv_from(1, j_prev, alphas, accs)
        ms, alphas = softmax_into(0, t, ms)
        qk_into(0, t + 2)
        accs = pv_from(0, t, alphas, accs)
        ms, alphas = softmax_into(1, t + 1, ms)
        return ms, alphas, accs, t + 1

    n_q = first[0][0].shape[1]
    init = ([f[0] for f in first], [f[2] for f in first],
            [jnp.zeros((HEAD_DIM + ONES_ROWS, n_q), F32)] * n_ch, j_first)
    _, alphas, accs, j_last = lax.fori_loop(0, (n_tiles + 1) // 2, pair, init)
    accs = pv_from(1, j_last, alphas, accs)
    return [accs[c][:HEAD_DIM] / jnp.maximum(accs[c][HEAD_DIM:HEAD_DIM + 1], 1e-30) for c in chains]


def _moba_kernel(qT_ref, k_ref, vT_ref, aug_ref, srow_ref, sl_ref, o_ref,
                 kmean_ref, s_ref, p_ref, *, nb, nbp, topk):
    qi = pl.program_id(2)
    blk = MOBA_BLOCK

    @pl.when(qi == 0)
    def _():
        kmean_ref[...] = jnp.zeros(kmean_ref.shape, F32)

        def body(n, carry):
            kb = k_ref[0, pl.ds(pl.multiple_of(n * blk, blk), blk), :].astype(F32)
            kmean_ref[pl.ds(n, 1), :] = jnp.mean(kb, axis=0, keepdims=True)
            return carry
        lax.fori_loop(0, nb, body, 0)

    qT = qT_ref[0]
    row = lax.broadcasted_iota(I32, qT.shape, 0)
    qpad = [jnp.where((row >> 6) == h, qT, jnp.zeros_like(qT)) for h in range(2)]

    bidx = lax.broadcasted_iota(I32, (nbp, blk), 0)
    rhs = []
    for h in range(2):
        gate = jnp.dot(kmean_ref[...], qpad[h].astype(F32),
                       precision=lax.Precision.HIGHEST, preferred_element_type=F32)
        gsc = jnp.where(bidx < qi, gate, -jnp.inf)
        bias = jnp.full((nbp, blk), NEG_BIG, F32)
        for _ in range(topk):
            mx = jnp.max(gsc, axis=0, keepdims=True)
            idx = jnp.min(jnp.where(gsc == mx, bidx, nbp), axis=0, keepdims=True)
            pick = jnp.logical_and(bidx == idx, mx > -jnp.inf)
            bias = jnp.where(pick, 0.0, bias)
            gsc = jnp.where(pick, -jnp.inf, gsc)
        pad = jnp.zeros((2 * HEAD_DIM - nbp - 16, blk), BF16)
        rhs.append(jnp.concatenate([qpad[h], bias.astype(BF16), srow_ref[h], pad], axis=0))

    def scores(j, a):
        k0 = pl.multiple_of(j * blk, blk)
        lhs = jnp.concatenate([k_ref[0, pl.ds(k0, blk), :], aug_ref[a]], axis=1)
        return [jnp.dot(lhs, rhs[h], preferred_element_type=F32) for h in range(2)]

    def values(j):
        k0 = pl.multiple_of(j * blk, blk)
        return [vT_ref[0, h * HEAD_DIM:(h + 1) * HEAD_DIM, pl.ds(k0, blk)] for h in range(2)]

    def offsets(j):
        dq = ((qi - j) * blk).astype(F32)
        return [sl_ref[h] * dq for h in range(2)]

    ik = lax.broadcasted_iota(I32, (blk, blk), 0)
    iq = lax.broadcasted_iota(I32, (blk, blk), 1)
    s_own = [jnp.where(ik <= iq, s, NEG_BIG) for s in scores(qi, nb)]
    m0 = jnp.full((1, blk), NEG_BIG, F32)
    first = [_softmax_stage(s_own[h], jnp.zeros((1, blk), F32), m0) for h in range(2)]

    outs = _pipelined_tiles(lambda j: scores(jnp.minimum(j, nb - 1), jnp.minimum(j, nb - 1)),
                            values, offsets, s_ref, p_ref, first, qi, qi)
    o_ref[0] = jnp.concatenate(outs, axis=0).T.astype(BF16)


def _moba(mqT, mk, mvT, aug, srow, sl):
    B, _, S = mqT.shape
    blk = MOBA_BLOCK
    nb = S // blk
    topk = min(MOBA_TOPK, nb)
    nbp = -(-nb // 16) * 16
    grid = (B, MOBA_HEADS // 2, nb)
    return pl.pallas_call(
        functools.partial(_moba_kernel, nb=nb, nbp=nbp, topk=topk), grid=grid,
        in_specs=[
            pl.BlockSpec((1, 128, blk), lambda b, p, i: (b, p, i)),
            pl.BlockSpec((1, S, 128), lambda b, p, i: (b, 0, p)),
            pl.BlockSpec((1, 128, S), lambda b, p, i: (b, p, 0)),
            pl.BlockSpec(aug.shape, lambda b, p, i: (0, 0, 0)),
            pl.BlockSpec((2, 16, blk), lambda b, p, i: (p, 0, 0)),
            pl.BlockSpec((2, 1, blk), lambda b, p, i: (p, 0, 0)),
        ],
        out_specs=pl.BlockSpec((1, blk, 128), lambda b, p, i: (b, i, p)),
        out_shape=jax.ShapeDtypeStruct((B, S, 512), BF16),
        scratch_shapes=[
            pltpu.VMEM((nbp, 128), F32),
            pltpu.VMEM((2, 2, blk, blk), F32),
            pltpu.VMEM((2, 2, blk, blk), BF16),
        ],
        compiler_params=_params(3), name="moba",
    )(mqT, mk, mvT, aug, srow, sl)


def _compress_kernel(kc_ref, vc_ref, wk_ref, wvT_ref, pek_ref, pev_ref, w2k_ref, w2vT_ref,
                     kcmp_ref, vcmpT_ref):
    nc = kc_ref.shape[1]

    wk = wk_ref[...]
    ab = jnp.dot(kc_ref[0], wk, preferred_element_type=F32)
    pt = (jnp.dot(pek_ref[0], wk[:, 0:128].astype(F32), preferred_element_type=F32)
          + jnp.dot(pek_ref[1], wk[:, 128:256].astype(F32), preferred_element_type=F32))
    pre = ab[:, 0:128] + pltpu.roll(ab[:, 128:256], nc - 1, 0) + pt[0:1]
    hid = jax.nn.gelu(pre)
    kcmp_ref[0] = jnp.dot(hid.astype(BF16), w2k_ref[...], preferred_element_type=F32).astype(BF16)

    wvT = wvT_ref[...]
    abT = lax.dot_general(wvT, vc_ref[0], NT_DIMS, preferred_element_type=F32)
    ptT = (lax.dot_general(wvT[0:128].astype(F32), pev_ref[0], NT_DIMS, preferred_element_type=F32)
           + lax.dot_general(wvT[128:256].astype(F32), pev_ref[1], NT_DIMS, preferred_element_type=F32))
    preT = abT[0:128] + pltpu.roll(abT[128:256], nc - 1, 1) + ptT[:, 0:1]
    hidT = jax.nn.gelu(preT)
    vcmpT_ref[0] = jnp.dot(w2vT_ref[...], hidT.astype(BF16), preferred_element_type=F32).astype(BF16)


def _compress(kc2, vc2, wk, wvT, pek, pev, w2k, w2vT):
    B, nc, _ = kc2.shape
    full = lambda a: pl.BlockSpec(a.shape, lambda b: (0,) * a.ndim)
    blk = pl.BlockSpec((1, nc, kc2.shape[2]), lambda b: (b, 0, 0))
    return pl.pallas_call(
        _compress_kernel, grid=(B,),
        in_specs=[blk, blk, full(wk), full(wvT), full(pek), full(pev), full(w2k), full(w2vT)],
        out_specs=[pl.BlockSpec((1, nc, 128), lambda b: (b, 0, 0)),
                   pl.BlockSpec((1, 128, nc), lambda b: (b, 0, 0))],
        out_shape=[jax.ShapeDtypeStruct((B, nc, 128), BF16),
                   jax.ShapeDtypeStruct((B, 128, nc), BF16)],
        compiler_params=_params(1), name="nsa_compress",
    )(kc2, vc2, wk, wvT, pek, pev, w2k, w2vT)


def _nsa_kernel(qT_ref, kcmp_ref, vcmpT_ref, ks_ref, vsT_ref, kw_ref, vwT_ref,
                g_ref, sl_ref, srow_ref, auga_ref, augb_ref, augc_ref, tblc_ref, bw_ref, o_ref,
                s_ref, p_ref, pc_ref, *, n_slc, topn):
    g = pl.program_id(1)
    qi = pl.program_id(2)
    tq = NSA_TQ
    hg = NSA_GROUP
    wl = hg * tq
    q0 = qi * tq

    q4 = qT_ref[0]
    qT = jnp.concatenate([q4[h * HEAD_DIM:(h + 1) * HEAD_DIM] for h in range(hg)], axis=1)
    qT2 = jnp.concatenate([qT, qT], axis=0)
    rowi = lax.broadcasted_iota(I32, qT2.shape, 0)
    qpad = jnp.where((rowi >> 6) == g, qT2, jnp.zeros_like(qT2))
    slope = sl_ref[0]
    lane = lax.broadcasted_iota(I32, (1, wl), 1)
    t_q = q0 + (lane & (tq - 1))

    nc = kcmp_ref.shape[1]
    rhs_top = jnp.where((rowi >> 6) == g, qT2, srow_ref[0])
    mine_c = (lax.broadcasted_iota(I32, (nc, LANES), 1) >> 6) == g
    lhs_c = jnp.where(mine_c, kcmp_ref[0], augc_ref[0])
    first_c = pl.multiple_of(nc - qi * (tq // NSA_CMP_STRIDE), 8)
    z = jnp.dot(lhs_c, rhs_top, preferred_element_type=F32) + tblc_ref[pl.ds(first_c, nc), :]
    mx = jnp.max(z, axis=0, keepdims=True)
    e = jnp.exp2(z - mx)
    den = jnp.maximum(jnp.sum(e, axis=0, keepdims=True), 1e-30)
    p = e * jnp.where(t_q >= NSA_CMP_LEN - 1, 1.0 / den, 0.0)
    o_c = jnp.dot(vcmpT_ref[0], p.astype(BF16), preferred_element_type=F32)

    pc = p[:, 0:tq]
    for h in range(1, hg):
        pc = pc + p[:, h * tq:(h + 1) * tq]
    n_lc = tq // LANES
    for c in range(n_lc):
        pc_ref[c] = pc[:, c * LANES:(c + 1) * LANES]
    su = NSA_SLC_BLOCK // NSA_CMP_STRIDE
    x = [jnp.concatenate([pc_ref[c, pl.ds(k, n_slc, stride=su), :] for c in range(n_lc)], axis=1)
         for k in range(su)]
    jb = lax.broadcasted_iota(I32, (n_slc, tq), 0)
    prev = jnp.where(jb == 0, 0.0, pltpu.roll(x[3], 1, 0))
    imp = 2.0 * (x[0] + x[1] + x[2]) + x[3] + prev
    cur = (q0 + lax.broadcasted_iota(I32, (1, tq), 1)) >> 6
    allowed = jb <= cur
    forced = jnp.logical_or(jb == 0, jnp.logical_or(jb == cur, jb == cur - 1))
    bias = jnp.where(jnp.logical_and(allowed, forced), 0.0, NEG_BIG)
    sc = jnp.where(jnp.logical_and(allowed, jnp.logical_not(forced)), imp, -1.0)
    for _ in range(topn - 3):
        smx = jnp.max(sc, axis=0, keepdims=True)
        idx = jnp.min(jnp.where(sc == smx, jb, n_slc), axis=0, keepdims=True)
        pick = jnp.logical_and(jb == idx, smx >= 0.0)
        bias = jnp.where(pick, 0.0, bias)
        sc = jnp.where(pick, -1.0, sc)

    wc = NSA_CHAIN_LANES
    n_ch = wl // wc

    def lane_split(a):
        return [a[:, c * wc:(c + 1) * wc] for c in range(n_ch)]

    if n_slc < LANES:
        bias = jnp.concatenate([bias, jnp.zeros((LANES - n_slc, tq), F32)], axis=0)
    bias4 = jnp.concatenate([bias.astype(BF16)] * hg, axis=1)
    rhs = jnp.concatenate([rhs_top, bias4], axis=0)
    mine = (lax.broadcasted_iota(I32, (SLC_TILE, LANES), 1) >> 6) == g
    n_tiles = augb_ref.shape[0]

    def scores(j, null):
        k0 = pl.multiple_of(j * SLC_TILE, SLC_TILE)
        kt = ks_ref[0, pl.ds(k0, SLC_TILE), :]
        lhs = jnp.concatenate([jnp.where(mine, kt, auga_ref[null]), augb_ref[j]], axis=1)
        return jnp.dot(lhs, rhs, preferred_element_type=F32)

    def values(j):
        return [vsT_ref[0, :, pl.ds(pl.multiple_of(j * SLC_TILE, SLC_TILE), SLC_TILE)]] * n_ch

    def offsets(j):
        return lane_split(slope * (q0 - j * SLC_TILE).astype(F32))

    jd = lax.div(q0, SLC_TILE)
    t_k = jd * SLC_TILE + lax.broadcasted_iota(I32, (SLC_TILE, wl), 0)
    s_diag = lane_split(jnp.where(t_k <= t_q, scores(jd, 0), NEG_BIG))
    c_diag = offsets(jd)
    first = [_softmax_stage(s_diag[c], c_diag[c], jnp.full((1, wc), NEG_BIG, F32))
             for c in range(n_ch)]

    def past_scores(j):
        return lane_split(scores(jnp.minimum(j, n_tiles - 1), (j >= jd).astype(I32)))
    o_s = jnp.concatenate(
        _pipelined_tiles(past_scores, values, offsets, s_ref, p_ref, first, jd, jd), axis=1)

    start = pl.multiple_of(jnp.maximum(q0 - NSA_WINDOW, 0), tq)
    first_w = pl.multiple_of(NSA_WINDOW - (q0 - start), tq)
    kt = kw_ref[0, pl.ds(start, WIN_KEYS), :]
    z = jnp.dot(kt, qpad, preferred_element_type=F32) + bw_ref[0, pl.ds(first_w, WIN_KEYS), :]
    mx = jnp.max(z, axis=0, keepdims=True)
    p = jnp.exp2(z - mx)
    den = jnp.maximum(jnp.sum(p, axis=0, keepdims=True), 1e-30)
    o_w = jnp.dot(vwT_ref[0, :, pl.ds(start, WIN_KEYS)], p.astype(BF16),
                  preferred_element_type=F32) / den

    gt = g_ref[0]

    def gate_row(br):
        return jnp.concatenate([gt[br * hg + h:br * hg + h + 1] for h in range(hg)], axis=1)

    o = gate_row(0) * o_c + gate_row(1) * o_s + gate_row(2) * o_w
    o4 = jnp.concatenate([o[:, h * tq:(h + 1) * tq] for h in range(hg)], axis=0)
    o_ref[0] = o4.T.astype(BF16)


def _nsa(nqT, kcmp, vcmpT, ks, vsT, kw, vwT, gT, sl, srow, auga, augb, augc, tblc, bw):
    B, _, S = nqT.shape
    tq = NSA_TQ
    nc = kcmp.shape[1]
    n_slc = S // NSA_SLC_BLOCK
    topn = min(NSA_SLC_TOPN, n_slc)
    wl = NSA_GROUP * tq
    grid = (B, NSA_KV_HEADS, S // tq)
    return pl.pallas_call(
        functools.partial(_nsa_kernel, n_slc=n_slc, topn=topn), grid=grid,
        in_specs=[
            pl.BlockSpec((1, NSA_GROUP * HEAD_DIM, tq), lambda b, g, i: (b, g, i)),
            pl.BlockSpec((1, nc, 128), lambda b, g, i: (b, 0, 0)),
            pl.BlockSpec((1, HEAD_DIM, nc), lambda b, g, i: (b, g, 0)),
            pl.BlockSpec((1, S, 128), lambda b, g, i: (b, 0, 0)),
            pl.BlockSpec((1, HEAD_DIM, S), lambda b, g, i: (b, g, 0)),
            pl.BlockSpec((1, S, 128), lambda b, g, i: (b, 0, 0)),
            pl.BlockSpec((1, HEAD_DIM, S), lambda b, g, i: (b, g, 0)),
            pl.BlockSpec((1, 16, tq), lambda b, g, i: (b, g, i)),
            pl.BlockSpec((1, 1, wl), lambda b, g, i: (g, 0, 0)),
            pl.BlockSpec((1, 2 * HEAD_DIM, wl), lambda b, g, i: (g, 0, 0)),
            pl.BlockSpec((2, SLC_TILE, LANES), lambda b, g, i: (g, 0, 0)),
            pl.BlockSpec(augb.shape, lambda b, g, i: (0, 0, 0)),
            pl.BlockSpec((1, nc, LANES), lambda b, g, i: (g, 0, 0)),
            pl.BlockSpec(tblc.shape, lambda b, g, i: (0, 0)),
            pl.BlockSpec((1,) + bw.shape[1:], lambda b, g, i: (g, 0, 0)),
        ],
        out_specs=pl.BlockSpec((1, tq, NSA_GROUP * HEAD_DIM), lambda b, g, i: (b, i, g)),
        out_shape=jax.ShapeDtypeStruct((B, S, 512), BF16),
        scratch_shapes=[
            pltpu.VMEM((2, wl // NSA_CHAIN_LANES, SLC_TILE, NSA_CHAIN_LANES), F32),
            pltpu.VMEM((2, wl // NSA_CHAIN_LANES, SLC_TILE, NSA_CHAIN_LANES), BF16),
            pltpu.VMEM((tq // LANES, nc, LANES), F32),
        ],
        compiler_params=_params(3), name="nsa",
    )(nqT, kcmp, vcmpT, ks, vsT, kw, vwT, gT, sl, srow, auga, augb, augc, tblc, bw)


def _outproj_kernel(om_ref, on_ref, x_ref, wo_ref, g_ref, wr_ref, br_ref,
                    x1_ref, hn_ref, e_ref, w_ref):
    attn = (jnp.dot(om_ref[...], wo_ref[0:512, :], preferred_element_type=F32)
            + jnp.dot(on_ref[...], wo_ref[512:1024, :], preferred_element_type=F32))
    x1 = x_ref[...] + attn
    x1_ref[...] = x1
    hn = _rmsnorm(x1, g_ref[...])
    _store_token_tiles(hn_ref, hn)
    logits = jnp.dot(hn, wr_ref[...], precision=lax.Precision.HIGHEST,
                     preferred_element_type=F32) + br_ref[...]
    tm = logits.shape[0]
    lane = lax.broadcasted_iota(I32, (tm, LANES), 1)
    sc = jnp.where(lane < N_EXPERTS, logits, -jnp.inf)
    e_out = jnp.zeros((tm, LANES), I32)
    vals = []
    for k in range(TOP_K):
        mx = jnp.max(sc, axis=1, keepdims=True)
        idx = jnp.min(jnp.where(sc == mx, lane, LANES), axis=1, keepdims=True)
        e_out = jnp.where(lane == k, idx, e_out)
        sc = jnp.where(lane == idx, -jnp.inf, sc)
        vals.append(mx)
    ex = [jnp.exp(v - vals[0]) for v in vals]
    den = ex[0] + ex[1] + ex[2] + ex[3]
    w_out = jnp.zeros((tm, LANES), F32)
    for k in range(TOP_K):
        w_out = jnp.where(lane == k, ex[k] / den, w_out)
    e_ref[...] = e_out
    w_ref[...] = w_out


def _outproj(om, on, x, wo, g, wr, br, tm):
    N, D = x.shape
    full = lambda a: pl.BlockSpec(a.shape, lambda i: (0,) * a.ndim)
    row = lambda w: pl.BlockSpec((tm, w), lambda i: (i, 0))
    return pl.pallas_call(
        _outproj_kernel, grid=(N // tm,),
        in_specs=[row(512), row(512), row(D), full(wo), full(g), full(wr), full(br)],
        out_specs=[row(D), pl.BlockSpec((tm * SUBLANES, LANES), lambda i: (i, 0)), row(LANES), row(LANES)],
        out_shape=[jax.ShapeDtypeStruct((N, D), F32), jax.ShapeDtypeStruct((N * SUBLANES, LANES), F32),
                   jax.ShapeDtypeStruct((N, LANES), I32), jax.ShapeDtypeStruct((N, LANES), F32)],
        compiler_params=_params(1), name="outproj_router",
    )(om, on, x, wo, g, wr, br)


def _rank_kernel(e_ref, rank_ref, cnt_ref, base_ref):
    i = pl.program_id(0)
    T = e_ref.shape[0]

    @pl.when(i == 0)
    def _():
        base_ref[...] = jnp.zeros(base_ref.shape, F32)

    e = e_ref[...]
    lane = lax.broadcasted_iota(I32, (T, LANES), 1)
    tril = jnp.where(lax.broadcasted_iota(I32, (T, T), 0) >= lax.broadcasted_iota(I32, (T, T), 1),
                     1.0, 0.0).astype(BF16)
    out = jnp.zeros((T, LANES), I32)
    for k in range(TOP_K):
        hit = lane == e[:, k:k + 1]
        oh = jnp.where(hit, 1.0, 0.0)
        cum = jnp.dot(tril, oh.astype(BF16), preferred_element_type=F32)
        base = base_ref[0:1, :]
        r = jnp.sum(jnp.where(hit, cum - 1.0 + base, 0.0), axis=1, keepdims=True)
        out = jnp.where(lane == k, r.astype(I32), out)
        base_ref[...] = base_ref[...] + jnp.sum(oh, axis=0, keepdims=True)
    rank_ref[...] = out
    cnt_ref[...] = base_ref[...]


def _ranks(e128):
    N = e128.shape[0]
    T = RANK_TILE
    return pl.pallas_call(
        _rank_kernel, grid=(N // T,),
        in_specs=[pl.BlockSpec((T, LANES), lambda i: (i, 0))],
        out_specs=[pl.BlockSpec((T, LANES), lambda i: (i, 0)),
                   pl.BlockSpec((8, LANES), lambda i: (0, 0))],
        out_shape=[jax.ShapeDtypeStruct((N, LANES), I32),
                   jax.ShapeDtypeStruct((8, LANES), F32)],
        scratch_shapes=[pltpu.VMEM((8, LANES), F32)],
        compiler_params=_params(1), name="route_ranks",
    )(e128)


def _row_copy(src, dst, i_src, i_dst, sem):
    return pltpu.make_async_copy(src.at[pl.ds(pl.multiple_of(i_src * SUBLANES, SUBLANES), SUBLANES)],
                                 dst.at[pl.ds(pl.multiple_of(i_dst * SUBLANES, SUBLANES), SUBLANES)], sem)


def _store_token_tiles(ref, x):
    rows = x.shape[0]
    for c in range(SUBLANES):
        ref[pl.ds(c, rows, stride=SUBLANES), :] = x[:, c * LANES:(c + 1) * LANES]


def _load_token_tiles(ref, rows):
    return jnp.concatenate([ref[pl.ds(c, rows, stride=SUBLANES), :] for c in range(SUBLANES)], axis=1)


def _dispatch_kernel(dest_hbm, hp_ref, xz_hbm, out_hbm, idx_ref, isem, sem):
    del xz_hbm
    i = pl.program_id(0)
    T = ROUTE_TILE
    cp = pltpu.make_async_copy(dest_hbm.at[i], idx_ref, isem)
    cp.start()
    cp.wait()

    def issue(t, carry):
        for k in range(TOP_K):
            _row_copy(hp_ref, out_hbm, t, idx_ref[t * TOP_K + k], sem).start()
        return carry
    lax.fori_loop(0, T, issue, 0, unroll=ROW_DMA_UNROLL)

    def drain(t, carry):
        for k in range(TOP_K):
            _row_copy(hp_ref, out_hbm, 0, 0, sem).wait()
        return carry
    lax.fori_loop(0, T, drain, 0, unroll=ROW_DMA_UNROLL)


def _dispatch(dest2, hp, xzero):
    nsteps = dest2.shape[0]
    T = ROUTE_TILE
    return pl.pallas_call(
        _dispatch_kernel, grid=(nsteps,),
        in_specs=[pl.BlockSpec(memory_space=pl.ANY),
                  pl.BlockSpec((T * SUBLANES, LANES), lambda i: (i, 0)),
                  pl.BlockSpec(memory_space=pl.ANY)],
        out_specs=pl.BlockSpec(memory_space=pl.ANY),
        out_shape=jax.ShapeDtypeStruct(xzero.shape, xzero.dtype),
        scratch_shapes=[pltpu.SMEM((T * TOP_K,), I32),
                        pltpu.SemaphoreType.DMA, pltpu.SemaphoreType.DMA],
        input_output_aliases={2: 0},
        compiler_params=_params(1), name="moe_dispatch",
    )(dest2, hp, xzero)


def _expert_kernel(be_ref, na_ref, x_ref, wg_ref, wu_ref, bg_ref, bu_ref, wd_ref, bd_ref, y_ref):
    b = pl.program_id(0)
    active = b < na_ref[0]

    @pl.when(active)
    def _():
        xb = _load_token_tiles(x_ref, MOE_ROWS).astype(BF16)
        gg = lax.dot_general(xb, wg_ref[0], NT_DIMS, preferred_element_type=F32) + bg_ref[0]
        uu = lax.dot_general(xb, wu_ref[0], NT_DIMS, preferred_element_type=F32) + bu_ref[0]
        gg = jnp.minimum(gg, SWIGLU_LIMIT)
        uu = jnp.clip(uu, -SWIGLU_LIMIT, SWIGLU_LIMIT)
        a = gg * jax.nn.sigmoid(SWIGLU_ALPHA * gg) * (uu + 1.0)
        _store_token_tiles(y_ref, jnp.dot(a.astype(BF16), wd_ref[0], preferred_element_type=F32) + bd_ref[0])

    @pl.when(jnp.logical_not(active))
    def _():
        y_ref[...] = jnp.zeros(y_ref.shape, F32)


def _experts(blk_e, n_act, xrows, wgT, wuT, bg, bu, wd, bd):
    _, F, D = wgT.shape
    assert D == SUBLANES * LANES
    P = xrows.shape[0] // SUBLANES
    n_blk = P // MOE_ROWS
    rows_spec = pl.BlockSpec((MOE_ROWS * SUBLANES, LANES), lambda b, be, na: (b, 0))
    wspec = lambda r, c: pl.BlockSpec((1, r, c), lambda b, be, na: (be[b], 0, 0))
    grid_spec = pltpu.PrefetchScalarGridSpec(
        num_scalar_prefetch=2, grid=(n_blk,),
        in_specs=[rows_spec, wspec(F, D), wspec(F, D), wspec(1, F), wspec(1, F), wspec(F, D), wspec(1, D)],
        out_specs=rows_spec,
    )
    return pl.pallas_call(
        _expert_kernel, grid_spec=grid_spec,
        out_shape=jax.ShapeDtypeStruct(xrows.shape, F32),
        compiler_params=_params(1), name="moe_experts",
    )(blk_e, n_act, xrows, wgT, wuT, bg, bu, wd, bd)


def _combine_kernel(dest_hbm, x1_ref, w_ref, g_ref, y_hbm, o_ref, idx_ref, buf_ref, isem, sem):
    i = pl.program_id(0)
    T = ROUTE_TILE
    cp = pltpu.make_async_copy(dest_hbm.at[i], idx_ref, isem)
    cp.start()
    cp.wait()

    def issue(t, carry):
        for k in range(TOP_K):
            _row_copy(y_hbm, buf_ref.at[k], idx_ref[t * TOP_K + k], t, sem).start()
        return carry
    lax.fori_loop(0, T, issue, 0, unroll=ROW_DMA_UNROLL)

    def drain(t, carry):
        for k in range(TOP_K):
            _row_copy(y_hbm, buf_ref.at[k], 0, 0, sem).wait()
        return carry
    lax.fori_loop(0, T, drain, 0, unroll=ROW_DMA_UNROLL)

    x1 = x1_ref[...]
    w = w_ref[...]
    cols = []
    for c in range(SUBLANES):
        acc = x1[:, c * LANES:(c + 1) * LANES]
        for k in range(TOP_K):
            acc = acc + w[:, k:k + 1] * buf_ref[k, pl.ds(c, T, stride=SUBLANES), :]
        cols.append(acc)
    o_ref[...] = _rmsnorm(jnp.concatenate(cols, axis=1), g_ref[...])


def _combine(dest2, x1, w128, g, yrows):
    N, D = x1.shape
    T = ROUTE_TILE
    return pl.pallas_call(
        _combine_kernel, grid=(N // T,),
        in_specs=[pl.BlockSpec(memory_space=pl.ANY),
                  pl.BlockSpec((T, D), lambda i: (i, 0)),
                  pl.BlockSpec((T, LANES), lambda i: (i, 0)),
                  pl.BlockSpec(g.shape, lambda i: (0, 0)),
                  pl.BlockSpec(memory_space=pl.ANY)],
        out_specs=pl.BlockSpec((T, D), lambda i: (i, 0)),
        out_shape=jax.ShapeDtypeStruct((N, D), F32),
        scratch_shapes=[pltpu.SMEM((T * TOP_K,), I32),
                        pltpu.VMEM((TOP_K, T * SUBLANES, LANES), F32),
                        pltpu.SemaphoreType.DMA, pltpu.SemaphoreType.DMA],
        compiler_params=_params(1), name="moe_combine",
    )(dest2, x1, w128, g, yrows)


def _alibi_slopes():
    n = MOBA_HEADS + NSA_HEADS
    s = jnp.exp2(-8.0 * jnp.arange(1, n + 1, dtype=F32) / n)
    return s[0::2], s[1::2]


def _prep_inproj(w_in):
    hd = HEAD_DIM
    sizes = [MOBA_HEADS * hd] * 3 + [NSA_HEADS * hd] + [NSA_KV_HEADS * hd] * 6 + [NSA_BRANCHES * NSA_HEADS]
    cuts = np.cumsum([0] + sizes)
    mq, mk, mv, nq, kc, vc, ks, vs, kw, vw, ng = [w_in[:, cuts[i]:cuts[i + 1]] for i in range(11)]
    qscale = (hd ** -0.5) * LOG2E
    wr = jnp.concatenate([mk, kc, vc, ks, kw], axis=1).astype(BF16)
    wt = jnp.concatenate([mq * qscale, mv, nq * qscale, vs, vw], axis=1).T.astype(BF16)
    ngr = ng.reshape(-1, NSA_KV_HEADS, NSA_GROUP, NSA_BRANCHES).transpose(1, 3, 2, 0)
    ngr = ngr.reshape(NSA_KV_HEADS, NSA_BRANCHES * NSA_GROUP, -1)
    wg = jnp.pad(ngr, ((0, 0), (0, 16 - NSA_BRANCHES * NSA_GROUP), (0, 0))).reshape(32, -1)
    return wr, wt, wg.astype(F32)


def _prep_compress(w1, w2, pe):
    hd, half = HEAD_DIM, NSA_CMP_STRIDE
    w1r = w1.reshape(2, half, hd, hd)
    eye = jnp.eye(NSA_KV_HEADS, dtype=w1.dtype)
    w = jnp.einsum('alde,gh->lgdahe', w1r, eye).reshape(half * NSA_KV_HEADS * hd, 2 * NSA_KV_HEADS * hd)
    w2b = jnp.einsum('de,gh->gdhe', w2, eye).reshape(NSA_KV_HEADS * hd, NSA_KV_HEADS * hd)
    per = pe.reshape(2, half, 1, hd)
    pe2 = jnp.broadcast_to(per, (2, half, NSA_KV_HEADS, hd)).reshape(2, 1, half * NSA_KV_HEADS * hd)
    pe2 = jnp.broadcast_to(pe2, (2, 8, pe2.shape[2]))
    return w.astype(BF16), w2b.astype(BF16), pe2.astype(F32)


def _attention_tables(S):
    moba_sl, nsa_sl = _alibi_slopes()
    moba_sl = moba_sl * LOG2E
    nsa_sl = nsa_sl * LOG2E
    blk = MOBA_BLOCK
    moba_row = jnp.broadcast_to(moba_sl[:, None, None], (MOBA_HEADS, 1, blk))
    nb = S // blk
    nbp = -(-nb // 16) * 16
    col = jnp.arange(LANES)[None, None, :]
    tile = jnp.arange(nb + 1)[:, None, None]
    off = jnp.arange(blk, dtype=F32)[None, :, None]
    moba_aug = jnp.where(jnp.logical_and(col == tile, tile < nb), 1.0,
                         jnp.where(jnp.logical_and(col >= nbp, col < nbp + 3), off, 0.0)).astype(BF16)
    parts = jnp.stack(list(_split3(moba_sl)) + [jnp.zeros_like(moba_sl)] * 13, axis=1)
    moba_srow = jnp.broadcast_to(parts[:, :, None], (MOBA_HEADS, 16, blk)).astype(BF16)

    wl = NSA_GROUP * NSA_TQ
    n_slc = S // NSA_SLC_BLOCK
    assert n_slc <= LANES, "block-choice rows must fit the spare contraction rows"
    nsa_row = jnp.repeat(nsa_sl.reshape(NSA_KV_HEADS, NSA_GROUP), NSA_TQ, axis=1)
    hi, mid, lo = [t[:, None, :] for t in _split3(nsa_row)]
    base = ((1 - jnp.arange(NSA_KV_HEADS)) * HEAD_DIM)[:, None, None]
    rows = jnp.arange(2 * HEAD_DIM)[None, :, None]
    nsa_srow = jnp.where(rows == base, hi, jnp.where(rows == base + 1, mid, jnp.where(
        rows == base + 2, lo, jnp.where(rows == base + 3, NEG_BIG, 0.0)))).astype(BF16)
    lane = jnp.arange(LANES)[None, None, None, :]
    base4 = base[:, None]
    null = jnp.arange(2, dtype=F32)[None, :, None, None]
    koff = jnp.arange(SLC_TILE, dtype=F32)[None, None, :, None]
    nsa_auga = jnp.where(jnp.logical_and(lane >= base4, lane < base4 + 3), koff,
                         jnp.where(lane == base4 + 3, null, 0.0))
    nsa_auga = nsa_auga.reshape(2 * NSA_KV_HEADS, SLC_TILE, LANES).astype(BF16)
    per = SLC_TILE // NSA_SLC_BLOCK
    tile = jnp.arange(S // SLC_TILE)[:, None, None]
    blk_of = tile * per + jnp.arange(SLC_TILE)[None, :, None] // NSA_SLC_BLOCK
    nsa_augb = (jnp.arange(LANES)[None, None, :] == blk_of).astype(BF16)

    nc = S // NSA_CMP_STRIDE
    ci = jnp.arange(nc)[None, :, None]
    lane3 = jnp.arange(LANES)[None, None, :]
    nsa_augc = jnp.where(jnp.logical_and(lane3 >= base + 4, lane3 < base + 7), (ci >> 1).astype(F32),
                         jnp.where(jnp.logical_and(lane3 >= base + 7, lane3 < base + 10),
                                   (ci & 1).astype(F32), 0.0)).astype(BF16)
    step2 = [t[:, None, :] for t in _split3(nsa_row * (2.0 * NSA_CMP_STRIDE))]
    step1 = [t[:, None, :] for t in _split3(nsa_row * (1.0 * NSA_CMP_STRIDE))]
    for k in range(3):
        nsa_srow = jnp.where(rows == base + 4 + k, step2[k].astype(BF16),
                             jnp.where(rows == base + 7 + k, step1[k].astype(BF16), nsa_srow))
    il = jnp.tile(jnp.arange(NSA_TQ), NSA_GROUP)[None, :]
    rel = (jnp.arange(2 * nc) - nc)[:, None]
    nsa_tblc = jnp.where(rel * NSA_CMP_STRIDE + (NSA_CMP_LEN - 1) <= il, 0.0, NEG_BIG).astype(F32)
    dist = (NSA_WINDOW + il - jnp.arange(NSA_WINDOW + WIN_KEYS)[:, None])[None]
    nsa_bw = jnp.where(jnp.logical_and(dist >= 0, dist < NSA_WINDOW),
                       -nsa_row[:, None, :] * dist.astype(F32), NEG_BIG)
    return ((moba_aug, moba_srow, moba_row),
            (nsa_row.reshape(NSA_KV_HEADS, 1, wl), nsa_srow, nsa_auga, nsa_augb, nsa_augc, nsa_tblc, nsa_bw))


def _split3(x):
    hi = x.astype(BF16).astype(F32)
    mid = (x - hi).astype(BF16).astype(F32)
    lo = (x - hi - mid).astype(BF16).astype(F32)
    return hi, mid, lo


def _attention(x, attn_norm_g, w_in, cmp_pe_k, cmp_pe_v, cmp_w1_k, cmp_w2_k, cmp_w1_v, cmp_w2_v):
    B, S, D = x.shape
    wr, wt, wg = _prep_inproj(w_in)
    (mk, kc, vc, ks, kw, mqT, mvT, nqT, vsT, vwT, gT) = _inproj(
        x, attn_norm_g.reshape(1, D), wr, wt, wg, tm=512)
    moba_tabs, nsa_tabs = _attention_tables(S)
    o_moba = _moba(mqT, mk, mvT, *moba_tabs)
    wk, w2k, pek = _prep_compress(cmp_w1_k, cmp_w2_k, cmp_pe_k)
    wv, w2v, pev = _prep_compress(cmp_w1_v, cmp_w2_v, cmp_pe_v)
    nc = S // NSA_CMP_STRIDE
    kcmp, vcmpT = _compress(kc.reshape(B, nc, -1), vc.reshape(B, nc, -1), wk, wv.T, pek, pev, w2k, w2v.T)
    o_nsa = _nsa(nqT, kcmp, vcmpT, ks, vsT, kw, vwT, gT, *nsa_tabs)
    return o_moba, o_nsa


def _moe(x1, hn, e128, w128, w_up, b_up, w_down, b_down, final_norm_g):
    N, D = x1.shape
    rank128, cnt = _ranks(e128)
    counts = cnt[0, :N_EXPERTS].astype(I32)
    padded = (counts + MOE_ROWS - 1) // MOE_ROWS * MOE_ROWS
    pends = jnp.cumsum(padded)
    pstarts = pends - padded
    e4 = e128[:, :TOP_K]
    dest = pstarts[e4] + rank128[:, :TOP_K]
    dest2 = dest.reshape(N // ROUTE_TILE, ROUTE_TILE * TOP_K)
    n_blk = (N * TOP_K + N_EXPERTS * MOE_ROWS + MOE_ROWS - 1) // MOE_ROWS
    P = n_blk * MOE_ROWS
    blk_start = jnp.arange(n_blk, dtype=I32) * MOE_ROWS
    blk_e = jnp.minimum(jnp.sum((pends[None, :] <= blk_start[:, None]).astype(I32), axis=1), N_EXPERTS - 1)
    n_act = (pends[-1:] // MOE_ROWS).astype(I32)
    xrows = _dispatch(dest2, hn, jnp.zeros((P * SUBLANES, LANES), F32))
    wT = jnp.swapaxes(w_up, 1, 2).astype(BF16).reshape(N_EXPERTS, -1, 2, D)
    bg = b_up[:, None, 0::2]
    bu = b_up[:, None, 1::2]
    yrows = _experts(blk_e, n_act, xrows, wT[:, :, 0], wT[:, :, 1], bg, bu,
                     w_down.astype(BF16), b_down[:, None, :])
    return _combine(dest2, x1, w128, final_norm_g.reshape(1, D), yrows)


def kernel(x, attn_norm_g, w_in, cmp_pe_k, cmp_pe_v, cmp_w1_k, cmp_w2_k, cmp_w1_v, cmp_w2_v, w_out, ffn_norm_g, w_router, b_router, w_up, b_up, w_down, b_down, final_norm_g):
    B, S, D = x.shape
    assert attn_norm_g.shape[0] == 1, "single-layer kernel"
    o_moba, o_nsa = _attention(x, attn_norm_g[0], w_in[0], cmp_pe_k[0], cmp_pe_v[0],
                               cmp_w1_k[0], cmp_w2_k[0], cmp_w1_v[0], cmp_w2_v[0])
    N = B * S
    wr = jnp.pad(w_router[0], ((0, 0), (0, LANES - N_EXPERTS)))
    br = jnp.pad(b_router[0], (0, LANES - N_EXPERTS)).reshape(1, LANES)
    x1, hn, e128, w128 = _outproj(o_moba.reshape(N, -1), o_nsa.reshape(N, -1), x.reshape(N, D),
                                  w_out[0].astype(BF16), ffn_norm_g[0].reshape(1, D), wr, br, tm=512)
    out = _moe(x1, hn, e128, w128, w_up[0], b_up[0], w_down[0], b_down[0], final_norm_g)
    return out.reshape(B, S, D)
```

```python
import functools

import jax
import jax.numpy as jnp
import numpy as np
from jax import lax
from jax.experimental import pallas as pl
from jax.experimental.pallas import tpu as pltpu

F32 = jnp.float32
BF16 = jnp.bfloat16
I32 = jnp.int32

HEAD_DIM = 64
MOBA_HEADS = 8
NSA_HEADS = 8
NSA_KV_HEADS = 2
NSA_GROUP = NSA_HEADS // NSA_KV_HEADS
MOBA_BLOCK = 256
MOBA_TOPK = 3
NSA_CMP_LEN = 32
NSA_CMP_STRIDE = 16
NSA_SLC_BLOCK = 64
NSA_SLC_TOPN = 16
NSA_WINDOW = 512
NSA_BRANCHES = 3
N_EXPERTS = 32
TOP_K = 4
SWIGLU_LIMIT = 7.0
SWIGLU_ALPHA = 1.702
RMS_EPS = 1e-5
NEG_BIG = -1e30
LOG2E = 1.4426950408889634

LANES = 128
SUBLANES = 8
VMEM_LIMIT = 56 * 1024 * 1024

NSA_TQ = 256
NSA_CHAIN_LANES = 256
SLC_TILE = 256
WIN_KEYS = NSA_WINDOW + NSA_TQ
MOE_ROWS = 512
ROUTE_TILE = 256
RANK_TILE = 512
ROW_DMA_UNROLL = 8

NT_DIMS = (((1,), (1,)), ((), ()))


def _params(n_grid):
    return pltpu.CompilerParams(
        dimension_semantics=("arbitrary",) * n_grid,
        vmem_limit_bytes=VMEM_LIMIT,
    )


def _rmsnorm(x, g):
    return x * lax.rsqrt(jnp.mean(x * x, axis=-1, keepdims=True) + RMS_EPS) * g


def _inproj_kernel(x_ref, g_ref, wr_ref, wt_ref, wg_ref,
                   mk_ref, kc_ref, vc_ref, ks_ref, kw_ref,
                   mqT_ref, mvT_ref, nqT_ref, vsT_ref, vwT_ref, gT_ref):
    xn = _rmsnorm(x_ref[0], g_ref[...])
    xb = xn.astype(BF16)
    yr = jnp.dot(xb, wr_ref[...], preferred_element_type=F32)
    mk_ref[0] = yr[:, 0:512].astype(BF16)
    kc_ref[0] = yr[:, 512:640].astype(BF16)
    vc_ref[0] = yr[:, 640:768].astype(BF16)
    ks_ref[0] = yr[:, 768:896].astype(BF16)
    kw_ref[0] = yr[:, 896:1024].astype(BF16)
    yt = lax.dot_general(wt_ref[...], xb, NT_DIMS, preferred_element_type=F32)
    mqT_ref[0] = yt[0:512].astype(BF16)
    mvT_ref[0] = yt[512:1024].astype(BF16)
    nqT_ref[0] = yt[1024:1536].astype(BF16)
    vsT_ref[0] = yt[1536:1664].astype(BF16)
    vwT_ref[0] = yt[1664:1792].astype(BF16)
    gl = lax.dot_general(wg_ref[...], xn, NT_DIMS, precision=lax.Precision.HIGHEST,
                         preferred_element_type=F32)
    gT_ref[0] = jax.nn.sigmoid(gl)


def _inproj(x, g, wr, wt, wg, tm):
    B, S, D = x.shape
    grid = (B, S // tm)
    row = lambda w: pl.BlockSpec((1, tm, w), lambda b, i: (b, i, 0))
    col = lambda h: pl.BlockSpec((1, h, tm), lambda b, i: (b, 0, i))
    full = lambda a: pl.BlockSpec(a.shape, lambda b, i: (0,) * a.ndim)
    out_shape = [
        jax.ShapeDtypeStruct((B, S, 512), BF16),
        jax.ShapeDtypeStruct((B, S, 128), BF16),
        jax.ShapeDtypeStruct((B, S, 128), BF16),
        jax.ShapeDtypeStruct((B, S, 128), BF16),
        jax.ShapeDtypeStruct((B, S, 128), BF16),
        jax.ShapeDtypeStruct((B, 512, S), BF16),
        jax.ShapeDtypeStruct((B, 512, S), BF16),
        jax.ShapeDtypeStruct((B, 512, S), BF16),
        jax.ShapeDtypeStruct((B, 128, S), BF16),
        jax.ShapeDtypeStruct((B, 128, S), BF16),
        jax.ShapeDtypeStruct((B, 32, S), F32),
    ]
    out_specs = [row(512), row(128), row(128), row(128), row(128),
                 col(512), col(512), col(512), col(128), col(128), col(32)]
    return pl.pallas_call(
        _inproj_kernel, grid=grid,
        in_specs=[pl.BlockSpec((1, tm, D), lambda b, i: (b, i, 0)),
                  full(g), full(wr), full(wt), full(wg)],
        out_specs=out_specs, out_shape=out_shape,
        compiler_params=_params(2), name="inproj",
    )(x, g, wr, wt, wg)


ONES_ROWS = 16


def _softmax_stage(s, c, m):
    mt = jnp.max(s, axis=0, keepdims=True) - c
    m_new = jnp.maximum(m, mt)
    alpha = jnp.exp2(m - m_new)
    p = jnp.exp2(s - (m_new + c))
    return m_new, p.astype(BF16), alpha


def _pipelined_tiles(scores, values, offsets, s_ref, p_ref, first, n_ch, n_tiles, j_first):
    chains = range(n_ch)

    def qk_into(slot, j):
        sc = scores(j)
        for c in chains:
            s_ref[slot, c] = sc[c]

    def pv_from(slot, j, alphas, accs):
        out = []
        for c, vt in zip(chains, values(j)):
            vt1 = jnp.concatenate([vt, jnp.ones((ONES_ROWS, vt.shape[1]), BF16)], axis=0)
            out.append(alphas[c] * accs[c] + jnp.dot(vt1, p_ref[slot, c], preferred_element_type=F32))
        return out

    def softmax_into(slot, j, ms):
        cs = offsets(j)
        new = [_softmax_stage(s_ref[slot, c], cs[c], ms[c]) for c in chains]
        for c in chains:
            p_ref[slot, c] = new[c][1]
        return [n[0] for n in new], [n[2] for n in new]

    qk_into(0, 0)
    first = first()
    for c in chains:
        p_ref[1, c] = first[c][1]

    def pair(i, carry):
        ms, alphas, accs, j_prev = carry
        t = 2 * i
        qk_into(1, t + 1)
        accs = pv_from(1, j_prev, alphas, accs)
        ms, alphas = softmax_into(0, t, ms)
        qk_into(0, t + 2)
        accs = pv_from(0, t, alphas, accs)
        ms, alphas = softmax_into(1, t + 1, ms)
        return ms, alphas, accs, t + 1

    n_q = first[0][0].shape[1]
    init = ([f[0] for f in first], [f[2] for f in first],
            [jnp.zeros((HEAD_DIM + ONES_ROWS, n_q), F32)] * n_ch, j_first)
    _, alphas, accs, j_last = lax.fori_loop(0, (n_tiles + 1) // 2, pair, init)
    accs = pv_from(1, j_last, alphas, accs)
    return [accs[c][:HEAD_DIM] / jnp.maximum(accs[c][HEAD_DIM:HEAD_DIM + 1], 1e-30) for c in chains]


def _moba_kernel(qT_ref, k_ref, vT_ref, aug_ref, srow_ref, sl_ref, o_ref,
                 kmean_ref, kparts_ref, s_ref, p_ref, *, nb, nbp, topk):
    qi = pl.program_id(2)
    blk = MOBA_BLOCK

    @pl.when(qi == 0)
    def _():
        kmean_ref[...] = jnp.zeros(kmean_ref.shape, F32)

        def body(n, carry):
            kb = k_ref[0, pl.ds(pl.multiple_of(n * blk, blk), blk), :].astype(F32)
            kmean_ref[pl.ds(n, 1), :] = jnp.mean(kb, axis=0, keepdims=True)
            return carry
        lax.fori_loop(0, nb, body, 0)
        km = kmean_ref[...]
        head = lax.broadcasted_iota(I32, km.shape, 1) >> 6
        km2 = jnp.concatenate([jnp.where(head == h, km, 0.0) for h in range(2)], axis=0)
        hi = km2.astype(BF16)
        mid = (km2 - hi.astype(F32)).astype(BF16)
        lo = (km2 - hi.astype(F32) - mid.astype(F32)).astype(BF16)
        kparts_ref[0] = hi
        kparts_ref[1] = mid
        kparts_ref[2] = lo

    qT = qT_ref[0]
    row = lax.broadcasted_iota(I32, qT.shape, 0)
    qpad = [jnp.where((row >> 6) == h, qT, jnp.zeros_like(qT)) for h in range(2)]

    gates = (jnp.dot(kparts_ref[0], qT, preferred_element_type=F32)
             + jnp.dot(kparts_ref[1], qT, preferred_element_type=F32)
             + jnp.dot(kparts_ref[2], qT, preferred_element_type=F32))
    bidx = lax.broadcasted_iota(I32, (nbp, blk), 0)
    rhs = []
    for h in range(2):
        gate = gates[h * nbp:(h + 1) * nbp]
        gsc = jnp.where(bidx < qi, gate, -jnp.inf)
        bias = jnp.full((nbp, blk), NEG_BIG, F32)
        for _ in range(topk):
            mx = jnp.max(gsc, axis=0, keepdims=True)
            idx = jnp.min(jnp.where(gsc == mx, bidx, nbp), axis=0, keepdims=True)
            pick = jnp.logical_and(bidx == idx, mx > -jnp.inf)
            bias = jnp.where(pick, 0.0, bias)
            gsc = jnp.where(pick, -jnp.inf, gsc)
        pad = jnp.zeros((2 * HEAD_DIM - nbp - 16, blk), BF16)
        rhs.append(jnp.concatenate([qpad[h], bias.astype(BF16), srow_ref[h], pad], axis=0))

    def scores(j, a):
        k0 = pl.multiple_of(j * blk, blk)
        lhs = jnp.concatenate([k_ref[0, pl.ds(k0, blk), :], aug_ref[a]], axis=1)
        return [jnp.dot(lhs, rhs[h], preferred_element_type=F32) for h in range(2)]

    def values(j):
        k0 = pl.multiple_of(j * blk, blk)
        return [vT_ref[0, h * HEAD_DIM:(h + 1) * HEAD_DIM, pl.ds(k0, blk)] for h in range(2)]

    def offsets(j):
        dq = ((qi - j) * blk).astype(F32)
        return [sl_ref[h] * dq for h in range(2)]

    ik = lax.broadcasted_iota(I32, (blk, blk), 0)
    iq = lax.broadcasted_iota(I32, (blk, blk), 1)
    s_own = [jnp.where(ik <= iq, s, NEG_BIG) for s in scores(qi, nb)]
    m0 = jnp.full((1, blk), NEG_BIG, F32)

    def first():
        return [_softmax_stage(s_own[h], jnp.zeros((1, blk), F32), m0) for h in range(2)]

    outs = _pipelined_tiles(lambda j: scores(jnp.minimum(j, nb - 1), jnp.minimum(j, nb - 1)),
                            values, offsets, s_ref, p_ref, first, 2, qi, qi)
    o_ref[0] = jnp.concatenate(outs, axis=0).T.astype(BF16)


def _moba(mqT, mk, mvT, aug, srow, sl):
    B, _, S = mqT.shape
    blk = MOBA_BLOCK
    nb = S // blk
    topk = min(MOBA_TOPK, nb)
    nbp = -(-nb // 16) * 16
    grid = (B, MOBA_HEADS // 2, nb)
    return pl.pallas_call(
        functools.partial(_moba_kernel, nb=nb, nbp=nbp, topk=topk), grid=grid,
        in_specs=[
            pl.BlockSpec((1, 128, blk), lambda b, p, i: (b, p, i)),
            pl.BlockSpec((1, S, 128), lambda b, p, i: (b, 0, p)),
            pl.BlockSpec((1, 128, S), lambda b, p, i: (b, p, 0)),
            pl.BlockSpec(aug.shape, lambda b, p, i: (0, 0, 0)),
            pl.BlockSpec((2, 16, blk), lambda b, p, i: (p, 0, 0)),
            pl.BlockSpec((2, 1, blk), lambda b, p, i: (p, 0, 0)),
        ],
        out_specs=pl.BlockSpec((1, blk, 128), lambda b, p, i: (b, i, p)),
        out_shape=jax.ShapeDtypeStruct((B, S, 512), BF16),
        scratch_shapes=[
            pltpu.VMEM((nbp, 128), F32),
            pltpu.VMEM((3, 2 * nbp, 128), BF16),
            pltpu.VMEM((2, 2, blk, blk), F32),
            pltpu.VMEM((2, 2, blk, blk), BF16),
        ],
        compiler_params=_params(3), name="moba",
    )(mqT, mk, mvT, aug, srow, sl)


def _compress_kernel(kc_ref, vc_ref, wk_ref, wvT_ref, pek_ref, pev_ref, w2k_ref, w2vT_ref,
                     kcmp_ref, vcmpT_ref):
    nc = kc_ref.shape[1]

    wk = wk_ref[...]
    ab = jnp.dot(kc_ref[0], wk, preferred_element_type=F32)
    pt = (jnp.dot(pek_ref[0], wk[:, 0:128].astype(F32), preferred_element_type=F32)
          + jnp.dot(pek_ref[1], wk[:, 128:256].astype(F32), preferred_element_type=F32))
    pre = ab[:, 0:128] + pltpu.roll(ab[:, 128:256], nc - 1, 0) + pt[0:1]
    hid = jax.nn.gelu(pre)
    kcmp_ref[0] = jnp.dot(hid.astype(BF16), w2k_ref[...], preferred_element_type=F32).astype(BF16)

    wvT = wvT_ref[...]
    abT = lax.dot_general(wvT, vc_ref[0], NT_DIMS, preferred_element_type=F32)
    ptT = (lax.dot_general(wvT[0:128].astype(F32), pev_ref[0], NT_DIMS, preferred_element_type=F32)
           + lax.dot_general(wvT[128:256].astype(F32), pev_ref[1], NT_DIMS, preferred_element_type=F32))
    preT = abT[0:128] + pltpu.roll(abT[128:256], nc - 1, 1) + ptT[:, 0:1]
    hidT = jax.nn.gelu(preT)
    vcmpT_ref[0] = jnp.dot(w2vT_ref[...], hidT.astype(BF16), preferred_element_type=F32).astype(BF16)


def _compress(kc2, vc2, wk, wvT, pek, pev, w2k, w2vT):
    B, nc, _ = kc2.shape
    full = lambda a: pl.BlockSpec(a.shape, lambda b: (0,) * a.ndim)
    blk = pl.BlockSpec((1, nc, kc2.shape[2]), lambda b: (b, 0, 0))
    return pl.pallas_call(
        _compress_kernel, grid=(B,),
        in_specs=[blk, blk, full(wk), full(wvT), full(pek), full(pev), full(w2k), full(w2vT)],
        out_specs=[pl.BlockSpec((1, nc, 128), lambda b: (b, 0, 0)),
                   pl.BlockSpec((1, 128, nc), lambda b: (b, 0, 0))],
        out_shape=[jax.ShapeDtypeStruct((B, nc, 128), BF16),
                   jax.ShapeDtypeStruct((B, 128, nc), BF16)],
        compiler_params=_params(1), name="nsa_compress",
    )(kc2, vc2, wk, wvT, pek, pev, w2k, w2vT)


def _nsa_kernel(qT_ref, kcmp_ref, vcmpT_ref, ks_ref, vsT_ref, kw_ref, vwT_ref,
                g_ref, sl_ref, srow_ref, auga_ref, augb_ref, augc_ref, tblc_ref, bw_ref, o_ref,
                s_ref, p_ref, pc_ref, *, n_slc, topn):
    g = pl.program_id(1)
    qi = pl.program_id(2)
    tq = NSA_TQ
    hg = NSA_GROUP
    wl = hg * tq
    q0 = qi * tq

    q4 = qT_ref[0]
    qT = jnp.concatenate([q4[h * HEAD_DIM:(h + 1) * HEAD_DIM] for h in range(hg)], axis=1)
    qT2 = jnp.concatenate([qT, qT], axis=0)
    rowi = lax.broadcasted_iota(I32, qT2.shape, 0)
    qpad = jnp.where((rowi >> 6) == g, qT2, jnp.zeros_like(qT2))
    slope = sl_ref[0]
    lane = lax.broadcasted_iota(I32, (1, wl), 1)
    t_q = q0 + (lane & (tq - 1))

    nc = kcmp_ref.shape[1]
    rhs_top = jnp.where((rowi >> 6) == g, qT2, srow_ref[0])
    mine_c = (lax.broadcasted_iota(I32, (nc, LANES), 1) >> 6) == g
    lhs_c = jnp.where(mine_c, kcmp_ref[0], augc_ref[0])
    first_c = pl.multiple_of(nc - qi * (tq // NSA_CMP_STRIDE), 8)
    z = jnp.dot(lhs_c, rhs_top, preferred_element_type=F32) + tblc_ref[pl.ds(first_c, nc), :]
    mx = jnp.max(z, axis=0, keepdims=True)
    e = jnp.exp2(z - mx)
    den = jnp.maximum(jnp.sum(e, axis=0, keepdims=True), 1e-30)
    p = e * jnp.where(t_q >= NSA_CMP_LEN - 1, 1.0 / den, 0.0)
    o_c = jnp.dot(vcmpT_ref[0], p.astype(BF16), preferred_element_type=F32)

    pc = p[:, 0:tq]
    for h in range(1, hg):
        pc = pc + p[:, h * tq:(h + 1) * tq]
    n_lc = tq // LANES
    for c in range(n_lc):
        pc_ref[c] = pc[:, c * LANES:(c + 1) * LANES]
    su = NSA_SLC_BLOCK // NSA_CMP_STRIDE
    x = [jnp.concatenate([pc_ref[c, pl.ds(k, n_slc, stride=su), :] for c in range(n_lc)], axis=1)
         for k in range(su)]
    jb = lax.broadcasted_iota(I32, (n_slc, tq), 0)
    prev = jnp.where(jb == 0, 0.0, pltpu.roll(x[3], 1, 0))
    imp = 2.0 * (x[0] + x[1] + x[2]) + x[3] + prev
    cur = (q0 + lax.broadcasted_iota(I32, (1, tq), 1)) >> 6
    allowed = jb <= cur
    forced = jnp.logical_or(jb == 0, jnp.logical_or(jb == cur, jb == cur - 1))
    bias = jnp.where(jnp.logical_and(allowed, forced), 0.0, NEG_BIG)
    sc = jnp.where(jnp.logical_and(allowed, jnp.logical_not(forced)), imp, -1.0)
    for _ in range(topn - 3):
        smx = jnp.max(sc, axis=0, keepdims=True)
        idx = jnp.min(jnp.where(sc == smx, jb, n_slc), axis=0, keepdims=True)
        pick = jnp.logical_and(jb == idx, smx >= 0.0)
        bias = jnp.where(pick, 0.0, bias)
        sc = jnp.where(pick, -1.0, sc)

    wc = NSA_CHAIN_LANES
    n_ch = wl // wc

    def lane_split(a):
        return [a[:, c * wc:(c + 1) * wc] for c in range(n_ch)]

    if n_slc < LANES:
        bias = jnp.concatenate([bias, jnp.zeros((LANES - n_slc, tq), F32)], axis=0)
    bias4 = jnp.concatenate([bias.astype(BF16)] * hg, axis=1)
    rhs = jnp.concatenate([rhs_top, bias4], axis=0)
    mine = (lax.broadcasted_iota(I32, (SLC_TILE, LANES), 1) >> 6) == g
    n_tiles = augb_ref.shape[0]

    def scores(j, null):
        k0 = pl.multiple_of(j * SLC_TILE, SLC_TILE)
        kt = ks_ref[0, pl.ds(k0, SLC_TILE), :]
        lhs = jnp.concatenate([jnp.where(mine, kt, auga_ref[null]), augb_ref[j]], axis=1)
        return jnp.dot(lhs, rhs, preferred_element_type=F32)

    def values(j):
        return [vsT_ref[0, :, pl.ds(pl.multiple_of(j * SLC_TILE, SLC_TILE), SLC_TILE)]] * n_ch

    def offsets(j):
        return lane_split(slope * (q0 - j * SLC_TILE).astype(F32))

    jd = lax.div(q0, SLC_TILE)
    t_k = jd * SLC_TILE + lax.broadcasted_iota(I32, (SLC_TILE, wl), 0)
    s_diag = lane_split(jnp.where(t_k <= t_q, scores(jd, 0), NEG_BIG))
    c_diag = offsets(jd)

    def first():
        return [_softmax_stage(s_diag[c], c_diag[c], jnp.full((1, wc), NEG_BIG, F32))
                for c in range(n_ch)]

    def past_scores(j):
        return lane_split(scores(jnp.minimum(j, n_tiles - 1), (j >= jd).astype(I32)))
    o_s = jnp.concatenate(
        _pipelined_tiles(past_scores, values, offsets, s_ref, p_ref, first, n_ch, jd, jd), axis=1)

    start = pl.multiple_of(jnp.maximum(q0 - NSA_WINDOW, 0), tq)
    first_w = pl.multiple_of(NSA_WINDOW - (q0 - start), tq)
    kt = kw_ref[0, pl.ds(start, WIN_KEYS), :]
    z = jnp.dot(kt, qpad, preferred_element_type=F32) + bw_ref[0, pl.ds(first_w, WIN_KEYS), :]
    mx = jnp.max(z, axis=0, keepdims=True)
    p = jnp.exp2(z - mx)
    den = jnp.maximum(jnp.sum(p, axis=0, keepdims=True), 1e-30)
    o_w = jnp.dot(vwT_ref[0, :, pl.ds(start, WIN_KEYS)], p.astype(BF16),
                  preferred_element_type=F32) / den

    gt = g_ref[0]

    def gate_row(br):
        return jnp.concatenate([gt[br * hg + h:br * hg + h + 1] for h in range(hg)], axis=1)

    o = gate_row(0) * o_c + gate_row(1) * o_s + gate_row(2) * o_w
    o4 = jnp.concatenate([o[:, h * tq:(h + 1) * tq] for h in range(hg)], axis=0)
    o_ref[0] = o4.T.astype(BF16)


def _nsa(nqT, kcmp, vcmpT, ks, vsT, kw, vwT, gT, sl, srow, auga, augb, augc, tblc, bw):
    B, _, S = nqT.shape
    tq = NSA_TQ
    nc = kcmp.shape[1]
    n_slc = S // NSA_SLC_BLOCK
    topn = min(NSA_SLC_TOPN, n_slc)
    wl = NSA_GROUP * tq
    grid = (B, NSA_KV_HEADS, S // tq)
    return pl.pallas_call(
        functools.partial(_nsa_kernel, n_slc=n_slc, topn=topn), grid=grid,
        in_specs=[
            pl.BlockSpec((1, NSA_GROUP * HEAD_DIM, tq), lambda b, g, i: (b, g, i)),
            pl.BlockSpec((1, nc, 128), lambda b, g, i: (b, 0, 0)),
            pl.BlockSpec((1, HEAD_DIM, nc), lambda b, g, i: (b, g, 0)),
            pl.BlockSpec((1, S, 128), lambda b, g, i: (b, 0, 0)),
            pl.BlockSpec((1, HEAD_DIM, S), lambda b, g, i: (b, g, 0)),
            pl.BlockSpec((1, S, 128), lambda b, g, i: (b, 0, 0)),
            pl.BlockSpec((1, HEAD_DIM, S), lambda b, g, i: (b, g, 0)),
            pl.BlockSpec((1, 16, tq), lambda b, g, i: (b, g, i)),
            pl.BlockSpec((1, 1, wl), lambda b, g, i: (g, 0, 0)),
            pl.BlockSpec((1, 2 * HEAD_DIM, wl), lambda b, g, i: (g, 0, 0)),
            pl.BlockSpec((2, SLC_TILE, LANES), lambda b, g, i: (g, 0, 0)),
            pl.BlockSpec(augb.shape, lambda b, g, i: (0, 0, 0)),
            pl.BlockSpec((1, nc, LANES), lambda b, g, i: (g, 0, 0)),
            pl.BlockSpec(tblc.shape, lambda b, g, i: (0, 0)),
            pl.BlockSpec((1,) + bw.shape[1:], lambda b, g, i: (g, 0, 0)),
        ],
        out_specs=pl.BlockSpec((1, tq, NSA_GROUP * HEAD_DIM), lambda b, g, i: (b, i, g)),
        out_shape=jax.ShapeDtypeStruct((B, S, 512), BF16),
        scratch_shapes=[
            pltpu.VMEM((2, wl // NSA_CHAIN_LANES, SLC_TILE, NSA_CHAIN_LANES), F32),
            pltpu.VMEM((2, wl // NSA_CHAIN_LANES, SLC_TILE, NSA_CHAIN_LANES), BF16),
            pltpu.VMEM((tq // LANES, nc, LANES), F32),
        ],
        compiler_params=_params(3), name="nsa",
    )(nqT, kcmp, vcmpT, ks, vsT, kw, vwT, gT, sl, srow, auga, augb, augc, tblc, bw)


def _outproj_kernel(om_ref, on_ref, x_ref, wo_ref, g_ref, wr_ref, br_ref,
                    x1_ref, hn_ref, e_ref, w_ref):
    attn = (jnp.dot(om_ref[...], wo_ref[0:512, :], preferred_element_type=F32)
            + jnp.dot(on_ref[...], wo_ref[512:1024, :], preferred_element_type=F32))
    x1 = x_ref[...] + attn
    x1_ref[...] = x1
    hn = _rmsnorm(x1, g_ref[...])
    _store_token_tiles(hn_ref, hn)
    logits = jnp.dot(hn, wr_ref[...], precision=lax.Precision.HIGHEST,
                     preferred_element_type=F32) + br_ref[...]
    tm = logits.shape[0]
    lane = lax.broadcasted_iota(I32, (tm, LANES), 1)
    sc = jnp.where(lane < N_EXPERTS, logits, -jnp.inf)
    e_out = jnp.zeros((tm, LANES), I32)
    vals = []
    for k in range(TOP_K):
        mx = jnp.max(sc, axis=1, keepdims=True)
        idx = jnp.min(jnp.where(sc == mx, lane, LANES), axis=1, keepdims=True)
        e_out = jnp.where(lane == k, idx, e_out)
        sc = jnp.where(lane == idx, -jnp.inf, sc)
        vals.append(mx)
    ex = [jnp.exp(v - vals[0]) for v in vals]
    den = ex[0] + ex[1] + ex[2] + ex[3]
    w_out = jnp.zeros((tm, LANES), F32)
    for k in range(TOP_K):
        w_out = jnp.where(lane == k, ex[k] / den, w_out)
    e_ref[...] = e_out
    w_ref[...] = w_out


def _outproj(om, on, x, wo, g, wr, br, tm):
    N, D = x.shape
    full = lambda a: pl.BlockSpec(a.shape, lambda i: (0,) * a.ndim)
    row = lambda w: pl.BlockSpec((tm, w), lambda i: (i, 0))
    return pl.pallas_call(
        _outproj_kernel, grid=(N // tm,),
        in_specs=[row(512), row(512), row(D), full(wo), full(g), full(wr), full(br)],
        out_specs=[row(D), pl.BlockSpec((tm * SUBLANES, LANES), lambda i: (i, 0)), row(LANES), row(LANES)],
        out_shape=[jax.ShapeDtypeStruct((N, D), F32), jax.ShapeDtypeStruct((N * SUBLANES, LANES), F32),
                   jax.ShapeDtypeStruct((N, LANES), I32), jax.ShapeDtypeStruct((N, LANES), F32)],
        compiler_params=_params(1), name="outproj_router",
    )(om, on, x, wo, g, wr, br)


def _rank_kernel(e_ref, rank_ref, cnt_ref, base_ref):
    i = pl.program_id(0)
    T = e_ref.shape[0]

    @pl.when(i == 0)
    def _():
        base_ref[...] = jnp.zeros(base_ref.shape, F32)

    e = e_ref[...]
    lane = lax.broadcasted_iota(I32, (T, LANES), 1)
    tril = jnp.where(lax.broadcasted_iota(I32, (T, T), 0) >= lax.broadcasted_iota(I32, (T, T), 1),
                     1.0, 0.0).astype(BF16)
    out = jnp.zeros((T, LANES), I32)
    for k in range(TOP_K):
        hit = lane == e[:, k:k + 1]
        oh = jnp.where(hit, 1.0, 0.0)
        cum = jnp.dot(tril, oh.astype(BF16), preferred_element_type=F32)
        base = base_ref[0:1, :]
        r = jnp.sum(jnp.where(hit, cum - 1.0 + base, 0.0), axis=1, keepdims=True)
        out = jnp.where(lane == k, r.astype(I32), out)
        base_ref[...] = base_ref[...] + jnp.sum(oh, axis=0, keepdims=True)
    rank_ref[...] = out
    cnt_ref[...] = base_ref[...]


def _ranks(e128):
    N = e128.shape[0]
    T = RANK_TILE
    return pl.pallas_call(
        _rank_kernel, grid=(N // T,),
        in_specs=[pl.BlockSpec((T, LANES), lambda i: (i, 0))],
        out_specs=[pl.BlockSpec((T, LANES), lambda i: (i, 0)),
                   pl.BlockSpec((8, LANES), lambda i: (0, 0))],
        out_shape=[jax.ShapeDtypeStruct((N, LANES), I32),
                   jax.ShapeDtypeStruct((8, LANES), F32)],
        scratch_shapes=[pltpu.VMEM((8, LANES), F32)],
        compiler_params=_params(1), name="route_ranks",
    )(e128)


def _row_copy(src, dst, i_src, i_dst, sem):
    return pltpu.make_async_copy(src.at[pl.ds(pl.multiple_of(i_src * SUBLANES, SUBLANES), SUBLANES)],
                                 dst.at[pl.ds(pl.multiple_of(i_dst * SUBLANES, SUBLANES), SUBLANES)], sem)


def _store_token_tiles(ref, x):
    rows = x.shape[0]
    for c in range(SUBLANES):
        ref[pl.ds(c, rows, stride=SUBLANES), :] = x[:, c * LANES:(c + 1) * LANES]


def _load_token_tiles(ref, rows):
    return jnp.concatenate([ref[pl.ds(c, rows, stride=SUBLANES), :] for c in range(SUBLANES)], axis=1)


def _dispatch_kernel(dest_hbm, hp_ref, xz_hbm, out_hbm, idx_ref, isem, sem):
    del xz_hbm
    i = pl.program_id(0)
    T = ROUTE_TILE
    cp = pltpu.make_async_copy(dest_hbm.at[i], idx_ref, isem)
    cp.start()
    cp.wait()

    def issue(t, carry):
        for k in range(TOP_K):
            _row_copy(hp_ref, out_hbm, t, idx_ref[t * TOP_K + k], sem).start()
        return carry
    lax.fori_loop(0, T, issue, 0, unroll=ROW_DMA_UNROLL)

    def drain(t, carry):
        for k in range(TOP_K):
            _row_copy(hp_ref, out_hbm, 0, 0, sem).wait()
        return carry
    lax.fori_loop(0, T, drain, 0, unroll=ROW_DMA_UNROLL)


def _dispatch(dest2, hp, xzero):
    nsteps = dest2.shape[0]
    T = ROUTE_TILE
    return pl.pallas_call(
        _dispatch_kernel, grid=(nsteps,),
        in_specs=[pl.BlockSpec(memory_space=pl.ANY),
                  pl.BlockSpec((T * SUBLANES, LANES), lambda i: (i, 0)),
                  pl.BlockSpec(memory_space=pl.ANY)],
        out_specs=pl.BlockSpec(memory_space=pl.ANY),
        out_shape=jax.ShapeDtypeStruct(xzero.shape, xzero.dtype),
        scratch_shapes=[pltpu.SMEM((T * TOP_K,), I32),
                        pltpu.SemaphoreType.DMA, pltpu.SemaphoreType.DMA],
        input_output_aliases={2: 0},
        compiler_params=_params(1), name="moe_dispatch",
    )(dest2, hp, xzero)


def _expert_kernel(be_ref, na_ref, x_ref, wg_ref, wu_ref, bg_ref, bu_ref, wd_ref, bd_ref, y_ref):
    b = pl.program_id(0)
    active = b < na_ref[0]

    @pl.when(active)
    def _():
        xb = _load_token_tiles(x_ref, MOE_ROWS).astype(BF16)
        gg = lax.dot_general(xb, wg_ref[0], NT_DIMS, preferred_element_type=F32) + bg_ref[0]
        uu = lax.dot_general(xb, wu_ref[0], NT_DIMS, preferred_element_type=F32) + bu_ref[0]
        gg = jnp.minimum(gg, SWIGLU_LIMIT)
        uu = jnp.clip(uu, -SWIGLU_LIMIT, SWIGLU_LIMIT)
        a = gg * jax.nn.sigmoid(SWIGLU_ALPHA * gg) * (uu + 1.0)
        _store_token_tiles(y_ref, jnp.dot(a.astype(BF16), wd_ref[0], preferred_element_type=F32) + bd_ref[0])

    @pl.when(jnp.logical_not(active))
    def _():
        y_ref[...] = jnp.zeros(y_ref.shape, F32)


def _experts(blk_e, n_act, xrows, wgT, wuT, bg, bu, wd, bd):
    _, F, D = wgT.shape
    assert D == SUBLANES * LANES
    P = xrows.shape[0] // SUBLANES
    n_blk = P // MOE_ROWS
    rows_spec = pl.BlockSpec((MOE_ROWS * SUBLANES, LANES), lambda b, be, na: (b, 0))
    wspec = lambda r, c: pl.BlockSpec((1, r, c), lambda b, be, na: (be[b], 0, 0))
    grid_spec = pltpu.PrefetchScalarGridSpec(
        num_scalar_prefetch=2, grid=(n_blk,),
        in_specs=[rows_spec, wspec(F, D), wspec(F, D), wspec(1, F), wspec(1, F), wspec(F, D), wspec(1, D)],
        out_specs=rows_spec,
    )
    return pl.pallas_call(
        _expert_kernel, grid_spec=grid_spec,
        out_shape=jax.ShapeDtypeStruct(xrows.shape, F32),
        compiler_params=_params(1), name="moe_experts",
    )(blk_e, n_act, xrows, wgT, wuT, bg, bu, wd, bd)


def _combine_kernel(dest_hbm, x1_ref, w_ref, g_ref, y_hbm, o_ref, idx_ref, buf_ref, isem, sem):
    i = pl.program_id(0)
    T = ROUTE_TILE
    cp = pltpu.make_async_copy(dest_hbm.at[i], idx_ref, isem)
    cp.start()
    cp.wait()

    def issue(t, carry):
        for k in range(TOP_K):
            _row_copy(y_hbm, buf_ref.at[k], idx_ref[t * TOP_K + k], t, sem).start()
        return carry
    lax.fori_loop(0, T, issue, 0, unroll=ROW_DMA_UNROLL)

    def drain(t, carry):
        for k in range(TOP_K):
            _row_copy(y_hbm, buf_ref.at[k], 0, 0, sem).wait()
        return carry
    lax.fori_loop(0, T, drain, 0, unroll=ROW_DMA_UNROLL)

    x1 = x1_ref[...]
    w = w_ref[...]
    cols = []
    for c in range(SUBLANES):
        acc = x1[:, c * LANES:(c + 1) * LANES]
        for k in range(TOP_K):
            acc = acc + w[:, k:k + 1] * buf_ref[k, pl.ds(c, T, stride=SUBLANES), :]
        cols.append(acc)
    o_ref[...] = _rmsnorm(jnp.concatenate(cols, axis=1), g_ref[...])


def _combine(dest2, x1, w128, g, yrows):
    N, D = x1.shape
    T = ROUTE_TILE
    return pl.pallas_call(
        _combine_kernel, grid=(N // T,),
        in_specs=[pl.BlockSpec(memory_space=pl.ANY),
                  pl.BlockSpec((T, D), lambda i: (i, 0)),
                  pl.BlockSpec((T, LANES), lambda i: (i, 0)),
                  pl.BlockSpec(g.shape, lambda i: (0, 0)),
                  pl.BlockSpec(memory_space=pl.ANY)],
        out_specs=pl.BlockSpec((T, D), lambda i: (i, 0)),
        out_shape=jax.ShapeDtypeStruct((N, D), F32),
        scratch_shapes=[pltpu.SMEM((T * TOP_K,), I32),
                        pltpu.VMEM((TOP_K, T * SUBLANES, LANES), F32),
                        pltpu.SemaphoreType.DMA, pltpu.SemaphoreType.DMA],
        compiler_params=_params(1), name="moe_combine",
    )(dest2, x1, w128, g, yrows)


def _alibi_slopes():
    n = MOBA_HEADS + NSA_HEADS
    s = jnp.exp2(-8.0 * jnp.arange(1, n + 1, dtype=F32) / n)
    return s[0::2], s[1::2]


def _prep_inproj(w_in):
    hd = HEAD_DIM
    sizes = [MOBA_HEADS * hd] * 3 + [NSA_HEADS * hd] + [NSA_KV_HEADS * hd] * 6 + [NSA_BRANCHES * NSA_HEADS]
    cuts = np.cumsum([0] + sizes)
    mq, mk, mv, nq, kc, vc, ks, vs, kw, vw, ng = [w_in[:, cuts[i]:cuts[i + 1]] for i in range(11)]
    qscale = (hd ** -0.5) * LOG2E
    wr = jnp.concatenate([mk, kc, vc, ks, kw], axis=1).astype(BF16)
    wt = jnp.concatenate([mq * qscale, mv, nq * qscale, vs, vw], axis=1).T.astype(BF16)
    ngr = ng.reshape(-1, NSA_KV_HEADS, NSA_GROUP, NSA_BRANCHES).transpose(1, 3, 2, 0)
    ngr = ngr.reshape(NSA_KV_HEADS, NSA_BRANCHES * NSA_GROUP, -1)
    wg = jnp.pad(ngr, ((0, 0), (0, 16 - NSA_BRANCHES * NSA_GROUP), (0, 0))).reshape(32, -1)
    return wr, wt, wg.astype(F32)


def _prep_compress(w1, w2, pe):
    hd, half = HEAD_DIM, NSA_CMP_STRIDE
    w1r = w1.reshape(2, half, hd, hd)
    eye = jnp.eye(NSA_KV_HEADS, dtype=w1.dtype)
    w = jnp.einsum('alde,gh->lgdahe', w1r, eye).reshape(half * NSA_KV_HEADS * hd, 2 * NSA_KV_HEADS * hd)
    w2b = jnp.einsum('de,gh->gdhe', w2, eye).reshape(NSA_KV_HEADS * hd, NSA_KV_HEADS * hd)
    per = pe.reshape(2, half, 1, hd)
    pe2 = jnp.broadcast_to(per, (2, half, NSA_KV_HEADS, hd)).reshape(2, 1, half * NSA_KV_HEADS * hd)
    pe2 = jnp.broadcast_to(pe2, (2, 8, pe2.shape[2]))
    return w.astype(BF16), w2b.astype(BF16), pe2.astype(F32)


def _attention_tables(S):
    moba_sl, nsa_sl = _alibi_slopes()
    moba_sl = moba_sl * LOG2E
    nsa_sl = nsa_sl * LOG2E
    blk = MOBA_BLOCK
    moba_row = jnp.broadcast_to(moba_sl[:, None, None], (MOBA_HEADS, 1, blk))
    nb = S // blk
    nbp = -(-nb // 16) * 16
    col = jnp.arange(LANES)[None, None, :]
    tile = jnp.arange(nb + 1)[:, None, None]
    off = jnp.arange(blk, dtype=F32)[None, :, None]
    moba_aug = jnp.where(jnp.logical_and(col == tile, tile < nb), 1.0,
                         jnp.where(jnp.logical_and(col >= nbp, col < nbp + 3), off, 0.0)).astype(BF16)
    parts = jnp.stack(list(_split3(moba_sl)) + [jnp.zeros_like(moba_sl)] * 13, axis=1)
    moba_srow = jnp.broadcast_to(parts[:, :, None], (MOBA_HEADS, 16, blk)).astype(BF16)

    wl = NSA_GROUP * NSA_TQ
    n_slc = S // NSA_SLC_BLOCK
    assert n_slc <= LANES, "block-choice rows must fit the spare contraction rows"
    nsa_row = jnp.repeat(nsa_sl.reshape(NSA_KV_HEADS, NSA_GROUP), NSA_TQ, axis=1)
    hi, mid, lo = [t[:, None, :] for t in _split3(nsa_row)]
    base = ((1 - jnp.arange(NSA_KV_HEADS)) * HEAD_DIM)[:, None, None]
    rows = jnp.arange(2 * HEAD_DIM)[None, :, None]
    nsa_srow = jnp.where(rows == base, hi, jnp.where(rows == base + 1, mid, jnp.where(
        rows == base + 2, lo, jnp.where(rows == base + 3, NEG_BIG, 0.0)))).astype(BF16)
    lane = jnp.arange(LANES)[None, None, None, :]
    base4 = base[:, None]
    null = jnp.arange(2, dtype=F32)[None, :, None, None]
    koff = jnp.arange(SLC_TILE, dtype=F32)[None, None, :, None]
    nsa_auga = jnp.where(jnp.logical_and(lane >= base4, lane < base4 + 3), koff,
                         jnp.where(lane == base4 + 3, null, 0.0))
    nsa_auga = nsa_auga.reshape(2 * NSA_KV_HEADS, SLC_TILE, LANES).astype(BF16)
    per = SLC_TILE // NSA_SLC_BLOCK
    tile = jnp.arange(S // SLC_TILE)[:, None, None]
    blk_of = tile * per + jnp.arange(SLC_TILE)[None, :, None] // NSA_SLC_BLOCK
    nsa_augb = (jnp.arange(LANES)[None, None, :] == blk_of).astype(BF16)

    nc = S // NSA_CMP_STRIDE
    ci = jnp.arange(nc)[None, :, None]
    lane3 = jnp.arange(LANES)[None, None, :]
    nsa_augc = jnp.where(jnp.logical_and(lane3 >= base + 4, lane3 < base + 7), (ci >> 1).astype(F32),
                         jnp.where(jnp.logical_and(lane3 >= base + 7, lane3 < base + 10),
                                   (ci & 1).astype(F32), 0.0)).astype(BF16)
    step2 = [t[:, None, :] for t in _split3(nsa_row * (2.0 * NSA_CMP_STRIDE))]
    step1 = [t[:, None, :] for t in _split3(nsa_row * (1.0 * NSA_CMP_STRIDE))]
    for k in range(3):
        nsa_srow = jnp.where(rows == base + 4 + k, step2[k].astype(BF16),
                             jnp.where(rows == base + 7 + k, step1[k].astype(BF16), nsa_srow))
    il = jnp.tile(jnp.arange(NSA_TQ), NSA_GROUP)[None, :]
    rel = (jnp.arange(2 * nc) - nc)[:, None]
    nsa_tblc = jnp.where(rel * NSA_CMP_STRIDE + (NSA_CMP_LEN - 1) <= il, 0.0, NEG_BIG).astype(F32)
    dist = (NSA_WINDOW + il - jnp.arange(NSA_WINDOW + WIN_KEYS)[:, None])[None]
    nsa_bw = jnp.where(jnp.logical_and(dist >= 0, dist < NSA_WINDOW),
                       -nsa_row[:, None, :] * dist.astype(F32), NEG_BIG)
    return ((moba_aug, moba_srow, moba_row),
            (nsa_row.reshape(NSA_KV_HEADS, 1, wl), nsa_srow, nsa_auga, nsa_augb, nsa_augc, nsa_tblc, nsa_bw))


def _split3(x):
    hi = x.astype(BF16).astype(F32)
    mid = (x - hi).astype(BF16).astype(F32)
    lo = (x - hi - mid).astype(BF16).astype(F32)
    return hi, mid, lo


def _attention(x, attn_norm_g, w_in, cmp_pe_k, cmp_pe_v, cmp_w1_k, cmp_w2_k, cmp_w1_v, cmp_w2_v):
    B, S, D = x.shape
    wr, wt, wg = _prep_inproj(w_in)
    (mk, kc, vc, ks, kw, mqT, mvT, nqT, vsT, vwT, gT) = _inproj(
        x, attn_norm_g.reshape(1, D), wr, wt, wg, tm=512)
    moba_tabs, nsa_tabs = _attention_tables(S)
    o_moba = _moba(mqT, mk, mvT, *moba_tabs)
    wk, w2k, pek = _prep_compress(cmp_w1_k, cmp_w2_k, cmp_pe_k)
    wv, w2v, pev = _prep_compress(cmp_w1_v, cmp_w2_v, cmp_pe_v)
    nc = S // NSA_CMP_STRIDE
    kcmp, vcmpT = _compress(kc.reshape(B, nc, -1), vc.reshape(B, nc, -1), wk, wv.T, pek, pev, w2k, w2v.T)
    o_nsa = _nsa(nqT, kcmp, vcmpT, ks, vsT, kw, vwT, gT, *nsa_tabs)
    return o_moba, o_nsa


def _moe(x1, hn, e128, w128, w_up, b_up, w_down, b_down, final_norm_g):
    N, D = x1.shape
    rank128, cnt = _ranks(e128)
    counts = cnt[0, :N_EXPERTS].astype(I32)
    padded = (counts + MOE_ROWS - 1) // MOE_ROWS * MOE_ROWS
    pends = jnp.cumsum(padded)
    pstarts = pends - padded
    e4 = e128[:, :TOP_K]
    dest = pstarts[e4] + rank128[:, :TOP_K]
    dest2 = dest.reshape(N // ROUTE_TILE, ROUTE_TILE * TOP_K)
    n_blk = (N * TOP_K + N_EXPERTS * MOE_ROWS + MOE_ROWS - 1) // MOE_ROWS
    P = n_blk * MOE_ROWS
    blk_start = jnp.arange(n_blk, dtype=I32) * MOE_ROWS
    blk_e = jnp.minimum(jnp.sum((pends[None, :] <= blk_start[:, None]).astype(I32), axis=1), N_EXPERTS - 1)
    n_act = (pends[-1:] // MOE_ROWS).astype(I32)
    xrows = _dispatch(dest2, hn, jnp.zeros((P * SUBLANES, LANES), F32))
    wT = jnp.transpose(w_up.reshape(N_EXPERTS, D, -1, 2), (3, 0, 2, 1)).astype(BF16)
    bg = b_up[:, None, 0::2]
    bu = b_up[:, None, 1::2]
    yrows = _experts(blk_e, n_act, xrows, wT[0], wT[1], bg, bu,
                     w_down.astype(BF16), b_down[:, None, :])
    return _combine(dest2, x1, w128, final_norm_g.reshape(1, D), yrows)


def kernel(x, attn_norm_g, w_in, cmp_pe_k, cmp_pe_v, cmp_w1_k, cmp_w2_k, cmp_w1_v, cmp_w2_v, w_out, ffn_norm_g, w_router, b_router, w_up, b_up, w_down, b_down, final_norm_g):
    B, S, D = x.shape
    assert attn_norm_g.shape[0] == 1, "single-layer kernel"
    o_moba, o_nsa = _attention(x, attn_norm_g[0], w_in[0], cmp_pe_k[0], cmp_pe_v[0],
                               cmp_w1_k[0], cmp_w2_k[0], cmp_w1_v[0], cmp_w2_v[0])
    N = B * S
    wr = jnp.pad(w_router[0], ((0, 0), (0, LANES - N_EXPERTS)))
    br = jnp.pad(b_router[0], (0, LANES - N_EXPERTS)).reshape(1, LANES)
    x1, hn, e128, w128 = _outproj(o_moba.reshape(N, -1), o_nsa.reshape(N, -1), x.reshape(N, D),
                                  w_out[0].astype(BF16), ffn_norm_g[0].reshape(1, D), wr, br, tm=512)
    out = _moe(x1, hn, e128, w128, w_up[0], b_up[0], w_down[0], b_down[0], final_norm_g)
    return out.reshape(B, S, D)
```

```python
import functools

import jax
import jax.numpy as jnp
import numpy as np
from jax import lax
from jax.experimental import pallas as pl
from jax.experimental.pallas import tpu as pltpu

F32 = jnp.float32
BF16 = jnp.bfloat16
I32 = jnp.int32

HEAD_DIM = 64
MOBA_HEADS = 8
NSA_HEADS = 8
NSA_KV_HEADS = 2
NSA_GROUP = NSA_HEADS // NSA_KV_HEADS
MOBA_BLOCK = 256
MOBA_TOPK = 3
NSA_CMP_LEN = 32
NSA_CMP_STRIDE = 16
NSA_SLC_BLOCK = 64
NSA_SLC_TOPN = 16
NSA_WINDOW = 512
NSA_BRANCHES = 3
N_EXPERTS = 32
TOP_K = 4
SWIGLU_LIMIT = 7.0
SWIGLU_ALPHA = 1.702
RMS_EPS = 1e-5
NEG_BIG = -1e30
LOG2E = 1.4426950408889634

LANES = 128
SUBLANES = 8
VMEM_LIMIT = 56 * 1024 * 1024

NSA_TQ = 256
NSA_CHAIN_LANES = 256
SLC_TILE = 256
WIN_KEYS = NSA_WINDOW + NSA_TQ
MOE_ROWS = 512
ROUTE_TILE = 256
RANK_TILE = 512
ROW_DMA_UNROLL = 8

NT_DIMS = (((1,), (1,)), ((), ()))


def _params(n_grid):
    return pltpu.CompilerParams(
        dimension_semantics=("arbitrary",) * n_grid,
        vmem_limit_bytes=VMEM_LIMIT,
    )


def _rmsnorm(x, g):
    return x * lax.rsqrt(jnp.mean(x * x, axis=-1, keepdims=True) + RMS_EPS) * g


def _inproj_kernel(x_ref, g_ref, wr_ref, wt_ref, wg_ref,
                   mk_ref, kc_ref, vc_ref, ks_ref, kw_ref,
                   mqT_ref, mvT_ref, nqT_ref, vsT_ref, vwT_ref, gT_ref):
    xn = _rmsnorm(x_ref[0], g_ref[...])
    xb = xn.astype(BF16)
    yr = jnp.dot(xb, wr_ref[...], preferred_element_type=F32)
    mk_ref[0] = yr[:, 0:512].astype(BF16)
    kc_ref[0] = yr[:, 512:640].astype(BF16)
    vc_ref[0] = yr[:, 640:768].astype(BF16)
    ks_ref[0] = yr[:, 768:896].astype(BF16)
    kw_ref[0] = yr[:, 896:1024].astype(BF16)
    yt = lax.dot_general(wt_ref[...], xb, NT_DIMS, preferred_element_type=F32)
    mqT_ref[0] = yt[0:512].astype(BF16)
    mvT_ref[0] = yt[512:1024].astype(BF16)
    nqT_ref[0] = yt[1024:1536].astype(BF16)
    vsT_ref[0] = yt[1536:1664].astype(BF16)
    vwT_ref[0] = yt[1664:1792].astype(BF16)
    gl = lax.dot_general(wg_ref[...], xn, NT_DIMS, precision=lax.Precision.HIGHEST,
                         preferred_element_type=F32)
    gT_ref[0] = jax.nn.sigmoid(gl)


def _inproj(x, g, wr, wt, wg, tm):
    B, S, D = x.shape
    grid = (B, S // tm)
    row = lambda w: pl.BlockSpec((1, tm, w), lambda b, i: (b, i, 0))
    col = lambda h: pl.BlockSpec((1, h, tm), lambda b, i: (b, 0, i))
    full = lambda a: pl.BlockSpec(a.shape, lambda b, i: (0,) * a.ndim)
    out_shape = [
        jax.ShapeDtypeStruct((B, S, 512), BF16),
        jax.ShapeDtypeStruct((B, S, 128), BF16),
        jax.ShapeDtypeStruct((B, S, 128), BF16),
        jax.ShapeDtypeStruct((B, S, 128), BF16),
        jax.ShapeDtypeStruct((B, S, 128), BF16),
        jax.ShapeDtypeStruct((B, 512, S), BF16),
        jax.ShapeDtypeStruct((B, 512, S), BF16),
        jax.ShapeDtypeStruct((B, 512, S), BF16),
        jax.ShapeDtypeStruct((B, 128, S), BF16),
        jax.ShapeDtypeStruct((B, 128, S), BF16),
        jax.ShapeDtypeStruct((B, 32, S), F32),
    ]
    out_specs = [row(512), row(128), row(128), row(128), row(128),
                 col(512), col(512), col(512), col(128), col(128), col(32)]
    return pl.pallas_call(
        _inproj_kernel, grid=grid,
        in_specs=[pl.BlockSpec((1, tm, D), lambda b, i: (b, i, 0)),
                  full(g), full(wr), full(wt), full(wg)],
        out_specs=out_specs, out_shape=out_shape,
        compiler_params=_params(2), name="inproj",
    )(x, g, wr, wt, wg)


ONES_ROWS = 16


def _softmax_stage(s, c, m):
    mt = jnp.max(s, axis=0, keepdims=True) - c
    m_new = jnp.maximum(m, mt)
    alpha = jnp.exp2(m - m_new)
    p = jnp.exp2(s - (m_new + c))
    return m_new, p.astype(BF16), alpha


def _pipelined_tiles(scores, values, offsets, s_ref, p_ref, first, n_ch, n_tiles, j_first):
    chains = range(n_ch)

    def qk_into(slot, j):
        sc = scores(j)
        for c in chains:
            s_ref[slot, c] = sc[c]

    def pv_from(slot, j, alphas, accs):
        out = []
        for c, vt in zip(chains, values(j)):
            vt1 = jnp.concatenate([vt, jnp.ones((ONES_ROWS, vt.shape[1]), BF16)], axis=0)
            out.append(alphas[c] * accs[c] + jnp.dot(vt1, p_ref[slot, c], preferred_element_type=F32))
        return out

    def softmax_into(slot, j, ms):
        cs = offsets(j)
        new = [_softmax_stage(s_ref[slot, c], cs[c], ms[c]) for c in chains]
        for c in chains:
            p_ref[slot, c] = new[c][1]
        return [n[0] for n in new], [n[2] for n in new]

    qk_into(0, 0)
    first = first()
    for c in chains:
        p_ref[1, c] = first[c][1]

    def pair(i, carry):
        ms, alphas, accs, j_prev = carry
        t = 2 * i
        qk_into(1, t + 1)
        accs = pv_from(1, j_prev, alphas, accs)
        ms, alphas = softmax_into(0, t, ms)
        qk_into(0, t + 2)
        accs = pv_from(0, t, alphas, accs)
        ms, alphas = softmax_into(1, t + 1, ms)
        return ms, alphas, accs, t + 1

    n_q = first[0][0].shape[1]
    init = ([f[0] for f in first], [f[2] for f in first],
            [jnp.zeros((HEAD_DIM + ONES_ROWS, n_q), F32)] * n_ch, j_first)
    _, alphas, accs, j_last = lax.fori_loop(0, (n_tiles + 1) // 2, pair, init)
    accs = pv_from(1, j_last, alphas, accs)
    return [accs[c][:HEAD_DIM] / jnp.maximum(accs[c][HEAD_DIM:HEAD_DIM + 1], 1e-30) for c in chains]


def _moba_kernel(qT_ref, k_ref, vT_ref, aug_ref, srow_ref, sl_ref, o_ref,
                 kmean_ref, kparts_ref, s_ref, p_ref, *, nb, nbp, topk):
    qi = pl.program_id(2)
    blk = MOBA_BLOCK

    @pl.when(qi == 0)
    def _():
        kmean_ref[...] = jnp.zeros(kmean_ref.shape, F32)

        def body(n, carry):
            kb = k_ref[0, pl.ds(pl.multiple_of(n * blk, blk), blk), :].astype(F32)
            kmean_ref[pl.ds(n, 1), :] = jnp.mean(kb, axis=0, keepdims=True)
            return carry
        lax.fori_loop(0, nb, body, 0)
        km = kmean_ref[...]
        head = lax.broadcasted_iota(I32, km.shape, 1) >> 6
        km2 = jnp.concatenate([jnp.where(head == h, km, 0.0) for h in range(2)], axis=0)
        hi = km2.astype(BF16)
        mid = (km2 - hi.astype(F32)).astype(BF16)
        lo = (km2 - hi.astype(F32) - mid.astype(F32)).astype(BF16)
        kparts_ref[0] = hi
        kparts_ref[1] = mid
        kparts_ref[2] = lo

    qT = qT_ref[0]
    row = lax.broadcasted_iota(I32, qT.shape, 0)
    qpad = [jnp.where((row >> 6) == h, qT, jnp.zeros_like(qT)) for h in range(2)]

    gates = (jnp.dot(kparts_ref[0], qT, preferred_element_type=F32)
             + jnp.dot(kparts_ref[1], qT, preferred_element_type=F32)
             + jnp.dot(kparts_ref[2], qT, preferred_element_type=F32))
    bidx = lax.broadcasted_iota(I32, (nbp, blk), 0)
    rhs = []
    for h in range(2):
        gate = gates[h * nbp:(h + 1) * nbp]
        gsc = jnp.where(bidx < qi, gate, -jnp.inf)
        bias = jnp.full((nbp, blk), NEG_BIG, F32)
        for _ in range(topk):
            mx = jnp.max(gsc, axis=0, keepdims=True)
            idx = jnp.min(jnp.where(gsc == mx, bidx, nbp), axis=0, keepdims=True)
            pick = jnp.logical_and(bidx == idx, mx > -jnp.inf)
            bias = jnp.where(pick, 0.0, bias)
            gsc = jnp.where(pick, -jnp.inf, gsc)
        pad = jnp.zeros((2 * HEAD_DIM - nbp - 16, blk), BF16)
        rhs.append(jnp.concatenate([qpad[h], bias.astype(BF16), srow_ref[h], pad], axis=0))

    def scores(j, a):
        k0 = pl.multiple_of(j * blk, blk)
        lhs = jnp.concatenate([k_ref[0, pl.ds(k0, blk), :], aug_ref[a]], axis=1)
        return [jnp.dot(lhs, rhs[h], preferred_element_type=F32) for h in range(2)]

    def values(j):
        k0 = pl.multiple_of(j * blk, blk)
        return [vT_ref[0, h * HEAD_DIM:(h + 1) * HEAD_DIM, pl.ds(k0, blk)] for h in range(2)]

    def offsets(j):
        dq = ((qi - j) * blk).astype(F32)
        return [sl_ref[h] * dq for h in range(2)]

    ik = lax.broadcasted_iota(I32, (blk, blk), 0)
    iq = lax.broadcasted_iota(I32, (blk, blk), 1)
    s_own = [jnp.where(ik <= iq, s, NEG_BIG) for s in scores(qi, nb)]
    m0 = jnp.full((1, blk), NEG_BIG, F32)

    def first():
        return [_softmax_stage(s_own[h], jnp.zeros((1, blk), F32), m0) for h in range(2)]

    outs = _pipelined_tiles(lambda j: scores(jnp.minimum(j, nb - 1), jnp.minimum(j, nb - 1)),
                            values, offsets, s_ref, p_ref, first, 2, qi, qi)
    o_ref[0] = jnp.concatenate(outs, axis=0).T.astype(BF16)


def _moba(mqT, mk, mvT, aug, srow, sl):
    B, _, S = mqT.shape
    blk = MOBA_BLOCK
    nb = S // blk
    topk = min(MOBA_TOPK, nb)
    nbp = -(-nb // 16) * 16
    grid = (B, MOBA_HEADS // 2, nb)
    return pl.pallas_call(
        functools.partial(_moba_kernel, nb=nb, nbp=nbp, topk=topk), grid=grid,
        in_specs=[
            pl.BlockSpec((1, 128, blk), lambda b, p, i: (b, p, i)),
            pl.BlockSpec((1, S, 128), lambda b, p, i: (b, 0, p)),
            pl.BlockSpec((1, 128, S), lambda b, p, i: (b, p, 0)),
            pl.BlockSpec(aug.shape, lambda b, p, i: (0, 0, 0)),
            pl.BlockSpec((2, 16, blk), lambda b, p, i: (p, 0, 0)),
            pl.BlockSpec((2, 1, blk), lambda b, p, i: (p, 0, 0)),
        ],
        out_specs=pl.BlockSpec((1, blk, 128), lambda b, p, i: (b, i, p)),
        out_shape=jax.ShapeDtypeStruct((B, S, 512), BF16),
        scratch_shapes=[
            pltpu.VMEM((nbp, 128), F32),
            pltpu.VMEM((3, 2 * nbp, 128), BF16),
            pltpu.VMEM((2, 2, blk, blk), F32),
            pltpu.VMEM((2, 2, blk, blk), BF16),
        ],
        compiler_params=_params(3), name="moba",
    )(mqT, mk, mvT, aug, srow, sl)


def _compress_kernel(kc_ref, vc_ref, wk_ref, wvT_ref, pek_ref, pev_ref, w2k_ref, w2vT_ref,
                     kcmp_ref, vcmpT_ref):
    nc = kc_ref.shape[1]

    wk = wk_ref[...]
    ab = jnp.dot(kc_ref[0], wk, preferred_element_type=F32)
    pt = (jnp.dot(pek_ref[0], wk[:, 0:128].astype(F32), preferred_element_type=F32)
          + jnp.dot(pek_ref[1], wk[:, 128:256].astype(F32), preferred_element_type=F32))
    pre = ab[:, 0:128] + pltpu.roll(ab[:, 128:256], nc - 1, 0) + pt[0:1]
    hid = jax.nn.gelu(pre)
    kcmp_ref[0] = jnp.dot(hid.astype(BF16), w2k_ref[...], preferred_element_type=F32).astype(BF16)

    wvT = wvT_ref[...]
    abT = lax.dot_general(wvT, vc_ref[0], NT_DIMS, preferred_element_type=F32)
    ptT = (lax.dot_general(wvT[0:128].astype(F32), pev_ref[0], NT_DIMS, preferred_element_type=F32)
           + lax.dot_general(wvT[128:256].astype(F32), pev_ref[1], NT_DIMS, preferred_element_type=F32))
    preT = abT[0:128] + pltpu.roll(abT[128:256], nc - 1, 1) + ptT[:, 0:1]
    hidT = jax.nn.gelu(preT)
    vcmpT_ref[0] = jnp.dot(w2vT_ref[...], hidT.astype(BF16), preferred_element_type=F32).astype(BF16)


def _compress(kc2, vc2, wk, wvT, pek, pev, w2k, w2vT):
    B, nc, _ = kc2.shape
    full = lambda a: pl.BlockSpec(a.shape, lambda b: (0,) * a.ndim)
    blk = pl.BlockSpec((1, nc, kc2.shape[2]), lambda b: (b, 0, 0))
    return pl.pallas_call(
        _compress_kernel, grid=(B,),
        in_specs=[blk, blk, full(wk), full(wvT), full(pek), full(pev), full(w2k), full(w2vT)],
        out_specs=[pl.BlockSpec((1, nc, 128), lambda b: (b, 0, 0)),
                   pl.BlockSpec((1, 128, nc), lambda b: (b, 0, 0))],
        out_shape=[jax.ShapeDtypeStruct((B, nc, 128), BF16),
                   jax.ShapeDtypeStruct((B, 128, nc), BF16)],
        compiler_params=_params(1), name="nsa_compress",
    )(kc2, vc2, wk, wvT, pek, pev, w2k, w2vT)


def _nsa_kernel(qT_ref, kcmp_ref, vcmpT_ref, ks_ref, vsT_ref, kw_ref, vwT_ref,
                g_ref, sl_ref, srow_ref, auga_ref, augb_ref, augc_ref, tblc_ref, bw_ref, o_ref,
                s_ref, p_ref, pc_ref, *, n_slc, topn):
    g = pl.program_id(1)
    qi = pl.program_id(2)
    tq = NSA_TQ
    hg = NSA_GROUP
    wl = hg * tq
    q0 = qi * tq

    q4 = qT_ref[0]
    qT = jnp.concatenate([q4[h * HEAD_DIM:(h + 1) * HEAD_DIM] for h in range(hg)], axis=1)
    qT2 = jnp.concatenate([qT, qT], axis=0)
    rowi = lax.broadcasted_iota(I32, qT2.shape, 0)
    qpad = jnp.where((rowi >> 6) == g, qT2, jnp.zeros_like(qT2))
    slope = sl_ref[0]
    lane = lax.broadcasted_iota(I32, (1, wl), 1)
    t_q = q0 + (lane & (tq - 1))

    nc = kcmp_ref.shape[1]
    rhs_top = jnp.where((rowi >> 6) == g, qT2, srow_ref[0])
    mine_c = (lax.broadcasted_iota(I32, (nc, LANES), 1) >> 6) == g
    lhs_c = jnp.where(mine_c, kcmp_ref[0], augc_ref[0])
    first_c = pl.multiple_of(nc - qi * (tq // NSA_CMP_STRIDE), 8)
    z = jnp.dot(lhs_c, rhs_top, preferred_element_type=F32) + tblc_ref[pl.ds(first_c, nc), :]
    mx = jnp.max(z, axis=0, keepdims=True)
    e = jnp.exp2(z - mx)
    den = jnp.maximum(jnp.sum(e, axis=0, keepdims=True), 1e-30)
    p = e * jnp.where(t_q >= NSA_CMP_LEN - 1, 1.0 / den, 0.0)
    o_c = jnp.dot(vcmpT_ref[0], p.astype(BF16), preferred_element_type=F32)

    pc = p[:, 0:tq]
    for h in range(1, hg):
        pc = pc + p[:, h * tq:(h + 1) * tq]
    n_lc = tq // LANES
    for c in range(n_lc):
        pc_ref[c] = pc[:, c * LANES:(c + 1) * LANES]
    su = NSA_SLC_BLOCK // NSA_CMP_STRIDE
    x = [jnp.concatenate([pc_ref[c, pl.ds(k, n_slc, stride=su), :] for c in range(n_lc)], axis=1)
         for k in range(su)]
    jb = lax.broadcasted_iota(I32, (n_slc, tq), 0)
    prev = jnp.where(jb == 0, 0.0, pltpu.roll(x[3], 1, 0))
    imp = 2.0 * (x[0] + x[1] + x[2]) + x[3] + prev
    cur = (q0 + lax.broadcasted_iota(I32, (1, tq), 1)) >> 6
    allowed = jb <= cur
    forced = jnp.logical_or(jb == 0, jnp.logical_or(jb == cur, jb == cur - 1))
    bias = jnp.where(jnp.logical_and(allowed, forced), 0.0, NEG_BIG)
    sc = jnp.where(jnp.logical_and(allowed, jnp.logical_not(forced)), imp, -1.0)
    for _ in range(topn - 3):
        smx = jnp.max(sc, axis=0, keepdims=True)
        idx = jnp.min(jnp.where(sc == smx, jb, n_slc), axis=0, keepdims=True)
        pick = jnp.logical_and(jb == idx, smx >= 0.0)
        bias = jnp.where(pick, 0.0, bias)
        sc = jnp.where(pick, -1.0, sc)

    wc = NSA_CHAIN_LANES
    n_ch = wl // wc

    def lane_split(a):
        return [a[:, c * wc:(c + 1) * wc] for c in range(n_ch)]

    if n_slc < LANES:
        bias = jnp.concatenate([bias, jnp.zeros((LANES - n_slc, tq), F32)], axis=0)
    bias4 = jnp.concatenate([bias.astype(BF16)] * hg, axis=1)
    rhs = jnp.concatenate([rhs_top, bias4], axis=0)
    mine = (lax.broadcasted_iota(I32, (SLC_TILE, LANES), 1) >> 6) == g
    n_tiles = augb_ref.shape[0]

    def scores(j, null):
        k0 = pl.multiple_of(j * SLC_TILE, SLC_TILE)
        kt = ks_ref[0, pl.ds(k0, SLC_TILE), :]
        lhs = jnp.concatenate([jnp.where(mine, kt, auga_ref[null]), augb_ref[j]], axis=1)
        return jnp.dot(lhs, rhs, preferred_element_type=F32)

    def values(j):
        return [vsT_ref[0, :, pl.ds(pl.multiple_of(j * SLC_TILE, SLC_TILE), SLC_TILE)]] * n_ch

    def offsets(j):
        return lane_split(slope * (q0 - j * SLC_TILE).astype(F32))

    jd = lax.div(q0, SLC_TILE)
    t_k = jd * SLC_TILE + lax.broadcasted_iota(I32, (SLC_TILE, wl), 0)
    s_diag = lane_split(jnp.where(t_k <= t_q, scores(jd, 0), NEG_BIG))
    c_diag = offsets(jd)

    def first():
        return [_softmax_stage(s_diag[c], c_diag[c], jnp.full((1, wc), NEG_BIG, F32))
                for c in range(n_ch)]

    def past_scores(j):
        return lane_split(scores(jnp.minimum(j, n_tiles - 1), (j >= jd).astype(I32)))
    o_s = jnp.concatenate(
        _pipelined_tiles(past_scores, values, offsets, s_ref, p_ref, first, n_ch, jd, jd), axis=1)

    start = pl.multiple_of(jnp.maximum(q0 - NSA_WINDOW, 0), tq)
    first_w = pl.multiple_of(NSA_WINDOW - (q0 - start), tq)
    kt = kw_ref[0, pl.ds(start, WIN_KEYS), :]
    z = jnp.dot(kt, qpad, preferred_element_type=F32) + bw_ref[0, pl.ds(first_w, WIN_KEYS), :]
    mx = jnp.max(z, axis=0, keepdims=True)
    p = jnp.exp2(z - mx)
    den = jnp.maximum(jnp.sum(p, axis=0, keepdims=True), 1e-30)
    o_w = jnp.dot(vwT_ref[0, :, pl.ds(start, WIN_KEYS)], p.astype(BF16),
                  preferred_element_type=F32) / den

    gt = g_ref[0]

    def gate_row(br):
        return jnp.concatenate([gt[br * hg + h:br * hg + h + 1] for h in range(hg)], axis=1)

    o = gate_row(0) * o_c + gate_row(1) * o_s + gate_row(2) * o_w
    o4 = jnp.concatenate([o[:, h * tq:(h + 1) * tq] for h in range(hg)], axis=0)
    o_ref[0] = o4.T.astype(BF16)


def _nsa(nqT, kcmp, vcmpT, ks, vsT, kw, vwT, gT, sl, srow, auga, augb, augc, tblc, bw):
    B, _, S = nqT.shape
    tq = NSA_TQ
    nc = kcmp.shape[1]
    n_slc = S // NSA_SLC_BLOCK
    topn = min(NSA_SLC_TOPN, n_slc)
    wl = NSA_GROUP * tq
    grid = (B, NSA_KV_HEADS, S // tq)
    return pl.pallas_call(
        functools.partial(_nsa_kernel, n_slc=n_slc, topn=topn), grid=grid,
        in_specs=[
            pl.BlockSpec((1, NSA_GROUP * HEAD_DIM, tq), lambda b, g, i: (b, g, i)),
            pl.BlockSpec((1, nc, 128), lambda b, g, i: (b, 0, 0)),
            pl.BlockSpec((1, HEAD_DIM, nc), lambda b, g, i: (b, g, 0)),
            pl.BlockSpec((1, S, 128), lambda b, g, i: (b, 0, 0)),
            pl.BlockSpec((1, HEAD_DIM, S), lambda b, g, i: (b, g, 0)),
            pl.BlockSpec((1, S, 128), lambda b, g, i: (b, 0, 0)),
            pl.BlockSpec((1, HEAD_DIM, S), lambda b, g, i: (b, g, 0)),
            pl.BlockSpec((1, 16, tq), lambda b, g, i: (b, g, i)),
            pl.BlockSpec((1, 1, wl), lambda b, g, i: (g, 0, 0)),
            pl.BlockSpec((1, 2 * HEAD_DIM, wl), lambda b, g, i: (g, 0, 0)),
            pl.BlockSpec((2, SLC_TILE, LANES), lambda b, g, i: (g, 0, 0)),
            pl.BlockSpec(augb.shape, lambda b, g, i: (0, 0, 0)),
            pl.BlockSpec((1, nc, LANES), lambda b, g, i: (g, 0, 0)),
            pl.BlockSpec(tblc.shape, lambda b, g, i: (0, 0)),
            pl.BlockSpec((1,) + bw.shape[1:], lambda b, g, i: (g, 0, 0)),
        ],
        out_specs=pl.BlockSpec((1, tq, NSA_GROUP * HEAD_DIM), lambda b, g, i: (b, i, g)),
        out_shape=jax.ShapeDtypeStruct((B, S, 512), BF16),
        scratch_shapes=[
            pltpu.VMEM((2, wl // NSA_CHAIN_LANES, SLC_TILE, NSA_CHAIN_LANES), F32),
            pltpu.VMEM((2, wl // NSA_CHAIN_LANES, SLC_TILE, NSA_CHAIN_LANES), BF16),
            pltpu.VMEM((tq // LANES, nc, LANES), F32),
        ],
        compiler_params=_params(3), name="nsa",
    )(nqT, kcmp, vcmpT, ks, vsT, kw, vwT, gT, sl, srow, auga, augb, augc, tblc, bw)


def _outproj_kernel(om_ref, on_ref, x_ref, wo_ref, g_ref, wr_ref, br_ref,
                    x1_ref, hn_ref, e_ref, w_ref):
    attn = (jnp.dot(om_ref[...], wo_ref[0:512, :], preferred_element_type=F32)
            + jnp.dot(on_ref[...], wo_ref[512:1024, :], preferred_element_type=F32))
    x1 = x_ref[...] + attn
    x1_ref[...] = x1
    hn = _rmsnorm(x1, g_ref[...])
    _store_token_tiles(hn_ref, hn)
    logits = jnp.dot(hn, wr_ref[...], precision=lax.Precision.HIGHEST,
                     preferred_element_type=F32) + br_ref[...]
    tm = logits.shape[0]
    lane = lax.broadcasted_iota(I32, (tm, LANES), 1)
    sc = jnp.where(lane < N_EXPERTS, logits, -jnp.inf)
    e_out = jnp.zeros((tm, LANES), I32)
    vals = []
    for k in range(TOP_K):
        mx = jnp.max(sc, axis=1, keepdims=True)
        idx = jnp.min(jnp.where(sc == mx, lane, LANES), axis=1, keepdims=True)
        e_out = jnp.where(lane == k, idx, e_out)
        sc = jnp.where(lane == idx, -jnp.inf, sc)
        vals.append(mx)
    ex = [jnp.exp(v - vals[0]) for v in vals]
    den = ex[0] + ex[1] + ex[2] + ex[3]
    w_out = jnp.zeros((tm, LANES), F32)
    for k in range(TOP_K):
        w_out = jnp.where(lane == k, ex[k] / den, w_out)
    e_ref[...] = e_out
    w_ref[...] = w_out


def _outproj(om, on, x, wo, g, wr, br, tm):
    N, D = x.shape
    full = lambda a: pl.BlockSpec(a.shape, lambda i: (0,) * a.ndim)
    row = lambda w: pl.BlockSpec((tm, w), lambda i: (i, 0))
    return pl.pallas_call(
        _outproj_kernel, grid=(N // tm,),
        in_specs=[row(512), row(512), row(D), full(wo), full(g), full(wr), full(br)],
        out_specs=[row(D), pl.BlockSpec((tm * SUBLANES, LANES), lambda i: (i, 0)), row(LANES), row(LANES)],
        out_shape=[jax.ShapeDtypeStruct((N, D), F32), jax.ShapeDtypeStruct((N * SUBLANES, LANES), F32),
                   jax.ShapeDtypeStruct((N, LANES), I32), jax.ShapeDtypeStruct((N, LANES), F32)],
        compiler_params=_params(1), name="outproj_router",
    )(om, on, x, wo, g, wr, br)


def _rank_kernel(e_ref, rank_ref, cnt_ref, base_ref):
    i = pl.program_id(0)
    T = e_ref.shape[0]

    @pl.when(i == 0)
    def _():
        base_ref[...] = jnp.zeros(base_ref.shape, F32)

    e = e_ref[...]
    lane = lax.broadcasted_iota(I32, (T, LANES), 1)
    tril = jnp.where(lax.broadcasted_iota(I32, (T, T), 0) >= lax.broadcasted_iota(I32, (T, T), 1),
                     1.0, 0.0).astype(BF16)
    out = jnp.zeros((T, LANES), I32)
    for k in range(TOP_K):
        hit = lane == e[:, k:k + 1]
        oh = jnp.where(hit, 1.0, 0.0)
        cum = jnp.dot(tril, oh.astype(BF16), preferred_element_type=F32)
        base = base_ref[0:1, :]
        r = jnp.sum(jnp.where(hit, cum - 1.0 + base, 0.0), axis=1, keepdims=True)
        out = jnp.where(lane == k, r.astype(I32), out)
        base_ref[...] = base_ref[...] + jnp.sum(oh, axis=0, keepdims=True)
    rank_ref[...] = out
    cnt_ref[...] = base_ref[...]


def _ranks(e128):
    N = e128.shape[0]
    T = RANK_TILE
    return pl.pallas_call(
        _rank_kernel, grid=(N // T,),
        in_specs=[pl.BlockSpec((T, LANES), lambda i: (i, 0))],
        out_specs=[pl.BlockSpec((T, LANES), lambda i: (i, 0)),
                   pl.BlockSpec((8, LANES), lambda i: (0, 0))],
        out_shape=[jax.ShapeDtypeStruct((N, LANES), I32),
                   jax.ShapeDtypeStruct((8, LANES), F32)],
        scratch_shapes=[pltpu.VMEM((8, LANES), F32)],
        compiler_params=_params(1), name="route_ranks",
    )(e128)


def _row_copy(src, dst, i_src, i_dst, sem):
    return pltpu.make_async_copy(src.at[pl.ds(pl.multiple_of(i_src * SUBLANES, SUBLANES), SUBLANES)],
                                 dst.at[pl.ds(pl.multiple_of(i_dst * SUBLANES, SUBLANES), SUBLANES)], sem)


def _store_token_tiles(ref, x):
    rows = x.shape[0]
    for c in range(SUBLANES):
        ref[pl.ds(c, rows, stride=SUBLANES), :] = x[:, c * LANES:(c + 1) * LANES]


def _load_token_tiles(ref, rows):
    return jnp.concatenate([ref[pl.ds(c, rows, stride=SUBLANES), :] for c in range(SUBLANES)], axis=1)


def _dispatch_kernel(dest_hbm, hp_ref, xz_hbm, out_hbm, idx_ref, isem, sem):
    del xz_hbm
    i = pl.program_id(0)
    T = ROUTE_TILE
    cp = pltpu.make_async_copy(dest_hbm.at[i], idx_ref, isem)
    cp.start()
    cp.wait()

    def issue(t, carry):
        for k in range(TOP_K):
            _row_copy(hp_ref, out_hbm, t, idx_ref[t * TOP_K + k], sem).start(priority=k % 2)
        return carry
    lax.fori_loop(0, T, issue, 0, unroll=ROW_DMA_UNROLL)

    def drain(t, carry):
        for k in range(TOP_K):
            _row_copy(hp_ref, out_hbm, 0, 0, sem).wait()
        return carry
    lax.fori_loop(0, T, drain, 0, unroll=ROW_DMA_UNROLL)


def _dispatch(dest2, hp, xzero):
    nsteps = dest2.shape[0]
    T = ROUTE_TILE
    return pl.pallas_call(
        _dispatch_kernel, grid=(nsteps,),
        in_specs=[pl.BlockSpec(memory_space=pl.ANY),
                  pl.BlockSpec((T * SUBLANES, LANES), lambda i: (i, 0)),
                  pl.BlockSpec(memory_space=pl.ANY)],
        out_specs=pl.BlockSpec(memory_space=pl.ANY),
        out_shape=jax.ShapeDtypeStruct(xzero.shape, xzero.dtype),
        scratch_shapes=[pltpu.SMEM((T * TOP_K,), I32),
                        pltpu.SemaphoreType.DMA, pltpu.SemaphoreType.DMA],
        input_output_aliases={2: 0},
        compiler_params=_params(1), name="moe_dispatch",
    )(dest2, hp, xzero)


def _expert_kernel(be_ref, na_ref, x_ref, wg_ref, wu_ref, bg_ref, bu_ref, wd_ref, bd_ref, y_ref):
    b = pl.program_id(0)
    active = b < na_ref[0]

    @pl.when(active)
    def _():
        xb = _load_token_tiles(x_ref, MOE_ROWS).astype(BF16)
        gg = lax.dot_general(xb, wg_ref[0], NT_DIMS, preferred_element_type=F32) + bg_ref[0]
        uu = lax.dot_general(xb, wu_ref[0], NT_DIMS, preferred_element_type=F32) + bu_ref[0]
        gg = jnp.minimum(gg, SWIGLU_LIMIT)
        uu = jnp.clip(uu, -SWIGLU_LIMIT, SWIGLU_LIMIT)
        a = gg * jax.nn.sigmoid(SWIGLU_ALPHA * gg) * (uu + 1.0)
        _store_token_tiles(y_ref, jnp.dot(a.astype(BF16), wd_ref[0], preferred_element_type=F32) + bd_ref[0])

    @pl.when(jnp.logical_not(active))
    def _():
        y_ref[...] = jnp.zeros(y_ref.shape, F32)


def _experts(blk_e, n_act, xrows, wgT, wuT, bg, bu, wd, bd):
    _, F, D = wgT.shape
    assert D == SUBLANES * LANES
    P = xrows.shape[0] // SUBLANES
    n_blk = P // MOE_ROWS
    rows_spec = pl.BlockSpec((MOE_ROWS * SUBLANES, LANES), lambda b, be, na: (b, 0))
    wspec = lambda r, c: pl.BlockSpec((1, r, c), lambda b, be, na: (be[b], 0, 0))
    grid_spec = pltpu.PrefetchScalarGridSpec(
        num_scalar_prefetch=2, grid=(n_blk,),
        in_specs=[rows_spec, wspec(F, D), wspec(F, D), wspec(1, F), wspec(1, F), wspec(F, D), wspec(1, D)],
        out_specs=rows_spec,
    )
    return pl.pallas_call(
        _expert_kernel, grid_spec=grid_spec,
        out_shape=jax.ShapeDtypeStruct(xrows.shape, F32),
        compiler_params=_params(1), name="moe_experts",
    )(blk_e, n_act, xrows, wgT, wuT, bg, bu, wd, bd)


def _combine_kernel(dest_hbm, x1_ref, w_ref, g_ref, y_hbm, o_ref, idx_ref, buf_ref, isem, sem):
    i = pl.program_id(0)
    T = ROUTE_TILE
    cp = pltpu.make_async_copy(dest_hbm.at[i], idx_ref, isem)
    cp.start()
    cp.wait()

    def issue(t, carry):
        for k in range(TOP_K):
            _row_copy(y_hbm, buf_ref.at[k], idx_ref[t * TOP_K + k], t, sem).start(priority=k % 2)
        return carry
    lax.fori_loop(0, T, issue, 0, unroll=ROW_DMA_UNROLL)

    def drain(t, carry):
        for k in range(TOP_K):
            _row_copy(y_hbm, buf_ref.at[k], 0, 0, sem).wait()
        return carry
    lax.fori_loop(0, T, drain, 0, unroll=ROW_DMA_UNROLL)

    x1 = x1_ref[...]
    w = w_ref[...]
    cols = []
    for c in range(SUBLANES):
        acc = x1[:, c * LANES:(c + 1) * LANES]
        for k in range(TOP_K):
            acc = acc + w[:, k:k + 1] * buf_ref[k, pl.ds(c, T, stride=SUBLANES), :]
        cols.append(acc)
    o_ref[...] = _rmsnorm(jnp.concatenate(cols, axis=1), g_ref[...])


def _combine(dest2, x1, w128, g, yrows):
    N, D = x1.shape
    T = ROUTE_TILE
    return pl.pallas_call(
        _combine_kernel, grid=(N // T,),
        in_specs=[pl.BlockSpec(memory_space=pl.ANY),
                  pl.BlockSpec((T, D), lambda i: (i, 0)),
                  pl.BlockSpec((T, LANES), lambda i: (i, 0)),
                  pl.BlockSpec(g.shape, lambda i: (0, 0)),
                  pl.BlockSpec(memory_space=pl.ANY)],
        out_specs=pl.BlockSpec((T, D), lambda i: (i, 0)),
        out_shape=jax.ShapeDtypeStruct((N, D), F32),
        scratch_shapes=[pltpu.SMEM((T * TOP_K,), I32),
                        pltpu.VMEM((TOP_K, T * SUBLANES, LANES), F32),
                        pltpu.SemaphoreType.DMA, pltpu.SemaphoreType.DMA],
        compiler_params=_params(1), name="moe_combine",
    )(dest2, x1, w128, g, yrows)


def _alibi_slopes():
    n = MOBA_HEADS + NSA_HEADS
    s = jnp.exp2(-8.0 * jnp.arange(1, n + 1, dtype=F32) / n)
    return s[0::2], s[1::2]


def _prep_inproj(w_in):
    hd = HEAD_DIM
    sizes = [MOBA_HEADS * hd] * 3 + [NSA_HEADS * hd] + [NSA_KV_HEADS * hd] * 6 + [NSA_BRANCHES * NSA_HEADS]
    cuts = np.cumsum([0] + sizes)
    mq, mk, mv, nq, kc, vc, ks, vs, kw, vw, ng = [w_in[:, cuts[i]:cuts[i + 1]] for i in range(11)]
    qscale = (hd ** -0.5) * LOG2E
    wr = jnp.concatenate([mk, kc, vc, ks, kw], axis=1).astype(BF16)
    wt = jnp.concatenate([mq * qscale, mv, nq * qscale, vs, vw], axis=1).T.astype(BF16)
    ngr = ng.reshape(-1, NSA_KV_HEADS, NSA_GROUP, NSA_BRANCHES).transpose(1, 3, 2, 0)
    ngr = ngr.reshape(NSA_KV_HEADS, NSA_BRANCHES * NSA_GROUP, -1)
    wg = jnp.pad(ngr, ((0, 0), (0, 16 - NSA_BRANCHES * NSA_GROUP), (0, 0))).reshape(32, -1)
    return wr, wt, wg.astype(F32)


def _prep_compress(w1, w2, pe):
    hd, half = HEAD_DIM, NSA_CMP_STRIDE
    w1r = w1.reshape(2, half, hd, hd)
    eye = jnp.eye(NSA_KV_HEADS, dtype=w1.dtype)
    w = jnp.einsum('alde,gh->lgdahe', w1r, eye).reshape(half * NSA_KV_HEADS * hd, 2 * NSA_KV_HEADS * hd)
    w2b = jnp.einsum('de,gh->gdhe', w2, eye).reshape(NSA_KV_HEADS * hd, NSA_KV_HEADS * hd)
    per = pe.reshape(2, half, 1, hd)
    pe2 = jnp.broadcast_to(per, (2, half, NSA_KV_HEADS, hd)).reshape(2, 1, half * NSA_KV_HEADS * hd)
    pe2 = jnp.broadcast_to(pe2, (2, 8, pe2.shape[2]))
    return w.astype(BF16), w2b.astype(BF16), pe2.astype(F32)


def _attention_tables(S):
    moba_sl, nsa_sl = _alibi_slopes()
    moba_sl = moba_sl * LOG2E
    nsa_sl = nsa_sl * LOG2E
    blk = MOBA_BLOCK
    moba_row = jnp.broadcast_to(moba_sl[:, None, None], (MOBA_HEADS, 1, blk))
    nb = S // blk
    nbp = -(-nb // 16) * 16
    col = jnp.arange(LANES)[None, None, :]
    tile = jnp.arange(nb + 1)[:, None, None]
    off = jnp.arange(blk, dtype=F32)[None, :, None]
    moba_aug = jnp.where(jnp.logical_and(col == tile, tile < nb), 1.0,
                         jnp.where(jnp.logical_and(col >= nbp, col < nbp + 3), off, 0.0)).astype(BF16)
    parts = jnp.stack(list(_split3(moba_sl)) + [jnp.zeros_like(moba_sl)] * 13, axis=1)
    moba_srow = jnp.broadcast_to(parts[:, :, None], (MOBA_HEADS, 16, blk)).astype(BF16)

    wl = NSA_GROUP * NSA_TQ
    n_slc = S // NSA_SLC_BLOCK
    assert n_slc <= LANES, "block-choice rows must fit the spare contraction rows"
    nsa_row = jnp.repeat(nsa_sl.reshape(NSA_KV_HEADS, NSA_GROUP), NSA_TQ, axis=1)
    hi, mid, lo = [t[:, None, :] for t in _split3(nsa_row)]
    base = ((1 - jnp.arange(NSA_KV_HEADS)) * HEAD_DIM)[:, None, None]
    rows = jnp.arange(2 * HEAD_DIM)[None, :, None]
    nsa_srow = jnp.where(rows == base, hi, jnp.where(rows == base + 1, mid, jnp.where(
        rows == base + 2, lo, jnp.where(rows == base + 3, NEG_BIG, 0.0)))).astype(BF16)
    lane = jnp.arange(LANES)[None, None, None, :]
    base4 = base[:, None]
    null = jnp.arange(2, dtype=F32)[None, :, None, None]
    koff = jnp.arange(SLC_TILE, dtype=F32)[None, None, :, None]
    nsa_auga = jnp.where(jnp.logical_and(lane >= base4, lane < base4 + 3), koff,
                         jnp.where(lane == base4 + 3, null, 0.0))
    nsa_auga = nsa_auga.reshape(2 * NSA_KV_HEADS, SLC_TILE, LANES).astype(BF16)
    per = SLC_TILE // NSA_SLC_BLOCK
    tile = jnp.arange(S // SLC_TILE)[:, None, None]
    blk_of = tile * per + jnp.arange(SLC_TILE)[None, :, None] // NSA_SLC_BLOCK
    nsa_augb = (jnp.arange(LANES)[None, None, :] == blk_of).astype(BF16)

    nc = S // NSA_CMP_STRIDE
    ci = jnp.arange(nc)[None, :, None]
    lane3 = jnp.arange(LANES)[None, None, :]
    nsa_augc = jnp.where(jnp.logical_and(lane3 >= base + 4, lane3 < base + 7), (ci >> 1).astype(F32),
                         jnp.where(jnp.logical_and(lane3 >= base + 7, lane3 < base + 10),
                                   (ci & 1).astype(F32), 0.0)).astype(BF16)
    step2 = [t[:, None, :] for t in _split3(nsa_row * (2.0 * NSA_CMP_STRIDE))]
    step1 = [t[:, None, :] for t in _split3(nsa_row * (1.0 * NSA_CMP_STRIDE))]
    for k in range(3):
        nsa_srow = jnp.where(rows == base + 4 + k, step2[k].astype(BF16),
                             jnp.where(rows == base + 7 + k, step1[k].astype(BF16), nsa_srow))
    il = jnp.tile(jnp.arange(NSA_TQ), NSA_GROUP)[None, :]
    rel = (jnp.arange(2 * nc) - nc)[:, None]
    nsa_tblc = jnp.where(rel * NSA_CMP_STRIDE + (NSA_CMP_LEN - 1) <= il, 0.0, NEG_BIG).astype(F32)
    dist = (NSA_WINDOW + il - jnp.arange(NSA_WINDOW + WIN_KEYS)[:, None])[None]
    nsa_bw = jnp.where(jnp.logical_and(dist >= 0, dist < NSA_WINDOW),
                       -nsa_row[:, None, :] * dist.astype(F32), NEG_BIG)
    return ((moba_aug, moba_srow, moba_row),
            (nsa_row.reshape(NSA_KV_HEADS, 1, wl), nsa_srow, nsa_auga, nsa_augb, nsa_augc, nsa_tblc, nsa_bw))


def _split3(x):
    hi = x.astype(BF16).astype(F32)
    mid = (x - hi).astype(BF16).astype(F32)
    lo = (x - hi - mid).astype(BF16).astype(F32)
    return hi, mid, lo


def _attention(x, attn_norm_g, w_in, cmp_pe_k, cmp_pe_v, cmp_w1_k, cmp_w2_k, cmp_w1_v, cmp_w2_v):
    B, S, D = x.shape
    wr, wt, wg = _prep_inproj(w_in)
    (mk, kc, vc, ks, kw, mqT, mvT, nqT, vsT, vwT, gT) = _inproj(
        x, attn_norm_g.reshape(1, D), wr, wt, wg, tm=512)
    moba_tabs, nsa_tabs = _attention_tables(S)
    o_moba = _moba(mqT, mk, mvT, *moba_tabs)
    wk, w2k, pek = _prep_compress(cmp_w1_k, cmp_w2_k, cmp_pe_k)
    wv, w2v, pev = _prep_compress(cmp_w1_v, cmp_w2_v, cmp_pe_v)
    nc = S // NSA_CMP_STRIDE
    kcmp, vcmpT = _compress(kc.reshape(B, nc, -1), vc.reshape(B, nc, -1), wk, wv.T, pek, pev, w2k, w2v.T)
    o_nsa = _nsa(nqT, kcmp, vcmpT, ks, vsT, kw, vwT, gT, *nsa_tabs)
    return o_moba, o_nsa


def _moe(x1, hn, e128, w128, w_up, b_up, w_down, b_down, final_norm_g):
    N, D = x1.shape
    rank128, cnt = _ranks(e128)
    counts = cnt[0, :N_EXPERTS].astype(I32)
    padded = (counts + MOE_ROWS - 1) // MOE_ROWS * MOE_ROWS
    pends = jnp.cumsum(padded)
    pstarts = pends - padded
    e4 = e128[:, :TOP_K]
    dest = pstarts[e4] + rank128[:, :TOP_K]
    dest2 = dest.reshape(N // ROUTE_TILE, ROUTE_TILE * TOP_K)
    n_blk = (N * TOP_K + N_EXPERTS * MOE_ROWS + MOE_ROWS - 1) // MOE_ROWS
    P = n_blk * MOE_ROWS
    blk_start = jnp.arange(n_blk, dtype=I32) * MOE_ROWS
    blk_e = jnp.minimum(jnp.sum((pends[None, :] <= blk_start[:, None]).astype(I32), axis=1), N_EXPERTS - 1)
    n_act = (pends[-1:] // MOE_ROWS).astype(I32)
    xrows = _dispatch(dest2, hn, jnp.zeros((P * SUBLANES, LANES), F32))
    wT = jnp.transpose(w_up.reshape(N_EXPERTS, D, -1, 2), (3, 0, 2, 1)).astype(BF16)
    bg = b_up[:, None, 0::2]
    bu = b_up[:, None, 1::2]
    yrows = _experts(blk_e, n_act, xrows, wT[0], wT[1], bg, bu,
                     w_down.astype(BF16), b_down[:, None, :])
    return _combine(dest2, x1, w128, final_norm_g.reshape(1, D), yrows)


def kernel(x, attn_norm_g, w_in, cmp_pe_k, cmp_pe_v, cmp_w1_k, cmp_w2_k, cmp_w1_v, cmp_w2_v, w_out, ffn_norm_g, w_router, b_router, w_up, b_up, w_down, b_down, final_norm_g):
    B, S, D = x.shape
    assert attn_norm_g.shape[0] == 1, "single-layer kernel"
    o_moba, o_nsa = _attention(x, attn_norm_g[0], w_in[0], cmp_pe_k[0], cmp_pe_v[0],
                               cmp_w1_k[0], cmp_w2_k[0], cmp_w1_v[0], cmp_w2_v[0])
    N = B * S
    wr = jnp.pad(w_router[0], ((0, 0), (0, LANES - N_EXPERTS)))
    br = jnp.pad(b_router[0], (0, LANES - N_EXPERTS)).reshape(1, LANES)
    x1, hn, e128, w128 = _outproj(o_moba.reshape(N, -1), o_nsa.reshape(N, -1), x.reshape(N, D),
                                  w_out[0].astype(BF16), ffn_norm_g[0].reshape(1, D), wr, br, tm=512)
    out = _moe(x1, hn, e128, w128, w_up[0], b_up[0], w_down[0], b_down[0], final_norm_g)
    return out.reshape(B, S, D)
```

```python
import functools

import jax
import jax.numpy as jnp
import numpy as np
from jax import lax
from jax.experimental import pallas as pl
from jax.experimental.pallas import tpu as pltpu

F32 = jnp.float32
BF16 = jnp.bfloat16
I32 = jnp.int32

HEAD_DIM = 64
MOBA_HEADS = 8
NSA_HEADS = 8
NSA_KV_HEADS = 2
NSA_GROUP = NSA_HEADS // NSA_KV_HEADS
MOBA_BLOCK = 256
MOBA_TOPK = 3
NSA_CMP_LEN = 32
NSA_CMP_STRIDE = 16
NSA_SLC_BLOCK = 64
NSA_SLC_TOPN = 16
NSA_WINDOW = 512
NSA_BRANCHES = 3
N_EXPERTS = 32
TOP_K = 4
SWIGLU_LIMIT = 7.0
SWIGLU_ALPHA = 1.702
RMS_EPS = 1e-5
NEG_BIG = -1e30
LOG2E = 1.4426950408889634

LANES = 128
SUBLANES = 8
VMEM_LIMIT = 56 * 1024 * 1024

NSA_TQ = 256
NSA_CHAIN_LANES = 256
SLC_TILE = 256
WIN_KEYS = NSA_WINDOW + NSA_TQ
MOE_ROWS = 512
ROUTE_TILE = 256
RANK_TILE = 512
ROW_DMA_UNROLL = 8

NT_DIMS = (((1,), (1,)), ((), ()))


def _params(n_grid):
    return pltpu.CompilerParams(
        dimension_semantics=("arbitrary",) * n_grid,
        vmem_limit_bytes=VMEM_LIMIT,
    )


def _rmsnorm(x, g):
    return x * lax.rsqrt(jnp.mean(x * x, axis=-1, keepdims=True) + RMS_EPS) * g


def _inproj_kernel(x_ref, g_ref, wr_ref, wt_ref, wg_ref,
                   mk_ref, kc_ref, vc_ref, ks_ref, kw_ref,
                   mqT_ref, mvT_ref, nqT_ref, vsT_ref, vwT_ref, gT_ref):
    xn = _rmsnorm(x_ref[0], g_ref[...])
    xb = xn.astype(BF16)
    yr = jnp.dot(xb, wr_ref[...], preferred_element_type=F32)
    mk_ref[0] = yr[:, 0:512].astype(BF16)
    kc_ref[0] = yr[:, 512:640].astype(BF16)
    vc_ref[0] = yr[:, 640:768].astype(BF16)
    ks_ref[0] = yr[:, 768:896].astype(BF16)
    kw_ref[0] = yr[:, 896:1024].astype(BF16)
    yt = lax.dot_general(wt_ref[...], xb, NT_DIMS, preferred_element_type=F32)
    mqT_ref[0] = yt[0:512].astype(BF16)
    mvT_ref[0] = yt[512:1024].astype(BF16)
    nqT_ref[0] = yt[1024:1536].astype(BF16)
    vsT_ref[0] = yt[1536:1664].astype(BF16)
    vwT_ref[0] = yt[1664:1792].astype(BF16)
    gl = lax.dot_general(wg_ref[...], xn, NT_DIMS, precision=lax.Precision.HIGHEST,
                         preferred_element_type=F32)
    gT_ref[0] = jax.nn.sigmoid(gl)


def _inproj(x, g, wr, wt, wg, tm):
    B, S, D = x.shape
    grid = (B, S // tm)
    row = lambda w: pl.BlockSpec((1, tm, w), lambda b, i: (b, i, 0))
    col = lambda h: pl.BlockSpec((1, h, tm), lambda b, i: (b, 0, i))
    full = lambda a: pl.BlockSpec(a.shape, lambda b, i: (0,) * a.ndim)
    out_shape = [
        jax.ShapeDtypeStruct((B, S, 512), BF16),
        jax.ShapeDtypeStruct((B, S, 128), BF16),
        jax.ShapeDtypeStruct((B, S, 128), BF16),
        jax.ShapeDtypeStruct((B, S, 128), BF16),
        jax.ShapeDtypeStruct((B, S, 128), BF16),
        jax.ShapeDtypeStruct((B, 512, S), BF16),
        jax.ShapeDtypeStruct((B, 512, S), BF16),
        jax.ShapeDtypeStruct((B, 512, S), BF16),
        jax.ShapeDtypeStruct((B, 128, S), BF16),
        jax.ShapeDtypeStruct((B, 128, S), BF16),
        jax.ShapeDtypeStruct((B, 32, S), F32),
    ]
    out_specs = [row(512), row(128), row(128), row(128), row(128),
                 col(512), col(512), col(512), col(128), col(128), col(32)]
    return pl.pallas_call(
        _inproj_kernel, grid=grid,
        in_specs=[pl.BlockSpec((1, tm, D), lambda b, i: (b, i, 0)),
                  full(g), full(wr), full(wt), full(wg)],
        out_specs=out_specs, out_shape=out_shape,
        compiler_params=_params(2), name="inproj",
    )(x, g, wr, wt, wg)


ONES_ROWS = 16


def _softmax_stage(s, c, m):
    mt = jnp.max(s, axis=0, keepdims=True) - c
    m_new = jnp.maximum(m, mt)
    alpha = jnp.exp2(m - m_new)
    p = jnp.exp2(s - (m_new + c))
    return m_new, p.astype(BF16), alpha


def _pipelined_tiles(scores, values, offsets, s_ref, p_ref, first, n_ch, n_tiles, j_first):
    chains = range(n_ch)

    def qk_into(slot, j):
        sc = scores(j)
        for c in chains:
            s_ref[slot, c] = sc[c]

    def pv_from(slot, j, alphas, accs):
        out = []
        for c, vt in zip(chains, values(j)):
            vt1 = jnp.concatenate([vt, jnp.ones((ONES_ROWS, vt.shape[1]), BF16)], axis=0)
            out.append(alphas[c] * accs[c] + jnp.dot(vt1, p_ref[slot, c], preferred_element_type=F32))
        return out

    def softmax_into(slot, j, ms):
        cs = offsets(j)
        new = [_softmax_stage(s_ref[slot, c], cs[c], ms[c]) for c in chains]
        for c in chains:
            p_ref[slot, c] = new[c][1]
        return [n[0] for n in new], [n[2] for n in new]

    qk_into(0, 0)
    first = first()
    for c in chains:
        p_ref[1, c] = first[c][1]

    def pair(i, carry):
        ms, alphas, accs, j_prev = carry
        t = 2 * i
        qk_into(1, t + 1)
        accs = pv_from(1, j_prev, alphas, accs)
        ms, alphas = softmax_into(0, t, ms)
        qk_into(0, t + 2)
        accs = pv_from(0, t, alphas, accs)
        ms, alphas = softmax_into(1, t + 1, ms)
        return ms, alphas, accs, t + 1

    n_q = first[0][0].shape[1]
    init = ([f[0] for f in first], [f[2] for f in first],
            [jnp.zeros((HEAD_DIM + ONES_ROWS, n_q), F32)] * n_ch, j_first)
    _, alphas, accs, j_last = lax.fori_loop(0, (n_tiles + 1) // 2, pair, init)
    accs = pv_from(1, j_last, alphas, accs)
    return [accs[c][:HEAD_DIM] / jnp.maximum(accs[c][HEAD_DIM:HEAD_DIM + 1], 1e-30) for c in chains]


def _moba_kernel(qT_ref, k_ref, vT_ref, aug_ref, srow_ref, sl_ref, o_ref,
                 kmean_ref, kparts_ref, s_ref, p_ref, *, nb, nbp, topk):
    qi = pl.program_id(2)
    blk = MOBA_BLOCK

    @pl.when(qi == 0)
    def _():
        kmean_ref[...] = jnp.zeros(kmean_ref.shape, F32)

        def body(n, carry):
            kb = k_ref[0, pl.ds(pl.multiple_of(n * blk, blk), blk), :].astype(F32)
            kmean_ref[pl.ds(n, 1), :] = jnp.mean(kb, axis=0, keepdims=True)
            return carry
        lax.fori_loop(0, nb, body, 0)
        km = kmean_ref[...]
        head = lax.broadcasted_iota(I32, km.shape, 1) >> 6
        km2 = jnp.concatenate([jnp.where(head == h, km, 0.0) for h in range(2)], axis=0)
        hi = km2.astype(BF16)
        mid = (km2 - hi.astype(F32)).astype(BF16)
        lo = (km2 - hi.astype(F32) - mid.astype(F32)).astype(BF16)
        kparts_ref[0] = hi
        kparts_ref[1] = mid
        kparts_ref[2] = lo

    qT = qT_ref[0]
    row = lax.broadcasted_iota(I32, qT.shape, 0)
    qpad = [jnp.where((row >> 6) == h, qT, jnp.zeros_like(qT)) for h in range(2)]

    gates = (jnp.dot(kparts_ref[0], qT, preferred_element_type=F32)
             + jnp.dot(kparts_ref[1], qT, preferred_element_type=F32)
             + jnp.dot(kparts_ref[2], qT, preferred_element_type=F32))
    bidx = lax.broadcasted_iota(I32, (nbp, blk), 0)
    rhs = []
    for h in range(2):
        gate = gates[h * nbp:(h + 1) * nbp]
        gsc = jnp.where(bidx < qi, gate, -jnp.inf)
        bias = jnp.full((nbp, blk), NEG_BIG, F32)
        for _ in range(topk):
            mx = jnp.max(gsc, axis=0, keepdims=True)
            idx = jnp.min(jnp.where(gsc == mx, bidx, nbp), axis=0, keepdims=True)
            pick = jnp.logical_and(bidx == idx, mx > -jnp.inf)
            bias = jnp.where(pick, 0.0, bias)
            gsc = jnp.where(pick, -jnp.inf, gsc)
        pad = jnp.zeros((2 * HEAD_DIM - nbp - 16, blk), BF16)
        rhs.append(jnp.concatenate([qpad[h], bias.astype(BF16), srow_ref[h], pad], axis=0))

    def scores(j, a):
        k0 = pl.multiple_of(j * blk, blk)
        lhs = jnp.concatenate([k_ref[0, pl.ds(k0, blk), :], aug_ref[a]], axis=1)
        return [jnp.dot(lhs, rhs[h], preferred_element_type=F32) for h in range(2)]

    def values(j):
        k0 = pl.multiple_of(j * blk, blk)
        return [vT_ref[0, h * HEAD_DIM:(h + 1) * HEAD_DIM, pl.ds(k0, blk)] for h in range(2)]

    def offsets(j):
        dq = ((qi - j) * blk).astype(F32)
        return [sl_ref[h] * dq for h in range(2)]

    ik = lax.broadcasted_iota(I32, (blk, blk), 0)
    iq = lax.broadcasted_iota(I32, (blk, blk), 1)
    s_own = [jnp.where(ik <= iq, s, NEG_BIG) for s in scores(qi, nb)]
    m0 = jnp.full((1, blk), NEG_BIG, F32)

    def first():
        return [_softmax_stage(s_own[h], jnp.zeros((1, blk), F32), m0) for h in range(2)]

    outs = _pipelined_tiles(lambda j: scores(jnp.minimum(j, nb - 1), jnp.minimum(j, nb - 1)),
                            values, offsets, s_ref, p_ref, first, 2, qi, qi)
    o_ref[0] = jnp.concatenate(outs, axis=0).T.astype(BF16)


def _moba(mqT, mk, mvT, aug, srow, sl):
    B, _, S = mqT.shape
    blk = MOBA_BLOCK
    nb = S // blk
    topk = min(MOBA_TOPK, nb)
    nbp = -(-nb // 16) * 16
    grid = (B, MOBA_HEADS // 2, nb)
    return pl.pallas_call(
        functools.partial(_moba_kernel, nb=nb, nbp=nbp, topk=topk), grid=grid,
        in_specs=[
            pl.BlockSpec((1, 128, blk), lambda b, p, i: (b, p, i)),
            pl.BlockSpec((1, S, 128), lambda b, p, i: (b, 0, p)),
            pl.BlockSpec((1, 128, S), lambda b, p, i: (b, p, 0)),
            pl.BlockSpec(aug.shape, lambda b, p, i: (0, 0, 0)),
            pl.BlockSpec((2, 16, blk), lambda b, p, i: (p, 0, 0)),
            pl.BlockSpec((2, 1, blk), lambda b, p, i: (p, 0, 0)),
        ],
        out_specs=pl.BlockSpec((1, blk, 128), lambda b, p, i: (b, i, p)),
        out_shape=jax.ShapeDtypeStruct((B, S, 512), BF16),
        scratch_shapes=[
            pltpu.VMEM((nbp, 128), F32),
            pltpu.VMEM((3, 2 * nbp, 128), BF16),
            pltpu.VMEM((2, 2, blk, blk), F32),
            pltpu.VMEM((2, 2, blk, blk), BF16),
        ],
        compiler_params=_params(3), name="moba",
    )(mqT, mk, mvT, aug, srow, sl)


def _compress_kernel(kc_ref, vc_ref, wk_ref, wvT_ref, pek_ref, pev_ref, w2k_ref, w2vT_ref,
                     kcmp_ref, vcmpT_ref):
    nc = kc_ref.shape[1]

    wk = wk_ref[...]
    ab = jnp.dot(kc_ref[0], wk, preferred_element_type=F32)
    pt = (jnp.dot(pek_ref[0], wk[:, 0:128].astype(F32), preferred_element_type=F32)
          + jnp.dot(pek_ref[1], wk[:, 128:256].astype(F32), preferred_element_type=F32))
    pre = ab[:, 0:128] + pltpu.roll(ab[:, 128:256], nc - 1, 0) + pt[0:1]
    hid = jax.nn.gelu(pre)
    kcmp_ref[0] = jnp.dot(hid.astype(BF16), w2k_ref[...], preferred_element_type=F32).astype(BF16)

    wvT = wvT_ref[...]
    abT = lax.dot_general(wvT, vc_ref[0], NT_DIMS, preferred_element_type=F32)
    ptT = (lax.dot_general(wvT[0:128].astype(F32), pev_ref[0], NT_DIMS, preferred_element_type=F32)
           + lax.dot_general(wvT[128:256].astype(F32), pev_ref[1], NT_DIMS, preferred_element_type=F32))
    preT = abT[0:128] + pltpu.roll(abT[128:256], nc - 1, 1) + ptT[:, 0:1]
    hidT = jax.nn.gelu(preT)
    vcmpT_ref[0] = jnp.dot(w2vT_ref[...], hidT.astype(BF16), preferred_element_type=F32).astype(BF16)


def _compress(kc2, vc2, wk, wvT, pek, pev, w2k, w2vT):
    B, nc, _ = kc2.shape
    full = lambda a: pl.BlockSpec(a.shape, lambda b: (0,) * a.ndim)
    blk = pl.BlockSpec((1, nc, kc2.shape[2]), lambda b: (b, 0, 0))
    return pl.pallas_call(
        _compress_kernel, grid=(B,),
        in_specs=[blk, blk, full(wk), full(wvT), full(pek), full(pev), full(w2k), full(w2vT)],
        out_specs=[pl.BlockSpec((1, nc, 128), lambda b: (b, 0, 0)),
                   pl.BlockSpec((1, 128, nc), lambda b: (b, 0, 0))],
        out_shape=[jax.ShapeDtypeStruct((B, nc, 128), BF16),
                   jax.ShapeDtypeStruct((B, 128, nc), BF16)],
        compiler_params=_params(1), name="nsa_compress",
    )(kc2, vc2, wk, wvT, pek, pev, w2k, w2vT)


def _nsa_kernel(qT_ref, kcmp_ref, vcmpT_ref, ks_ref, vsT_ref, kw_ref, vwT_ref,
                g_ref, sl_ref, srow_ref, auga_ref, augb_ref, augc_ref, tblc_ref, bw_ref, o_ref,
                s_ref, p_ref, pc_ref, *, n_slc, topn):
    g = pl.program_id(1)
    qi = pl.program_id(2)
    tq = NSA_TQ
    hg = NSA_GROUP
    wl = hg * tq
    q0 = qi * tq

    q4 = qT_ref[0]
    qT = jnp.concatenate([q4[h * HEAD_DIM:(h + 1) * HEAD_DIM] for h in range(hg)], axis=1)
    qT2 = jnp.concatenate([qT, qT], axis=0)
    rowi = lax.broadcasted_iota(I32, qT2.shape, 0)
    qpad = jnp.where((rowi >> 6) == g, qT2, jnp.zeros_like(qT2))
    slope = sl_ref[0]
    lane = lax.broadcasted_iota(I32, (1, wl), 1)
    t_q = q0 + (lane & (tq - 1))

    nc = kcmp_ref.shape[1]
    rhs_top = jnp.where((rowi >> 6) == g, qT2, srow_ref[0])
    mine_c = (lax.broadcasted_iota(I32, (nc, LANES), 1) >> 6) == g
    lhs_c = jnp.where(mine_c, kcmp_ref[0], augc_ref[0])
    first_c = pl.multiple_of(nc - qi * (tq // NSA_CMP_STRIDE), 8)
    z = jnp.dot(lhs_c, rhs_top, preferred_element_type=F32) + tblc_ref[pl.ds(first_c, nc), :]
    mx = jnp.max(z, axis=0, keepdims=True)
    e = jnp.exp2(z - mx)
    den = jnp.maximum(jnp.sum(e, axis=0, keepdims=True), 1e-30)
    p = e * jnp.where(t_q >= NSA_CMP_LEN - 1, 1.0 / den, 0.0)
    o_c = jnp.dot(vcmpT_ref[0], p.astype(BF16), preferred_element_type=F32)

    pc = p[:, 0:tq]
    for h in range(1, hg):
        pc = pc + p[:, h * tq:(h + 1) * tq]
    n_lc = tq // LANES
    for c in range(n_lc):
        pc_ref[c] = pc[:, c * LANES:(c + 1) * LANES]
    su = NSA_SLC_BLOCK // NSA_CMP_STRIDE
    x = [jnp.concatenate([pc_ref[c, pl.ds(k, n_slc, stride=su), :] for c in range(n_lc)], axis=1)
         for k in range(su)]
    jb = lax.broadcasted_iota(I32, (n_slc, tq), 0)
    prev = jnp.where(jb == 0, 0.0, pltpu.roll(x[3], 1, 0))
    imp = 2.0 * (x[0] + x[1] + x[2]) + x[3] + prev
    cur = (q0 + lax.broadcasted_iota(I32, (1, tq), 1)) >> 6
    allowed = jb <= cur
    forced = jnp.logical_or(jb == 0, jnp.logical_or(jb == cur, jb == cur - 1))
    bias = jnp.where(jnp.logical_and(allowed, forced), 0.0, NEG_BIG)
    sc = jnp.where(jnp.logical_and(allowed, jnp.logical_not(forced)), imp, -1.0)
    for _ in range(topn - 3):
        smx = jnp.max(sc, axis=0, keepdims=True)
        idx = jnp.min(jnp.where(sc == smx, jb, n_slc), axis=0, keepdims=True)
        pick = jnp.logical_and(jb == idx, smx >= 0.0)
        bias = jnp.where(pick, 0.0, bias)
        sc = jnp.where(pick, -1.0, sc)

    wc = NSA_CHAIN_LANES
    n_ch = wl // wc

    def lane_split(a):
        return [a[:, c * wc:(c + 1) * wc] for c in range(n_ch)]

    if n_slc < LANES:
        bias = jnp.concatenate([bias, jnp.zeros((LANES - n_slc, tq), F32)], axis=0)
    bias4 = jnp.concatenate([bias.astype(BF16)] * hg, axis=1)
    rhs = jnp.concatenate([rhs_top, bias4], axis=0)
    mine = (lax.broadcasted_iota(I32, (SLC_TILE, LANES), 1) >> 6) == g
    n_tiles = augb_ref.shape[0]

    def scores(j, null):
        k0 = pl.multiple_of(j * SLC_TILE, SLC_TILE)
        kt = ks_ref[0, pl.ds(k0, SLC_TILE), :]
        lhs = jnp.concatenate([jnp.where(mine, kt, auga_ref[null]), augb_ref[j]], axis=1)
        return jnp.dot(lhs, rhs, preferred_element_type=F32)

    def values(j):
        return [vsT_ref[0, :, pl.ds(pl.multiple_of(j * SLC_TILE, SLC_TILE), SLC_TILE)]] * n_ch

    def offsets(j):
        return lane_split(slope * (q0 - j * SLC_TILE).astype(F32))

    jd = lax.div(q0, SLC_TILE)
    t_k = jd * SLC_TILE + lax.broadcasted_iota(I32, (SLC_TILE, wl), 0)
    s_diag = lane_split(jnp.where(t_k <= t_q, scores(jd, 0), NEG_BIG))
    c_diag = offsets(jd)

    def first():
        return [_softmax_stage(s_diag[c], c_diag[c], jnp.full((1, wc), NEG_BIG, F32))
                for c in range(n_ch)]

    def past_scores(j):
        return lane_split(scores(jnp.minimum(j, n_tiles - 1), (j >= jd).astype(I32)))
    o_s = jnp.concatenate(
        _pipelined_tiles(past_scores, values, offsets, s_ref, p_ref, first, n_ch, jd, jd), axis=1)

    start = pl.multiple_of(jnp.maximum(q0 - NSA_WINDOW, 0), tq)
    first_w = pl.multiple_of(NSA_WINDOW - (q0 - start), tq)
    kt = kw_ref[0, pl.ds(start, WIN_KEYS), :]
    z = jnp.dot(kt, qpad, preferred_element_type=F32) + bw_ref[0, pl.ds(first_w, WIN_KEYS), :]
    mx = jnp.max(z, axis=0, keepdims=True)
    p = jnp.exp2(z - mx)
    den = jnp.maximum(jnp.sum(p, axis=0, keepdims=True), 1e-30)
    o_w = jnp.dot(vwT_ref[0, :, pl.ds(start, WIN_KEYS)], p.astype(BF16),
                  preferred_element_type=F32) / den

    gt = g_ref[0]

    def gate_row(br):
        return jnp.concatenate([gt[br * hg + h:br * hg + h + 1] for h in range(hg)], axis=1)

    o = gate_row(0) * o_c + gate_row(1) * o_s + gate_row(2) * o_w
    o4 = jnp.concatenate([o[:, h * tq:(h + 1) * tq] for h in range(hg)], axis=0)
    o_ref[0] = o4.T.astype(BF16)


def _nsa(nqT, kcmp, vcmpT, ks, vsT, kw, vwT, gT, sl, srow, auga, augb, augc, tblc, bw):
    B, _, S = nqT.shape
    tq = NSA_TQ
    nc = kcmp.shape[1]
    n_slc = S // NSA_SLC_BLOCK
    topn = min(NSA_SLC_TOPN, n_slc)
    wl = NSA_GROUP * tq
    grid = (B, NSA_KV_HEADS, S // tq)
    return pl.pallas_call(
        functools.partial(_nsa_kernel, n_slc=n_slc, topn=topn), grid=grid,
        in_specs=[
            pl.BlockSpec((1, NSA_GROUP * HEAD_DIM, tq), lambda b, g, i: (b, g, i)),
            pl.BlockSpec((1, nc, 128), lambda b, g, i: (b, 0, 0)),
            pl.BlockSpec((1, HEAD_DIM, nc), lambda b, g, i: (b, g, 0)),
            pl.BlockSpec((1, S, 128), lambda b, g, i: (b, 0, 0)),
            pl.BlockSpec((1, HEAD_DIM, S), lambda b, g, i: (b, g, 0)),
            pl.BlockSpec((1, S, 128), lambda b, g, i: (b, 0, 0)),
            pl.BlockSpec((1, HEAD_DIM, S), lambda b, g, i: (b, g, 0)),
            pl.BlockSpec((1, 16, tq), lambda b, g, i: (b, g, i)),
            pl.BlockSpec((1, 1, wl), lambda b, g, i: (g, 0, 0)),
            pl.BlockSpec((1, 2 * HEAD_DIM, wl), lambda b, g, i: (g, 0, 0)),
            pl.BlockSpec((2, SLC_TILE, LANES), lambda b, g, i: (g, 0, 0)),
            pl.BlockSpec(augb.shape, lambda b, g, i: (0, 0, 0)),
            pl.BlockSpec((1, nc, LANES), lambda b, g, i: (g, 0, 0)),
            pl.BlockSpec(tblc.shape, lambda b, g, i: (0, 0)),
            pl.BlockSpec((1,) + bw.shape[1:], lambda b, g, i: (g, 0, 0)),
        ],
        out_specs=pl.BlockSpec((1, tq, NSA_GROUP * HEAD_DIM), lambda b, g, i: (b, i, g)),
        out_shape=jax.ShapeDtypeStruct((B, S, 512), BF16),
        scratch_shapes=[
            pltpu.VMEM((2, wl // NSA_CHAIN_LANES, SLC_TILE, NSA_CHAIN_LANES), F32),
            pltpu.VMEM((2, wl // NSA_CHAIN_LANES, SLC_TILE, NSA_CHAIN_LANES), BF16),
            pltpu.VMEM((tq // LANES, nc, LANES), F32),
        ],
        compiler_params=_params(3), name="nsa",
    )(nqT, kcmp, vcmpT, ks, vsT, kw, vwT, gT, sl, srow, auga, augb, augc, tblc, bw)


def _outproj_kernel(om_ref, on_ref, x_ref, wo_ref, g_ref, wr_ref, br_ref,
                    x1_ref, hn_ref, e_ref, w_ref):
    attn = (jnp.dot(om_ref[...], wo_ref[0:512, :], preferred_element_type=F32)
            + jnp.dot(on_ref[...], wo_ref[512:1024, :], preferred_element_type=F32))
    x1 = x_ref[...] + attn
    x1_ref[...] = x1
    hn = _rmsnorm(x1, g_ref[...])
    _store_token_tiles(hn_ref, hn)
    logits = jnp.dot(hn, wr_ref[...], precision=lax.Precision.HIGHEST,
                     preferred_element_type=F32) + br_ref[...]
    tm = logits.shape[0]
    lane = lax.broadcasted_iota(I32, (tm, LANES), 1)
    sc = jnp.where(lane < N_EXPERTS, logits, -jnp.inf)
    e_out = jnp.zeros((tm, LANES), I32)
    vals = []
    for k in range(TOP_K):
        mx = jnp.max(sc, axis=1, keepdims=True)
        idx = jnp.min(jnp.where(sc == mx, lane, LANES), axis=1, keepdims=True)
        e_out = jnp.where(lane == k, idx, e_out)
        sc = jnp.where(lane == idx, -jnp.inf, sc)
        vals.append(mx)
    ex = [jnp.exp(v - vals[0]) for v in vals]
    den = ex[0] + ex[1] + ex[2] + ex[3]
    w_out = jnp.zeros((tm, LANES), F32)
    for k in range(TOP_K):
        w_out = jnp.where(lane == k, ex[k] / den, w_out)
    e_ref[...] = e_out
    w_ref[...] = w_out


def _outproj(om, on, x, wo, g, wr, br, tm):
    N, D = x.shape
    full = lambda a: pl.BlockSpec(a.shape, lambda i: (0,) * a.ndim)
    row = lambda w: pl.BlockSpec((tm, w), lambda i: (i, 0))
    return pl.pallas_call(
        _outproj_kernel, grid=(N // tm,),
        in_specs=[row(512), row(512), row(D), full(wo), full(g), full(wr), full(br)],
        out_specs=[row(D), pl.BlockSpec((tm * SUBLANES, LANES), lambda i: (i, 0)), row(LANES), row(LANES)],
        out_shape=[jax.ShapeDtypeStruct((N, D), F32), jax.ShapeDtypeStruct((N * SUBLANES, LANES), F32),
                   jax.ShapeDtypeStruct((N, LANES), I32), jax.ShapeDtypeStruct((N, LANES), F32)],
        compiler_params=_params(1), name="outproj_router",
    )(om, on, x, wo, g, wr, br)


def _rank_kernel(e_ref, rank_ref, cnt_ref, base_ref):
    i = pl.program_id(0)
    T = e_ref.shape[0]

    @pl.when(i == 0)
    def _():
        base_ref[...] = jnp.zeros(base_ref.shape, F32)

    e = e_ref[...]
    lane = lax.broadcasted_iota(I32, (T, LANES), 1)
    tril = jnp.where(lax.broadcasted_iota(I32, (T, T), 0) >= lax.broadcasted_iota(I32, (T, T), 1),
                     1.0, 0.0).astype(BF16)
    out = jnp.zeros((T, LANES), I32)
    for k in range(TOP_K):
        hit = lane == e[:, k:k + 1]
        oh = jnp.where(hit, 1.0, 0.0)
        cum = jnp.dot(tril, oh.astype(BF16), preferred_element_type=F32)
        base = base_ref[0:1, :]
        r = jnp.sum(jnp.where(hit, cum - 1.0 + base, 0.0), axis=1, keepdims=True)
        out = jnp.where(lane == k, r.astype(I32), out)
        base_ref[...] = base_ref[...] + jnp.sum(oh, axis=0, keepdims=True)
    rank_ref[...] = out
    cnt_ref[...] = base_ref[...]


def _ranks(e128):
    N = e128.shape[0]
    T = RANK_TILE
    return pl.pallas_call(
        _rank_kernel, grid=(N // T,),
        in_specs=[pl.BlockSpec((T, LANES), lambda i: (i, 0))],
        out_specs=[pl.BlockSpec((T, LANES), lambda i: (i, 0)),
                   pl.BlockSpec((8, LANES), lambda i: (0, 0))],
        out_shape=[jax.ShapeDtypeStruct((N, LANES), I32),
                   jax.ShapeDtypeStruct((8, LANES), F32)],
        scratch_shapes=[pltpu.VMEM((8, LANES), F32)],
        compiler_params=_params(1), name="route_ranks",
    )(e128)


def _row_copy(src, dst, i_src, i_dst, sem):
    return pltpu.make_async_copy(src.at[pl.ds(pl.multiple_of(i_src * SUBLANES, SUBLANES), SUBLANES)],
                                 dst.at[pl.ds(pl.multiple_of(i_dst * SUBLANES, SUBLANES), SUBLANES)], sem)


def _store_token_tiles(ref, x):
    rows = x.shape[0]
    for c in range(SUBLANES):
        ref[pl.ds(c, rows, stride=SUBLANES), :] = x[:, c * LANES:(c + 1) * LANES]


def _load_token_tiles(ref, rows):
    return jnp.concatenate([ref[pl.ds(c, rows, stride=SUBLANES), :] for c in range(SUBLANES)], axis=1)


def _dispatch_kernel(dest_hbm, hp_ref, xz_hbm, out_hbm, idx_ref, isem, sem):
    del xz_hbm
    i = pl.program_id(0)
    T = ROUTE_TILE
    cp = pltpu.make_async_copy(dest_hbm.at[i], idx_ref, isem)
    cp.start()
    cp.wait()

    def issue(t, carry):
        for k in range(TOP_K):
            _row_copy(hp_ref, out_hbm, t, idx_ref[t * TOP_K + k], sem).start(priority=k % 2)
        return carry
    lax.fori_loop(0, T, issue, 0, unroll=ROW_DMA_UNROLL)

    def drain(t, carry):
        for k in range(TOP_K):
            _row_copy(hp_ref, out_hbm, 0, 0, sem).wait()
        return carry
    lax.fori_loop(0, T, drain, 0, unroll=ROW_DMA_UNROLL)


def _dispatch(dest2, hp, xzero):
    nsteps = dest2.shape[0]
    T = ROUTE_TILE
    return pl.pallas_call(
        _dispatch_kernel, grid=(nsteps,),
        in_specs=[pl.BlockSpec(memory_space=pl.ANY),
                  pl.BlockSpec((T * SUBLANES, LANES), lambda i: (i, 0)),
                  pl.BlockSpec(memory_space=pl.ANY)],
        out_specs=pl.BlockSpec(memory_space=pl.ANY),
        out_shape=jax.ShapeDtypeStruct(xzero.shape, xzero.dtype),
        scratch_shapes=[pltpu.SMEM((T * TOP_K,), I32),
                        pltpu.SemaphoreType.DMA, pltpu.SemaphoreType.DMA],
        input_output_aliases={2: 0},
        compiler_params=_params(1), name="moe_dispatch",
    )(dest2, hp, xzero)


W_PREP_COLS = 256


def _expert_kernel(be_ref, na_ref, x_ref, wup_ref, bg_ref, bu_ref, wd_ref, bd_ref, y_ref,
                   wg_ref, wu_ref, wt_ref):
    b = pl.program_id(0)
    active = b < na_ref[0]
    new_expert = jnp.logical_or(b == 0, be_ref[b] != be_ref[jnp.maximum(b - 1, 0)])

    @pl.when(jnp.logical_and(active, new_expert))
    def _():
        half = W_PREP_COLS // 2
        n_lc = wt_ref.shape[0]

        def every_other(first):
            return jnp.concatenate(
                [wt_ref[c, pl.ds(first, half, stride=2), :] for c in range(n_lc)], axis=1).astype(BF16)

        for ch in range(wup_ref.shape[2] // W_PREP_COLS):
            panel = wup_ref[0, :, ch * W_PREP_COLS:(ch + 1) * W_PREP_COLS].T
            for c in range(n_lc):
                wt_ref[c] = panel[:, c * LANES:(c + 1) * LANES]
            wg_ref[ch * half:(ch + 1) * half, :] = every_other(0)
            wu_ref[ch * half:(ch + 1) * half, :] = every_other(1)

    @pl.when(active)
    def _():
        xb = _load_token_tiles(x_ref, MOE_ROWS).astype(BF16)
        gg = lax.dot_general(xb, wg_ref[...], NT_DIMS, preferred_element_type=F32) + bg_ref[0]
        uu = lax.dot_general(xb, wu_ref[...], NT_DIMS, preferred_element_type=F32) + bu_ref[0]
        gg = jnp.minimum(gg, SWIGLU_LIMIT)
        uu = jnp.clip(uu, -SWIGLU_LIMIT, SWIGLU_LIMIT)
        a = gg * jax.nn.sigmoid(SWIGLU_ALPHA * gg) * (uu + 1.0)
        _store_token_tiles(y_ref, jnp.dot(a.astype(BF16), wd_ref[0], preferred_element_type=F32) + bd_ref[0])

    @pl.when(jnp.logical_not(active))
    def _():
        y_ref[...] = jnp.zeros(y_ref.shape, F32)


def _experts(blk_e, n_act, xrows, w_up, bg, bu, wd, bd):
    _, D, F2 = w_up.shape
    F = F2 // 2
    assert D == SUBLANES * LANES
    P = xrows.shape[0] // SUBLANES
    n_blk = P // MOE_ROWS
    rows_spec = pl.BlockSpec((MOE_ROWS * SUBLANES, LANES), lambda b, be, na: (b, 0))
    wspec = lambda r, c: pl.BlockSpec((1, r, c), lambda b, be, na: (be[b], 0, 0))
    grid_spec = pltpu.PrefetchScalarGridSpec(
        num_scalar_prefetch=2, grid=(n_blk,),
        in_specs=[rows_spec, wspec(D, F2), wspec(1, F), wspec(1, F), wspec(F, D), wspec(1, D)],
        out_specs=rows_spec,
        scratch_shapes=[pltpu.VMEM((F, D), BF16), pltpu.VMEM((F, D), BF16),
                        pltpu.VMEM((D // LANES, W_PREP_COLS, LANES), F32)],
    )
    return pl.pallas_call(
        _expert_kernel, grid_spec=grid_spec,
        out_shape=jax.ShapeDtypeStruct(xrows.shape, F32),
        compiler_params=_params(1), name="moe_experts",
    )(blk_e, n_act, xrows, w_up, bg, bu, wd, bd)


def _combine_kernel(dest_hbm, x1_ref, w_ref, g_ref, y_hbm, o_ref, idx_ref, buf_ref, isem, sem):
    i = pl.program_id(0)
    T = ROUTE_TILE
    cp = pltpu.make_async_copy(dest_hbm.at[i], idx_ref, isem)
    cp.start()
    cp.wait()

    def issue(t, carry):
        for k in range(TOP_K):
            _row_copy(y_hbm, buf_ref.at[k], idx_ref[t * TOP_K + k], t, sem).start(priority=k % 2)
        return carry
    lax.fori_loop(0, T, issue, 0, unroll=ROW_DMA_UNROLL)

    def drain(t, carry):
        for k in range(TOP_K):
            _row_copy(y_hbm, buf_ref.at[k], 0, 0, sem).wait()
        return carry
    lax.fori_loop(0, T, drain, 0, unroll=ROW_DMA_UNROLL)

    x1 = x1_ref[...]
    w = w_ref[...]
    cols = []
    for c in range(SUBLANES):
        acc = x1[:, c * LANES:(c + 1) * LANES]
        for k in range(TOP_K):
            acc = acc + w[:, k:k + 1] * buf_ref[k, pl.ds(c, T, stride=SUBLANES), :]
        cols.append(acc)
    o_ref[...] = _rmsnorm(jnp.concatenate(cols, axis=1), g_ref[...])


def _combine(dest2, x1, w128, g, yrows):
    N, D = x1.shape
    T = ROUTE_TILE
    return pl.pallas_call(
        _combine_kernel, grid=(N // T,),
        in_specs=[pl.BlockSpec(memory_space=pl.ANY),
                  pl.BlockSpec((T, D), lambda i: (i, 0)),
                  pl.BlockSpec((T, LANES), lambda i: (i, 0)),
                  pl.BlockSpec(g.shape, lambda i: (0, 0)),
                  pl.BlockSpec(memory_space=pl.ANY)],
        out_specs=pl.BlockSpec((T, D), lambda i: (i, 0)),
        out_shape=jax.ShapeDtypeStruct((N, D), F32),
        scratch_shapes=[pltpu.SMEM((T * TOP_K,), I32),
                        pltpu.VMEM((TOP_K, T * SUBLANES, LANES), F32),
                        pltpu.SemaphoreType.DMA, pltpu.SemaphoreType.DMA],
        compiler_params=_params(1), name="moe_combine",
    )(dest2, x1, w128, g, yrows)


def _alibi_slopes():
    n = MOBA_HEADS + NSA_HEADS
    s = jnp.exp2(-8.0 * jnp.arange(1, n + 1, dtype=F32) / n)
    return s[0::2], s[1::2]


def _prep_inproj(w_in):
    hd = HEAD_DIM
    sizes = [MOBA_HEADS * hd] * 3 + [NSA_HEADS * hd] + [NSA_KV_HEADS * hd] * 6 + [NSA_BRANCHES * NSA_HEADS]
    cuts = np.cumsum([0] + sizes)
    mq, mk, mv, nq, kc, vc, ks, vs, kw, vw, ng = [w_in[:, cuts[i]:cuts[i + 1]] for i in range(11)]
    qscale = (hd ** -0.5) * LOG2E
    wr = jnp.concatenate([mk, kc, vc, ks, kw], axis=1).astype(BF16)
    wt = jnp.concatenate([mq * qscale, mv, nq * qscale, vs, vw], axis=1).T.astype(BF16)
    ngr = ng.reshape(-1, NSA_KV_HEADS, NSA_GROUP, NSA_BRANCHES).transpose(1, 3, 2, 0)
    ngr = ngr.reshape(NSA_KV_HEADS, NSA_BRANCHES * NSA_GROUP, -1)
    wg = jnp.pad(ngr, ((0, 0), (0, 16 - NSA_BRANCHES * NSA_GROUP), (0, 0))).reshape(32, -1)
    return wr, wt, wg.astype(F32)


def _prep_compress(w1, w2, pe):
    hd, half = HEAD_DIM, NSA_CMP_STRIDE
    w1r = w1.reshape(2, half, hd, hd)
    eye = jnp.eye(NSA_KV_HEADS, dtype=w1.dtype)
    w = jnp.einsum('alde,gh->lgdahe', w1r, eye).reshape(half * NSA_KV_HEADS * hd, 2 * NSA_KV_HEADS * hd)
    w2b = jnp.einsum('de,gh->gdhe', w2, eye).reshape(NSA_KV_HEADS * hd, NSA_KV_HEADS * hd)
    per = pe.reshape(2, half, 1, hd)
    pe2 = jnp.broadcast_to(per, (2, half, NSA_KV_HEADS, hd)).reshape(2, 1, half * NSA_KV_HEADS * hd)
    pe2 = jnp.broadcast_to(pe2, (2, 8, pe2.shape[2]))
    return w.astype(BF16), w2b.astype(BF16), pe2.astype(F32)


def _attention_tables(S):
    moba_sl, nsa_sl = _alibi_slopes()
    moba_sl = moba_sl * LOG2E
    nsa_sl = nsa_sl * LOG2E
    blk = MOBA_BLOCK
    moba_row = jnp.broadcast_to(moba_sl[:, None, None], (MOBA_HEADS, 1, blk))
    nb = S // blk
    nbp = -(-nb // 16) * 16
    col = jnp.arange(LANES)[None, None, :]
    tile = jnp.arange(nb + 1)[:, None, None]
    off = jnp.arange(blk, dtype=F32)[None, :, None]
    moba_aug = jnp.where(jnp.logical_and(col == tile, tile < nb), 1.0,
                         jnp.where(jnp.logical_and(col >= nbp, col < nbp + 3), off, 0.0)).astype(BF16)
    parts = jnp.stack(list(_split3(moba_sl)) + [jnp.zeros_like(moba_sl)] * 13, axis=1)
    moba_srow = jnp.broadcast_to(parts[:, :, None], (MOBA_HEADS, 16, blk)).astype(BF16)

    wl = NSA_GROUP * NSA_TQ
    n_slc = S // NSA_SLC_BLOCK
    assert n_slc <= LANES, "block-choice rows must fit the spare contraction rows"
    nsa_row = jnp.repeat(nsa_sl.reshape(NSA_KV_HEADS, NSA_GROUP), NSA_TQ, axis=1)
    hi, mid, lo = [t[:, None, :] for t in _split3(nsa_row)]
    base = ((1 - jnp.arange(NSA_KV_HEADS)) * HEAD_DIM)[:, None, None]
    rows = jnp.arange(2 * HEAD_DIM)[None, :, None]
    nsa_srow = jnp.where(rows == base, hi, jnp.where(rows == base + 1, mid, jnp.where(
        rows == base + 2, lo, jnp.where(rows == base + 3, NEG_BIG, 0.0)))).astype(BF16)
    lane = jnp.arange(LANES)[None, None, None, :]
    base4 = base[:, None]
    null = jnp.arange(2, dtype=F32)[None, :, None, None]
    koff = jnp.arange(SLC_TILE, dtype=F32)[None, None, :, None]
    nsa_auga = jnp.where(jnp.logical_and(lane >= base4, lane < base4 + 3), koff,
                         jnp.where(lane == base4 + 3, null, 0.0))
    nsa_auga = nsa_auga.reshape(2 * NSA_KV_HEADS, SLC_TILE, LANES).astype(BF16)
    per = SLC_TILE // NSA_SLC_BLOCK
    tile = jnp.arange(S // SLC_TILE)[:, None, None]
    blk_of = tile * per + jnp.arange(SLC_TILE)[None, :, None] // NSA_SLC_BLOCK
    nsa_augb = (jnp.arange(LANES)[None, None, :] == blk_of).astype(BF16)

    nc = S // NSA_CMP_STRIDE
    ci = jnp.arange(nc)[None, :, None]
    lane3 = jnp.arange(LANES)[None, None, :]
    nsa_augc = jnp.where(jnp.logical_and(lane3 >= base + 4, lane3 < base + 7), (ci >> 1).astype(F32),
                         jnp.where(jnp.logical_and(lane3 >= base + 7, lane3 < base + 10),
                                   (ci & 1).astype(F32), 0.0)).astype(BF16)
    step2 = [t[:, None, :] for t in _split3(nsa_row * (2.0 * NSA_CMP_STRIDE))]
    step1 = [t[:, None, :] for t in _split3(nsa_row * (1.0 * NSA_CMP_STRIDE))]
    for k in range(3):
        nsa_srow = jnp.where(rows == base + 4 + k, step2[k].astype(BF16),
                             jnp.where(rows == base + 7 + k, step1[k].astype(BF16), nsa_srow))
    il = jnp.tile(jnp.arange(NSA_TQ), NSA_GROUP)[None, :]
    rel = (jnp.arange(2 * nc) - nc)[:, None]
    nsa_tblc = jnp.where(rel * NSA_CMP_STRIDE + (NSA_CMP_LEN - 1) <= il, 0.0, NEG_BIG).astype(F32)
    dist = (NSA_WINDOW + il - jnp.arange(NSA_WINDOW + WIN_KEYS)[:, None])[None]
    nsa_bw = jnp.where(jnp.logical_and(dist >= 0, dist < NSA_WINDOW),
                       -nsa_row[:, None, :] * dist.astype(F32), NEG_BIG)
    return ((moba_aug, moba_srow, moba_row),
            (nsa_row.reshape(NSA_KV_HEADS, 1, wl), nsa_srow, nsa_auga, nsa_augb, nsa_augc, nsa_tblc, nsa_bw))


def _split3(x):
    hi = x.astype(BF16).astype(F32)
    mid = (x - hi).astype(BF16).astype(F32)
    lo = (x - hi - mid).astype(BF16).astype(F32)
    return hi, mid, lo


def _attention(x, attn_norm_g, w_in, cmp_pe_k, cmp_pe_v, cmp_w1_k, cmp_w2_k, cmp_w1_v, cmp_w2_v):
    B, S, D = x.shape
    wr, wt, wg = _prep_inproj(w_in)
    (mk, kc, vc, ks, kw, mqT, mvT, nqT, vsT, vwT, gT) = _inproj(
        x, attn_norm_g.reshape(1, D), wr, wt, wg, tm=512)
    moba_tabs, nsa_tabs = _attention_tables(S)
    o_moba = _moba(mqT, mk, mvT, *moba_tabs)
    wk, w2k, pek = _prep_compress(cmp_w1_k, cmp_w2_k, cmp_pe_k)
    wv, w2v, pev = _prep_compress(cmp_w1_v, cmp_w2_v, cmp_pe_v)
    nc = S // NSA_CMP_STRIDE
    kcmp, vcmpT = _compress(kc.reshape(B, nc, -1), vc.reshape(B, nc, -1), wk, wv.T, pek, pev, w2k, w2v.T)
    o_nsa = _nsa(nqT, kcmp, vcmpT, ks, vsT, kw, vwT, gT, *nsa_tabs)
    return o_moba, o_nsa


def _moe(x1, hn, e128, w128, w_up, b_up, w_down, b_down, final_norm_g):
    N, D = x1.shape
    rank128, cnt = _ranks(e128)
    counts = cnt[0, :N_EXPERTS].astype(I32)
    padded = (counts + MOE_ROWS - 1) // MOE_ROWS * MOE_ROWS
    pends = jnp.cumsum(padded)
    pstarts = pends - padded
    e4 = e128[:, :TOP_K]
    dest = pstarts[e4] + rank128[:, :TOP_K]
    dest2 = dest.reshape(N // ROUTE_TILE, ROUTE_TILE * TOP_K)
    n_blk = (N * TOP_K + N_EXPERTS * MOE_ROWS + MOE_ROWS - 1) // MOE_ROWS
    P = n_blk * MOE_ROWS
    blk_start = jnp.arange(n_blk, dtype=I32) * MOE_ROWS
    blk_e = jnp.minimum(jnp.sum((pends[None, :] <= blk_start[:, None]).astype(I32), axis=1), N_EXPERTS - 1)
    n_act = (pends[-1:] // MOE_ROWS).astype(I32)
    xrows = _dispatch(dest2, hn, jnp.zeros((P * SUBLANES, LANES), F32))
    bg = b_up[:, None, 0::2]
    bu = b_up[:, None, 1::2]
    yrows = _experts(blk_e, n_act, xrows, w_up, bg, bu, w_down.astype(BF16), b_down[:, None, :])
    return _combine(dest2, x1, w128, final_norm_g.reshape(1, D), yrows)


def kernel(x, attn_norm_g, w_in, cmp_pe_k, cmp_pe_v, cmp_w1_k, cmp_w2_k, cmp_w1_v, cmp_w2_v, w_out, ffn_norm_g, w_router, b_router, w_up, b_up, w_down, b_down, final_norm_g):
    B, S, D = x.shape
    assert attn_norm_g.shape[0] == 1, "single-layer kernel"
    o_moba, o_nsa = _attention(x, attn_norm_g[0], w_in[0], cmp_pe_k[0], cmp_pe_v[0],
                               cmp_w1_k[0], cmp_w2_k[0], cmp_w1_v[0], cmp_w2_v[0])
    N = B * S
    wr = jnp.pad(w_router[0], ((0, 0), (0, LANES - N_EXPERTS)))
    br = jnp.pad(b_router[0], (0, LANES - N_EXPERTS)).reshape(1, LANES)
    x1, hn, e128, w128 = _outproj(o_moba.reshape(N, -1), o_nsa.reshape(N, -1), x.reshape(N, D),
                                  w_out[0].astype(BF16), ffn_norm_g[0].reshape(1, D), wr, br, tm=512)
    out = _moe(x1, hn, e128, w128, w_up[0], b_up[0], w_down[0], b_down[0], final_norm_g)
    return out.reshape(B, S, D)
```

```python
import functools

import jax
import jax.numpy as jnp
import numpy as np
from jax import lax
from jax.experimental import pallas as pl
from jax.experimental.pallas import tpu as pltpu

F32 = jnp.float32
BF16 = jnp.bfloat16
I32 = jnp.int32

HEAD_DIM = 64
MOBA_HEADS = 8
NSA_HEADS = 8
NSA_KV_HEADS = 2
NSA_GROUP = NSA_HEADS // NSA_KV_HEADS
MOBA_BLOCK = 256
MOBA_TOPK = 3
NSA_CMP_LEN = 32
NSA_CMP_STRIDE = 16
NSA_SLC_BLOCK = 64
NSA_SLC_TOPN = 16
NSA_WINDOW = 512
NSA_BRANCHES = 3
N_EXPERTS = 32
TOP_K = 4
SWIGLU_LIMIT = 7.0
SWIGLU_ALPHA = 1.702
RMS_EPS = 1e-5
NEG_BIG = -1e30
LOG2E = 1.4426950408889634

LANES = 128
SUBLANES = 8
VMEM_LIMIT = 56 * 1024 * 1024

NSA_TQ = 256
NSA_CHAIN_LANES = 256
TILE_LIST_LEN = 64
DIAG_SLOT = TILE_LIST_LEN - 1
SLC_TILE = 256
WIN_KEYS = NSA_WINDOW + NSA_TQ
MOE_ROWS = 512
ROUTE_TILE = 256
RANK_TILE = 512
ROW_DMA_UNROLL = 8

NT_DIMS = (((1,), (1,)), ((), ()))


def _params(n_grid):
    return pltpu.CompilerParams(
        dimension_semantics=("arbitrary",) * n_grid,
        vmem_limit_bytes=VMEM_LIMIT,
    )


def _rmsnorm(x, g):
    return x * lax.rsqrt(jnp.mean(x * x, axis=-1, keepdims=True) + RMS_EPS) * g


def _inproj_kernel(x_ref, g_ref, wr_ref, wt_ref, wg_ref,
                   mk_ref, kc_ref, vc_ref, ks_ref, kw_ref,
                   mqT_ref, mvT_ref, nqT_ref, vsT_ref, vwT_ref, gT_ref):
    xn = _rmsnorm(x_ref[0], g_ref[...])
    xb = xn.astype(BF16)
    yr = jnp.dot(xb, wr_ref[...], preferred_element_type=F32)
    mk_ref[0] = yr[:, 0:512].astype(BF16)
    kc_ref[0] = yr[:, 512:640].astype(BF16)
    vc_ref[0] = yr[:, 640:768].astype(BF16)
    ks_ref[0] = yr[:, 768:896].astype(BF16)
    kw_ref[0] = yr[:, 896:1024].astype(BF16)
    yt = lax.dot_general(wt_ref[...], xb, NT_DIMS, preferred_element_type=F32)
    mqT_ref[0] = yt[0:512].astype(BF16)
    mvT_ref[0] = yt[512:1024].astype(BF16)
    nqT_ref[0] = yt[1024:1536].astype(BF16)
    vsT_ref[0] = yt[1536:1664].astype(BF16)
    vwT_ref[0] = yt[1664:1792].astype(BF16)
    gl = lax.dot_general(wg_ref[...], xn, NT_DIMS, precision=lax.Precision.HIGHEST,
                         preferred_element_type=F32)
    gT_ref[0] = jax.nn.sigmoid(gl)


def _inproj(x, g, wr, wt, wg, tm):
    B, S, D = x.shape
    grid = (B, S // tm)
    row = lambda w: pl.BlockSpec((1, tm, w), lambda b, i: (b, i, 0))
    col = lambda h: pl.BlockSpec((1, h, tm), lambda b, i: (b, 0, i))
    full = lambda a: pl.BlockSpec(a.shape, lambda b, i: (0,) * a.ndim)
    out_shape = [
        jax.ShapeDtypeStruct((B, S, 512), BF16),
        jax.ShapeDtypeStruct((B, S, 128), BF16),
        jax.ShapeDtypeStruct((B, S, 128), BF16),
        jax.ShapeDtypeStruct((B, S, 128), BF16),
        jax.ShapeDtypeStruct((B, S, 128), BF16),
        jax.ShapeDtypeStruct((B, 512, S), BF16),
        jax.ShapeDtypeStruct((B, 512, S), BF16),
        jax.ShapeDtypeStruct((B, 512, S), BF16),
        jax.ShapeDtypeStruct((B, 128, S), BF16),
        jax.ShapeDtypeStruct((B, 128, S), BF16),
        jax.ShapeDtypeStruct((B, 32, S), F32),
    ]
    out_specs = [row(512), row(128), row(128), row(128), row(128),
                 col(512), col(512), col(512), col(128), col(128), col(32)]
    return pl.pallas_call(
        _inproj_kernel, grid=grid,
        in_specs=[pl.BlockSpec((1, tm, D), lambda b, i: (b, i, 0)),
                  full(g), full(wr), full(wt), full(wg)],
        out_specs=out_specs, out_shape=out_shape,
        compiler_params=_params(2), name="inproj",
    )(x, g, wr, wt, wg)


ONES_ROWS = 16


def _softmax_stage(s, c, m):
    mt = jnp.max(s, axis=0, keepdims=True) - c
    m_new = jnp.maximum(m, mt)
    alpha = jnp.exp2(m - m_new)
    p = jnp.exp2(s - (m_new + c))
    return m_new, p.astype(BF16), alpha


def _pipelined_tiles(scores, values, offsets, s_ref, p_ref, first, n_ch, n_tiles, j_first):
    chains = range(n_ch)

    def qk_into(slot, j):
        sc = scores(j)
        for c in chains:
            s_ref[slot, c] = sc[c]

    def pv_from(slot, j, alphas, accs):
        out = []
        for c, vt in zip(chains, values(j)):
            vt1 = jnp.concatenate([vt, jnp.ones((ONES_ROWS, vt.shape[1]), BF16)], axis=0)
            out.append(alphas[c] * accs[c] + jnp.dot(vt1, p_ref[slot, c], preferred_element_type=F32))
        return out

    def softmax_into(slot, j, ms):
        cs = offsets(j)
        new = [_softmax_stage(s_ref[slot, c], cs[c], ms[c]) for c in chains]
        for c in chains:
            p_ref[slot, c] = new[c][1]
        return [n[0] for n in new], [n[2] for n in new]

    qk_into(0, 0)
    first = first()
    for c in chains:
        p_ref[1, c] = first[c][1]

    def pair(i, carry):
        ms, alphas, accs, j_prev = carry
        t = 2 * i
        qk_into(1, t + 1)
        accs = pv_from(1, j_prev, alphas, accs)
        ms, alphas = softmax_into(0, t, ms)
        qk_into(0, t + 2)
        accs = pv_from(0, t, alphas, accs)
        ms, alphas = softmax_into(1, t + 1, ms)
        return ms, alphas, accs, t + 1

    n_q = first[0][0].shape[1]
    init = ([f[0] for f in first], [f[2] for f in first],
            [jnp.zeros((HEAD_DIM + ONES_ROWS, n_q), F32)] * n_ch, j_first)
    _, alphas, accs, j_last = lax.fori_loop(0, (n_tiles + 1) // 2, pair, init)
    accs = pv_from(1, j_last, alphas, accs)
    return [accs[c][:HEAD_DIM] / jnp.maximum(accs[c][HEAD_DIM:HEAD_DIM + 1], 1e-30) for c in chains]


def _moba_kernel(qT_ref, k_ref, vT_ref, aug_ref, srow_ref, sl_ref, o_ref,
                 kmean_ref, kparts_ref, s_ref, p_ref, *, nb, nbp, topk):
    qi = pl.program_id(2)
    blk = MOBA_BLOCK

    @pl.when(qi == 0)
    def _():
        kmean_ref[...] = jnp.zeros(kmean_ref.shape, F32)

        def body(n, carry):
            kb = k_ref[0, pl.ds(pl.multiple_of(n * blk, blk), blk), :].astype(F32)
            kmean_ref[pl.ds(n, 1), :] = jnp.mean(kb, axis=0, keepdims=True)
            return carry
        lax.fori_loop(0, nb, body, 0)
        km = kmean_ref[...]
        head = lax.broadcasted_iota(I32, km.shape, 1) >> 6
        km2 = jnp.concatenate([jnp.where(head == h, km, 0.0) for h in range(2)], axis=0)
        hi = km2.astype(BF16)
        mid = (km2 - hi.astype(F32)).astype(BF16)
        lo = (km2 - hi.astype(F32) - mid.astype(F32)).astype(BF16)
        kparts_ref[0] = hi
        kparts_ref[1] = mid
        kparts_ref[2] = lo

    qT = qT_ref[0]
    row = lax.broadcasted_iota(I32, qT.shape, 0)
    qpad = [jnp.where((row >> 6) == h, qT, jnp.zeros_like(qT)) for h in range(2)]

    gates = (jnp.dot(kparts_ref[0], qT, preferred_element_type=F32)
             + jnp.dot(kparts_ref[1], qT, preferred_element_type=F32)
             + jnp.dot(kparts_ref[2], qT, preferred_element_type=F32))
    bidx = lax.broadcasted_iota(I32, (nbp, blk), 0)
    rhs = []
    for h in range(2):
        gate = gates[h * nbp:(h + 1) * nbp]
        gsc = jnp.where(bidx < qi, gate, -jnp.inf)
        bias = jnp.full((nbp, blk), NEG_BIG, F32)
        for _ in range(topk):
            mx = jnp.max(gsc, axis=0, keepdims=True)
            idx = jnp.min(jnp.where(gsc == mx, bidx, nbp), axis=0, keepdims=True)
            pick = jnp.logical_and(bidx == idx, mx > -jnp.inf)
            bias = jnp.where(pick, 0.0, bias)
            gsc = jnp.where(pick, -jnp.inf, gsc)
        pad = jnp.zeros((2 * HEAD_DIM - nbp - 16, blk), BF16)
        rhs.append(jnp.concatenate([qpad[h], bias.astype(BF16), srow_ref[h], pad], axis=0))

    def scores(j, a):
        k0 = pl.multiple_of(j * blk, blk)
        lhs = jnp.concatenate([k_ref[0, pl.ds(k0, blk), :], aug_ref[a]], axis=1)
        return [jnp.dot(lhs, rhs[h], preferred_element_type=F32) for h in range(2)]

    def values(j):
        k0 = pl.multiple_of(j * blk, blk)
        return [vT_ref[0, h * HEAD_DIM:(h + 1) * HEAD_DIM, pl.ds(k0, blk)] for h in range(2)]

    def offsets(j):
        dq = ((qi - j) * blk).astype(F32)
        return [sl_ref[h] * dq for h in range(2)]

    ik = lax.broadcasted_iota(I32, (blk, blk), 0)
    iq = lax.broadcasted_iota(I32, (blk, blk), 1)
    s_own = [jnp.where(ik <= iq, s, NEG_BIG) for s in scores(qi, nb)]
    m0 = jnp.full((1, blk), NEG_BIG, F32)

    def first():
        return [_softmax_stage(s_own[h], jnp.zeros((1, blk), F32), m0) for h in range(2)]

    outs = _pipelined_tiles(lambda j: scores(jnp.minimum(j, nb - 1), jnp.minimum(j, nb - 1)),
                            values, offsets, s_ref, p_ref, first, 2, qi, qi)
    o_ref[0] = jnp.concatenate(outs, axis=0).T.astype(BF16)


def _moba(mqT, mk, mvT, aug, srow, sl):
    B, _, S = mqT.shape
    blk = MOBA_BLOCK
    nb = S // blk
    topk = min(MOBA_TOPK, nb)
    nbp = -(-nb // 16) * 16
    grid = (B, MOBA_HEADS // 2, nb)
    return pl.pallas_call(
        functools.partial(_moba_kernel, nb=nb, nbp=nbp, topk=topk), grid=grid,
        in_specs=[
            pl.BlockSpec((1, 128, blk), lambda b, p, i: (b, p, i)),
            pl.BlockSpec((1, S, 128), lambda b, p, i: (b, 0, p)),
            pl.BlockSpec((1, 128, S), lambda b, p, i: (b, p, 0)),
            pl.BlockSpec(aug.shape, lambda b, p, i: (0, 0, 0)),
            pl.BlockSpec((2, 16, blk), lambda b, p, i: (p, 0, 0)),
            pl.BlockSpec((2, 1, blk), lambda b, p, i: (p, 0, 0)),
        ],
        out_specs=pl.BlockSpec((1, blk, 128), lambda b, p, i: (b, i, p)),
        out_shape=jax.ShapeDtypeStruct((B, S, 512), BF16),
        scratch_shapes=[
            pltpu.VMEM((nbp, 128), F32),
            pltpu.VMEM((3, 2 * nbp, 128), BF16),
            pltpu.VMEM((2, 2, blk, blk), F32),
            pltpu.VMEM((2, 2, blk, blk), BF16),
        ],
        compiler_params=_params(3), name="moba",
    )(mqT, mk, mvT, aug, srow, sl)


def _compress_kernel(kc_ref, vc_ref, wk_ref, wvT_ref, pek_ref, pev_ref, w2k_ref, w2vT_ref,
                     kcmp_ref, vcmpT_ref):
    nc = kc_ref.shape[1]

    wk = wk_ref[...]
    ab = jnp.dot(kc_ref[0], wk, preferred_element_type=F32)
    pt = (jnp.dot(pek_ref[0], wk[:, 0:128].astype(F32), preferred_element_type=F32)
          + jnp.dot(pek_ref[1], wk[:, 128:256].astype(F32), preferred_element_type=F32))
    pre = ab[:, 0:128] + pltpu.roll(ab[:, 128:256], nc - 1, 0) + pt[0:1]
    hid = jax.nn.gelu(pre)
    kcmp_ref[0] = jnp.dot(hid.astype(BF16), w2k_ref[...], preferred_element_type=F32).astype(BF16)

    wvT = wvT_ref[...]
    abT = lax.dot_general(wvT, vc_ref[0], NT_DIMS, preferred_element_type=F32)
    ptT = (lax.dot_general(wvT[0:128].astype(F32), pev_ref[0], NT_DIMS, preferred_element_type=F32)
           + lax.dot_general(wvT[128:256].astype(F32), pev_ref[1], NT_DIMS, preferred_element_type=F32))
    preT = abT[0:128] + pltpu.roll(abT[128:256], nc - 1, 1) + ptT[:, 0:1]
    hidT = jax.nn.gelu(preT)
    vcmpT_ref[0] = jnp.dot(w2vT_ref[...], hidT.astype(BF16), preferred_element_type=F32).astype(BF16)


def _compress(kc2, vc2, wk, wvT, pek, pev, w2k, w2vT):
    B, nc, _ = kc2.shape
    full = lambda a: pl.BlockSpec(a.shape, lambda b: (0,) * a.ndim)
    blk = pl.BlockSpec((1, nc, kc2.shape[2]), lambda b: (b, 0, 0))
    return pl.pallas_call(
        _compress_kernel, grid=(B,),
        in_specs=[blk, blk, full(wk), full(wvT), full(pek), full(pev), full(w2k), full(w2vT)],
        out_specs=[pl.BlockSpec((1, nc, 128), lambda b: (b, 0, 0)),
                   pl.BlockSpec((1, 128, nc), lambda b: (b, 0, 0))],
        out_shape=[jax.ShapeDtypeStruct((B, nc, 128), BF16),
                   jax.ShapeDtypeStruct((B, 128, nc), BF16)],
        compiler_params=_params(1), name="nsa_compress",
    )(kc2, vc2, wk, wvT, pek, pev, w2k, w2vT)


def _nsa_kernel(qT_ref, kcmp_ref, vcmpT_ref, ks_ref, vsT_ref, kw_ref, vwT_ref,
                g_ref, sl_ref, srow_ref, auga_ref, augb_ref, augc_ref, tblc_ref, bw_ref, o_ref,
                s_ref, p_ref, pc_ref, flagv_ref, flags_ref, list_ref, fsem, *, n_slc, topn):
    g = pl.program_id(1)
    qi = pl.program_id(2)
    tq = NSA_TQ
    hg = NSA_GROUP
    wl = hg * tq
    q0 = qi * tq

    q4 = qT_ref[0]
    qT = jnp.concatenate([q4[h * HEAD_DIM:(h + 1) * HEAD_DIM] for h in range(hg)], axis=1)
    qT2 = jnp.concatenate([qT, qT], axis=0)
    rowi = lax.broadcasted_iota(I32, qT2.shape, 0)
    qpad = jnp.where((rowi >> 6) == g, qT2, jnp.zeros_like(qT2))
    slope = sl_ref[0]
    lane = lax.broadcasted_iota(I32, (1, wl), 1)
    t_q = q0 + (lane & (tq - 1))

    nc = kcmp_ref.shape[1]
    rhs_top = jnp.where((rowi >> 6) == g, qT2, srow_ref[0])
    mine_c = (lax.broadcasted_iota(I32, (nc, LANES), 1) >> 6) == g
    lhs_c = jnp.where(mine_c, kcmp_ref[0], augc_ref[0])
    first_c = pl.multiple_of(nc - qi * (tq // NSA_CMP_STRIDE), 8)
    z = jnp.dot(lhs_c, rhs_top, preferred_element_type=F32) + tblc_ref[pl.ds(first_c, nc), :]
    mx = jnp.max(z, axis=0, keepdims=True)
    e = jnp.exp2(z - mx)
    den = jnp.maximum(jnp.sum(e, axis=0, keepdims=True), 1e-30)
    p = e * jnp.where(t_q >= NSA_CMP_LEN - 1, 1.0 / den, 0.0)
    o_c = jnp.dot(vcmpT_ref[0], p.astype(BF16), preferred_element_type=F32)

    pc = p[:, 0:tq]
    for h in range(1, hg):
        pc = pc + p[:, h * tq:(h + 1) * tq]
    n_lc = tq // LANES
    for c in range(n_lc):
        pc_ref[c] = pc[:, c * LANES:(c + 1) * LANES]
    su = NSA_SLC_BLOCK // NSA_CMP_STRIDE
    x = [jnp.concatenate([pc_ref[c, pl.ds(k, n_slc, stride=su), :] for c in range(n_lc)], axis=1)
         for k in range(su)]
    jb = lax.broadcasted_iota(I32, (n_slc, tq), 0)
    prev = jnp.where(jb == 0, 0.0, pltpu.roll(x[3], 1, 0))
    imp = 2.0 * (x[0] + x[1] + x[2]) + x[3] + prev
    cur = (q0 + lax.broadcasted_iota(I32, (1, tq), 1)) >> 6
    allowed = jb <= cur
    forced = jnp.logical_or(jb == 0, jnp.logical_or(jb == cur, jb == cur - 1))
    bias = jnp.where(jnp.logical_and(allowed, forced), 0.0, NEG_BIG)
    sc = jnp.where(jnp.logical_and(allowed, jnp.logical_not(forced)), imp, -1.0)
    for _ in range(topn - 3):
        smx = jnp.max(sc, axis=0, keepdims=True)
        idx = jnp.min(jnp.where(sc == smx, jb, n_slc), axis=0, keepdims=True)
        pick = jnp.logical_and(jb == idx, smx >= 0.0)
        bias = jnp.where(pick, 0.0, bias)
        sc = jnp.where(pick, -1.0, sc)

    n_tiles = augb_ref.shape[0]
    jd = lax.div(q0, SLC_TILE)
    chosen = jnp.where(bias == 0.0, 1.0, 0.0).astype(BF16)
    per_blk = lax.dot_general(jnp.ones((SUBLANES, tq), BF16), chosen, NT_DIMS,
                              preferred_element_type=F32)
    per_shift = (SLC_TILE // NSA_SLC_BLOCK).bit_length() - 1
    in_tile = (lax.broadcasted_iota(I32, (n_slc, LANES), 0) >> per_shift) == lax.broadcasted_iota(
        I32, (n_slc, LANES), 1)
    per_tile = jnp.dot(per_blk.astype(BF16), jnp.where(in_tile, 1.0, 0.0).astype(BF16),
                       preferred_element_type=F32)
    is_past = lax.broadcasted_iota(I32, (SUBLANES, LANES), 1) < jd
    flagv_ref[...] = jnp.where(jnp.logical_and(per_tile > 0.0, is_past), 1, 0).astype(I32)
    flag_copy = pltpu.make_async_copy(flagv_ref, flags_ref, fsem)
    flag_copy.start()

    start = pl.multiple_of(jnp.maximum(q0 - NSA_WINDOW, 0), tq)
    first_w = pl.multiple_of(NSA_WINDOW - (q0 - start), tq)
    kt = kw_ref[0, pl.ds(start, WIN_KEYS), :]
    z = jnp.dot(kt, qpad, preferred_element_type=F32) + bw_ref[0, pl.ds(first_w, WIN_KEYS), :]
    mx = jnp.max(z, axis=0, keepdims=True)
    p = jnp.exp2(z - mx)
    den = jnp.maximum(jnp.sum(p, axis=0, keepdims=True), 1e-30)
    o_w = jnp.dot(vwT_ref[0, :, pl.ds(start, WIN_KEYS)], p.astype(BF16),
                  preferred_element_type=F32) / den

    flag_copy.wait()
    for i in range(TILE_LIST_LEN):
        list_ref[i] = 0

    def add_tile(t, n):
        list_ref[n] = t
        return n + flags_ref[0, t]
    n_used = lax.fori_loop(0, n_tiles, add_tile, 0)
    list_ref[DIAG_SLOT] = jd

    wc = NSA_CHAIN_LANES
    n_ch = wl // wc

    def lane_split(a):
        return [a[:, c * wc:(c + 1) * wc] for c in range(n_ch)]

    if n_slc < LANES:
        bias = jnp.concatenate([bias, jnp.zeros((LANES - n_slc, tq), F32)], axis=0)
    bias4 = jnp.concatenate([bias.astype(BF16)] * hg, axis=1)
    rhs = jnp.concatenate([rhs_top, bias4], axis=0)
    mine = (lax.broadcasted_iota(I32, (SLC_TILE, LANES), 1) >> 6) == g

    def scores(j, null):
        k0 = pl.multiple_of(j * SLC_TILE, SLC_TILE)
        kt = ks_ref[0, pl.ds(k0, SLC_TILE), :]
        lhs = jnp.concatenate([jnp.where(mine, kt, auga_ref[null]), augb_ref[j]], axis=1)
        return jnp.dot(lhs, rhs, preferred_element_type=F32)

    def values(i):
        k0 = pl.multiple_of(list_ref[i] * SLC_TILE, SLC_TILE)
        return [vsT_ref[0, :, pl.ds(k0, SLC_TILE)]] * n_ch

    def offsets(i):
        return lane_split(slope * (q0 - list_ref[i] * SLC_TILE).astype(F32))

    def past_scores(i):
        return lane_split(scores(list_ref[jnp.minimum(i, n_tiles - 1)], (i >= n_used).astype(I32)))

    t_k = jd * SLC_TILE + lax.broadcasted_iota(I32, (SLC_TILE, wl), 0)
    s_diag = lane_split(jnp.where(t_k <= t_q, scores(jd, 0), NEG_BIG))
    c_diag = offsets(DIAG_SLOT)

    def first():
        return [_softmax_stage(s_diag[c], c_diag[c], jnp.full((1, wc), NEG_BIG, F32))
                for c in range(n_ch)]

    o_s = jnp.concatenate(
        _pipelined_tiles(past_scores, values, offsets, s_ref, p_ref, first, n_ch, n_used, DIAG_SLOT),
        axis=1)

    gt = g_ref[0]

    def gate_row(br):
        return jnp.concatenate([gt[br * hg + h:br * hg + h + 1] for h in range(hg)], axis=1)

    o = gate_row(0) * o_c + gate_row(1) * o_s + gate_row(2) * o_w
    o4 = jnp.concatenate([o[:, h * tq:(h + 1) * tq] for h in range(hg)], axis=0)
    o_ref[0] = o4.T.astype(BF16)


def _nsa(nqT, kcmp, vcmpT, ks, vsT, kw, vwT, gT, sl, srow, auga, augb, augc, tblc, bw):
    B, _, S = nqT.shape
    tq = NSA_TQ
    nc = kcmp.shape[1]
    n_slc = S // NSA_SLC_BLOCK
    topn = min(NSA_SLC_TOPN, n_slc)
    wl = NSA_GROUP * tq
    grid = (B, NSA_KV_HEADS, S // tq)
    return pl.pallas_call(
        functools.partial(_nsa_kernel, n_slc=n_slc, topn=topn), grid=grid,
        in_specs=[
            pl.BlockSpec((1, NSA_GROUP * HEAD_DIM, tq), lambda b, g, i: (b, g, i)),
            pl.BlockSpec((1, nc, 128), lambda b, g, i: (b, 0, 0)),
            pl.BlockSpec((1, HEAD_DIM, nc), lambda b, g, i: (b, g, 0)),
            pl.BlockSpec((1, S, 128), lambda b, g, i: (b, 0, 0)),
            pl.BlockSpec((1, HEAD_DIM, S), lambda b, g, i: (b, g, 0)),
            pl.BlockSpec((1, S, 128), lambda b, g, i: (b, 0, 0)),
            pl.BlockSpec((1, HEAD_DIM, S), lambda b, g, i: (b, g, 0)),
            pl.BlockSpec((1, 16, tq), lambda b, g, i: (b, g, i)),
            pl.BlockSpec((1, 1, wl), lambda b, g, i: (g, 0, 0)),
            pl.BlockSpec((1, 2 * HEAD_DIM, wl), lambda b, g, i: (g, 0, 0)),
            pl.BlockSpec((2, SLC_TILE, LANES), lambda b, g, i: (g, 0, 0)),
            pl.BlockSpec(augb.shape, lambda b, g, i: (0, 0, 0)),
            pl.BlockSpec((1, nc, LANES), lambda b, g, i: (g, 0, 0)),
            pl.BlockSpec(tblc.shape, lambda b, g, i: (0, 0)),
            pl.BlockSpec((1,) + bw.shape[1:], lambda b, g, i: (g, 0, 0)),
        ],
        out_specs=pl.BlockSpec((1, tq, NSA_GROUP * HEAD_DIM), lambda b, g, i: (b, i, g)),
        out_shape=jax.ShapeDtypeStruct((B, S, 512), BF16),
        scratch_shapes=[
            pltpu.VMEM((2, wl // NSA_CHAIN_LANES, SLC_TILE, NSA_CHAIN_LANES), F32),
            pltpu.VMEM((2, wl // NSA_CHAIN_LANES, SLC_TILE, NSA_CHAIN_LANES), BF16),
            pltpu.VMEM((tq // LANES, nc, LANES), F32),
            pltpu.VMEM((SUBLANES, LANES), I32),
            pltpu.SMEM((SUBLANES, LANES), I32),
            pltpu.SMEM((TILE_LIST_LEN,), I32),
            pltpu.SemaphoreType.DMA,
        ],
        compiler_params=_params(3), name="nsa",
    )(nqT, kcmp, vcmpT, ks, vsT, kw, vwT, gT, sl, srow, auga, augb, augc, tblc, bw)


def _outproj_kernel(om_ref, on_ref, x_ref, wo_ref, g_ref, wr_ref, br_ref,
                    x1_ref, hn_ref, e_ref, w_ref):
    attn = (jnp.dot(om_ref[...], wo_ref[0:512, :], preferred_element_type=F32)
            + jnp.dot(on_ref[...], wo_ref[512:1024, :], preferred_element_type=F32))
    x1 = x_ref[...] + attn
    x1_ref[...] = x1
    hn = _rmsnorm(x1, g_ref[...])
    _store_token_tiles(hn_ref, hn)
    logits = jnp.dot(hn, wr_ref[...], precision=lax.Precision.HIGHEST,
                     preferred_element_type=F32) + br_ref[...]
    tm = logits.shape[0]
    lane = lax.broadcasted_iota(I32, (tm, LANES), 1)
    sc = jnp.where(lane < N_EXPERTS, logits, -jnp.inf)
    e_out = jnp.zeros((tm, LANES), I32)
    vals = []
    for k in range(TOP_K):
        mx = jnp.max(sc, axis=1, keepdims=True)
        idx = jnp.min(jnp.where(sc == mx, lane, LANES), axis=1, keepdims=True)
        e_out = jnp.where(lane == k, idx, e_out)
        sc = jnp.where(lane == idx, -jnp.inf, sc)
        vals.append(mx)
    ex = [jnp.exp(v - vals[0]) for v in vals]
    den = ex[0] + ex[1] + ex[2] + ex[3]
    w_out = jnp.zeros((tm, LANES), F32)
    for k in range(TOP_K):
        w_out = jnp.where(lane == k, ex[k] / den, w_out)
    e_ref[...] = e_out
    w_ref[...] = w_out


def _outproj(om, on, x, wo, g, wr, br, tm):
    N, D = x.shape
    full = lambda a: pl.BlockSpec(a.shape, lambda i: (0,) * a.ndim)
    row = lambda w: pl.BlockSpec((tm, w), lambda i: (i, 0))
    return pl.pallas_call(
        _outproj_kernel, grid=(N // tm,),
        in_specs=[row(512), row(512), row(D), full(wo), full(g), full(wr), full(br)],
        out_specs=[row(D), pl.BlockSpec((tm * SUBLANES, LANES), lambda i: (i, 0)), row(LANES), row(LANES)],
        out_shape=[jax.ShapeDtypeStruct((N, D), F32), jax.ShapeDtypeStruct((N * SUBLANES, LANES), F32),
                   jax.ShapeDtypeStruct((N, LANES), I32), jax.ShapeDtypeStruct((N, LANES), F32)],
        compiler_params=_params(1), name="outproj_router",
    )(om, on, x, wo, g, wr, br)


def _rank_kernel(e_ref, rank_ref, cnt_ref, base_ref):
    i = pl.program_id(0)
    T = e_ref.shape[0]

    @pl.when(i == 0)
    def _():
        base_ref[...] = jnp.zeros(base_ref.shape, F32)

    e = e_ref[...]
    lane = lax.broadcasted_iota(I32, (T, LANES), 1)
    tril = jnp.where(lax.broadcasted_iota(I32, (T, T), 0) >= lax.broadcasted_iota(I32, (T, T), 1),
                     1.0, 0.0).astype(BF16)
    out = jnp.zeros((T, LANES), I32)
    for k in range(TOP_K):
        hit = lane == e[:, k:k + 1]
        oh = jnp.where(hit, 1.0, 0.0)
        cum = jnp.dot(tril, oh.astype(BF16), preferred_element_type=F32)
        base = base_ref[0:1, :]
        r = jnp.sum(jnp.where(hit, cum - 1.0 + base, 0.0), axis=1, keepdims=True)
        out = jnp.where(lane == k, r.astype(I32), out)
        base_ref[...] = base_ref[...] + jnp.sum(oh, axis=0, keepdims=True)
    rank_ref[...] = out
    cnt_ref[...] = base_ref[...]


def _ranks(e128):
    N = e128.shape[0]
    T = RANK_TILE
    return pl.pallas_call(
        _rank_kernel, grid=(N // T,),
        in_specs=[pl.BlockSpec((T, LANES), lambda i: (i, 0))],
        out_specs=[pl.BlockSpec((T, LANES), lambda i: (i, 0)),
                   pl.BlockSpec((8, LANES), lambda i: (0, 0))],
        out_shape=[jax.ShapeDtypeStruct((N, LANES), I32),
                   jax.ShapeDtypeStruct((8, LANES), F32)],
        scratch_shapes=[pltpu.VMEM((8, LANES), F32)],
        compiler_params=_params(1), name="route_ranks",
    )(e128)


def _row_copy(src, dst, i_src, i_dst, sem):
    return pltpu.make_async_copy(src.at[pl.ds(pl.multiple_of(i_src * SUBLANES, SUBLANES), SUBLANES)],
                                 dst.at[pl.ds(pl.multiple_of(i_dst * SUBLANES, SUBLANES), SUBLANES)], sem)


def _store_token_tiles(ref, x):
    rows = x.shape[0]
    for c in range(SUBLANES):
        ref[pl.ds(c, rows, stride=SUBLANES), :] = x[:, c * LANES:(c + 1) * LANES]


def _load_token_tiles(ref, rows):
    return jnp.concatenate([ref[pl.ds(c, rows, stride=SUBLANES), :] for c in range(SUBLANES)], axis=1)


def _dispatch_kernel(dest_hbm, hp_ref, xz_hbm, out_hbm, idx_ref, isem, sem):
    del xz_hbm
    i = pl.program_id(0)
    T = ROUTE_TILE
    cp = pltpu.make_async_copy(dest_hbm.at[i], idx_ref, isem)
    cp.start()
    cp.wait()

    def issue(t, carry):
        for k in range(TOP_K):
            _row_copy(hp_ref, out_hbm, t, idx_ref[t * TOP_K + k], sem).start(priority=k % 2)
        return carry
    lax.fori_loop(0, T, issue, 0, unroll=ROW_DMA_UNROLL)

    def drain(t, carry):
        for k in range(TOP_K):
            _row_copy(hp_ref, out_hbm, 0, 0, sem).wait()
        return carry
    lax.fori_loop(0, T, drain, 0, unroll=ROW_DMA_UNROLL)


def _dispatch(dest2, hp, xzero):
    nsteps = dest2.shape[0]
    T = ROUTE_TILE
    return pl.pallas_call(
        _dispatch_kernel, grid=(nsteps,),
        in_specs=[pl.BlockSpec(memory_space=pl.ANY),
                  pl.BlockSpec((T * SUBLANES, LANES), lambda i: (i, 0)),
                  pl.BlockSpec(memory_space=pl.ANY)],
        out_specs=pl.BlockSpec(memory_space=pl.ANY),
        out_shape=jax.ShapeDtypeStruct(xzero.shape, xzero.dtype),
        scratch_shapes=[pltpu.SMEM((T * TOP_K,), I32),
                        pltpu.SemaphoreType.DMA, pltpu.SemaphoreType.DMA],
        input_output_aliases={2: 0},
        compiler_params=_params(1), name="moe_dispatch",
    )(dest2, hp, xzero)


W_PREP_COLS = 256


def _expert_kernel(be_ref, na_ref, x_ref, wup_ref, bg_ref, bu_ref, wd_ref, bd_ref, y_ref,
                   wg_ref, wu_ref, wt_ref):
    b = pl.program_id(0)
    active = b < na_ref[0]
    new_expert = jnp.logical_or(b == 0, be_ref[b] != be_ref[jnp.maximum(b - 1, 0)])

    @pl.when(jnp.logical_and(active, new_expert))
    def _():
        half = W_PREP_COLS // 2
        n_lc = wt_ref.shape[0]

        def every_other(first):
            return jnp.concatenate(
                [wt_ref[c, pl.ds(first, half, stride=2), :] for c in range(n_lc)], axis=1).astype(BF16)

        for ch in range(wup_ref.shape[2] // W_PREP_COLS):
            panel = wup_ref[0, :, ch * W_PREP_COLS:(ch + 1) * W_PREP_COLS].T
            for c in range(n_lc):
                wt_ref[c] = panel[:, c * LANES:(c + 1) * LANES]
            wg_ref[ch * half:(ch + 1) * half, :] = every_other(0)
            wu_ref[ch * half:(ch + 1) * half, :] = every_other(1)

    @pl.when(active)
    def _():
        xb = _load_token_tiles(x_ref, MOE_ROWS).astype(BF16)
        gg = lax.dot_general(xb, wg_ref[...], NT_DIMS, preferred_element_type=F32) + bg_ref[0]
        uu = lax.dot_general(xb, wu_ref[...], NT_DIMS, preferred_element_type=F32) + bu_ref[0]
        gg = jnp.minimum(gg, SWIGLU_LIMIT)
        uu = jnp.clip(uu, -SWIGLU_LIMIT, SWIGLU_LIMIT)
        a = gg * jax.nn.sigmoid(SWIGLU_ALPHA * gg) * (uu + 1.0)
        _store_token_tiles(y_ref, jnp.dot(a.astype(BF16), wd_ref[0], preferred_element_type=F32) + bd_ref[0])

    @pl.when(jnp.logical_not(active))
    def _():
        y_ref[...] = jnp.zeros(y_ref.shape, F32)


def _experts(blk_e, n_act, xrows, w_up, bg, bu, wd, bd):
    _, D, F2 = w_up.shape
    F = F2 // 2
    assert D == SUBLANES * LANES
    P = xrows.shape[0] // SUBLANES
    n_blk = P // MOE_ROWS
    rows_spec = pl.BlockSpec((MOE_ROWS * SUBLANES, LANES), lambda b, be, na: (b, 0))
    wspec = lambda r, c: pl.BlockSpec((1, r, c), lambda b, be, na: (be[b], 0, 0))
    grid_spec = pltpu.PrefetchScalarGridSpec(
        num_scalar_prefetch=2, grid=(n_blk,),
        in_specs=[rows_spec, wspec(D, F2), wspec(1, F), wspec(1, F), wspec(F, D), wspec(1, D)],
        out_specs=rows_spec,
        scratch_shapes=[pltpu.VMEM((F, D), BF16), pltpu.VMEM((F, D), BF16),
                        pltpu.VMEM((D // LANES, W_PREP_COLS, LANES), F32)],
    )
    return pl.pallas_call(
        _expert_kernel, grid_spec=grid_spec,
        out_shape=jax.ShapeDtypeStruct(xrows.shape, F32),
        compiler_params=_params(1), name="moe_experts",
    )(blk_e, n_act, xrows, w_up, bg, bu, wd, bd)


def _combine_kernel(dest_hbm, x1_ref, w_ref, g_ref, y_hbm, o_ref, idx_ref, buf_ref, isem, sem):
    i = pl.program_id(0)
    T = ROUTE_TILE
    cp = pltpu.make_async_copy(dest_hbm.at[i], idx_ref, isem)
    cp.start()
    cp.wait()

    def issue(t, carry):
        for k in range(TOP_K):
            _row_copy(y_hbm, buf_ref.at[k], idx_ref[t * TOP_K + k], t, sem).start(priority=k % 2)
        return carry
    lax.fori_loop(0, T, issue, 0, unroll=ROW_DMA_UNROLL)

    def drain(t, carry):
        for k in range(TOP_K):
            _row_copy(y_hbm, buf_ref.at[k], 0, 0, sem).wait()
        return carry
    lax.fori_loop(0, T, drain, 0, unroll=ROW_DMA_UNROLL)

    x1 = x1_ref[...]
    w = w_ref[...]
    cols = []
    for c in range(SUBLANES):
        acc = x1[:, c * LANES:(c + 1) * LANES]
        for k in range(TOP_K):
            acc = acc + w[:, k:k + 1] * buf_ref[k, pl.ds(c, T, stride=SUBLANES), :]
        cols.append(acc)
    o_ref[...] = _rmsnorm(jnp.concatenate(cols, axis=1), g_ref[...])


def _combine(dest2, x1, w128, g, yrows):
    N, D = x1.shape
    T = ROUTE_TILE
    return pl.pallas_call(
        _combine_kernel, grid=(N // T,),
        in_specs=[pl.BlockSpec(memory_space=pl.ANY),
                  pl.BlockSpec((T, D), lambda i: (i, 0)),
                  pl.BlockSpec((T, LANES), lambda i: (i, 0)),
                  pl.BlockSpec(g.shape, lambda i: (0, 0)),
                  pl.BlockSpec(memory_space=pl.ANY)],
        out_specs=pl.BlockSpec((T, D), lambda i: (i, 0)),
        out_shape=jax.ShapeDtypeStruct((N, D), F32),
        scratch_shapes=[pltpu.SMEM((T * TOP_K,), I32),
                        pltpu.VMEM((TOP_K, T * SUBLANES, LANES), F32),
                        pltpu.SemaphoreType.DMA, pltpu.SemaphoreType.DMA],
        compiler_params=_params(1), name="moe_combine",
    )(dest2, x1, w128, g, yrows)


def _alibi_slopes():
    n = MOBA_HEADS + NSA_HEADS
    s = jnp.exp2(-8.0 * jnp.arange(1, n + 1, dtype=F32) / n)
    return s[0::2], s[1::2]


def _prep_inproj(w_in):
    hd = HEAD_DIM
    sizes = [MOBA_HEADS * hd] * 3 + [NSA_HEADS * hd] + [NSA_KV_HEADS * hd] * 6 + [NSA_BRANCHES * NSA_HEADS]
    cuts = np.cumsum([0] + sizes)
    mq, mk, mv, nq, kc, vc, ks, vs, kw, vw, ng = [w_in[:, cuts[i]:cuts[i + 1]] for i in range(11)]
    qscale = (hd ** -0.5) * LOG2E
    wr = jnp.concatenate([mk, kc, vc, ks, kw], axis=1).astype(BF16)
    wt = jnp.concatenate([mq * qscale, mv, nq * qscale, vs, vw], axis=1).T.astype(BF16)
    ngr = ng.reshape(-1, NSA_KV_HEADS, NSA_GROUP, NSA_BRANCHES).transpose(1, 3, 2, 0)
    ngr = ngr.reshape(NSA_KV_HEADS, NSA_BRANCHES * NSA_GROUP, -1)
    wg = jnp.pad(ngr, ((0, 0), (0, 16 - NSA_BRANCHES * NSA_GROUP), (0, 0))).reshape(32, -1)
    return wr, wt, wg.astype(F32)


def _prep_compress(w1, w2, pe):
    hd, half = HEAD_DIM, NSA_CMP_STRIDE
    w1r = w1.reshape(2, half, hd, hd)
    eye = jnp.eye(NSA_KV_HEADS, dtype=w1.dtype)
    w = jnp.einsum('alde,gh->lgdahe', w1r, eye).reshape(half * NSA_KV_HEADS * hd, 2 * NSA_KV_HEADS * hd)
    w2b = jnp.einsum('de,gh->gdhe', w2, eye).reshape(NSA_KV_HEADS * hd, NSA_KV_HEADS * hd)
    per = pe.reshape(2, half, 1, hd)
    pe2 = jnp.broadcast_to(per, (2, half, NSA_KV_HEADS, hd)).reshape(2, 1, half * NSA_KV_HEADS * hd)
    pe2 = jnp.broadcast_to(pe2, (2, 8, pe2.shape[2]))
    return w.astype(BF16), w2b.astype(BF16), pe2.astype(F32)


def _attention_tables(S):
    moba_sl, nsa_sl = _alibi_slopes()
    moba_sl = moba_sl * LOG2E
    nsa_sl = nsa_sl * LOG2E
    blk = MOBA_BLOCK
    moba_row = jnp.broadcast_to(moba_sl[:, None, None], (MOBA_HEADS, 1, blk))
    nb = S // blk
    nbp = -(-nb // 16) * 16
    col = jnp.arange(LANES)[None, None, :]
    tile = jnp.arange(nb + 1)[:, None, None]
    off = jnp.arange(blk, dtype=F32)[None, :, None]
    moba_aug = jnp.where(jnp.logical_and(col == tile, tile < nb), 1.0,
                         jnp.where(jnp.logical_and(col >= nbp, col < nbp + 3), off, 0.0)).astype(BF16)
    parts = jnp.stack(list(_split3(moba_sl)) + [jnp.zeros_like(moba_sl)] * 13, axis=1)
    moba_srow = jnp.broadcast_to(parts[:, :, None], (MOBA_HEADS, 16, blk)).astype(BF16)

    wl = NSA_GROUP * NSA_TQ
    n_slc = S // NSA_SLC_BLOCK
    assert n_slc <= LANES, "block-choice rows must fit the spare contraction rows"
    nsa_row = jnp.repeat(nsa_sl.reshape(NSA_KV_HEADS, NSA_GROUP), NSA_TQ, axis=1)
    hi, mid, lo = [t[:, None, :] for t in _split3(nsa_row)]
    base = ((1 - jnp.arange(NSA_KV_HEADS)) * HEAD_DIM)[:, None, None]
    rows = jnp.arange(2 * HEAD_DIM)[None, :, None]
    nsa_srow = jnp.where(rows == base, hi, jnp.where(rows == base + 1, mid, jnp.where(
        rows == base + 2, lo, jnp.where(rows == base + 3, NEG_BIG, 0.0)))).astype(BF16)
    lane = jnp.arange(LANES)[None, None, None, :]
    base4 = base[:, None]
    null = jnp.arange(2, dtype=F32)[None, :, None, None]
    koff = jnp.arange(SLC_TILE, dtype=F32)[None, None, :, None]
    nsa_auga = jnp.where(jnp.logical_and(lane >= base4, lane < base4 + 3), koff,
                         jnp.where(lane == base4 + 3, null, 0.0))
    nsa_auga = nsa_auga.reshape(2 * NSA_KV_HEADS, SLC_TILE, LANES).astype(BF16)
    per = SLC_TILE // NSA_SLC_BLOCK
    tile = jnp.arange(S // SLC_TILE)[:, None, None]
    blk_of = tile * per + jnp.arange(SLC_TILE)[None, :, None] // NSA_SLC_BLOCK
    nsa_augb = (jnp.arange(LANES)[None, None, :] == blk_of).astype(BF16)

    nc = S // NSA_CMP_STRIDE
    ci = jnp.arange(nc)[None, :, None]
    lane3 = jnp.arange(LANES)[None, None, :]
    nsa_augc = jnp.where(jnp.logical_and(lane3 >= base + 4, lane3 < base + 7), (ci >> 1).astype(F32),
                         jnp.where(jnp.logical_and(lane3 >= base + 7, lane3 < base + 10),
                                   (ci & 1).astype(F32), 0.0)).astype(BF16)
    step2 = [t[:, None, :] for t in _split3(nsa_row * (2.0 * NSA_CMP_STRIDE))]
    step1 = [t[:, None, :] for t in _split3(nsa_row * (1.0 * NSA_CMP_STRIDE))]
    for k in range(3):
        nsa_srow = jnp.where(rows == base + 4 + k, step2[k].astype(BF16),
                             jnp.where(rows == base + 7 + k, step1[k].astype(BF16), nsa_srow))
    il = jnp.tile(jnp.arange(NSA_TQ), NSA_GROUP)[None, :]
    rel = (jnp.arange(2 * nc) - nc)[:, None]
    nsa_tblc = jnp.where(rel * NSA_CMP_STRIDE + (NSA_CMP_LEN - 1) <= il, 0.0, NEG_BIG).astype(F32)
    dist = (NSA_WINDOW + il - jnp.arange(NSA_WINDOW + WIN_KEYS)[:, None])[None]
    nsa_bw = jnp.where(jnp.logical_and(dist >= 0, dist < NSA_WINDOW),
                       -nsa_row[:, None, :] * dist.astype(F32), NEG_BIG)
    return ((moba_aug, moba_srow, moba_row),
            (nsa_row.reshape(NSA_KV_HEADS, 1, wl), nsa_srow, nsa_auga, nsa_augb, nsa_augc, nsa_tblc, nsa_bw))


def _split3(x):
    hi = x.astype(BF16).astype(F32)
    mid = (x - hi).astype(BF16).astype(F32)
    lo = (x - hi - mid).astype(BF16).astype(F32)
    return hi, mid, lo


def _attention(x, attn_norm_g, w_in, cmp_pe_k, cmp_pe_v, cmp_w1_k, cmp_w2_k, cmp_w1_v, cmp_w2_v):
    B, S, D = x.shape
    wr, wt, wg = _prep_inproj(w_in)
    (mk, kc, vc, ks, kw, mqT, mvT, nqT, vsT, vwT, gT) = _inproj(
        x, attn_norm_g.reshape(1, D), wr, wt, wg, tm=512)
    moba_tabs, nsa_tabs = _attention_tables(S)
    o_moba = _moba(mqT, mk, mvT, *moba_tabs)
    wk, w2k, pek = _prep_compress(cmp_w1_k, cmp_w2_k, cmp_pe_k)
    wv, w2v, pev = _prep_compress(cmp_w1_v, cmp_w2_v, cmp_pe_v)
    nc = S // NSA_CMP_STRIDE
    kcmp, vcmpT = _compress(kc.reshape(B, nc, -1), vc.reshape(B, nc, -1), wk, wv.T, pek, pev, w2k, w2v.T)
    o_nsa = _nsa(nqT, kcmp, vcmpT, ks, vsT, kw, vwT, gT, *nsa_tabs)
    return o_moba, o_nsa


def _moe(x1, hn, e128, w128, w_up, b_up, w_down, b_down, final_norm_g):
    N, D = x1.shape
    rank128, cnt = _ranks(e128)
    counts = cnt[0, :N_EXPERTS].astype(I32)
    padded = (counts + MOE_ROWS - 1) // MOE_ROWS * MOE_ROWS
    pends = jnp.cumsum(padded)
    pstarts = pends - padded
    e4 = e128[:, :TOP_K]
    dest = pstarts[e4] + rank128[:, :TOP_K]
    dest2 = dest.reshape(N // ROUTE_TILE, ROUTE_TILE * TOP_K)
    n_blk = (N * TOP_K + N_EXPERTS * MOE_ROWS + MOE_ROWS - 1) // MOE_ROWS
    P = n_blk * MOE_ROWS
    blk_start = jnp.arange(n_blk, dtype=I32) * MOE_ROWS
    blk_e = jnp.minimum(jnp.sum((pends[None, :] <= blk_start[:, None]).astype(I32), axis=1), N_EXPERTS - 1)
    n_act = (pends[-1:] // MOE_ROWS).astype(I32)
    xrows = _dispatch(dest2, hn, jnp.zeros((P * SUBLANES, LANES), F32))
    bg = b_up[:, None, 0::2]
    bu = b_up[:, None, 1::2]
    yrows = _experts(blk_e, n_act, xrows, w_up, bg, bu, w_down.astype(BF16), b_down[:, None, :])
    return _combine(dest2, x1, w128, final_norm_g.reshape(1, D), yrows)


def kernel(x, attn_norm_g, w_in, cmp_pe_k, cmp_pe_v, cmp_w1_k, cmp_w2_k, cmp_w1_v, cmp_w2_v, w_out, ffn_norm_g, w_router, b_router, w_up, b_up, w_down, b_down, final_norm_g):
    B, S, D = x.shape
    assert attn_norm_g.shape[0] == 1, "single-layer kernel"
    o_moba, o_nsa = _attention(x, attn_norm_g[0], w_in[0], cmp_pe_k[0], cmp_pe_v[0],
                               cmp_w1_k[0], cmp_w2_k[0], cmp_w1_v[0], cmp_w2_v[0])
    N = B * S
    wr = jnp.pad(w_router[0], ((0, 0), (0, LANES - N_EXPERTS)))
    br = jnp.pad(b_router[0], (0, LANES - N_EXPERTS)).reshape(1, LANES)
    x1, hn, e128, w128 = _outproj(o_moba.reshape(N, -1), o_nsa.reshape(N, -1), x.reshape(N, D),
                                  w_out[0].astype(BF16), ffn_norm_g[0].reshape(1, D), wr, br, tm=512)
    out = _moe(x1, hn, e128, w128, w_up[0], b_up[0], w_down[0], b_down[0], final_norm_g)
    return out.reshape(B, S, D)
```

```python
import functools

import jax
import jax.numpy as jnp
import numpy as np
from jax import lax
from jax.experimental import pallas as pl
from jax.experimental.pallas import tpu as pltpu

F32 = jnp.float32
BF16 = jnp.bfloat16
I32 = jnp.int32

HEAD_DIM = 64
MOBA_HEADS = 8
NSA_HEADS = 8
NSA_KV_HEADS = 2
NSA_GROUP = NSA_HEADS // NSA_KV_HEADS
MOBA_BLOCK = 256
MOBA_TOPK = 3
NSA_CMP_LEN = 32
NSA_CMP_STRIDE = 16
NSA_SLC_BLOCK = 64
NSA_SLC_TOPN = 16
NSA_WINDOW = 512
NSA_BRANCHES = 3
N_EXPERTS = 32
TOP_K = 4
SWIGLU_LIMIT = 7.0
SWIGLU_ALPHA = 1.702
RMS_EPS = 1e-5
NEG_BIG = -1e30
LOG2E = 1.4426950408889634

LANES = 128
SUBLANES = 8
VMEM_LIMIT = 56 * 1024 * 1024

MOBA_TILES_PER_TRIP = 2
NSA_TQ = 256
NSA_CHAIN_LANES = 256
TILE_LIST_LEN = 64
DIAG_SLOT = TILE_LIST_LEN - 1
SLC_TILE = 256
WIN_KEYS = NSA_WINDOW + NSA_TQ
MOE_ROWS = 512
ROUTE_TILE = 512
RANK_TILE = 512
ROW_DMA_UNROLL = 8

NT_DIMS = (((1,), (1,)), ((), ()))


def _params(n_grid):
    return pltpu.CompilerParams(
        dimension_semantics=("arbitrary",) * n_grid,
        vmem_limit_bytes=VMEM_LIMIT,
    )


def _rmsnorm(x, g):
    return x * lax.rsqrt(jnp.mean(x * x, axis=-1, keepdims=True) + RMS_EPS) * g


def _inproj_kernel(x_ref, g_ref, wr_ref, wt_ref, wg_ref,
                   mk_ref, kc_ref, vc_ref, ks_ref, kw_ref,
                   mqT_ref, mvT_ref, nqT_ref, vsT_ref, vwT_ref, gT_ref):
    xn = _rmsnorm(x_ref[0], g_ref[...])
    xb = xn.astype(BF16)
    yr = jnp.dot(xb, wr_ref[...], preferred_element_type=F32)
    mk_ref[0] = yr[:, 0:512].astype(BF16)
    kc_ref[0] = yr[:, 512:640].astype(BF16)
    vc_ref[0] = yr[:, 640:768].astype(BF16)
    ks_ref[0] = yr[:, 768:896].astype(BF16)
    kw_ref[0] = yr[:, 896:1024].astype(BF16)
    yt = lax.dot_general(wt_ref[...], xb, NT_DIMS, preferred_element_type=F32)
    mqT_ref[0] = yt[0:512].astype(BF16)
    mvT_ref[0] = yt[512:1024].astype(BF16)
    nqT_ref[0] = yt[1024:1536].astype(BF16)
    vsT_ref[0] = yt[1536:1664].astype(BF16)
    vwT_ref[0] = yt[1664:1792].astype(BF16)
    gl = lax.dot_general(wg_ref[...], xn, NT_DIMS, precision=lax.Precision.HIGHEST,
                         preferred_element_type=F32)
    gT_ref[0] = jax.nn.sigmoid(gl)


def _inproj(x, g, wr, wt, wg, tm):
    B, S, D = x.shape
    grid = (B, S // tm)
    row = lambda w: pl.BlockSpec((1, tm, w), lambda b, i: (b, i, 0))
    col = lambda h: pl.BlockSpec((1, h, tm), lambda b, i: (b, 0, i))
    full = lambda a: pl.BlockSpec(a.shape, lambda b, i: (0,) * a.ndim)
    out_shape = [
        jax.ShapeDtypeStruct((B, S, 512), BF16),
        jax.ShapeDtypeStruct((B, S, 128), BF16),
        jax.ShapeDtypeStruct((B, S, 128), BF16),
        jax.ShapeDtypeStruct((B, S, 128), BF16),
        jax.ShapeDtypeStruct((B, S, 128), BF16),
        jax.ShapeDtypeStruct((B, 512, S), BF16),
        jax.ShapeDtypeStruct((B, 512, S), BF16),
        jax.ShapeDtypeStruct((B, 512, S), BF16),
        jax.ShapeDtypeStruct((B, 128, S), BF16),
        jax.ShapeDtypeStruct((B, 128, S), BF16),
        jax.ShapeDtypeStruct((B, 32, S), F32),
    ]
    out_specs = [row(512), row(128), row(128), row(128), row(128),
                 col(512), col(512), col(512), col(128), col(128), col(32)]
    return pl.pallas_call(
        _inproj_kernel, grid=grid,
        in_specs=[pl.BlockSpec((1, tm, D), lambda b, i: (b, i, 0)),
                  full(g), full(wr), full(wt), full(wg)],
        out_specs=out_specs, out_shape=out_shape,
        compiler_params=_params(2), name="inproj",
    )(x, g, wr, wt, wg)


ONES_ROWS = 16


def _softmax_stage(s, c, m):
    mt = jnp.max(s, axis=0, keepdims=True) - c
    m_new = jnp.maximum(m, mt)
    alpha = jnp.exp2(m - m_new)
    p = jnp.exp2(s - (m_new + c))
    return m_new, p.astype(BF16), alpha


def _pipelined_tiles(scores, values, offsets, s_ref, p_ref, first, n_ch, n_tiles, j_first, per_trip):
    chains = range(n_ch)

    def qk_into(slot, j):
        sc = scores(j)
        for c in chains:
            s_ref[slot, c] = sc[c]

    def pv_from(slot, j, alphas, accs):
        out = []
        for c, vt in zip(chains, values(j)):
            vt1 = jnp.concatenate([vt, jnp.ones((ONES_ROWS, vt.shape[1]), BF16)], axis=0)
            out.append(alphas[c] * accs[c] + jnp.dot(vt1, p_ref[slot, c], preferred_element_type=F32))
        return out

    def softmax_into(slot, j, ms):
        cs = offsets(j)
        new = [_softmax_stage(s_ref[slot, c], cs[c], ms[c]) for c in chains]
        for c in chains:
            p_ref[slot, c] = new[c][1]
        return [n[0] for n in new], [n[2] for n in new]

    qk_into(0, 0)
    first = first()
    for c in chains:
        p_ref[1, c] = first[c][1]

    def trip(i, carry):
        ms, alphas, accs, j_prev = carry
        for u in range(per_trip):
            t = per_trip * i + u
            qk_into(1 - u % 2, t + 1)
            accs = pv_from(1 - u % 2, j_prev, alphas, accs)
            ms, alphas = softmax_into(u % 2, t, ms)
            j_prev = t
        return ms, alphas, accs, j_prev

    n_q = first[0][0].shape[1]
    init = ([f[0] for f in first], [f[2] for f in first],
            [jnp.zeros((HEAD_DIM + ONES_ROWS, n_q), F32)] * n_ch, j_first)
    _, alphas, accs, j_last = lax.fori_loop(0, (n_tiles + per_trip - 1) // per_trip, trip, init)
    accs = pv_from(1, j_last, alphas, accs)
    return [accs[c][:HEAD_DIM] / jnp.maximum(accs[c][HEAD_DIM:HEAD_DIM + 1], 1e-30) for c in chains]


def _moba_kernel(qT_ref, k_ref, vT_ref, aug_ref, srow_ref, sl_ref, o_ref,
                 kmean_ref, kparts_ref, s_ref, p_ref, *, nb, nbp, topk):
    qi = pl.program_id(2)
    blk = MOBA_BLOCK

    @pl.when(qi == 0)
    def _():
        kmean_ref[...] = jnp.zeros(kmean_ref.shape, F32)

        def body(n, carry):
            kb = k_ref[0, pl.ds(pl.multiple_of(n * blk, blk), blk), :].astype(F32)
            kmean_ref[pl.ds(n, 1), :] = jnp.mean(kb, axis=0, keepdims=True)
            return carry
        lax.fori_loop(0, nb, body, 0)
        km = kmean_ref[...]
        head = lax.broadcasted_iota(I32, km.shape, 1) >> 6
        km2 = jnp.concatenate([jnp.where(head == h, km, 0.0) for h in range(2)], axis=0)
        hi = km2.astype(BF16)
        mid = (km2 - hi.astype(F32)).astype(BF16)
        lo = (km2 - hi.astype(F32) - mid.astype(F32)).astype(BF16)
        kparts_ref[0] = hi
        kparts_ref[1] = mid
        kparts_ref[2] = lo

    qT = qT_ref[0]
    row = lax.broadcasted_iota(I32, qT.shape, 0)
    qpad = [jnp.where((row >> 6) == h, qT, jnp.zeros_like(qT)) for h in range(2)]

    gates = (jnp.dot(kparts_ref[0], qT, preferred_element_type=F32)
             + jnp.dot(kparts_ref[1], qT, preferred_element_type=F32)
             + jnp.dot(kparts_ref[2], qT, preferred_element_type=F32))
    bidx = lax.broadcasted_iota(I32, (nbp, blk), 0)
    rhs = []
    for h in range(2):
        gate = gates[h * nbp:(h + 1) * nbp]
        gsc = jnp.where(bidx < qi, gate, -jnp.inf)
        bias = jnp.full((nbp, blk), NEG_BIG, F32)
        for _ in range(topk):
            mx = jnp.max(gsc, axis=0, keepdims=True)
            idx = jnp.min(jnp.where(gsc == mx, bidx, nbp), axis=0, keepdims=True)
            pick = jnp.logical_and(bidx == idx, mx > -jnp.inf)
            bias = jnp.where(pick, 0.0, bias)
            gsc = jnp.where(pick, -jnp.inf, gsc)
        pad = jnp.zeros((2 * HEAD_DIM - nbp - 16, blk), BF16)
        rhs.append(jnp.concatenate([qpad[h], bias.astype(BF16), srow_ref[h], pad], axis=0))

    def scores(j, a):
        k0 = pl.multiple_of(j * blk, blk)
        lhs = jnp.concatenate([k_ref[0, pl.ds(k0, blk), :], aug_ref[a]], axis=1)
        return [jnp.dot(lhs, rhs[h], preferred_element_type=F32) for h in range(2)]

    def values(j):
        k0 = pl.multiple_of(j * blk, blk)
        return [vT_ref[0, h * HEAD_DIM:(h + 1) * HEAD_DIM, pl.ds(k0, blk)] for h in range(2)]

    def offsets(j):
        dq = ((qi - j) * blk).astype(F32)
        return [sl_ref[h] * dq for h in range(2)]

    ik = lax.broadcasted_iota(I32, (blk, blk), 0)
    iq = lax.broadcasted_iota(I32, (blk, blk), 1)
    s_own = [jnp.where(ik <= iq, s, NEG_BIG) for s in scores(qi, nb)]
    m0 = jnp.full((1, blk), NEG_BIG, F32)

    def first():
        return [_softmax_stage(s_own[h], jnp.zeros((1, blk), F32), m0) for h in range(2)]

    outs = _pipelined_tiles(lambda j: scores(jnp.minimum(j, nb - 1), jnp.minimum(j, nb - 1)),
                            values, offsets, s_ref, p_ref, first, 2, qi, qi, MOBA_TILES_PER_TRIP)
    o_ref[0] = jnp.concatenate(outs, axis=0).T.astype(BF16)


def _moba(mqT, mk, mvT, aug, srow, sl):
    B, _, S = mqT.shape
    blk = MOBA_BLOCK
    nb = S // blk
    topk = min(MOBA_TOPK, nb)
    nbp = -(-nb // 16) * 16
    grid = (B, MOBA_HEADS // 2, nb)
    return pl.pallas_call(
        functools.partial(_moba_kernel, nb=nb, nbp=nbp, topk=topk), grid=grid,
        in_specs=[
            pl.BlockSpec((1, 128, blk), lambda b, p, i: (b, p, i)),
            pl.BlockSpec((1, S, 128), lambda b, p, i: (b, 0, p)),
            pl.BlockSpec((1, 128, S), lambda b, p, i: (b, p, 0)),
            pl.BlockSpec(aug.shape, lambda b, p, i: (0, 0, 0)),
            pl.BlockSpec((2, 16, blk), lambda b, p, i: (p, 0, 0)),
            pl.BlockSpec((2, 1, blk), lambda b, p, i: (p, 0, 0)),
        ],
        out_specs=pl.BlockSpec((1, blk, 128), lambda b, p, i: (b, i, p)),
        out_shape=jax.ShapeDtypeStruct((B, S, 512), BF16),
        scratch_shapes=[
            pltpu.VMEM((nbp, 128), F32),
            pltpu.VMEM((3, 2 * nbp, 128), BF16),
            pltpu.VMEM((2, 2, blk, blk), F32),
            pltpu.VMEM((2, 2, blk, blk), BF16),
        ],
        compiler_params=_params(3), name="moba",
    )(mqT, mk, mvT, aug, srow, sl)


def _compress_kernel(kc_ref, vc_ref, wk_ref, wvT_ref, pek_ref, pev_ref, w2k_ref, w2vT_ref,
                     kcmp_ref, vcmpT_ref):
    nc = kc_ref.shape[1]

    wk = wk_ref[...]
    ab = jnp.dot(kc_ref[0], wk, preferred_element_type=F32)
    pt = (jnp.dot(pek_ref[0], wk[:, 0:128].astype(F32), preferred_element_type=F32)
          + jnp.dot(pek_ref[1], wk[:, 128:256].astype(F32), preferred_element_type=F32))
    pre = ab[:, 0:128] + pltpu.roll(ab[:, 128:256], nc - 1, 0) + pt[0:1]
    hid = jax.nn.gelu(pre)
    kcmp_ref[0] = jnp.dot(hid.astype(BF16), w2k_ref[...], preferred_element_type=F32).astype(BF16)

    wvT = wvT_ref[...]
    abT = lax.dot_general(wvT, vc_ref[0], NT_DIMS, preferred_element_type=F32)
    ptT = (lax.dot_general(wvT[0:128].astype(F32), pev_ref[0], NT_DIMS, preferred_element_type=F32)
           + lax.dot_general(wvT[128:256].astype(F32), pev_ref[1], NT_DIMS, preferred_element_type=F32))
    preT = abT[0:128] + pltpu.roll(abT[128:256], nc - 1, 1) + ptT[:, 0:1]
    hidT = jax.nn.gelu(preT)
    vcmpT_ref[0] = jnp.dot(w2vT_ref[...], hidT.astype(BF16), preferred_element_type=F32).astype(BF16)


def _compress(kc2, vc2, wk, wvT, pek, pev, w2k, w2vT):
    B, nc, _ = kc2.shape
    full = lambda a: pl.BlockSpec(a.shape, lambda b: (0,) * a.ndim)
    blk = pl.BlockSpec((1, nc, kc2.shape[2]), lambda b: (b, 0, 0))
    return pl.pallas_call(
        _compress_kernel, grid=(B,),
        in_specs=[blk, blk, full(wk), full(wvT), full(pek), full(pev), full(w2k), full(w2vT)],
        out_specs=[pl.BlockSpec((1, nc, 128), lambda b: (b, 0, 0)),
                   pl.BlockSpec((1, 128, nc), lambda b: (b, 0, 0))],
        out_shape=[jax.ShapeDtypeStruct((B, nc, 128), BF16),
                   jax.ShapeDtypeStruct((B, 128, nc), BF16)],
        compiler_params=_params(1), name="nsa_compress",
    )(kc2, vc2, wk, wvT, pek, pev, w2k, w2vT)


def _nsa_kernel(qT_ref, kcmp_ref, vcmpT_ref, ks_ref, vsT_ref, kw_ref, vwT_ref,
                g_ref, sl_ref, srow_ref, auga_ref, augb_ref, augc_ref, tblc_ref, bw_ref, o_ref,
                s_ref, p_ref, pc_ref, flagv_ref, flags_ref, list_ref, fsem, *, n_slc, topn):
    g = pl.program_id(1)
    qi = pl.program_id(2)
    tq = NSA_TQ
    hg = NSA_GROUP
    wl = hg * tq
    q0 = qi * tq

    q4 = qT_ref[0]
    qT = jnp.concatenate([q4[h * HEAD_DIM:(h + 1) * HEAD_DIM] for h in range(hg)], axis=1)
    qT2 = jnp.concatenate([qT, qT], axis=0)
    rowi = lax.broadcasted_iota(I32, qT2.shape, 0)
    qpad = jnp.where((rowi >> 6) == g, qT2, jnp.zeros_like(qT2))
    slope = sl_ref[0]
    lane = lax.broadcasted_iota(I32, (1, wl), 1)
    t_q = q0 + (lane & (tq - 1))

    nc = kcmp_ref.shape[1]
    rhs_top = jnp.where((rowi >> 6) == g, qT2, srow_ref[0])
    mine_c = (lax.broadcasted_iota(I32, (nc, LANES), 1) >> 6) == g
    lhs_c = jnp.where(mine_c, kcmp_ref[0], augc_ref[0])
    first_c = pl.multiple_of(nc - qi * (tq // NSA_CMP_STRIDE), 8)
    z = jnp.dot(lhs_c, rhs_top, preferred_element_type=F32) + tblc_ref[pl.ds(first_c, nc), :]
    mx = jnp.max(z, axis=0, keepdims=True)
    e = jnp.exp2(z - mx)
    den = jnp.maximum(jnp.sum(e, axis=0, keepdims=True), 1e-30)
    p = e * jnp.where(t_q >= NSA_CMP_LEN - 1, 1.0 / den, 0.0)
    o_c = jnp.dot(vcmpT_ref[0], p.astype(BF16), preferred_element_type=F32)

    pc = p[:, 0:tq]
    for h in range(1, hg):
        pc = pc + p[:, h * tq:(h + 1) * tq]
    n_lc = tq // LANES
    for c in range(n_lc):
        pc_ref[c] = pc[:, c * LANES:(c + 1) * LANES]
    su = NSA_SLC_BLOCK // NSA_CMP_STRIDE
    x = [jnp.concatenate([pc_ref[c, pl.ds(k, n_slc, stride=su), :] for c in range(n_lc)], axis=1)
         for k in range(su)]
    jb = lax.broadcasted_iota(I32, (n_slc, tq), 0)
    prev = jnp.where(jb == 0, 0.0, pltpu.roll(x[3], 1, 0))
    imp = 2.0 * (x[0] + x[1] + x[2]) + x[3] + prev
    cur = (q0 + lax.broadcasted_iota(I32, (1, tq), 1)) >> 6
    allowed = jb <= cur
    forced = jnp.logical_or(jb == 0, jnp.logical_or(jb == cur, jb == cur - 1))
    bias = jnp.where(jnp.logical_and(allowed, forced), 0.0, NEG_BIG)
    sc = jnp.where(jnp.logical_and(allowed, jnp.logical_not(forced)), imp, -1.0)
    for _ in range(topn - 3):
        smx = jnp.max(sc, axis=0, keepdims=True)
        idx = jnp.min(jnp.where(sc == smx, jb, n_slc), axis=0, keepdims=True)
        pick = jnp.logical_and(jb == idx, smx >= 0.0)
        bias = jnp.where(pick, 0.0, bias)
        sc = jnp.where(pick, -1.0, sc)

    n_tiles = augb_ref.shape[0]
    jd = lax.div(q0, SLC_TILE)
    chosen = jnp.where(bias == 0.0, 1.0, 0.0).astype(BF16)
    per_blk = lax.dot_general(jnp.ones((SUBLANES, tq), BF16), chosen, NT_DIMS,
                              preferred_element_type=F32)
    per_shift = (SLC_TILE // NSA_SLC_BLOCK).bit_length() - 1
    in_tile = (lax.broadcasted_iota(I32, (n_slc, LANES), 0) >> per_shift) == lax.broadcasted_iota(
        I32, (n_slc, LANES), 1)
    per_tile = jnp.dot(per_blk.astype(BF16), jnp.where(in_tile, 1.0, 0.0).astype(BF16),
                       preferred_element_type=F32)
    is_past = lax.broadcasted_iota(I32, (SUBLANES, LANES), 1) < jd
    flagv_ref[...] = jnp.where(jnp.logical_and(per_tile > 0.0, is_past), 1, 0).astype(I32)
    flag_copy = pltpu.make_async_copy(flagv_ref, flags_ref, fsem)
    flag_copy.start()

    start = pl.multiple_of(jnp.maximum(q0 - NSA_WINDOW, 0), tq)
    first_w = pl.multiple_of(NSA_WINDOW - (q0 - start), tq)
    kt = kw_ref[0, pl.ds(start, WIN_KEYS), :]
    z = jnp.dot(kt, qpad, preferred_element_type=F32) + bw_ref[0, pl.ds(first_w, WIN_KEYS), :]
    mx = jnp.max(z, axis=0, keepdims=True)
    p = jnp.exp2(z - mx)
    den = jnp.maximum(jnp.sum(p, axis=0, keepdims=True), 1e-30)
    o_w = jnp.dot(vwT_ref[0, :, pl.ds(start, WIN_KEYS)], p.astype(BF16),
                  preferred_element_type=F32) / den

    flag_copy.wait()
    for i in range(TILE_LIST_LEN):
        list_ref[i] = 0

    def add_tile(t, n):
        list_ref[n] = t
        return n + flags_ref[0, t]
    n_used = lax.fori_loop(0, n_tiles, add_tile, 0)
    list_ref[DIAG_SLOT] = jd

    wc = NSA_CHAIN_LANES
    n_ch = wl // wc

    def lane_split(a):
        return [a[:, c * wc:(c + 1) * wc] for c in range(n_ch)]

    if n_slc < LANES:
        bias = jnp.concatenate([bias, jnp.zeros((LANES - n_slc, tq), F32)], axis=0)
    bias4 = jnp.concatenate([bias.astype(BF16)] * hg, axis=1)
    rhs = jnp.concatenate([rhs_top, bias4], axis=0)
    mine = (lax.broadcasted_iota(I32, (SLC_TILE, LANES), 1) >> 6) == g

    def scores(j, null):
        k0 = pl.multiple_of(j * SLC_TILE, SLC_TILE)
        kt = ks_ref[0, pl.ds(k0, SLC_TILE), :]
        lhs = jnp.concatenate([jnp.where(mine, kt, auga_ref[null]), augb_ref[j]], axis=1)
        return jnp.dot(lhs, rhs, preferred_element_type=F32)

    def values(i):
        k0 = pl.multiple_of(list_ref[i] * SLC_TILE, SLC_TILE)
        return [vsT_ref[0, :, pl.ds(k0, SLC_TILE)]] * n_ch

    def offsets(i):
        return lane_split(slope * (q0 - list_ref[i] * SLC_TILE).astype(F32))

    def past_scores(i):
        return lane_split(scores(list_ref[jnp.minimum(i, n_tiles - 1)], (i >= n_used).astype(I32)))

    t_k = jd * SLC_TILE + lax.broadcasted_iota(I32, (SLC_TILE, wl), 0)
    s_diag = lane_split(jnp.where(t_k <= t_q, scores(jd, 0), NEG_BIG))
    c_diag = offsets(DIAG_SLOT)

    def first():
        return [_softmax_stage(s_diag[c], c_diag[c], jnp.full((1, wc), NEG_BIG, F32))
                for c in range(n_ch)]

    o_s = jnp.concatenate(
        _pipelined_tiles(past_scores, values, offsets, s_ref, p_ref, first, n_ch, n_used, DIAG_SLOT, 2),
        axis=1)

    gt = g_ref[0]

    def gate_row(br):
        return jnp.concatenate([gt[br * hg + h:br * hg + h + 1] for h in range(hg)], axis=1)

    o = gate_row(0) * o_c + gate_row(1) * o_s + gate_row(2) * o_w
    o4 = jnp.concatenate([o[:, h * tq:(h + 1) * tq] for h in range(hg)], axis=0)
    o_ref[0] = o4.T.astype(BF16)


def _nsa(nqT, kcmp, vcmpT, ks, vsT, kw, vwT, gT, sl, srow, auga, augb, augc, tblc, bw):
    B, _, S = nqT.shape
    tq = NSA_TQ
    nc = kcmp.shape[1]
    n_slc = S // NSA_SLC_BLOCK
    topn = min(NSA_SLC_TOPN, n_slc)
    wl = NSA_GROUP * tq
    grid = (B, NSA_KV_HEADS, S // tq)
    return pl.pallas_call(
        functools.partial(_nsa_kernel, n_slc=n_slc, topn=topn), grid=grid,
        in_specs=[
            pl.BlockSpec((1, NSA_GROUP * HEAD_DIM, tq), lambda b, g, i: (b, g, i)),
            pl.BlockSpec((1, nc, 128), lambda b, g, i: (b, 0, 0)),
            pl.BlockSpec((1, HEAD_DIM, nc), lambda b, g, i: (b, g, 0)),
            pl.BlockSpec((1, S, 128), lambda b, g, i: (b, 0, 0)),
            pl.BlockSpec((1, HEAD_DIM, S), lambda b, g, i: (b, g, 0)),
            pl.BlockSpec((1, S, 128), lambda b, g, i: (b, 0, 0)),
            pl.BlockSpec((1, HEAD_DIM, S), lambda b, g, i: (b, g, 0)),
            pl.BlockSpec((1, 16, tq), lambda b, g, i: (b, g, i)),
            pl.BlockSpec((1, 1, wl), lambda b, g, i: (g, 0, 0)),
            pl.BlockSpec((1, 2 * HEAD_DIM, wl), lambda b, g, i: (g, 0, 0)),
            pl.BlockSpec((2, SLC_TILE, LANES), lambda b, g, i: (g, 0, 0)),
            pl.BlockSpec(augb.shape, lambda b, g, i: (0, 0, 0)),
            pl.BlockSpec((1, nc, LANES), lambda b, g, i: (g, 0, 0)),
            pl.BlockSpec(tblc.shape, lambda b, g, i: (0, 0)),
            pl.BlockSpec((1,) + bw.shape[1:], lambda b, g, i: (g, 0, 0)),
        ],
        out_specs=pl.BlockSpec((1, tq, NSA_GROUP * HEAD_DIM), lambda b, g, i: (b, i, g)),
        out_shape=jax.ShapeDtypeStruct((B, S, 512), BF16),
        scratch_shapes=[
            pltpu.VMEM((2, wl // NSA_CHAIN_LANES, SLC_TILE, NSA_CHAIN_LANES), F32),
            pltpu.VMEM((2, wl // NSA_CHAIN_LANES, SLC_TILE, NSA_CHAIN_LANES), BF16),
            pltpu.VMEM((tq // LANES, nc, LANES), F32),
            pltpu.VMEM((SUBLANES, LANES), I32),
            pltpu.SMEM((SUBLANES, LANES), I32),
            pltpu.SMEM((TILE_LIST_LEN,), I32),
            pltpu.SemaphoreType.DMA,
        ],
        compiler_params=_params(3), name="nsa",
    )(nqT, kcmp, vcmpT, ks, vsT, kw, vwT, gT, sl, srow, auga, augb, augc, tblc, bw)


def _outproj_kernel(om_ref, on_ref, x_ref, wo_ref, g_ref, wr_ref, br_ref,
                    x1_ref, hn_ref, e_ref, w_ref):
    attn = (jnp.dot(om_ref[...], wo_ref[0:512, :], preferred_element_type=F32)
            + jnp.dot(on_ref[...], wo_ref[512:1024, :], preferred_element_type=F32))
    x1 = x_ref[...] + attn
    x1_ref[...] = x1
    hn = _rmsnorm(x1, g_ref[...])
    _store_token_tiles(hn_ref, hn)
    logits = jnp.dot(hn, wr_ref[...], precision=lax.Precision.HIGHEST,
                     preferred_element_type=F32) + br_ref[...]
    tm = logits.shape[0]
    lane = lax.broadcasted_iota(I32, (tm, LANES), 1)
    sc = jnp.where(lane < N_EXPERTS, logits, -jnp.inf)
    e_out = jnp.zeros((tm, LANES), I32)
    vals = []
    for k in range(TOP_K):
        mx = jnp.max(sc, axis=1, keepdims=True)
        idx = jnp.min(jnp.where(sc == mx, lane, LANES), axis=1, keepdims=True)
        e_out = jnp.where(lane == k, idx, e_out)
        sc = jnp.where(lane == idx, -jnp.inf, sc)
        vals.append(mx)
    ex = [jnp.exp(v - vals[0]) for v in vals]
    den = ex[0] + ex[1] + ex[2] + ex[3]
    w_out = jnp.zeros((tm, LANES), F32)
    for k in range(TOP_K):
        w_out = jnp.where(lane == k, ex[k] / den, w_out)
    e_ref[...] = e_out
    w_ref[...] = w_out


def _outproj(om, on, x, wo, g, wr, br, tm):
    N, D = x.shape
    full = lambda a: pl.BlockSpec(a.shape, lambda i: (0,) * a.ndim)
    row = lambda w: pl.BlockSpec((tm, w), lambda i: (i, 0))
    return pl.pallas_call(
        _outproj_kernel, grid=(N // tm,),
        in_specs=[row(512), row(512), row(D), full(wo), full(g), full(wr), full(br)],
        out_specs=[row(D), pl.BlockSpec((tm * SUBLANES, LANES), lambda i: (i, 0)), row(LANES), row(LANES)],
        out_shape=[jax.ShapeDtypeStruct((N, D), F32), jax.ShapeDtypeStruct((N * SUBLANES, LANES), F32),
                   jax.ShapeDtypeStruct((N, LANES), I32), jax.ShapeDtypeStruct((N, LANES), F32)],
        compiler_params=_params(1), name="outproj_router",
    )(om, on, x, wo, g, wr, br)


def _rank_kernel(e_ref, rank_ref, cnt_ref, base_ref):
    i = pl.program_id(0)
    T = e_ref.shape[0]

    @pl.when(i == 0)
    def _():
        base_ref[...] = jnp.zeros(base_ref.shape, F32)

    e = e_ref[...]
    lane = lax.broadcasted_iota(I32, (T, LANES), 1)
    tril = jnp.where(lax.broadcasted_iota(I32, (T, T), 0) >= lax.broadcasted_iota(I32, (T, T), 1),
                     1.0, 0.0).astype(BF16)
    out = jnp.zeros((T, LANES), I32)
    for k in range(TOP_K):
        hit = lane == e[:, k:k + 1]
        oh = jnp.where(hit, 1.0, 0.0)
        cum = jnp.dot(tril, oh.astype(BF16), preferred_element_type=F32)
        base = base_ref[0:1, :]
        r = jnp.sum(jnp.where(hit, cum - 1.0 + base, 0.0), axis=1, keepdims=True)
        out = jnp.where(lane == k, r.astype(I32), out)
        base_ref[...] = base_ref[...] + jnp.sum(oh, axis=0, keepdims=True)
    rank_ref[...] = out
    cnt_ref[...] = base_ref[...]


def _ranks(e128):
    N = e128.shape[0]
    T = RANK_TILE
    return pl.pallas_call(
        _rank_kernel, grid=(N // T,),
        in_specs=[pl.BlockSpec((T, LANES), lambda i: (i, 0))],
        out_specs=[pl.BlockSpec((T, LANES), lambda i: (i, 0)),
                   pl.BlockSpec((8, LANES), lambda i: (0, 0))],
        out_shape=[jax.ShapeDtypeStruct((N, LANES), I32),
                   jax.ShapeDtypeStruct((8, LANES), F32)],
        scratch_shapes=[pltpu.VMEM((8, LANES), F32)],
        compiler_params=_params(1), name="route_ranks",
    )(e128)


def _row_copy(src, dst, i_src, i_dst, sem):
    return pltpu.make_async_copy(src.at[pl.ds(pl.multiple_of(i_src * SUBLANES, SUBLANES), SUBLANES)],
                                 dst.at[pl.ds(pl.multiple_of(i_dst * SUBLANES, SUBLANES), SUBLANES)], sem)


def _store_token_tiles(ref, x):
    rows = x.shape[0]
    for c in range(SUBLANES):
        ref[pl.ds(c, rows, stride=SUBLANES), :] = x[:, c * LANES:(c + 1) * LANES]


def _load_token_tiles(ref, rows):
    return jnp.concatenate([ref[pl.ds(c, rows, stride=SUBLANES), :] for c in range(SUBLANES)], axis=1)


def _dispatch_kernel(dest_hbm, hp_ref, xz_hbm, out_hbm, idx_ref, isem, sem):
    del xz_hbm
    i = pl.program_id(0)
    T = ROUTE_TILE
    cp = pltpu.make_async_copy(dest_hbm.at[i], idx_ref, isem)
    cp.start()
    cp.wait()

    def issue(t, carry):
        for k in range(TOP_K):
            _row_copy(hp_ref, out_hbm, t, idx_ref[t * TOP_K + k], sem).start(priority=k % 2)
        return carry
    lax.fori_loop(0, T, issue, 0, unroll=ROW_DMA_UNROLL)

    def drain(t, carry):
        for k in range(TOP_K):
            _row_copy(hp_ref, out_hbm, 0, 0, sem).wait()
        return carry
    lax.fori_loop(0, T, drain, 0, unroll=ROW_DMA_UNROLL)


def _dispatch(dest2, hp, xzero):
    nsteps = dest2.shape[0]
    T = ROUTE_TILE
    return pl.pallas_call(
        _dispatch_kernel, grid=(nsteps,),
        in_specs=[pl.BlockSpec(memory_space=pl.ANY),
                  pl.BlockSpec((T * SUBLANES, LANES), lambda i: (i, 0)),
                  pl.BlockSpec(memory_space=pl.ANY)],
        out_specs=pl.BlockSpec(memory_space=pl.ANY),
        out_shape=jax.ShapeDtypeStruct(xzero.shape, xzero.dtype),
        scratch_shapes=[pltpu.SMEM((T * TOP_K,), I32),
                        pltpu.SemaphoreType.DMA, pltpu.SemaphoreType.DMA],
        input_output_aliases={2: 0},
        compiler_params=_params(1), name="moe_dispatch",
    )(dest2, hp, xzero)


W_PREP_COLS = 256


def _expert_kernel(be_ref, na_ref, x_ref, wup_ref, bg_ref, bu_ref, wd_ref, bd_ref, y_ref,
                   wg_ref, wu_ref, wt_ref):
    b = pl.program_id(0)
    active = b < na_ref[0]
    new_expert = jnp.logical_or(b == 0, be_ref[b] != be_ref[jnp.maximum(b - 1, 0)])

    @pl.when(jnp.logical_and(active, new_expert))
    def _():
        half = W_PREP_COLS // 2
        n_lc = wt_ref.shape[0]

        def every_other(first):
            return jnp.concatenate(
                [wt_ref[c, pl.ds(first, half, stride=2), :] for c in range(n_lc)], axis=1).astype(BF16)

        for ch in range(wup_ref.shape[2] // W_PREP_COLS):
            panel = wup_ref[0, :, ch * W_PREP_COLS:(ch + 1) * W_PREP_COLS].T
            for c in range(n_lc):
                wt_ref[c] = panel[:, c * LANES:(c + 1) * LANES]
            wg_ref[ch * half:(ch + 1) * half, :] = every_other(0)
            wu_ref[ch * half:(ch + 1) * half, :] = every_other(1)

    @pl.when(active)
    def _():
        xb = _load_token_tiles(x_ref, MOE_ROWS).astype(BF16)
        gg = lax.dot_general(xb, wg_ref[...], NT_DIMS, preferred_element_type=F32) + bg_ref[0]
        uu = lax.dot_general(xb, wu_ref[...], NT_DIMS, preferred_element_type=F32) + bu_ref[0]
        gg = jnp.minimum(gg, SWIGLU_LIMIT)
        uu = jnp.clip(uu, -SWIGLU_LIMIT, SWIGLU_LIMIT)
        a = gg * jax.nn.sigmoid(SWIGLU_ALPHA * gg) * (uu + 1.0)
        _store_token_tiles(y_ref, jnp.dot(a.astype(BF16), wd_ref[0], preferred_element_type=F32) + bd_ref[0])

    @pl.when(jnp.logical_not(active))
    def _():
        y_ref[...] = jnp.zeros(y_ref.shape, F32)


def _experts(blk_e, n_act, xrows, w_up, bg, bu, wd, bd):
    _, D, F2 = w_up.shape
    F = F2 // 2
    assert D == SUBLANES * LANES
    P = xrows.shape[0] // SUBLANES
    n_blk = P // MOE_ROWS
    rows_spec = pl.BlockSpec((MOE_ROWS * SUBLANES, LANES), lambda b, be, na: (b, 0))
    wspec = lambda r, c: pl.BlockSpec((1, r, c), lambda b, be, na: (be[b], 0, 0))
    grid_spec = pltpu.PrefetchScalarGridSpec(
        num_scalar_prefetch=2, grid=(n_blk,),
        in_specs=[rows_spec, wspec(D, F2), wspec(1, F), wspec(1, F), wspec(F, D), wspec(1, D)],
        out_specs=rows_spec,
        scratch_shapes=[pltpu.VMEM((F, D), BF16), pltpu.VMEM((F, D), BF16),
                        pltpu.VMEM((D // LANES, W_PREP_COLS, LANES), F32)],
    )
    return pl.pallas_call(
        _expert_kernel, grid_spec=grid_spec,
        out_shape=jax.ShapeDtypeStruct(xrows.shape, F32),
        compiler_params=_params(1), name="moe_experts",
    )(blk_e, n_act, xrows, w_up, bg, bu, wd, bd)


def _combine_kernel(dest_hbm, x1_ref, w_ref, g_ref, y_hbm, o_ref, idx_ref, buf_ref, isem, sem):
    i = pl.program_id(0)
    T = ROUTE_TILE
    cp = pltpu.make_async_copy(dest_hbm.at[i], idx_ref, isem)
    cp.start()
    cp.wait()

    def issue(t, carry):
        for k in range(TOP_K):
            _row_copy(y_hbm, buf_ref.at[k], idx_ref[t * TOP_K + k], t, sem).start(priority=k % 2)
        return carry
    lax.fori_loop(0, T, issue, 0, unroll=ROW_DMA_UNROLL)

    def drain(t, carry):
        for k in range(TOP_K):
            _row_copy(y_hbm, buf_ref.at[k], 0, 0, sem).wait()
        return carry
    lax.fori_loop(0, T, drain, 0, unroll=ROW_DMA_UNROLL)

    x1 = x1_ref[...]
    w = w_ref[...]
    cols = []
    for c in range(SUBLANES):
        acc = x1[:, c * LANES:(c + 1) * LANES]
        for k in range(TOP_K):
            acc = acc + w[:, k:k + 1] * buf_ref[k, pl.ds(c, T, stride=SUBLANES), :]
        cols.append(acc)
    o_ref[...] = _rmsnorm(jnp.concatenate(cols, axis=1), g_ref[...])


def _combine(dest2, x1, w128, g, yrows):
    N, D = x1.shape
    T = ROUTE_TILE
    return pl.pallas_call(
        _combine_kernel, grid=(N // T,),
        in_specs=[pl.BlockSpec(memory_space=pl.ANY),
                  pl.BlockSpec((T, D), lambda i: (i, 0)),
                  pl.BlockSpec((T, LANES), lambda i: (i, 0)),
                  pl.BlockSpec(g.shape, lambda i: (0, 0)),
                  pl.BlockSpec(memory_space=pl.ANY)],
        out_specs=pl.BlockSpec((T, D), lambda i: (i, 0)),
        out_shape=jax.ShapeDtypeStruct((N, D), F32),
        scratch_shapes=[pltpu.SMEM((T * TOP_K,), I32),
                        pltpu.VMEM((TOP_K, T * SUBLANES, LANES), F32),
                        pltpu.SemaphoreType.DMA, pltpu.SemaphoreType.DMA],
        compiler_params=_params(1), name="moe_combine",
    )(dest2, x1, w128, g, yrows)


def _alibi_slopes():
    n = MOBA_HEADS + NSA_HEADS
    s = jnp.exp2(-8.0 * jnp.arange(1, n + 1, dtype=F32) / n)
    return s[0::2], s[1::2]


def _prep_inproj(w_in):
    hd = HEAD_DIM
    sizes = [MOBA_HEADS * hd] * 3 + [NSA_HEADS * hd] + [NSA_KV_HEADS * hd] * 6 + [NSA_BRANCHES * NSA_HEADS]
    cuts = np.cumsum([0] + sizes)
    mq, mk, mv, nq, kc, vc, ks, vs, kw, vw, ng = [w_in[:, cuts[i]:cuts[i + 1]] for i in range(11)]
    qscale = (hd ** -0.5) * LOG2E
    wr = jnp.concatenate([mk, kc, vc, ks, kw], axis=1).astype(BF16)
    wt = jnp.concatenate([mq * qscale, mv, nq * qscale, vs, vw], axis=1).T.astype(BF16)
    ngr = ng.reshape(-1, NSA_KV_HEADS, NSA_GROUP, NSA_BRANCHES).transpose(1, 3, 2, 0)
    ngr = ngr.reshape(NSA_KV_HEADS, NSA_BRANCHES * NSA_GROUP, -1)
    wg = jnp.pad(ngr, ((0, 0), (0, 16 - NSA_BRANCHES * NSA_GROUP), (0, 0))).reshape(32, -1)
    return wr, wt, wg.astype(F32)


def _prep_compress(w1, w2, pe):
    hd, half = HEAD_DIM, NSA_CMP_STRIDE
    w1r = w1.reshape(2, half, hd, hd)
    eye = jnp.eye(NSA_KV_HEADS, dtype=w1.dtype)
    w = jnp.einsum('alde,gh->lgdahe', w1r, eye).reshape(half * NSA_KV_HEADS * hd, 2 * NSA_KV_HEADS * hd)
    w2b = jnp.einsum('de,gh->gdhe', w2, eye).reshape(NSA_KV_HEADS * hd, NSA_KV_HEADS * hd)
    per = pe.reshape(2, half, 1, hd)
    pe2 = jnp.broadcast_to(per, (2, half, NSA_KV_HEADS, hd)).reshape(2, 1, half * NSA_KV_HEADS * hd)
    pe2 = jnp.broadcast_to(pe2, (2, 8, pe2.shape[2]))
    return w.astype(BF16), w2b.astype(BF16), pe2.astype(F32)


def _attention_tables(S):
    moba_sl, nsa_sl = _alibi_slopes()
    moba_sl = moba_sl * LOG2E
    nsa_sl = nsa_sl * LOG2E
    blk = MOBA_BLOCK
    moba_row = jnp.broadcast_to(moba_sl[:, None, None], (MOBA_HEADS, 1, blk))
    nb = S // blk
    nbp = -(-nb // 16) * 16
    col = jnp.arange(LANES)[None, None, :]
    tile = jnp.arange(nb + 1)[:, None, None]
    off = jnp.arange(blk, dtype=F32)[None, :, None]
    moba_aug = jnp.where(jnp.logical_and(col == tile, tile < nb), 1.0,
                         jnp.where(jnp.logical_and(col >= nbp, col < nbp + 3), off, 0.0)).astype(BF16)
    parts = jnp.stack(list(_split3(moba_sl)) + [jnp.zeros_like(moba_sl)] * 13, axis=1)
    moba_srow = jnp.broadcast_to(parts[:, :, None], (MOBA_HEADS, 16, blk)).astype(BF16)

    wl = NSA_GROUP * NSA_TQ
    n_slc = S // NSA_SLC_BLOCK
    assert n_slc <= LANES, "block-choice rows must fit the spare contraction rows"
    nsa_row = jnp.repeat(nsa_sl.reshape(NSA_KV_HEADS, NSA_GROUP), NSA_TQ, axis=1)
    hi, mid, lo = [t[:, None, :] for t in _split3(nsa_row)]
    base = ((1 - jnp.arange(NSA_KV_HEADS)) * HEAD_DIM)[:, None, None]
    rows = jnp.arange(2 * HEAD_DIM)[None, :, None]
    nsa_srow = jnp.where(rows == base, hi, jnp.where(rows == base + 1, mid, jnp.where(
        rows == base + 2, lo, jnp.where(rows == base + 3, NEG_BIG, 0.0)))).astype(BF16)
    lane = jnp.arange(LANES)[None, None, None, :]
    base4 = base[:, None]
    null = jnp.arange(2, dtype=F32)[None, :, None, None]
    koff = jnp.arange(SLC_TILE, dtype=F32)[None, None, :, None]
    nsa_auga = jnp.where(jnp.logical_and(lane >= base4, lane < base4 + 3), koff,
                         jnp.where(lane == base4 + 3, null, 0.0))
    nsa_auga = nsa_auga.reshape(2 * NSA_KV_HEADS, SLC_TILE, LANES).astype(BF16)
    per = SLC_TILE // NSA_SLC_BLOCK
    tile = jnp.arange(S // SLC_TILE)[:, None, None]
    blk_of = tile * per + jnp.arange(SLC_TILE)[None, :, None] // NSA_SLC_BLOCK
    nsa_augb = (jnp.arange(LANES)[None, None, :] == blk_of).astype(BF16)

    nc = S // NSA_CMP_STRIDE
    ci = jnp.arange(nc)[None, :, None]
    lane3 = jnp.arange(LANES)[None, None, :]
    nsa_augc = jnp.where(jnp.logical_and(lane3 >= base + 4, lane3 < base + 7), (ci >> 1).astype(F32),
                         jnp.where(jnp.logical_and(lane3 >= base + 7, lane3 < base + 10),
                                   (ci & 1).astype(F32), 0.0)).astype(BF16)
    step2 = [t[:, None, :] for t in _split3(nsa_row * (2.0 * NSA_CMP_STRIDE))]
    step1 = [t[:, None, :] for t in _split3(nsa_row * (1.0 * NSA_CMP_STRIDE))]
    for k in range(3):
        nsa_srow = jnp.where(rows == base + 4 + k, step2[k].astype(BF16),
                             jnp.where(rows == base + 7 + k, step1[k].astype(BF16), nsa_srow))
    il = jnp.tile(jnp.arange(NSA_TQ), NSA_GROUP)[None, :]
    rel = (jnp.arange(2 * nc) - nc)[:, None]
    nsa_tblc = jnp.where(rel * NSA_CMP_STRIDE + (NSA_CMP_LEN - 1) <= il, 0.0, NEG_BIG).astype(F32)
    dist = (NSA_WINDOW + il - jnp.arange(NSA_WINDOW + WIN_KEYS)[:, None])[None]
    nsa_bw = jnp.where(jnp.logical_and(dist >= 0, dist < NSA_WINDOW),
                       -nsa_row[:, None, :] * dist.astype(F32), NEG_BIG)
    return ((moba_aug, moba_srow, moba_row),
            (nsa_row.reshape(NSA_KV_HEADS, 1, wl), nsa_srow, nsa_auga, nsa_augb, nsa_augc, nsa_tblc, nsa_bw))


def _split3(x):
    hi = x.astype(BF16).astype(F32)
    mid = (x - hi).astype(BF16).astype(F32)
    lo = (x - hi - mid).astype(BF16).astype(F32)
    return hi, mid, lo


def _attention(x, attn_norm_g, w_in, cmp_pe_k, cmp_pe_v, cmp_w1_k, cmp_w2_k, cmp_w1_v, cmp_w2_v):
    B, S, D = x.shape
    wr, wt, wg = _prep_inproj(w_in)
    (mk, kc, vc, ks, kw, mqT, mvT, nqT, vsT, vwT, gT) = _inproj(
        x, attn_norm_g.reshape(1, D), wr, wt, wg, tm=512)
    moba_tabs, nsa_tabs = _attention_tables(S)
    o_moba = _moba(mqT, mk, mvT, *moba_tabs)
    wk, w2k, pek = _prep_compress(cmp_w1_k, cmp_w2_k, cmp_pe_k)
    wv, w2v, pev = _prep_compress(cmp_w1_v, cmp_w2_v, cmp_pe_v)
    nc = S // NSA_CMP_STRIDE
    kcmp, vcmpT = _compress(kc.reshape(B, nc, -1), vc.reshape(B, nc, -1), wk, wv.T, pek, pev, w2k, w2v.T)
    o_nsa = _nsa(nqT, kcmp, vcmpT, ks, vsT, kw, vwT, gT, *nsa_tabs)
    return o_moba, o_nsa


def _moe(x1, hn, e128, w128, w_up, b_up, w_down, b_down, final_norm_g):
    N, D = x1.shape
    rank128, cnt = _ranks(e128)
    counts = cnt[0, :N_EXPERTS].astype(I32)
    padded = (counts + MOE_ROWS - 1) // MOE_ROWS * MOE_ROWS
    pends = jnp.cumsum(padded)
    pstarts = pends - padded
    e4 = e128[:, :TOP_K]
    dest = pstarts[e4] + rank128[:, :TOP_K]
    dest2 = dest.reshape(N // ROUTE_TILE, ROUTE_TILE * TOP_K)
    n_blk = (N * TOP_K + N_EXPERTS * MOE_ROWS + MOE_ROWS - 1) // MOE_ROWS
    P = n_blk * MOE_ROWS
    blk_start = jnp.arange(n_blk, dtype=I32) * MOE_ROWS
    blk_e = jnp.minimum(jnp.sum((pends[None, :] <= blk_start[:, None]).astype(I32), axis=1), N_EXPERTS - 1)
    n_act = (pends[-1:] // MOE_ROWS).astype(I32)
    xrows = _dispatch(dest2, hn, jnp.zeros((P * SUBLANES, LANES), F32))
    bg = b_up[:, None, 0::2]
    bu = b_up[:, None, 1::2]
    yrows = _experts(blk_e, n_act, xrows, w_up, bg, bu, w_down.astype(BF16), b_down[:, None, :])
    return _combine(dest2, x1, w128, final_norm_g.reshape(1, D), yrows)


def kernel(x, attn_norm_g, w_in, cmp_pe_k, cmp_pe_v, cmp_w1_k, cmp_w2_k, cmp_w1_v, cmp_w2_v, w_out, ffn_norm_g, w_router, b_router, w_up, b_up, w_down, b_down, final_norm_g):
    B, S, D = x.shape
    assert attn_norm_g.shape[0] == 1, "single-layer kernel"
    o_moba, o_nsa = _attention(x, attn_norm_g[0], w_in[0], cmp_pe_k[0], cmp_pe_v[0],
                               cmp_w1_k[0], cmp_w2_k[0], cmp_w1_v[0], cmp_w2_v[0])
    N = B * S
    wr = jnp.pad(w_router[0], ((0, 0), (0, LANES - N_EXPERTS)))
    br = jnp.pad(b_router[0], (0, LANES - N_EXPERTS)).reshape(1, LANES)
    x1, hn, e128, w128 = _outproj(o_moba.reshape(N, -1), o_nsa.reshape(N, -1), x.reshape(N, D),
                                  w_out[0].astype(BF16), ffn_norm_g[0].reshape(1, D), wr, br, tm=512)
    out = _moe(x1, hn, e128, w128, w_up[0], b_up[0], w_down[0], b_down[0], final_norm_g)
    return out.reshape(B, S, D)
```

```python
import functools

import jax
import jax.numpy as jnp
import numpy as np
from jax import lax
from jax.experimental import pallas as pl
from jax.experimental.pallas import tpu as pltpu

F32 = jnp.float32
BF16 = jnp.bfloat16
I32 = jnp.int32

HEAD_DIM = 64
MOBA_HEADS = 8
NSA_HEADS = 8
NSA_KV_HEADS = 2
NSA_GROUP = NSA_HEADS // NSA_KV_HEADS
MOBA_BLOCK = 256
MOBA_TOPK = 3
NSA_CMP_LEN = 32
NSA_CMP_STRIDE = 16
NSA_SLC_BLOCK = 64
NSA_SLC_TOPN = 16
NSA_WINDOW = 512
NSA_BRANCHES = 3
N_EXPERTS = 32
TOP_K = 4
SWIGLU_LIMIT = 7.0
SWIGLU_ALPHA = 1.702
RMS_EPS = 1e-5
NEG_BIG = -1e30
LOG2E = 1.4426950408889634

LANES = 128
SUBLANES = 8
VMEM_LIMIT = 56 * 1024 * 1024

MOBA_TILES_PER_TRIP = 2
NSA_TQ = 256
NSA_CHAIN_LANES = 256
TILE_LIST_LEN = 64
DIAG_SLOT = TILE_LIST_LEN - 1
SLC_TILE = 256
WIN_KEYS = NSA_WINDOW + NSA_TQ
MOE_ROWS = 512
ROUTE_TILE = 1024
RANK_TILE = 512
ROW_DMA_UNROLL = 8

NT_DIMS = (((1,), (1,)), ((), ()))


def _params(n_grid):
    return pltpu.CompilerParams(
        dimension_semantics=("arbitrary",) * n_grid,
        vmem_limit_bytes=VMEM_LIMIT,
    )


def _rmsnorm(x, g):
    return x * lax.rsqrt(jnp.mean(x * x, axis=-1, keepdims=True) + RMS_EPS) * g


def _inproj_kernel(x_ref, g_ref, wr_ref, wt_ref, wg_ref,
                   mk_ref, kc_ref, vc_ref, ks_ref, kw_ref,
                   mqT_ref, mvT_ref, nqT_ref, vsT_ref, vwT_ref, gT_ref):
    xn = _rmsnorm(x_ref[0], g_ref[...])
    xb = xn.astype(BF16)
    yr = jnp.dot(xb, wr_ref[...], preferred_element_type=F32)
    mk_ref[0] = yr[:, 0:512].astype(BF16)
    kc_ref[0] = yr[:, 512:640].astype(BF16)
    vc_ref[0] = yr[:, 640:768].astype(BF16)
    ks_ref[0] = yr[:, 768:896].astype(BF16)
    kw_ref[0] = yr[:, 896:1024].astype(BF16)
    yt = lax.dot_general(wt_ref[...], xb, NT_DIMS, preferred_element_type=F32)
    mqT_ref[0] = yt[0:512].astype(BF16)
    mvT_ref[0] = yt[512:1024].astype(BF16)
    nqT_ref[0] = yt[1024:1536].astype(BF16)
    vsT_ref[0] = yt[1536:1664].astype(BF16)
    vwT_ref[0] = yt[1664:1792].astype(BF16)
    gl = lax.dot_general(wg_ref[...], xn, NT_DIMS, precision=lax.Precision.HIGHEST,
                         preferred_element_type=F32)
    gT_ref[0] = jax.nn.sigmoid(gl)


def _inproj(x, g, wr, wt, wg, tm):
    B, S, D = x.shape
    grid = (B, S // tm)
    row = lambda w: pl.BlockSpec((1, tm, w), lambda b, i: (b, i, 0))
    col = lambda h: pl.BlockSpec((1, h, tm), lambda b, i: (b, 0, i))
    full = lambda a: pl.BlockSpec(a.shape, lambda b, i: (0,) * a.ndim)
    out_shape = [
        jax.ShapeDtypeStruct((B, S, 512), BF16),
        jax.ShapeDtypeStruct((B, S, 128), BF16),
        jax.ShapeDtypeStruct((B, S, 128), BF16),
        jax.ShapeDtypeStruct((B, S, 128), BF16),
        jax.ShapeDtypeStruct((B, S, 128), BF16),
        jax.ShapeDtypeStruct((B, 512, S), BF16),
        jax.ShapeDtypeStruct((B, 512, S), BF16),
        jax.ShapeDtypeStruct((B, 512, S), BF16),
        jax.ShapeDtypeStruct((B, 128, S), BF16),
        jax.ShapeDtypeStruct((B, 128, S), BF16),
        jax.ShapeDtypeStruct((B, 32, S), F32),
    ]
    out_specs = [row(512), row(128), row(128), row(128), row(128),
                 col(512), col(512), col(512), col(128), col(128), col(32)]
    return pl.pallas_call(
        _inproj_kernel, grid=grid,
        in_specs=[pl.BlockSpec((1, tm, D), lambda b, i: (b, i, 0)),
                  full(g), full(wr), full(wt), full(wg)],
        out_specs=out_specs, out_shape=out_shape,
        compiler_params=_params(2), name="inproj",
    )(x, g, wr, wt, wg)


ONES_ROWS = 16


def _softmax_stage(s, c, m):
    mt = jnp.max(s, axis=0, keepdims=True) - c
    m_new = jnp.maximum(m, mt)
    alpha = jnp.exp2(m - m_new)
    p = jnp.exp2(s - (m_new + c))
    return m_new, p.astype(BF16), alpha


def _pipelined_tiles(scores, values, offsets, s_ref, p_ref, first, n_ch, n_tiles, j_first, per_trip):
    chains = range(n_ch)

    def qk_into(slot, j):
        sc = scores(j)
        for c in chains:
            s_ref[slot, c] = sc[c]

    def pv_from(slot, j, alphas, accs):
        out = []
        for c, vt in zip(chains, values(j)):
            vt1 = jnp.concatenate([vt, jnp.ones((ONES_ROWS, vt.shape[1]), BF16)], axis=0)
            out.append(alphas[c] * accs[c] + jnp.dot(vt1, p_ref[slot, c], preferred_element_type=F32))
        return out

    def softmax_into(slot, j, ms):
        cs = offsets(j)
        new = [_softmax_stage(s_ref[slot, c], cs[c], ms[c]) for c in chains]
        for c in chains:
            p_ref[slot, c] = new[c][1]
        return [n[0] for n in new], [n[2] for n in new]

    qk_into(0, 0)
    first = first()
    for c in chains:
        p_ref[1, c] = first[c][1]

    def trip(i, carry):
        ms, alphas, accs, j_prev = carry
        for u in range(per_trip):
            t = per_trip * i + u
            qk_into(1 - u % 2, t + 1)
            accs = pv_from(1 - u % 2, j_prev, alphas, accs)
            ms, alphas = softmax_into(u % 2, t, ms)
            j_prev = t
        return ms, alphas, accs, j_prev

    n_q = first[0][0].shape[1]
    init = ([f[0] for f in first], [f[2] for f in first],
            [jnp.zeros((HEAD_DIM + ONES_ROWS, n_q), F32)] * n_ch, j_first)
    _, alphas, accs, j_last = lax.fori_loop(0, (n_tiles + per_trip - 1) // per_trip, trip, init)
    accs = pv_from(1, j_last, alphas, accs)
    return [accs[c][:HEAD_DIM] / jnp.maximum(accs[c][HEAD_DIM:HEAD_DIM + 1], 1e-30) for c in chains]


def _moba_kernel(qT_ref, k_ref, vT_ref, aug_ref, srow_ref, sl_ref, o_ref,
                 kmean_ref, kparts_ref, s_ref, p_ref, *, nb, nbp, topk):
    qi = pl.program_id(2)
    blk = MOBA_BLOCK

    @pl.when(qi == 0)
    def _():
        kmean_ref[...] = jnp.zeros(kmean_ref.shape, F32)

        def body(n, carry):
            kb = k_ref[0, pl.ds(pl.multiple_of(n * blk, blk), blk), :].astype(F32)
            kmean_ref[pl.ds(n, 1), :] = jnp.mean(kb, axis=0, keepdims=True)
            return carry
        lax.fori_loop(0, nb, body, 0)
        km = kmean_ref[...]
        head = lax.broadcasted_iota(I32, km.shape, 1) >> 6
        km2 = jnp.concatenate([jnp.where(head == h, km, 0.0) for h in range(2)], axis=0)
        hi = km2.astype(BF16)
        mid = (km2 - hi.astype(F32)).astype(BF16)
        lo = (km2 - hi.astype(F32) - mid.astype(F32)).astype(BF16)
        kparts_ref[0] = hi
        kparts_ref[1] = mid
        kparts_ref[2] = lo

    qT = qT_ref[0]
    row = lax.broadcasted_iota(I32, qT.shape, 0)
    qpad = [jnp.where((row >> 6) == h, qT, jnp.zeros_like(qT)) for h in range(2)]

    gates = (jnp.dot(kparts_ref[0], qT, preferred_element_type=F32)
             + jnp.dot(kparts_ref[1], qT, preferred_element_type=F32)
             + jnp.dot(kparts_ref[2], qT, preferred_element_type=F32))
    bidx = lax.broadcasted_iota(I32, (nbp, blk), 0)
    rhs = []
    for h in range(2):
        gate = gates[h * nbp:(h + 1) * nbp]
        gsc = jnp.where(bidx < qi, gate, -jnp.inf)
        bias = jnp.full((nbp, blk), NEG_BIG, F32)
        for _ in range(topk):
            mx = jnp.max(gsc, axis=0, keepdims=True)
            idx = jnp.min(jnp.where(gsc == mx, bidx, nbp), axis=0, keepdims=True)
            pick = jnp.logical_and(bidx == idx, mx > -jnp.inf)
            bias = jnp.where(pick, 0.0, bias)
            gsc = jnp.where(pick, -jnp.inf, gsc)
        pad = jnp.zeros((2 * HEAD_DIM - nbp - 16, blk), BF16)
        rhs.append(jnp.concatenate([qpad[h], bias.astype(BF16), srow_ref[h], pad], axis=0))

    def scores(j, a):
        k0 = pl.multiple_of(j * blk, blk)
        lhs = jnp.concatenate([k_ref[0, pl.ds(k0, blk), :], aug_ref[a]], axis=1)
        return [jnp.dot(lhs, rhs[h], preferred_element_type=F32) for h in range(2)]

    def values(j):
        k0 = pl.multiple_of(j * blk, blk)
        return [vT_ref[0, h * HEAD_DIM:(h + 1) * HEAD_DIM, pl.ds(k0, blk)] for h in range(2)]

    def offsets(j):
        dq = ((qi - j) * blk).astype(F32)
        return [sl_ref[h] * dq for h in range(2)]

    ik = lax.broadcasted_iota(I32, (blk, blk), 0)
    iq = lax.broadcasted_iota(I32, (blk, blk), 1)
    s_own = [jnp.where(ik <= iq, s, NEG_BIG) for s in scores(qi, nb)]
    m0 = jnp.full((1, blk), NEG_BIG, F32)

    def first():
        return [_softmax_stage(s_own[h], jnp.zeros((1, blk), F32), m0) for h in range(2)]

    outs = _pipelined_tiles(lambda j: scores(jnp.minimum(j, nb - 1), jnp.minimum(j, nb - 1)),
                            values, offsets, s_ref, p_ref, first, 2, qi, qi, MOBA_TILES_PER_TRIP)
    o_ref[0] = jnp.concatenate(outs, axis=0).T.astype(BF16)


def _moba(mqT, mk, mvT, aug, srow, sl):
    B, _, S = mqT.shape
    blk = MOBA_BLOCK
    nb = S // blk
    topk = min(MOBA_TOPK, nb)
    nbp = -(-nb // 16) * 16
    grid = (B, MOBA_HEADS // 2, nb)
    return pl.pallas_call(
        functools.partial(_moba_kernel, nb=nb, nbp=nbp, topk=topk), grid=grid,
        in_specs=[
            pl.BlockSpec((1, 128, blk), lambda b, p, i: (b, p, i)),
            pl.BlockSpec((1, S, 128), lambda b, p, i: (b, 0, p)),
            pl.BlockSpec((1, 128, S), lambda b, p, i: (b, p, 0)),
            pl.BlockSpec(aug.shape, lambda b, p, i: (0, 0, 0)),
            pl.BlockSpec((2, 16, blk), lambda b, p, i: (p, 0, 0)),
            pl.BlockSpec((2, 1, blk), lambda b, p, i: (p, 0, 0)),
        ],
        out_specs=pl.BlockSpec((1, blk, 128), lambda b, p, i: (b, i, p)),
        out_shape=jax.ShapeDtypeStruct((B, S, 512), BF16),
        scratch_shapes=[
            pltpu.VMEM((nbp, 128), F32),
            pltpu.VMEM((3, 2 * nbp, 128), BF16),
            pltpu.VMEM((2, 2, blk, blk), F32),
            pltpu.VMEM((2, 2, blk, blk), BF16),
        ],
        compiler_params=_params(3), name="moba",
    )(mqT, mk, mvT, aug, srow, sl)


def _compress_kernel(kc_ref, vc_ref, wk_ref, wvT_ref, pek_ref, pev_ref, w2k_ref, w2vT_ref,
                     kcmp_ref, vcmpT_ref):
    nc = kc_ref.shape[1]

    wk = wk_ref[...]
    ab = jnp.dot(kc_ref[0], wk, preferred_element_type=F32)
    pt = (jnp.dot(pek_ref[0], wk[:, 0:128].astype(F32), preferred_element_type=F32)
          + jnp.dot(pek_ref[1], wk[:, 128:256].astype(F32), preferred_element_type=F32))
    pre = ab[:, 0:128] + pltpu.roll(ab[:, 128:256], nc - 1, 0) + pt[0:1]
    hid = jax.nn.gelu(pre)
    kcmp_ref[0] = jnp.dot(hid.astype(BF16), w2k_ref[...], preferred_element_type=F32).astype(BF16)

    wvT = wvT_ref[...]
    abT = lax.dot_general(wvT, vc_ref[0], NT_DIMS, preferred_element_type=F32)
    ptT = (lax.dot_general(wvT[0:128].astype(F32), pev_ref[0], NT_DIMS, preferred_element_type=F32)
           + lax.dot_general(wvT[128:256].astype(F32), pev_ref[1], NT_DIMS, preferred_element_type=F32))
    preT = abT[0:128] + pltpu.roll(abT[128:256], nc - 1, 1) + ptT[:, 0:1]
    hidT = jax.nn.gelu(preT)
    vcmpT_ref[0] = jnp.dot(w2vT_ref[...], hidT.astype(BF16), preferred_element_type=F32).astype(BF16)


def _compress(kc2, vc2, wk, wvT, pek, pev, w2k, w2vT):
    B, nc, _ = kc2.shape
    full = lambda a: pl.BlockSpec(a.shape, lambda b: (0,) * a.ndim)
    blk = pl.BlockSpec((1, nc, kc2.shape[2]), lambda b: (b, 0, 0))
    return pl.pallas_call(
        _compress_kernel, grid=(B,),
        in_specs=[blk, blk, full(wk), full(wvT), full(pek), full(pev), full(w2k), full(w2vT)],
        out_specs=[pl.BlockSpec((1, nc, 128), lambda b: (b, 0, 0)),
                   pl.BlockSpec((1, 128, nc), lambda b: (b, 0, 0))],
        out_shape=[jax.ShapeDtypeStruct((B, nc, 128), BF16),
                   jax.ShapeDtypeStruct((B, 128, nc), BF16)],
        compiler_params=_params(1), name="nsa_compress",
    )(kc2, vc2, wk, wvT, pek, pev, w2k, w2vT)


def _nsa_kernel(qT_ref, kcmp_ref, vcmpT_ref, ks_ref, vsT_ref, kw_ref, vwT_ref,
                g_ref, sl_ref, srow_ref, auga_ref, augb_ref, augc_ref, tblc_ref, bw_ref, o_ref,
                s_ref, p_ref, pc_ref, flagv_ref, flags_ref, list_ref, fsem, *, n_slc, topn):
    g = pl.program_id(1)
    qi = pl.program_id(2)
    tq = NSA_TQ
    hg = NSA_GROUP
    wl = hg * tq
    q0 = qi * tq

    q4 = qT_ref[0]
    qT = jnp.concatenate([q4[h * HEAD_DIM:(h + 1) * HEAD_DIM] for h in range(hg)], axis=1)
    qT2 = jnp.concatenate([qT, qT], axis=0)
    rowi = lax.broadcasted_iota(I32, qT2.shape, 0)
    qpad = jnp.where((rowi >> 6) == g, qT2, jnp.zeros_like(qT2))
    slope = sl_ref[0]
    lane = lax.broadcasted_iota(I32, (1, wl), 1)
    t_q = q0 + (lane & (tq - 1))

    nc = kcmp_ref.shape[1]
    rhs_top = jnp.where((rowi >> 6) == g, qT2, srow_ref[0])
    mine_c = (lax.broadcasted_iota(I32, (nc, LANES), 1) >> 6) == g
    lhs_c = jnp.where(mine_c, kcmp_ref[0], augc_ref[0])
    first_c = pl.multiple_of(nc - qi * (tq // NSA_CMP_STRIDE), 8)
    z = jnp.dot(lhs_c, rhs_top, preferred_element_type=F32) + tblc_ref[pl.ds(first_c, nc), :]
    mx = jnp.max(z, axis=0, keepdims=True)
    e = jnp.exp2(z - mx)
    den = jnp.maximum(jnp.sum(e, axis=0, keepdims=True), 1e-30)
    p = e * jnp.where(t_q >= NSA_CMP_LEN - 1, 1.0 / den, 0.0)
    o_c = jnp.dot(vcmpT_ref[0], p.astype(BF16), preferred_element_type=F32)

    pc = p[:, 0:tq]
    for h in range(1, hg):
        pc = pc + p[:, h * tq:(h + 1) * tq]
    n_lc = tq // LANES
    for c in range(n_lc):
        pc_ref[c] = pc[:, c * LANES:(c + 1) * LANES]
    su = NSA_SLC_BLOCK // NSA_CMP_STRIDE
    x = [jnp.concatenate([pc_ref[c, pl.ds(k, n_slc, stride=su), :] for c in range(n_lc)], axis=1)
         for k in range(su)]
    jb = lax.broadcasted_iota(I32, (n_slc, tq), 0)
    prev = jnp.where(jb == 0, 0.0, pltpu.roll(x[3], 1, 0))
    imp = 2.0 * (x[0] + x[1] + x[2]) + x[3] + prev
    cur = (q0 + lax.broadcasted_iota(I32, (1, tq), 1)) >> 6
    allowed = jb <= cur
    forced = jnp.logical_or(jb == 0, jnp.logical_or(jb == cur, jb == cur - 1))
    bias = jnp.where(jnp.logical_and(allowed, forced), 0.0, NEG_BIG)
    sc = jnp.where(jnp.logical_and(allowed, jnp.logical_not(forced)), imp, -1.0)
    for _ in range(topn - 3):
        smx = jnp.max(sc, axis=0, keepdims=True)
        idx = jnp.min(jnp.where(sc == smx, jb, n_slc), axis=0, keepdims=True)
        pick = jnp.logical_and(jb == idx, smx >= 0.0)
        bias = jnp.where(pick, 0.0, bias)
        sc = jnp.where(pick, -1.0, sc)

    n_tiles = augb_ref.shape[0]
    jd = lax.div(q0, SLC_TILE)
    chosen = jnp.where(bias == 0.0, 1.0, 0.0).astype(BF16)
    per_blk = lax.dot_general(jnp.ones((SUBLANES, tq), BF16), chosen, NT_DIMS,
                              preferred_element_type=F32)
    per_shift = (SLC_TILE // NSA_SLC_BLOCK).bit_length() - 1
    in_tile = (lax.broadcasted_iota(I32, (n_slc, LANES), 0) >> per_shift) == lax.broadcasted_iota(
        I32, (n_slc, LANES), 1)
    per_tile = jnp.dot(per_blk.astype(BF16), jnp.where(in_tile, 1.0, 0.0).astype(BF16),
                       preferred_element_type=F32)
    is_past = lax.broadcasted_iota(I32, (SUBLANES, LANES), 1) < jd
    flagv_ref[...] = jnp.where(jnp.logical_and(per_tile > 0.0, is_past), 1, 0).astype(I32)
    flag_copy = pltpu.make_async_copy(flagv_ref, flags_ref, fsem)
    flag_copy.start()

    start = pl.multiple_of(jnp.maximum(q0 - NSA_WINDOW, 0), tq)
    first_w = pl.multiple_of(NSA_WINDOW - (q0 - start), tq)
    kt = kw_ref[0, pl.ds(start, WIN_KEYS), :]
    z = jnp.dot(kt, qpad, preferred_element_type=F32) + bw_ref[0, pl.ds(first_w, WIN_KEYS), :]
    mx = jnp.max(z, axis=0, keepdims=True)
    p = jnp.exp2(z - mx)
    den = jnp.maximum(jnp.sum(p, axis=0, keepdims=True), 1e-30)
    o_w = jnp.dot(vwT_ref[0, :, pl.ds(start, WIN_KEYS)], p.astype(BF16),
                  preferred_element_type=F32) / den

    flag_copy.wait()
    for i in range(TILE_LIST_LEN):
        list_ref[i] = 0

    def add_tile(t, n):
        list_ref[n] = t
        return n + flags_ref[0, t]
    n_used = lax.fori_loop(0, n_tiles, add_tile, 0)
    list_ref[DIAG_SLOT] = jd

    wc = NSA_CHAIN_LANES
    n_ch = wl // wc

    def lane_split(a):
        return [a[:, c * wc:(c + 1) * wc] for c in range(n_ch)]

    if n_slc < LANES:
        bias = jnp.concatenate([bias, jnp.zeros((LANES - n_slc, tq), F32)], axis=0)
    bias4 = jnp.concatenate([bias.astype(BF16)] * hg, axis=1)
    rhs = jnp.concatenate([rhs_top, bias4], axis=0)
    mine = (lax.broadcasted_iota(I32, (SLC_TILE, LANES), 1) >> 6) == g

    def scores(j, null):
        k0 = pl.multiple_of(j * SLC_TILE, SLC_TILE)
        kt = ks_ref[0, pl.ds(k0, SLC_TILE), :]
        lhs = jnp.concatenate([jnp.where(mine, kt, auga_ref[null]), augb_ref[j]], axis=1)
        return jnp.dot(lhs, rhs, preferred_element_type=F32)

    def values(i):
        k0 = pl.multiple_of(list_ref[i] * SLC_TILE, SLC_TILE)
        return [vsT_ref[0, :, pl.ds(k0, SLC_TILE)]] * n_ch

    def offsets(i):
        return lane_split(slope * (q0 - list_ref[i] * SLC_TILE).astype(F32))

    def past_scores(i):
        return lane_split(scores(list_ref[jnp.minimum(i, n_tiles - 1)], (i >= n_used).astype(I32)))

    t_k = jd * SLC_TILE + lax.broadcasted_iota(I32, (SLC_TILE, wl), 0)
    s_diag = lane_split(jnp.where(t_k <= t_q, scores(jd, 0), NEG_BIG))
    c_diag = offsets(DIAG_SLOT)

    def first():
        return [_softmax_stage(s_diag[c], c_diag[c], jnp.full((1, wc), NEG_BIG, F32))
                for c in range(n_ch)]

    o_s = jnp.concatenate(
        _pipelined_tiles(past_scores, values, offsets, s_ref, p_ref, first, n_ch, n_used, DIAG_SLOT, 2),
        axis=1)

    gt = g_ref[0]

    def gate_row(br):
        return jnp.concatenate([gt[br * hg + h:br * hg + h + 1] for h in range(hg)], axis=1)

    o = gate_row(0) * o_c + gate_row(1) * o_s + gate_row(2) * o_w
    o4 = jnp.concatenate([o[:, h * tq:(h + 1) * tq] for h in range(hg)], axis=0)
    o_ref[0] = o4.T.astype(BF16)


def _nsa(nqT, kcmp, vcmpT, ks, vsT, kw, vwT, gT, sl, srow, auga, augb, augc, tblc, bw):
    B, _, S = nqT.shape
    tq = NSA_TQ
    nc = kcmp.shape[1]
    n_slc = S // NSA_SLC_BLOCK
    topn = min(NSA_SLC_TOPN, n_slc)
    wl = NSA_GROUP * tq
    grid = (B, NSA_KV_HEADS, S // tq)
    return pl.pallas_call(
        functools.partial(_nsa_kernel, n_slc=n_slc, topn=topn), grid=grid,
        in_specs=[
            pl.BlockSpec((1, NSA_GROUP * HEAD_DIM, tq), lambda b, g, i: (b, g, i)),
            pl.BlockSpec((1, nc, 128), lambda b, g, i: (b, 0, 0)),
            pl.BlockSpec((1, HEAD_DIM, nc), lambda b, g, i: (b, g, 0)),
            pl.BlockSpec((1, S, 128), lambda b, g, i: (b, 0, 0)),
            pl.BlockSpec((1, HEAD_DIM, S), lambda b, g, i: (b, g, 0)),
            pl.BlockSpec((1, S, 128), lambda b, g, i: (b, 0, 0)),
            pl.BlockSpec((1, HEAD_DIM, S), lambda b, g, i: (b, g, 0)),
            pl.BlockSpec((1, 16, tq), lambda b, g, i: (b, g, i)),
            pl.BlockSpec((1, 1, wl), lambda b, g, i: (g, 0, 0)),
            pl.BlockSpec((1, 2 * HEAD_DIM, wl), lambda b, g, i: (g, 0, 0)),
            pl.BlockSpec((2, SLC_TILE, LANES), lambda b, g, i: (g, 0, 0)),
            pl.BlockSpec(augb.shape, lambda b, g, i: (0, 0, 0)),
            pl.BlockSpec((1, nc, LANES), lambda b, g, i: (g, 0, 0)),
            pl.BlockSpec(tblc.shape, lambda b, g, i: (0, 0)),
            pl.BlockSpec((1,) + bw.shape[1:], lambda b, g, i: (g, 0, 0)),
        ],
        out_specs=pl.BlockSpec((1, tq, NSA_GROUP * HEAD_DIM), lambda b, g, i: (b, i, g)),
        out_shape=jax.ShapeDtypeStruct((B, S, 512), BF16),
        scratch_shapes=[
            pltpu.VMEM((2, wl // NSA_CHAIN_LANES, SLC_TILE, NSA_CHAIN_LANES), F32),
            pltpu.VMEM((2, wl // NSA_CHAIN_LANES, SLC_TILE, NSA_CHAIN_LANES), BF16),
            pltpu.VMEM((tq // LANES, nc, LANES), F32),
            pltpu.VMEM((SUBLANES, LANES), I32),
            pltpu.SMEM((SUBLANES, LANES), I32),
            pltpu.SMEM((TILE_LIST_LEN,), I32),
            pltpu.SemaphoreType.DMA,
        ],
        compiler_params=_params(3), name="nsa",
    )(nqT, kcmp, vcmpT, ks, vsT, kw, vwT, gT, sl, srow, auga, augb, augc, tblc, bw)


def _outproj_kernel(om_ref, on_ref, x_ref, wo_ref, g_ref, wr_ref, br_ref,
                    x1_ref, hn_ref, e_ref, w_ref):
    attn = (jnp.dot(om_ref[...], wo_ref[0:512, :], preferred_element_type=F32)
            + jnp.dot(on_ref[...], wo_ref[512:1024, :], preferred_element_type=F32))
    x1 = x_ref[...] + attn
    x1_ref[...] = x1
    hn = _rmsnorm(x1, g_ref[...])
    _store_token_tiles(hn_ref, hn)
    logits = jnp.dot(hn, wr_ref[...], precision=lax.Precision.HIGHEST,
                     preferred_element_type=F32) + br_ref[...]
    tm = logits.shape[0]
    lane = lax.broadcasted_iota(I32, (tm, LANES), 1)
    sc = jnp.where(lane < N_EXPERTS, logits, -jnp.inf)
    e_out = jnp.zeros((tm, LANES), I32)
    vals = []
    for k in range(TOP_K):
        mx = jnp.max(sc, axis=1, keepdims=True)
        idx = jnp.min(jnp.where(sc == mx, lane, LANES), axis=1, keepdims=True)
        e_out = jnp.where(lane == k, idx, e_out)
        sc = jnp.where(lane == idx, -jnp.inf, sc)
        vals.append(mx)
    ex = [jnp.exp(v - vals[0]) for v in vals]
    den = ex[0] + ex[1] + ex[2] + ex[3]
    w_out = jnp.zeros((tm, LANES), F32)
    for k in range(TOP_K):
        w_out = jnp.where(lane == k, ex[k] / den, w_out)
    e_ref[...] = e_out
    w_ref[...] = w_out


def _outproj(om, on, x, wo, g, wr, br, tm):
    N, D = x.shape
    full = lambda a: pl.BlockSpec(a.shape, lambda i: (0,) * a.ndim)
    row = lambda w: pl.BlockSpec((tm, w), lambda i: (i, 0))
    return pl.pallas_call(
        _outproj_kernel, grid=(N // tm,),
        in_specs=[row(512), row(512), row(D), full(wo), full(g), full(wr), full(br)],
        out_specs=[row(D), pl.BlockSpec((tm * SUBLANES, LANES), lambda i: (i, 0)), row(LANES), row(LANES)],
        out_shape=[jax.ShapeDtypeStruct((N, D), F32), jax.ShapeDtypeStruct((N * SUBLANES, LANES), F32),
                   jax.ShapeDtypeStruct((N, LANES), I32), jax.ShapeDtypeStruct((N, LANES), F32)],
        compiler_params=_params(1), name="outproj_router",
    )(om, on, x, wo, g, wr, br)


def _rank_kernel(e_ref, rank_ref, cnt_ref, base_ref):
    i = pl.program_id(0)
    T = e_ref.shape[0]

    @pl.when(i == 0)
    def _():
        base_ref[...] = jnp.zeros(base_ref.shape, F32)

    e = e_ref[...]
    lane = lax.broadcasted_iota(I32, (T, LANES), 1)
    tril = jnp.where(lax.broadcasted_iota(I32, (T, T), 0) >= lax.broadcasted_iota(I32, (T, T), 1),
                     1.0, 0.0).astype(BF16)
    out = jnp.zeros((T, LANES), I32)
    for k in range(TOP_K):
        hit = lane == e[:, k:k + 1]
        oh = jnp.where(hit, 1.0, 0.0)
        cum = jnp.dot(tril, oh.astype(BF16), preferred_element_type=F32)
        base = base_ref[0:1, :]
        r = jnp.sum(jnp.where(hit, cum - 1.0 + base, 0.0), axis=1, keepdims=True)
        out = jnp.where(lane == k, r.astype(I32), out)
        base_ref[...] = base_ref[...] + jnp.sum(oh, axis=0, keepdims=True)
    rank_ref[...] = out
    cnt_ref[...] = base_ref[...]


def _ranks(e128):
    N = e128.shape[0]
    T = RANK_TILE
    return pl.pallas_call(
        _rank_kernel, grid=(N // T,),
        in_specs=[pl.BlockSpec((T, LANES), lambda i: (i, 0))],
        out_specs=[pl.BlockSpec((T, LANES), lambda i: (i, 0)),
                   pl.BlockSpec((8, LANES), lambda i: (0, 0))],
        out_shape=[jax.ShapeDtypeStruct((N, LANES), I32),
                   jax.ShapeDtypeStruct((8, LANES), F32)],
        scratch_shapes=[pltpu.VMEM((8, LANES), F32)],
        compiler_params=_params(1), name="route_ranks",
    )(e128)


def _row_copy(src, dst, i_src, i_dst, sem):
    return pltpu.make_async_copy(src.at[pl.ds(pl.multiple_of(i_src * SUBLANES, SUBLANES), SUBLANES)],
                                 dst.at[pl.ds(pl.multiple_of(i_dst * SUBLANES, SUBLANES), SUBLANES)], sem)


def _store_token_tiles(ref, x):
    rows = x.shape[0]
    for c in range(SUBLANES):
        ref[pl.ds(c, rows, stride=SUBLANES), :] = x[:, c * LANES:(c + 1) * LANES]


def _load_token_tiles(ref, rows):
    return jnp.concatenate([ref[pl.ds(c, rows, stride=SUBLANES), :] for c in range(SUBLANES)], axis=1)


def _dispatch_kernel(dest_hbm, hp_ref, xz_hbm, out_hbm, idx_ref, isem, sem):
    del xz_hbm
    i = pl.program_id(0)
    T = ROUTE_TILE
    cp = pltpu.make_async_copy(dest_hbm.at[i], idx_ref, isem)
    cp.start()
    cp.wait()

    def issue(t, carry):
        for k in range(TOP_K):
            _row_copy(hp_ref, out_hbm, t, idx_ref[t * TOP_K + k], sem).start(priority=k % 2)
        return carry
    lax.fori_loop(0, T, issue, 0, unroll=ROW_DMA_UNROLL)

    def drain(t, carry):
        for k in range(TOP_K):
            _row_copy(hp_ref, out_hbm, 0, 0, sem).wait()
        return carry
    lax.fori_loop(0, T, drain, 0, unroll=ROW_DMA_UNROLL)


def _dispatch(dest2, hp, xzero):
    nsteps = dest2.shape[0]
    T = ROUTE_TILE
    return pl.pallas_call(
        _dispatch_kernel, grid=(nsteps,),
        in_specs=[pl.BlockSpec(memory_space=pl.ANY),
                  pl.BlockSpec((T * SUBLANES, LANES), lambda i: (i, 0)),
                  pl.BlockSpec(memory_space=pl.ANY)],
        out_specs=pl.BlockSpec(memory_space=pl.ANY),
        out_shape=jax.ShapeDtypeStruct(xzero.shape, xzero.dtype),
        scratch_shapes=[pltpu.SMEM((T * TOP_K,), I32),
                        pltpu.SemaphoreType.DMA, pltpu.SemaphoreType.DMA],
        input_output_aliases={2: 0},
        compiler_params=_params(1), name="moe_dispatch",
    )(dest2, hp, xzero)


W_PREP_COLS = 256


def _expert_kernel(be_ref, na_ref, x_ref, wup_ref, bg_ref, bu_ref, wd_ref, bd_ref, y_ref,
                   wg_ref, wu_ref, wt_ref):
    b = pl.program_id(0)
    active = b < na_ref[0]
    new_expert = jnp.logical_or(b == 0, be_ref[b] != be_ref[jnp.maximum(b - 1, 0)])

    @pl.when(jnp.logical_and(active, new_expert))
    def _():
        half = W_PREP_COLS // 2
        n_lc = wt_ref.shape[0]

        def every_other(first):
            return jnp.concatenate(
                [wt_ref[c, pl.ds(first, half, stride=2), :] for c in range(n_lc)], axis=1).astype(BF16)

        for ch in range(wup_ref.shape[2] // W_PREP_COLS):
            panel = wup_ref[0, :, ch * W_PREP_COLS:(ch + 1) * W_PREP_COLS].T
            for c in range(n_lc):
                wt_ref[c] = panel[:, c * LANES:(c + 1) * LANES]
            wg_ref[ch * half:(ch + 1) * half, :] = every_other(0)
            wu_ref[ch * half:(ch + 1) * half, :] = every_other(1)

    @pl.when(active)
    def _():
        xb = _load_token_tiles(x_ref, MOE_ROWS).astype(BF16)
        gg = lax.dot_general(xb, wg_ref[...], NT_DIMS, preferred_element_type=F32) + bg_ref[0]
        uu = lax.dot_general(xb, wu_ref[...], NT_DIMS, preferred_element_type=F32) + bu_ref[0]
        gg = jnp.minimum(gg, SWIGLU_LIMIT)
        uu = jnp.clip(uu, -SWIGLU_LIMIT, SWIGLU_LIMIT)
        a = gg * jax.nn.sigmoid(SWIGLU_ALPHA * gg) * (uu + 1.0)
        _store_token_tiles(y_ref, jnp.dot(a.astype(BF16), wd_ref[0], preferred_element_type=F32) + bd_ref[0])

    @pl.when(jnp.logical_not(active))
    def _():
        y_ref[...] = jnp.zeros(y_ref.shape, F32)


def _experts(blk_e, n_act, xrows, w_up, bg, bu, wd, bd):
    _, D, F2 = w_up.shape
    F = F2 // 2
    assert D == SUBLANES * LANES
    P = xrows.shape[0] // SUBLANES
    n_blk = P // MOE_ROWS
    rows_spec = pl.BlockSpec((MOE_ROWS * SUBLANES, LANES), lambda b, be, na: (b, 0))
    wspec = lambda r, c: pl.BlockSpec((1, r, c), lambda b, be, na: (be[b], 0, 0))
    grid_spec = pltpu.PrefetchScalarGridSpec(
        num_scalar_prefetch=2, grid=(n_blk,),
        in_specs=[rows_spec, wspec(D, F2), wspec(1, F), wspec(1, F), wspec(F, D), wspec(1, D)],
        out_specs=rows_spec,
        scratch_shapes=[pltpu.VMEM((F, D), BF16), pltpu.VMEM((F, D), BF16),
                        pltpu.VMEM((D // LANES, W_PREP_COLS, LANES), F32)],
    )
    return pl.pallas_call(
        _expert_kernel, grid_spec=grid_spec,
        out_shape=jax.ShapeDtypeStruct(xrows.shape, F32),
        compiler_params=_params(1), name="moe_experts",
    )(blk_e, n_act, xrows, w_up, bg, bu, wd, bd)


def _combine_kernel(dest_hbm, x1_ref, w_ref, g_ref, y_hbm, o_ref, idx_ref, buf_ref, isem, sem):
    i = pl.program_id(0)
    T = ROUTE_TILE
    cp = pltpu.make_async_copy(dest_hbm.at[i], idx_ref, isem)
    cp.start()
    cp.wait()

    def issue(t, carry):
        for k in range(TOP_K):
            _row_copy(y_hbm, buf_ref.at[k], idx_ref[t * TOP_K + k], t, sem).start(priority=k % 2)
        return carry
    lax.fori_loop(0, T, issue, 0, unroll=ROW_DMA_UNROLL)

    def drain(t, carry):
        for k in range(TOP_K):
            _row_copy(y_hbm, buf_ref.at[k], 0, 0, sem).wait()
        return carry
    lax.fori_loop(0, T, drain, 0, unroll=ROW_DMA_UNROLL)

    x1 = x1_ref[...]
    w = w_ref[...]
    cols = []
    for c in range(SUBLANES):
        acc = x1[:, c * LANES:(c + 1) * LANES]
        for k in range(TOP_K):
            acc = acc + w[:, k:k + 1] * buf_ref[k, pl.ds(c, T, stride=SUBLANES), :]
        cols.append(acc)
    o_ref[...] = _rmsnorm(jnp.concatenate(cols, axis=1), g_ref[...])


def _combine(dest2, x1, w128, g, yrows):
    N, D = x1.shape
    T = ROUTE_TILE
    return pl.pallas_call(
        _combine_kernel, grid=(N // T,),
        in_specs=[pl.BlockSpec(memory_space=pl.ANY),
                  pl.BlockSpec((T, D), lambda i: (i, 0)),
                  pl.BlockSpec((T, LANES), lambda i: (i, 0)),
                  pl.BlockSpec(g.shape, lambda i: (0, 0)),
                  pl.BlockSpec(memory_space=pl.ANY)],
        out_specs=pl.BlockSpec((T, D), lambda i: (i, 0)),
        out_shape=jax.ShapeDtypeStruct((N, D), F32),
        scratch_shapes=[pltpu.SMEM((T * TOP_K,), I32),
                        pltpu.VMEM((TOP_K, T * SUBLANES, LANES), F32),
                        pltpu.SemaphoreType.DMA, pltpu.SemaphoreType.DMA],
        compiler_params=_params(1), name="moe_combine",
    )(dest2, x1, w128, g, yrows)


def _alibi_slopes():
    n = MOBA_HEADS + NSA_HEADS
    s = jnp.exp2(-8.0 * jnp.arange(1, n + 1, dtype=F32) / n)
    return s[0::2], s[1::2]


def _prep_inproj(w_in):
    hd = HEAD_DIM
    sizes = [MOBA_HEADS * hd] * 3 + [NSA_HEADS * hd] + [NSA_KV_HEADS * hd] * 6 + [NSA_BRANCHES * NSA_HEADS]
    cuts = np.cumsum([0] + sizes)
    mq, mk, mv, nq, kc, vc, ks, vs, kw, vw, ng = [w_in[:, cuts[i]:cuts[i + 1]] for i in range(11)]
    qscale = (hd ** -0.5) * LOG2E
    wr = jnp.concatenate([mk, kc, vc, ks, kw], axis=1).astype(BF16)
    wt = jnp.concatenate([mq * qscale, mv, nq * qscale, vs, vw], axis=1).T.astype(BF16)
    ngr = ng.reshape(-1, NSA_KV_HEADS, NSA_GROUP, NSA_BRANCHES).transpose(1, 3, 2, 0)
    ngr = ngr.reshape(NSA_KV_HEADS, NSA_BRANCHES * NSA_GROUP, -1)
    wg = jnp.pad(ngr, ((0, 0), (0, 16 - NSA_BRANCHES * NSA_GROUP), (0, 0))).reshape(32, -1)
    return wr, wt, wg.astype(F32)


def _prep_compress(w1, w2, pe):
    hd, half = HEAD_DIM, NSA_CMP_STRIDE
    w1r = w1.reshape(2, half, hd, hd)
    eye = jnp.eye(NSA_KV_HEADS, dtype=w1.dtype)
    w = jnp.einsum('alde,gh->lgdahe', w1r, eye).reshape(half * NSA_KV_HEADS * hd, 2 * NSA_KV_HEADS * hd)
    w2b = jnp.einsum('de,gh->gdhe', w2, eye).reshape(NSA_KV_HEADS * hd, NSA_KV_HEADS * hd)
    per = pe.reshape(2, half, 1, hd)
    pe2 = jnp.broadcast_to(per, (2, half, NSA_KV_HEADS, hd)).reshape(2, 1, half * NSA_KV_HEADS * hd)
    pe2 = jnp.broadcast_to(pe2, (2, 8, pe2.shape[2]))
    return w.astype(BF16), w2b.astype(BF16), pe2.astype(F32)


def _attention_tables(S):
    moba_sl, nsa_sl = _alibi_slopes()
    moba_sl = moba_sl * LOG2E
    nsa_sl = nsa_sl * LOG2E
    blk = MOBA_BLOCK
    moba_row = jnp.broadcast_to(moba_sl[:, None, None], (MOBA_HEADS, 1, blk))
    nb = S // blk
    nbp = -(-nb // 16) * 16
    col = jnp.arange(LANES)[None, None, :]
    tile = jnp.arange(nb + 1)[:, None, None]
    off = jnp.arange(blk, dtype=F32)[None, :, None]
    moba_aug = jnp.where(jnp.logical_and(col == tile, tile < nb), 1.0,
                         jnp.where(jnp.logical_and(col >= nbp, col < nbp + 3), off, 0.0)).astype(BF16)
    parts = jnp.stack(list(_split3(moba_sl)) + [jnp.zeros_like(moba_sl)] * 13, axis=1)
    moba_srow = jnp.broadcast_to(parts[:, :, None], (MOBA_HEADS, 16, blk)).astype(BF16)

    wl = NSA_GROUP * NSA_TQ
    n_slc = S // NSA_SLC_BLOCK
    assert n_slc <= LANES, "block-choice rows must fit the spare contraction rows"
    nsa_row = jnp.repeat(nsa_sl.reshape(NSA_KV_HEADS, NSA_GROUP), NSA_TQ, axis=1)
    hi, mid, lo = [t[:, None, :] for t in _split3(nsa_row)]
    base = ((1 - jnp.arange(NSA_KV_HEADS)) * HEAD_DIM)[:, None, None]
    rows = jnp.arange(2 * HEAD_DIM)[None, :, None]
    nsa_srow = jnp.where(rows == base, hi, jnp.where(rows == base + 1, mid, jnp.where(
        rows == base + 2, lo, jnp.where(rows == base + 3, NEG_BIG, 0.0)))).astype(BF16)
    lane = jnp.arange(LANES)[None, None, None, :]
    base4 = base[:, None]
    null = jnp.arange(2, dtype=F32)[None, :, None, None]
    koff = jnp.arange(SLC_TILE, dtype=F32)[None, None, :, None]
    nsa_auga = jnp.where(jnp.logical_and(lane >= base4, lane < base4 + 3), koff,
                         jnp.where(lane == base4 + 3, null, 0.0))
    nsa_auga = nsa_auga.reshape(2 * NSA_KV_HEADS, SLC_TILE, LANES).astype(BF16)
    per = SLC_TILE // NSA_SLC_BLOCK
    tile = jnp.arange(S // SLC_TILE)[:, None, None]
    blk_of = tile * per + jnp.arange(SLC_TILE)[None, :, None] // NSA_SLC_BLOCK
    nsa_augb = (jnp.arange(LANES)[None, None, :] == blk_of).astype(BF16)

    nc = S // NSA_CMP_STRIDE
    ci = jnp.arange(nc)[None, :, None]
    lane3 = jnp.arange(LANES)[None, None, :]
    nsa_augc = jnp.where(jnp.logical_and(lane3 >= base + 4, lane3 < base + 7), (ci >> 1).astype(F32),
                         jnp.where(jnp.logical_and(lane3 >= base + 7, lane3 < base + 10),
                                   (ci & 1).astype(F32), 0.0)).astype(BF16)
    step2 = [t[:, None, :] for t in _split3(nsa_row * (2.0 * NSA_CMP_STRIDE))]
    step1 = [t[:, None, :] for t in _split3(nsa_row * (1.0 * NSA_CMP_STRIDE))]
    for k in range(3):
        nsa_srow = jnp.where(rows == base + 4 + k, step2[k].astype(BF16),
                             jnp.where(rows == base + 7 + k, step1[k].astype(BF16), nsa_srow))
    il = jnp.tile(jnp.arange(NSA_TQ), NSA_GROUP)[None, :]
    rel = (jnp.arange(2 * nc) - nc)[:, None]
    nsa_tblc = jnp.where(rel * NSA_CMP_STRIDE + (NSA_CMP_LEN - 1) <= il, 0.0, NEG_BIG).astype(F32)
    dist = (NSA_WINDOW + il - jnp.arange(NSA_WINDOW + WIN_KEYS)[:, None])[None]
    nsa_bw = jnp.where(jnp.logical_and(dist >= 0, dist < NSA_WINDOW),
                       -nsa_row[:, None, :] * dist.astype(F32), NEG_BIG)
    return ((moba_aug, moba_srow, moba_row),
            (nsa_row.reshape(NSA_KV_HEADS, 1, wl), nsa_srow, nsa_auga, nsa_augb, nsa_augc, nsa_tblc, nsa_bw))


def _split3(x):
    hi = x.astype(BF16).astype(F32)
    mid = (x - hi).astype(BF16).astype(F32)
    lo = (x - hi - mid).astype(BF16).astype(F32)
    return hi, mid, lo


def _attention(x, attn_norm_g, w_in, cmp_pe_k, cmp_pe_v, cmp_w1_k, cmp_w2_k, cmp_w1_v, cmp_w2_v):
    B, S, D = x.shape
    wr, wt, wg = _prep_inproj(w_in)
    (mk, kc, vc, ks, kw, mqT, mvT, nqT, vsT, vwT, gT) = _inproj(
        x, attn_norm_g.reshape(1, D), wr, wt, wg, tm=512)
    moba_tabs, nsa_tabs = _attention_tables(S)
    o_moba = _moba(mqT, mk, mvT, *moba_tabs)
    wk, w2k, pek = _prep_compress(cmp_w1_k, cmp_w2_k, cmp_pe_k)
    wv, w2v, pev = _prep_compress(cmp_w1_v, cmp_w2_v, cmp_pe_v)
    nc = S // NSA_CMP_STRIDE
    kcmp, vcmpT = _compress(kc.reshape(B, nc, -1), vc.reshape(B, nc, -1), wk, wv.T, pek, pev, w2k, w2v.T)
    o_nsa = _nsa(nqT, kcmp, vcmpT, ks, vsT, kw, vwT, gT, *nsa_tabs)
    return o_moba, o_nsa


def _moe(x1, hn, e128, w128, w_up, b_up, w_down, b_down, final_norm_g):
    N, D = x1.shape
    rank128, cnt = _ranks(e128)
    counts = cnt[0, :N_EXPERTS].astype(I32)
    padded = (counts + MOE_ROWS - 1) // MOE_ROWS * MOE_ROWS
    pends = jnp.cumsum(padded)
    pstarts = pends - padded
    e4 = e128[:, :TOP_K]
    dest = pstarts[e4] + rank128[:, :TOP_K]
    dest2 = dest.reshape(N // ROUTE_TILE, ROUTE_TILE * TOP_K)
    n_blk = (N * TOP_K + N_EXPERTS * MOE_ROWS + MOE_ROWS - 1) // MOE_ROWS
    P = n_blk * MOE_ROWS
    blk_start = jnp.arange(n_blk, dtype=I32) * MOE_ROWS
    blk_e = jnp.minimum(jnp.sum((pends[None, :] <= blk_start[:, None]).astype(I32), axis=1), N_EXPERTS - 1)
    n_act = (pends[-1:] // MOE_ROWS).astype(I32)
    xrows = _dispatch(dest2, hn, jnp.zeros((P * SUBLANES, LANES), F32))
    bg = b_up[:, None, 0::2]
    bu = b_up[:, None, 1::2]
    yrows = _experts(blk_e, n_act, xrows, w_up, bg, bu, w_down.astype(BF16), b_down[:, None, :])
    return _combine(dest2, x1, w128, final_norm_g.reshape(1, D), yrows)


def kernel(x, attn_norm_g, w_in, cmp_pe_k, cmp_pe_v, cmp_w1_k, cmp_w2_k, cmp_w1_v, cmp_w2_v, w_out, ffn_norm_g, w_router, b_router, w_up, b_up, w_down, b_down, final_norm_g):
    B, S, D = x.shape
    assert attn_norm_g.shape[0] == 1, "single-layer kernel"
    o_moba, o_nsa = _attention(x, attn_norm_g[0], w_in[0], cmp_pe_k[0], cmp_pe_v[0],
                               cmp_w1_k[0], cmp_w2_k[0], cmp_w1_v[0], cmp_w2_v[0])
    N = B * S
    wr = jnp.pad(w_router[0], ((0, 0), (0, LANES - N_EXPERTS)))
    br = jnp.pad(b_router[0], (0, LANES - N_EXPERTS)).reshape(1, LANES)
    x1, hn, e128, w128 = _outproj(o_moba.reshape(N, -1), o_nsa.reshape(N, -1), x.reshape(N, D),
                                  w_out[0].astype(BF16), ffn_norm_g[0].reshape(1, D), wr, br, tm=512)
    out = _moe(x1, hn, e128, w128, w_up[0], b_up[0], w_down[0], b_down[0], final_norm_g)
    return out.reshape(B, S, D)
```

```python
import functools

import jax
import jax.numpy as jnp
import numpy as np
from jax import lax
from jax.experimental import pallas as pl
from jax.experimental.pallas import tpu as pltpu

F32 = jnp.float32
BF16 = jnp.bfloat16
I32 = jnp.int32

HEAD_DIM = 64
MOBA_HEADS = 8
NSA_HEADS = 8
NSA_KV_HEADS = 2
NSA_GROUP = NSA_HEADS // NSA_KV_HEADS
MOBA_BLOCK = 256
MOBA_TOPK = 3
NSA_CMP_LEN = 32
NSA_CMP_STRIDE = 16
NSA_SLC_BLOCK = 64
NSA_SLC_TOPN = 16
NSA_WINDOW = 512
NSA_BRANCHES = 3
N_EXPERTS = 32
TOP_K = 4
SWIGLU_LIMIT = 7.0
SWIGLU_ALPHA = 1.702
RMS_EPS = 1e-5
NEG_BIG = -1e30
LOG2E = 1.4426950408889634

LANES = 128
SUBLANES = 8
VMEM_LIMIT = 56 * 1024 * 1024

NSA_TQ = 256
NSA_CHAIN_LANES = 256
TILE_LIST_LEN = 64
DIAG_SLOT = TILE_LIST_LEN - 1
SLC_TILE = 256
WIN_KEYS = NSA_WINDOW + NSA_TQ
MOE_ROWS = 512
ROUTE_TILE = 1024
RANK_TILE = 512
ROW_DMA_UNROLL = 8

NT_DIMS = (((1,), (1,)), ((), ()))


def _params(n_grid):
    return pltpu.CompilerParams(
        dimension_semantics=("arbitrary",) * n_grid,
        vmem_limit_bytes=VMEM_LIMIT,
    )


def _rmsnorm(x, g):
    return x * lax.rsqrt(jnp.mean(x * x, axis=-1, keepdims=True) + RMS_EPS) * g


def _inproj_kernel(x_ref, g_ref, wr_ref, wt_ref, wg_ref,
                   mk_ref, kc_ref, vc_ref, ks_ref, kw_ref,
                   mqT_ref, mvT_ref, nqT_ref, vsT_ref, vwT_ref, gT_ref):
    xn = _rmsnorm(x_ref[0], g_ref[...])
    xb = xn.astype(BF16)
    yr = jnp.dot(xb, wr_ref[...], preferred_element_type=F32)
    mk_ref[0] = yr[:, 0:512].astype(BF16)
    kc_ref[0] = yr[:, 512:640].astype(BF16)
    vc_ref[0] = yr[:, 640:768].astype(BF16)
    ks_ref[0] = yr[:, 768:896].astype(BF16)
    kw_ref[0] = yr[:, 896:1024].astype(BF16)
    yt = lax.dot_general(wt_ref[...], xb, NT_DIMS, preferred_element_type=F32)
    mqT_ref[0] = yt[0:512].astype(BF16)
    mvT_ref[0] = yt[512:1024].astype(BF16)
    nqT_ref[0] = yt[1024:1536].astype(BF16)
    vsT_ref[0] = yt[1536:1664].astype(BF16)
    vwT_ref[0] = yt[1664:1792].astype(BF16)
    gl = lax.dot_general(wg_ref[...], xn, NT_DIMS, precision=lax.Precision.HIGHEST,
                         preferred_element_type=F32)
    gT_ref[0] = jax.nn.sigmoid(gl)


def _inproj(x, g, wr, wt, wg, tm):
    B, S, D = x.shape
    grid = (B, S // tm)
    row = lambda w: pl.BlockSpec((1, tm, w), lambda b, i: (b, i, 0))
    col = lambda h: pl.BlockSpec((1, h, tm), lambda b, i: (b, 0, i))
    full = lambda a: pl.BlockSpec(a.shape, lambda b, i: (0,) * a.ndim)
    out_shape = [
        jax.ShapeDtypeStruct((B, S, 512), BF16),
        jax.ShapeDtypeStruct((B, S, 128), BF16),
        jax.ShapeDtypeStruct((B, S, 128), BF16),
        jax.ShapeDtypeStruct((B, S, 128), BF16),
        jax.ShapeDtypeStruct((B, S, 128), BF16),
        jax.ShapeDtypeStruct((B, 512, S), BF16),
        jax.ShapeDtypeStruct((B, 512, S), BF16),
        jax.ShapeDtypeStruct((B, 512, S), BF16),
        jax.ShapeDtypeStruct((B, 128, S), BF16),
        jax.ShapeDtypeStruct((B, 128, S), BF16),
        jax.ShapeDtypeStruct((B, 32, S), F32),
    ]
    out_specs = [row(512), row(128), row(128), row(128), row(128),
                 col(512), col(512), col(512), col(128), col(128), col(32)]
    return pl.pallas_call(
        _inproj_kernel, grid=grid,
        in_specs=[pl.BlockSpec((1, tm, D), lambda b, i: (b, i, 0)),
                  full(g), full(wr), full(wt), full(wg)],
        out_specs=out_specs, out_shape=out_shape,
        compiler_params=_params(2), name="inproj",
    )(x, g, wr, wt, wg)


ONES_ROWS = 16


def _softmax_stage(s, c, m):
    mt = jnp.max(s, axis=0, keepdims=True) - c
    m_new = jnp.maximum(m, mt)
    alpha = jnp.exp2(m - m_new)
    p = jnp.exp2(s - (m_new + c))
    return m_new, p.astype(BF16), alpha


def _pipelined_tiles(scores, values, offsets, s_ref, p_ref, first, n_ch, n_tiles, j_first):
    chains = range(n_ch)

    def qk_into(slot, j):
        sc = scores(j)
        for c in chains:
            s_ref[slot, c] = sc[c]

    def pv_from(slot, j, alphas, accs):
        out = []
        for c, vt in zip(chains, values(j)):
            vt1 = jnp.concatenate([vt, jnp.ones((ONES_ROWS, vt.shape[1]), BF16)], axis=0)
            out.append(alphas[c] * accs[c] + jnp.dot(vt1, p_ref[slot, c], preferred_element_type=F32))
        return out

    def softmax_into(slot, j, ms):
        cs = offsets(j)
        new = [_softmax_stage(s_ref[slot, c], cs[c], ms[c]) for c in chains]
        for c in chains:
            p_ref[slot, c] = new[c][1]
        return [n[0] for n in new], [n[2] for n in new]

    qk_into(0, 0)
    first = first()
    for c in chains:
        p_ref[1, c] = first[c][1]

    def pair(i, carry):
        ms, alphas, accs, j_prev = carry
        for slot in range(2):
            t = 2 * i + slot
            qk_into(1 - slot, t + 1)
            accs = pv_from(1 - slot, j_prev, alphas, accs)
            ms, alphas = softmax_into(slot, t, ms)
            j_prev = t
        return ms, alphas, accs, j_prev

    n_q = first[0][0].shape[1]
    init = ([f[0] for f in first], [f[2] for f in first],
            [jnp.zeros((HEAD_DIM + ONES_ROWS, n_q), F32)] * n_ch, j_first)
    _, alphas, accs, j_last = lax.fori_loop(0, (n_tiles + 1) // 2, pair, init)
    accs = pv_from(1, j_last, alphas, accs)
    return [accs[c][:HEAD_DIM] / jnp.maximum(accs[c][HEAD_DIM:HEAD_DIM + 1], 1e-30) for c in chains]


def _moba_kernel(qT_ref, k_ref, vT_ref, aug_ref, srow_ref, sl_ref, o_ref,
                 kmean_ref, kparts_ref, s_ref, p_ref, *, nb, nbp, topk):
    qi = pl.program_id(2)
    blk = MOBA_BLOCK

    @pl.when(qi == 0)
    def _():
        kmean_ref[...] = jnp.zeros(kmean_ref.shape, F32)

        def body(n, carry):
            kb = k_ref[0, pl.ds(pl.multiple_of(n * blk, blk), blk), :].astype(F32)
            kmean_ref[pl.ds(n, 1), :] = jnp.mean(kb, axis=0, keepdims=True)
            return carry
        lax.fori_loop(0, nb, body, 0)
        km = kmean_ref[...]
        head = lax.broadcasted_iota(I32, km.shape, 1) >> 6
        km2 = jnp.concatenate([jnp.where(head == h, km, 0.0) for h in range(2)], axis=0)
        hi = km2.astype(BF16)
        mid = (km2 - hi.astype(F32)).astype(BF16)
        lo = (km2 - hi.astype(F32) - mid.astype(F32)).astype(BF16)
        kparts_ref[0] = hi
        kparts_ref[1] = mid
        kparts_ref[2] = lo

    qT = qT_ref[0]
    row = lax.broadcasted_iota(I32, qT.shape, 0)
    qpad = [jnp.where((row >> 6) == h, qT, jnp.zeros_like(qT)) for h in range(2)]

    gates = (jnp.dot(kparts_ref[0], qT, preferred_element_type=F32)
             + jnp.dot(kparts_ref[1], qT, preferred_element_type=F32)
             + jnp.dot(kparts_ref[2], qT, preferred_element_type=F32))
    bidx = lax.broadcasted_iota(I32, (nbp, blk), 0)
    rhs = []
    for h in range(2):
        gate = gates[h * nbp:(h + 1) * nbp]
        gsc = jnp.where(bidx < qi, gate, -jnp.inf)
        bias = jnp.full((nbp, blk), NEG_BIG, F32)
        for _ in range(topk):
            mx = jnp.max(gsc, axis=0, keepdims=True)
            idx = jnp.min(jnp.where(gsc == mx, bidx, nbp), axis=0, keepdims=True)
            pick = jnp.logical_and(bidx == idx, mx > -jnp.inf)
            bias = jnp.where(pick, 0.0, bias)
            gsc = jnp.where(pick, -jnp.inf, gsc)
        pad = jnp.zeros((2 * HEAD_DIM - nbp - 16, blk), BF16)
        rhs.append(jnp.concatenate([qpad[h], bias.astype(BF16), srow_ref[h], pad], axis=0))

    def scores(j, a):
        k0 = pl.multiple_of(j * blk, blk)
        lhs = jnp.concatenate([k_ref[0, pl.ds(k0, blk), :], aug_ref[a]], axis=1)
        return [jnp.dot(lhs, rhs[h], preferred_element_type=F32) for h in range(2)]

    def values(j):
        k0 = pl.multiple_of(j * blk, blk)
        return [vT_ref[0, h * HEAD_DIM:(h + 1) * HEAD_DIM, pl.ds(k0, blk)] for h in range(2)]

    def offsets(j):
        dq = ((qi - j) * blk).astype(F32)
        return [sl_ref[h] * dq for h in range(2)]

    ik = lax.broadcasted_iota(I32, (blk, blk), 0)
    iq = lax.broadcasted_iota(I32, (blk, blk), 1)
    s_own = [jnp.where(ik <= iq, s, NEG_BIG) for s in scores(qi, nb)]
    m0 = jnp.full((1, blk), NEG_BIG, F32)

    def first():
        return [_softmax_stage(s_own[h], jnp.zeros((1, blk), F32), m0) for h in range(2)]

    outs = _pipelined_tiles(lambda j: scores(jnp.minimum(j, nb - 1), jnp.minimum(j, nb - 1)),
                            values, offsets, s_ref, p_ref, first, 2, qi, qi)
    o_ref[0] = jnp.concatenate(outs, axis=0).T.astype(BF16)


def _moba(mqT, mk, mvT, aug, srow, sl):
    B, _, S = mqT.shape
    blk = MOBA_BLOCK
    nb = S // blk
    topk = min(MOBA_TOPK, nb)
    nbp = -(-nb // 16) * 16
    grid = (B, MOBA_HEADS // 2, nb)
    return pl.pallas_call(
        functools.partial(_moba_kernel, nb=nb, nbp=nbp, topk=topk), grid=grid,
        in_specs=[
            pl.BlockSpec((1, 128, blk), lambda b, p, i: (b, p, i)),
            pl.BlockSpec((1, S, 128), lambda b, p, i: (b, 0, p)),
            pl.BlockSpec((1, 128, S), lambda b, p, i: (b, p, 0)),
            pl.BlockSpec(aug.shape, lambda b, p, i: (0, 0, 0)),
            pl.BlockSpec((2, 16, blk), lambda b, p, i: (p, 0, 0)),
            pl.BlockSpec((2, 1, blk), lambda b, p, i: (p, 0, 0)),
        ],
        out_specs=pl.BlockSpec((1, blk, 128), lambda b, p, i: (b, i, p)),
        out_shape=jax.ShapeDtypeStruct((B, S, 512), BF16),
        scratch_shapes=[
            pltpu.VMEM((nbp, 128), F32),
            pltpu.VMEM((3, 2 * nbp, 128), BF16),
            pltpu.VMEM((2, 2, blk, blk), F32),
            pltpu.VMEM((2, 2, blk, blk), BF16),
        ],
        compiler_params=_params(3), name="moba",
    )(mqT, mk, mvT, aug, srow, sl)


def _compress_kernel(kc_ref, vc_ref, wk_ref, wvT_ref, pek_ref, pev_ref, w2k_ref, w2vT_ref,
                     kcmp_ref, vcmpT_ref):
    nc = kc_ref.shape[1]

    wk = wk_ref[...]
    ab = jnp.dot(kc_ref[0], wk, preferred_element_type=F32)
    pt = (jnp.dot(pek_ref[0], wk[:, 0:128].astype(F32), preferred_element_type=F32)
          + jnp.dot(pek_ref[1], wk[:, 128:256].astype(F32), preferred_element_type=F32))
    pre = ab[:, 0:128] + pltpu.roll(ab[:, 128:256], nc - 1, 0) + pt[0:1]
    hid = jax.nn.gelu(pre)
    kcmp_ref[0] = jnp.dot(hid.astype(BF16), w2k_ref[...], preferred_element_type=F32).astype(BF16)

    wvT = wvT_ref[...]
    abT = lax.dot_general(wvT, vc_ref[0], NT_DIMS, preferred_element_type=F32)
    ptT = (lax.dot_general(wvT[0:128].astype(F32), pev_ref[0], NT_DIMS, preferred_element_type=F32)
           + lax.dot_general(wvT[128:256].astype(F32), pev_ref[1], NT_DIMS, preferred_element_type=F32))
    preT = abT[0:128] + pltpu.roll(abT[128:256], nc - 1, 1) + ptT[:, 0:1]
    hidT = jax.nn.gelu(preT)
    vcmpT_ref[0] = jnp.dot(w2vT_ref[...], hidT.astype(BF16), preferred_element_type=F32).astype(BF16)


def _compress(kc2, vc2, wk, wvT, pek, pev, w2k, w2vT):
    B, nc, _ = kc2.shape
    full = lambda a: pl.BlockSpec(a.shape, lambda b: (0,) * a.ndim)
    blk = pl.BlockSpec((1, nc, kc2.shape[2]), lambda b: (b, 0, 0))
    return pl.pallas_call(
        _compress_kernel, grid=(B,),
        in_specs=[blk, blk, full(wk), full(wvT), full(pek), full(pev), full(w2k), full(w2vT)],
        out_specs=[pl.BlockSpec((1, nc, 128), lambda b: (b, 0, 0)),
                   pl.BlockSpec((1, 128, nc), lambda b: (b, 0, 0))],
        out_shape=[jax.ShapeDtypeStruct((B, nc, 128), BF16),
                   jax.ShapeDtypeStruct((B, 128, nc), BF16)],
        compiler_params=_params(1), name="nsa_compress",
    )(kc2, vc2, wk, wvT, pek, pev, w2k, w2vT)


def _nsa_kernel(qT_ref, kcmp_ref, vcmpT_ref, ks_ref, vsT_ref, kw_ref, vwT_ref,
                g_ref, sl_ref, srow_ref, auga_ref, augb_ref, augc_ref, tblc_ref, bw_ref, o_ref,
                s_ref, p_ref, pc_ref, flagv_ref, flags_ref, list_ref, fsem, *, n_slc, topn):
    g = pl.program_id(1)
    qi = pl.program_id(2)
    tq = NSA_TQ
    hg = NSA_GROUP
    wl = hg * tq
    q0 = qi * tq

    q4 = qT_ref[0]
    qT = jnp.concatenate([q4[h * HEAD_DIM:(h + 1) * HEAD_DIM] for h in range(hg)], axis=1)
    qT2 = jnp.concatenate([qT, qT], axis=0)
    rowi = lax.broadcasted_iota(I32, qT2.shape, 0)
    qpad = jnp.where((rowi >> 6) == g, qT2, jnp.zeros_like(qT2))
    slope = sl_ref[0]
    lane = lax.broadcasted_iota(I32, (1, wl), 1)
    t_q = q0 + (lane & (tq - 1))

    nc = kcmp_ref.shape[1]
    rhs_top = jnp.where((rowi >> 6) == g, qT2, srow_ref[0])
    mine_c = (lax.broadcasted_iota(I32, (nc, LANES), 1) >> 6) == g
    lhs_c = jnp.where(mine_c, kcmp_ref[0], augc_ref[0])
    first_c = pl.multiple_of(nc - qi * (tq // NSA_CMP_STRIDE), 8)
    z = jnp.dot(lhs_c, rhs_top, preferred_element_type=F32) + tblc_ref[pl.ds(first_c, nc), :]
    mx = jnp.max(z, axis=0, keepdims=True)
    e = jnp.exp2(z - mx)
    den = jnp.maximum(jnp.sum(e, axis=0, keepdims=True), 1e-30)
    p = e * jnp.where(t_q >= NSA_CMP_LEN - 1, 1.0 / den, 0.0)
    o_c = jnp.dot(vcmpT_ref[0], p.astype(BF16), preferred_element_type=F32)

    pc = p[:, 0:tq]
    for h in range(1, hg):
        pc = pc + p[:, h * tq:(h + 1) * tq]
    n_lc = tq // LANES
    for c in range(n_lc):
        pc_ref[c] = pc[:, c * LANES:(c + 1) * LANES]
    su = NSA_SLC_BLOCK // NSA_CMP_STRIDE
    x = [jnp.concatenate([pc_ref[c, pl.ds(k, n_slc, stride=su), :] for c in range(n_lc)], axis=1)
         for k in range(su)]
    jb = lax.broadcasted_iota(I32, (n_slc, tq), 0)
    prev = jnp.where(jb == 0, 0.0, pltpu.roll(x[3], 1, 0))
    imp = 2.0 * (x[0] + x[1] + x[2]) + x[3] + prev
    cur = (q0 + lax.broadcasted_iota(I32, (1, tq), 1)) >> 6
    allowed = jb <= cur
    forced = jnp.logical_or(jb == 0, jnp.logical_or(jb == cur, jb == cur - 1))
    bias = jnp.where(jnp.logical_and(allowed, forced), 0.0, NEG_BIG)
    sc = jnp.where(jnp.logical_and(allowed, jnp.logical_not(forced)), imp, -1.0)
    for _ in range(topn - 3):
        smx = jnp.max(sc, axis=0, keepdims=True)
        idx = jnp.min(jnp.where(sc == smx, jb, n_slc), axis=0, keepdims=True)
        pick = jnp.logical_and(jb == idx, smx >= 0.0)
        bias = jnp.where(pick, 0.0, bias)
        sc = jnp.where(pick, -1.0, sc)

    n_tiles = augb_ref.shape[0]
    jd = lax.div(q0, SLC_TILE)
    chosen = jnp.where(bias == 0.0, 1.0, 0.0).astype(BF16)
    per_blk = lax.dot_general(jnp.ones((SUBLANES, tq), BF16), chosen, NT_DIMS,
                              preferred_element_type=F32)
    per_shift = (SLC_TILE // NSA_SLC_BLOCK).bit_length() - 1
    in_tile = (lax.broadcasted_iota(I32, (n_slc, LANES), 0) >> per_shift) == lax.broadcasted_iota(
        I32, (n_slc, LANES), 1)
    per_tile = jnp.dot(per_blk.astype(BF16), jnp.where(in_tile, 1.0, 0.0).astype(BF16),
                       preferred_element_type=F32)
    is_past = lax.broadcasted_iota(I32, (SUBLANES, LANES), 1) < jd
    flagv_ref[...] = jnp.where(jnp.logical_and(per_tile > 0.0, is_past), 1, 0).astype(I32)
    flag_copy = pltpu.make_async_copy(flagv_ref, flags_ref, fsem)
    flag_copy.start()

    start = pl.multiple_of(jnp.maximum(q0 - NSA_WINDOW, 0), tq)
    first_w = pl.multiple_of(NSA_WINDOW - (q0 - start), tq)
    kt = kw_ref[0, pl.ds(start, WIN_KEYS), :]
    z = jnp.dot(kt, qpad, preferred_element_type=F32) + bw_ref[0, pl.ds(first_w, WIN_KEYS), :]
    mx = jnp.max(z, axis=0, keepdims=True)
    p = jnp.exp2(z - mx)
    den = jnp.maximum(jnp.sum(p, axis=0, keepdims=True), 1e-30)
    o_w = jnp.dot(vwT_ref[0, :, pl.ds(start, WIN_KEYS)], p.astype(BF16),
                  preferred_element_type=F32) / den

    flag_copy.wait()
    for i in range(TILE_LIST_LEN):
        list_ref[i] = 0

    def add_tile(t, n):
        list_ref[n] = t
        return n + flags_ref[0, t]
    n_used = lax.fori_loop(0, n_tiles, add_tile, 0)
    list_ref[DIAG_SLOT] = jd

    wc = NSA_CHAIN_LANES
    n_ch = wl // wc

    def lane_split(a):
        return [a[:, c * wc:(c + 1) * wc] for c in range(n_ch)]

    if n_slc < LANES:
        bias = jnp.concatenate([bias, jnp.zeros((LANES - n_slc, tq), F32)], axis=0)
    bias4 = jnp.concatenate([bias.astype(BF16)] * hg, axis=1)
    rhs = jnp.concatenate([rhs_top, bias4], axis=0)
    mine = (lax.broadcasted_iota(I32, (SLC_TILE, LANES), 1) >> 6) == g

    def scores(j, null):
        k0 = pl.multiple_of(j * SLC_TILE, SLC_TILE)
        kt = ks_ref[0, pl.ds(k0, SLC_TILE), :]
        lhs = jnp.concatenate([jnp.where(mine, kt, auga_ref[null]), augb_ref[j]], axis=1)
        return jnp.dot(lhs, rhs, preferred_element_type=F32)

    def values(i):
        k0 = pl.multiple_of(list_ref[i] * SLC_TILE, SLC_TILE)
        return [vsT_ref[0, :, pl.ds(k0, SLC_TILE)]] * n_ch

    def offsets(i):
        return lane_split(slope * (q0 - list_ref[i] * SLC_TILE).astype(F32))

    def past_scores(i):
        return lane_split(scores(list_ref[jnp.minimum(i, n_tiles - 1)], (i >= n_used).astype(I32)))

    t_k = jd * SLC_TILE + lax.broadcasted_iota(I32, (SLC_TILE, wl), 0)
    s_diag = lane_split(jnp.where(t_k <= t_q, scores(jd, 0), NEG_BIG))
    c_diag = offsets(DIAG_SLOT)

    def first():
        return [_softmax_stage(s_diag[c], c_diag[c], jnp.full((1, wc), NEG_BIG, F32))
                for c in range(n_ch)]

    o_s = jnp.concatenate(
        _pipelined_tiles(past_scores, values, offsets, s_ref, p_ref, first, n_ch, n_used, DIAG_SLOT),
        axis=1)

    gt = g_ref[0]

    def gate_row(br):
        return jnp.concatenate([gt[br * hg + h:br * hg + h + 1] for h in range(hg)], axis=1)

    o = gate_row(0) * o_c + gate_row(1) * o_s + gate_row(2) * o_w
    o4 = jnp.concatenate([o[:, h * tq:(h + 1) * tq] for h in range(hg)], axis=0)
    o_ref[0] = o4.T.astype(BF16)


def _nsa(nqT, kcmp, vcmpT, ks, vsT, kw, vwT, gT, sl, srow, auga, augb, augc, tblc, bw):
    B, _, S = nqT.shape
    tq = NSA_TQ
    nc = kcmp.shape[1]
    n_slc = S // NSA_SLC_BLOCK
    topn = min(NSA_SLC_TOPN, n_slc)
    wl = NSA_GROUP * tq
    grid = (B, NSA_KV_HEADS, S // tq)
    return pl.pallas_call(
        functools.partial(_nsa_kernel, n_slc=n_slc, topn=topn), grid=grid,
        in_specs=[
            pl.BlockSpec((1, NSA_GROUP * HEAD_DIM, tq), lambda b, g, i: (b, g, i)),
            pl.BlockSpec((1, nc, 128), lambda b, g, i: (b, 0, 0)),
            pl.BlockSpec((1, HEAD_DIM, nc), lambda b, g, i: (b, g, 0)),
            pl.BlockSpec((1, S, 128), lambda b, g, i: (b, 0, 0)),
            pl.BlockSpec((1, HEAD_DIM, S), lambda b, g, i: (b, g, 0)),
            pl.BlockSpec((1, S, 128), lambda b, g, i: (b, 0, 0)),
            pl.BlockSpec((1, HEAD_DIM, S), lambda b, g, i: (b, g, 0)),
            pl.BlockSpec((1, 16, tq), lambda b, g, i: (b, g, i)),
            pl.BlockSpec((1, 1, wl), lambda b, g, i: (g, 0, 0)),
            pl.BlockSpec((1, 2 * HEAD_DIM, wl), lambda b, g, i: (g, 0, 0)),
            pl.BlockSpec((2, SLC_TILE, LANES), lambda b, g, i: (g, 0, 0)),
            pl.BlockSpec(augb.shape, lambda b, g, i: (0, 0, 0)),
            pl.BlockSpec((1, nc, LANES), lambda b, g, i: (g, 0, 0)),
            pl.BlockSpec(tblc.shape, lambda b, g, i: (0, 0)),
            pl.BlockSpec((1,) + bw.shape[1:], lambda b, g, i: (g, 0, 0)),
        ],
        out_specs=pl.BlockSpec((1, tq, NSA_GROUP * HEAD_DIM), lambda b, g, i: (b, i, g)),
        out_shape=jax.ShapeDtypeStruct((B, S, 512), BF16),
        scratch_shapes=[
            pltpu.VMEM((2, wl // NSA_CHAIN_LANES, SLC_TILE, NSA_CHAIN_LANES), F32),
            pltpu.VMEM((2, wl // NSA_CHAIN_LANES, SLC_TILE, NSA_CHAIN_LANES), BF16),
            pltpu.VMEM((tq // LANES, nc, LANES), F32),
            pltpu.VMEM((SUBLANES, LANES), I32),
            pltpu.SMEM((SUBLANES, LANES), I32),
            pltpu.SMEM((TILE_LIST_LEN,), I32),
            pltpu.SemaphoreType.DMA,
        ],
        compiler_params=_params(3), name="nsa",
    )(nqT, kcmp, vcmpT, ks, vsT, kw, vwT, gT, sl, srow, auga, augb, augc, tblc, bw)


def _outproj_kernel(om_ref, on_ref, x_ref, wo_ref, g_ref, wr_ref, br_ref,
                    x1_ref, hn_ref, e_ref, w_ref):
    attn = (jnp.dot(om_ref[...], wo_ref[0:512, :], preferred_element_type=F32)
            + jnp.dot(on_ref[...], wo_ref[512:1024, :], preferred_element_type=F32))
    x1 = x_ref[...] + attn
    x1_ref[...] = x1
    hn = _rmsnorm(x1, g_ref[...])
    _store_token_tiles(hn_ref, hn)
    logits = jnp.dot(hn, wr_ref[...], precision=lax.Precision.HIGHEST,
                     preferred_element_type=F32) + br_ref[...]
    tm = logits.shape[0]
    lane = lax.broadcasted_iota(I32, (tm, LANES), 1)
    sc = jnp.where(lane < N_EXPERTS, logits, -jnp.inf)
    e_out = jnp.zeros((tm, LANES), I32)
    vals = []
    for k in range(TOP_K):
        mx = jnp.max(sc, axis=1, keepdims=True)
        idx = jnp.min(jnp.where(sc == mx, lane, LANES), axis=1, keepdims=True)
        e_out = jnp.where(lane == k, idx, e_out)
        sc = jnp.where(lane == idx, -jnp.inf, sc)
        vals.append(mx)
    ex = [jnp.exp(v - vals[0]) for v in vals]
    den = ex[0] + ex[1] + ex[2] + ex[3]
    w_out = jnp.zeros((tm, LANES), F32)
    for k in range(TOP_K):
        w_out = jnp.where(lane == k, ex[k] / den, w_out)
    e_ref[...] = e_out
    w_ref[...] = w_out


def _outproj(om, on, x, wo, g, wr, br, tm):
    N, D = x.shape
    full = lambda a: pl.BlockSpec(a.shape, lambda i: (0,) * a.ndim)
    row = lambda w: pl.BlockSpec((tm, w), lambda i: (i, 0))
    return pl.pallas_call(
        _outproj_kernel, grid=(N // tm,),
        in_specs=[row(512), row(512), row(D), full(wo), full(g), full(wr), full(br)],
        out_specs=[row(D), pl.BlockSpec((tm * SUBLANES, LANES), lambda i: (i, 0)), row(LANES), row(LANES)],
        out_shape=[jax.ShapeDtypeStruct((N, D), F32), jax.ShapeDtypeStruct((N * SUBLANES, LANES), F32),
                   jax.ShapeDtypeStruct((N, LANES), I32), jax.ShapeDtypeStruct((N, LANES), F32)],
        compiler_params=_params(1), name="outproj_router",
    )(om, on, x, wo, g, wr, br)


def _rank_kernel(e_ref, rank_ref, cnt_ref, base_ref):
    i = pl.program_id(0)
    T = e_ref.shape[0]

    @pl.when(i == 0)
    def _():
        base_ref[...] = jnp.zeros(base_ref.shape, F32)

    e = e_ref[...]
    lane = lax.broadcasted_iota(I32, (T, LANES), 1)
    tril = jnp.where(lax.broadcasted_iota(I32, (T, T), 0) >= lax.broadcasted_iota(I32, (T, T), 1),
                     1.0, 0.0).astype(BF16)
    out = jnp.zeros((T, LANES), I32)
    for k in range(TOP_K):
        hit = lane == e[:, k:k + 1]
        oh = jnp.where(hit, 1.0, 0.0)
        cum = jnp.dot(tril, oh.astype(BF16), preferred_element_type=F32)
        base = base_ref[0:1, :]
        r = jnp.sum(jnp.where(hit, cum - 1.0 + base, 0.0), axis=1, keepdims=True)
        out = jnp.where(lane == k, r.astype(I32), out)
        base_ref[...] = base_ref[...] + jnp.sum(oh, axis=0, keepdims=True)
    rank_ref[...] = out
    cnt_ref[...] = base_ref[...]


def _ranks(e128):
    N = e128.shape[0]
    T = RANK_TILE
    return pl.pallas_call(
        _rank_kernel, grid=(N // T,),
        in_specs=[pl.BlockSpec((T, LANES), lambda i: (i, 0))],
        out_specs=[pl.BlockSpec((T, LANES), lambda i: (i, 0)),
                   pl.BlockSpec((8, LANES), lambda i: (0, 0))],
        out_shape=[jax.ShapeDtypeStruct((N, LANES), I32),
                   jax.ShapeDtypeStruct((8, LANES), F32)],
        scratch_shapes=[pltpu.VMEM((8, LANES), F32)],
        compiler_params=_params(1), name="route_ranks",
    )(e128)


def _row_copy(src, dst, i_src, i_dst, sem):
    return pltpu.make_async_copy(src.at[pl.ds(pl.multiple_of(i_src * SUBLANES, SUBLANES), SUBLANES)],
                                 dst.at[pl.ds(pl.multiple_of(i_dst * SUBLANES, SUBLANES), SUBLANES)], sem)


def _store_token_tiles(ref, x):
    rows = x.shape[0]
    for c in range(SUBLANES):
        ref[pl.ds(c, rows, stride=SUBLANES), :] = x[:, c * LANES:(c + 1) * LANES]


def _load_token_tiles(ref, rows):
    return jnp.concatenate([ref[pl.ds(c, rows, stride=SUBLANES), :] for c in range(SUBLANES)], axis=1)


def _dispatch_kernel(gap_lo_ref, gap_hi_ref, dest_hbm, hp_ref, out_hbm, idx_ref, zero_ref, isem, sem, zsem):
    i = pl.program_id(0)
    T = ROUTE_TILE

    @pl.when(i == 0)
    def _():
        zero_ref[...] = jnp.zeros(zero_ref.shape, F32)

        def for_each_gap_row(action):
            def gap(e, carry):
                def row(r, c):
                    action(_row_copy(zero_ref, out_hbm, 0, r, zsem))
                    return c
                return lax.fori_loop(gap_lo_ref[e], gap_hi_ref[e], row, carry)
            lax.fori_loop(0, gap_lo_ref.shape[0], gap, 0)
        for_each_gap_row(lambda copy: copy.start())
        for_each_gap_row(lambda copy: copy.wait())

    cp = pltpu.make_async_copy(dest_hbm.at[i], idx_ref, isem)
    cp.start()
    cp.wait()

    def issue(t, carry):
        for k in range(TOP_K):
            _row_copy(hp_ref, out_hbm, t, idx_ref[t * TOP_K + k], sem).start(priority=k % 2)
        return carry
    lax.fori_loop(0, T, issue, 0, unroll=ROW_DMA_UNROLL)

    def drain(t, carry):
        for k in range(TOP_K):
            _row_copy(hp_ref, out_hbm, 0, 0, sem).wait()
        return carry
    lax.fori_loop(0, T, drain, 0, unroll=ROW_DMA_UNROLL)


def _dispatch(gap_lo, gap_hi, dest2, hp, n_rows):
    nsteps = dest2.shape[0]
    T = ROUTE_TILE
    grid_spec = pltpu.PrefetchScalarGridSpec(
        num_scalar_prefetch=2, grid=(nsteps,),
        in_specs=[pl.BlockSpec(memory_space=pl.ANY),
                  pl.BlockSpec((T * SUBLANES, LANES), lambda i, lo, hi: (i, 0))],
        out_specs=pl.BlockSpec(memory_space=pl.ANY),
        scratch_shapes=[pltpu.SMEM((T * TOP_K,), I32), pltpu.VMEM((SUBLANES, LANES), F32),
                        pltpu.SemaphoreType.DMA, pltpu.SemaphoreType.DMA, pltpu.SemaphoreType.DMA],
    )
    return pl.pallas_call(
        _dispatch_kernel, grid_spec=grid_spec,
        out_shape=jax.ShapeDtypeStruct((n_rows * SUBLANES, LANES), F32),
        compiler_params=_params(1), name="moe_dispatch",
    )(gap_lo, gap_hi, dest2, hp)


W_PREP_COLS = 256


def _expert_kernel(be_ref, na_ref, x_ref, wup_ref, bg_ref, bu_ref, wd_ref, bd_ref, y_ref,
                   wg_ref, wu_ref, wt_ref):
    b = pl.program_id(0)
    active = b < na_ref[0]
    new_expert = jnp.logical_or(b == 0, be_ref[b] != be_ref[jnp.maximum(b - 1, 0)])

    @pl.when(jnp.logical_and(active, new_expert))
    def _():
        half = W_PREP_COLS // 2
        n_lc = wt_ref.shape[0]

        def every_other(first):
            return jnp.concatenate(
                [wt_ref[c, pl.ds(first, half, stride=2), :] for c in range(n_lc)], axis=1).astype(BF16)

        for ch in range(wup_ref.shape[2] // W_PREP_COLS):
            panel = wup_ref[0, :, ch * W_PREP_COLS:(ch + 1) * W_PREP_COLS].T
            for c in range(n_lc):
                wt_ref[c] = panel[:, c * LANES:(c + 1) * LANES]
            wg_ref[ch * half:(ch + 1) * half, :] = every_other(0)
            wu_ref[ch * half:(ch + 1) * half, :] = every_other(1)

    @pl.when(active)
    def _():
        xb = _load_token_tiles(x_ref, MOE_ROWS).astype(BF16)
        gg = lax.dot_general(xb, wg_ref[...], NT_DIMS, preferred_element_type=F32) + bg_ref[0]
        uu = lax.dot_general(xb, wu_ref[...], NT_DIMS, preferred_element_type=F32) + bu_ref[0]
        gg = jnp.minimum(gg, SWIGLU_LIMIT)
        uu = jnp.clip(uu, -SWIGLU_LIMIT, SWIGLU_LIMIT)
        a = gg * jax.nn.sigmoid(SWIGLU_ALPHA * gg) * (uu + 1.0)
        _store_token_tiles(y_ref, jnp.dot(a.astype(BF16), wd_ref[0], preferred_element_type=F32) + bd_ref[0])

    @pl.when(jnp.logical_not(active))
    def _():
        y_ref[...] = jnp.zeros(y_ref.shape, F32)


def _experts(blk_e, n_act, xrows, w_up, bg, bu, wd, bd):
    _, D, F2 = w_up.shape
    F = F2 // 2
    assert D == SUBLANES * LANES
    P = xrows.shape[0] // SUBLANES
    n_blk = P // MOE_ROWS
    rows_spec = pl.BlockSpec((MOE_ROWS * SUBLANES, LANES), lambda b, be, na: (b, 0))
    wspec = lambda r, c: pl.BlockSpec((1, r, c), lambda b, be, na: (be[b], 0, 0))
    grid_spec = pltpu.PrefetchScalarGridSpec(
        num_scalar_prefetch=2, grid=(n_blk,),
        in_specs=[rows_spec, wspec(D, F2), wspec(1, F), wspec(1, F), wspec(F, D), wspec(1, D)],
        out_specs=rows_spec,
        scratch_shapes=[pltpu.VMEM((F, D), BF16), pltpu.VMEM((F, D), BF16),
                        pltpu.VMEM((D // LANES, W_PREP_COLS, LANES), F32)],
    )
    return pl.pallas_call(
        _expert_kernel, grid_spec=grid_spec,
        out_shape=jax.ShapeDtypeStruct(xrows.shape, F32),
        compiler_params=_params(1), name="moe_experts",
    )(blk_e, n_act, xrows, w_up, bg, bu, wd, bd)


def _combine_kernel(dest_hbm, x1_ref, w_ref, g_ref, y_hbm, o_ref, idx_ref, buf_ref, isem, sem):
    i = pl.program_id(0)
    T = ROUTE_TILE
    cp = pltpu.make_async_copy(dest_hbm.at[i], idx_ref, isem)
    cp.start()
    cp.wait()

    def issue(t, carry):
        for k in range(TOP_K):
            _row_copy(y_hbm, buf_ref.at[k], idx_ref[t * TOP_K + k], t, sem).start(priority=k % 2)
        return carry
    lax.fori_loop(0, T, issue, 0, unroll=ROW_DMA_UNROLL)

    def drain(t, carry):
        for k in range(TOP_K):
            _row_copy(y_hbm, buf_ref.at[k], 0, 0, sem).wait()
        return carry
    lax.fori_loop(0, T, drain, 0, unroll=ROW_DMA_UNROLL)

    x1 = x1_ref[...]
    w = w_ref[...]
    cols = []
    for c in range(SUBLANES):
        acc = x1[:, c * LANES:(c + 1) * LANES]
        for k in range(TOP_K):
            acc = acc + w[:, k:k + 1] * buf_ref[k, pl.ds(c, T, stride=SUBLANES), :]
        cols.append(acc)
    o_ref[...] = _rmsnorm(jnp.concatenate(cols, axis=1), g_ref[...])


def _combine(dest2, x1, w128, g, yrows):
    N, D = x1.shape
    T = ROUTE_TILE
    return pl.pallas_call(
        _combine_kernel, grid=(N // T,),
        in_specs=[pl.BlockSpec(memory_space=pl.ANY),
                  pl.BlockSpec((T, D), lambda i: (i, 0)),
                  pl.BlockSpec((T, LANES), lambda i: (i, 0)),
                  pl.BlockSpec(g.shape, lambda i: (0, 0)),
                  pl.BlockSpec(memory_space=pl.ANY)],
        out_specs=pl.BlockSpec((T, D), lambda i: (i, 0)),
        out_shape=jax.ShapeDtypeStruct((N, D), F32),
        scratch_shapes=[pltpu.SMEM((T * TOP_K,), I32),
                        pltpu.VMEM((TOP_K, T * SUBLANES, LANES), F32),
                        pltpu.SemaphoreType.DMA, pltpu.SemaphoreType.DMA],
        compiler_params=_params(1), name="moe_combine",
    )(dest2, x1, w128, g, yrows)


def _alibi_slopes():
    n = MOBA_HEADS + NSA_HEADS
    s = jnp.exp2(-8.0 * jnp.arange(1, n + 1, dtype=F32) / n)
    return s[0::2], s[1::2]


def _prep_inproj(w_in):
    hd = HEAD_DIM
    sizes = [MOBA_HEADS * hd] * 3 + [NSA_HEADS * hd] + [NSA_KV_HEADS * hd] * 6 + [NSA_BRANCHES * NSA_HEADS]
    cuts = np.cumsum([0] + sizes)
    mq, mk, mv, nq, kc, vc, ks, vs, kw, vw, ng = [w_in[:, cuts[i]:cuts[i + 1]] for i in range(11)]
    qscale = (hd ** -0.5) * LOG2E
    wr = jnp.concatenate([mk, kc, vc, ks, kw], axis=1).astype(BF16)
    wt = jnp.concatenate([mq * qscale, mv, nq * qscale, vs, vw], axis=1).T.astype(BF16)
    ngr = ng.reshape(-1, NSA_KV_HEADS, NSA_GROUP, NSA_BRANCHES).transpose(1, 3, 2, 0)
    ngr = ngr.reshape(NSA_KV_HEADS, NSA_BRANCHES * NSA_GROUP, -1)
    wg = jnp.pad(ngr, ((0, 0), (0, 16 - NSA_BRANCHES * NSA_GROUP), (0, 0))).reshape(32, -1)
    return wr, wt, wg.astype(F32)


def _prep_compress(w1, w2, pe):
    hd, half = HEAD_DIM, NSA_CMP_STRIDE
    w1r = w1.reshape(2, half, hd, hd)
    eye = jnp.eye(NSA_KV_HEADS, dtype=w1.dtype)
    w = jnp.einsum('alde,gh->lgdahe', w1r, eye).reshape(half * NSA_KV_HEADS * hd, 2 * NSA_KV_HEADS * hd)
    w2b = jnp.einsum('de,gh->gdhe', w2, eye).reshape(NSA_KV_HEADS * hd, NSA_KV_HEADS * hd)
    per = pe.reshape(2, half, 1, hd)
    pe2 = jnp.broadcast_to(per, (2, half, NSA_KV_HEADS, hd)).reshape(2, 1, half * NSA_KV_HEADS * hd)
    pe2 = jnp.broadcast_to(pe2, (2, 8, pe2.shape[2]))
    return w.astype(BF16), w2b.astype(BF16), pe2.astype(F32)


def _attention_tables(S):
    moba_sl, nsa_sl = _alibi_slopes()
    moba_sl = moba_sl * LOG2E
    nsa_sl = nsa_sl * LOG2E
    blk = MOBA_BLOCK
    moba_row = jnp.broadcast_to(moba_sl[:, None, None], (MOBA_HEADS, 1, blk))
    nb = S // blk
    nbp = -(-nb // 16) * 16
    col = jnp.arange(LANES)[None, None, :]
    tile = jnp.arange(nb + 1)[:, None, None]
    off = jnp.arange(blk, dtype=F32)[None, :, None]
    moba_aug = jnp.where(jnp.logical_and(col == tile, tile < nb), 1.0,
                         jnp.where(jnp.logical_and(col >= nbp, col < nbp + 3), off, 0.0)).astype(BF16)
    parts = jnp.stack(list(_split3(moba_sl)) + [jnp.zeros_like(moba_sl)] * 13, axis=1)
    moba_srow = jnp.broadcast_to(parts[:, :, None], (MOBA_HEADS, 16, blk)).astype(BF16)

    wl = NSA_GROUP * NSA_TQ
    n_slc = S // NSA_SLC_BLOCK
    assert n_slc <= LANES, "block-choice rows must fit the spare contraction rows"
    nsa_row = jnp.repeat(nsa_sl.reshape(NSA_KV_HEADS, NSA_GROUP), NSA_TQ, axis=1)
    hi, mid, lo = [t[:, None, :] for t in _split3(nsa_row)]
    base = ((1 - jnp.arange(NSA_KV_HEADS)) * HEAD_DIM)[:, None, None]
    rows = jnp.arange(2 * HEAD_DIM)[None, :, None]
    nsa_srow = jnp.where(rows == base, hi, jnp.where(rows == base + 1, mid, jnp.where(
        rows == base + 2, lo, jnp.where(rows == base + 3, NEG_BIG, 0.0)))).astype(BF16)
    lane = jnp.arange(LANES)[None, None, None, :]
    base4 = base[:, None]
    null = jnp.arange(2, dtype=F32)[None, :, None, None]
    koff = jnp.arange(SLC_TILE, dtype=F32)[None, None, :, None]
    nsa_auga = jnp.where(jnp.logical_and(lane >= base4, lane < base4 + 3), koff,
                         jnp.where(lane == base4 + 3, null, 0.0))
    nsa_auga = nsa_auga.reshape(2 * NSA_KV_HEADS, SLC_TILE, LANES).astype(BF16)
    per = SLC_TILE // NSA_SLC_BLOCK
    tile = jnp.arange(S // SLC_TILE)[:, None, None]
    blk_of = tile * per + jnp.arange(SLC_TILE)[None, :, None] // NSA_SLC_BLOCK
    nsa_augb = (jnp.arange(LANES)[None, None, :] == blk_of).astype(BF16)

    nc = S // NSA_CMP_STRIDE
    ci = jnp.arange(nc)[None, :, None]
    lane3 = jnp.arange(LANES)[None, None, :]
    nsa_augc = jnp.where(jnp.logical_and(lane3 >= base + 4, lane3 < base + 7), (ci >> 1).astype(F32),
                         jnp.where(jnp.logical_and(lane3 >= base + 7, lane3 < base + 10),
                                   (ci & 1).astype(F32), 0.0)).astype(BF16)
    step2 = [t[:, None, :] for t in _split3(nsa_row * (2.0 * NSA_CMP_STRIDE))]
    step1 = [t[:, None, :] for t in _split3(nsa_row * (1.0 * NSA_CMP_STRIDE))]
    for k in range(3):
        nsa_srow = jnp.where(rows == base + 4 + k, step2[k].astype(BF16),
                             jnp.where(rows == base + 7 + k, step1[k].astype(BF16), nsa_srow))
    il = jnp.tile(jnp.arange(NSA_TQ), NSA_GROUP)[None, :]
    rel = (jnp.arange(2 * nc) - nc)[:, None]
    nsa_tblc = jnp.where(rel * NSA_CMP_STRIDE + (NSA_CMP_LEN - 1) <= il, 0.0, NEG_BIG).astype(F32)
    dist = (NSA_WINDOW + il - jnp.arange(NSA_WINDOW + WIN_KEYS)[:, None])[None]
    nsa_bw = jnp.where(jnp.logical_and(dist >= 0, dist < NSA_WINDOW),
                       -nsa_row[:, None, :] * dist.astype(F32), NEG_BIG)
    return ((moba_aug, moba_srow, moba_row),
            (nsa_row.reshape(NSA_KV_HEADS, 1, wl), nsa_srow, nsa_auga, nsa_augb, nsa_augc, nsa_tblc, nsa_bw))


def _split3(x):
    hi = x.astype(BF16).astype(F32)
    mid = (x - hi).astype(BF16).astype(F32)
    lo = (x - hi - mid).astype(BF16).astype(F32)
    return hi, mid, lo


def _attention(x, attn_norm_g, w_in, cmp_pe_k, cmp_pe_v, cmp_w1_k, cmp_w2_k, cmp_w1_v, cmp_w2_v):
    B, S, D = x.shape
    wr, wt, wg = _prep_inproj(w_in)
    (mk, kc, vc, ks, kw, mqT, mvT, nqT, vsT, vwT, gT) = _inproj(
        x, attn_norm_g.reshape(1, D), wr, wt, wg, tm=512)
    moba_tabs, nsa_tabs = _attention_tables(S)
    o_moba = _moba(mqT, mk, mvT, *moba_tabs)
    wk, w2k, pek = _prep_compress(cmp_w1_k, cmp_w2_k, cmp_pe_k)
    wv, w2v, pev = _prep_compress(cmp_w1_v, cmp_w2_v, cmp_pe_v)
    nc = S // NSA_CMP_STRIDE
    kcmp, vcmpT = _compress(kc.reshape(B, nc, -1), vc.reshape(B, nc, -1), wk, wv.T, pek, pev, w2k, w2v.T)
    o_nsa = _nsa(nqT, kcmp, vcmpT, ks, vsT, kw, vwT, gT, *nsa_tabs)
    return o_moba, o_nsa


def _moe(x1, hn, e128, w128, w_up, b_up, w_down, b_down, final_norm_g):
    N, D = x1.shape
    rank128, cnt = _ranks(e128)
    counts = cnt[0, :N_EXPERTS].astype(I32)
    padded = (counts + MOE_ROWS - 1) // MOE_ROWS * MOE_ROWS
    pends = jnp.cumsum(padded)
    pstarts = pends - padded
    e4 = e128[:, :TOP_K]
    dest = pstarts[e4] + rank128[:, :TOP_K]
    dest2 = dest.reshape(N // ROUTE_TILE, ROUTE_TILE * TOP_K)
    n_blk = (N * TOP_K + N_EXPERTS * MOE_ROWS + MOE_ROWS - 1) // MOE_ROWS
    P = n_blk * MOE_ROWS
    blk_start = jnp.arange(n_blk, dtype=I32) * MOE_ROWS
    blk_e = jnp.minimum(jnp.sum((pends[None, :] <= blk_start[:, None]).astype(I32), axis=1), N_EXPERTS - 1)
    n_act = (pends[-1:] // MOE_ROWS).astype(I32)
    gap_lo = jnp.concatenate([pstarts + counts, pends[-1:]]).astype(I32)
    gap_hi = jnp.concatenate([pends, jnp.full((1,), P, I32)]).astype(I32)
    xrows = _dispatch(gap_lo, gap_hi, dest2, hn, P)
    bg = b_up[:, None, 0::2]
    bu = b_up[:, None, 1::2]
    yrows = _experts(blk_e, n_act, xrows, w_up, bg, bu, w_down.astype(BF16), b_down[:, None, :])
    return _combine(dest2, x1, w128, final_norm_g.reshape(1, D), yrows)


def kernel(x, attn_norm_g, w_in, cmp_pe_k, cmp_pe_v, cmp_w1_k, cmp_w2_k, cmp_w1_v, cmp_w2_v, w_out, ffn_norm_g, w_router, b_router, w_up, b_up, w_down, b_down, final_norm_g):
    B, S, D = x.shape
    assert attn_norm_g.shape[0] == 1, "single-layer kernel"
    o_moba, o_nsa = _attention(x, attn_norm_g[0], w_in[0], cmp_pe_k[0], cmp_pe_v[0],
                               cmp_w1_k[0], cmp_w2_k[0], cmp_w1_v[0], cmp_w2_v[0])
    N = B * S
    wr = jnp.pad(w_router[0], ((0, 0), (0, LANES - N_EXPERTS)))
    br = jnp.pad(b_router[0], (0, LANES - N_EXPERTS)).reshape(1, LANES)
    x1, hn, e128, w128 = _outproj(o_moba.reshape(N, -1), o_nsa.reshape(N, -1), x.reshape(N, D),
                                  w_out[0].astype(BF16), ffn_norm_g[0].reshape(1, D), wr, br, tm=512)
    out = _moe(x1, hn, e128, w128, w_up[0], b_up[0], w_down[0], b_down[0], final_norm_g)
    return out.reshape(B, S, D)
```

```python
import functools

import jax
import jax.numpy as jnp
import numpy as np
from jax import lax
from jax.experimental import pallas as pl
from jax.experimental.pallas import tpu as pltpu

F32 = jnp.float32
BF16 = jnp.bfloat16
I32 = jnp.int32

HEAD_DIM = 64
MOBA_HEADS = 8
NSA_HEADS = 8
NSA_KV_HEADS = 2
NSA_GROUP = NSA_HEADS // NSA_KV_HEADS
MOBA_BLOCK = 256
MOBA_TOPK = 3
NSA_CMP_LEN = 32
NSA_CMP_STRIDE = 16
NSA_SLC_BLOCK = 64
NSA_SLC_TOPN = 16
NSA_WINDOW = 512
NSA_BRANCHES = 3
N_EXPERTS = 32
TOP_K = 4
SWIGLU_LIMIT = 7.0
SWIGLU_ALPHA = 1.702
RMS_EPS = 1e-5
NEG_BIG = -1e30
LOG2E = 1.4426950408889634

LANES = 128
SUBLANES = 8
VMEM_LIMIT = 56 * 1024 * 1024

NSA_TQ = 256
NSA_CHAIN_LANES = 256
TILE_LIST_LEN = 64
DIAG_SLOT = TILE_LIST_LEN - 1
SLC_TILE = 256
WIN_KEYS = NSA_WINDOW + NSA_TQ
MOE_ROWS = 512
ROUTE_TILE = 1024
RANK_TILE = 512
ROW_DMA_UNROLL = 8

NT_DIMS = (((1,), (1,)), ((), ()))


def _params(n_grid):
    return pltpu.CompilerParams(
        dimension_semantics=("arbitrary",) * n_grid,
        vmem_limit_bytes=VMEM_LIMIT,
    )


def _rmsnorm(x, g):
    return x * lax.rsqrt(jnp.mean(x * x, axis=-1, keepdims=True) + RMS_EPS) * g


def _inproj_kernel(x_ref, g_ref, wr_ref, wt_ref, wg_ref,
                   mk_ref, kc_ref, vc_ref, ks_ref, kw_ref,
                   mqT_ref, mvT_ref, nqT_ref, vsT_ref, vwT_ref, gT_ref):
    xn = _rmsnorm(x_ref[0], g_ref[...])
    xb = xn.astype(BF16)
    yr = jnp.dot(xb, wr_ref[...], preferred_element_type=F32)
    mk_ref[0] = yr[:, 0:512].astype(BF16)
    kc_ref[0] = yr[:, 512:640].astype(BF16)
    vc_ref[0] = yr[:, 640:768].astype(BF16)
    ks_ref[0] = yr[:, 768:896].astype(BF16)
    kw_ref[0] = yr[:, 896:1024].astype(BF16)
    yt = lax.dot_general(wt_ref[...], xb, NT_DIMS, preferred_element_type=F32)
    mqT_ref[0] = yt[0:512].astype(BF16)
    mvT_ref[0] = yt[512:1024].astype(BF16)
    nqT_ref[0] = yt[1024:1536].astype(BF16)
    vsT_ref[0] = yt[1536:1664].astype(BF16)
    vwT_ref[0] = yt[1664:1792].astype(BF16)
    gl = lax.dot_general(wg_ref[...], xn, NT_DIMS, precision=lax.Precision.HIGHEST,
                         preferred_element_type=F32)
    gT_ref[0] = jax.nn.sigmoid(gl)


def _inproj(x, g, wr, wt, wg, tm):
    B, S, D = x.shape
    grid = (B, S // tm)
    row = lambda w: pl.BlockSpec((1, tm, w), lambda b, i: (b, i, 0))
    col = lambda h: pl.BlockSpec((1, h, tm), lambda b, i: (b, 0, i))
    full = lambda a: pl.BlockSpec(a.shape, lambda b, i: (0,) * a.ndim)
    out_shape = [
        jax.ShapeDtypeStruct((B, S, 512), BF16),
        jax.ShapeDtypeStruct((B, S, 128), BF16),
        jax.ShapeDtypeStruct((B, S, 128), BF16),
        jax.ShapeDtypeStruct((B, S, 128), BF16),
        jax.ShapeDtypeStruct((B, S, 128), BF16),
        jax.ShapeDtypeStruct((B, 512, S), BF16),
        jax.ShapeDtypeStruct((B, 512, S), BF16),
        jax.ShapeDtypeStruct((B, 512, S), BF16),
        jax.ShapeDtypeStruct((B, 128, S), BF16),
        jax.ShapeDtypeStruct((B, 128, S), BF16),
        jax.ShapeDtypeStruct((B, 32, S), F32),
    ]
    out_specs = [row(512), row(128), row(128), row(128), row(128),
                 col(512), col(512), col(512), col(128), col(128), col(32)]
    return pl.pallas_call(
        _inproj_kernel, grid=grid,
        in_specs=[pl.BlockSpec((1, tm, D), lambda b, i: (b, i, 0)),
                  full(g), full(wr), full(wt), full(wg)],
        out_specs=out_specs, out_shape=out_shape,
        compiler_params=_params(2), name="inproj",
    )(x, g, wr, wt, wg)


ONES_ROWS = 16


def _softmax_stage(s, c, m):
    mt = jnp.max(s, axis=0, keepdims=True) - c
    m_new = jnp.maximum(m, mt)
    alpha = jnp.exp2(m - m_new)
    p = jnp.exp2(s - (m_new + c))
    return m_new, p.astype(BF16), alpha


def _pipelined_tiles(scores, values, offsets, s_ref, p_ref, first, n_ch, n_tiles, j_first):
    chains = range(n_ch)

    def qk_into(slot, j):
        sc = scores(j)
        for c in chains:
            s_ref[slot, c] = sc[c]

    def pv_from(slot, j, alphas, accs):
        out = []
        for c, vt in zip(chains, values(j)):
            vt1 = jnp.concatenate([vt, jnp.ones((ONES_ROWS, vt.shape[1]), BF16)], axis=0)
            out.append(alphas[c] * accs[c] + jnp.dot(vt1, p_ref[slot, c], preferred_element_type=F32))
        return out

    def softmax_into(slot, j, ms):
        cs = offsets(j)
        new = [_softmax_stage(s_ref[slot, c], cs[c], ms[c]) for c in chains]
        for c in chains:
            p_ref[slot, c] = new[c][1]
        return [n[0] for n in new], [n[2] for n in new]

    qk_into(0, 0)
    first = first()
    for c in chains:
        p_ref[1, c] = first[c][1]

    def pair(i, carry):
        ms, alphas, accs, j_prev = carry
        for slot in range(2):
            t = 2 * i + slot
            qk_into(1 - slot, t + 1)
            accs = pv_from(1 - slot, j_prev, alphas, accs)
            ms, alphas = softmax_into(slot, t, ms)
            j_prev = t
        return ms, alphas, accs, j_prev

    n_q = first[0][0].shape[1]
    init = ([f[0] for f in first], [f[2] for f in first],
            [jnp.zeros((HEAD_DIM + ONES_ROWS, n_q), F32)] * n_ch, j_first)
    _, alphas, accs, j_last = lax.fori_loop(0, (n_tiles + 1) // 2, pair, init)
    accs = pv_from(1, j_last, alphas, accs)
    return [accs[c][:HEAD_DIM] / jnp.maximum(accs[c][HEAD_DIM:HEAD_DIM + 1], 1e-30) for c in chains]


def _moba_kernel(qT_ref, k_ref, vT_ref, aug_ref, srow_ref, sl_ref, o_ref,
                 kmean_ref, kparts_ref, s_ref, p_ref, *, nb, nbp, topk):
    qi = pl.program_id(2)
    blk = MOBA_BLOCK

    @pl.when(qi == 0)
    def _():
        kmean_ref[...] = jnp.zeros(kmean_ref.shape, F32)

        def body(n, carry):
            kb = k_ref[0, pl.ds(pl.multiple_of(n * blk, blk), blk), :].astype(F32)
            kmean_ref[pl.ds(n, 1), :] = jnp.mean(kb, axis=0, keepdims=True)
            return carry
        lax.fori_loop(0, nb, body, 0)
        km = kmean_ref[...]
        head = lax.broadcasted_iota(I32, km.shape, 1) >> 6
        km2 = jnp.concatenate([jnp.where(head == h, km, 0.0) for h in range(2)], axis=0)
        hi = km2.astype(BF16)
        mid = (km2 - hi.astype(F32)).astype(BF16)
        lo = (km2 - hi.astype(F32) - mid.astype(F32)).astype(BF16)
        kparts_ref[0] = hi
        kparts_ref[1] = mid
        kparts_ref[2] = lo

    qT = qT_ref[0]
    row = lax.broadcasted_iota(I32, qT.shape, 0)
    qpad = [jnp.where((row >> 6) == h, qT, jnp.zeros_like(qT)) for h in range(2)]

    gates = (jnp.dot(kparts_ref[0], qT, preferred_element_type=F32)
             + jnp.dot(kparts_ref[1], qT, preferred_element_type=F32)
             + jnp.dot(kparts_ref[2], qT, preferred_element_type=F32))
    bidx = lax.broadcasted_iota(I32, (nbp, blk), 0)
    rhs = []
    for h in range(2):
        gate = gates[h * nbp:(h + 1) * nbp]
        gsc = jnp.where(bidx < qi, gate, -jnp.inf)
        bias = jnp.full((nbp, blk), NEG_BIG, F32)
        for _ in range(topk):
            mx = jnp.max(gsc, axis=0, keepdims=True)
            idx = jnp.min(jnp.where(gsc == mx, bidx, nbp), axis=0, keepdims=True)
            pick = jnp.logical_and(bidx == idx, mx > -jnp.inf)
            bias = jnp.where(pick, 0.0, bias)
            gsc = jnp.where(pick, -jnp.inf, gsc)
        pad = jnp.zeros((2 * HEAD_DIM - nbp - 16, blk), BF16)
        rhs.append(jnp.concatenate([qpad[h], bias.astype(BF16), srow_ref[h], pad], axis=0))

    def scores(j, a):
        k0 = pl.multiple_of(j * blk, blk)
        lhs = jnp.concatenate([k_ref[0, pl.ds(k0, blk), :], aug_ref[a]], axis=1)
        return [jnp.dot(lhs, rhs[h], preferred_element_type=F32) for h in range(2)]

    def values(j):
        k0 = pl.multiple_of(j * blk, blk)
        return [vT_ref[0, h * HEAD_DIM:(h + 1) * HEAD_DIM, pl.ds(k0, blk)] for h in range(2)]

    def offsets(j):
        dq = ((qi - j) * blk).astype(F32)
        return [sl_ref[h] * dq for h in range(2)]

    ik = lax.broadcasted_iota(I32, (blk, blk), 0)
    iq = lax.broadcasted_iota(I32, (blk, blk), 1)
    s_own = [jnp.where(ik <= iq, s, NEG_BIG) for s in scores(qi, nb)]
    m0 = jnp.full((1, blk), NEG_BIG, F32)

    def first():
        return [_softmax_stage(s_own[h], jnp.zeros((1, blk), F32), m0) for h in range(2)]

    outs = _pipelined_tiles(lambda j: scores(jnp.minimum(j, nb - 1), jnp.minimum(j, nb - 1)),
                            values, offsets, s_ref, p_ref, first, 2, qi, qi)
    o_ref[0] = jnp.concatenate(outs, axis=0).T.astype(BF16)


def _moba(mqT, mk, mvT, aug, srow, sl):
    B, _, S = mqT.shape
    blk = MOBA_BLOCK
    nb = S // blk
    topk = min(MOBA_TOPK, nb)
    nbp = -(-nb // 16) * 16
    grid = (B, MOBA_HEADS // 2, nb)
    return pl.pallas_call(
        functools.partial(_moba_kernel, nb=nb, nbp=nbp, topk=topk), grid=grid,
        in_specs=[
            pl.BlockSpec((1, 128, blk), lambda b, p, i: (b, p, i)),
            pl.BlockSpec((1, S, 128), lambda b, p, i: (b, 0, p)),
            pl.BlockSpec((1, 128, S), lambda b, p, i: (b, p, 0)),
            pl.BlockSpec(aug.shape, lambda b, p, i: (0, 0, 0)),
            pl.BlockSpec((2, 16, blk), lambda b, p, i: (p, 0, 0)),
            pl.BlockSpec((2, 1, blk), lambda b, p, i: (p, 0, 0)),
        ],
        out_specs=pl.BlockSpec((1, blk, 128), lambda b, p, i: (b, i, p)),
        out_shape=jax.ShapeDtypeStruct((B, S, 512), BF16),
        scratch_shapes=[
            pltpu.VMEM((nbp, 128), F32),
            pltpu.VMEM((3, 2 * nbp, 128), BF16),
            pltpu.VMEM((2, 2, blk, blk), F32),
            pltpu.VMEM((2, 2, blk, blk), BF16),
        ],
        compiler_params=_params(3), name="moba",
    )(mqT, mk, mvT, aug, srow, sl)


def _compress_kernel(kc_ref, vc_ref, wk_ref, wvT_ref, pek_ref, pev_ref, w2k_ref, w2vT_ref,
                     kcmp_ref, vcmpT_ref):
    nc = kc_ref.shape[1]

    wk = wk_ref[...]
    ab = jnp.dot(kc_ref[0], wk, preferred_element_type=F32)
    pt = (jnp.dot(pek_ref[0], wk[:, 0:128].astype(F32), preferred_element_type=F32)
          + jnp.dot(pek_ref[1], wk[:, 128:256].astype(F32), preferred_element_type=F32))
    pre = ab[:, 0:128] + pltpu.roll(ab[:, 128:256], nc - 1, 0) + pt[0:1]
    hid = jax.nn.gelu(pre)
    kcmp_ref[0] = jnp.dot(hid.astype(BF16), w2k_ref[...], preferred_element_type=F32).astype(BF16)

    wvT = wvT_ref[...]
    abT = lax.dot_general(wvT, vc_ref[0], NT_DIMS, preferred_element_type=F32)
    ptT = (lax.dot_general(wvT[0:128].astype(F32), pev_ref[0], NT_DIMS, preferred_element_type=F32)
           + lax.dot_general(wvT[128:256].astype(F32), pev_ref[1], NT_DIMS, preferred_element_type=F32))
    preT = abT[0:128] + pltpu.roll(abT[128:256], nc - 1, 1) + ptT[:, 0:1]
    hidT = jax.nn.gelu(preT)
    vcmpT_ref[0] = jnp.dot(w2vT_ref[...], hidT.astype(BF16), preferred_element_type=F32).astype(BF16)


def _compress(kc2, vc2, wk, wvT, pek, pev, w2k, w2vT):
    B, nc, _ = kc2.shape
    full = lambda a: pl.BlockSpec(a.shape, lambda b: (0,) * a.ndim)
    blk = pl.BlockSpec((1, nc, kc2.shape[2]), lambda b: (b, 0, 0))
    return pl.pallas_call(
        _compress_kernel, grid=(B,),
        in_specs=[blk, blk, full(wk), full(wvT), full(pek), full(pev), full(w2k), full(w2vT)],
        out_specs=[pl.BlockSpec((1, nc, 128), lambda b: (b, 0, 0)),
                   pl.BlockSpec((1, 128, nc), lambda b: (b, 0, 0))],
        out_shape=[jax.ShapeDtypeStruct((B, nc, 128), BF16),
                   jax.ShapeDtypeStruct((B, 128, nc), BF16)],
        compiler_params=_params(1), name="nsa_compress",
    )(kc2, vc2, wk, wvT, pek, pev, w2k, w2vT)


def _nsa_kernel(qT_ref, kcmp_ref, vcmpT_ref, ks_ref, vsT_ref, kw_ref, vwT_ref,
                g_ref, sl_ref, srow_ref, auga_ref, augb_ref, augc_ref, tblc_ref, bw_ref, o_ref,
                s_ref, p_ref, pc_ref, flagv_ref, flags_ref, list_ref, fsem, *, n_slc, topn):
    g = pl.program_id(1)
    qi = pl.program_id(2)
    tq = NSA_TQ
    hg = NSA_GROUP
    wl = hg * tq
    q0 = qi * tq

    q4 = qT_ref[0]
    qT = jnp.concatenate([q4[h * HEAD_DIM:(h + 1) * HEAD_DIM] for h in range(hg)], axis=1)
    qT2 = jnp.concatenate([qT, qT], axis=0)
    rowi = lax.broadcasted_iota(I32, qT2.shape, 0)
    qpad = jnp.where((rowi >> 6) == g, qT2, jnp.zeros_like(qT2))
    slope = sl_ref[0]
    lane = lax.broadcasted_iota(I32, (1, wl), 1)
    t_q = q0 + (lane & (tq - 1))

    nc = kcmp_ref.shape[1]
    rhs_top = jnp.where((rowi >> 6) == g, qT2, srow_ref[0])
    mine_c = (lax.broadcasted_iota(I32, (nc, LANES), 1) >> 6) == g
    lhs_c = jnp.where(mine_c, kcmp_ref[0], augc_ref[0])
    first_c = pl.multiple_of(nc - qi * (tq // NSA_CMP_STRIDE), 8)
    z = jnp.dot(lhs_c, rhs_top, preferred_element_type=F32) + tblc_ref[pl.ds(first_c, nc), :]
    mx = jnp.max(z, axis=0, keepdims=True)
    e = jnp.exp2(z - mx)
    den = jnp.maximum(jnp.sum(e, axis=0, keepdims=True), 1e-30)
    p = e * jnp.where(t_q >= NSA_CMP_LEN - 1, 1.0 / den, 0.0)
    o_c = jnp.dot(vcmpT_ref[0], p.astype(BF16), preferred_element_type=F32)

    pc = p[:, 0:tq]
    for h in range(1, hg):
        pc = pc + p[:, h * tq:(h + 1) * tq]
    n_lc = tq // LANES
    for c in range(n_lc):
        pc_ref[c] = pc[:, c * LANES:(c + 1) * LANES]
    su = NSA_SLC_BLOCK // NSA_CMP_STRIDE
    x = [jnp.concatenate([pc_ref[c, pl.ds(k, n_slc, stride=su), :] for c in range(n_lc)], axis=1)
         for k in range(su)]
    jb = lax.broadcasted_iota(I32, (n_slc, tq), 0)
    prev = jnp.where(jb == 0, 0.0, pltpu.roll(x[3], 1, 0))
    imp = 2.0 * (x[0] + x[1] + x[2]) + x[3] + prev
    cur = (q0 + lax.broadcasted_iota(I32, (1, tq), 1)) >> 6
    allowed = jb <= cur
    forced = jnp.logical_or(jb == 0, jnp.logical_or(jb == cur, jb == cur - 1))
    bias = jnp.where(jnp.logical_and(allowed, forced), 0.0, NEG_BIG)
    sc = jnp.where(jnp.logical_and(allowed, jnp.logical_not(forced)), imp, -1.0)
    for _ in range(topn - 3):
        smx = jnp.max(sc, axis=0, keepdims=True)
        idx = jnp.min(jnp.where(sc == smx, jb, n_slc), axis=0, keepdims=True)
        pick = jnp.logical_and(jb == idx, smx >= 0.0)
        bias = jnp.where(pick, 0.0, bias)
        sc = jnp.where(pick, -1.0, sc)

    n_tiles = augb_ref.shape[0]
    jd = lax.div(q0, SLC_TILE)
    chosen = jnp.where(bias == 0.0, 1.0, 0.0).astype(BF16)
    per_blk = lax.dot_general(jnp.ones((SUBLANES, tq), BF16), chosen, NT_DIMS,
                              preferred_element_type=F32)
    per_shift = (SLC_TILE // NSA_SLC_BLOCK).bit_length() - 1
    in_tile = (lax.broadcasted_iota(I32, (n_slc, LANES), 0) >> per_shift) == lax.broadcasted_iota(
        I32, (n_slc, LANES), 1)
    per_tile = jnp.dot(per_blk.astype(BF16), jnp.where(in_tile, 1.0, 0.0).astype(BF16),
                       preferred_element_type=F32)
    is_past = lax.broadcasted_iota(I32, (SUBLANES, LANES), 1) < jd
    flagv_ref[...] = jnp.where(jnp.logical_and(per_tile > 0.0, is_past), 1, 0).astype(I32)
    flag_copy = pltpu.make_async_copy(flagv_ref, flags_ref, fsem)
    flag_copy.start()

    start = pl.multiple_of(jnp.maximum(q0 - NSA_WINDOW, 0), tq)
    first_w = pl.multiple_of(NSA_WINDOW - (q0 - start), tq)
    kt = kw_ref[0, pl.ds(start, WIN_KEYS), :]
    z = jnp.dot(kt, qpad, preferred_element_type=F32) + bw_ref[0, pl.ds(first_w, WIN_KEYS), :]
    mx = jnp.max(z, axis=0, keepdims=True)
    p = jnp.exp2(z - mx)
    den = jnp.maximum(jnp.sum(p, axis=0, keepdims=True), 1e-30)
    o_w = jnp.dot(vwT_ref[0, :, pl.ds(start, WIN_KEYS)], p.astype(BF16),
                  preferred_element_type=F32) / den

    flag_copy.wait()
    for i in range(TILE_LIST_LEN):
        list_ref[i] = 0

    def add_tile(t, n):
        list_ref[n] = t
        return n + flags_ref[0, t]
    n_used = lax.fori_loop(0, n_tiles, add_tile, 0)
    list_ref[DIAG_SLOT] = jd

    wc = NSA_CHAIN_LANES
    n_ch = wl // wc

    def lane_split(a):
        return [a[:, c * wc:(c + 1) * wc] for c in range(n_ch)]

    if n_slc < LANES:
        bias = jnp.concatenate([bias, jnp.zeros((LANES - n_slc, tq), F32)], axis=0)
    bias4 = jnp.concatenate([bias.astype(BF16)] * hg, axis=1)
    rhs = jnp.concatenate([rhs_top, bias4], axis=0)
    mine = (lax.broadcasted_iota(I32, (SLC_TILE, LANES), 1) >> 6) == g

    def scores(j, null):
        k0 = pl.multiple_of(j * SLC_TILE, SLC_TILE)
        kt = ks_ref[0, pl.ds(k0, SLC_TILE), :]
        lhs = jnp.concatenate([jnp.where(mine, kt, auga_ref[null]), augb_ref[j]], axis=1)
        return jnp.dot(lhs, rhs, preferred_element_type=F32)

    def values(i):
        k0 = pl.multiple_of(list_ref[i] * SLC_TILE, SLC_TILE)
        return [vsT_ref[0, :, pl.ds(k0, SLC_TILE)]] * n_ch

    def offsets(i):
        return lane_split(slope * (q0 - list_ref[i] * SLC_TILE).astype(F32))

    def past_scores(i):
        return lane_split(scores(list_ref[jnp.minimum(i, n_tiles - 1)], (i >= n_used).astype(I32)))

    t_k = jd * SLC_TILE + lax.broadcasted_iota(I32, (SLC_TILE, wl), 0)
    s_diag = lane_split(jnp.where(t_k <= t_q, scores(jd, 0), NEG_BIG))
    c_diag = offsets(DIAG_SLOT)

    def first():
        return [_softmax_stage(s_diag[c], c_diag[c], jnp.full((1, wc), NEG_BIG, F32))
                for c in range(n_ch)]

    o_s = jnp.concatenate(
        _pipelined_tiles(past_scores, values, offsets, s_ref, p_ref, first, n_ch, n_used, DIAG_SLOT),
        axis=1)

    gt = g_ref[0]

    def gate_row(br):
        return jnp.concatenate([gt[br * hg + h:br * hg + h + 1] for h in range(hg)], axis=1)

    o = gate_row(0) * o_c + gate_row(1) * o_s + gate_row(2) * o_w
    o4 = jnp.concatenate([o[:, h * tq:(h + 1) * tq] for h in range(hg)], axis=0)
    o_ref[0] = o4.T.astype(BF16)


def _nsa(nqT, kcmp, vcmpT, ks, vsT, kw, vwT, gT, sl, srow, auga, augb, augc, tblc, bw):
    B, _, S = nqT.shape
    tq = NSA_TQ
    nc = kcmp.shape[1]
    n_slc = S // NSA_SLC_BLOCK
    topn = min(NSA_SLC_TOPN, n_slc)
    wl = NSA_GROUP * tq
    grid = (B, NSA_KV_HEADS, S // tq)
    return pl.pallas_call(
        functools.partial(_nsa_kernel, n_slc=n_slc, topn=topn), grid=grid,
        in_specs=[
            pl.BlockSpec((1, NSA_GROUP * HEAD_DIM, tq), lambda b, g, i: (b, g, i)),
            pl.BlockSpec((1, nc, 128), lambda b, g, i: (b, 0, 0)),
            pl.BlockSpec((1, HEAD_DIM, nc), lambda b, g, i: (b, g, 0)),
            pl.BlockSpec((1, S, 128), lambda b, g, i: (b, 0, 0)),
            pl.BlockSpec((1, HEAD_DIM, S), lambda b, g, i: (b, g, 0)),
            pl.BlockSpec((1, S, 128), lambda b, g, i: (b, 0, 0)),
            pl.BlockSpec((1, HEAD_DIM, S), lambda b, g, i: (b, g, 0)),
            pl.BlockSpec((1, 16, tq), lambda b, g, i: (b, g, i)),
            pl.BlockSpec((1, 1, wl), lambda b, g, i: (g, 0, 0)),
            pl.BlockSpec((1, 2 * HEAD_DIM, wl), lambda b, g, i: (g, 0, 0)),
            pl.BlockSpec((2, SLC_TILE, LANES), lambda b, g, i: (g, 0, 0)),
            pl.BlockSpec(augb.shape, lambda b, g, i: (0, 0, 0)),
            pl.BlockSpec((1, nc, LANES), lambda b, g, i: (g, 0, 0)),
            pl.BlockSpec(tblc.shape, lambda b, g, i: (0, 0)),
            pl.BlockSpec((1,) + bw.shape[1:], lambda b, g, i: (g, 0, 0)),
        ],
        out_specs=pl.BlockSpec((1, tq, NSA_GROUP * HEAD_DIM), lambda b, g, i: (b, i, g)),
        out_shape=jax.ShapeDtypeStruct((B, S, 512), BF16),
        scratch_shapes=[
            pltpu.VMEM((2, wl // NSA_CHAIN_LANES, SLC_TILE, NSA_CHAIN_LANES), F32),
            pltpu.VMEM((2, wl // NSA_CHAIN_LANES, SLC_TILE, NSA_CHAIN_LANES), BF16),
            pltpu.VMEM((tq // LANES, nc, LANES), F32),
            pltpu.VMEM((SUBLANES, LANES), I32),
            pltpu.SMEM((SUBLANES, LANES), I32),
            pltpu.SMEM((TILE_LIST_LEN,), I32),
            pltpu.SemaphoreType.DMA,
        ],
        compiler_params=_params(3), name="nsa",
    )(nqT, kcmp, vcmpT, ks, vsT, kw, vwT, gT, sl, srow, auga, augb, augc, tblc, bw)


def _outproj_kernel(om_ref, on_ref, x_ref, wo_ref, g_ref, wr_ref, br_ref,
                    x1_ref, hn_ref, e_ref, w_ref):
    attn = (jnp.dot(om_ref[...], wo_ref[0:512, :], preferred_element_type=F32)
            + jnp.dot(on_ref[...], wo_ref[512:1024, :], preferred_element_type=F32))
    x1 = x_ref[...] + attn
    x1_ref[...] = x1
    hn = _rmsnorm(x1, g_ref[...])
    _store_token_tiles(hn_ref, hn)
    logits = jnp.dot(hn, wr_ref[...], precision=lax.Precision.HIGHEST,
                     preferred_element_type=F32) + br_ref[...]
    tm = logits.shape[0]
    lane = lax.broadcasted_iota(I32, (tm, LANES), 1)
    sc = jnp.where(lane < N_EXPERTS, logits, -jnp.inf)
    e_out = jnp.zeros((tm, LANES), I32)
    vals = []
    for k in range(TOP_K):
        mx = jnp.max(sc, axis=1, keepdims=True)
        idx = jnp.min(jnp.where(sc == mx, lane, LANES), axis=1, keepdims=True)
        e_out = jnp.where(lane == k, idx, e_out)
        sc = jnp.where(lane == idx, -jnp.inf, sc)
        vals.append(mx)
    ex = [jnp.exp(v - vals[0]) for v in vals]
    den = ex[0] + ex[1] + ex[2] + ex[3]
    w_out = jnp.zeros((tm, LANES), F32)
    for k in range(TOP_K):
        w_out = jnp.where(lane == k, ex[k] / den, w_out)
    e_ref[...] = e_out
    w_ref[...] = w_out


def _outproj(om, on, x, wo, g, wr, br, tm):
    N, D = x.shape
    full = lambda a: pl.BlockSpec(a.shape, lambda i: (0,) * a.ndim)
    row = lambda w: pl.BlockSpec((tm, w), lambda i: (i, 0))
    return pl.pallas_call(
        _outproj_kernel, grid=(N // tm,),
        in_specs=[row(512), row(512), row(D), full(wo), full(g), full(wr), full(br)],
        out_specs=[row(D), pl.BlockSpec((tm * SUBLANES, LANES), lambda i: (i, 0)), row(LANES), row(LANES)],
        out_shape=[jax.ShapeDtypeStruct((N, D), F32), jax.ShapeDtypeStruct((N * SUBLANES, LANES), F32),
                   jax.ShapeDtypeStruct((N, LANES), I32), jax.ShapeDtypeStruct((N, LANES), F32)],
        compiler_params=_params(1), name="outproj_router",
    )(om, on, x, wo, g, wr, br)


def _rank_kernel(e_ref, rank_ref, cnt_ref, base_ref):
    i = pl.program_id(0)
    T = e_ref.shape[0]

    @pl.when(i == 0)
    def _():
        base_ref[...] = jnp.zeros(base_ref.shape, F32)

    e = e_ref[...]
    lane = lax.broadcasted_iota(I32, (T, LANES), 1)
    tril = jnp.where(lax.broadcasted_iota(I32, (T, T), 0) >= lax.broadcasted_iota(I32, (T, T), 1),
                     1.0, 0.0).astype(BF16)
    out = jnp.zeros((T, LANES), I32)
    for k in range(TOP_K):
        hit = lane == e[:, k:k + 1]
        oh = jnp.where(hit, 1.0, 0.0)
        cum = jnp.dot(tril, oh.astype(BF16), preferred_element_type=F32)
        base = base_ref[0:1, :]
        r = jnp.sum(jnp.where(hit, cum - 1.0 + base, 0.0), axis=1, keepdims=True)
        out = jnp.where(lane == k, r.astype(I32), out)
        base_ref[...] = base_ref[...] + jnp.sum(oh, axis=0, keepdims=True)
    rank_ref[...] = out
    cnt_ref[...] = base_ref[...]


def _ranks(e128):
    N = e128.shape[0]
    T = RANK_TILE
    return pl.pallas_call(
        _rank_kernel, grid=(N // T,),
        in_specs=[pl.BlockSpec((T, LANES), lambda i: (i, 0))],
        out_specs=[pl.BlockSpec((T, LANES), lambda i: (i, 0)),
                   pl.BlockSpec((8, LANES), lambda i: (0, 0))],
        out_shape=[jax.ShapeDtypeStruct((N, LANES), I32),
                   jax.ShapeDtypeStruct((8, LANES), F32)],
        scratch_shapes=[pltpu.VMEM((8, LANES), F32)],
        compiler_params=_params(1), name="route_ranks",
    )(e128)


def _row_copy(src, dst, i_src, i_dst, sem):
    return pltpu.make_async_copy(src.at[pl.ds(pl.multiple_of(i_src * SUBLANES, SUBLANES), SUBLANES)],
                                 dst.at[pl.ds(pl.multiple_of(i_dst * SUBLANES, SUBLANES), SUBLANES)], sem)


def _store_token_tiles(ref, x):
    rows = x.shape[0]
    for c in range(SUBLANES):
        ref[pl.ds(c, rows, stride=SUBLANES), :] = x[:, c * LANES:(c + 1) * LANES]


def _load_token_tiles(ref, rows):
    return jnp.concatenate([ref[pl.ds(c, rows, stride=SUBLANES), :] for c in range(SUBLANES)], axis=1)


def _dispatch_kernel(gap_lo_ref, gap_hi_ref, dest_hbm, hp_ref, out_hbm, idx_ref, zero_ref,
                     isem, sem, zsem, bsem):
    i = pl.program_id(0)
    T = ROUTE_TILE

    @pl.when(i == 0)
    def _():
        zero_ref[...] = jnp.zeros(zero_ref.shape, F32)
        n_exp = gap_lo_ref.shape[0] - 1

        def for_each_pad_row(start):
            def gap(e, carry):
                lo, hi = gap_lo_ref[e], gap_hi_ref[e]
                for par in range(2):
                    def row(r, c):
                        copy = _row_copy(zero_ref, out_hbm, 0, lo + 2 * r + par, zsem)
                        if start:
                            copy.start(priority=par)
                        else:
                            copy.wait()
                        return c
                    lax.fori_loop(0, (hi - lo + 1 - par) // 2, row, carry)
                return carry
            lax.fori_loop(0, n_exp, gap, 0)

        def for_each_tail_block(start):
            def blk(b, carry):
                rows = pl.ds(pl.multiple_of(b * (MOE_ROWS * SUBLANES), MOE_ROWS * SUBLANES), MOE_ROWS * SUBLANES)
                copy = pltpu.make_async_copy(zero_ref, out_hbm.at[rows], bsem)
                if start:
                    copy.start()
                else:
                    copy.wait()
                return carry
            lax.fori_loop(gap_lo_ref[n_exp] // MOE_ROWS, gap_hi_ref[n_exp] // MOE_ROWS, blk, 0)

        for_each_pad_row(True)
        for_each_tail_block(True)
        for_each_pad_row(False)
        for_each_tail_block(False)

    cp = pltpu.make_async_copy(dest_hbm.at[i], idx_ref, isem)
    cp.start()
    cp.wait()

    def issue(t, carry):
        for k in range(TOP_K):
            _row_copy(hp_ref, out_hbm, t, idx_ref[t * TOP_K + k], sem).start(priority=k % 2)
        return carry
    lax.fori_loop(0, T, issue, 0, unroll=ROW_DMA_UNROLL)

    def drain(t, carry):
        for k in range(TOP_K):
            _row_copy(hp_ref, out_hbm, 0, 0, sem).wait()
        return carry
    lax.fori_loop(0, T, drain, 0, unroll=ROW_DMA_UNROLL)


def _dispatch(gap_lo, gap_hi, dest2, hp, n_rows):
    nsteps = dest2.shape[0]
    T = ROUTE_TILE
    grid_spec = pltpu.PrefetchScalarGridSpec(
        num_scalar_prefetch=2, grid=(nsteps,),
        in_specs=[pl.BlockSpec(memory_space=pl.ANY),
                  pl.BlockSpec((T * SUBLANES, LANES), lambda i, lo, hi: (i, 0))],
        out_specs=pl.BlockSpec(memory_space=pl.ANY),
        scratch_shapes=[pltpu.SMEM((T * TOP_K,), I32), pltpu.VMEM((MOE_ROWS * SUBLANES, LANES), F32),
                        pltpu.SemaphoreType.DMA, pltpu.SemaphoreType.DMA, pltpu.SemaphoreType.DMA,
                        pltpu.SemaphoreType.DMA],
    )
    return pl.pallas_call(
        _dispatch_kernel, grid_spec=grid_spec,
        out_shape=jax.ShapeDtypeStruct((n_rows * SUBLANES, LANES), F32),
        compiler_params=_params(1), name="moe_dispatch",
    )(gap_lo, gap_hi, dest2, hp)


W_PREP_COLS = 256


def _expert_kernel(be_ref, na_ref, x_ref, wup_ref, bg_ref, bu_ref, wd_ref, bd_ref, y_ref,
                   wg_ref, wu_ref, wt_ref):
    b = pl.program_id(0)
    active = b < na_ref[0]
    new_expert = jnp.logical_or(b == 0, be_ref[b] != be_ref[jnp.maximum(b - 1, 0)])

    @pl.when(jnp.logical_and(active, new_expert))
    def _():
        half = W_PREP_COLS // 2
        n_lc = wt_ref.shape[0]

        def every_other(first):
            return jnp.concatenate(
                [wt_ref[c, pl.ds(first, half, stride=2), :] for c in range(n_lc)], axis=1).astype(BF16)

        for ch in range(wup_ref.shape[2] // W_PREP_COLS):
            panel = wup_ref[0, :, ch * W_PREP_COLS:(ch + 1) * W_PREP_COLS].T
            for c in range(n_lc):
                wt_ref[c] = panel[:, c * LANES:(c + 1) * LANES]
            wg_ref[ch * half:(ch + 1) * half, :] = every_other(0)
            wu_ref[ch * half:(ch + 1) * half, :] = every_other(1)

    @pl.when(active)
    def _():
        xb = _load_token_tiles(x_ref, MOE_ROWS).astype(BF16)
        gg = lax.dot_general(xb, wg_ref[...], NT_DIMS, preferred_element_type=F32) + bg_ref[0]
        uu = lax.dot_general(xb, wu_ref[...], NT_DIMS, preferred_element_type=F32) + bu_ref[0]
        gg = jnp.minimum(gg, SWIGLU_LIMIT)
        uu = jnp.clip(uu, -SWIGLU_LIMIT, SWIGLU_LIMIT)
        a = gg * jax.nn.sigmoid(SWIGLU_ALPHA * gg) * (uu + 1.0)
        _store_token_tiles(y_ref, jnp.dot(a.astype(BF16), wd_ref[0], preferred_element_type=F32) + bd_ref[0])

    @pl.when(jnp.logical_not(active))
    def _():
        y_ref[...] = jnp.zeros(y_ref.shape, F32)


def _experts(blk_e, n_act, xrows, w_up, bg, bu, wd, bd):
    _, D, F2 = w_up.shape
    F = F2 // 2
    assert D == SUBLANES * LANES
    P = xrows.shape[0] // SUBLANES
    n_blk = P // MOE_ROWS
    rows_spec = pl.BlockSpec((MOE_ROWS * SUBLANES, LANES), lambda b, be, na: (b, 0))
    wspec = lambda r, c: pl.BlockSpec((1, r, c), lambda b, be, na: (be[b], 0, 0))
    grid_spec = pltpu.PrefetchScalarGridSpec(
        num_scalar_prefetch=2, grid=(n_blk,),
        in_specs=[rows_spec, wspec(D, F2), wspec(1, F), wspec(1, F), wspec(F, D), wspec(1, D)],
        out_specs=rows_spec,
        scratch_shapes=[pltpu.VMEM((F, D), BF16), pltpu.VMEM((F, D), BF16),
                        pltpu.VMEM((D // LANES, W_PREP_COLS, LANES), F32)],
    )
    return pl.pallas_call(
        _expert_kernel, grid_spec=grid_spec,
        out_shape=jax.ShapeDtypeStruct(xrows.shape, F32),
        compiler_params=_params(1), name="moe_experts",
    )(blk_e, n_act, xrows, w_up, bg, bu, wd, bd)


def _combine_kernel(dest_hbm, x1_ref, w_ref, g_ref, y_hbm, o_ref, idx_ref, buf_ref, isem, sem):
    i = pl.program_id(0)
    T = ROUTE_TILE
    cp = pltpu.make_async_copy(dest_hbm.at[i], idx_ref, isem)
    cp.start()
    cp.wait()

    def issue(t, carry):
        for k in range(TOP_K):
            _row_copy(y_hbm, buf_ref.at[k], idx_ref[t * TOP_K + k], t, sem).start(priority=k % 2)
        return carry
    lax.fori_loop(0, T, issue, 0, unroll=ROW_DMA_UNROLL)

    def drain(t, carry):
        for k in range(TOP_K):
            _row_copy(y_hbm, buf_ref.at[k], 0, 0, sem).wait()
        return carry
    lax.fori_loop(0, T, drain, 0, unroll=ROW_DMA_UNROLL)

    x1 = x1_ref[...]
    w = w_ref[...]
    cols = []
    for c in range(SUBLANES):
        acc = x1[:, c * LANES:(c + 1) * LANES]
        for k in range(TOP_K):
            acc = acc + w[:, k:k + 1] * buf_ref[k, pl.ds(c, T, stride=SUBLANES), :]
        cols.append(acc)
    o_ref[...] = _rmsnorm(jnp.concatenate(cols, axis=1), g_ref[...])


def _combine(dest2, x1, w128, g, yrows):
    N, D = x1.shape
    T = ROUTE_TILE
    return pl.pallas_call(
        _combine_kernel, grid=(N // T,),
        in_specs=[pl.BlockSpec(memory_space=pl.ANY),
                  pl.BlockSpec((T, D), lambda i: (i, 0)),
                  pl.BlockSpec((T, LANES), lambda i: (i, 0)),
                  pl.BlockSpec(g.shape, lambda i: (0, 0)),
                  pl.BlockSpec(memory_space=pl.ANY)],
        out_specs=pl.BlockSpec((T, D), lambda i: (i, 0)),
        out_shape=jax.ShapeDtypeStruct((N, D), F32),
        scratch_shapes=[pltpu.SMEM((T * TOP_K,), I32),
                        pltpu.VMEM((TOP_K, T * SUBLANES, LANES), F32),
                        pltpu.SemaphoreType.DMA, pltpu.SemaphoreType.DMA],
        compiler_params=_params(1), name="moe_combine",
    )(dest2, x1, w128, g, yrows)


def _alibi_slopes():
    n = MOBA_HEADS + NSA_HEADS
    s = jnp.exp2(-8.0 * jnp.arange(1, n + 1, dtype=F32) / n)
    return s[0::2], s[1::2]


def _prep_inproj(w_in):
    hd = HEAD_DIM
    sizes = [MOBA_HEADS * hd] * 3 + [NSA_HEADS * hd] + [NSA_KV_HEADS * hd] * 6 + [NSA_BRANCHES * NSA_HEADS]
    cuts = np.cumsum([0] + sizes)
    mq, mk, mv, nq, kc, vc, ks, vs, kw, vw, ng = [w_in[:, cuts[i]:cuts[i + 1]] for i in range(11)]
    qscale = (hd ** -0.5) * LOG2E
    wr = jnp.concatenate([mk, kc, vc, ks, kw], axis=1).astype(BF16)
    wt = jnp.concatenate([mq * qscale, mv, nq * qscale, vs, vw], axis=1).T.astype(BF16)
    ngr = ng.reshape(-1, NSA_KV_HEADS, NSA_GROUP, NSA_BRANCHES).transpose(1, 3, 2, 0)
    ngr = ngr.reshape(NSA_KV_HEADS, NSA_BRANCHES * NSA_GROUP, -1)
    wg = jnp.pad(ngr, ((0, 0), (0, 16 - NSA_BRANCHES * NSA_GROUP), (0, 0))).reshape(32, -1)
    return wr, wt, wg.astype(F32)


def _prep_compress(w1, w2, pe):
    hd, half = HEAD_DIM, NSA_CMP_STRIDE
    w1r = w1.reshape(2, half, hd, hd)
    eye = jnp.eye(NSA_KV_HEADS, dtype=w1.dtype)
    w = jnp.einsum('alde,gh->lgdahe', w1r, eye).reshape(half * NSA_KV_HEADS * hd, 2 * NSA_KV_HEADS * hd)
    w2b = jnp.einsum('de,gh->gdhe', w2, eye).reshape(NSA_KV_HEADS * hd, NSA_KV_HEADS * hd)
    per = pe.reshape(2, half, 1, hd)
    pe2 = jnp.broadcast_to(per, (2, half, NSA_KV_HEADS, hd)).reshape(2, 1, half * NSA_KV_HEADS * hd)
    pe2 = jnp.broadcast_to(pe2, (2, 8, pe2.shape[2]))
    return w.astype(BF16), w2b.astype(BF16), pe2.astype(F32)


def _attention_tables(S):
    moba_sl, nsa_sl = _alibi_slopes()
    moba_sl = moba_sl * LOG2E
    nsa_sl = nsa_sl * LOG2E
    blk = MOBA_BLOCK
    moba_row = jnp.broadcast_to(moba_sl[:, None, None], (MOBA_HEADS, 1, blk))
    nb = S // blk
    nbp = -(-nb // 16) * 16
    col = jnp.arange(LANES)[None, None, :]
    tile = jnp.arange(nb + 1)[:, None, None]
    off = jnp.arange(blk, dtype=F32)[None, :, None]
    moba_aug = jnp.where(jnp.logical_and(col == tile, tile < nb), 1.0,
                         jnp.where(jnp.logical_and(col >= nbp, col < nbp + 3), off, 0.0)).astype(BF16)
    parts = jnp.stack(list(_split3(moba_sl)) + [jnp.zeros_like(moba_sl)] * 13, axis=1)
    moba_srow = jnp.broadcast_to(parts[:, :, None], (MOBA_HEADS, 16, blk)).astype(BF16)

    wl = NSA_GROUP * NSA_TQ
    n_slc = S // NSA_SLC_BLOCK
    assert n_slc <= LANES, "block-choice rows must fit the spare contraction rows"
    nsa_row = jnp.repeat(nsa_sl.reshape(NSA_KV_HEADS, NSA_GROUP), NSA_TQ, axis=1)
    hi, mid, lo = [t[:, None, :] for t in _split3(nsa_row)]
    base = ((1 - jnp.arange(NSA_KV_HEADS)) * HEAD_DIM)[:, None, None]
    rows = jnp.arange(2 * HEAD_DIM)[None, :, None]
    nsa_srow = jnp.where(rows == base, hi, jnp.where(rows == base + 1, mid, jnp.where(
        rows == base + 2, lo, jnp.where(rows == base + 3, NEG_BIG, 0.0)))).astype(BF16)
    lane = jnp.arange(LANES)[None, None, None, :]
    base4 = base[:, None]
    null = jnp.arange(2, dtype=F32)[None, :, None, None]
    koff = jnp.arange(SLC_TILE, dtype=F32)[None, None, :, None]
    nsa_auga = jnp.where(jnp.logical_and(lane >= base4, lane < base4 + 3), koff,
                         jnp.where(lane == base4 + 3, null, 0.0))
    nsa_auga = nsa_auga.reshape(2 * NSA_KV_HEADS, SLC_TILE, LANES).astype(BF16)
    per = SLC_TILE // NSA_SLC_BLOCK
    tile = jnp.arange(S // SLC_TILE)[:, None, None]
    blk_of = tile * per + jnp.arange(SLC_TILE)[None, :, None] // NSA_SLC_BLOCK
    nsa_augb = (jnp.arange(LANES)[None, None, :] == blk_of).astype(BF16)

    nc = S // NSA_CMP_STRIDE
    ci = jnp.arange(nc)[None, :, None]
    lane3 = jnp.arange(LANES)[None, None, :]
    nsa_augc = jnp.where(jnp.logical_and(lane3 >= base + 4, lane3 < base + 7), (ci >> 1).astype(F32),
                         jnp.where(jnp.logical_and(lane3 >= base + 7, lane3 < base + 10),
                                   (ci & 1).astype(F32), 0.0)).astype(BF16)
    step2 = [t[:, None, :] for t in _split3(nsa_row * (2.0 * NSA_CMP_STRIDE))]
    step1 = [t[:, None, :] for t in _split3(nsa_row * (1.0 * NSA_CMP_STRIDE))]
    for k in range(3):
        nsa_srow = jnp.where(rows == base + 4 + k, step2[k].astype(BF16),
                             jnp.where(rows == base + 7 + k, step1[k].astype(BF16), nsa_srow))
    il = jnp.tile(jnp.arange(NSA_TQ), NSA_GROUP)[None, :]
    rel = (jnp.arange(2 * nc) - nc)[:, None]
    nsa_tblc = jnp.where(rel * NSA_CMP_STRIDE + (NSA_CMP_LEN - 1) <= il, 0.0, NEG_BIG).astype(F32)
    dist = (NSA_WINDOW + il - jnp.arange(NSA_WINDOW + WIN_KEYS)[:, None])[None]
    nsa_bw = jnp.where(jnp.logical_and(dist >= 0, dist < NSA_WINDOW),
                       -nsa_row[:, None, :] * dist.astype(F32), NEG_BIG)
    return ((moba_aug, moba_srow, moba_row),
            (nsa_row.reshape(NSA_KV_HEADS, 1, wl), nsa_srow, nsa_auga, nsa_augb, nsa_augc, nsa_tblc, nsa_bw))


def _split3(x):
    hi = x.astype(BF16).astype(F32)
    mid = (x - hi).astype(BF16).astype(F32)
    lo = (x - hi - mid).astype(BF16).astype(F32)
    return hi, mid, lo


def _attention(x, attn_norm_g, w_in, cmp_pe_k, cmp_pe_v, cmp_w1_k, cmp_w2_k, cmp_w1_v, cmp_w2_v):
    B, S, D = x.shape
    wr, wt, wg = _prep_inproj(w_in)
    (mk, kc, vc, ks, kw, mqT, mvT, nqT, vsT, vwT, gT) = _inproj(
        x, attn_norm_g.reshape(1, D), wr, wt, wg, tm=512)
    moba_tabs, nsa_tabs = _attention_tables(S)
    o_moba = _moba(mqT, mk, mvT, *moba_tabs)
    wk, w2k, pek = _prep_compress(cmp_w1_k, cmp_w2_k, cmp_pe_k)
    wv, w2v, pev = _prep_compress(cmp_w1_v, cmp_w2_v, cmp_pe_v)
    nc = S // NSA_CMP_STRIDE
    kcmp, vcmpT = _compress(kc.reshape(B, nc, -1), vc.reshape(B, nc, -1), wk, wv.T, pek, pev, w2k, w2v.T)
    o_nsa = _nsa(nqT, kcmp, vcmpT, ks, vsT, kw, vwT, gT, *nsa_tabs)
    return o_moba, o_nsa


def _moe(x1, hn, e128, w128, w_up, b_up, w_down, b_down, final_norm_g):
    N, D = x1.shape
    rank128, cnt = _ranks(e128)
    counts = cnt[0, :N_EXPERTS].astype(I32)
    padded = (counts + MOE_ROWS - 1) // MOE_ROWS * MOE_ROWS
    pends = jnp.cumsum(padded)
    pstarts = pends - padded
    e4 = e128[:, :TOP_K]
    dest = pstarts[e4] + rank128[:, :TOP_K]
    dest2 = dest.reshape(N // ROUTE_TILE, ROUTE_TILE * TOP_K)
    n_blk = (N * TOP_K + N_EXPERTS * MOE_ROWS + MOE_ROWS - 1) // MOE_ROWS
    P = n_blk * MOE_ROWS
    blk_start = jnp.arange(n_blk, dtype=I32) * MOE_ROWS
    blk_e = jnp.minimum(jnp.sum((pends[None, :] <= blk_start[:, None]).astype(I32), axis=1), N_EXPERTS - 1)
    n_act = (pends[-1:] // MOE_ROWS).astype(I32)
    gap_lo = jnp.concatenate([pstarts + counts, pends[-1:]]).astype(I32)
    gap_hi = jnp.concatenate([pends, jnp.full((1,), P, I32)]).astype(I32)
    xrows = _dispatch(gap_lo, gap_hi, dest2, hn, P)
    bg = b_up[:, None, 0::2]
    bu = b_up[:, None, 1::2]
    yrows = _experts(blk_e, n_act, xrows, w_up, bg, bu, w_down.astype(BF16), b_down[:, None, :])
    return _combine(dest2, x1, w128, final_norm_g.reshape(1, D), yrows)


def kernel(x, attn_norm_g, w_in, cmp_pe_k, cmp_pe_v, cmp_w1_k, cmp_w2_k, cmp_w1_v, cmp_w2_v, w_out, ffn_norm_g, w_router, b_router, w_up, b_up, w_down, b_down, final_norm_g):
    B, S, D = x.shape
    assert attn_norm_g.shape[0] == 1, "single-layer kernel"
    o_moba, o_nsa = _attention(x, attn_norm_g[0], w_in[0], cmp_pe_k[0], cmp_pe_v[0],
                               cmp_w1_k[0], cmp_w2_k[0], cmp_w1_v[0], cmp_w2_v[0])
    N = B * S
    wr = jnp.pad(w_router[0], ((0, 0), (0, LANES - N_EXPERTS)))
    br = jnp.pad(b_router[0], (0, LANES - N_EXPERTS)).reshape(1, LANES)
    x1, hn, e128, w128 = _outproj(o_moba.reshape(N, -1), o_nsa.reshape(N, -1), x.reshape(N, D),
                                  w_out[0].astype(BF16), ffn_norm_g[0].reshape(1, D), wr, br, tm=512)
    out = _moe(x1, hn, e128, w128, w_up[0], b_up[0], w_down[0], b_down[0], final_norm_g)
    return out.reshape(B, S, D)
```

```python
import functools

import jax
import jax.numpy as jnp
import numpy as np
from jax import lax
from jax.experimental import pallas as pl
from jax.experimental.pallas import tpu as pltpu

F32 = jnp.float32
BF16 = jnp.bfloat16
I32 = jnp.int32

HEAD_DIM = 64
MOBA_HEADS = 8
NSA_HEADS = 8
NSA_KV_HEADS = 2
NSA_GROUP = NSA_HEADS // NSA_KV_HEADS
MOBA_BLOCK = 256
MOBA_TOPK = 3
NSA_CMP_LEN = 32
NSA_CMP_STRIDE = 16
NSA_SLC_BLOCK = 64
NSA_SLC_TOPN = 16
NSA_WINDOW = 512
NSA_BRANCHES = 3
N_EXPERTS = 32
TOP_K = 4
SWIGLU_LIMIT = 7.0
SWIGLU_ALPHA = 1.702
RMS_EPS = 1e-5
NEG_BIG = -1e30
LOG2E = 1.4426950408889634

LANES = 128
SUBLANES = 8
VMEM_LIMIT = 56 * 1024 * 1024

NSA_TQ = 256
NSA_CHAIN_LANES = 256
TILE_LIST_LEN = 64
DIAG_SLOT = TILE_LIST_LEN - 1
SLC_TILE = 256
WIN_KEYS = NSA_WINDOW + NSA_TQ
MOE_ROWS = 512
ROUTE_TILE = 1024
COMBINE_TILE = 512
RANK_TILE = 512
ROW_DMA_UNROLL = 8

NT_DIMS = (((1,), (1,)), ((), ()))


def _params(n_grid):
    return pltpu.CompilerParams(
        dimension_semantics=("arbitrary",) * n_grid,
        vmem_limit_bytes=VMEM_LIMIT,
    )


def _rmsnorm(x, g):
    return x * lax.rsqrt(jnp.mean(x * x, axis=-1, keepdims=True) + RMS_EPS) * g


def _inproj_kernel(x_ref, g_ref, wr_ref, wt_ref, wg_ref,
                   mk_ref, kc_ref, vc_ref, ks_ref, kw_ref,
                   mqT_ref, mvT_ref, nqT_ref, vsT_ref, vwT_ref, gT_ref):
    xn = _rmsnorm(x_ref[0], g_ref[...])
    xb = xn.astype(BF16)
    yr = jnp.dot(xb, wr_ref[...], preferred_element_type=F32)
    mk_ref[0] = yr[:, 0:512].astype(BF16)
    kc_ref[0] = yr[:, 512:640].astype(BF16)
    vc_ref[0] = yr[:, 640:768].astype(BF16)
    ks_ref[0] = yr[:, 768:896].astype(BF16)
    kw_ref[0] = yr[:, 896:1024].astype(BF16)
    yt = lax.dot_general(wt_ref[...], xb, NT_DIMS, preferred_element_type=F32)
    mqT_ref[0] = yt[0:512].astype(BF16)
    mvT_ref[0] = yt[512:1024].astype(BF16)
    nqT_ref[0] = yt[1024:1536].astype(BF16)
    vsT_ref[0] = yt[1536:1664].astype(BF16)
    vwT_ref[0] = yt[1664:1792].astype(BF16)
    gl = lax.dot_general(wg_ref[...], xn, NT_DIMS, precision=lax.Precision.HIGHEST,
                         preferred_element_type=F32)
    gT_ref[0] = jax.nn.sigmoid(gl)


def _inproj(x, g, wr, wt, wg, tm):
    B, S, D = x.shape
    grid = (B, S // tm)
    row = lambda w: pl.BlockSpec((1, tm, w), lambda b, i: (b, i, 0))
    col = lambda h: pl.BlockSpec((1, h, tm), lambda b, i: (b, 0, i))
    full = lambda a: pl.BlockSpec(a.shape, lambda b, i: (0,) * a.ndim)
    out_shape = [
        jax.ShapeDtypeStruct((B, S, 512), BF16),
        jax.ShapeDtypeStruct((B, S, 128), BF16),
        jax.ShapeDtypeStruct((B, S, 128), BF16),
        jax.ShapeDtypeStruct((B, S, 128), BF16),
        jax.ShapeDtypeStruct((B, S, 128), BF16),
        jax.ShapeDtypeStruct((B, 512, S), BF16),
        jax.ShapeDtypeStruct((B, 512, S), BF16),
        jax.ShapeDtypeStruct((B, 512, S), BF16),
        jax.ShapeDtypeStruct((B, 128, S), BF16),
        jax.ShapeDtypeStruct((B, 128, S), BF16),
        jax.ShapeDtypeStruct((B, 32, S), F32),
    ]
    out_specs = [row(512), row(128), row(128), row(128), row(128),
                 col(512), col(512), col(512), col(128), col(128), col(32)]
    return pl.pallas_call(
        _inproj_kernel, grid=grid,
        in_specs=[pl.BlockSpec((1, tm, D), lambda b, i: (b, i, 0)),
                  full(g), full(wr), full(wt), full(wg)],
        out_specs=out_specs, out_shape=out_shape,
        compiler_params=_params(2), name="inproj",
    )(x, g, wr, wt, wg)


ONES_ROWS = 16


def _softmax_stage(s, c, m):
    mt = jnp.max(s, axis=0, keepdims=True) - c
    m_new = jnp.maximum(m, mt)
    alpha = jnp.exp2(m - m_new)
    p = jnp.exp2(s - (m_new + c))
    return m_new, p.astype(BF16), alpha


def _pipelined_tiles(scores, values, offsets, s_ref, p_ref, first, n_ch, n_tiles, j_first):
    chains = range(n_ch)

    def qk_into(slot, j):
        sc = scores(j)
        for c in chains:
            s_ref[slot, c] = sc[c]

    def pv_from(slot, j, alphas, accs):
        out = []
        for c, vt in zip(chains, values(j)):
            vt1 = jnp.concatenate([vt, jnp.ones((ONES_ROWS, vt.shape[1]), BF16)], axis=0)
            out.append(alphas[c] * accs[c] + jnp.dot(vt1, p_ref[slot, c], preferred_element_type=F32))
        return out

    def softmax_into(slot, j, ms):
        cs = offsets(j)
        new = [_softmax_stage(s_ref[slot, c], cs[c], ms[c]) for c in chains]
        for c in chains:
            p_ref[slot, c] = new[c][1]
        return [n[0] for n in new], [n[2] for n in new]

    qk_into(0, 0)
    first = first()
    for c in chains:
        p_ref[1, c] = first[c][1]

    def pair(i, carry):
        ms, alphas, accs, j_prev = carry
        for slot in range(2):
            t = 2 * i + slot
            qk_into(1 - slot, t + 1)
            accs = pv_from(1 - slot, j_prev, alphas, accs)
            ms, alphas = softmax_into(slot, t, ms)
            j_prev = t
        return ms, alphas, accs, j_prev

    n_q = first[0][0].shape[1]
    init = ([f[0] for f in first], [f[2] for f in first],
            [jnp.zeros((HEAD_DIM + ONES_ROWS, n_q), F32)] * n_ch, j_first)
    _, alphas, accs, j_last = lax.fori_loop(0, (n_tiles + 1) // 2, pair, init)
    accs = pv_from(1, j_last, alphas, accs)
    return [accs[c][:HEAD_DIM] / jnp.maximum(accs[c][HEAD_DIM:HEAD_DIM + 1], 1e-30) for c in chains]


def _moba_kernel(qT_ref, k_ref, vT_ref, aug_ref, srow_ref, sl_ref, o_ref,
                 kmean_ref, kparts_ref, s_ref, p_ref, *, nb, nbp, topk):
    qi = pl.program_id(2)
    blk = MOBA_BLOCK

    @pl.when(qi == 0)
    def _():
        kmean_ref[...] = jnp.zeros(kmean_ref.shape, F32)

        def body(n, carry):
            kb = k_ref[0, pl.ds(pl.multiple_of(n * blk, blk), blk), :].astype(F32)
            kmean_ref[pl.ds(n, 1), :] = jnp.mean(kb, axis=0, keepdims=True)
            return carry
        lax.fori_loop(0, nb, body, 0)
        km = kmean_ref[...]
        head = lax.broadcasted_iota(I32, km.shape, 1) >> 6
        km2 = jnp.concatenate([jnp.where(head == h, km, 0.0) for h in range(2)], axis=0)
        hi = km2.astype(BF16)
        mid = (km2 - hi.astype(F32)).astype(BF16)
        lo = (km2 - hi.astype(F32) - mid.astype(F32)).astype(BF16)
        kparts_ref[0] = hi
        kparts_ref[1] = mid
        kparts_ref[2] = lo

    qT = qT_ref[0]
    row = lax.broadcasted_iota(I32, qT.shape, 0)
    qpad = [jnp.where((row >> 6) == h, qT, jnp.zeros_like(qT)) for h in range(2)]

    gates = (jnp.dot(kparts_ref[0], qT, preferred_element_type=F32)
             + jnp.dot(kparts_ref[1], qT, preferred_element_type=F32)
             + jnp.dot(kparts_ref[2], qT, preferred_element_type=F32))
    bidx = lax.broadcasted_iota(I32, (nbp, blk), 0)
    rhs = []
    for h in range(2):
        gate = gates[h * nbp:(h + 1) * nbp]
        gsc = jnp.where(bidx < qi, gate, -jnp.inf)
        bias = jnp.full((nbp, blk), NEG_BIG, F32)
        for _ in range(topk):
            mx = jnp.max(gsc, axis=0, keepdims=True)
            idx = jnp.min(jnp.where(gsc == mx, bidx, nbp), axis=0, keepdims=True)
            pick = jnp.logical_and(bidx == idx, mx > -jnp.inf)
            bias = jnp.where(pick, 0.0, bias)
            gsc = jnp.where(pick, -jnp.inf, gsc)
        pad = jnp.zeros((2 * HEAD_DIM - nbp - 16, blk), BF16)
        rhs.append(jnp.concatenate([qpad[h], bias.astype(BF16), srow_ref[h], pad], axis=0))
    rhs_own = [jnp.concatenate([qpad[h], jnp.zeros((nbp, blk), BF16), srow_ref[h], pad], axis=0)
               for h in range(2)]

    def scores(j, a, weights):
        k0 = pl.multiple_of(j * blk, blk)
        lhs = jnp.concatenate([k_ref[0, pl.ds(k0, blk), :], aug_ref[a]], axis=1)
        return [jnp.dot(lhs, weights[h], preferred_element_type=F32) for h in range(2)]

    def values(j):
        k0 = pl.multiple_of(j * blk, blk)
        return [vT_ref[0, h * HEAD_DIM:(h + 1) * HEAD_DIM, pl.ds(k0, blk)] for h in range(2)]

    def offsets(j):
        dq = ((qi - j) * blk).astype(F32)
        return [sl_ref[h] * dq for h in range(2)]

    ik = lax.broadcasted_iota(I32, (blk, blk), 0)
    iq = lax.broadcasted_iota(I32, (blk, blk), 1)
    s_own = [jnp.where(ik <= iq, s, NEG_BIG) for s in scores(qi, nb, rhs_own)]
    m0 = jnp.full((1, blk), NEG_BIG, F32)

    def first():
        return [_softmax_stage(s_own[h], jnp.zeros((1, blk), F32), m0) for h in range(2)]

    outs = _pipelined_tiles(lambda j: scores(jnp.minimum(j, nb - 1), jnp.minimum(j, nb - 1), rhs),
                            values, offsets, s_ref, p_ref, first, 2, qi, qi)
    o_ref[0] = jnp.concatenate(outs, axis=0).T.astype(BF16)


def _moba(mqT, mk, mvT, aug, srow, sl):
    B, _, S = mqT.shape
    blk = MOBA_BLOCK
    nb = S // blk
    topk = min(MOBA_TOPK, nb)
    nbp = -(-nb // 16) * 16
    grid = (B, MOBA_HEADS // 2, nb)
    return pl.pallas_call(
        functools.partial(_moba_kernel, nb=nb, nbp=nbp, topk=topk), grid=grid,
        in_specs=[
            pl.BlockSpec((1, 128, blk), lambda b, p, i: (b, p, i)),
            pl.BlockSpec((1, S, 128), lambda b, p, i: (b, 0, p)),
            pl.BlockSpec((1, 128, S), lambda b, p, i: (b, p, 0)),
            pl.BlockSpec(aug.shape, lambda b, p, i: (0, 0, 0)),
            pl.BlockSpec((2, 16, blk), lambda b, p, i: (p, 0, 0)),
            pl.BlockSpec((2, 1, blk), lambda b, p, i: (p, 0, 0)),
        ],
        out_specs=pl.BlockSpec((1, blk, 128), lambda b, p, i: (b, i, p)),
        out_shape=jax.ShapeDtypeStruct((B, S, 512), BF16),
        scratch_shapes=[
            pltpu.VMEM((nbp, 128), F32),
            pltpu.VMEM((3, 2 * nbp, 128), BF16),
            pltpu.VMEM((2, 2, blk, blk), F32),
            pltpu.VMEM((2, 2, blk, blk), BF16),
        ],
        compiler_params=_params(3), name="moba",
    )(mqT, mk, mvT, aug, srow, sl)


def _compress_kernel(kc_ref, vc_ref, wk_ref, wvT_ref, pek_ref, pev_ref, w2k_ref, w2vT_ref,
                     kcmp_ref, vcmpT_ref):
    nc = kc_ref.shape[1]

    wk = wk_ref[...]
    ab = jnp.dot(kc_ref[0], wk, preferred_element_type=F32)
    pt = (jnp.dot(pek_ref[0], wk[:, 0:128].astype(F32), preferred_element_type=F32)
          + jnp.dot(pek_ref[1], wk[:, 128:256].astype(F32), preferred_element_type=F32))
    pre = ab[:, 0:128] + pltpu.roll(ab[:, 128:256], nc - 1, 0) + pt[0:1]
    hid = jax.nn.gelu(pre)
    kcmp_ref[0] = jnp.dot(hid.astype(BF16), w2k_ref[...], preferred_element_type=F32).astype(BF16)

    wvT = wvT_ref[...]
    abT = lax.dot_general(wvT, vc_ref[0], NT_DIMS, preferred_element_type=F32)
    ptT = (lax.dot_general(wvT[0:128].astype(F32), pev_ref[0], NT_DIMS, preferred_element_type=F32)
           + lax.dot_general(wvT[128:256].astype(F32), pev_ref[1], NT_DIMS, preferred_element_type=F32))
    preT = abT[0:128] + pltpu.roll(abT[128:256], nc - 1, 1) + ptT[:, 0:1]
    hidT = jax.nn.gelu(preT)
    vcmpT_ref[0] = jnp.dot(w2vT_ref[...], hidT.astype(BF16), preferred_element_type=F32).astype(BF16)


def _compress(kc2, vc2, wk, wvT, pek, pev, w2k, w2vT):
    B, nc, _ = kc2.shape
    full = lambda a: pl.BlockSpec(a.shape, lambda b: (0,) * a.ndim)
    blk = pl.BlockSpec((1, nc, kc2.shape[2]), lambda b: (b, 0, 0))
    return pl.pallas_call(
        _compress_kernel, grid=(B,),
        in_specs=[blk, blk, full(wk), full(wvT), full(pek), full(pev), full(w2k), full(w2vT)],
        out_specs=[pl.BlockSpec((1, nc, 128), lambda b: (b, 0, 0)),
                   pl.BlockSpec((1, 128, nc), lambda b: (b, 0, 0))],
        out_shape=[jax.ShapeDtypeStruct((B, nc, 128), BF16),
                   jax.ShapeDtypeStruct((B, 128, nc), BF16)],
        compiler_params=_params(1), name="nsa_compress",
    )(kc2, vc2, wk, wvT, pek, pev, w2k, w2vT)


def _nsa_kernel(qT_ref, kcmp_ref, vcmpT_ref, ks_ref, vsT_ref, kw_ref, vwT_ref,
                g_ref, sl_ref, srow_ref, auga_ref, augb_ref, augc_ref, tblc_ref, bw_ref, o_ref,
                s_ref, p_ref, pc_ref, flagv_ref, flags_ref, list_ref, fsem, *, n_slc, topn):
    g = pl.program_id(1)
    qi = pl.program_id(2)
    tq = NSA_TQ
    hg = NSA_GROUP
    wl = hg * tq
    q0 = qi * tq

    q4 = qT_ref[0]
    qT = jnp.concatenate([q4[h * HEAD_DIM:(h + 1) * HEAD_DIM] for h in range(hg)], axis=1)
    qT2 = jnp.concatenate([qT, qT], axis=0)
    rowi = lax.broadcasted_iota(I32, qT2.shape, 0)
    qpad = jnp.where((rowi >> 6) == g, qT2, jnp.zeros_like(qT2))
    slope = sl_ref[0]
    lane = lax.broadcasted_iota(I32, (1, wl), 1)
    t_q = q0 + (lane & (tq - 1))

    nc = kcmp_ref.shape[1]
    rhs_top = jnp.where((rowi >> 6) == g, qT2, srow_ref[0])
    mine_c = (lax.broadcasted_iota(I32, (nc, LANES), 1) >> 6) == g
    lhs_c = jnp.where(mine_c, kcmp_ref[0], augc_ref[0])
    first_c = pl.multiple_of(nc - qi * (tq // NSA_CMP_STRIDE), 8)
    z = jnp.dot(lhs_c, rhs_top, preferred_element_type=F32) + tblc_ref[pl.ds(first_c, nc), :]
    mx = jnp.max(z, axis=0, keepdims=True)
    e = jnp.exp2(z - mx)
    den = jnp.maximum(jnp.sum(e, axis=0, keepdims=True), 1e-30)
    p = e * jnp.where(t_q >= NSA_CMP_LEN - 1, 1.0 / den, 0.0)
    o_c = jnp.dot(vcmpT_ref[0], p.astype(BF16), preferred_element_type=F32)

    pc = p[:, 0:tq]
    for h in range(1, hg):
        pc = pc + p[:, h * tq:(h + 1) * tq]
    n_lc = tq // LANES
    for c in range(n_lc):
        pc_ref[c] = pc[:, c * LANES:(c + 1) * LANES]
    su = NSA_SLC_BLOCK // NSA_CMP_STRIDE
    x = [jnp.concatenate([pc_ref[c, pl.ds(k, n_slc, stride=su), :] for c in range(n_lc)], axis=1)
         for k in range(su)]
    jb = lax.broadcasted_iota(I32, (n_slc, tq), 0)
    prev = jnp.where(jb == 0, 0.0, pltpu.roll(x[3], 1, 0))
    imp = 2.0 * (x[0] + x[1] + x[2]) + x[3] + prev
    cur = (q0 + lax.broadcasted_iota(I32, (1, tq), 1)) >> 6
    allowed = jb <= cur
    forced = jnp.logical_or(jb == 0, jnp.logical_or(jb == cur, jb == cur - 1))
    bias = jnp.where(jnp.logical_and(allowed, forced), 0.0, NEG_BIG)
    sc = jnp.where(jnp.logical_and(allowed, jnp.logical_not(forced)), imp, -1.0)
    for _ in range(topn - 3):
        smx = jnp.max(sc, axis=0, keepdims=True)
        idx = jnp.min(jnp.where(sc == smx, jb, n_slc), axis=0, keepdims=True)
        pick = jnp.logical_and(jb == idx, smx >= 0.0)
        bias = jnp.where(pick, 0.0, bias)
        sc = jnp.where(pick, -1.0, sc)

    n_tiles = augb_ref.shape[0]
    jd = lax.div(q0, SLC_TILE)
    chosen = jnp.where(bias == 0.0, 1.0, 0.0).astype(BF16)
    per_blk = lax.dot_general(jnp.ones((SUBLANES, tq), BF16), chosen, NT_DIMS,
                              preferred_element_type=F32)
    per_shift = (SLC_TILE // NSA_SLC_BLOCK).bit_length() - 1
    in_tile = (lax.broadcasted_iota(I32, (n_slc, LANES), 0) >> per_shift) == lax.broadcasted_iota(
        I32, (n_slc, LANES), 1)
    per_tile = jnp.dot(per_blk.astype(BF16), jnp.where(in_tile, 1.0, 0.0).astype(BF16),
                       preferred_element_type=F32)
    is_past = lax.broadcasted_iota(I32, (SUBLANES, LANES), 1) < jd
    flagv_ref[...] = jnp.where(jnp.logical_and(per_tile > 0.0, is_past), 1, 0).astype(I32)
    flag_copy = pltpu.make_async_copy(flagv_ref, flags_ref, fsem)
    flag_copy.start()

    start = pl.multiple_of(jnp.maximum(q0 - NSA_WINDOW, 0), tq)
    first_w = pl.multiple_of(NSA_WINDOW - (q0 - start), tq)
    kt = kw_ref[0, pl.ds(start, WIN_KEYS), :]
    z = jnp.dot(kt, qpad, preferred_element_type=F32) + bw_ref[0, pl.ds(first_w, WIN_KEYS), :]
    mx = jnp.max(z, axis=0, keepdims=True)
    p = jnp.exp2(z - mx)
    den = jnp.maximum(jnp.sum(p, axis=0, keepdims=True), 1e-30)
    o_w = jnp.dot(vwT_ref[0, :, pl.ds(start, WIN_KEYS)], p.astype(BF16),
                  preferred_element_type=F32) / den

    flag_copy.wait()
    for i in range(TILE_LIST_LEN):
        list_ref[i] = 0

    def add_tile(t, n):
        list_ref[n] = t
        return n + flags_ref[0, t]
    n_used = lax.fori_loop(0, n_tiles, add_tile, 0)
    list_ref[DIAG_SLOT] = jd

    wc = NSA_CHAIN_LANES
    n_ch = wl // wc

    def lane_split(a):
        return [a[:, c * wc:(c + 1) * wc] for c in range(n_ch)]

    if n_slc < LANES:
        bias = jnp.concatenate([bias, jnp.zeros((LANES - n_slc, tq), F32)], axis=0)
    bias4 = jnp.concatenate([bias.astype(BF16)] * hg, axis=1)
    rhs = jnp.concatenate([rhs_top, bias4], axis=0)
    mine = (lax.broadcasted_iota(I32, (SLC_TILE, LANES), 1) >> 6) == g

    def scores(j, null):
        k0 = pl.multiple_of(j * SLC_TILE, SLC_TILE)
        kt = ks_ref[0, pl.ds(k0, SLC_TILE), :]
        lhs = jnp.concatenate([jnp.where(mine, kt, auga_ref[null]), augb_ref[j]], axis=1)
        return jnp.dot(lhs, rhs, preferred_element_type=F32)

    def values(i):
        k0 = pl.multiple_of(list_ref[i] * SLC_TILE, SLC_TILE)
        return [vsT_ref[0, :, pl.ds(k0, SLC_TILE)]] * n_ch

    def offsets(i):
        return lane_split(slope * (q0 - list_ref[i] * SLC_TILE).astype(F32))

    def past_scores(i):
        return lane_split(scores(list_ref[jnp.minimum(i, n_tiles - 1)], (i >= n_used).astype(I32)))

    t_k = jd * SLC_TILE + lax.broadcasted_iota(I32, (SLC_TILE, wl), 0)
    s_diag = lane_split(jnp.where(t_k <= t_q, scores(jd, 0), NEG_BIG))
    c_diag = offsets(DIAG_SLOT)

    def first():
        return [_softmax_stage(s_diag[c], c_diag[c], jnp.full((1, wc), NEG_BIG, F32))
                for c in range(n_ch)]

    o_s = jnp.concatenate(
        _pipelined_tiles(past_scores, values, offsets, s_ref, p_ref, first, n_ch, n_used, DIAG_SLOT),
        axis=1)

    gt = g_ref[0]

    def gate_row(br):
        return jnp.concatenate([gt[br * hg + h:br * hg + h + 1] for h in range(hg)], axis=1)

    o = gate_row(0) * o_c + gate_row(1) * o_s + gate_row(2) * o_w
    o4 = jnp.concatenate([o[:, h * tq:(h + 1) * tq] for h in range(hg)], axis=0)
    o_ref[0] = o4.T.astype(BF16)


def _nsa(nqT, kcmp, vcmpT, ks, vsT, kw, vwT, gT, sl, srow, auga, augb, augc, tblc, bw):
    B, _, S = nqT.shape
    tq = NSA_TQ
    nc = kcmp.shape[1]
    n_slc = S // NSA_SLC_BLOCK
    topn = min(NSA_SLC_TOPN, n_slc)
    wl = NSA_GROUP * tq
    grid = (B, NSA_KV_HEADS, S // tq)
    return pl.pallas_call(
        functools.partial(_nsa_kernel, n_slc=n_slc, topn=topn), grid=grid,
        in_specs=[
            pl.BlockSpec((1, NSA_GROUP * HEAD_DIM, tq), lambda b, g, i: (b, g, i)),
            pl.BlockSpec((1, nc, 128), lambda b, g, i: (b, 0, 0)),
            pl.BlockSpec((1, HEAD_DIM, nc), lambda b, g, i: (b, g, 0)),
            pl.BlockSpec((1, S, 128), lambda b, g, i: (b, 0, 0)),
            pl.BlockSpec((1, HEAD_DIM, S), lambda b, g, i: (b, g, 0)),
            pl.BlockSpec((1, S, 128), lambda b, g, i: (b, 0, 0)),
            pl.BlockSpec((1, HEAD_DIM, S), lambda b, g, i: (b, g, 0)),
            pl.BlockSpec((1, 16, tq), lambda b, g, i: (b, g, i)),
            pl.BlockSpec((1, 1, wl), lambda b, g, i: (g, 0, 0)),
            pl.BlockSpec((1, 2 * HEAD_DIM, wl), lambda b, g, i: (g, 0, 0)),
            pl.BlockSpec((2, SLC_TILE, LANES), lambda b, g, i: (g, 0, 0)),
            pl.BlockSpec(augb.shape, lambda b, g, i: (0, 0, 0)),
            pl.BlockSpec((1, nc, LANES), lambda b, g, i: (g, 0, 0)),
            pl.BlockSpec(tblc.shape, lambda b, g, i: (0, 0)),
            pl.BlockSpec((1,) + bw.shape[1:], lambda b, g, i: (g, 0, 0)),
        ],
        out_specs=pl.BlockSpec((1, tq, NSA_GROUP * HEAD_DIM), lambda b, g, i: (b, i, g)),
        out_shape=jax.ShapeDtypeStruct((B, S, 512), BF16),
        scratch_shapes=[
            pltpu.VMEM((2, wl // NSA_CHAIN_LANES, SLC_TILE, NSA_CHAIN_LANES), F32),
            pltpu.VMEM((2, wl // NSA_CHAIN_LANES, SLC_TILE, NSA_CHAIN_LANES), BF16),
            pltpu.VMEM((tq // LANES, nc, LANES), F32),
            pltpu.VMEM((SUBLANES, LANES), I32),
            pltpu.SMEM((SUBLANES, LANES), I32),
            pltpu.SMEM((TILE_LIST_LEN,), I32),
            pltpu.SemaphoreType.DMA,
        ],
        compiler_params=_params(3), name="nsa",
    )(nqT, kcmp, vcmpT, ks, vsT, kw, vwT, gT, sl, srow, auga, augb, augc, tblc, bw)


def _outproj_kernel(om_ref, on_ref, x_ref, wo_ref, g_ref, wr_ref, br_ref,
                    x1_ref, hn_ref, e_ref, w_ref):
    attn = (jnp.dot(om_ref[...], wo_ref[0:512, :], preferred_element_type=F32)
            + jnp.dot(on_ref[...], wo_ref[512:1024, :], preferred_element_type=F32))
    x1 = x_ref[...] + attn
    x1_ref[...] = x1
    hn = _rmsnorm(x1, g_ref[...])
    _store_token_tiles(hn_ref, hn)
    logits = jnp.dot(hn, wr_ref[...], precision=lax.Precision.HIGHEST,
                     preferred_element_type=F32) + br_ref[...]
    tm = logits.shape[0]
    lane = lax.broadcasted_iota(I32, (tm, LANES), 1)
    sc = jnp.where(lane < N_EXPERTS, logits, -jnp.inf)
    e_out = jnp.zeros((tm, LANES), I32)
    vals = []
    for k in range(TOP_K):
        mx = jnp.max(sc, axis=1, keepdims=True)
        idx = jnp.min(jnp.where(sc == mx, lane, LANES), axis=1, keepdims=True)
        e_out = jnp.where(lane == k, idx, e_out)
        sc = jnp.where(lane == idx, -jnp.inf, sc)
        vals.append(mx)
    ex = [jnp.exp(v - vals[0]) for v in vals]
    den = ex[0] + ex[1] + ex[2] + ex[3]
    w_out = jnp.zeros((tm, LANES), F32)
    for k in range(TOP_K):
        w_out = jnp.where(lane == k, ex[k] / den, w_out)
    e_ref[...] = e_out
    w_ref[...] = w_out


def _outproj(om, on, x, wo, g, wr, br, tm):
    N, D = x.shape
    full = lambda a: pl.BlockSpec(a.shape, lambda i: (0,) * a.ndim)
    row = lambda w: pl.BlockSpec((tm, w), lambda i: (i, 0))
    return pl.pallas_call(
        _outproj_kernel, grid=(N // tm,),
        in_specs=[row(512), row(512), row(D), full(wo), full(g), full(wr), full(br)],
        out_specs=[row(D), pl.BlockSpec((tm * SUBLANES, LANES), lambda i: (i, 0)), row(LANES), row(LANES)],
        out_shape=[jax.ShapeDtypeStruct((N, D), F32), jax.ShapeDtypeStruct((N * SUBLANES, LANES), F32),
                   jax.ShapeDtypeStruct((N, LANES), I32), jax.ShapeDtypeStruct((N, LANES), F32)],
        compiler_params=_params(1), name="outproj_router",
    )(om, on, x, wo, g, wr, br)


def _rank_kernel(e_ref, rank_ref, cnt_ref, base_ref):
    i = pl.program_id(0)
    T = e_ref.shape[0]

    @pl.when(i == 0)
    def _():
        base_ref[...] = jnp.zeros(base_ref.shape, F32)

    e = e_ref[...]
    lane = lax.broadcasted_iota(I32, (T, LANES), 1)
    tril = jnp.where(lax.broadcasted_iota(I32, (T, T), 0) >= lax.broadcasted_iota(I32, (T, T), 1),
                     1.0, 0.0).astype(BF16)
    out = jnp.zeros((T, LANES), I32)
    for k in range(TOP_K):
        hit = lane == e[:, k:k + 1]
        oh = jnp.where(hit, 1.0, 0.0)
        cum = jnp.dot(tril, oh.astype(BF16), preferred_element_type=F32)
        base = base_ref[0:1, :]
        r = jnp.sum(jnp.where(hit, cum - 1.0 + base, 0.0), axis=1, keepdims=True)
        out = jnp.where(lane == k, r.astype(I32), out)
        base_ref[...] = base_ref[...] + jnp.sum(oh, axis=0, keepdims=True)
    rank_ref[...] = out
    cnt_ref[...] = base_ref[...]


def _ranks(e128):
    N = e128.shape[0]
    T = RANK_TILE
    return pl.pallas_call(
        _rank_kernel, grid=(N // T,),
        in_specs=[pl.BlockSpec((T, LANES), lambda i: (i, 0))],
        out_specs=[pl.BlockSpec((T, LANES), lambda i: (i, 0)),
                   pl.BlockSpec((8, LANES), lambda i: (0, 0))],
        out_shape=[jax.ShapeDtypeStruct((N, LANES), I32),
                   jax.ShapeDtypeStruct((8, LANES), F32)],
        scratch_shapes=[pltpu.VMEM((8, LANES), F32)],
        compiler_params=_params(1), name="route_ranks",
    )(e128)


def _row_copy(src, dst, i_src, i_dst, sem):
    return pltpu.make_async_copy(src.at[pl.ds(pl.multiple_of(i_src * SUBLANES, SUBLANES), SUBLANES)],
                                 dst.at[pl.ds(pl.multiple_of(i_dst * SUBLANES, SUBLANES), SUBLANES)], sem)


def _store_token_tiles(ref, x):
    rows = x.shape[0]
    for c in range(SUBLANES):
        ref[pl.ds(c, rows, stride=SUBLANES), :] = x[:, c * LANES:(c + 1) * LANES]


def _load_token_tiles(ref, rows):
    return jnp.concatenate([ref[pl.ds(c, rows, stride=SUBLANES), :] for c in range(SUBLANES)], axis=1)


def _dispatch_kernel(gap_lo_ref, gap_hi_ref, dest_hbm, hp_ref, out_hbm, idx_ref, zero_ref,
                     isem, sem, zsem, bsem):
    i = pl.program_id(0)
    T = ROUTE_TILE

    @pl.when(i == 0)
    def _():
        zero_ref[...] = jnp.zeros(zero_ref.shape, F32)
        n_exp = gap_lo_ref.shape[0] - 1

        def for_each_pad_row(start):
            def gap(e, carry):
                lo, hi = gap_lo_ref[e], gap_hi_ref[e]
                for par in range(2):
                    def row(r, c):
                        copy = _row_copy(zero_ref, out_hbm, 0, lo + 2 * r + par, zsem)
                        if start:
                            copy.start(priority=par)
                        else:
                            copy.wait()
                        return c
                    lax.fori_loop(0, (hi - lo + 1 - par) // 2, row, carry)
                return carry
            lax.fori_loop(0, n_exp, gap, 0)

        def for_each_tail_block(start):
            def blk(b, carry):
                rows = pl.ds(pl.multiple_of(b * (MOE_ROWS * SUBLANES), MOE_ROWS * SUBLANES), MOE_ROWS * SUBLANES)
                copy = pltpu.make_async_copy(zero_ref, out_hbm.at[rows], bsem)
                if start:
                    copy.start()
                else:
                    copy.wait()
                return carry
            lax.fori_loop(gap_lo_ref[n_exp] // MOE_ROWS, gap_hi_ref[n_exp] // MOE_ROWS, blk, 0)

        for_each_pad_row(True)
        for_each_tail_block(True)
        for_each_pad_row(False)
        for_each_tail_block(False)

    cp = pltpu.make_async_copy(dest_hbm.at[i], idx_ref, isem)
    cp.start()
    cp.wait()

    def issue(t, carry):
        for k in range(TOP_K):
            _row_copy(hp_ref, out_hbm, t, idx_ref[t * TOP_K + k], sem).start(priority=k % 2)
        return carry
    lax.fori_loop(0, T, issue, 0, unroll=ROW_DMA_UNROLL)

    def drain(t, carry):
        for k in range(TOP_K):
            _row_copy(hp_ref, out_hbm, 0, 0, sem).wait()
        return carry
    lax.fori_loop(0, T, drain, 0, unroll=ROW_DMA_UNROLL)


def _dispatch(gap_lo, gap_hi, dest2, hp, n_rows):
    nsteps = dest2.shape[0]
    T = ROUTE_TILE
    grid_spec = pltpu.PrefetchScalarGridSpec(
        num_scalar_prefetch=2, grid=(nsteps,),
        in_specs=[pl.BlockSpec(memory_space=pl.ANY),
                  pl.BlockSpec((T * SUBLANES, LANES), lambda i, lo, hi: (i, 0))],
        out_specs=pl.BlockSpec(memory_space=pl.ANY),
        scratch_shapes=[pltpu.SMEM((T * TOP_K,), I32), pltpu.VMEM((MOE_ROWS * SUBLANES, LANES), F32),
                        pltpu.SemaphoreType.DMA, pltpu.SemaphoreType.DMA, pltpu.SemaphoreType.DMA,
                        pltpu.SemaphoreType.DMA],
    )
    return pl.pallas_call(
        _dispatch_kernel, grid_spec=grid_spec,
        out_shape=jax.ShapeDtypeStruct((n_rows * SUBLANES, LANES), F32),
        compiler_params=_params(1), name="moe_dispatch",
    )(gap_lo, gap_hi, dest2, hp)


W_PREP_COLS = 256


def _expert_kernel(be_ref, na_ref, x_ref, wup_ref, bg_ref, bu_ref, wd_ref, bd_ref, y_ref,
                   wg_ref, wu_ref, wt_ref):
    b = pl.program_id(0)
    active = b < na_ref[0]
    new_expert = jnp.logical_or(b == 0, be_ref[b] != be_ref[jnp.maximum(b - 1, 0)])

    @pl.when(jnp.logical_and(active, new_expert))
    def _():
        half = W_PREP_COLS // 2
        n_lc = wt_ref.shape[0]

        def every_other(first):
            return jnp.concatenate(
                [wt_ref[c, pl.ds(first, half, stride=2), :] for c in range(n_lc)], axis=1).astype(BF16)

        for ch in range(wup_ref.shape[2] // W_PREP_COLS):
            panel = wup_ref[0, :, ch * W_PREP_COLS:(ch + 1) * W_PREP_COLS].T
            for c in range(n_lc):
                wt_ref[c] = panel[:, c * LANES:(c + 1) * LANES]
            wg_ref[ch * half:(ch + 1) * half, :] = every_other(0)
            wu_ref[ch * half:(ch + 1) * half, :] = every_other(1)

    @pl.when(active)
    def _():
        xb = _load_token_tiles(x_ref, MOE_ROWS).astype(BF16)
        gg = lax.dot_general(xb, wg_ref[...], NT_DIMS, preferred_element_type=F32) + bg_ref[0]
        uu = lax.dot_general(xb, wu_ref[...], NT_DIMS, preferred_element_type=F32) + bu_ref[0]
        gg = jnp.minimum(gg, SWIGLU_LIMIT)
        uu = jnp.clip(uu, -SWIGLU_LIMIT, SWIGLU_LIMIT)
        a = gg * jax.nn.sigmoid(SWIGLU_ALPHA * gg) * (uu + 1.0)
        _store_token_tiles(y_ref, jnp.dot(a.astype(BF16), wd_ref[0], preferred_element_type=F32) + bd_ref[0])

    @pl.when(jnp.logical_not(active))
    def _():
        y_ref[...] = jnp.zeros(y_ref.shape, F32)


def _experts(blk_e, n_act, xrows, w_up, bg, bu, wd, bd):
    _, D, F2 = w_up.shape
    F = F2 // 2
    assert D == SUBLANES * LANES
    P = xrows.shape[0] // SUBLANES
    n_blk = P // MOE_ROWS
    rows_spec = pl.BlockSpec((MOE_ROWS * SUBLANES, LANES), lambda b, be, na: (b, 0))
    wspec = lambda r, c: pl.BlockSpec((1, r, c), lambda b, be, na: (be[b], 0, 0))
    grid_spec = pltpu.PrefetchScalarGridSpec(
        num_scalar_prefetch=2, grid=(n_blk,),
        in_specs=[rows_spec, wspec(D, F2), wspec(1, F), wspec(1, F), wspec(F, D), wspec(1, D)],
        out_specs=rows_spec,
        scratch_shapes=[pltpu.VMEM((F, D), BF16), pltpu.VMEM((F, D), BF16),
                        pltpu.VMEM((D // LANES, W_PREP_COLS, LANES), F32)],
    )
    return pl.pallas_call(
        _expert_kernel, grid_spec=grid_spec,
        out_shape=jax.ShapeDtypeStruct(xrows.shape, F32),
        compiler_params=_params(1), name="moe_experts",
    )(blk_e, n_act, xrows, w_up, bg, bu, wd, bd)


def _combine_kernel(dest_hbm, x1_ref, w_ref, g_ref, y_hbm, o_ref, idx_ref, buf_ref, isem, sems):
    i = pl.program_id(0)
    T = COMBINE_TILE
    slot = i % 2

    def gather(tile, into):
        cp = pltpu.make_async_copy(dest_hbm.at[tile], idx_ref, isem)
        cp.start()
        cp.wait()

        def issue(t, carry):
            for k in range(TOP_K):
                _row_copy(y_hbm, buf_ref.at[into, k], idx_ref[t * TOP_K + k], t,
                          sems.at[into]).start(priority=k % 2)
            return carry
        lax.fori_loop(0, T, issue, 0, unroll=ROW_DMA_UNROLL)

    @pl.when(i == 0)
    def _():
        gather(0, 0)

    @pl.when(i + 1 < pl.num_programs(0))
    def _():
        gather(i + 1, 1 - slot)

    def drain(t, carry):
        for k in range(TOP_K):
            _row_copy(y_hbm, buf_ref.at[slot, k], 0, 0, sems.at[slot]).wait()
        return carry
    lax.fori_loop(0, T, drain, 0, unroll=ROW_DMA_UNROLL)

    x1 = x1_ref[...]
    w = w_ref[...]
    cols = []
    for c in range(SUBLANES):
        acc = x1[:, c * LANES:(c + 1) * LANES]
        for k in range(TOP_K):
            acc = acc + w[:, k:k + 1] * buf_ref[slot, k, pl.ds(c, T, stride=SUBLANES), :]
        cols.append(acc)
    o_ref[...] = _rmsnorm(jnp.concatenate(cols, axis=1), g_ref[...])


def _combine(dest, x1, w128, g, yrows):
    N, D = x1.shape
    T = COMBINE_TILE
    return pl.pallas_call(
        _combine_kernel, grid=(N // T,),
        in_specs=[pl.BlockSpec(memory_space=pl.ANY),
                  pl.BlockSpec((T, D), lambda i: (i, 0)),
                  pl.BlockSpec((T, LANES), lambda i: (i, 0)),
                  pl.BlockSpec(g.shape, lambda i: (0, 0)),
                  pl.BlockSpec(memory_space=pl.ANY)],
        out_specs=pl.BlockSpec((T, D), lambda i: (i, 0)),
        out_shape=jax.ShapeDtypeStruct((N, D), F32),
        scratch_shapes=[pltpu.SMEM((T * TOP_K,), I32),
                        pltpu.VMEM((2, TOP_K, T * SUBLANES, LANES), F32),
                        pltpu.SemaphoreType.DMA, pltpu.SemaphoreType.DMA((2,))],
        compiler_params=_params(1), name="moe_combine",
    )(dest.reshape(N // T, T * TOP_K), x1, w128, g, yrows)


def _alibi_slopes():
    n = MOBA_HEADS + NSA_HEADS
    s = jnp.exp2(-8.0 * jnp.arange(1, n + 1, dtype=F32) / n)
    return s[0::2], s[1::2]


def _prep_inproj(w_in):
    hd = HEAD_DIM
    sizes = [MOBA_HEADS * hd] * 3 + [NSA_HEADS * hd] + [NSA_KV_HEADS * hd] * 6 + [NSA_BRANCHES * NSA_HEADS]
    cuts = np.cumsum([0] + sizes)
    mq, mk, mv, nq, kc, vc, ks, vs, kw, vw, ng = [w_in[:, cuts[i]:cuts[i + 1]] for i in range(11)]
    qscale = (hd ** -0.5) * LOG2E
    wr = jnp.concatenate([mk, kc, vc, ks, kw], axis=1).astype(BF16)
    wt = jnp.concatenate([mq * qscale, mv, nq * qscale, vs, vw], axis=1).T.astype(BF16)
    ngr = ng.reshape(-1, NSA_KV_HEADS, NSA_GROUP, NSA_BRANCHES).transpose(1, 3, 2, 0)
    ngr = ngr.reshape(NSA_KV_HEADS, NSA_BRANCHES * NSA_GROUP, -1)
    wg = jnp.pad(ngr, ((0, 0), (0, 16 - NSA_BRANCHES * NSA_GROUP), (0, 0))).reshape(32, -1)
    return wr, wt, wg.astype(F32)


def _prep_compress(w1, w2, pe):
    hd, half = HEAD_DIM, NSA_CMP_STRIDE
    w1r = w1.reshape(2, half, hd, hd)
    eye = jnp.eye(NSA_KV_HEADS, dtype=w1.dtype)
    w = jnp.einsum('alde,gh->lgdahe', w1r, eye).reshape(half * NSA_KV_HEADS * hd, 2 * NSA_KV_HEADS * hd)
    w2b = jnp.einsum('de,gh->gdhe', w2, eye).reshape(NSA_KV_HEADS * hd, NSA_KV_HEADS * hd)
    per = pe.reshape(2, half, 1, hd)
    pe2 = jnp.broadcast_to(per, (2, half, NSA_KV_HEADS, hd)).reshape(2, 1, half * NSA_KV_HEADS * hd)
    pe2 = jnp.broadcast_to(pe2, (2, 8, pe2.shape[2]))
    return w.astype(BF16), w2b.astype(BF16), pe2.astype(F32)


def _attention_tables(S):
    moba_sl, nsa_sl = _alibi_slopes()
    moba_sl = moba_sl * LOG2E
    nsa_sl = nsa_sl * LOG2E
    blk = MOBA_BLOCK
    moba_row = jnp.broadcast_to(moba_sl[:, None, None], (MOBA_HEADS, 1, blk))
    nb = S // blk
    nbp = -(-nb // 16) * 16
    col = jnp.arange(LANES)[None, None, :]
    tile = jnp.arange(nb + 1)[:, None, None]
    off = jnp.arange(blk, dtype=F32)[None, :, None]
    moba_aug = jnp.where(jnp.logical_and(col == tile, tile < nb), 1.0,
                         jnp.where(jnp.logical_and(col >= nbp, col < nbp + 3), off, 0.0)).astype(BF16)
    parts = jnp.stack(list(_split3(moba_sl)) + [jnp.zeros_like(moba_sl)] * 13, axis=1)
    moba_srow = jnp.broadcast_to(parts[:, :, None], (MOBA_HEADS, 16, blk)).astype(BF16)

    wl = NSA_GROUP * NSA_TQ
    n_slc = S // NSA_SLC_BLOCK
    assert n_slc <= LANES, "block-choice rows must fit the spare contraction rows"
    nsa_row = jnp.repeat(nsa_sl.reshape(NSA_KV_HEADS, NSA_GROUP), NSA_TQ, axis=1)
    hi, mid, lo = [t[:, None, :] for t in _split3(nsa_row)]
    base = ((1 - jnp.arange(NSA_KV_HEADS)) * HEAD_DIM)[:, None, None]
    rows = jnp.arange(2 * HEAD_DIM)[None, :, None]
    nsa_srow = jnp.where(rows == base, hi, jnp.where(rows == base + 1, mid, jnp.where(
        rows == base + 2, lo, jnp.where(rows == base + 3, NEG_BIG, 0.0)))).astype(BF16)
    lane = jnp.arange(LANES)[None, None, None, :]
    base4 = base[:, None]
    null = jnp.arange(2, dtype=F32)[None, :, None, None]
    koff = jnp.arange(SLC_TILE, dtype=F32)[None, None, :, None]
    nsa_auga = jnp.where(jnp.logical_and(lane >= base4, lane < base4 + 3), koff,
                         jnp.where(lane == base4 + 3, null, 0.0))
    nsa_auga = nsa_auga.reshape(2 * NSA_KV_HEADS, SLC_TILE, LANES).astype(BF16)
    per = SLC_TILE // NSA_SLC_BLOCK
    tile = jnp.arange(S // SLC_TILE)[:, None, None]
    blk_of = tile * per + jnp.arange(SLC_TILE)[None, :, None] // NSA_SLC_BLOCK
    nsa_augb = (jnp.arange(LANES)[None, None, :] == blk_of).astype(BF16)

    nc = S // NSA_CMP_STRIDE
    ci = jnp.arange(nc)[None, :, None]
    lane3 = jnp.arange(LANES)[None, None, :]
    nsa_augc = jnp.where(jnp.logical_and(lane3 >= base + 4, lane3 < base + 7), (ci >> 1).astype(F32),
                         jnp.where(jnp.logical_and(lane3 >= base + 7, lane3 < base + 10),
                                   (ci & 1).astype(F32), 0.0)).astype(BF16)
    step2 = [t[:, None, :] for t in _split3(nsa_row * (2.0 * NSA_CMP_STRIDE))]
    step1 = [t[:, None, :] for t in _split3(nsa_row * (1.0 * NSA_CMP_STRIDE))]
    for k in range(3):
        nsa_srow = jnp.where(rows == base + 4 + k, step2[k].astype(BF16),
                             jnp.where(rows == base + 7 + k, step1[k].astype(BF16), nsa_srow))
    il = jnp.tile(jnp.arange(NSA_TQ), NSA_GROUP)[None, :]
    rel = (jnp.arange(2 * nc) - nc)[:, None]
    nsa_tblc = jnp.where(rel * NSA_CMP_STRIDE + (NSA_CMP_LEN - 1) <= il, 0.0, NEG_BIG).astype(F32)
    dist = (NSA_WINDOW + il - jnp.arange(NSA_WINDOW + WIN_KEYS)[:, None])[None]
    nsa_bw = jnp.where(jnp.logical_and(dist >= 0, dist < NSA_WINDOW),
                       -nsa_row[:, None, :] * dist.astype(F32), NEG_BIG)
    return ((moba_aug, moba_srow, moba_row),
            (nsa_row.reshape(NSA_KV_HEADS, 1, wl), nsa_srow, nsa_auga, nsa_augb, nsa_augc, nsa_tblc, nsa_bw))


def _split3(x):
    hi = x.astype(BF16).astype(F32)
    mid = (x - hi).astype(BF16).astype(F32)
    lo = (x - hi - mid).astype(BF16).astype(F32)
    return hi, mid, lo


def _attention(x, attn_norm_g, w_in, cmp_pe_k, cmp_pe_v, cmp_w1_k, cmp_w2_k, cmp_w1_v, cmp_w2_v):
    B, S, D = x.shape
    wr, wt, wg = _prep_inproj(w_in)
    (mk, kc, vc, ks, kw, mqT, mvT, nqT, vsT, vwT, gT) = _inproj(
        x, attn_norm_g.reshape(1, D), wr, wt, wg, tm=512)
    moba_tabs, nsa_tabs = _attention_tables(S)
    o_moba = _moba(mqT, mk, mvT, *moba_tabs)
    wk, w2k, pek = _prep_compress(cmp_w1_k, cmp_w2_k, cmp_pe_k)
    wv, w2v, pev = _prep_compress(cmp_w1_v, cmp_w2_v, cmp_pe_v)
    nc = S // NSA_CMP_STRIDE
    kcmp, vcmpT = _compress(kc.reshape(B, nc, -1), vc.reshape(B, nc, -1), wk, wv.T, pek, pev, w2k, w2v.T)
    o_nsa = _nsa(nqT, kcmp, vcmpT, ks, vsT, kw, vwT, gT, *nsa_tabs)
    return o_moba, o_nsa


def _moe(x1, hn, e128, w128, w_up, b_up, w_down, b_down, final_norm_g):
    N, D = x1.shape
    rank128, cnt = _ranks(e128)
    counts = cnt[0, :N_EXPERTS].astype(I32)
    padded = (counts + MOE_ROWS - 1) // MOE_ROWS * MOE_ROWS
    pends = jnp.cumsum(padded)
    pstarts = pends - padded
    e4 = e128[:, :TOP_K]
    dest = pstarts[e4] + rank128[:, :TOP_K]
    dest2 = dest.reshape(N // ROUTE_TILE, ROUTE_TILE * TOP_K)
    n_blk = (N * TOP_K + N_EXPERTS * MOE_ROWS + MOE_ROWS - 1) // MOE_ROWS
    P = n_blk * MOE_ROWS
    blk_start = jnp.arange(n_blk, dtype=I32) * MOE_ROWS
    blk_e = jnp.minimum(jnp.sum((pends[None, :] <= blk_start[:, None]).astype(I32), axis=1), N_EXPERTS - 1)
    n_act = (pends[-1:] // MOE_ROWS).astype(I32)
    gap_lo = jnp.concatenate([pstarts + counts, pends[-1:]]).astype(I32)
    gap_hi = jnp.concatenate([pends, jnp.full((1,), P, I32)]).astype(I32)
    xrows = _dispatch(gap_lo, gap_hi, dest2, hn, P)
    bg = b_up[:, None, 0::2]
    bu = b_up[:, None, 1::2]
    yrows = _experts(blk_e, n_act, xrows, w_up, bg, bu, w_down.astype(BF16), b_down[:, None, :])
    return _combine(dest, x1, w128, final_norm_g.reshape(1, D), yrows)


def kernel(x, attn_norm_g, w_in, cmp_pe_k, cmp_pe_v, cmp_w1_k, cmp_w2_k, cmp_w1_v, cmp_w2_v, w_out, ffn_norm_g, w_router, b_router, w_up, b_up, w_down, b_down, final_norm_g):
    B, S, D = x.shape
    assert attn_norm_g.shape[0] == 1, "single-layer kernel"
    o_moba, o_nsa = _attention(x, attn_norm_g[0], w_in[0], cmp_pe_k[0], cmp_pe_v[0],
                               cmp_w1_k[0], cmp_w2_k[0], cmp_w1_v[0], cmp_w2_v[0])
    N = B * S
    wr = jnp.pad(w_router[0], ((0, 0), (0, LANES - N_EXPERTS)))
    br = jnp.pad(b_router[0], (0, LANES - N_EXPERTS)).reshape(1, LANES)
    x1, hn, e128, w128 = _outproj(o_moba.reshape(N, -1), o_nsa.reshape(N, -1), x.reshape(N, D),
                                  w_out[0].astype(BF16), ffn_norm_g[0].reshape(1, D), wr, br, tm=512)
    out = _moe(x1, hn, e128, w128, w_up[0], b_up[0], w_down[0], b_down[0], final_norm_g)
    return out.reshape(B, S, D)
```

```python
import functools

import jax
import jax.numpy as jnp
import numpy as np
from jax import lax
from jax.experimental import pallas as pl
from jax.experimental.pallas import tpu as pltpu

F32 = jnp.float32
BF16 = jnp.bfloat16
I32 = jnp.int32

HEAD_DIM = 64
MOBA_HEADS = 8
NSA_HEADS = 8
NSA_KV_HEADS = 2
NSA_GROUP = NSA_HEADS // NSA_KV_HEADS
MOBA_BLOCK = 256
MOBA_TOPK = 3
NSA_CMP_LEN = 32
NSA_CMP_STRIDE = 16
NSA_SLC_BLOCK = 64
NSA_SLC_TOPN = 16
NSA_WINDOW = 512
NSA_BRANCHES = 3
N_EXPERTS = 32
TOP_K = 4
SWIGLU_LIMIT = 7.0
SWIGLU_ALPHA = 1.702
RMS_EPS = 1e-5
NEG_BIG = -1e30
LOG2E = 1.4426950408889634

LANES = 128
SUBLANES = 8
VMEM_LIMIT = 56 * 1024 * 1024

NSA_TQ = 256
NSA_CHAIN_LANES = 256
TILE_LIST_LEN = 64
DIAG_SLOT = TILE_LIST_LEN - 1
SLC_TILE = 256
WIN_KEYS = NSA_WINDOW + NSA_TQ
MOE_ROWS = 512
ROUTE_TILE = 1024
RANK_TILE = 512
ROW_DMA_UNROLL = 8

NT_DIMS = (((1,), (1,)), ((), ()))


def _params(n_grid):
    return pltpu.CompilerParams(
        dimension_semantics=("arbitrary",) * n_grid,
        vmem_limit_bytes=VMEM_LIMIT,
    )


def _rmsnorm(x, g):
    return x * lax.rsqrt(jnp.mean(x * x, axis=-1, keepdims=True) + RMS_EPS) * g


def _inproj_kernel(x_ref, g_ref, wr_ref, wt_ref, wg_ref,
                   mk_ref, kc_ref, vc_ref, ks_ref, kw_ref,
                   mqT_ref, mvT_ref, nqT_ref, vsT_ref, vwT_ref, gT_ref):
    xn = _rmsnorm(x_ref[0], g_ref[...])
    xb = xn.astype(BF16)
    yr = jnp.dot(xb, wr_ref[...], preferred_element_type=F32)
    mk_ref[0] = yr[:, 0:512].astype(BF16)
    kc_ref[0] = yr[:, 512:640].astype(BF16)
    vc_ref[0] = yr[:, 640:768].astype(BF16)
    ks_ref[0] = yr[:, 768:896].astype(BF16)
    kw_ref[0] = yr[:, 896:1024].astype(BF16)
    yt = lax.dot_general(wt_ref[...], xb, NT_DIMS, preferred_element_type=F32)
    mqT_ref[0] = yt[0:512].astype(BF16)
    mvT_ref[0] = yt[512:1024].astype(BF16)
    nqT_ref[0] = yt[1024:1536].astype(BF16)
    vsT_ref[0] = yt[1536:1664].astype(BF16)
    vwT_ref[0] = yt[1664:1792].astype(BF16)
    gl = lax.dot_general(wg_ref[...], xn, NT_DIMS, precision=lax.Precision.HIGHEST,
                         preferred_element_type=F32)
    gT_ref[0] = jax.nn.sigmoid(gl)


def _inproj(x, g, wr, wt, wg, tm):
    B, S, D = x.shape
    grid = (B, S // tm)
    row = lambda w: pl.BlockSpec((1, tm, w), lambda b, i: (b, i, 0))
    col = lambda h: pl.BlockSpec((1, h, tm), lambda b, i: (b, 0, i))
    full = lambda a: pl.BlockSpec(a.shape, lambda b, i: (0,) * a.ndim)
    out_shape = [
        jax.ShapeDtypeStruct((B, S, 512), BF16),
        jax.ShapeDtypeStruct((B, S, 128), BF16),
        jax.ShapeDtypeStruct((B, S, 128), BF16),
        jax.ShapeDtypeStruct((B, S, 128), BF16),
        jax.ShapeDtypeStruct((B, S, 128), BF16),
        jax.ShapeDtypeStruct((B, 512, S), BF16),
        jax.ShapeDtypeStruct((B, 512, S), BF16),
        jax.ShapeDtypeStruct((B, 512, S), BF16),
        jax.ShapeDtypeStruct((B, 128, S), BF16),
        jax.ShapeDtypeStruct((B, 128, S), BF16),
        jax.ShapeDtypeStruct((B, 32, S), F32),
    ]
    out_specs = [row(512), row(128), row(128), row(128), row(128),
                 col(512), col(512), col(512), col(128), col(128), col(32)]
    return pl.pallas_call(
        _inproj_kernel, grid=grid,
        in_specs=[pl.BlockSpec((1, tm, D), lambda b, i: (b, i, 0)),
                  full(g), full(wr), full(wt), full(wg)],
        out_specs=out_specs, out_shape=out_shape,
        compiler_params=_params(2), name="inproj",
    )(x, g, wr, wt, wg)


ONES_ROWS = 16


def _softmax_stage(s, c, m):
    mt = jnp.max(s, axis=0, keepdims=True) - c
    m_new = jnp.maximum(m, mt)
    alpha = jnp.exp2(m - m_new)
    p = jnp.exp2(s - (m_new + c))
    return m_new, p.astype(BF16), alpha


def _pipelined_tiles(scores, values, offsets, s_ref, p_ref, first, n_ch, n_tiles, j_first):
    chains = range(n_ch)

    def qk_into(slot, j):
        sc = scores(j)
        for c in chains:
            s_ref[slot, c] = sc[c]

    def pv_from(slot, j, alphas, accs):
        out = []
        for c, vt in zip(chains, values(j)):
            vt1 = jnp.concatenate([vt, jnp.ones((ONES_ROWS, vt.shape[1]), BF16)], axis=0)
            out.append(alphas[c] * accs[c] + jnp.dot(vt1, p_ref[slot, c], preferred_element_type=F32))
        return out

    def softmax_into(slot, j, ms):
        cs = offsets(j)
        new = [_softmax_stage(s_ref[slot, c], cs[c], ms[c]) for c in chains]
        for c in chains:
            p_ref[slot, c] = new[c][1]
        return [n[0] for n in new], [n[2] for n in new]

    qk_into(0, 0)
    first = first()
    for c in chains:
        p_ref[1, c] = first[c][1]

    def pair(i, carry):
        ms, alphas, accs, j_prev = carry
        for slot in range(2):
            t = 2 * i + slot
            qk_into(1 - slot, t + 1)
            accs = pv_from(1 - slot, j_prev, alphas, accs)
            ms, alphas = softmax_into(slot, t, ms)
            j_prev = t
        return ms, alphas, accs, j_prev

    n_q = first[0][0].shape[1]
    init = ([f[0] for f in first], [f[2] for f in first],
            [jnp.zeros((HEAD_DIM + ONES_ROWS, n_q), F32)] * n_ch, j_first)
    _, alphas, accs, j_last = lax.fori_loop(0, (n_tiles + 1) // 2, pair, init)
    accs = pv_from(1, j_last, alphas, accs)
    return [accs[c][:HEAD_DIM] / jnp.maximum(accs[c][HEAD_DIM:HEAD_DIM + 1], 1e-30) for c in chains]


def _moba_kernel(qT_ref, k_ref, vT_ref, aug_ref, srow_ref, sl_ref, o_ref,
                 kmean_ref, kparts_ref, s_ref, p_ref, *, nb, nbp, topk):
    qi = pl.program_id(2)
    blk = MOBA_BLOCK

    @pl.when(qi == 0)
    def _():
        kmean_ref[...] = jnp.zeros(kmean_ref.shape, F32)

        def body(n, carry):
            kb = k_ref[0, pl.ds(pl.multiple_of(n * blk, blk), blk), :].astype(F32)
            kmean_ref[pl.ds(n, 1), :] = jnp.mean(kb, axis=0, keepdims=True)
            return carry
        lax.fori_loop(0, nb, body, 0)
        km = kmean_ref[...]
        head = lax.broadcasted_iota(I32, km.shape, 1) >> 6
        km2 = jnp.concatenate([jnp.where(head == h, km, 0.0) for h in range(2)], axis=0)
        hi = km2.astype(BF16)
        mid = (km2 - hi.astype(F32)).astype(BF16)
        lo = (km2 - hi.astype(F32) - mid.astype(F32)).astype(BF16)
        kparts_ref[0] = hi
        kparts_ref[1] = mid
        kparts_ref[2] = lo

    qT = qT_ref[0]
    row = lax.broadcasted_iota(I32, qT.shape, 0)
    qpad = [jnp.where((row >> 6) == h, qT, jnp.zeros_like(qT)) for h in range(2)]

    gates = (jnp.dot(kparts_ref[0], qT, preferred_element_type=F32)
             + jnp.dot(kparts_ref[1], qT, preferred_element_type=F32)
             + jnp.dot(kparts_ref[2], qT, preferred_element_type=F32))
    bidx = lax.broadcasted_iota(I32, (nbp, blk), 0)
    rhs = []
    for h in range(2):
        gate = gates[h * nbp:(h + 1) * nbp]
        gsc = jnp.where(bidx < qi, gate, -jnp.inf)
        bias = jnp.full((nbp, blk), NEG_BIG, F32)
        for _ in range(topk):
            mx = jnp.max(gsc, axis=0, keepdims=True)
            idx = jnp.min(jnp.where(gsc == mx, bidx, nbp), axis=0, keepdims=True)
            pick = jnp.logical_and(bidx == idx, mx > -jnp.inf)
            bias = jnp.where(pick, 0.0, bias)
            gsc = jnp.where(pick, -jnp.inf, gsc)
        pad = jnp.zeros((2 * HEAD_DIM - nbp - 16, blk), BF16)
        rhs.append(jnp.concatenate([qpad[h], bias.astype(BF16), srow_ref[h], pad], axis=0))

    def scores(j, a):
        k0 = pl.multiple_of(j * blk, blk)
        lhs = jnp.concatenate([k_ref[0, pl.ds(k0, blk), :], aug_ref[a]], axis=1)
        return [jnp.dot(lhs, rhs[h], preferred_element_type=F32) for h in range(2)]

    def values(j):
        k0 = pl.multiple_of(j * blk, blk)
        return [vT_ref[0, h * HEAD_DIM:(h + 1) * HEAD_DIM, pl.ds(k0, blk)] for h in range(2)]

    def offsets(j):
        dq = ((qi - j) * blk).astype(F32)
        return [sl_ref[h] * dq for h in range(2)]

    ik = lax.broadcasted_iota(I32, (blk, blk), 0)
    iq = lax.broadcasted_iota(I32, (blk, blk), 1)
    s_own = [jnp.where(ik <= iq, s, NEG_BIG) for s in scores(qi, nb)]
    m0 = jnp.full((1, blk), NEG_BIG, F32)

    def first():
        return [_softmax_stage(s_own[h], jnp.zeros((1, blk), F32), m0) for h in range(2)]

    outs = _pipelined_tiles(lambda j: scores(jnp.minimum(j, nb - 1), jnp.minimum(j, nb - 1)),
                            values, offsets, s_ref, p_ref, first, 2, qi, qi)
    o_ref[0] = jnp.concatenate(outs, axis=0).T.astype(BF16)


def _moba(mqT, mk, mvT, aug, srow, sl):
    B, _, S = mqT.shape
    blk = MOBA_BLOCK
    nb = S // blk
    topk = min(MOBA_TOPK, nb)
    nbp = -(-nb // 16) * 16
    grid = (B, MOBA_HEADS // 2, nb)
    return pl.pallas_call(
        functools.partial(_moba_kernel, nb=nb, nbp=nbp, topk=topk), grid=grid,
        in_specs=[
            pl.BlockSpec((1, 128, blk), lambda b, p, i: (b, p, i)),
            pl.BlockSpec((1, S, 128), lambda b, p, i: (b, 0, p)),
            pl.BlockSpec((1, 128, S), lambda b, p, i: (b, p, 0)),
            pl.BlockSpec(aug.shape, lambda b, p, i: (0, 0, 0)),
            pl.BlockSpec((2, 16, blk), lambda b, p, i: (p, 0, 0)),
            pl.BlockSpec((2, 1, blk), lambda b, p, i: (p, 0, 0)),
        ],
        out_specs=pl.BlockSpec((1, blk, 128), lambda b, p, i: (b, i, p)),
        out_shape=jax.ShapeDtypeStruct((B, S, 512), BF16),
        scratch_shapes=[
            pltpu.VMEM((nbp, 128), F32),
            pltpu.VMEM((3, 2 * nbp, 128), BF16),
            pltpu.VMEM((2, 2, blk, blk), F32),
            pltpu.VMEM((2, 2, blk, blk), BF16),
        ],
        compiler_params=_params(3), name="moba",
    )(mqT, mk, mvT, aug, srow, sl)


def _compress_kernel(kc_ref, vc_ref, wk_ref, wvT_ref, pek_ref, pev_ref, w2k_ref, w2vT_ref,
                     kcmp_ref, vcmpT_ref):
    nc = kc_ref.shape[1]

    wk = wk_ref[...]
    ab = jnp.dot(kc_ref[0], wk, preferred_element_type=F32)
    pt = (jnp.dot(pek_ref[0], wk[:, 0:128].astype(F32), preferred_element_type=F32)
          + jnp.dot(pek_ref[1], wk[:, 128:256].astype(F32), preferred_element_type=F32))
    pre = ab[:, 0:128] + pltpu.roll(ab[:, 128:256], nc - 1, 0) + pt[0:1]
    hid = jax.nn.gelu(pre)
    kcmp_ref[0] = jnp.dot(hid.astype(BF16), w2k_ref[...], preferred_element_type=F32).astype(BF16)

    wvT = wvT_ref[...]
    abT = lax.dot_general(wvT, vc_ref[0], NT_DIMS, preferred_element_type=F32)
    ptT = (lax.dot_general(wvT[0:128].astype(F32), pev_ref[0], NT_DIMS, preferred_element_type=F32)
           + lax.dot_general(wvT[128:256].astype(F32), pev_ref[1], NT_DIMS, preferred_element_type=F32))
    preT = abT[0:128] + pltpu.roll(abT[128:256], nc - 1, 1) + ptT[:, 0:1]
    hidT = jax.nn.gelu(preT)
    vcmpT_ref[0] = jnp.dot(w2vT_ref[...], hidT.astype(BF16), preferred_element_type=F32).astype(BF16)


def _compress(kc2, vc2, wk, wvT, pek, pev, w2k, w2vT):
    B, nc, _ = kc2.shape
    full = lambda a: pl.BlockSpec(a.shape, lambda b: (0,) * a.ndim)
    blk = pl.BlockSpec((1, nc, kc2.shape[2]), lambda b: (b, 0, 0))
    return pl.pallas_call(
        _compress_kernel, grid=(B,),
        in_specs=[blk, blk, full(wk), full(wvT), full(pek), full(pev), full(w2k), full(w2vT)],
        out_specs=[pl.BlockSpec((1, nc, 128), lambda b: (b, 0, 0)),
                   pl.BlockSpec((1, 128, nc), lambda b: (b, 0, 0))],
        out_shape=[jax.ShapeDtypeStruct((B, nc, 128), BF16),
                   jax.ShapeDtypeStruct((B, 128, nc), BF16)],
        compiler_params=_params(1), name="nsa_compress",
    )(kc2, vc2, wk, wvT, pek, pev, w2k, w2vT)


def _nsa_kernel(qT_ref, kcmp_ref, vcmpT_ref, ks_ref, vsT_ref, kw_ref, vwT_ref,
                g_ref, sl_ref, srow_ref, auga_ref, augb_ref, augc_ref, tblc_ref, bw_ref, o_ref,
                s_ref, p_ref, pc_ref, flagv_ref, flags_ref, list_ref, fsem, *, n_slc, topn):
    g = pl.program_id(1)
    qi = pl.program_id(2)
    tq = NSA_TQ
    hg = NSA_GROUP
    wl = hg * tq
    q0 = qi * tq

    q4 = qT_ref[0]
    qT = jnp.concatenate([q4[h * HEAD_DIM:(h + 1) * HEAD_DIM] for h in range(hg)], axis=1)
    qT2 = jnp.concatenate([qT, qT], axis=0)
    rowi = lax.broadcasted_iota(I32, qT2.shape, 0)
    qpad = jnp.where((rowi >> 6) == g, qT2, jnp.zeros_like(qT2))
    slope = sl_ref[0]
    lane = lax.broadcasted_iota(I32, (1, wl), 1)
    t_q = q0 + (lane & (tq - 1))

    nc = kcmp_ref.shape[1]
    rhs_top = jnp.where((rowi >> 6) == g, qT2, srow_ref[0])
    mine_c = (lax.broadcasted_iota(I32, (nc, LANES), 1) >> 6) == g
    lhs_c = jnp.where(mine_c, kcmp_ref[0], augc_ref[0])
    first_c = pl.multiple_of(nc - qi * (tq // NSA_CMP_STRIDE), 8)
    z = jnp.dot(lhs_c, rhs_top, preferred_element_type=F32) + tblc_ref[pl.ds(first_c, nc), :]
    mx = jnp.max(z, axis=0, keepdims=True)
    e = jnp.exp2(z - mx)
    den = jnp.maximum(jnp.sum(e, axis=0, keepdims=True), 1e-30)
    p = e * jnp.where(t_q >= NSA_CMP_LEN - 1, 1.0 / den, 0.0)
    o_c = jnp.dot(vcmpT_ref[0], p.astype(BF16), preferred_element_type=F32)

    pc = p[:, 0:tq]
    for h in range(1, hg):
        pc = pc + p[:, h * tq:(h + 1) * tq]
    n_lc = tq // LANES
    for c in range(n_lc):
        pc_ref[c] = pc[:, c * LANES:(c + 1) * LANES]
    su = NSA_SLC_BLOCK // NSA_CMP_STRIDE
    x = [jnp.concatenate([pc_ref[c, pl.ds(k, n_slc, stride=su), :] for c in range(n_lc)], axis=1)
         for k in range(su)]
    jb = lax.broadcasted_iota(I32, (n_slc, tq), 0)
    prev = jnp.where(jb == 0, 0.0, pltpu.roll(x[3], 1, 0))
    imp = 2.0 * (x[0] + x[1] + x[2]) + x[3] + prev
    cur = (q0 + lax.broadcasted_iota(I32, (1, tq), 1)) >> 6
    allowed = jb <= cur
    forced = jnp.logical_or(jb == 0, jnp.logical_or(jb == cur, jb == cur - 1))
    bias = jnp.where(jnp.logical_and(allowed, forced), 0.0, NEG_BIG)
    sc = jnp.where(jnp.logical_and(allowed, jnp.logical_not(forced)), imp, -1.0)
    for _ in range(topn - 3):
        smx = jnp.max(sc, axis=0, keepdims=True)
        idx = jnp.min(jnp.where(sc == smx, jb, n_slc), axis=0, keepdims=True)
        pick = jnp.logical_and(jb == idx, smx >= 0.0)
        bias = jnp.where(pick, 0.0, bias)
        sc = jnp.where(pick, -1.0, sc)

    n_tiles = augb_ref.shape[0]
    jd = lax.div(q0, SLC_TILE)
    chosen = jnp.where(bias == 0.0, 1.0, 0.0).astype(BF16)
    per_blk = lax.dot_general(jnp.ones((SUBLANES, tq), BF16), chosen, NT_DIMS,
                              preferred_element_type=F32)
    per_shift = (SLC_TILE // NSA_SLC_BLOCK).bit_length() - 1
    in_tile = (lax.broadcasted_iota(I32, (n_slc, LANES), 0) >> per_shift) == lax.broadcasted_iota(
        I32, (n_slc, LANES), 1)
    per_tile = jnp.dot(per_blk.astype(BF16), jnp.where(in_tile, 1.0, 0.0).astype(BF16),
                       preferred_element_type=F32)
    is_past = lax.broadcasted_iota(I32, (SUBLANES, LANES), 1) < jd
    flagv_ref[...] = jnp.where(jnp.logical_and(per_tile > 0.0, is_past), 1, 0).astype(I32)
    flag_copy = pltpu.make_async_copy(flagv_ref, flags_ref, fsem)
    flag_copy.start()

    start = pl.multiple_of(jnp.maximum(q0 - NSA_WINDOW, 0), tq)
    first_w = pl.multiple_of(NSA_WINDOW - (q0 - start), tq)
    kt = kw_ref[0, pl.ds(start, WIN_KEYS), :]
    z = jnp.dot(kt, qpad, preferred_element_type=F32) + bw_ref[0, pl.ds(first_w, WIN_KEYS), :]
    mx = jnp.max(z, axis=0, keepdims=True)
    p = jnp.exp2(z - mx)
    den = jnp.maximum(jnp.sum(p, axis=0, keepdims=True), 1e-30)
    o_w = jnp.dot(vwT_ref[0, :, pl.ds(start, WIN_KEYS)], p.astype(BF16),
                  preferred_element_type=F32) / den

    flag_copy.wait()
    for i in range(TILE_LIST_LEN):
        list_ref[i] = 0

    def add_tile(t, n):
        list_ref[n] = t
        return n + flags_ref[0, t]
    n_used = lax.fori_loop(0, n_tiles, add_tile, 0)
    list_ref[DIAG_SLOT] = jd

    wc = NSA_CHAIN_LANES
    n_ch = wl // wc

    def lane_split(a):
        return [a[:, c * wc:(c + 1) * wc] for c in range(n_ch)]

    if n_slc < LANES:
        bias = jnp.concatenate([bias, jnp.zeros((LANES - n_slc, tq), F32)], axis=0)
    bias4 = jnp.concatenate([bias.astype(BF16)] * hg, axis=1)
    rhs = jnp.concatenate([rhs_top, bias4], axis=0)
    mine = (lax.broadcasted_iota(I32, (SLC_TILE, LANES), 1) >> 6) == g

    def scores(j, null):
        k0 = pl.multiple_of(j * SLC_TILE, SLC_TILE)
        kt = ks_ref[0, pl.ds(k0, SLC_TILE), :]
        lhs = jnp.concatenate([jnp.where(mine, kt, auga_ref[null]), augb_ref[j]], axis=1)
        return jnp.dot(lhs, rhs, preferred_element_type=F32)

    def values(i):
        k0 = pl.multiple_of(list_ref[i] * SLC_TILE, SLC_TILE)
        return [vsT_ref[0, :, pl.ds(k0, SLC_TILE)]] * n_ch

    def offsets(i):
        return lane_split(slope * (q0 - list_ref[i] * SLC_TILE).astype(F32))

    def past_scores(i):
        return lane_split(scores(list_ref[jnp.minimum(i, n_tiles - 1)], (i >= n_used).astype(I32)))

    t_k = jd * SLC_TILE + lax.broadcasted_iota(I32, (SLC_TILE, wl), 0)
    s_diag = lane_split(jnp.where(t_k <= t_q, scores(jd, 0), NEG_BIG))
    c_diag = offsets(DIAG_SLOT)

    def first():
        return [_softmax_stage(s_diag[c], c_diag[c], jnp.full((1, wc), NEG_BIG, F32))
                for c in range(n_ch)]

    o_s = jnp.concatenate(
        _pipelined_tiles(past_scores, values, offsets, s_ref, p_ref, first, n_ch, n_used, DIAG_SLOT),
        axis=1)

    gt = g_ref[0]

    def gate_row(br):
        return jnp.concatenate([gt[br * hg + h:br * hg + h + 1] for h in range(hg)], axis=1)

    o = gate_row(0) * o_c + gate_row(1) * o_s + gate_row(2) * o_w
    o4 = jnp.concatenate([o[:, h * tq:(h + 1) * tq] for h in range(hg)], axis=0)
    o_ref[0] = o4.T.astype(BF16)


def _nsa(nqT, kcmp, vcmpT, ks, vsT, kw, vwT, gT, sl, srow, auga, augb, augc, tblc, bw):
    B, _, S = nqT.shape
    tq = NSA_TQ
    nc = kcmp.shape[1]
    n_slc = S // NSA_SLC_BLOCK
    topn = min(NSA_SLC_TOPN, n_slc)
    wl = NSA_GROUP * tq
    grid = (B, NSA_KV_HEADS, S // tq)
    return pl.pallas_call(
        functools.partial(_nsa_kernel, n_slc=n_slc, topn=topn), grid=grid,
        in_specs=[
            pl.BlockSpec((1, NSA_GROUP * HEAD_DIM, tq), lambda b, g, i: (b, g, i)),
            pl.BlockSpec((1, nc, 128), lambda b, g, i: (b, 0, 0)),
            pl.BlockSpec((1, HEAD_DIM, nc), lambda b, g, i: (b, g, 0)),
            pl.BlockSpec((1, S, 128), lambda b, g, i: (b, 0, 0)),
            pl.BlockSpec((1, HEAD_DIM, S), lambda b, g, i: (b, g, 0)),
            pl.BlockSpec((1, S, 128), lambda b, g, i: (b, 0, 0)),
            pl.BlockSpec((1, HEAD_DIM, S), lambda b, g, i: (b, g, 0)),
            pl.BlockSpec((1, 16, tq), lambda b, g, i: (b, g, i)),
            pl.BlockSpec((1, 1, wl), lambda b, g, i: (g, 0, 0)),
            pl.BlockSpec((1, 2 * HEAD_DIM, wl), lambda b, g, i: (g, 0, 0)),
            pl.BlockSpec((2, SLC_TILE, LANES), lambda b, g, i: (g, 0, 0)),
            pl.BlockSpec(augb.shape, lambda b, g, i: (0, 0, 0)),
            pl.BlockSpec((1, nc, LANES), lambda b, g, i: (g, 0, 0)),
            pl.BlockSpec(tblc.shape, lambda b, g, i: (0, 0)),
            pl.BlockSpec((1,) + bw.shape[1:], lambda b, g, i: (g, 0, 0)),
        ],
        out_specs=pl.BlockSpec((1, tq, NSA_GROUP * HEAD_DIM), lambda b, g, i: (b, i, g)),
        out_shape=jax.ShapeDtypeStruct((B, S, 512), BF16),
        scratch_shapes=[
            pltpu.VMEM((2, wl // NSA_CHAIN_LANES, SLC_TILE, NSA_CHAIN_LANES), F32),
            pltpu.VMEM((2, wl // NSA_CHAIN_LANES, SLC_TILE, NSA_CHAIN_LANES), BF16),
            pltpu.VMEM((tq // LANES, nc, LANES), F32),
            pltpu.VMEM((SUBLANES, LANES), I32),
            pltpu.SMEM((SUBLANES, LANES), I32),
            pltpu.SMEM((TILE_LIST_LEN,), I32),
            pltpu.SemaphoreType.DMA,
        ],
        compiler_params=_params(3), name="nsa",
    )(nqT, kcmp, vcmpT, ks, vsT, kw, vwT, gT, sl, srow, auga, augb, augc, tblc, bw)


def _outproj_kernel(om_ref, on_ref, x_ref, wo_ref, g_ref, wr_ref, br_ref,
                    x1_ref, hn_ref, e_ref, w_ref):
    attn = (jnp.dot(om_ref[...], wo_ref[0:512, :], preferred_element_type=F32)
            + jnp.dot(on_ref[...], wo_ref[512:1024, :], preferred_element_type=F32))
    x1 = x_ref[...] + attn
    x1_ref[...] = x1
    hn = _rmsnorm(x1, g_ref[...])
    _store_token_tiles(hn_ref, hn)
    logits = jnp.dot(hn, wr_ref[...], precision=lax.Precision.HIGHEST,
                     preferred_element_type=F32) + br_ref[...]
    tm = logits.shape[0]
    lane = lax.broadcasted_iota(I32, (tm, LANES), 1)
    sc = jnp.where(lane < N_EXPERTS, logits, -jnp.inf)
    e_out = jnp.zeros((tm, LANES), I32)
    vals = []
    for k in range(TOP_K):
        mx = jnp.max(sc, axis=1, keepdims=True)
        idx = jnp.min(jnp.where(sc == mx, lane, LANES), axis=1, keepdims=True)
        e_out = jnp.where(lane == k, idx, e_out)
        sc = jnp.where(lane == idx, -jnp.inf, sc)
        vals.append(mx)
    ex = [jnp.exp(v - vals[0]) for v in vals]
    den = ex[0] + ex[1] + ex[2] + ex[3]
    w_out = jnp.zeros((tm, LANES), F32)
    for k in range(TOP_K):
        w_out = jnp.where(lane == k, ex[k] / den, w_out)
    e_ref[...] = e_out
    w_ref[...] = w_out


def _outproj(om, on, x, wo, g, wr, br, tm):
    N, D = x.shape
    full = lambda a: pl.BlockSpec(a.shape, lambda i: (0,) * a.ndim)
    row = lambda w: pl.BlockSpec((tm, w), lambda i: (i, 0))
    return pl.pallas_call(
        _outproj_kernel, grid=(N // tm,),
        in_specs=[row(512), row(512), row(D), full(wo), full(g), full(wr), full(br)],
        out_specs=[row(D), pl.BlockSpec((tm * SUBLANES, LANES), lambda i: (i, 0)), row(LANES), row(LANES)],
        out_shape=[jax.ShapeDtypeStruct((N, D), F32), jax.ShapeDtypeStruct((N * SUBLANES, LANES), F32),
                   jax.ShapeDtypeStruct((N, LANES), I32), jax.ShapeDtypeStruct((N, LANES), F32)],
        compiler_params=_params(1), name="outproj_router",
    )(om, on, x, wo, g, wr, br)


def _rank_kernel(e_ref, rank_ref, cnt_ref, base_ref):
    i = pl.program_id(0)
    T = e_ref.shape[0]

    @pl.when(i == 0)
    def _():
        base_ref[...] = jnp.zeros(base_ref.shape, F32)

    e = e_ref[...]
    lane = lax.broadcasted_iota(I32, (T, LANES), 1)
    tril = jnp.where(lax.broadcasted_iota(I32, (T, T), 0) >= lax.broadcasted_iota(I32, (T, T), 1),
                     1.0, 0.0).astype(BF16)
    out = jnp.zeros((T, LANES), I32)
    for k in range(TOP_K):
        hit = lane == e[:, k:k + 1]
        oh = jnp.where(hit, 1.0, 0.0)
        cum = jnp.dot(tril, oh.astype(BF16), preferred_element_type=F32)
        base = base_ref[0:1, :]
        r = jnp.sum(jnp.where(hit, cum - 1.0 + base, 0.0), axis=1, keepdims=True)
        out = jnp.where(lane == k, r.astype(I32), out)
        base_ref[...] = base_ref[...] + jnp.sum(oh, axis=0, keepdims=True)
    rank_ref[...] = out
    cnt_ref[...] = base_ref[...]


def _ranks(e128):
    N = e128.shape[0]
    T = RANK_TILE
    return pl.pallas_call(
        _rank_kernel, grid=(N // T,),
        in_specs=[pl.BlockSpec((T, LANES), lambda i: (i, 0))],
        out_specs=[pl.BlockSpec((T, LANES), lambda i: (i, 0)),
                   pl.BlockSpec((8, LANES), lambda i: (0, 0))],
        out_shape=[jax.ShapeDtypeStruct((N, LANES), I32),
                   jax.ShapeDtypeStruct((8, LANES), F32)],
        scratch_shapes=[pltpu.VMEM((8, LANES), F32)],
        compiler_params=_params(1), name="route_ranks",
    )(e128)


def _row_copy(src, dst, i_src, i_dst, sem):
    return pltpu.make_async_copy(src.at[pl.ds(pl.multiple_of(i_src * SUBLANES, SUBLANES), SUBLANES)],
                                 dst.at[pl.ds(pl.multiple_of(i_dst * SUBLANES, SUBLANES), SUBLANES)], sem)


def _store_token_tiles(ref, x):
    rows = x.shape[0]
    for c in range(SUBLANES):
        ref[pl.ds(c, rows, stride=SUBLANES), :] = x[:, c * LANES:(c + 1) * LANES]


def _load_token_tiles(ref, rows):
    return jnp.concatenate([ref[pl.ds(c, rows, stride=SUBLANES), :] for c in range(SUBLANES)], axis=1)


def _dispatch_kernel(dest_hbm, hp_ref, xz_hbm, out_hbm, idx_ref, isem, sem):
    del xz_hbm
    i = pl.program_id(0)
    T = ROUTE_TILE
    cp = pltpu.make_async_copy(dest_hbm.at[i], idx_ref, isem)
    cp.start()
    cp.wait()

    def issue(t, carry):
        for k in range(TOP_K):
            _row_copy(hp_ref, out_hbm, t, idx_ref[t * TOP_K + k], sem).start(priority=k % 2)
        return carry
    lax.fori_loop(0, T, issue, 0, unroll=ROW_DMA_UNROLL)

    def drain(t, carry):
        for k in range(TOP_K):
            _row_copy(hp_ref, out_hbm, 0, 0, sem).wait()
        return carry
    lax.fori_loop(0, T, drain, 0, unroll=ROW_DMA_UNROLL)


def _dispatch(dest2, hp, xzero):
    nsteps = dest2.shape[0]
    T = ROUTE_TILE
    return pl.pallas_call(
        _dispatch_kernel, grid=(nsteps,),
        in_specs=[pl.BlockSpec(memory_space=pl.ANY),
                  pl.BlockSpec((T * SUBLANES, LANES), lambda i: (i, 0)),
                  pl.BlockSpec(memory_space=pl.ANY)],
        out_specs=pl.BlockSpec(memory_space=pl.ANY),
        out_shape=jax.ShapeDtypeStruct(xzero.shape, xzero.dtype),
        scratch_shapes=[pltpu.SMEM((T * TOP_K,), I32),
                        pltpu.SemaphoreType.DMA, pltpu.SemaphoreType.DMA],
        input_output_aliases={2: 0},
        compiler_params=_params(1), name="moe_dispatch",
    )(dest2, hp, xzero)


W_PREP_COLS = 256


def _expert_kernel(be_ref, na_ref, x_ref, wup_ref, bg_ref, bu_ref, wd_ref, bd_ref, y_ref,
                   wg_ref, wu_ref, wt_ref):
    b = pl.program_id(0)
    active = b < na_ref[0]
    new_expert = jnp.logical_or(b == 0, be_ref[b] != be_ref[jnp.maximum(b - 1, 0)])

    @pl.when(jnp.logical_and(active, new_expert))
    def _():
        half = W_PREP_COLS // 2
        n_lc = wt_ref.shape[0]

        def every_other(first):
            return jnp.concatenate(
                [wt_ref[c, pl.ds(first, half, stride=2), :] for c in range(n_lc)], axis=1).astype(BF16)

        for ch in range(wup_ref.shape[2] // W_PREP_COLS):
            panel = wup_ref[0, :, ch * W_PREP_COLS:(ch + 1) * W_PREP_COLS].T
            for c in range(n_lc):
                wt_ref[c] = panel[:, c * LANES:(c + 1) * LANES]
            wg_ref[ch * half:(ch + 1) * half, :] = every_other(0)
            wu_ref[ch * half:(ch + 1) * half, :] = every_other(1)

    @pl.when(active)
    def _():
        xb = _load_token_tiles(x_ref, MOE_ROWS).astype(BF16)
        gg = lax.dot_general(xb, wg_ref[...], NT_DIMS, preferred_element_type=F32) + bg_ref[0]
        uu = lax.dot_general(xb, wu_ref[...], NT_DIMS, preferred_element_type=F32) + bu_ref[0]
        gg = jnp.minimum(gg, SWIGLU_LIMIT)
        uu = jnp.clip(uu, -SWIGLU_LIMIT, SWIGLU_LIMIT)
        a = gg * jax.nn.sigmoid(SWIGLU_ALPHA * gg) * (uu + 1.0)
        _store_token_tiles(y_ref, jnp.dot(a.astype(BF16), wd_ref[0], preferred_element_type=F32) + bd_ref[0])

    @pl.when(jnp.logical_not(active))
    def _():
        y_ref[...] = jnp.zeros(y_ref.shape, F32)


def _experts(blk_e, n_act, xrows, w_up, bg, bu, wd, bd):
    _, D, F2 = w_up.shape
    F = F2 // 2
    assert D == SUBLANES * LANES
    P = xrows.shape[0] // SUBLANES
    n_blk = P // MOE_ROWS
    rows_spec = pl.BlockSpec((MOE_ROWS * SUBLANES, LANES), lambda b, be, na: (b, 0))
    wspec = lambda r, c: pl.BlockSpec((1, r, c), lambda b, be, na: (be[b], 0, 0))
    grid_spec = pltpu.PrefetchScalarGridSpec(
        num_scalar_prefetch=2, grid=(n_blk,),
        in_specs=[rows_spec, wspec(D, F2), wspec(1, F), wspec(1, F), wspec(F, D), wspec(1, D)],
        out_specs=rows_spec,
        scratch_shapes=[pltpu.VMEM((F, D), BF16), pltpu.VMEM((F, D), BF16),
                        pltpu.VMEM((D // LANES, W_PREP_COLS, LANES), F32)],
    )
    return pl.pallas_call(
        _expert_kernel, grid_spec=grid_spec,
        out_shape=jax.ShapeDtypeStruct(xrows.shape, F32),
        compiler_params=_params(1), name="moe_experts",
    )(blk_e, n_act, xrows, w_up, bg, bu, wd, bd)


def _combine_kernel(dest_hbm, x1_ref, w_ref, g_ref, y_hbm, o_ref, idx_ref, buf_ref, isem, sem):
    i = pl.program_id(0)
    T = ROUTE_TILE
    cp = pltpu.make_async_copy(dest_hbm.at[i], idx_ref, isem)
    cp.start()
    cp.wait()

    def issue(t, carry):
        for k in range(TOP_K):
            _row_copy(y_hbm, buf_ref.at[k], idx_ref[t * TOP_K + k], t, sem).start(priority=k % 2)
        return carry
    lax.fori_loop(0, T, issue, 0, unroll=ROW_DMA_UNROLL)

    def drain(t, carry):
        for k in range(TOP_K):
            _row_copy(y_hbm, buf_ref.at[k], 0, 0, sem).wait()
        return carry
    lax.fori_loop(0, T, drain, 0, unroll=ROW_DMA_UNROLL)

    x1 = x1_ref[...]
    w = w_ref[...]
    cols = []
    for c in range(SUBLANES):
        acc = x1[:, c * LANES:(c + 1) * LANES]
        for k in range(TOP_K):
            acc = acc + w[:, k:k + 1] * buf_ref[k, pl.ds(c, T, stride=SUBLANES), :]
        cols.append(acc)
    o_ref[...] = _rmsnorm(jnp.concatenate(cols, axis=1), g_ref[...])


def _combine(dest2, x1, w128, g, yrows):
    N, D = x1.shape
    T = ROUTE_TILE
    return pl.pallas_call(
        _combine_kernel, grid=(N // T,),
        in_specs=[pl.BlockSpec(memory_space=pl.ANY),
                  pl.BlockSpec((T, D), lambda i: (i, 0)),
                  pl.BlockSpec((T, LANES), lambda i: (i, 0)),
                  pl.BlockSpec(g.shape, lambda i: (0, 0)),
                  pl.BlockSpec(memory_space=pl.ANY)],
        out_specs=pl.BlockSpec((T, D), lambda i: (i, 0)),
        out_shape=jax.ShapeDtypeStruct((N, D), F32),
        scratch_shapes=[pltpu.SMEM((T * TOP_K,), I32),
                        pltpu.VMEM((TOP_K, T * SUBLANES, LANES), F32),
                        pltpu.SemaphoreType.DMA, pltpu.SemaphoreType.DMA],
        compiler_params=_params(1), name="moe_combine",
    )(dest2, x1, w128, g, yrows)


def _alibi_slopes():
    n = MOBA_HEADS + NSA_HEADS
    s = jnp.exp2(-8.0 * jnp.arange(1, n + 1, dtype=F32) / n)
    return s[0::2], s[1::2]


def _prep_inproj(w_in):
    hd = HEAD_DIM
    sizes = [MOBA_HEADS * hd] * 3 + [NSA_HEADS * hd] + [NSA_KV_HEADS * hd] * 6 + [NSA_BRANCHES * NSA_HEADS]
    cuts = np.cumsum([0] + sizes)
    mq, mk, mv, nq, kc, vc, ks, vs, kw, vw, ng = [w_in[:, cuts[i]:cuts[i + 1]] for i in range(11)]
    qscale = (hd ** -0.5) * LOG2E
    wr = jnp.concatenate([mk, kc, vc, ks, kw], axis=1).astype(BF16)
    wt = jnp.concatenate([mq * qscale, mv, nq * qscale, vs, vw], axis=1).T.astype(BF16)
    ngr = ng.reshape(-1, NSA_KV_HEADS, NSA_GROUP, NSA_BRANCHES).transpose(1, 3, 2, 0)
    ngr = ngr.reshape(NSA_KV_HEADS, NSA_BRANCHES * NSA_GROUP, -1)
    wg = jnp.pad(ngr, ((0, 0), (0, 16 - NSA_BRANCHES * NSA_GROUP), (0, 0))).reshape(32, -1)
    return wr, wt, wg.astype(F32)


def _prep_compress(w1, w2, pe):
    hd, half = HEAD_DIM, NSA_CMP_STRIDE
    w1r = w1.reshape(2, half, hd, hd)
    eye = jnp.eye(NSA_KV_HEADS, dtype=w1.dtype)
    w = jnp.einsum('alde,gh->lgdahe', w1r, eye).reshape(half * NSA_KV_HEADS * hd, 2 * NSA_KV_HEADS * hd)
    w2b = jnp.einsum('de,gh->gdhe', w2, eye).reshape(NSA_KV_HEADS * hd, NSA_KV_HEADS * hd)
    per = pe.reshape(2, half, 1, hd)
    pe2 = jnp.broadcast_to(per, (2, half, NSA_KV_HEADS, hd)).reshape(2, 1, half * NSA_KV_HEADS * hd)
    pe2 = jnp.broadcast_to(pe2, (2, 8, pe2.shape[2]))
    return w.astype(BF16), w2b.astype(BF16), pe2.astype(F32)


def _attention_tables(S):
    moba_sl, nsa_sl = _alibi_slopes()
    moba_sl = moba_sl * LOG2E
    nsa_sl = nsa_sl * LOG2E
    blk = MOBA_BLOCK
    moba_row = jnp.broadcast_to(moba_sl[:, None, None], (MOBA_HEADS, 1, blk))
    nb = S // blk
    nbp = -(-nb // 16) * 16
    col = jnp.arange(LANES)[None, None, :]
    tile = jnp.arange(nb + 1)[:, None, None]
    off = jnp.arange(blk, dtype=F32)[None, :, None]
    moba_aug = jnp.where(jnp.logical_and(col == tile, tile < nb), 1.0,
                         jnp.where(jnp.logical_and(col >= nbp, col < nbp + 3), off, 0.0)).astype(BF16)
    parts = jnp.stack(list(_split3(moba_sl)) + [jnp.zeros_like(moba_sl)] * 13, axis=1)
    moba_srow = jnp.broadcast_to(parts[:, :, None], (MOBA_HEADS, 16, blk)).astype(BF16)

    wl = NSA_GROUP * NSA_TQ
    n_slc = S // NSA_SLC_BLOCK
    assert n_slc <= LANES, "block-choice rows must fit the spare contraction rows"
    nsa_row = jnp.repeat(nsa_sl.reshape(NSA_KV_HEADS, NSA_GROUP), NSA_TQ, axis=1)
    hi, mid, lo = [t[:, None, :] for t in _split3(nsa_row)]
    base = ((1 - jnp.arange(NSA_KV_HEADS)) * HEAD_DIM)[:, None, None]
    rows = jnp.arange(2 * HEAD_DIM)[None, :, None]
    nsa_srow = jnp.where(rows == base, hi, jnp.where(rows == base + 1, mid, jnp.where(
        rows == base + 2, lo, jnp.where(rows == base + 3, NEG_BIG, 0.0)))).astype(BF16)
    lane = jnp.arange(LANES)[None, None, None, :]
    base4 = base[:, None]
    null = jnp.arange(2, dtype=F32)[None, :, None, None]
    koff = jnp.arange(SLC_TILE, dtype=F32)[None, None, :, None]
    nsa_auga = jnp.where(jnp.logical_and(lane >= base4, lane < base4 + 3), koff,
                         jnp.where(lane == base4 + 3, null, 0.0))
    nsa_auga = nsa_auga.reshape(2 * NSA_KV_HEADS, SLC_TILE, LANES).astype(BF16)
    per = SLC_TILE // NSA_SLC_BLOCK
    tile = jnp.arange(S // SLC_TILE)[:, None, None]
    blk_of = tile * per + jnp.arange(SLC_TILE)[None, :, None] // NSA_SLC_BLOCK
    nsa_augb = (jnp.arange(LANES)[None, None, :] == blk_of).astype(BF16)

    nc = S // NSA_CMP_STRIDE
    ci = jnp.arange(nc)[None, :, None]
    lane3 = jnp.arange(LANES)[None, None, :]
    nsa_augc = jnp.where(jnp.logical_and(lane3 >= base + 4, lane3 < base + 7), (ci >> 1).astype(F32),
                         jnp.where(jnp.logical_and(lane3 >= base + 7, lane3 < base + 10),
                                   (ci & 1).astype(F32), 0.0)).astype(BF16)
    step2 = [t[:, None, :] for t in _split3(nsa_row * (2.0 * NSA_CMP_STRIDE))]
    step1 = [t[:, None, :] for t in _split3(nsa_row * (1.0 * NSA_CMP_STRIDE))]
    for k in range(3):
        nsa_srow = jnp.where(rows == base + 4 + k, step2[k].astype(BF16),
                             jnp.where(rows == base + 7 + k, step1[k].astype(BF16), nsa_srow))
    il = jnp.tile(jnp.arange(NSA_TQ), NSA_GROUP)[None, :]
    rel = (jnp.arange(2 * nc) - nc)[:, None]
    nsa_tblc = jnp.where(rel * NSA_CMP_STRIDE + (NSA_CMP_LEN - 1) <= il, 0.0, NEG_BIG).astype(F32)
    dist = (NSA_WINDOW + il - jnp.arange(NSA_WINDOW + WIN_KEYS)[:, None])[None]
    nsa_bw = jnp.where(jnp.logical_and(dist >= 0, dist < NSA_WINDOW),
                       -nsa_row[:, None, :] * dist.astype(F32), NEG_BIG)
    return ((moba_aug, moba_srow, moba_row),
            (nsa_row.reshape(NSA_KV_HEADS, 1, wl), nsa_srow, nsa_auga, nsa_augb, nsa_augc, nsa_tblc, nsa_bw))


def _split3(x):
    hi = x.astype(BF16).astype(F32)
    mid = (x - hi).astype(BF16).astype(F32)
    lo = (x - hi - mid).astype(BF16).astype(F32)
    return hi, mid, lo


def _attention(x, attn_norm_g, w_in, cmp_pe_k, cmp_pe_v, cmp_w1_k, cmp_w2_k, cmp_w1_v, cmp_w2_v):
    B, S, D = x.shape
    wr, wt, wg = _prep_inproj(w_in)
    (mk, kc, vc, ks, kw, mqT, mvT, nqT, vsT, vwT, gT) = _inproj(
        x, attn_norm_g.reshape(1, D), wr, wt, wg, tm=512)
    moba_tabs, nsa_tabs = _attention_tables(S)
    o_moba = _moba(mqT, mk, mvT, *moba_tabs)
    wk, w2k, pek = _prep_compress(cmp_w1_k, cmp_w2_k, cmp_pe_k)
    wv, w2v, pev = _prep_compress(cmp_w1_v, cmp_w2_v, cmp_pe_v)
    nc = S // NSA_CMP_STRIDE
    kcmp, vcmpT = _compress(kc.reshape(B, nc, -1), vc.reshape(B, nc, -1), wk, wv.T, pek, pev, w2k, w2v.T)
    o_nsa = _nsa(nqT, kcmp, vcmpT, ks, vsT, kw, vwT, gT, *nsa_tabs)
    return o_moba, o_nsa


def _moe(x1, hn, e128, w128, w_up, b_up, w_down, b_down, final_norm_g):
    N, D = x1.shape
    rank128, cnt = _ranks(e128)
    counts = cnt[0, :N_EXPERTS].astype(I32)
    padded = (counts + MOE_ROWS - 1) // MOE_ROWS * MOE_ROWS
    pends = jnp.cumsum(padded)
    pstarts = pends - padded
    e4 = e128[:, :TOP_K]
    dest = pstarts[e4] + rank128[:, :TOP_K]
    dest2 = dest.reshape(N // ROUTE_TILE, ROUTE_TILE * TOP_K)
    n_blk = (N * TOP_K + N_EXPERTS * MOE_ROWS + MOE_ROWS - 1) // MOE_ROWS
    P = n_blk * MOE_ROWS
    blk_start = jnp.arange(n_blk, dtype=I32) * MOE_ROWS
    blk_e = jnp.minimum(jnp.sum((pends[None, :] <= blk_start[:, None]).astype(I32), axis=1), N_EXPERTS - 1)
    n_act = (pends[-1:] // MOE_ROWS).astype(I32)
    xrows = _dispatch(dest2, hn, jnp.zeros((P * SUBLANES, LANES), F32))
    bg = b_up[:, None, 0::2]
    bu = b_up[:, None, 1::2]
    yrows = _experts(blk_e, n_act, xrows, w_up, bg, bu, w_down.astype(BF16), b_down[:, None, :])
    return _combine(dest2, x1, w128, final_norm_g.reshape(1, D), yrows)


def kernel(x, attn_norm_g, w_in, cmp_pe_k, cmp_pe_v, cmp_w1_k, cmp_w2_k, cmp_w1_v, cmp_w2_v, w_out, ffn_norm_g, w_router, b_router, w_up, b_up, w_down, b_down, final_norm_g):
    B, S, D = x.shape
    assert attn_norm_g.shape[0] == 1, "single-layer kernel"
    o_moba, o_nsa = _attention(x, attn_norm_g[0], w_in[0], cmp_pe_k[0], cmp_pe_v[0],
                               cmp_w1_k[0], cmp_w2_k[0], cmp_w1_v[0], cmp_w2_v[0])
    N = B * S
    wr = jnp.pad(w_router[0], ((0, 0), (0, LANES - N_EXPERTS)))
    br = jnp.pad(b_router[0], (0, LANES - N_EXPERTS)).reshape(1, LANES)
    x1, hn, e128, w128 = _outproj(o_moba.reshape(N, -1), o_nsa.reshape(N, -1), x.reshape(N, D),
                                  w_out[0].astype(BF16), ffn_norm_g[0].reshape(1, D), wr, br, tm=512)
    out = _moe(x1, hn, e128, w128, w_up[0], b_up[0], w_down[0], b_down[0], final_norm_g)
    return out.reshape(B, S, D)
```

```python
import functools

import jax
import jax.numpy as jnp
import numpy as np
from jax import lax
from jax.experimental import pallas as pl
from jax.experimental.pallas import tpu as pltpu

F32 = jnp.float32
BF16 = jnp.bfloat16
I32 = jnp.int32

HEAD_DIM = 64
MOBA_HEADS = 8
NSA_HEADS = 8
NSA_KV_HEADS = 2
NSA_GROUP = NSA_HEADS // NSA_KV_HEADS
MOBA_BLOCK = 256
MOBA_TOPK = 3
NSA_CMP_LEN = 32
NSA_CMP_STRIDE = 16
NSA_SLC_BLOCK = 64
NSA_SLC_TOPN = 16
NSA_WINDOW = 512
NSA_BRANCHES = 3
N_EXPERTS = 32
TOP_K = 4
SWIGLU_LIMIT = 7.0
SWIGLU_ALPHA = 1.702
RMS_EPS = 1e-5
NEG_BIG = -1e30
LOG2E = 1.4426950408889634

LANES = 128
SUBLANES = 8
VMEM_LIMIT = 56 * 1024 * 1024

NSA_TQ = 256
NSA_CHAIN_LANES = 256
TILE_LIST_LEN = 64
DIAG_SLOT = TILE_LIST_LEN - 1
SLC_TILE = 256
WIN_KEYS = NSA_WINDOW + NSA_TQ
MOE_ROWS = 512
ROUTE_TILE = 512
RANK_TILE = 512
ROW_DMA_UNROLL = 8

NT_DIMS = (((1,), (1,)), ((), ()))


def _params(n_grid):
    return pltpu.CompilerParams(
        dimension_semantics=("arbitrary",) * n_grid,
        vmem_limit_bytes=VMEM_LIMIT,
    )


def _rmsnorm(x, g):
    return x * lax.rsqrt(jnp.mean(x * x, axis=-1, keepdims=True) + RMS_EPS) * g


def _inproj_kernel(x_ref, g_ref, wr_ref, wt_ref, wg_ref,
                   mk_ref, kc_ref, vc_ref, ks_ref, kw_ref,
                   mqT_ref, mvT_ref, nqT_ref, vsT_ref, vwT_ref, gT_ref):
    xn = _rmsnorm(x_ref[0], g_ref[...])
    xb = xn.astype(BF16)
    yr = jnp.dot(xb, wr_ref[...], preferred_element_type=F32)
    mk_ref[0] = yr[:, 0:512].astype(BF16)
    kc_ref[0] = yr[:, 512:640].astype(BF16)
    vc_ref[0] = yr[:, 640:768].astype(BF16)
    ks_ref[0] = yr[:, 768:896].astype(BF16)
    kw_ref[0] = yr[:, 896:1024].astype(BF16)
    yt = lax.dot_general(wt_ref[...], xb, NT_DIMS, preferred_element_type=F32)
    mqT_ref[0] = yt[0:512].astype(BF16)
    mvT_ref[0] = yt[512:1024].astype(BF16)
    nqT_ref[0] = yt[1024:1536].astype(BF16)
    vsT_ref[0] = yt[1536:1664].astype(BF16)
    vwT_ref[0] = yt[1664:1792].astype(BF16)
    gl = lax.dot_general(wg_ref[...], xn, NT_DIMS, precision=lax.Precision.HIGHEST,
                         preferred_element_type=F32)
    gT_ref[0] = jax.nn.sigmoid(gl)


def _inproj(x, g, wr, wt, wg, tm):
    B, S, D = x.shape
    grid = (B, S // tm)
    row = lambda w: pl.BlockSpec((1, tm, w), lambda b, i: (b, i, 0))
    col = lambda h: pl.BlockSpec((1, h, tm), lambda b, i: (b, 0, i))
    full = lambda a: pl.BlockSpec(a.shape, lambda b, i: (0,) * a.ndim)
    out_shape = [
        jax.ShapeDtypeStruct((B, S, 512), BF16),
        jax.ShapeDtypeStruct((B, S, 128), BF16),
        jax.ShapeDtypeStruct((B, S, 128), BF16),
        jax.ShapeDtypeStruct((B, S, 128), BF16),
        jax.ShapeDtypeStruct((B, S, 128), BF16),
        jax.ShapeDtypeStruct((B, 512, S), BF16),
        jax.ShapeDtypeStruct((B, 512, S), BF16),
        jax.ShapeDtypeStruct((B, 512, S), BF16),
        jax.ShapeDtypeStruct((B, 128, S), BF16),
        jax.ShapeDtypeStruct((B, 128, S), BF16),
        jax.ShapeDtypeStruct((B, 32, S), F32),
    ]
    out_specs = [row(512), row(128), row(128), row(128), row(128),
                 col(512), col(512), col(512), col(128), col(128), col(32)]
    return pl.pallas_call(
        _inproj_kernel, grid=grid,
        in_specs=[pl.BlockSpec((1, tm, D), lambda b, i: (b, i, 0)),
                  full(g), full(wr), full(wt), full(wg)],
        out_specs=out_specs, out_shape=out_shape,
        compiler_params=_params(2), name="inproj",
    )(x, g, wr, wt, wg)


ONES_ROWS = 16


def _softmax_stage(s, c, m):
    mt = jnp.max(s, axis=0, keepdims=True) - c
    m_new = jnp.maximum(m, mt)
    alpha = jnp.exp2(m - m_new)
    p = jnp.exp2(s - (m_new + c))
    return m_new, p.astype(BF16), alpha


def _pipelined_tiles(scores, values, offsets, s_ref, p_ref, first, n_ch, n_tiles, j_first):
    chains = range(n_ch)

    def qk_into(slot, j):
        sc = scores(j)
        for c in chains:
            s_ref[slot, c] = sc[c]

    def pv_from(slot, j, alphas, accs):
        out = []
        for c, vt in zip(chains, values(j)):
            vt1 = jnp.concatenate([vt, jnp.ones((ONES_ROWS, vt.shape[1]), BF16)], axis=0)
            out.append(alphas[c] * accs[c] + jnp.dot(vt1, p_ref[slot, c], preferred_element_type=F32))
        return out

    def softmax_into(slot, j, ms):
        cs = offsets(j)
        new = [_softmax_stage(s_ref[slot, c], cs[c], ms[c]) for c in chains]
        for c in chains:
            p_ref[slot, c] = new[c][1]
        return [n[0] for n in new], [n[2] for n in new]

    qk_into(0, 0)
    first = first()
    for c in chains:
        p_ref[1, c] = first[c][1]

    def pair(i, carry):
        ms, alphas, accs, j_prev = carry
        for slot in range(2):
            t = 2 * i + slot
            qk_into(1 - slot, t + 1)
            accs = pv_from(1 - slot, j_prev, alphas, accs)
            ms, alphas = softmax_into(slot, t, ms)
            j_prev = t
        return ms, alphas, accs, j_prev

    n_q = first[0][0].shape[1]
    init = ([f[0] for f in first], [f[2] for f in first],
            [jnp.zeros((HEAD_DIM + ONES_ROWS, n_q), F32)] * n_ch, j_first)
    _, alphas, accs, j_last = lax.fori_loop(0, (n_tiles + 1) // 2, pair, init)
    accs = pv_from(1, j_last, alphas, accs)
    return [accs[c][:HEAD_DIM] / jnp.maximum(accs[c][HEAD_DIM:HEAD_DIM + 1], 1e-30) for c in chains]


def _moba_kernel(qT_ref, k_ref, vT_ref, aug_ref, srow_ref, sl_ref, o_ref,
                 kmean_ref, kparts_ref, s_ref, p_ref, *, nb, nbp, topk):
    qi = pl.program_id(2)
    blk = MOBA_BLOCK

    @pl.when(qi == 0)
    def _():
        kmean_ref[...] = jnp.zeros(kmean_ref.shape, F32)

        def body(n, carry):
            kb = k_ref[0, pl.ds(pl.multiple_of(n * blk, blk), blk), :].astype(F32)
            kmean_ref[pl.ds(n, 1), :] = jnp.mean(kb, axis=0, keepdims=True)
            return carry
        lax.fori_loop(0, nb, body, 0)
        km = kmean_ref[...]
        head = lax.broadcasted_iota(I32, km.shape, 1) >> 6
        km2 = jnp.concatenate([jnp.where(head == h, km, 0.0) for h in range(2)], axis=0)
        hi = km2.astype(BF16)
        mid = (km2 - hi.astype(F32)).astype(BF16)
        lo = (km2 - hi.astype(F32) - mid.astype(F32)).astype(BF16)
        kparts_ref[0] = hi
        kparts_ref[1] = mid
        kparts_ref[2] = lo

    qT = qT_ref[0]
    row = lax.broadcasted_iota(I32, qT.shape, 0)
    qpad = [jnp.where((row >> 6) == h, qT, jnp.zeros_like(qT)) for h in range(2)]

    gates = (jnp.dot(kparts_ref[0], qT, preferred_element_type=F32)
             + jnp.dot(kparts_ref[1], qT, preferred_element_type=F32)
             + jnp.dot(kparts_ref[2], qT, preferred_element_type=F32))
    bidx = lax.broadcasted_iota(I32, (nbp, blk), 0)
    rhs = []
    for h in range(2):
        gate = gates[h * nbp:(h + 1) * nbp]
        gsc = jnp.where(bidx < qi, gate, -jnp.inf)
        bias = jnp.full((nbp, blk), NEG_BIG, F32)
        for _ in range(topk):
            mx = jnp.max(gsc, axis=0, keepdims=True)
            idx = jnp.min(jnp.where(gsc == mx, bidx, nbp), axis=0, keepdims=True)
            pick = jnp.logical_and(bidx == idx, mx > -jnp.inf)
            bias = jnp.where(pick, 0.0, bias)
            gsc = jnp.where(pick, -jnp.inf, gsc)
        pad = jnp.zeros((2 * HEAD_DIM - nbp - 16, blk), BF16)
        rhs.append(jnp.concatenate([qpad[h], bias.astype(BF16), srow_ref[h], pad], axis=0))

    def scores(j, a):
        k0 = pl.multiple_of(j * blk, blk)
        lhs = jnp.concatenate([k_ref[0, pl.ds(k0, blk), :], aug_ref[a]], axis=1)
        return [jnp.dot(lhs, rhs[h], preferred_element_type=F32) for h in range(2)]

    def values(j):
        k0 = pl.multiple_of(j * blk, blk)
        return [vT_ref[0, h * HEAD_DIM:(h + 1) * HEAD_DIM, pl.ds(k0, blk)] for h in range(2)]

    def offsets(j):
        dq = ((qi - j) * blk).astype(F32)
        return [sl_ref[h] * dq for h in range(2)]

    ik = lax.broadcasted_iota(I32, (blk, blk), 0)
    iq = lax.broadcasted_iota(I32, (blk, blk), 1)
    s_own = [jnp.where(ik <= iq, s, NEG_BIG) for s in scores(qi, nb)]
    m0 = jnp.full((1, blk), NEG_BIG, F32)

    def first():
        return [_softmax_stage(s_own[h], jnp.zeros((1, blk), F32), m0) for h in range(2)]

    outs = _pipelined_tiles(lambda j: scores(jnp.minimum(j, nb - 1), jnp.minimum(j, nb - 1)),
                            values, offsets, s_ref, p_ref, first, 2, qi, qi)
    o_ref[0] = jnp.concatenate(outs, axis=0).T.astype(BF16)


def _moba(mqT, mk, mvT, aug, srow, sl):
    B, _, S = mqT.shape
    blk = MOBA_BLOCK
    nb = S // blk
    topk = min(MOBA_TOPK, nb)
    nbp = -(-nb // 16) * 16
    grid = (B, MOBA_HEADS // 2, nb)
    return pl.pallas_call(
        functools.partial(_moba_kernel, nb=nb, nbp=nbp, topk=topk), grid=grid,
        in_specs=[
            pl.BlockSpec((1, 128, blk), lambda b, p, i: (b, p, i)),
            pl.BlockSpec((1, S, 128), lambda b, p, i: (b, 0, p)),
            pl.BlockSpec((1, 128, S), lambda b, p, i: (b, p, 0)),
            pl.BlockSpec(aug.shape, lambda b, p, i: (0, 0, 0)),
            pl.BlockSpec((2, 16, blk), lambda b, p, i: (p, 0, 0)),
            pl.BlockSpec((2, 1, blk), lambda b, p, i: (p, 0, 0)),
        ],
        out_specs=pl.BlockSpec((1, blk, 128), lambda b, p, i: (b, i, p)),
        out_shape=jax.ShapeDtypeStruct((B, S, 512), BF16),
        scratch_shapes=[
            pltpu.VMEM((nbp, 128), F32),
            pltpu.VMEM((3, 2 * nbp, 128), BF16),
            pltpu.VMEM((2, 2, blk, blk), F32),
            pltpu.VMEM((2, 2, blk, blk), BF16),
        ],
        compiler_params=_params(3), name="moba",
    )(mqT, mk, mvT, aug, srow, sl)


def _compress_kernel(kc_ref, vc_ref, wk_ref, wvT_ref, pek_ref, pev_ref, w2k_ref, w2vT_ref,
                     kcmp_ref, vcmpT_ref):
    nc = kc_ref.shape[1]

    wk = wk_ref[...]
    ab = jnp.dot(kc_ref[0], wk, preferred_element_type=F32)
    pt = (jnp.dot(pek_ref[0], wk[:, 0:128].astype(F32), preferred_element_type=F32)
          + jnp.dot(pek_ref[1], wk[:, 128:256].astype(F32), preferred_element_type=F32))
    pre = ab[:, 0:128] + pltpu.roll(ab[:, 128:256], nc - 1, 0) + pt[0:1]
    hid = jax.nn.gelu(pre)
    kcmp_ref[0] = jnp.dot(hid.astype(BF16), w2k_ref[...], preferred_element_type=F32).astype(BF16)

    wvT = wvT_ref[...]
    abT = lax.dot_general(wvT, vc_ref[0], NT_DIMS, preferred_element_type=F32)
    ptT = (lax.dot_general(wvT[0:128].astype(F32), pev_ref[0], NT_DIMS, preferred_element_type=F32)
           + lax.dot_general(wvT[128:256].astype(F32), pev_ref[1], NT_DIMS, preferred_element_type=F32))
    preT = abT[0:128] + pltpu.roll(abT[128:256], nc - 1, 1) + ptT[:, 0:1]
    hidT = jax.nn.gelu(preT)
    vcmpT_ref[0] = jnp.dot(w2vT_ref[...], hidT.astype(BF16), preferred_element_type=F32).astype(BF16)


def _compress(kc2, vc2, wk, wvT, pek, pev, w2k, w2vT):
    B, nc, _ = kc2.shape
    full = lambda a: pl.BlockSpec(a.shape, lambda b: (0,) * a.ndim)
    blk = pl.BlockSpec((1, nc, kc2.shape[2]), lambda b: (b, 0, 0))
    return pl.pallas_call(
        _compress_kernel, grid=(B,),
        in_specs=[blk, blk, full(wk), full(wvT), full(pek), full(pev), full(w2k), full(w2vT)],
        out_specs=[pl.BlockSpec((1, nc, 128), lambda b: (b, 0, 0)),
                   pl.BlockSpec((1, 128, nc), lambda b: (b, 0, 0))],
        out_shape=[jax.ShapeDtypeStruct((B, nc, 128), BF16),
                   jax.ShapeDtypeStruct((B, 128, nc), BF16)],
        compiler_params=_params(1), name="nsa_compress",
    )(kc2, vc2, wk, wvT, pek, pev, w2k, w2vT)


def _nsa_kernel(qT_ref, kcmp_ref, vcmpT_ref, ks_ref, vsT_ref, kw_ref, vwT_ref,
                g_ref, sl_ref, srow_ref, auga_ref, augb_ref, augc_ref, tblc_ref, bw_ref, o_ref,
                s_ref, p_ref, pc_ref, flagv_ref, flags_ref, list_ref, fsem, *, n_slc, topn):
    g = pl.program_id(1)
    qi = pl.program_id(2)
    tq = NSA_TQ
    hg = NSA_GROUP
    wl = hg * tq
    q0 = qi * tq

    q4 = qT_ref[0]
    qT = jnp.concatenate([q4[h * HEAD_DIM:(h + 1) * HEAD_DIM] for h in range(hg)], axis=1)
    qT2 = jnp.concatenate([qT, qT], axis=0)
    rowi = lax.broadcasted_iota(I32, qT2.shape, 0)
    qpad = jnp.where((rowi >> 6) == g, qT2, jnp.zeros_like(qT2))
    slope = sl_ref[0]
    lane = lax.broadcasted_iota(I32, (1, wl), 1)
    t_q = q0 + (lane & (tq - 1))

    nc = kcmp_ref.shape[1]
    rhs_top = jnp.where((rowi >> 6) == g, qT2, srow_ref[0])
    mine_c = (lax.broadcasted_iota(I32, (nc, LANES), 1) >> 6) == g
    lhs_c = jnp.where(mine_c, kcmp_ref[0], augc_ref[0])
    first_c = pl.multiple_of(nc - qi * (tq // NSA_CMP_STRIDE), 8)
    z = jnp.dot(lhs_c, rhs_top, preferred_element_type=F32) + tblc_ref[pl.ds(first_c, nc), :]
    mx = jnp.max(z, axis=0, keepdims=True)
    e = jnp.exp2(z - mx)
    den = jnp.maximum(jnp.sum(e, axis=0, keepdims=True), 1e-30)
    p = e * jnp.where(t_q >= NSA_CMP_LEN - 1, 1.0 / den, 0.0)
    o_c = jnp.dot(vcmpT_ref[0], p.astype(BF16), preferred_element_type=F32)

    pc = p[:, 0:tq]
    for h in range(1, hg):
        pc = pc + p[:, h * tq:(h + 1) * tq]
    n_lc = tq // LANES
    for c in range(n_lc):
        pc_ref[c] = pc[:, c * LANES:(c + 1) * LANES]
    su = NSA_SLC_BLOCK // NSA_CMP_STRIDE
    x = [jnp.concatenate([pc_ref[c, pl.ds(k, n_slc, stride=su), :] for c in range(n_lc)], axis=1)
         for k in range(su)]
    jb = lax.broadcasted_iota(I32, (n_slc, tq), 0)
    prev = jnp.where(jb == 0, 0.0, pltpu.roll(x[3], 1, 0))
    imp = 2.0 * (x[0] + x[1] + x[2]) + x[3] + prev
    cur = (q0 + lax.broadcasted_iota(I32, (1, tq), 1)) >> 6
    allowed = jb <= cur
    forced = jnp.logical_or(jb == 0, jnp.logical_or(jb == cur, jb == cur - 1))
    bias = jnp.where(jnp.logical_and(allowed, forced), 0.0, NEG_BIG)
    sc = jnp.where(jnp.logical_and(allowed, jnp.logical_not(forced)), imp, -1.0)
    for _ in range(topn - 3):
        smx = jnp.max(sc, axis=0, keepdims=True)
        idx = jnp.min(jnp.where(sc == smx, jb, n_slc), axis=0, keepdims=True)
        pick = jnp.logical_and(jb == idx, smx >= 0.0)
        bias = jnp.where(pick, 0.0, bias)
        sc = jnp.where(pick, -1.0, sc)

    n_tiles = augb_ref.shape[0]
    jd = lax.div(q0, SLC_TILE)
    chosen = jnp.where(bias == 0.0, 1.0, 0.0).astype(BF16)
    per_blk = lax.dot_general(jnp.ones((SUBLANES, tq), BF16), chosen, NT_DIMS,
                              preferred_element_type=F32)
    per_shift = (SLC_TILE // NSA_SLC_BLOCK).bit_length() - 1
    in_tile = (lax.broadcasted_iota(I32, (n_slc, LANES), 0) >> per_shift) == lax.broadcasted_iota(
        I32, (n_slc, LANES), 1)
    per_tile = jnp.dot(per_blk.astype(BF16), jnp.where(in_tile, 1.0, 0.0).astype(BF16),
                       preferred_element_type=F32)
    is_past = lax.broadcasted_iota(I32, (SUBLANES, LANES), 1) < jd
    flagv_ref[...] = jnp.where(jnp.logical_and(per_tile > 0.0, is_past), 1, 0).astype(I32)
    flag_copy = pltpu.make_async_copy(flagv_ref, flags_ref, fsem)
    flag_copy.start()

    start = pl.multiple_of(jnp.maximum(q0 - NSA_WINDOW, 0), tq)
    first_w = pl.multiple_of(NSA_WINDOW - (q0 - start), tq)
    kt = kw_ref[0, pl.ds(start, WIN_KEYS), :]
    z = jnp.dot(kt, qpad, preferred_element_type=F32) + bw_ref[0, pl.ds(first_w, WIN_KEYS), :]
    mx = jnp.max(z, axis=0, keepdims=True)
    p = jnp.exp2(z - mx)
    den = jnp.maximum(jnp.sum(p, axis=0, keepdims=True), 1e-30)
    o_w = jnp.dot(vwT_ref[0, :, pl.ds(start, WIN_KEYS)], p.astype(BF16),
                  preferred_element_type=F32) / den

    flag_copy.wait()
    for i in range(TILE_LIST_LEN):
        list_ref[i] = 0

    def add_tile(t, n):
        list_ref[n] = t
        return n + flags_ref[0, t]
    n_used = lax.fori_loop(0, n_tiles, add_tile, 0)
    list_ref[DIAG_SLOT] = jd

    wc = NSA_CHAIN_LANES
    n_ch = wl // wc

    def lane_split(a):
        return [a[:, c * wc:(c + 1) * wc] for c in range(n_ch)]

    if n_slc < LANES:
        bias = jnp.concatenate([bias, jnp.zeros((LANES - n_slc, tq), F32)], axis=0)
    bias4 = jnp.concatenate([bias.astype(BF16)] * hg, axis=1)
    rhs = jnp.concatenate([rhs_top, bias4], axis=0)
    mine = (lax.broadcasted_iota(I32, (SLC_TILE, LANES), 1) >> 6) == g

    def scores(j, null):
        k0 = pl.multiple_of(j * SLC_TILE, SLC_TILE)
        kt = ks_ref[0, pl.ds(k0, SLC_TILE), :]
        lhs = jnp.concatenate([jnp.where(mine, kt, auga_ref[null]), augb_ref[j]], axis=1)
        return jnp.dot(lhs, rhs, preferred_element_type=F32)

    def values(i):
        k0 = pl.multiple_of(list_ref[i] * SLC_TILE, SLC_TILE)
        return [vsT_ref[0, :, pl.ds(k0, SLC_TILE)]] * n_ch

    def offsets(i):
        return lane_split(slope * (q0 - list_ref[i] * SLC_TILE).astype(F32))

    def past_scores(i):
        return lane_split(scores(list_ref[jnp.minimum(i, n_tiles - 1)], (i >= n_used).astype(I32)))

    t_k = jd * SLC_TILE + lax.broadcasted_iota(I32, (SLC_TILE, wl), 0)
    s_diag = lane_split(jnp.where(t_k <= t_q, scores(jd, 0), NEG_BIG))
    c_diag = offsets(DIAG_SLOT)

    def first():
        return [_softmax_stage(s_diag[c], c_diag[c], jnp.full((1, wc), NEG_BIG, F32))
                for c in range(n_ch)]

    o_s = jnp.concatenate(
        _pipelined_tiles(past_scores, values, offsets, s_ref, p_ref, first, n_ch, n_used, DIAG_SLOT),
        axis=1)

    gt = g_ref[0]

    def gate_row(br):
        return jnp.concatenate([gt[br * hg + h:br * hg + h + 1] for h in range(hg)], axis=1)

    o = gate_row(0) * o_c + gate_row(1) * o_s + gate_row(2) * o_w
    o4 = jnp.concatenate([o[:, h * tq:(h + 1) * tq] for h in range(hg)], axis=0)
    o_ref[0] = o4.T.astype(BF16)


def _nsa(nqT, kcmp, vcmpT, ks, vsT, kw, vwT, gT, sl, srow, auga, augb, augc, tblc, bw):
    B, _, S = nqT.shape
    tq = NSA_TQ
    nc = kcmp.shape[1]
    n_slc = S // NSA_SLC_BLOCK
    topn = min(NSA_SLC_TOPN, n_slc)
    wl = NSA_GROUP * tq
    grid = (B, NSA_KV_HEADS, S // tq)
    return pl.pallas_call(
        functools.partial(_nsa_kernel, n_slc=n_slc, topn=topn), grid=grid,
        in_specs=[
            pl.BlockSpec((1, NSA_GROUP * HEAD_DIM, tq), lambda b, g, i: (b, g, i)),
            pl.BlockSpec((1, nc, 128), lambda b, g, i: (b, 0, 0)),
            pl.BlockSpec((1, HEAD_DIM, nc), lambda b, g, i: (b, g, 0)),
            pl.BlockSpec((1, S, 128), lambda b, g, i: (b, 0, 0)),
            pl.BlockSpec((1, HEAD_DIM, S), lambda b, g, i: (b, g, 0)),
            pl.BlockSpec((1, S, 128), lambda b, g, i: (b, 0, 0)),
            pl.BlockSpec((1, HEAD_DIM, S), lambda b, g, i: (b, g, 0)),
            pl.BlockSpec((1, 16, tq), lambda b, g, i: (b, g, i)),
            pl.BlockSpec((1, 1, wl), lambda b, g, i: (g, 0, 0)),
            pl.BlockSpec((1, 2 * HEAD_DIM, wl), lambda b, g, i: (g, 0, 0)),
            pl.BlockSpec((2, SLC_TILE, LANES), lambda b, g, i: (g, 0, 0)),
            pl.BlockSpec(augb.shape, lambda b, g, i: (0, 0, 0)),
            pl.BlockSpec((1, nc, LANES), lambda b, g, i: (g, 0, 0)),
            pl.BlockSpec(tblc.shape, lambda b, g, i: (0, 0)),
            pl.BlockSpec((1,) + bw.shape[1:], lambda b, g, i: (g, 0, 0)),
        ],
        out_specs=pl.BlockSpec((1, tq, NSA_GROUP * HEAD_DIM), lambda b, g, i: (b, i, g)),
        out_shape=jax.ShapeDtypeStruct((B, S, 512), BF16),
        scratch_shapes=[
            pltpu.VMEM((2, wl // NSA_CHAIN_LANES, SLC_TILE, NSA_CHAIN_LANES), F32),
            pltpu.VMEM((2, wl // NSA_CHAIN_LANES, SLC_TILE, NSA_CHAIN_LANES), BF16),
            pltpu.VMEM((tq // LANES, nc, LANES), F32),
            pltpu.VMEM((SUBLANES, LANES), I32),
            pltpu.SMEM((SUBLANES, LANES), I32),
            pltpu.SMEM((TILE_LIST_LEN,), I32),
            pltpu.SemaphoreType.DMA,
        ],
        compiler_params=_params(3), name="nsa",
    )(nqT, kcmp, vcmpT, ks, vsT, kw, vwT, gT, sl, srow, auga, augb, augc, tblc, bw)


def _outproj_kernel(om_ref, on_ref, x_ref, wo_ref, g_ref, wr_ref, br_ref,
                    x1_ref, hn_ref, e_ref, w_ref):
    attn = (jnp.dot(om_ref[...], wo_ref[0:512, :], preferred_element_type=F32)
            + jnp.dot(on_ref[...], wo_ref[512:1024, :], preferred_element_type=F32))
    x1 = x_ref[...] + attn
    x1_ref[...] = x1
    hn = _rmsnorm(x1, g_ref[...])
    _store_token_tiles(hn_ref, hn)
    logits = jnp.dot(hn, wr_ref[...], precision=lax.Precision.HIGHEST,
                     preferred_element_type=F32) + br_ref[...]
    tm = logits.shape[0]
    lane = lax.broadcasted_iota(I32, (tm, LANES), 1)
    sc = jnp.where(lane < N_EXPERTS, logits, -jnp.inf)
    e_out = jnp.zeros((tm, LANES), I32)
    vals = []
    for k in range(TOP_K):
        mx = jnp.max(sc, axis=1, keepdims=True)
        idx = jnp.min(jnp.where(sc == mx, lane, LANES), axis=1, keepdims=True)
        e_out = jnp.where(lane == k, idx, e_out)
        sc = jnp.where(lane == idx, -jnp.inf, sc)
        vals.append(mx)
    ex = [jnp.exp(v - vals[0]) for v in vals]
    den = ex[0] + ex[1] + ex[2] + ex[3]
    w_out = jnp.zeros((tm, LANES), F32)
    for k in range(TOP_K):
        w_out = jnp.where(lane == k, ex[k] / den, w_out)
    e_ref[...] = e_out
    w_ref[...] = w_out


def _outproj(om, on, x, wo, g, wr, br, tm):
    N, D = x.shape
    full = lambda a: pl.BlockSpec(a.shape, lambda i: (0,) * a.ndim)
    row = lambda w: pl.BlockSpec((tm, w), lambda i: (i, 0))
    return pl.pallas_call(
        _outproj_kernel, grid=(N // tm,),
        in_specs=[row(512), row(512), row(D), full(wo), full(g), full(wr), full(br)],
        out_specs=[row(D), pl.BlockSpec((tm * SUBLANES, LANES), lambda i: (i, 0)), row(LANES), row(LANES)],
        out_shape=[jax.ShapeDtypeStruct((N, D), F32), jax.ShapeDtypeStruct((N * SUBLANES, LANES), F32),
                   jax.ShapeDtypeStruct((N, LANES), I32), jax.ShapeDtypeStruct((N, LANES), F32)],
        compiler_params=_params(1), name="outproj_router",
    )(om, on, x, wo, g, wr, br)


def _rank_kernel(e_ref, rank_ref, cnt_ref, base_ref):
    i = pl.program_id(0)
    T = e_ref.shape[0]

    @pl.when(i == 0)
    def _():
        base_ref[...] = jnp.zeros(base_ref.shape, F32)

    e = e_ref[...]
    lane = lax.broadcasted_iota(I32, (T, LANES), 1)
    tril = jnp.where(lax.broadcasted_iota(I32, (T, T), 0) >= lax.broadcasted_iota(I32, (T, T), 1),
                     1.0, 0.0).astype(BF16)
    out = jnp.zeros((T, LANES), I32)
    for k in range(TOP_K):
        hit = lane == e[:, k:k + 1]
        oh = jnp.where(hit, 1.0, 0.0)
        cum = jnp.dot(tril, oh.astype(BF16), preferred_element_type=F32)
        base = base_ref[0:1, :]
        r = jnp.sum(jnp.where(hit, cum - 1.0 + base, 0.0), axis=1, keepdims=True)
        out = jnp.where(lane == k, r.astype(I32), out)
        base_ref[...] = base_ref[...] + jnp.sum(oh, axis=0, keepdims=True)
    rank_ref[...] = out
    cnt_ref[...] = base_ref[...]


def _ranks(e128):
    N = e128.shape[0]
    T = RANK_TILE
    return pl.pallas_call(
        _rank_kernel, grid=(N // T,),
        in_specs=[pl.BlockSpec((T, LANES), lambda i: (i, 0))],
        out_specs=[pl.BlockSpec((T, LANES), lambda i: (i, 0)),
                   pl.BlockSpec((8, LANES), lambda i: (0, 0))],
        out_shape=[jax.ShapeDtypeStruct((N, LANES), I32),
                   jax.ShapeDtypeStruct((8, LANES), F32)],
        scratch_shapes=[pltpu.VMEM((8, LANES), F32)],
        compiler_params=_params(1), name="route_ranks",
    )(e128)


def _row_copy(src, dst, i_src, i_dst, sem):
    return pltpu.make_async_copy(src.at[pl.ds(pl.multiple_of(i_src * SUBLANES, SUBLANES), SUBLANES)],
                                 dst.at[pl.ds(pl.multiple_of(i_dst * SUBLANES, SUBLANES), SUBLANES)], sem)


def _store_token_tiles(ref, x):
    rows = x.shape[0]
    for c in range(SUBLANES):
        ref[pl.ds(c, rows, stride=SUBLANES), :] = x[:, c * LANES:(c + 1) * LANES]


def _load_token_tiles(ref, rows):
    return jnp.concatenate([ref[pl.ds(c, rows, stride=SUBLANES), :] for c in range(SUBLANES)], axis=1)


def _dispatch_kernel(dest_hbm, hp_ref, xz_hbm, out_hbm, idx_ref, isem, sem):
    del xz_hbm
    i = pl.program_id(0)
    T = ROUTE_TILE
    cp = pltpu.make_async_copy(dest_hbm.at[i], idx_ref, isem)
    cp.start()
    cp.wait()

    def issue(t, carry):
        for k in range(TOP_K):
            _row_copy(hp_ref, out_hbm, t, idx_ref[t * TOP_K + k], sem).start(priority=k % 2)
        return carry
    lax.fori_loop(0, T, issue, 0, unroll=ROW_DMA_UNROLL)

    def drain(t, carry):
        for k in range(TOP_K):
            _row_copy(hp_ref, out_hbm, 0, 0, sem).wait()
        return carry
    lax.fori_loop(0, T, drain, 0, unroll=ROW_DMA_UNROLL)


def _dispatch(dest2, hp, xzero):
    nsteps = dest2.shape[0]
    T = ROUTE_TILE
    return pl.pallas_call(
        _dispatch_kernel, grid=(nsteps,),
        in_specs=[pl.BlockSpec(memory_space=pl.ANY),
                  pl.BlockSpec((T * SUBLANES, LANES), lambda i: (i, 0)),
                  pl.BlockSpec(memory_space=pl.ANY)],
        out_specs=pl.BlockSpec(memory_space=pl.ANY),
        out_shape=jax.ShapeDtypeStruct(xzero.shape, xzero.dtype),
        scratch_shapes=[pltpu.SMEM((T * TOP_K,), I32),
                        pltpu.SemaphoreType.DMA, pltpu.SemaphoreType.DMA],
        input_output_aliases={2: 0},
        compiler_params=_params(1), name="moe_dispatch",
    )(dest2, hp, xzero)


W_PREP_COLS = 256


def _expert_kernel(be_ref, na_ref, x_ref, wup_ref, bg_ref, bu_ref, wd_ref, bd_ref, y_ref,
                   wg_ref, wu_ref, wt_ref):
    b = pl.program_id(0)
    active = b < na_ref[0]
    new_expert = jnp.logical_or(b == 0, be_ref[b] != be_ref[jnp.maximum(b - 1, 0)])

    @pl.when(jnp.logical_and(active, new_expert))
    def _():
        half = W_PREP_COLS // 2
        n_lc = wt_ref.shape[0]

        def every_other(first):
            return jnp.concatenate(
                [wt_ref[c, pl.ds(first, half, stride=2), :] for c in range(n_lc)], axis=1).astype(BF16)

        for ch in range(wup_ref.shape[2] // W_PREP_COLS):
            panel = wup_ref[0, :, ch * W_PREP_COLS:(ch + 1) * W_PREP_COLS].T
            for c in range(n_lc):
                wt_ref[c] = panel[:, c * LANES:(c + 1) * LANES]
            wg_ref[ch * half:(ch + 1) * half, :] = every_other(0)
            wu_ref[ch * half:(ch + 1) * half, :] = every_other(1)

    @pl.when(active)
    def _():
        xb = _load_token_tiles(x_ref, MOE_ROWS).astype(BF16)
        gg = lax.dot_general(xb, wg_ref[...], NT_DIMS, preferred_element_type=F32) + bg_ref[0]
        uu = lax.dot_general(xb, wu_ref[...], NT_DIMS, preferred_element_type=F32) + bu_ref[0]
        gg = jnp.minimum(gg, SWIGLU_LIMIT)
        uu = jnp.clip(uu, -SWIGLU_LIMIT, SWIGLU_LIMIT)
        a = gg * jax.nn.sigmoid(SWIGLU_ALPHA * gg) * (uu + 1.0)
        _store_token_tiles(y_ref, jnp.dot(a.astype(BF16), wd_ref[0], preferred_element_type=F32) + bd_ref[0])

    @pl.when(jnp.logical_not(active))
    def _():
        y_ref[...] = jnp.zeros(y_ref.shape, F32)


def _experts(blk_e, n_act, xrows, w_up, bg, bu, wd, bd):
    _, D, F2 = w_up.shape
    F = F2 // 2
    assert D == SUBLANES * LANES
    P = xrows.shape[0] // SUBLANES
    n_blk = P // MOE_ROWS
    rows_spec = pl.BlockSpec((MOE_ROWS * SUBLANES, LANES), lambda b, be, na: (b, 0))
    wspec = lambda r, c: pl.BlockSpec((1, r, c), lambda b, be, na: (be[b], 0, 0))
    grid_spec = pltpu.PrefetchScalarGridSpec(
        num_scalar_prefetch=2, grid=(n_blk,),
        in_specs=[rows_spec, wspec(D, F2), wspec(1, F), wspec(1, F), wspec(F, D), wspec(1, D)],
        out_specs=rows_spec,
        scratch_shapes=[pltpu.VMEM((F, D), BF16), pltpu.VMEM((F, D), BF16),
                        pltpu.VMEM((D // LANES, W_PREP_COLS, LANES), F32)],
    )
    return pl.pallas_call(
        _expert_kernel, grid_spec=grid_spec,
        out_shape=jax.ShapeDtypeStruct(xrows.shape, F32),
        compiler_params=_params(1), name="moe_experts",
    )(blk_e, n_act, xrows, w_up, bg, bu, wd, bd)


def _combine_kernel(dest_hbm, x1_ref, w_ref, g_ref, y_hbm, o_ref, idx_ref, buf_ref, isem, sem):
    i = pl.program_id(0)
    T = ROUTE_TILE
    cp = pltpu.make_async_copy(dest_hbm.at[i], idx_ref, isem)
    cp.start()
    cp.wait()

    def issue(t, carry):
        for k in range(TOP_K):
            _row_copy(y_hbm, buf_ref.at[k], idx_ref[t * TOP_K + k], t, sem).start(priority=k % 2)
        return carry
    lax.fori_loop(0, T, issue, 0, unroll=ROW_DMA_UNROLL)

    def drain(t, carry):
        for k in range(TOP_K):
            _row_copy(y_hbm, buf_ref.at[k], 0, 0, sem).wait()
        return carry
    lax.fori_loop(0, T, drain, 0, unroll=ROW_DMA_UNROLL)

    x1 = x1_ref[...]
    w = w_ref[...]
    cols = []
    for c in range(SUBLANES):
        acc = x1[:, c * LANES:(c + 1) * LANES]
        for k in range(TOP_K):
            acc = acc + w[:, k:k + 1] * buf_ref[k, pl.ds(c, T, stride=SUBLANES), :]
        cols.append(acc)
    o_ref[...] = _rmsnorm(jnp.concatenate(cols, axis=1), g_ref[...])


def _combine(dest2, x1, w128, g, yrows):
    N, D = x1.shape
    T = ROUTE_TILE
    return pl.pallas_call(
        _combine_kernel, grid=(N // T,),
        in_specs=[pl.BlockSpec(memory_space=pl.ANY),
                  pl.BlockSpec((T, D), lambda i: (i, 0)),
                  pl.BlockSpec((T, LANES), lambda i: (i, 0)),
                  pl.BlockSpec(g.shape, lambda i: (0, 0)),
                  pl.BlockSpec(memory_space=pl.ANY)],
        out_specs=pl.BlockSpec((T, D), lambda i: (i, 0)),
        out_shape=jax.ShapeDtypeStruct((N, D), F32),
        scratch_shapes=[pltpu.SMEM((T * TOP_K,), I32),
                        pltpu.VMEM((TOP_K, T * SUBLANES, LANES), F32),
                        pltpu.SemaphoreType.DMA, pltpu.SemaphoreType.DMA],
        compiler_params=_params(1), name="moe_combine",
    )(dest2, x1, w128, g, yrows)


def _alibi_slopes():
    n = MOBA_HEADS + NSA_HEADS
    s = jnp.exp2(-8.0 * jnp.arange(1, n + 1, dtype=F32) / n)
    return s[0::2], s[1::2]


def _prep_inproj(w_in):
    hd = HEAD_DIM
    sizes = [MOBA_HEADS * hd] * 3 + [NSA_HEADS * hd] + [NSA_KV_HEADS * hd] * 6 + [NSA_BRANCHES * NSA_HEADS]
    cuts = np.cumsum([0] + sizes)
    mq, mk, mv, nq, kc, vc, ks, vs, kw, vw, ng = [w_in[:, cuts[i]:cuts[i + 1]] for i in range(11)]
    qscale = (hd ** -0.5) * LOG2E
    wr = jnp.concatenate([mk, kc, vc, ks, kw], axis=1).astype(BF16)
    wt = jnp.concatenate([mq * qscale, mv, nq * qscale, vs, vw], axis=1).T.astype(BF16)
    ngr = ng.reshape(-1, NSA_KV_HEADS, NSA_GROUP, NSA_BRANCHES).transpose(1, 3, 2, 0)
    ngr = ngr.reshape(NSA_KV_HEADS, NSA_BRANCHES * NSA_GROUP, -1)
    wg = jnp.pad(ngr, ((0, 0), (0, 16 - NSA_BRANCHES * NSA_GROUP), (0, 0))).reshape(32, -1)
    return wr, wt, wg.astype(F32)


def _prep_compress(w1, w2, pe):
    hd, half = HEAD_DIM, NSA_CMP_STRIDE
    w1r = w1.reshape(2, half, hd, hd)
    eye = jnp.eye(NSA_KV_HEADS, dtype=w1.dtype)
    w = jnp.einsum('alde,gh->lgdahe', w1r, eye).reshape(half * NSA_KV_HEADS * hd, 2 * NSA_KV_HEADS * hd)
    w2b = jnp.einsum('de,gh->gdhe', w2, eye).reshape(NSA_KV_HEADS * hd, NSA_KV_HEADS * hd)
    per = pe.reshape(2, half, 1, hd)
    pe2 = jnp.broadcast_to(per, (2, half, NSA_KV_HEADS, hd)).reshape(2, 1, half * NSA_KV_HEADS * hd)
    pe2 = jnp.broadcast_to(pe2, (2, 8, pe2.shape[2]))
    return w.astype(BF16), w2b.astype(BF16), pe2.astype(F32)


def _attention_tables(S):
    moba_sl, nsa_sl = _alibi_slopes()
    moba_sl = moba_sl * LOG2E
    nsa_sl = nsa_sl * LOG2E
    blk = MOBA_BLOCK
    moba_row = jnp.broadcast_to(moba_sl[:, None, None], (MOBA_HEADS, 1, blk))
    nb = S // blk
    nbp = -(-nb // 16) * 16
    col = jnp.arange(LANES)[None, None, :]
    tile = jnp.arange(nb + 1)[:, None, None]
    off = jnp.arange(blk, dtype=F32)[None, :, None]
    moba_aug = jnp.where(jnp.logical_and(col == tile, tile < nb), 1.0,
                         jnp.where(jnp.logical_and(col >= nbp, col < nbp + 3), off, 0.0)).astype(BF16)
    parts = jnp.stack(list(_split3(moba_sl)) + [jnp.zeros_like(moba_sl)] * 13, axis=1)
    moba_srow = jnp.broadcast_to(parts[:, :, None], (MOBA_HEADS, 16, blk)).astype(BF16)

    wl = NSA_GROUP * NSA_TQ
    n_slc = S // NSA_SLC_BLOCK
    assert n_slc <= LANES, "block-choice rows must fit the spare contraction rows"
    nsa_row = jnp.repeat(nsa_sl.reshape(NSA_KV_HEADS, NSA_GROUP), NSA_TQ, axis=1)
    hi, mid, lo = [t[:, None, :] for t in _split3(nsa_row)]
    base = ((1 - jnp.arange(NSA_KV_HEADS)) * HEAD_DIM)[:, None, None]
    rows = jnp.arange(2 * HEAD_DIM)[None, :, None]
    nsa_srow = jnp.where(rows == base, hi, jnp.where(rows == base + 1, mid, jnp.where(
        rows == base + 2, lo, jnp.where(rows == base + 3, NEG_BIG, 0.0)))).astype(BF16)
    lane = jnp.arange(LANES)[None, None, None, :]
    base4 = base[:, None]
    null = jnp.arange(2, dtype=F32)[None, :, None, None]
    koff = jnp.arange(SLC_TILE, dtype=F32)[None, None, :, None]
    nsa_auga = jnp.where(jnp.logical_and(lane >= base4, lane < base4 + 3), koff,
                         jnp.where(lane == base4 + 3, null, 0.0))
    nsa_auga = nsa_auga.reshape(2 * NSA_KV_HEADS, SLC_TILE, LANES).astype(BF16)
    per = SLC_TILE // NSA_SLC_BLOCK
    tile = jnp.arange(S // SLC_TILE)[:, None, None]
    blk_of = tile * per + jnp.arange(SLC_TILE)[None, :, None] // NSA_SLC_BLOCK
    nsa_augb = (jnp.arange(LANES)[None, None, :] == blk_of).astype(BF16)

    nc = S // NSA_CMP_STRIDE
    ci = jnp.arange(nc)[None, :, None]
    lane3 = jnp.arange(LANES)[None, None, :]
    nsa_augc = jnp.where(jnp.logical_and(lane3 >= base + 4, lane3 < base + 7), (ci >> 1).astype(F32),
                         jnp.where(jnp.logical_and(lane3 >= base + 7, lane3 < base + 10),
                                   (ci & 1).astype(F32), 0.0)).astype(BF16)
    step2 = [t[:, None, :] for t in _split3(nsa_row * (2.0 * NSA_CMP_STRIDE))]
    step1 = [t[:, None, :] for t in _split3(nsa_row * (1.0 * NSA_CMP_STRIDE))]
    for k in range(3):
        nsa_srow = jnp.where(rows == base + 4 + k, step2[k].astype(BF16),
                             jnp.where(rows == base + 7 + k, step1[k].astype(BF16), nsa_srow))
    il = jnp.tile(jnp.arange(NSA_TQ), NSA_GROUP)[None, :]
    rel = (jnp.arange(2 * nc) - nc)[:, None]
    nsa_tblc = jnp.where(rel * NSA_CMP_STRIDE + (NSA_CMP_LEN - 1) <= il, 0.0, NEG_BIG).astype(F32)
    dist = (NSA_WINDOW + il - jnp.arange(NSA_WINDOW + WIN_KEYS)[:, None])[None]
    nsa_bw = jnp.where(jnp.logical_and(dist >= 0, dist < NSA_WINDOW),
                       -nsa_row[:, None, :] * dist.astype(F32), NEG_BIG)
    return ((moba_aug, moba_srow, moba_row),
            (nsa_row.reshape(NSA_KV_HEADS, 1, wl), nsa_srow, nsa_auga, nsa_augb, nsa_augc, nsa_tblc, nsa_bw))


def _split3(x):
    hi = x.astype(BF16).astype(F32)
    mid = (x - hi).astype(BF16).astype(F32)
    lo = (x - hi - mid).astype(BF16).astype(F32)
    return hi, mid, lo


def _attention(x, attn_norm_g, w_in, cmp_pe_k, cmp_pe_v, cmp_w1_k, cmp_w2_k, cmp_w1_v, cmp_w2_v):
    B, S, D = x.shape
    wr, wt, wg = _prep_inproj(w_in)
    (mk, kc, vc, ks, kw, mqT, mvT, nqT, vsT, vwT, gT) = _inproj(
        x, attn_norm_g.reshape(1, D), wr, wt, wg, tm=512)
    moba_tabs, nsa_tabs = _attention_tables(S)
    o_moba = _moba(mqT, mk, mvT, *moba_tabs)
    wk, w2k, pek = _prep_compress(cmp_w1_k, cmp_w2_k, cmp_pe_k)
    wv, w2v, pev = _prep_compress(cmp_w1_v, cmp_w2_v, cmp_pe_v)
    nc = S // NSA_CMP_STRIDE
    kcmp, vcmpT = _compress(kc.reshape(B, nc, -1), vc.reshape(B, nc, -1), wk, wv.T, pek, pev, w2k, w2v.T)
    o_nsa = _nsa(nqT, kcmp, vcmpT, ks, vsT, kw, vwT, gT, *nsa_tabs)
    return o_moba, o_nsa


def _moe(x1, hn, e128, w128, w_up, b_up, w_down, b_down, final_norm_g):
    N, D = x1.shape
    rank128, cnt = _ranks(e128)
    counts = cnt[0, :N_EXPERTS].astype(I32)
    padded = (counts + MOE_ROWS - 1) // MOE_ROWS * MOE_ROWS
    pends = jnp.cumsum(padded)
    pstarts = pends - padded
    e4 = e128[:, :TOP_K]
    dest = pstarts[e4] + rank128[:, :TOP_K]
    dest2 = dest.reshape(N // ROUTE_TILE, ROUTE_TILE * TOP_K)
    n_blk = (N * TOP_K + N_EXPERTS * MOE_ROWS + MOE_ROWS - 1) // MOE_ROWS
    P = n_blk * MOE_ROWS
    blk_start = jnp.arange(n_blk, dtype=I32) * MOE_ROWS
    blk_e = jnp.minimum(jnp.sum((pends[None, :] <= blk_start[:, None]).astype(I32), axis=1), N_EXPERTS - 1)
    n_act = (pends[-1:] // MOE_ROWS).astype(I32)
    xrows = _dispatch(dest2, hn, jnp.zeros((P * SUBLANES, LANES), F32))
    bg = b_up[:, None, 0::2]
    bu = b_up[:, None, 1::2]
    yrows = _experts(blk_e, n_act, xrows, w_up, bg, bu, w_down.astype(BF16), b_down[:, None, :])
    return _combine(dest2, x1, w128, final_norm_g.reshape(1, D), yrows)


def kernel(x, attn_norm_g, w_in, cmp_pe_k, cmp_pe_v, cmp_w1_k, cmp_w2_k, cmp_w1_v, cmp_w2_v, w_out, ffn_norm_g, w_router, b_router, w_up, b_up, w_down, b_down, final_norm_g):
    B, S, D = x.shape
    assert attn_norm_g.shape[0] == 1, "single-layer kernel"
    o_moba, o_nsa = _attention(x, attn_norm_g[0], w_in[0], cmp_pe_k[0], cmp_pe_v[0],
                               cmp_w1_k[0], cmp_w2_k[0], cmp_w1_v[0], cmp_w2_v[0])
    N = B * S
    wr = jnp.pad(w_router[0], ((0, 0), (0, LANES - N_EXPERTS)))
    br = jnp.pad(b_router[0], (0, LANES - N_EXPERTS)).reshape(1, LANES)
    x1, hn, e128, w128 = _outproj(o_moba.reshape(N, -1), o_nsa.reshape(N, -1), x.reshape(N, D),
                                  w_out[0].astype(BF16), ffn_norm_g[0].reshape(1, D), wr, br, tm=512)
    out = _moe(x1, hn, e128, w128, w_up[0], b_up[0], w_down[0], b_down[0], final_norm_g)
    return out.reshape(B, S, D)
```

```python
import functools

import jax
import jax.numpy as jnp
import numpy as np
from jax import lax
from jax.experimental import pallas as pl
from jax.experimental.pallas import tpu as pltpu

F32 = jnp.float32
BF16 = jnp.bfloat16
I32 = jnp.int32

HEAD_DIM = 64
MOBA_HEADS = 8
NSA_HEADS = 8
NSA_KV_HEADS = 2
NSA_GROUP = NSA_HEADS // NSA_KV_HEADS
MOBA_BLOCK = 256
MOBA_TOPK = 3
NSA_CMP_LEN = 32
NSA_CMP_STRIDE = 16
NSA_SLC_BLOCK = 64
NSA_SLC_TOPN = 16
NSA_WINDOW = 512
NSA_BRANCHES = 3
N_EXPERTS = 32
TOP_K = 4
SWIGLU_LIMIT = 7.0
SWIGLU_ALPHA = 1.702
RMS_EPS = 1e-5
NEG_BIG = -1e30
LOG2E = 1.4426950408889634

LANES = 128
SUBLANES = 8
VMEM_LIMIT = 56 * 1024 * 1024

NSA_TQ = 256
NSA_CHAIN_LANES = 256
TILE_LIST_LEN = 64
DIAG_SLOT = TILE_LIST_LEN - 1
SLC_TILE = 256
WIN_KEYS = NSA_WINDOW + NSA_TQ
MOE_ROWS = 512
ROUTE_TILE = 512
RANK_TILE = 512
ROW_DMA_UNROLL = 8
ROW_DMA_BATCH = 256

NT_DIMS = (((1,), (1,)), ((), ()))


def _params(n_grid):
    return pltpu.CompilerParams(
        dimension_semantics=("arbitrary",) * n_grid,
        vmem_limit_bytes=VMEM_LIMIT,
    )


def _rmsnorm(x, g):
    return x * lax.rsqrt(jnp.mean(x * x, axis=-1, keepdims=True) + RMS_EPS) * g


def _inproj_kernel(x_ref, g_ref, wr_ref, wt_ref, wg_ref,
                   mk_ref, kc_ref, vc_ref, ks_ref, kw_ref,
                   mqT_ref, mvT_ref, nqT_ref, vsT_ref, vwT_ref, gT_ref):
    xn = _rmsnorm(x_ref[0], g_ref[...])
    xb = xn.astype(BF16)
    yr = jnp.dot(xb, wr_ref[...], preferred_element_type=F32)
    mk_ref[0] = yr[:, 0:512].astype(BF16)
    kc_ref[0] = yr[:, 512:640].astype(BF16)
    vc_ref[0] = yr[:, 640:768].astype(BF16)
    ks_ref[0] = yr[:, 768:896].astype(BF16)
    kw_ref[0] = yr[:, 896:1024].astype(BF16)
    yt = lax.dot_general(wt_ref[...], xb, NT_DIMS, preferred_element_type=F32)
    mqT_ref[0] = yt[0:512].astype(BF16)
    mvT_ref[0] = yt[512:1024].astype(BF16)
    nqT_ref[0] = yt[1024:1536].astype(BF16)
    vsT_ref[0] = yt[1536:1664].astype(BF16)
    vwT_ref[0] = yt[1664:1792].astype(BF16)
    gl = lax.dot_general(wg_ref[...], xn, NT_DIMS, precision=lax.Precision.HIGHEST,
                         preferred_element_type=F32)
    gT_ref[0] = jax.nn.sigmoid(gl)


def _inproj(x, g, wr, wt, wg, tm):
    B, S, D = x.shape
    grid = (B, S // tm)
    row = lambda w: pl.BlockSpec((1, tm, w), lambda b, i: (b, i, 0))
    col = lambda h: pl.BlockSpec((1, h, tm), lambda b, i: (b, 0, i))
    full = lambda a: pl.BlockSpec(a.shape, lambda b, i: (0,) * a.ndim)
    out_shape = [
        jax.ShapeDtypeStruct((B, S, 512), BF16),
        jax.ShapeDtypeStruct((B, S, 128), BF16),
        jax.ShapeDtypeStruct((B, S, 128), BF16),
        jax.ShapeDtypeStruct((B, S, 128), BF16),
        jax.ShapeDtypeStruct((B, S, 128), BF16),
        jax.ShapeDtypeStruct((B, 512, S), BF16),
        jax.ShapeDtypeStruct((B, 512, S), BF16),
        jax.ShapeDtypeStruct((B, 512, S), BF16),
        jax.ShapeDtypeStruct((B, 128, S), BF16),
        jax.ShapeDtypeStruct((B, 128, S), BF16),
        jax.ShapeDtypeStruct((B, 32, S), F32),
    ]
    out_specs = [row(512), row(128), row(128), row(128), row(128),
                 col(512), col(512), col(512), col(128), col(128), col(32)]
    return pl.pallas_call(
        _inproj_kernel, grid=grid,
        in_specs=[pl.BlockSpec((1, tm, D), lambda b, i: (b, i, 0)),
                  full(g), full(wr), full(wt), full(wg)],
        out_specs=out_specs, out_shape=out_shape,
        compiler_params=_params(2), name="inproj",
    )(x, g, wr, wt, wg)


ONES_ROWS = 16


def _softmax_stage(s, c, m):
    mt = jnp.max(s, axis=0, keepdims=True) - c
    m_new = jnp.maximum(m, mt)
    alpha = jnp.exp2(m - m_new)
    p = jnp.exp2(s - (m_new + c))
    return m_new, p.astype(BF16), alpha


def _pipelined_tiles(scores, values, offsets, s_ref, p_ref, first, n_ch, n_tiles, j_first):
    chains = range(n_ch)

    def qk_into(slot, j):
        sc = scores(j)
        for c in chains:
            s_ref[slot, c] = sc[c]

    def pv_from(slot, j, alphas, accs):
        out = []
        for c, vt in zip(chains, values(j)):
            vt1 = jnp.concatenate([vt, jnp.ones((ONES_ROWS, vt.shape[1]), BF16)], axis=0)
            out.append(alphas[c] * accs[c] + jnp.dot(vt1, p_ref[slot, c], preferred_element_type=F32))
        return out

    def softmax_into(slot, j, ms):
        cs = offsets(j)
        new = [_softmax_stage(s_ref[slot, c], cs[c], ms[c]) for c in chains]
        for c in chains:
            p_ref[slot, c] = new[c][1]
        return [n[0] for n in new], [n[2] for n in new]

    qk_into(0, 0)
    first = first()
    for c in chains:
        p_ref[1, c] = first[c][1]

    def pair(i, carry):
        ms, alphas, accs, j_prev = carry
        for slot in range(2):
            t = 2 * i + slot
            qk_into(1 - slot, t + 1)
            accs = pv_from(1 - slot, j_prev, alphas, accs)
            ms, alphas = softmax_into(slot, t, ms)
            j_prev = t
        return ms, alphas, accs, j_prev

    n_q = first[0][0].shape[1]
    init = ([f[0] for f in first], [f[2] for f in first],
            [jnp.zeros((HEAD_DIM + ONES_ROWS, n_q), F32)] * n_ch, j_first)
    _, alphas, accs, j_last = lax.fori_loop(0, (n_tiles + 1) // 2, pair, init)
    accs = pv_from(1, j_last, alphas, accs)
    return [accs[c][:HEAD_DIM] / jnp.maximum(accs[c][HEAD_DIM:HEAD_DIM + 1], 1e-30) for c in chains]


def _moba_kernel(qT_ref, k_ref, vT_ref, aug_ref, srow_ref, sl_ref, o_ref,
                 kmean_ref, kparts_ref, s_ref, p_ref, *, nb, nbp, topk):
    qi = pl.program_id(2)
    blk = MOBA_BLOCK

    @pl.when(qi == 0)
    def _():
        kmean_ref[...] = jnp.zeros(kmean_ref.shape, F32)

        def body(n, carry):
            kb = k_ref[0, pl.ds(pl.multiple_of(n * blk, blk), blk), :].astype(F32)
            kmean_ref[pl.ds(n, 1), :] = jnp.mean(kb, axis=0, keepdims=True)
            return carry
        lax.fori_loop(0, nb, body, 0)
        km = kmean_ref[...]
        head = lax.broadcasted_iota(I32, km.shape, 1) >> 6
        km2 = jnp.concatenate([jnp.where(head == h, km, 0.0) for h in range(2)], axis=0)
        hi = km2.astype(BF16)
        mid = (km2 - hi.astype(F32)).astype(BF16)
        lo = (km2 - hi.astype(F32) - mid.astype(F32)).astype(BF16)
        kparts_ref[0] = hi
        kparts_ref[1] = mid
        kparts_ref[2] = lo

    qT = qT_ref[0]
    row = lax.broadcasted_iota(I32, qT.shape, 0)
    qpad = [jnp.where((row >> 6) == h, qT, jnp.zeros_like(qT)) for h in range(2)]

    gates = (jnp.dot(kparts_ref[0], qT, preferred_element_type=F32)
             + jnp.dot(kparts_ref[1], qT, preferred_element_type=F32)
             + jnp.dot(kparts_ref[2], qT, preferred_element_type=F32))
    bidx = lax.broadcasted_iota(I32, (nbp, blk), 0)
    rhs = []
    for h in range(2):
        gate = gates[h * nbp:(h + 1) * nbp]
        gsc = jnp.where(bidx < qi, gate, -jnp.inf)
        bias = jnp.full((nbp, blk), NEG_BIG, F32)
        for _ in range(topk):
            mx = jnp.max(gsc, axis=0, keepdims=True)
            idx = jnp.min(jnp.where(gsc == mx, bidx, nbp), axis=0, keepdims=True)
            pick = jnp.logical_and(bidx == idx, mx > -jnp.inf)
            bias = jnp.where(pick, 0.0, bias)
            gsc = jnp.where(pick, -jnp.inf, gsc)
        pad = jnp.zeros((2 * HEAD_DIM - nbp - 16, blk), BF16)
        rhs.append(jnp.concatenate([qpad[h], bias.astype(BF16), srow_ref[h], pad], axis=0))

    def scores(j, a):
        k0 = pl.multiple_of(j * blk, blk)
        lhs = jnp.concatenate([k_ref[0, pl.ds(k0, blk), :], aug_ref[a]], axis=1)
        return [jnp.dot(lhs, rhs[h], preferred_element_type=F32) for h in range(2)]

    def values(j):
        k0 = pl.multiple_of(j * blk, blk)
        return [vT_ref[0, h * HEAD_DIM:(h + 1) * HEAD_DIM, pl.ds(k0, blk)] for h in range(2)]

    def offsets(j):
        dq = ((qi - j) * blk).astype(F32)
        return [sl_ref[h] * dq for h in range(2)]

    ik = lax.broadcasted_iota(I32, (blk, blk), 0)
    iq = lax.broadcasted_iota(I32, (blk, blk), 1)
    s_own = [jnp.where(ik <= iq, s, NEG_BIG) for s in scores(qi, nb)]
    m0 = jnp.full((1, blk), NEG_BIG, F32)

    def first():
        return [_softmax_stage(s_own[h], jnp.zeros((1, blk), F32), m0) for h in range(2)]

    outs = _pipelined_tiles(lambda j: scores(jnp.minimum(j, nb - 1), jnp.minimum(j, nb - 1)),
                            values, offsets, s_ref, p_ref, first, 2, qi, qi)
    o_ref[0] = jnp.concatenate(outs, axis=0).T.astype(BF16)


def _moba(mqT, mk, mvT, aug, srow, sl):
    B, _, S = mqT.shape
    blk = MOBA_BLOCK
    nb = S // blk
    topk = min(MOBA_TOPK, nb)
    nbp = -(-nb // 16) * 16
    grid = (B, MOBA_HEADS // 2, nb)
    return pl.pallas_call(
        functools.partial(_moba_kernel, nb=nb, nbp=nbp, topk=topk), grid=grid,
        in_specs=[
            pl.BlockSpec((1, 128, blk), lambda b, p, i: (b, p, i)),
            pl.BlockSpec((1, S, 128), lambda b, p, i: (b, 0, p)),
            pl.BlockSpec((1, 128, S), lambda b, p, i: (b, p, 0)),
            pl.BlockSpec(aug.shape, lambda b, p, i: (0, 0, 0)),
            pl.BlockSpec((2, 16, blk), lambda b, p, i: (p, 0, 0)),
            pl.BlockSpec((2, 1, blk), lambda b, p, i: (p, 0, 0)),
        ],
        out_specs=pl.BlockSpec((1, blk, 128), lambda b, p, i: (b, i, p)),
        out_shape=jax.ShapeDtypeStruct((B, S, 512), BF16),
        scratch_shapes=[
            pltpu.VMEM((nbp, 128), F32),
            pltpu.VMEM((3, 2 * nbp, 128), BF16),
            pltpu.VMEM((2, 2, blk, blk), F32),
            pltpu.VMEM((2, 2, blk, blk), BF16),
        ],
        compiler_params=_params(3), name="moba",
    )(mqT, mk, mvT, aug, srow, sl)


def _compress_kernel(kc_ref, vc_ref, wk_ref, wvT_ref, pek_ref, pev_ref, w2k_ref, w2vT_ref,
                     kcmp_ref, vcmpT_ref):
    nc = kc_ref.shape[1]

    wk = wk_ref[...]
    ab = jnp.dot(kc_ref[0], wk, preferred_element_type=F32)
    pt = (jnp.dot(pek_ref[0], wk[:, 0:128].astype(F32), preferred_element_type=F32)
          + jnp.dot(pek_ref[1], wk[:, 128:256].astype(F32), preferred_element_type=F32))
    pre = ab[:, 0:128] + pltpu.roll(ab[:, 128:256], nc - 1, 0) + pt[0:1]
    hid = jax.nn.gelu(pre)
    kcmp_ref[0] = jnp.dot(hid.astype(BF16), w2k_ref[...], preferred_element_type=F32).astype(BF16)

    wvT = wvT_ref[...]
    abT = lax.dot_general(wvT, vc_ref[0], NT_DIMS, preferred_element_type=F32)
    ptT = (lax.dot_general(wvT[0:128].astype(F32), pev_ref[0], NT_DIMS, preferred_element_type=F32)
           + lax.dot_general(wvT[128:256].astype(F32), pev_ref[1], NT_DIMS, preferred_element_type=F32))
    preT = abT[0:128] + pltpu.roll(abT[128:256], nc - 1, 1) + ptT[:, 0:1]
    hidT = jax.nn.gelu(preT)
    vcmpT_ref[0] = jnp.dot(w2vT_ref[...], hidT.astype(BF16), preferred_element_type=F32).astype(BF16)


def _compress(kc2, vc2, wk, wvT, pek, pev, w2k, w2vT):
    B, nc, _ = kc2.shape
    full = lambda a: pl.BlockSpec(a.shape, lambda b: (0,) * a.ndim)
    blk = pl.BlockSpec((1, nc, kc2.shape[2]), lambda b: (b, 0, 0))
    return pl.pallas_call(
        _compress_kernel, grid=(B,),
        in_specs=[blk, blk, full(wk), full(wvT), full(pek), full(pev), full(w2k), full(w2vT)],
        out_specs=[pl.BlockSpec((1, nc, 128), lambda b: (b, 0, 0)),
                   pl.BlockSpec((1, 128, nc), lambda b: (b, 0, 0))],
        out_shape=[jax.ShapeDtypeStruct((B, nc, 128), BF16),
                   jax.ShapeDtypeStruct((B, 128, nc), BF16)],
        compiler_params=_params(1), name="nsa_compress",
    )(kc2, vc2, wk, wvT, pek, pev, w2k, w2vT)


def _nsa_kernel(qT_ref, kcmp_ref, vcmpT_ref, ks_ref, vsT_ref, kw_ref, vwT_ref,
                g_ref, sl_ref, srow_ref, auga_ref, augb_ref, augc_ref, tblc_ref, bw_ref, o_ref,
                s_ref, p_ref, pc_ref, flagv_ref, flags_ref, list_ref, fsem, *, n_slc, topn):
    g = pl.program_id(1)
    qi = pl.program_id(2)
    tq = NSA_TQ
    hg = NSA_GROUP
    wl = hg * tq
    q0 = qi * tq

    q4 = qT_ref[0]
    qT = jnp.concatenate([q4[h * HEAD_DIM:(h + 1) * HEAD_DIM] for h in range(hg)], axis=1)
    qT2 = jnp.concatenate([qT, qT], axis=0)
    rowi = lax.broadcasted_iota(I32, qT2.shape, 0)
    qpad = jnp.where((rowi >> 6) == g, qT2, jnp.zeros_like(qT2))
    slope = sl_ref[0]
    lane = lax.broadcasted_iota(I32, (1, wl), 1)
    t_q = q0 + (lane & (tq - 1))

    nc = kcmp_ref.shape[1]
    rhs_top = jnp.where((rowi >> 6) == g, qT2, srow_ref[0])
    mine_c = (lax.broadcasted_iota(I32, (nc, LANES), 1) >> 6) == g
    lhs_c = jnp.where(mine_c, kcmp_ref[0], augc_ref[0])
    first_c = pl.multiple_of(nc - qi * (tq // NSA_CMP_STRIDE), 8)
    z = jnp.dot(lhs_c, rhs_top, preferred_element_type=F32) + tblc_ref[pl.ds(first_c, nc), :]
    mx = jnp.max(z, axis=0, keepdims=True)
    e = jnp.exp2(z - mx)
    den = jnp.maximum(jnp.sum(e, axis=0, keepdims=True), 1e-30)
    p = e * jnp.where(t_q >= NSA_CMP_LEN - 1, 1.0 / den, 0.0)
    o_c = jnp.dot(vcmpT_ref[0], p.astype(BF16), preferred_element_type=F32)

    pc = p[:, 0:tq]
    for h in range(1, hg):
        pc = pc + p[:, h * tq:(h + 1) * tq]
    n_lc = tq // LANES
    for c in range(n_lc):
        pc_ref[c] = pc[:, c * LANES:(c + 1) * LANES]
    su = NSA_SLC_BLOCK // NSA_CMP_STRIDE
    x = [jnp.concatenate([pc_ref[c, pl.ds(k, n_slc, stride=su), :] for c in range(n_lc)], axis=1)
         for k in range(su)]
    jb = lax.broadcasted_iota(I32, (n_slc, tq), 0)
    prev = jnp.where(jb == 0, 0.0, pltpu.roll(x[3], 1, 0))
    imp = 2.0 * (x[0] + x[1] + x[2]) + x[3] + prev
    cur = (q0 + lax.broadcasted_iota(I32, (1, tq), 1)) >> 6
    allowed = jb <= cur
    forced = jnp.logical_or(jb == 0, jnp.logical_or(jb == cur, jb == cur - 1))
    bias = jnp.where(jnp.logical_and(allowed, forced), 0.0, NEG_BIG)
    sc = jnp.where(jnp.logical_and(allowed, jnp.logical_not(forced)), imp, -1.0)
    for _ in range(topn - 3):
        smx = jnp.max(sc, axis=0, keepdims=True)
        idx = jnp.min(jnp.where(sc == smx, jb, n_slc), axis=0, keepdims=True)
        pick = jnp.logical_and(jb == idx, smx >= 0.0)
        bias = jnp.where(pick, 0.0, bias)
        sc = jnp.where(pick, -1.0, sc)

    n_tiles = augb_ref.shape[0]
    jd = lax.div(q0, SLC_TILE)
    chosen = jnp.where(bias == 0.0, 1.0, 0.0).astype(BF16)
    per_blk = lax.dot_general(jnp.ones((SUBLANES, tq), BF16), chosen, NT_DIMS,
                              preferred_element_type=F32)
    per_shift = (SLC_TILE // NSA_SLC_BLOCK).bit_length() - 1
    in_tile = (lax.broadcasted_iota(I32, (n_slc, LANES), 0) >> per_shift) == lax.broadcasted_iota(
        I32, (n_slc, LANES), 1)
    per_tile = jnp.dot(per_blk.astype(BF16), jnp.where(in_tile, 1.0, 0.0).astype(BF16),
                       preferred_element_type=F32)
    is_past = lax.broadcasted_iota(I32, (SUBLANES, LANES), 1) < jd
    flagv_ref[...] = jnp.where(jnp.logical_and(per_tile > 0.0, is_past), 1, 0).astype(I32)
    flag_copy = pltpu.make_async_copy(flagv_ref, flags_ref, fsem)
    flag_copy.start()

    start = pl.multiple_of(jnp.maximum(q0 - NSA_WINDOW, 0), tq)
    first_w = pl.multiple_of(NSA_WINDOW - (q0 - start), tq)
    kt = kw_ref[0, pl.ds(start, WIN_KEYS), :]
    z = jnp.dot(kt, qpad, preferred_element_type=F32) + bw_ref[0, pl.ds(first_w, WIN_KEYS), :]
    mx = jnp.max(z, axis=0, keepdims=True)
    p = jnp.exp2(z - mx)
    den = jnp.maximum(jnp.sum(p, axis=0, keepdims=True), 1e-30)
    o_w = jnp.dot(vwT_ref[0, :, pl.ds(start, WIN_KEYS)], p.astype(BF16),
                  preferred_element_type=F32) / den

    flag_copy.wait()
    for i in range(TILE_LIST_LEN):
        list_ref[i] = 0

    def add_tile(t, n):
        list_ref[n] = t
        return n + flags_ref[0, t]
    n_used = lax.fori_loop(0, n_tiles, add_tile, 0)
    list_ref[DIAG_SLOT] = jd

    wc = NSA_CHAIN_LANES
    n_ch = wl // wc

    def lane_split(a):
        return [a[:, c * wc:(c + 1) * wc] for c in range(n_ch)]

    if n_slc < LANES:
        bias = jnp.concatenate([bias, jnp.zeros((LANES - n_slc, tq), F32)], axis=0)
    bias4 = jnp.concatenate([bias.astype(BF16)] * hg, axis=1)
    rhs = jnp.concatenate([rhs_top, bias4], axis=0)
    mine = (lax.broadcasted_iota(I32, (SLC_TILE, LANES), 1) >> 6) == g

    def scores(j, null):
        k0 = pl.multiple_of(j * SLC_TILE, SLC_TILE)
        kt = ks_ref[0, pl.ds(k0, SLC_TILE), :]
        lhs = jnp.concatenate([jnp.where(mine, kt, auga_ref[null]), augb_ref[j]], axis=1)
        return jnp.dot(lhs, rhs, preferred_element_type=F32)

    def values(i):
        k0 = pl.multiple_of(list_ref[i] * SLC_TILE, SLC_TILE)
        return [vsT_ref[0, :, pl.ds(k0, SLC_TILE)]] * n_ch

    def offsets(i):
        return lane_split(slope * (q0 - list_ref[i] * SLC_TILE).astype(F32))

    def past_scores(i):
        return lane_split(scores(list_ref[jnp.minimum(i, n_tiles - 1)], (i >= n_used).astype(I32)))

    t_k = jd * SLC_TILE + lax.broadcasted_iota(I32, (SLC_TILE, wl), 0)
    s_diag = lane_split(jnp.where(t_k <= t_q, scores(jd, 0), NEG_BIG))
    c_diag = offsets(DIAG_SLOT)

    def first():
        return [_softmax_stage(s_diag[c], c_diag[c], jnp.full((1, wc), NEG_BIG, F32))
                for c in range(n_ch)]

    o_s = jnp.concatenate(
        _pipelined_tiles(past_scores, values, offsets, s_ref, p_ref, first, n_ch, n_used, DIAG_SLOT),
        axis=1)

    gt = g_ref[0]

    def gate_row(br):
        return jnp.concatenate([gt[br * hg + h:br * hg + h + 1] for h in range(hg)], axis=1)

    o = gate_row(0) * o_c + gate_row(1) * o_s + gate_row(2) * o_w
    o4 = jnp.concatenate([o[:, h * tq:(h + 1) * tq] for h in range(hg)], axis=0)
    o_ref[0] = o4.T.astype(BF16)


def _nsa(nqT, kcmp, vcmpT, ks, vsT, kw, vwT, gT, sl, srow, auga, augb, augc, tblc, bw):
    B, _, S = nqT.shape
    tq = NSA_TQ
    nc = kcmp.shape[1]
    n_slc = S // NSA_SLC_BLOCK
    topn = min(NSA_SLC_TOPN, n_slc)
    wl = NSA_GROUP * tq
    grid = (B, NSA_KV_HEADS, S // tq)
    return pl.pallas_call(
        functools.partial(_nsa_kernel, n_slc=n_slc, topn=topn), grid=grid,
        in_specs=[
            pl.BlockSpec((1, NSA_GROUP * HEAD_DIM, tq), lambda b, g, i: (b, g, i)),
            pl.BlockSpec((1, nc, 128), lambda b, g, i: (b, 0, 0)),
            pl.BlockSpec((1, HEAD_DIM, nc), lambda b, g, i: (b, g, 0)),
            pl.BlockSpec((1, S, 128), lambda b, g, i: (b, 0, 0)),
            pl.BlockSpec((1, HEAD_DIM, S), lambda b, g, i: (b, g, 0)),
            pl.BlockSpec((1, S, 128), lambda b, g, i: (b, 0, 0)),
            pl.BlockSpec((1, HEAD_DIM, S), lambda b, g, i: (b, g, 0)),
            pl.BlockSpec((1, 16, tq), lambda b, g, i: (b, g, i)),
            pl.BlockSpec((1, 1, wl), lambda b, g, i: (g, 0, 0)),
            pl.BlockSpec((1, 2 * HEAD_DIM, wl), lambda b, g, i: (g, 0, 0)),
            pl.BlockSpec((2, SLC_TILE, LANES), lambda b, g, i: (g, 0, 0)),
            pl.BlockSpec(augb.shape, lambda b, g, i: (0, 0, 0)),
            pl.BlockSpec((1, nc, LANES), lambda b, g, i: (g, 0, 0)),
            pl.BlockSpec(tblc.shape, lambda b, g, i: (0, 0)),
            pl.BlockSpec((1,) + bw.shape[1:], lambda b, g, i: (g, 0, 0)),
        ],
        out_specs=pl.BlockSpec((1, tq, NSA_GROUP * HEAD_DIM), lambda b, g, i: (b, i, g)),
        out_shape=jax.ShapeDtypeStruct((B, S, 512), BF16),
        scratch_shapes=[
            pltpu.VMEM((2, wl // NSA_CHAIN_LANES, SLC_TILE, NSA_CHAIN_LANES), F32),
            pltpu.VMEM((2, wl // NSA_CHAIN_LANES, SLC_TILE, NSA_CHAIN_LANES), BF16),
            pltpu.VMEM((tq // LANES, nc, LANES), F32),
            pltpu.VMEM((SUBLANES, LANES), I32),
            pltpu.SMEM((SUBLANES, LANES), I32),
            pltpu.SMEM((TILE_LIST_LEN,), I32),
            pltpu.SemaphoreType.DMA,
        ],
        compiler_params=_params(3), name="nsa",
    )(nqT, kcmp, vcmpT, ks, vsT, kw, vwT, gT, sl, srow, auga, augb, augc, tblc, bw)


def _outproj_kernel(om_ref, on_ref, x_ref, wo_ref, g_ref, wr_ref, br_ref,
                    x1_ref, hn_ref, e_ref, w_ref):
    attn = (jnp.dot(om_ref[...], wo_ref[0:512, :], preferred_element_type=F32)
            + jnp.dot(on_ref[...], wo_ref[512:1024, :], preferred_element_type=F32))
    x1 = x_ref[...] + attn
    x1_ref[...] = x1
    hn = _rmsnorm(x1, g_ref[...])
    _store_token_tiles(hn_ref, hn)
    logits = jnp.dot(hn, wr_ref[...], precision=lax.Precision.HIGHEST,
                     preferred_element_type=F32) + br_ref[...]
    tm = logits.shape[0]
    lane = lax.broadcasted_iota(I32, (tm, LANES), 1)
    sc = jnp.where(lane < N_EXPERTS, logits, -jnp.inf)
    e_out = jnp.zeros((tm, LANES), I32)
    vals = []
    for k in range(TOP_K):
        mx = jnp.max(sc, axis=1, keepdims=True)
        idx = jnp.min(jnp.where(sc == mx, lane, LANES), axis=1, keepdims=True)
        e_out = jnp.where(lane == k, idx, e_out)
        sc = jnp.where(lane == idx, -jnp.inf, sc)
        vals.append(mx)
    ex = [jnp.exp(v - vals[0]) for v in vals]
    den = ex[0] + ex[1] + ex[2] + ex[3]
    w_out = jnp.zeros((tm, LANES), F32)
    for k in range(TOP_K):
        w_out = jnp.where(lane == k, ex[k] / den, w_out)
    e_ref[...] = e_out
    w_ref[...] = w_out


def _outproj(om, on, x, wo, g, wr, br, tm):
    N, D = x.shape
    full = lambda a: pl.BlockSpec(a.shape, lambda i: (0,) * a.ndim)
    row = lambda w: pl.BlockSpec((tm, w), lambda i: (i, 0))
    return pl.pallas_call(
        _outproj_kernel, grid=(N // tm,),
        in_specs=[row(512), row(512), row(D), full(wo), full(g), full(wr), full(br)],
        out_specs=[row(D), pl.BlockSpec((tm * SUBLANES, LANES), lambda i: (i, 0)), row(LANES), row(LANES)],
        out_shape=[jax.ShapeDtypeStruct((N, D), F32), jax.ShapeDtypeStruct((N * SUBLANES, LANES), F32),
                   jax.ShapeDtypeStruct((N, LANES), I32), jax.ShapeDtypeStruct((N, LANES), F32)],
        compiler_params=_params(1), name="outproj_router",
    )(om, on, x, wo, g, wr, br)


def _rank_kernel(e_ref, rank_ref, cnt_ref, base_ref):
    i = pl.program_id(0)
    T = e_ref.shape[0]

    @pl.when(i == 0)
    def _():
        base_ref[...] = jnp.zeros(base_ref.shape, F32)

    e = e_ref[...]
    lane = lax.broadcasted_iota(I32, (T, LANES), 1)
    tril = jnp.where(lax.broadcasted_iota(I32, (T, T), 0) >= lax.broadcasted_iota(I32, (T, T), 1),
                     1.0, 0.0).astype(BF16)
    out = jnp.zeros((T, LANES), I32)
    for k in range(TOP_K):
        hit = lane == e[:, k:k + 1]
        oh = jnp.where(hit, 1.0, 0.0)
        cum = jnp.dot(tril, oh.astype(BF16), preferred_element_type=F32)
        base = base_ref[0:1, :]
        r = jnp.sum(jnp.where(hit, cum - 1.0 + base, 0.0), axis=1, keepdims=True)
        out = jnp.where(lane == k, r.astype(I32), out)
        base_ref[...] = base_ref[...] + jnp.sum(oh, axis=0, keepdims=True)
    rank_ref[...] = out
    cnt_ref[...] = base_ref[...]


def _ranks(e128):
    N = e128.shape[0]
    T = RANK_TILE
    return pl.pallas_call(
        _rank_kernel, grid=(N // T,),
        in_specs=[pl.BlockSpec((T, LANES), lambda i: (i, 0))],
        out_specs=[pl.BlockSpec((T, LANES), lambda i: (i, 0)),
                   pl.BlockSpec((8, LANES), lambda i: (0, 0))],
        out_shape=[jax.ShapeDtypeStruct((N, LANES), I32),
                   jax.ShapeDtypeStruct((8, LANES), F32)],
        scratch_shapes=[pltpu.VMEM((8, LANES), F32)],
        compiler_params=_params(1), name="route_ranks",
    )(e128)


def _row_copy(src, dst, i_src, i_dst, sem):
    return pltpu.make_async_copy(src.at[pl.ds(pl.multiple_of(i_src * SUBLANES, SUBLANES), SUBLANES)],
                                 dst.at[pl.ds(pl.multiple_of(i_dst * SUBLANES, SUBLANES), SUBLANES)], sem)


def _store_token_tiles(ref, x):
    rows = x.shape[0]
    for c in range(SUBLANES):
        ref[pl.ds(c, rows, stride=SUBLANES), :] = x[:, c * LANES:(c + 1) * LANES]


def _load_token_tiles(ref, rows):
    return jnp.concatenate([ref[pl.ds(c, rows, stride=SUBLANES), :] for c in range(SUBLANES)], axis=1)


def _dispatch_kernel(dest_hbm, hp_ref, xz_hbm, out_hbm, idx_ref, isem, sem):
    del xz_hbm
    i = pl.program_id(0)
    T = ROUTE_TILE
    cp = pltpu.make_async_copy(dest_hbm.at[i], idx_ref, isem)
    cp.start()
    cp.wait()

    def issue(t, carry):
        for k in range(TOP_K):
            _row_copy(hp_ref, out_hbm, t, idx_ref[t * TOP_K + k], sem).start(priority=k % 2)
        return carry

    def drain(t, carry):
        for k in range(TOP_K):
            _row_copy(hp_ref, out_hbm, 0, 0, sem).wait()
        return carry

    for t0 in range(0, T, ROW_DMA_BATCH):
        lax.fori_loop(t0, t0 + ROW_DMA_BATCH, issue, 0, unroll=ROW_DMA_UNROLL)
        lax.fori_loop(t0, t0 + ROW_DMA_BATCH, drain, 0, unroll=ROW_DMA_UNROLL)


def _dispatch(dest2, hp, xzero):
    nsteps = dest2.shape[0]
    T = ROUTE_TILE
    return pl.pallas_call(
        _dispatch_kernel, grid=(nsteps,),
        in_specs=[pl.BlockSpec(memory_space=pl.ANY),
                  pl.BlockSpec((T * SUBLANES, LANES), lambda i: (i, 0)),
                  pl.BlockSpec(memory_space=pl.ANY)],
        out_specs=pl.BlockSpec(memory_space=pl.ANY),
        out_shape=jax.ShapeDtypeStruct(xzero.shape, xzero.dtype),
        scratch_shapes=[pltpu.SMEM((T * TOP_K,), I32),
                        pltpu.SemaphoreType.DMA, pltpu.SemaphoreType.DMA],
        input_output_aliases={2: 0},
        compiler_params=_params(1), name="moe_dispatch",
    )(dest2, hp, xzero)


W_PREP_COLS = 256


def _expert_kernel(be_ref, na_ref, x_ref, wup_ref, bg_ref, bu_ref, wd_ref, bd_ref, y_ref,
                   wg_ref, wu_ref, wt_ref):
    b = pl.program_id(0)
    active = b < na_ref[0]
    new_expert = jnp.logical_or(b == 0, be_ref[b] != be_ref[jnp.maximum(b - 1, 0)])

    @pl.when(jnp.logical_and(active, new_expert))
    def _():
        half = W_PREP_COLS // 2
        n_lc = wt_ref.shape[0]

        def every_other(first):
            return jnp.concatenate(
                [wt_ref[c, pl.ds(first, half, stride=2), :] for c in range(n_lc)], axis=1).astype(BF16)

        for ch in range(wup_ref.shape[2] // W_PREP_COLS):
            panel = wup_ref[0, :, ch * W_PREP_COLS:(ch + 1) * W_PREP_COLS].T
            for c in range(n_lc):
                wt_ref[c] = panel[:, c * LANES:(c + 1) * LANES]
            wg_ref[ch * half:(ch + 1) * half, :] = every_other(0)
            wu_ref[ch * half:(ch + 1) * half, :] = every_other(1)

    @pl.when(active)
    def _():
        xb = _load_token_tiles(x_ref, MOE_ROWS).astype(BF16)
        gg = lax.dot_general(xb, wg_ref[...], NT_DIMS, preferred_element_type=F32) + bg_ref[0]
        uu = lax.dot_general(xb, wu_ref[...], NT_DIMS, preferred_element_type=F32) + bu_ref[0]
        gg = jnp.minimum(gg, SWIGLU_LIMIT)
        uu = jnp.clip(uu, -SWIGLU_LIMIT, SWIGLU_LIMIT)
        a = gg * jax.nn.sigmoid(SWIGLU_ALPHA * gg) * (uu + 1.0)
        _store_token_tiles(y_ref, jnp.dot(a.astype(BF16), wd_ref[0], preferred_element_type=F32) + bd_ref[0])

    @pl.when(jnp.logical_not(active))
    def _():
        y_ref[...] = jnp.zeros(y_ref.shape, F32)


def _experts(blk_e, n_act, xrows, w_up, bg, bu, wd, bd):
    _, D, F2 = w_up.shape
    F = F2 // 2
    assert D == SUBLANES * LANES
    P = xrows.shape[0] // SUBLANES
    n_blk = P // MOE_ROWS
    rows_spec = pl.BlockSpec((MOE_ROWS * SUBLANES, LANES), lambda b, be, na: (b, 0))
    wspec = lambda r, c: pl.BlockSpec((1, r, c), lambda b, be, na: (be[b], 0, 0))
    grid_spec = pltpu.PrefetchScalarGridSpec(
        num_scalar_prefetch=2, grid=(n_blk,),
        in_specs=[rows_spec, wspec(D, F2), wspec(1, F), wspec(1, F), wspec(F, D), wspec(1, D)],
        out_specs=rows_spec,
        scratch_shapes=[pltpu.VMEM((F, D), BF16), pltpu.VMEM((F, D), BF16),
                        pltpu.VMEM((D // LANES, W_PREP_COLS, LANES), F32)],
    )
    return pl.pallas_call(
        _expert_kernel, grid_spec=grid_spec,
        out_shape=jax.ShapeDtypeStruct(xrows.shape, F32),
        compiler_params=_params(1), name="moe_experts",
    )(blk_e, n_act, xrows, w_up, bg, bu, wd, bd)


def _combine_kernel(dest_hbm, x1_ref, w_ref, g_ref, y_hbm, o_ref, idx_ref, buf_ref, isem, sem):
    i = pl.program_id(0)
    T = ROUTE_TILE
    cp = pltpu.make_async_copy(dest_hbm.at[i], idx_ref, isem)
    cp.start()
    cp.wait()

    def issue(t, carry):
        for k in range(TOP_K):
            _row_copy(y_hbm, buf_ref.at[k], idx_ref[t * TOP_K + k], t, sem).start(priority=k % 2)
        return carry

    def drain(t, carry):
        for k in range(TOP_K):
            _row_copy(y_hbm, buf_ref.at[k], 0, 0, sem).wait()
        return carry

    for t0 in range(0, T, ROW_DMA_BATCH):
        lax.fori_loop(t0, t0 + ROW_DMA_BATCH, issue, 0, unroll=ROW_DMA_UNROLL)
        lax.fori_loop(t0, t0 + ROW_DMA_BATCH, drain, 0, unroll=ROW_DMA_UNROLL)

    x1 = x1_ref[...]
    w = w_ref[...]
    cols = []
    for c in range(SUBLANES):
        acc = x1[:, c * LANES:(c + 1) * LANES]
        for k in range(TOP_K):
            acc = acc + w[:, k:k + 1] * buf_ref[k, pl.ds(c, T, stride=SUBLANES), :]
        cols.append(acc)
    o_ref[...] = _rmsnorm(jnp.concatenate(cols, axis=1), g_ref[...])


def _combine(dest2, x1, w128, g, yrows):
    N, D = x1.shape
    T = ROUTE_TILE
    return pl.pallas_call(
        _combine_kernel, grid=(N // T,),
        in_specs=[pl.BlockSpec(memory_space=pl.ANY),
                  pl.BlockSpec((T, D), lambda i: (i, 0)),
                  pl.BlockSpec((T, LANES), lambda i: (i, 0)),
                  pl.BlockSpec(g.shape, lambda i: (0, 0)),
                  pl.BlockSpec(memory_space=pl.ANY)],
        out_specs=pl.BlockSpec((T, D), lambda i: (i, 0)),
        out_shape=jax.ShapeDtypeStruct((N, D), F32),
        scratch_shapes=[pltpu.SMEM((T * TOP_K,), I32),
                        pltpu.VMEM((TOP_K, T * SUBLANES, LANES), F32),
                        pltpu.SemaphoreType.DMA, pltpu.SemaphoreType.DMA],
        compiler_params=_params(1), name="moe_combine",
    )(dest2, x1, w128, g, yrows)


def _alibi_slopes():
    n = MOBA_HEADS + NSA_HEADS
    s = jnp.exp2(-8.0 * jnp.arange(1, n + 1, dtype=F32) / n)
    return s[0::2], s[1::2]


def _prep_inproj(w_in):
    hd = HEAD_DIM
    sizes = [MOBA_HEADS * hd] * 3 + [NSA_HEADS * hd] + [NSA_KV_HEADS * hd] * 6 + [NSA_BRANCHES * NSA_HEADS]
    cuts = np.cumsum([0] + sizes)
    mq, mk, mv, nq, kc, vc, ks, vs, kw, vw, ng = [w_in[:, cuts[i]:cuts[i + 1]] for i in range(11)]
    qscale = (hd ** -0.5) * LOG2E
    wr = jnp.concatenate([mk, kc, vc, ks, kw], axis=1).astype(BF16)
    wt = jnp.concatenate([mq * qscale, mv, nq * qscale, vs, vw], axis=1).T.astype(BF16)
    ngr = ng.reshape(-1, NSA_KV_HEADS, NSA_GROUP, NSA_BRANCHES).transpose(1, 3, 2, 0)
    ngr = ngr.reshape(NSA_KV_HEADS, NSA_BRANCHES * NSA_GROUP, -1)
    wg = jnp.pad(ngr, ((0, 0), (0, 16 - NSA_BRANCHES * NSA_GROUP), (0, 0))).reshape(32, -1)
    return wr, wt, wg.astype(F32)


def _prep_compress(w1, w2, pe):
    hd, half = HEAD_DIM, NSA_CMP_STRIDE
    w1r = w1.reshape(2, half, hd, hd)
    eye = jnp.eye(NSA_KV_HEADS, dtype=w1.dtype)
    w = jnp.einsum('alde,gh->lgdahe', w1r, eye).reshape(half * NSA_KV_HEADS * hd, 2 * NSA_KV_HEADS * hd)
    w2b = jnp.einsum('de,gh->gdhe', w2, eye).reshape(NSA_KV_HEADS * hd, NSA_KV_HEADS * hd)
    per = pe.reshape(2, half, 1, hd)
    pe2 = jnp.broadcast_to(per, (2, half, NSA_KV_HEADS, hd)).reshape(2, 1, half * NSA_KV_HEADS * hd)
    pe2 = jnp.broadcast_to(pe2, (2, 8, pe2.shape[2]))
    return w.astype(BF16), w2b.astype(BF16), pe2.astype(F32)


def _attention_tables(S):
    moba_sl, nsa_sl = _alibi_slopes()
    moba_sl = moba_sl * LOG2E
    nsa_sl = nsa_sl * LOG2E
    blk = MOBA_BLOCK
    moba_row = jnp.broadcast_to(moba_sl[:, None, None], (MOBA_HEADS, 1, blk))
    nb = S // blk
    nbp = -(-nb // 16) * 16
    col = jnp.arange(LANES)[None, None, :]
    tile = jnp.arange(nb + 1)[:, None, None]
    off = jnp.arange(blk, dtype=F32)[None, :, None]
    moba_aug = jnp.where(jnp.logical_and(col == tile, tile < nb), 1.0,
                         jnp.where(jnp.logical_and(col >= nbp, col < nbp + 3), off, 0.0)).astype(BF16)
    parts = jnp.stack(list(_split3(moba_sl)) + [jnp.zeros_like(moba_sl)] * 13, axis=1)
    moba_srow = jnp.broadcast_to(parts[:, :, None], (MOBA_HEADS, 16, blk)).astype(BF16)

    wl = NSA_GROUP * NSA_TQ
    n_slc = S // NSA_SLC_BLOCK
    assert n_slc <= LANES, "block-choice rows must fit the spare contraction rows"
    nsa_row = jnp.repeat(nsa_sl.reshape(NSA_KV_HEADS, NSA_GROUP), NSA_TQ, axis=1)
    hi, mid, lo = [t[:, None, :] for t in _split3(nsa_row)]
    base = ((1 - jnp.arange(NSA_KV_HEADS)) * HEAD_DIM)[:, None, None]
    rows = jnp.arange(2 * HEAD_DIM)[None, :, None]
    nsa_srow = jnp.where(rows == base, hi, jnp.where(rows == base + 1, mid, jnp.where(
        rows == base + 2, lo, jnp.where(rows == base + 3, NEG_BIG, 0.0)))).astype(BF16)
    lane = jnp.arange(LANES)[None, None, None, :]
    base4 = base[:, None]
    null = jnp.arange(2, dtype=F32)[None, :, None, None]
    koff = jnp.arange(SLC_TILE, dtype=F32)[None, None, :, None]
    nsa_auga = jnp.where(jnp.logical_and(lane >= base4, lane < base4 + 3), koff,
                         jnp.where(lane == base4 + 3, null, 0.0))
    nsa_auga = nsa_auga.reshape(2 * NSA_KV_HEADS, SLC_TILE, LANES).astype(BF16)
    per = SLC_TILE // NSA_SLC_BLOCK
    tile = jnp.arange(S // SLC_TILE)[:, None, None]
    blk_of = tile * per + jnp.arange(SLC_TILE)[None, :, None] // NSA_SLC_BLOCK
    nsa_augb = (jnp.arange(LANES)[None, None, :] == blk_of).astype(BF16)

    nc = S // NSA_CMP_STRIDE
    ci = jnp.arange(nc)[None, :, None]
    lane3 = jnp.arange(LANES)[None, None, :]
    nsa_augc = jnp.where(jnp.logical_and(lane3 >= base + 4, lane3 < base + 7), (ci >> 1).astype(F32),
                         jnp.where(jnp.logical_and(lane3 >= base + 7, lane3 < base + 10),
                                   (ci & 1).astype(F32), 0.0)).astype(BF16)
    step2 = [t[:, None, :] for t in _split3(nsa_row * (2.0 * NSA_CMP_STRIDE))]
    step1 = [t[:, None, :] for t in _split3(nsa_row * (1.0 * NSA_CMP_STRIDE))]
    for k in range(3):
        nsa_srow = jnp.where(rows == base + 4 + k, step2[k].astype(BF16),
                             jnp.where(rows == base + 7 + k, step1[k].astype(BF16), nsa_srow))
    il = jnp.tile(jnp.arange(NSA_TQ), NSA_GROUP)[None, :]
    rel = (jnp.arange(2 * nc) - nc)[:, None]
    nsa_tblc = jnp.where(rel * NSA_CMP_STRIDE + (NSA_CMP_LEN - 1) <= il, 0.0, NEG_BIG).astype(F32)
    dist = (NSA_WINDOW + il - jnp.arange(NSA_WINDOW + WIN_KEYS)[:, None])[None]
    nsa_bw = jnp.where(jnp.logical_and(dist >= 0, dist < NSA_WINDOW),
                       -nsa_row[:, None, :] * dist.astype(F32), NEG_BIG)
    return ((moba_aug, moba_srow, moba_row),
            (nsa_row.reshape(NSA_KV_HEADS, 1, wl), nsa_srow, nsa_auga, nsa_augb, nsa_augc, nsa_tblc, nsa_bw))


def _split3(x):
    hi = x.astype(BF16).astype(F32)
    mid = (x - hi).astype(BF16).astype(F32)
    lo = (x - hi - mid).astype(BF16).astype(F32)
    return hi, mid, lo


def _attention(x, attn_norm_g, w_in, cmp_pe_k, cmp_pe_v, cmp_w1_k, cmp_w2_k, cmp_w1_v, cmp_w2_v):
    B, S, D = x.shape
    wr, wt, wg = _prep_inproj(w_in)
    (mk, kc, vc, ks, kw, mqT, mvT, nqT, vsT, vwT, gT) = _inproj(
        x, attn_norm_g.reshape(1, D), wr, wt, wg, tm=512)
    moba_tabs, nsa_tabs = _attention_tables(S)
    o_moba = _moba(mqT, mk, mvT, *moba_tabs)
    wk, w2k, pek = _prep_compress(cmp_w1_k, cmp_w2_k, cmp_pe_k)
    wv, w2v, pev = _prep_compress(cmp_w1_v, cmp_w2_v, cmp_pe_v)
    nc = S // NSA_CMP_STRIDE
    kcmp, vcmpT = _compress(kc.reshape(B, nc, -1), vc.reshape(B, nc, -1), wk, wv.T, pek, pev, w2k, w2v.T)
    o_nsa = _nsa(nqT, kcmp, vcmpT, ks, vsT, kw, vwT, gT, *nsa_tabs)
    return o_moba, o_nsa


def _moe(x1, hn, e128, w128, w_up, b_up, w_down, b_down, final_norm_g):
    N, D = x1.shape
    rank128, cnt = _ranks(e128)
    counts = cnt[0, :N_EXPERTS].astype(I32)
    padded = (counts + MOE_ROWS - 1) // MOE_ROWS * MOE_ROWS
    pends = jnp.cumsum(padded)
    pstarts = pends - padded
    e4 = e128[:, :TOP_K]
    dest = pstarts[e4] + rank128[:, :TOP_K]
    dest2 = dest.reshape(N // ROUTE_TILE, ROUTE_TILE * TOP_K)
    n_blk = (N * TOP_K + N_EXPERTS * MOE_ROWS + MOE_ROWS - 1) // MOE_ROWS
    P = n_blk * MOE_ROWS
    blk_start = jnp.arange(n_blk, dtype=I32) * MOE_ROWS
    blk_e = jnp.minimum(jnp.sum((pends[None, :] <= blk_start[:, None]).astype(I32), axis=1), N_EXPERTS - 1)
    n_act = (pends[-1:] // MOE_ROWS).astype(I32)
    xrows = _dispatch(dest2, hn, jnp.zeros((P * SUBLANES, LANES), F32))
    bg = b_up[:, None, 0::2]
    bu = b_up[:, None, 1::2]
    yrows = _experts(blk_e, n_act, xrows, w_up, bg, bu, w_down.astype(BF16), b_down[:, None, :])
    return _combine(dest2, x1, w128, final_norm_g.reshape(1, D), yrows)


def kernel(x, attn_norm_g, w_in, cmp_pe_k, cmp_pe_v, cmp_w1_k, cmp_w2_k, cmp_w1_v, cmp_w2_v, w_out, ffn_norm_g, w_router, b_router, w_up, b_up, w_down, b_down, final_norm_g):
    B, S, D = x.shape
    assert attn_norm_g.shape[0] == 1, "single-layer kernel"
    o_moba, o_nsa = _attention(x, attn_norm_g[0], w_in[0], cmp_pe_k[0], cmp_pe_v[0],
                               cmp_w1_k[0], cmp_w2_k[0], cmp_w1_v[0], cmp_w2_v[0])
    N = B * S
    wr = jnp.pad(w_router[0], ((0, 0), (0, LANES - N_EXPERTS)))
    br = jnp.pad(b_router[0], (0, LANES - N_EXPERTS)).reshape(1, LANES)
    x1, hn, e128, w128 = _outproj(o_moba.reshape(N, -1), o_nsa.reshape(N, -1), x.reshape(N, D),
                                  w_out[0].astype(BF16), ffn_norm_g[0].reshape(1, D), wr, br, tm=512)
    out = _moe(x1, hn, e128, w128, w_up[0], b_up[0], w_down[0], b_down[0], final_norm_g)
    return out.reshape(B, S, D)
```

```python
import functools

import jax
import jax.numpy as jnp
import numpy as np
from jax import lax
from jax.experimental import pallas as pl
from jax.experimental.pallas import tpu as pltpu

F32 = jnp.float32
BF16 = jnp.bfloat16
I32 = jnp.int32

HEAD_DIM = 64
MOBA_HEADS = 8
NSA_HEADS = 8
NSA_KV_HEADS = 2
NSA_GROUP = NSA_HEADS // NSA_KV_HEADS
MOBA_BLOCK = 256
MOBA_TOPK = 3
NSA_CMP_LEN = 32
NSA_CMP_STRIDE = 16
NSA_SLC_BLOCK = 64
NSA_SLC_TOPN = 16
NSA_WINDOW = 512
NSA_BRANCHES = 3
N_EXPERTS = 32
TOP_K = 4
SWIGLU_LIMIT = 7.0
SWIGLU_ALPHA = 1.702
RMS_EPS = 1e-5
NEG_BIG = -1e30
LOG2E = 1.4426950408889634

LANES = 128
SUBLANES = 8
VMEM_LIMIT = 56 * 1024 * 1024

NSA_TQ = 256
NSA_CHAIN_LANES = 256
TILE_LIST_LEN = 64
DIAG_SLOT = TILE_LIST_LEN - 1
SLC_TILE = 256
WIN_KEYS = NSA_WINDOW + NSA_TQ
MOE_ROWS = 512
ROUTE_TILE = 1024
RANK_TILE = 512
ROW_DMA_UNROLL = 8
ROW_DMA_BATCH = 256

NT_DIMS = (((1,), (1,)), ((), ()))


def _params(n_grid):
    return pltpu.CompilerParams(
        dimension_semantics=("arbitrary",) * n_grid,
        vmem_limit_bytes=VMEM_LIMIT,
    )


def _rmsnorm(x, g):
    return x * lax.rsqrt(jnp.mean(x * x, axis=-1, keepdims=True) + RMS_EPS) * g


def _inproj_kernel(x_ref, g_ref, wr_ref, wt_ref, wg_ref,
                   mk_ref, kc_ref, vc_ref, ks_ref, kw_ref,
                   mqT_ref, mvT_ref, nqT_ref, vsT_ref, vwT_ref, gT_ref):
    xn = _rmsnorm(x_ref[0], g_ref[...])
    xb = xn.astype(BF16)
    yr = jnp.dot(xb, wr_ref[...], preferred_element_type=F32)
    mk_ref[0] = yr[:, 0:512].astype(BF16)
    kc_ref[0] = yr[:, 512:640].astype(BF16)
    vc_ref[0] = yr[:, 640:768].astype(BF16)
    ks_ref[0] = yr[:, 768:896].astype(BF16)
    kw_ref[0] = yr[:, 896:1024].astype(BF16)
    yt = lax.dot_general(wt_ref[...], xb, NT_DIMS, preferred_element_type=F32)
    mqT_ref[0] = yt[0:512].astype(BF16)
    mvT_ref[0] = yt[512:1024].astype(BF16)
    nqT_ref[0] = yt[1024:1536].astype(BF16)
    vsT_ref[0] = yt[1536:1664].astype(BF16)
    vwT_ref[0] = yt[1664:1792].astype(BF16)
    gl = lax.dot_general(wg_ref[...], xn, NT_DIMS, precision=lax.Precision.HIGHEST,
                         preferred_element_type=F32)
    gT_ref[0] = jax.nn.sigmoid(gl)


def _inproj(x, g, wr, wt, wg, tm):
    B, S, D = x.shape
    grid = (B, S // tm)
    row = lambda w: pl.BlockSpec((1, tm, w), lambda b, i: (b, i, 0))
    col = lambda h: pl.BlockSpec((1, h, tm), lambda b, i: (b, 0, i))
    full = lambda a: pl.BlockSpec(a.shape, lambda b, i: (0,) * a.ndim)
    out_shape = [
        jax.ShapeDtypeStruct((B, S, 512), BF16),
        jax.ShapeDtypeStruct((B, S, 128), BF16),
        jax.ShapeDtypeStruct((B, S, 128), BF16),
        jax.ShapeDtypeStruct((B, S, 128), BF16),
        jax.ShapeDtypeStruct((B, S, 128), BF16),
        jax.ShapeDtypeStruct((B, 512, S), BF16),
        jax.ShapeDtypeStruct((B, 512, S), BF16),
        jax.ShapeDtypeStruct((B, 512, S), BF16),
        jax.ShapeDtypeStruct((B, 128, S), BF16),
        jax.ShapeDtypeStruct((B, 128, S), BF16),
        jax.ShapeDtypeStruct((B, 32, S), F32),
    ]
    out_specs = [row(512), row(128), row(128), row(128), row(128),
                 col(512), col(512), col(512), col(128), col(128), col(32)]
    return pl.pallas_call(
        _inproj_kernel, grid=grid,
        in_specs=[pl.BlockSpec((1, tm, D), lambda b, i: (b, i, 0)),
                  full(g), full(wr), full(wt), full(wg)],
        out_specs=out_specs, out_shape=out_shape,
        compiler_params=_params(2), name="inproj",
    )(x, g, wr, wt, wg)


ONES_ROWS = 16


def _softmax_stage(s, c, m):
    mt = jnp.max(s, axis=0, keepdims=True) - c
    m_new = jnp.maximum(m, mt)
    alpha = jnp.exp2(m - m_new)
    p = jnp.exp2(s - (m_new + c))
    return m_new, p.astype(BF16), alpha


def _pipelined_tiles(scores, values, offsets, s_ref, p_ref, first, n_ch, n_tiles, j_first):
    chains = range(n_ch)

    def qk_into(slot, j):
        sc = scores(j)
        for c in chains:
            s_ref[slot, c] = sc[c]

    def pv_from(slot, j, alphas, accs):
        out = []
        for c, vt in zip(chains, values(j)):
            vt1 = jnp.concatenate([vt, jnp.ones((ONES_ROWS, vt.shape[1]), BF16)], axis=0)
            out.append(alphas[c] * accs[c] + jnp.dot(vt1, p_ref[slot, c], preferred_element_type=F32))
        return out

    def softmax_into(slot, j, ms):
        cs = offsets(j)
        new = [_softmax_stage(s_ref[slot, c], cs[c], ms[c]) for c in chains]
        for c in chains:
            p_ref[slot, c] = new[c][1]
        return [n[0] for n in new], [n[2] for n in new]

    qk_into(0, 0)
    first = first()
    for c in chains:
        p_ref[1, c] = first[c][1]

    def pair(i, carry):
        ms, alphas, accs, j_prev = carry
        for slot in range(2):
            t = 2 * i + slot
            qk_into(1 - slot, t + 1)
            accs = pv_from(1 - slot, j_prev, alphas, accs)
            ms, alphas = softmax_into(slot, t, ms)
            j_prev = t
        return ms, alphas, accs, j_prev

    n_q = first[0][0].shape[1]
    init = ([f[0] for f in first], [f[2] for f in first],
            [jnp.zeros((HEAD_DIM + ONES_ROWS, n_q), F32)] * n_ch, j_first)
    _, alphas, accs, j_last = lax.fori_loop(0, (n_tiles + 1) // 2, pair, init)
    accs = pv_from(1, j_last, alphas, accs)
    return [accs[c][:HEAD_DIM] / jnp.maximum(accs[c][HEAD_DIM:HEAD_DIM + 1], 1e-30) for c in chains]


def _moba_kernel(qT_ref, k_ref, vT_ref, aug_ref, srow_ref, sl_ref, o_ref,
                 kmean_ref, kparts_ref, s_ref, p_ref, *, nb, nbp, topk):
    qi = pl.program_id(2)
    blk = MOBA_BLOCK

    @pl.when(qi == 0)
    def _():
        kmean_ref[...] = jnp.zeros(kmean_ref.shape, F32)

        def body(n, carry):
            kb = k_ref[0, pl.ds(pl.multiple_of(n * blk, blk), blk), :].astype(F32)
            kmean_ref[pl.ds(n, 1), :] = jnp.mean(kb, axis=0, keepdims=True)
            return carry
        lax.fori_loop(0, nb, body, 0)
        km = kmean_ref[...]
        head = lax.broadcasted_iota(I32, km.shape, 1) >> 6
        km2 = jnp.concatenate([jnp.where(head == h, km, 0.0) for h in range(2)], axis=0)
        hi = km2.astype(BF16)
        mid = (km2 - hi.astype(F32)).astype(BF16)
        lo = (km2 - hi.astype(F32) - mid.astype(F32)).astype(BF16)
        kparts_ref[0] = hi
        kparts_ref[1] = mid
        kparts_ref[2] = lo

    qT = qT_ref[0]
    row = lax.broadcasted_iota(I32, qT.shape, 0)
    qpad = [jnp.where((row >> 6) == h, qT, jnp.zeros_like(qT)) for h in range(2)]

    gates = (jnp.dot(kparts_ref[0], qT, preferred_element_type=F32)
             + jnp.dot(kparts_ref[1], qT, preferred_element_type=F32)
             + jnp.dot(kparts_ref[2], qT, preferred_element_type=F32))
    bidx = lax.broadcasted_iota(I32, (nbp, blk), 0)
    rhs = []
    for h in range(2):
        gate = gates[h * nbp:(h + 1) * nbp]
        gsc = jnp.where(bidx < qi, gate, -jnp.inf)
        bias = jnp.full((nbp, blk), NEG_BIG, F32)
        for _ in range(topk):
            mx = jnp.max(gsc, axis=0, keepdims=True)
            idx = jnp.min(jnp.where(gsc == mx, bidx, nbp), axis=0, keepdims=True)
            pick = jnp.logical_and(bidx == idx, mx > -jnp.inf)
            bias = jnp.where(pick, 0.0, bias)
            gsc = jnp.where(pick, -jnp.inf, gsc)
        pad = jnp.zeros((2 * HEAD_DIM - nbp - 16, blk), BF16)
        rhs.append(jnp.concatenate([qpad[h], bias.astype(BF16), srow_ref[h], pad], axis=0))

    def scores(j, a):
        k0 = pl.multiple_of(j * blk, blk)
        lhs = jnp.concatenate([k_ref[0, pl.ds(k0, blk), :], aug_ref[a]], axis=1)
        return [jnp.dot(lhs, rhs[h], preferred_element_type=F32) for h in range(2)]

    def values(j):
        k0 = pl.multiple_of(j * blk, blk)
        return [vT_ref[0, h * HEAD_DIM:(h + 1) * HEAD_DIM, pl.ds(k0, blk)] for h in range(2)]

    def offsets(j):
        dq = ((qi - j) * blk).astype(F32)
        return [sl_ref[h] * dq for h in range(2)]

    ik = lax.broadcasted_iota(I32, (blk, blk), 0)
    iq = lax.broadcasted_iota(I32, (blk, blk), 1)
    s_own = [jnp.where(ik <= iq, s, NEG_BIG) for s in scores(qi, nb)]
    m0 = jnp.full((1, blk), NEG_BIG, F32)

    def first():
        return [_softmax_stage(s_own[h], jnp.zeros((1, blk), F32), m0) for h in range(2)]

    outs = _pipelined_tiles(lambda j: scores(jnp.minimum(j, nb - 1), jnp.minimum(j, nb - 1)),
                            values, offsets, s_ref, p_ref, first, 2, qi, qi)
    o_ref[0] = jnp.concatenate(outs, axis=0).T.astype(BF16)


def _moba(mqT, mk, mvT, aug, srow, sl):
    B, _, S = mqT.shape
    blk = MOBA_BLOCK
    nb = S // blk
    topk = min(MOBA_TOPK, nb)
    nbp = -(-nb // 16) * 16
    grid = (B, MOBA_HEADS // 2, nb)
    return pl.pallas_call(
        functools.partial(_moba_kernel, nb=nb, nbp=nbp, topk=topk), grid=grid,
        in_specs=[
            pl.BlockSpec((1, 128, blk), lambda b, p, i: (b, p, i)),
            pl.BlockSpec((1, S, 128), lambda b, p, i: (b, 0, p)),
            pl.BlockSpec((1, 128, S), lambda b, p, i: (b, p, 0)),
            pl.BlockSpec(aug.shape, lambda b, p, i: (0, 0, 0)),
            pl.BlockSpec((2, 16, blk), lambda b, p, i: (p, 0, 0)),
            pl.BlockSpec((2, 1, blk), lambda b, p, i: (p, 0, 0)),
        ],
        out_specs=pl.BlockSpec((1, blk, 128), lambda b, p, i: (b, i, p)),
        out_shape=jax.ShapeDtypeStruct((B, S, 512), BF16),
        scratch_shapes=[
            pltpu.VMEM((nbp, 128), F32),
            pltpu.VMEM((3, 2 * nbp, 128), BF16),
            pltpu.VMEM((2, 2, blk, blk), F32),
            pltpu.VMEM((2, 2, blk, blk), BF16),
        ],
        compiler_params=_params(3), name="moba",
    )(mqT, mk, mvT, aug, srow, sl)


def _compress_kernel(kc_ref, vc_ref, wk_ref, wvT_ref, pek_ref, pev_ref, w2k_ref, w2vT_ref,
                     kcmp_ref, vcmpT_ref):
    nc = kc_ref.shape[1]

    wk = wk_ref[...]
    ab = jnp.dot(kc_ref[0], wk, preferred_element_type=F32)
    pt = (jnp.dot(pek_ref[0], wk[:, 0:128].astype(F32), preferred_element_type=F32)
          + jnp.dot(pek_ref[1], wk[:, 128:256].astype(F32), preferred_element_type=F32))
    pre = ab[:, 0:128] + pltpu.roll(ab[:, 128:256], nc - 1, 0) + pt[0:1]
    hid = jax.nn.gelu(pre)
    kcmp_ref[0] = jnp.dot(hid.astype(BF16), w2k_ref[...], preferred_element_type=F32).astype(BF16)

    wvT = wvT_ref[...]
    abT = lax.dot_general(wvT, vc_ref[0], NT_DIMS, preferred_element_type=F32)
    ptT = (lax.dot_general(wvT[0:128].astype(F32), pev_ref[0], NT_DIMS, preferred_element_type=F32)
           + lax.dot_general(wvT[128:256].astype(F32), pev_ref[1], NT_DIMS, preferred_element_type=F32))
    preT = abT[0:128] + pltpu.roll(abT[128:256], nc - 1, 1) + ptT[:, 0:1]
    hidT = jax.nn.gelu(preT)
    vcmpT_ref[0] = jnp.dot(w2vT_ref[...], hidT.astype(BF16), preferred_element_type=F32).astype(BF16)


def _compress(kc2, vc2, wk, wvT, pek, pev, w2k, w2vT):
    B, nc, _ = kc2.shape
    full = lambda a: pl.BlockSpec(a.shape, lambda b: (0,) * a.ndim)
    blk = pl.BlockSpec((1, nc, kc2.shape[2]), lambda b: (b, 0, 0))
    return pl.pallas_call(
        _compress_kernel, grid=(B,),
        in_specs=[blk, blk, full(wk), full(wvT), full(pek), full(pev), full(w2k), full(w2vT)],
        out_specs=[pl.BlockSpec((1, nc, 128), lambda b: (b, 0, 0)),
                   pl.BlockSpec((1, 128, nc), lambda b: (b, 0, 0))],
        out_shape=[jax.ShapeDtypeStruct((B, nc, 128), BF16),
                   jax.ShapeDtypeStruct((B, 128, nc), BF16)],
        compiler_params=_params(1), name="nsa_compress",
    )(kc2, vc2, wk, wvT, pek, pev, w2k, w2vT)


def _nsa_kernel(qT_ref, kcmp_ref, vcmpT_ref, ks_ref, vsT_ref, kw_ref, vwT_ref,
                g_ref, sl_ref, srow_ref, auga_ref, augb_ref, augc_ref, tblc_ref, bw_ref, o_ref,
                s_ref, p_ref, pc_ref, flagv_ref, flags_ref, list_ref, fsem, *, n_slc, topn):
    g = pl.program_id(1)
    qi = pl.program_id(2)
    tq = NSA_TQ
    hg = NSA_GROUP
    wl = hg * tq
    q0 = qi * tq

    q4 = qT_ref[0]
    qT = jnp.concatenate([q4[h * HEAD_DIM:(h + 1) * HEAD_DIM] for h in range(hg)], axis=1)
    qT2 = jnp.concatenate([qT, qT], axis=0)
    rowi = lax.broadcasted_iota(I32, qT2.shape, 0)
    qpad = jnp.where((rowi >> 6) == g, qT2, jnp.zeros_like(qT2))
    slope = sl_ref[0]
    lane = lax.broadcasted_iota(I32, (1, wl), 1)
    t_q = q0 + (lane & (tq - 1))

    nc = kcmp_ref.shape[1]
    rhs_top = jnp.where((rowi >> 6) == g, qT2, srow_ref[0])
    mine_c = (lax.broadcasted_iota(I32, (nc, LANES), 1) >> 6) == g
    lhs_c = jnp.where(mine_c, kcmp_ref[0], augc_ref[0])
    first_c = pl.multiple_of(nc - qi * (tq // NSA_CMP_STRIDE), 8)
    z = jnp.dot(lhs_c, rhs_top, preferred_element_type=F32) + tblc_ref[pl.ds(first_c, nc), :]
    mx = jnp.max(z, axis=0, keepdims=True)
    e = jnp.exp2(z - mx)
    den = jnp.maximum(jnp.sum(e, axis=0, keepdims=True), 1e-30)
    p = e * jnp.where(t_q >= NSA_CMP_LEN - 1, 1.0 / den, 0.0)
    o_c = jnp.dot(vcmpT_ref[0], p.astype(BF16), preferred_element_type=F32)

    pc = p[:, 0:tq]
    for h in range(1, hg):
        pc = pc + p[:, h * tq:(h + 1) * tq]
    n_lc = tq // LANES
    for c in range(n_lc):
        pc_ref[c] = pc[:, c * LANES:(c + 1) * LANES]
    su = NSA_SLC_BLOCK // NSA_CMP_STRIDE
    x = [jnp.concatenate([pc_ref[c, pl.ds(k, n_slc, stride=su), :] for c in range(n_lc)], axis=1)
         for k in range(su)]
    jb = lax.broadcasted_iota(I32, (n_slc, tq), 0)
    prev = jnp.where(jb == 0, 0.0, pltpu.roll(x[3], 1, 0))
    imp = 2.0 * (x[0] + x[1] + x[2]) + x[3] + prev
    cur = (q0 + lax.broadcasted_iota(I32, (1, tq), 1)) >> 6
    allowed = jb <= cur
    forced = jnp.logical_or(jb == 0, jnp.logical_or(jb == cur, jb == cur - 1))
    bias = jnp.where(jnp.logical_and(allowed, forced), 0.0, NEG_BIG)
    sc = jnp.where(jnp.logical_and(allowed, jnp.logical_not(forced)), imp, -1.0)
    for _ in range(topn - 3):
        smx = jnp.max(sc, axis=0, keepdims=True)
        idx = jnp.min(jnp.where(sc == smx, jb, n_slc), axis=0, keepdims=True)
        pick = jnp.logical_and(jb == idx, smx >= 0.0)
        bias = jnp.where(pick, 0.0, bias)
        sc = jnp.where(pick, -1.0, sc)

    n_tiles = augb_ref.shape[0]
    jd = lax.div(q0, SLC_TILE)
    chosen = jnp.where(bias == 0.0, 1.0, 0.0).astype(BF16)
    per_blk = lax.dot_general(jnp.ones((SUBLANES, tq), BF16), chosen, NT_DIMS,
                              preferred_element_type=F32)
    per_shift = (SLC_TILE // NSA_SLC_BLOCK).bit_length() - 1
    in_tile = (lax.broadcasted_iota(I32, (n_slc, LANES), 0) >> per_shift) == lax.broadcasted_iota(
        I32, (n_slc, LANES), 1)
    per_tile = jnp.dot(per_blk.astype(BF16), jnp.where(in_tile, 1.0, 0.0).astype(BF16),
                       preferred_element_type=F32)
    is_past = lax.broadcasted_iota(I32, (SUBLANES, LANES), 1) < jd
    flagv_ref[...] = jnp.where(jnp.logical_and(per_tile > 0.0, is_past), 1, 0).astype(I32)
    flag_copy = pltpu.make_async_copy(flagv_ref, flags_ref, fsem)
    flag_copy.start()

    start = pl.multiple_of(jnp.maximum(q0 - NSA_WINDOW, 0), tq)
    first_w = pl.multiple_of(NSA_WINDOW - (q0 - start), tq)
    kt = kw_ref[0, pl.ds(start, WIN_KEYS), :]
    z = jnp.dot(kt, qpad, preferred_element_type=F32) + bw_ref[0, pl.ds(first_w, WIN_KEYS), :]
    mx = jnp.max(z, axis=0, keepdims=True)
    p = jnp.exp2(z - mx)
    den = jnp.maximum(jnp.sum(p, axis=0, keepdims=True), 1e-30)
    o_w = jnp.dot(vwT_ref[0, :, pl.ds(start, WIN_KEYS)], p.astype(BF16),
                  preferred_element_type=F32) / den

    flag_copy.wait()
    for i in range(TILE_LIST_LEN):
        list_ref[i] = 0

    def add_tile(t, n):
        list_ref[n] = t
        return n + flags_ref[0, t]
    n_used = lax.fori_loop(0, n_tiles, add_tile, 0)
    list_ref[DIAG_SLOT] = jd

    wc = NSA_CHAIN_LANES
    n_ch = wl // wc

    def lane_split(a):
        return [a[:, c * wc:(c + 1) * wc] for c in range(n_ch)]

    if n_slc < LANES:
        bias = jnp.concatenate([bias, jnp.zeros((LANES - n_slc, tq), F32)], axis=0)
    bias4 = jnp.concatenate([bias.astype(BF16)] * hg, axis=1)
    rhs = jnp.concatenate([rhs_top, bias4], axis=0)
    mine = (lax.broadcasted_iota(I32, (SLC_TILE, LANES), 1) >> 6) == g

    def scores(j, null):
        k0 = pl.multiple_of(j * SLC_TILE, SLC_TILE)
        kt = ks_ref[0, pl.ds(k0, SLC_TILE), :]
        lhs = jnp.concatenate([jnp.where(mine, kt, auga_ref[null]), augb_ref[j]], axis=1)
        return jnp.dot(lhs, rhs, preferred_element_type=F32)

    def values(i):
        k0 = pl.multiple_of(list_ref[i] * SLC_TILE, SLC_TILE)
        return [vsT_ref[0, :, pl.ds(k0, SLC_TILE)]] * n_ch

    def offsets(i):
        return lane_split(slope * (q0 - list_ref[i] * SLC_TILE).astype(F32))

    def past_scores(i):
        return lane_split(scores(list_ref[jnp.minimum(i, n_tiles - 1)], (i >= n_used).astype(I32)))

    t_k = jd * SLC_TILE + lax.broadcasted_iota(I32, (SLC_TILE, wl), 0)
    s_diag = lane_split(jnp.where(t_k <= t_q, scores(jd, 0), NEG_BIG))
    c_diag = offsets(DIAG_SLOT)

    def first():
        return [_softmax_stage(s_diag[c], c_diag[c], jnp.full((1, wc), NEG_BIG, F32))
                for c in range(n_ch)]

    o_s = jnp.concatenate(
        _pipelined_tiles(past_scores, values, offsets, s_ref, p_ref, first, n_ch, n_used, DIAG_SLOT),
        axis=1)

    gt = g_ref[0]

    def gate_row(br):
        return jnp.concatenate([gt[br * hg + h:br * hg + h + 1] for h in range(hg)], axis=1)

    o = gate_row(0) * o_c + gate_row(1) * o_s + gate_row(2) * o_w
    o4 = jnp.concatenate([o[:, h * tq:(h + 1) * tq] for h in range(hg)], axis=0)
    o_ref[0] = o4.T.astype(BF16)


def _nsa(nqT, kcmp, vcmpT, ks, vsT, kw, vwT, gT, sl, srow, auga, augb, augc, tblc, bw):
    B, _, S = nqT.shape
    tq = NSA_TQ
    nc = kcmp.shape[1]
    n_slc = S // NSA_SLC_BLOCK
    topn = min(NSA_SLC_TOPN, n_slc)
    wl = NSA_GROUP * tq
    grid = (B, NSA_KV_HEADS, S // tq)
    return pl.pallas_call(
        functools.partial(_nsa_kernel, n_slc=n_slc, topn=topn), grid=grid,
        in_specs=[
            pl.BlockSpec((1, NSA_GROUP * HEAD_DIM, tq), lambda b, g, i: (b, g, i)),
            pl.BlockSpec((1, nc, 128), lambda b, g, i: (b, 0, 0)),
            pl.BlockSpec((1, HEAD_DIM, nc), lambda b, g, i: (b, g, 0)),
            pl.BlockSpec((1, S, 128), lambda b, g, i: (b, 0, 0)),
            pl.BlockSpec((1, HEAD_DIM, S), lambda b, g, i: (b, g, 0)),
            pl.BlockSpec((1, S, 128), lambda b, g, i: (b, 0, 0)),
            pl.BlockSpec((1, HEAD_DIM, S), lambda b, g, i: (b, g, 0)),
            pl.BlockSpec((1, 16, tq), lambda b, g, i: (b, g, i)),
            pl.BlockSpec((1, 1, wl), lambda b, g, i: (g, 0, 0)),
            pl.BlockSpec((1, 2 * HEAD_DIM, wl), lambda b, g, i: (g, 0, 0)),
            pl.BlockSpec((2, SLC_TILE, LANES), lambda b, g, i: (g, 0, 0)),
            pl.BlockSpec(augb.shape, lambda b, g, i: (0, 0, 0)),
            pl.BlockSpec((1, nc, LANES), lambda b, g, i: (g, 0, 0)),
            pl.BlockSpec(tblc.shape, lambda b, g, i: (0, 0)),
            pl.BlockSpec((1,) + bw.shape[1:], lambda b, g, i: (g, 0, 0)),
        ],
        out_specs=pl.BlockSpec((1, tq, NSA_GROUP * HEAD_DIM), lambda b, g, i: (b, i, g)),
        out_shape=jax.ShapeDtypeStruct((B, S, 512), BF16),
        scratch_shapes=[
            pltpu.VMEM((2, wl // NSA_CHAIN_LANES, SLC_TILE, NSA_CHAIN_LANES), F32),
            pltpu.VMEM((2, wl // NSA_CHAIN_LANES, SLC_TILE, NSA_CHAIN_LANES), BF16),
            pltpu.VMEM((tq // LANES, nc, LANES), F32),
            pltpu.VMEM((SUBLANES, LANES), I32),
            pltpu.SMEM((SUBLANES, LANES), I32),
            pltpu.SMEM((TILE_LIST_LEN,), I32),
            pltpu.SemaphoreType.DMA,
        ],
        compiler_params=_params(3), name="nsa",
    )(nqT, kcmp, vcmpT, ks, vsT, kw, vwT, gT, sl, srow, auga, augb, augc, tblc, bw)


def _outproj_kernel(om_ref, on_ref, x_ref, wo_ref, g_ref, wr_ref, br_ref,
                    x1_ref, hn_ref, e_ref, w_ref):
    attn = (jnp.dot(om_ref[...], wo_ref[0:512, :], preferred_element_type=F32)
            + jnp.dot(on_ref[...], wo_ref[512:1024, :], preferred_element_type=F32))
    x1 = x_ref[...] + attn
    x1_ref[...] = x1
    hn = _rmsnorm(x1, g_ref[...])
    _store_token_tiles(hn_ref, hn)
    logits = jnp.dot(hn, wr_ref[...], precision=lax.Precision.HIGHEST,
                     preferred_element_type=F32) + br_ref[...]
    tm = logits.shape[0]
    lane = lax.broadcasted_iota(I32, (tm, LANES), 1)
    sc = jnp.where(lane < N_EXPERTS, logits, -jnp.inf)
    e_out = jnp.zeros((tm, LANES), I32)
    vals = []
    for k in range(TOP_K):
        mx = jnp.max(sc, axis=1, keepdims=True)
        idx = jnp.min(jnp.where(sc == mx, lane, LANES), axis=1, keepdims=True)
        e_out = jnp.where(lane == k, idx, e_out)
        sc = jnp.where(lane == idx, -jnp.inf, sc)
        vals.append(mx)
    ex = [jnp.exp(v - vals[0]) for v in vals]
    den = ex[0] + ex[1] + ex[2] + ex[3]
    w_out = jnp.zeros((tm, LANES), F32)
    for k in range(TOP_K):
        w_out = jnp.where(lane == k, ex[k] / den, w_out)
    e_ref[...] = e_out
    w_ref[...] = w_out


def _outproj(om, on, x, wo, g, wr, br, tm):
    N, D = x.shape
    full = lambda a: pl.BlockSpec(a.shape, lambda i: (0,) * a.ndim)
    row = lambda w: pl.BlockSpec((tm, w), lambda i: (i, 0))
    return pl.pallas_call(
        _outproj_kernel, grid=(N // tm,),
        in_specs=[row(512), row(512), row(D), full(wo), full(g), full(wr), full(br)],
        out_specs=[row(D), pl.BlockSpec((tm * SUBLANES, LANES), lambda i: (i, 0)), row(LANES), row(LANES)],
        out_shape=[jax.ShapeDtypeStruct((N, D), F32), jax.ShapeDtypeStruct((N * SUBLANES, LANES), F32),
                   jax.ShapeDtypeStruct((N, LANES), I32), jax.ShapeDtypeStruct((N, LANES), F32)],
        compiler_params=_params(1), name="outproj_router",
    )(om, on, x, wo, g, wr, br)


def _rank_kernel(e_ref, rank_ref, cnt_ref, base_ref):
    i = pl.program_id(0)
    T = e_ref.shape[0]

    @pl.when(i == 0)
    def _():
        base_ref[...] = jnp.zeros(base_ref.shape, F32)

    e = e_ref[...]
    lane = lax.broadcasted_iota(I32, (T, LANES), 1)
    tril = jnp.where(lax.broadcasted_iota(I32, (T, T), 0) >= lax.broadcasted_iota(I32, (T, T), 1),
                     1.0, 0.0).astype(BF16)
    out = jnp.zeros((T, LANES), I32)
    for k in range(TOP_K):
        hit = lane == e[:, k:k + 1]
        oh = jnp.where(hit, 1.0, 0.0)
        cum = jnp.dot(tril, oh.astype(BF16), preferred_element_type=F32)
        base = base_ref[0:1, :]
        r = jnp.sum(jnp.where(hit, cum - 1.0 + base, 0.0), axis=1, keepdims=True)
        out = jnp.where(lane == k, r.astype(I32), out)
        base_ref[...] = base_ref[...] + jnp.sum(oh, axis=0, keepdims=True)
    rank_ref[...] = out
    cnt_ref[...] = base_ref[...]


def _ranks(e128):
    N = e128.shape[0]
    T = RANK_TILE
    return pl.pallas_call(
        _rank_kernel, grid=(N // T,),
        in_specs=[pl.BlockSpec((T, LANES), lambda i: (i, 0))],
        out_specs=[pl.BlockSpec((T, LANES), lambda i: (i, 0)),
                   pl.BlockSpec((8, LANES), lambda i: (0, 0))],
        out_shape=[jax.ShapeDtypeStruct((N, LANES), I32),
                   jax.ShapeDtypeStruct((8, LANES), F32)],
        scratch_shapes=[pltpu.VMEM((8, LANES), F32)],
        compiler_params=_params(1), name="route_ranks",
    )(e128)


def _row_copy(src, dst, i_src, i_dst, sem):
    return pltpu.make_async_copy(src.at[pl.ds(pl.multiple_of(i_src * SUBLANES, SUBLANES), SUBLANES)],
                                 dst.at[pl.ds(pl.multiple_of(i_dst * SUBLANES, SUBLANES), SUBLANES)], sem)


def _store_token_tiles(ref, x):
    rows = x.shape[0]
    for c in range(SUBLANES):
        ref[pl.ds(c, rows, stride=SUBLANES), :] = x[:, c * LANES:(c + 1) * LANES]


def _load_token_tiles(ref, rows):
    return jnp.concatenate([ref[pl.ds(c, rows, stride=SUBLANES), :] for c in range(SUBLANES)], axis=1)


def _dispatch_kernel(gap_lo_ref, gap_hi_ref, dest_hbm, hp_ref, out_hbm, idx_ref, zero_ref,
                     isem, sem, zsem, bsem):
    i = pl.program_id(0)
    T = ROUTE_TILE

    @pl.when(i == 0)
    def _():
        zero_ref[...] = jnp.zeros(zero_ref.shape, F32)
        n_exp = gap_lo_ref.shape[0] - 1

        def pad_rows(e, carry):
            lo, hi = gap_lo_ref[e], gap_hi_ref[e]
            for start in (True, False):
                for par in range(2):
                    def row(r, c):
                        copy = _row_copy(zero_ref, out_hbm, 0, lo + 2 * r + par, zsem)
                        if start:
                            copy.start(priority=par)
                        else:
                            copy.wait()
                        return c
                    lax.fori_loop(0, (hi - lo + 1 - par) // 2, row, 0)
            return carry
        lax.fori_loop(0, n_exp, pad_rows, 0)

        def tail_block(b, carry):
            rows = pl.ds(pl.multiple_of(b * (MOE_ROWS * SUBLANES), MOE_ROWS * SUBLANES), MOE_ROWS * SUBLANES)
            copy = pltpu.make_async_copy(zero_ref, out_hbm.at[rows], bsem)
            copy.start()
            copy.wait()
            return carry
        lax.fori_loop(gap_lo_ref[n_exp] // MOE_ROWS, gap_hi_ref[n_exp] // MOE_ROWS, tail_block, 0)

    cp = pltpu.make_async_copy(dest_hbm.at[i], idx_ref, isem)
    cp.start()
    cp.wait()

    def issue(t, carry):
        for k in range(TOP_K):
            _row_copy(hp_ref, out_hbm, t, idx_ref[t * TOP_K + k], sem).start(priority=k % 2)
        return carry

    def drain(t, carry):
        for k in range(TOP_K):
            _row_copy(hp_ref, out_hbm, 0, 0, sem).wait()
        return carry

    for t0 in range(0, T, ROW_DMA_BATCH):
        lax.fori_loop(t0, t0 + ROW_DMA_BATCH, issue, 0, unroll=ROW_DMA_UNROLL)
        lax.fori_loop(t0, t0 + ROW_DMA_BATCH, drain, 0, unroll=ROW_DMA_UNROLL)


def _dispatch(gap_lo, gap_hi, dest2, hp, n_rows):
    nsteps = dest2.shape[0]
    T = ROUTE_TILE
    grid_spec = pltpu.PrefetchScalarGridSpec(
        num_scalar_prefetch=2, grid=(nsteps,),
        in_specs=[pl.BlockSpec(memory_space=pl.ANY),
                  pl.BlockSpec((T * SUBLANES, LANES), lambda i, lo, hi: (i, 0))],
        out_specs=pl.BlockSpec(memory_space=pl.ANY),
        scratch_shapes=[pltpu.SMEM((T * TOP_K,), I32), pltpu.VMEM((MOE_ROWS * SUBLANES, LANES), F32),
                        pltpu.SemaphoreType.DMA, pltpu.SemaphoreType.DMA, pltpu.SemaphoreType.DMA,
                        pltpu.SemaphoreType.DMA],
    )
    return pl.pallas_call(
        _dispatch_kernel, grid_spec=grid_spec,
        out_shape=jax.ShapeDtypeStruct((n_rows * SUBLANES, LANES), F32),
        compiler_params=_params(1), name="moe_dispatch",
    )(gap_lo, gap_hi, dest2, hp)


W_PREP_COLS = 256


def _expert_kernel(be_ref, na_ref, x_ref, wup_ref, bg_ref, bu_ref, wd_ref, bd_ref, y_ref,
                   wg_ref, wu_ref, wt_ref):
    b = pl.program_id(0)
    active = b < na_ref[0]
    new_expert = jnp.logical_or(b == 0, be_ref[b] != be_ref[jnp.maximum(b - 1, 0)])

    @pl.when(jnp.logical_and(active, new_expert))
    def _():
        half = W_PREP_COLS // 2
        n_lc = wt_ref.shape[0]

        def every_other(first):
            return jnp.concatenate(
                [wt_ref[c, pl.ds(first, half, stride=2), :] for c in range(n_lc)], axis=1).astype(BF16)

        for ch in range(wup_ref.shape[2] // W_PREP_COLS):
            panel = wup_ref[0, :, ch * W_PREP_COLS:(ch + 1) * W_PREP_COLS].T
            for c in range(n_lc):
                wt_ref[c] = panel[:, c * LANES:(c + 1) * LANES]
            wg_ref[ch * half:(ch + 1) * half, :] = every_other(0)
            wu_ref[ch * half:(ch + 1) * half, :] = every_other(1)

    @pl.when(active)
    def _():
        xb = _load_token_tiles(x_ref, MOE_ROWS).astype(BF16)
        gg = lax.dot_general(xb, wg_ref[...], NT_DIMS, preferred_element_type=F32) + bg_ref[0]
        uu = lax.dot_general(xb, wu_ref[...], NT_DIMS, preferred_element_type=F32) + bu_ref[0]
        gg = jnp.minimum(gg, SWIGLU_LIMIT)
        uu = jnp.clip(uu, -SWIGLU_LIMIT, SWIGLU_LIMIT)
        a = gg * jax.nn.sigmoid(SWIGLU_ALPHA * gg) * (uu + 1.0)
        _store_token_tiles(y_ref, jnp.dot(a.astype(BF16), wd_ref[0], preferred_element_type=F32) + bd_ref[0])

    @pl.when(jnp.logical_not(active))
    def _():
        y_ref[...] = jnp.zeros(y_ref.shape, F32)


def _experts(blk_e, n_act, xrows, w_up, bg, bu, wd, bd):
    _, D, F2 = w_up.shape
    F = F2 // 2
    assert D == SUBLANES * LANES
    P = xrows.shape[0] // SUBLANES
    n_blk = P // MOE_ROWS
    rows_spec = pl.BlockSpec((MOE_ROWS * SUBLANES, LANES), lambda b, be, na: (b, 0))
    wspec = lambda r, c: pl.BlockSpec((1, r, c), lambda b, be, na: (be[b], 0, 0))
    grid_spec = pltpu.PrefetchScalarGridSpec(
        num_scalar_prefetch=2, grid=(n_blk,),
        in_specs=[rows_spec, wspec(D, F2), wspec(1, F), wspec(1, F), wspec(F, D), wspec(1, D)],
        out_specs=rows_spec,
        scratch_shapes=[pltpu.VMEM((F, D), BF16), pltpu.VMEM((F, D), BF16),
                        pltpu.VMEM((D // LANES, W_PREP_COLS, LANES), F32)],
    )
    return pl.pallas_call(
        _expert_kernel, grid_spec=grid_spec,
        out_shape=jax.ShapeDtypeStruct(xrows.shape, F32),
        compiler_params=_params(1), name="moe_experts",
    )(blk_e, n_act, xrows, w_up, bg, bu, wd, bd)


def _combine_kernel(dest_hbm, x1_ref, w_ref, g_ref, y_hbm, o_ref, idx_ref, buf_ref, isem, sem):
    i = pl.program_id(0)
    T = ROUTE_TILE
    cp = pltpu.make_async_copy(dest_hbm.at[i], idx_ref, isem)
    cp.start()
    cp.wait()

    def issue(t, carry):
        for k in range(TOP_K):
            _row_copy(y_hbm, buf_ref.at[k], idx_ref[t * TOP_K + k], t, sem).start(priority=k % 2)
        return carry

    def drain(t, carry):
        for k in range(TOP_K):
            _row_copy(y_hbm, buf_ref.at[k], 0, 0, sem).wait()
        return carry

    for t0 in range(0, T, ROW_DMA_BATCH):
        lax.fori_loop(t0, t0 + ROW_DMA_BATCH, issue, 0, unroll=ROW_DMA_UNROLL)
        lax.fori_loop(t0, t0 + ROW_DMA_BATCH, drain, 0, unroll=ROW_DMA_UNROLL)

    x1 = x1_ref[...]
    w = w_ref[...]
    cols = []
    for c in range(SUBLANES):
        acc = x1[:, c * LANES:(c + 1) * LANES]
        for k in range(TOP_K):
            acc = acc + w[:, k:k + 1] * buf_ref[k, pl.ds(c, T, stride=SUBLANES), :]
        cols.append(acc)
    o_ref[...] = _rmsnorm(jnp.concatenate(cols, axis=1), g_ref[...])


def _combine(dest2, x1, w128, g, yrows):
    N, D = x1.shape
    T = ROUTE_TILE
    return pl.pallas_call(
        _combine_kernel, grid=(N // T,),
        in_specs=[pl.BlockSpec(memory_space=pl.ANY),
                  pl.BlockSpec((T, D), lambda i: (i, 0)),
                  pl.BlockSpec((T, LANES), lambda i: (i, 0)),
                  pl.BlockSpec(g.shape, lambda i: (0, 0)),
                  pl.BlockSpec(memory_space=pl.ANY)],
        out_specs=pl.BlockSpec((T, D), lambda i: (i, 0)),
        out_shape=jax.ShapeDtypeStruct((N, D), F32),
        scratch_shapes=[pltpu.SMEM((T * TOP_K,), I32),
                        pltpu.VMEM((TOP_K, T * SUBLANES, LANES), F32),
                        pltpu.SemaphoreType.DMA, pltpu.SemaphoreType.DMA],
        compiler_params=_params(1), name="moe_combine",
    )(dest2, x1, w128, g, yrows)


def _alibi_slopes():
    n = MOBA_HEADS + NSA_HEADS
    s = jnp.exp2(-8.0 * jnp.arange(1, n + 1, dtype=F32) / n)
    return s[0::2], s[1::2]


def _prep_inproj(w_in):
    hd = HEAD_DIM
    sizes = [MOBA_HEADS * hd] * 3 + [NSA_HEADS * hd] + [NSA_KV_HEADS * hd] * 6 + [NSA_BRANCHES * NSA_HEADS]
    cuts = np.cumsum([0] + sizes)
    mq, mk, mv, nq, kc, vc, ks, vs, kw, vw, ng = [w_in[:, cuts[i]:cuts[i + 1]] for i in range(11)]
    qscale = (hd ** -0.5) * LOG2E
    wr = jnp.concatenate([mk, kc, vc, ks, kw], axis=1).astype(BF16)
    wt = jnp.concatenate([mq * qscale, mv, nq * qscale, vs, vw], axis=1).T.astype(BF16)
    ngr = ng.reshape(-1, NSA_KV_HEADS, NSA_GROUP, NSA_BRANCHES).transpose(1, 3, 2, 0)
    ngr = ngr.reshape(NSA_KV_HEADS, NSA_BRANCHES * NSA_GROUP, -1)
    wg = jnp.pad(ngr, ((0, 0), (0, 16 - NSA_BRANCHES * NSA_GROUP), (0, 0))).reshape(32, -1)
    return wr, wt, wg.astype(F32)


def _prep_compress(w1, w2, pe):
    hd, half = HEAD_DIM, NSA_CMP_STRIDE
    w1r = w1.reshape(2, half, hd, hd)
    eye = jnp.eye(NSA_KV_HEADS, dtype=w1.dtype)
    w = jnp.einsum('alde,gh->lgdahe', w1r, eye).reshape(half * NSA_KV_HEADS * hd, 2 * NSA_KV_HEADS * hd)
    w2b = jnp.einsum('de,gh->gdhe', w2, eye).reshape(NSA_KV_HEADS * hd, NSA_KV_HEADS * hd)
    per = pe.reshape(2, half, 1, hd)
    pe2 = jnp.broadcast_to(per, (2, half, NSA_KV_HEADS, hd)).reshape(2, 1, half * NSA_KV_HEADS * hd)
    pe2 = jnp.broadcast_to(pe2, (2, 8, pe2.shape[2]))
    return w.astype(BF16), w2b.astype(BF16), pe2.astype(F32)


def _attention_tables(S):
    moba_sl, nsa_sl = _alibi_slopes()
    moba_sl = moba_sl * LOG2E
    nsa_sl = nsa_sl * LOG2E
    blk = MOBA_BLOCK
    moba_row = jnp.broadcast_to(moba_sl[:, None, None], (MOBA_HEADS, 1, blk))
    nb = S // blk
    nbp = -(-nb // 16) * 16
    col = jnp.arange(LANES)[None, None, :]
    tile = jnp.arange(nb + 1)[:, None, None]
    off = jnp.arange(blk, dtype=F32)[None, :, None]
    moba_aug = jnp.where(jnp.logical_and(col == tile, tile < nb), 1.0,
                         jnp.where(jnp.logical_and(col >= nbp, col < nbp + 3), off, 0.0)).astype(BF16)
    parts = jnp.stack(list(_split3(moba_sl)) + [jnp.zeros_like(moba_sl)] * 13, axis=1)
    moba_srow = jnp.broadcast_to(parts[:, :, None], (MOBA_HEADS, 16, blk)).astype(BF16)

    wl = NSA_GROUP * NSA_TQ
    n_slc = S // NSA_SLC_BLOCK
    assert n_slc <= LANES, "block-choice rows must fit the spare contraction rows"
    nsa_row = jnp.repeat(nsa_sl.reshape(NSA_KV_HEADS, NSA_GROUP), NSA_TQ, axis=1)
    hi, mid, lo = [t[:, None, :] for t in _split3(nsa_row)]
    base = ((1 - jnp.arange(NSA_KV_HEADS)) * HEAD_DIM)[:, None, None]
    rows = jnp.arange(2 * HEAD_DIM)[None, :, None]
    nsa_srow = jnp.where(rows == base, hi, jnp.where(rows == base + 1, mid, jnp.where(
        rows == base + 2, lo, jnp.where(rows == base + 3, NEG_BIG, 0.0)))).astype(BF16)
    lane = jnp.arange(LANES)[None, None, None, :]
    base4 = base[:, None]
    null = jnp.arange(2, dtype=F32)[None, :, None, None]
    koff = jnp.arange(SLC_TILE, dtype=F32)[None, None, :, None]
    nsa_auga = jnp.where(jnp.logical_and(lane >= base4, lane < base4 + 3), koff,
                         jnp.where(lane == base4 + 3, null, 0.0))
    nsa_auga = nsa_auga.reshape(2 * NSA_KV_HEADS, SLC_TILE, LANES).astype(BF16)
    per = SLC_TILE // NSA_SLC_BLOCK
    tile = jnp.arange(S // SLC_TILE)[:, None, None]
    blk_of = tile * per + jnp.arange(SLC_TILE)[None, :, None] // NSA_SLC_BLOCK
    nsa_augb = (jnp.arange(LANES)[None, None, :] == blk_of).astype(BF16)

    nc = S // NSA_CMP_STRIDE
    ci = jnp.arange(nc)[None, :, None]
    lane3 = jnp.arange(LANES)[None, None, :]
    nsa_augc = jnp.where(jnp.logical_and(lane3 >= base + 4, lane3 < base + 7), (ci >> 1).astype(F32),
                         jnp.where(jnp.logical_and(lane3 >= base + 7, lane3 < base + 10),
                                   (ci & 1).astype(F32), 0.0)).astype(BF16)
    step2 = [t[:, None, :] for t in _split3(nsa_row * (2.0 * NSA_CMP_STRIDE))]
    step1 = [t[:, None, :] for t in _split3(nsa_row * (1.0 * NSA_CMP_STRIDE))]
    for k in range(3):
        nsa_srow = jnp.where(rows == base + 4 + k, step2[k].astype(BF16),
                             jnp.where(rows == base + 7 + k, step1[k].astype(BF16), nsa_srow))
    il = jnp.tile(jnp.arange(NSA_TQ), NSA_GROUP)[None, :]
    rel = (jnp.arange(2 * nc) - nc)[:, None]
    nsa_tblc = jnp.where(rel * NSA_CMP_STRIDE + (NSA_CMP_LEN - 1) <= il, 0.0, NEG_BIG).astype(F32)
    dist = (NSA_WINDOW + il - jnp.arange(NSA_WINDOW + WIN_KEYS)[:, None])[None]
    nsa_bw = jnp.where(jnp.logical_and(dist >= 0, dist < NSA_WINDOW),
                       -nsa_row[:, None, :] * dist.astype(F32), NEG_BIG)
    return ((moba_aug, moba_srow, moba_row),
            (nsa_row.reshape(NSA_KV_HEADS, 1, wl), nsa_srow, nsa_auga, nsa_augb, nsa_augc, nsa_tblc, nsa_bw))


def _split3(x):
    hi = x.astype(BF16).astype(F32)
    mid = (x - hi).astype(BF16).astype(F32)
    lo = (x - hi - mid).astype(BF16).astype(F32)
    return hi, mid, lo


def _attention(x, attn_norm_g, w_in, cmp_pe_k, cmp_pe_v, cmp_w1_k, cmp_w2_k, cmp_w1_v, cmp_w2_v):
    B, S, D = x.shape
    wr, wt, wg = _prep_inproj(w_in)
    (mk, kc, vc, ks, kw, mqT, mvT, nqT, vsT, vwT, gT) = _inproj(
        x, attn_norm_g.reshape(1, D), wr, wt, wg, tm=512)
    moba_tabs, nsa_tabs = _attention_tables(S)
    o_moba = _moba(mqT, mk, mvT, *moba_tabs)
    wk, w2k, pek = _prep_compress(cmp_w1_k, cmp_w2_k, cmp_pe_k)
    wv, w2v, pev = _prep_compress(cmp_w1_v, cmp_w2_v, cmp_pe_v)
    nc = S // NSA_CMP_STRIDE
    kcmp, vcmpT = _compress(kc.reshape(B, nc, -1), vc.reshape(B, nc, -1), wk, wv.T, pek, pev, w2k, w2v.T)
    o_nsa = _nsa(nqT, kcmp, vcmpT, ks, vsT, kw, vwT, gT, *nsa_tabs)
    return o_moba, o_nsa


def _moe(x1, hn, e128, w128, w_up, b_up, w_down, b_down, final_norm_g):
    N, D = x1.shape
    rank128, cnt = _ranks(e128)
    counts = cnt[0, :N_EXPERTS].astype(I32)
    padded = (counts + MOE_ROWS - 1) // MOE_ROWS * MOE_ROWS
    pends = jnp.cumsum(padded)
    pstarts = pends - padded
    e4 = e128[:, :TOP_K]
    dest = pstarts[e4] + rank128[:, :TOP_K]
    dest2 = dest.reshape(N // ROUTE_TILE, ROUTE_TILE * TOP_K)
    n_blk = (N * TOP_K + N_EXPERTS * MOE_ROWS + MOE_ROWS - 1) // MOE_ROWS
    P = n_blk * MOE_ROWS
    blk_start = jnp.arange(n_blk, dtype=I32) * MOE_ROWS
    blk_e = jnp.minimum(jnp.sum((pends[None, :] <= blk_start[:, None]).astype(I32), axis=1), N_EXPERTS - 1)
    n_act = (pends[-1:] // MOE_ROWS).astype(I32)
    gap_lo = jnp.concatenate([pstarts + counts, pends[-1:]]).astype(I32)
    gap_hi = jnp.concatenate([pends, jnp.full((1,), P, I32)]).astype(I32)
    xrows = _dispatch(gap_lo, gap_hi, dest2, hn, P)
    bg = b_up[:, None, 0::2]
    bu = b_up[:, None, 1::2]
    yrows = _experts(blk_e, n_act, xrows, w_up, bg, bu, w_down.astype(BF16), b_down[:, None, :])
    return _combine(dest2, x1, w128, final_norm_g.reshape(1, D), yrows)


def kernel(x, attn_norm_g, w_in, cmp_pe_k, cmp_pe_v, cmp_w1_k, cmp_w2_k, cmp_w1_v, cmp_w2_v, w_out, ffn_norm_g, w_router, b_router, w_up, b_up, w_down, b_down, final_norm_g):
    B, S, D = x.shape
    assert attn_norm_g.shape[0] == 1, "single-layer kernel"
    o_moba, o_nsa = _attention(x, attn_norm_g[0], w_in[0], cmp_pe_k[0], cmp_pe_v[0],
                               cmp_w1_k[0], cmp_w2_k[0], cmp_w1_v[0], cmp_w2_v[0])
    N = B * S
    wr = jnp.pad(w_router[0], ((0, 0), (0, LANES - N_EXPERTS)))
    br = jnp.pad(b_router[0], (0, LANES - N_EXPERTS)).reshape(1, LANES)
    x1, hn, e128, w128 = _outproj(o_moba.reshape(N, -1), o_nsa.reshape(N, -1), x.reshape(N, D),
                                  w_out[0].astype(BF16), ffn_norm_g[0].reshape(1, D), wr, br, tm=512)
    out = _moe(x1, hn, e128, w128, w_up[0], b_up[0], w_down[0], b_down[0], final_norm_g)
    return out.reshape(B, S, D)
```
